```python
import math
import jax, jax.numpy as jnp
from jax import lax
import numpy as np

D_MODEL = 2048
BATCH = 8
SEQ = 4096
DEPTH = 4

N_MIXERS = 3
N_SSD_LAYERS = (DEPTH + 2) // 3
N_MLA_LAYERS = (DEPTH + 1) // 3
N_GDN_LAYERS = DEPTH // 3
DEEPNORM_ALPHA = (2.0 * DEPTH) ** 0.25
DEEPNORM_BETA = (8.0 * DEPTH) ** -0.25
LN_EPS = 1e-5
RMS_EPS = 1e-6

SSD_D_INNER = 2 * D_MODEL
SSD_HEAD_DIM = 64
SSD_N_HEADS = SSD_D_INNER // SSD_HEAD_DIM
SSD_N_GROUPS = 8
SSD_D_STATE = 128
SSD_CONV = 4
SSD_CHUNK = 128
SSD_CONV_DIM = SSD_D_INNER + 2 * SSD_N_GROUPS * SSD_D_STATE
SSD_PROJ = SSD_D_INNER + SSD_CONV_DIM + SSD_N_HEADS

MLA_N_HEADS = D_MODEL // 128
MLA_Q_RANK = 768
MLA_KV_RANK = 512
MLA_NOPE = 128
MLA_ROPE = 64
MLA_V = 128
MLA_GATE = MLA_N_HEADS * MLA_V
MLA_PROJ = MLA_Q_RANK + MLA_KV_RANK + MLA_ROPE + MLA_GATE
MLA_Q_BLOCK = 128
ROPE_THETA = 10000.0

GDN_N_QK_HEADS = 16
GDN_N_V_HEADS = 32
GDN_DK = 128
GDN_DV = 128
GDN_KEY_DIM = GDN_N_QK_HEADS * GDN_DK
GDN_VAL_DIM = GDN_N_V_HEADS * GDN_DV
GDN_CONV = 4
GDN_CHUNK = 64
GDN_CONV_DIM = 2 * GDN_KEY_DIM + GDN_VAL_DIM
GDN_PROJ = GDN_CONV_DIM + GDN_VAL_DIM + 2 * GDN_N_V_HEADS

kernel_name = "hybrid_ssd_mla_gdn_deepnorm"


def _layernorm(x, g, b):
    xf = x.astype(jnp.float32)
    mu = jnp.mean(xf, -1, keepdims=True)
    var = jnp.mean(jnp.square(xf - mu), -1, keepdims=True)
    return ((xf - mu) * lax.rsqrt(var + LN_EPS) * g + b).astype(x.dtype)


def _rmsnorm(x, w):
    xf = x.astype(jnp.float32)
    return (xf * lax.rsqrt(jnp.mean(xf * xf, -1, keepdims=True) + RMS_EPS) * w).astype(x.dtype)


def _l2norm(x):
    xf = x.astype(jnp.float32)
    return xf * lax.rsqrt(jnp.sum(xf * xf, -1, keepdims=True) + RMS_EPS)


def _causal_dwconv(x, w):
    K, C = w.shape
    return lax.conv_general_dilated(
        x, w[:, None, :].astype(x.dtype), window_strides=(1,), padding=[(K - 1, 0)],
        dimension_numbers=("NWC", "WIO", "NWC"), feature_group_count=C)


def _rope(x, cos, sin):
    x1, x2 = jnp.split(x, 2, axis=-1)
    return jnp.concatenate([x1 * cos - x2 * sin, x1 * sin + x2 * cos], axis=-1)


def _ssd_mixer(u, in_w, conv_w, conv_b, dt_bias, a_log, d_skip, norm_w, out_w):
    f32 = jnp.float32
    Bsz, S, _ = u.shape
    G, N, H, P, L = SSD_N_GROUPS, SSD_D_STATE, SSD_N_HEADS, SSD_HEAD_DIM, SSD_CHUNK
    R = H // G
    nc = S // L
    proj = u @ in_w
    z, xbc, dt = jnp.split(proj, [SSD_D_INNER, SSD_D_INNER + SSD_CONV_DIM], axis=-1)
    xbc = jax.nn.silu(_causal_dwconv(xbc, conv_w) + conv_b)
    xs, Bm, Cm = jnp.split(xbc, [SSD_D_INNER, SSD_D_INNER + G * N], axis=-1)
    dt = jax.nn.softplus(dt.astype(f32) + dt_bias.astype(f32))
    A = -jnp.exp(a_log.astype(f32)).reshape(G, R)
    x_c = xs.astype(f32).reshape(Bsz, nc, L, G, R, P)
    B_c = Bm.astype(f32).reshape(Bsz, nc, L, G, N)
    C_c = Cm.astype(f32).reshape(Bsz, nc, L, G, N)
    dt_c = dt.reshape(Bsz, nc, L, G, R)
    xdt = x_c * dt_c[..., None]
    a_cs = jnp.moveaxis(jnp.cumsum(dt_c * A, axis=2), 2, -1)
    causal = jnp.tril(jnp.ones((L, L), dtype=bool))
    decay = jnp.exp(jnp.where(causal, a_cs[..., :, None] - a_cs[..., None, :], -jnp.inf))
    cb = jnp.einsum("bclgn,bcsgn->bcgls", C_c, B_c)
    y_diag = jnp.einsum("bcgrls,bcsgrp->bclgrp", cb[:, :, :, None] * decay, xdt)
    decay_states = jnp.moveaxis(jnp.exp(a_cs[..., -1:] - a_cs), -1, 2)
    states = jnp.einsum("bcsgn,bcsgrp->bcgrpn", B_c, xdt * decay_states[..., None])
    chunk_decay = jnp.exp(a_cs[..., -1])

    def step(h, inp):
        st, dec = inp
        return h * dec[..., None, None] + st, h

    h0 = jnp.zeros((Bsz, G, R, P, N), f32)
    _, prev = lax.scan(step, h0, (jnp.moveaxis(states, 1, 0), jnp.moveaxis(chunk_decay, 1, 0)))
    prev = jnp.moveaxis(prev, 0, 1)
    decay_out = jnp.moveaxis(jnp.exp(a_cs), -1, 2)
    y_off = jnp.einsum("bclgn,bcgrpn->bclgrp", C_c, prev) * decay_out[..., None]
    y = y_diag + y_off + x_c * d_skip.astype(f32).reshape(G, R)[..., None]
    gs = SSD_D_INNER // G
    yg = y.reshape(Bsz, S, G, gs) * jax.nn.silu(z.astype(f32).reshape(Bsz, S, G, gs))
    yg = yg * lax.rsqrt(jnp.mean(yg * yg, -1, keepdims=True) + RMS_EPS) * norm_w.astype(f32).reshape(G, gs)
    return yg.reshape(Bsz, S, SSD_D_INNER).astype(u.dtype) @ out_w


def _mla_mixer(u, positions, in_w, q_norm_w, q_up_w, kv_norm_w, kv_up_w, out_w):
    f32 = jnp.float32
    Bsz, S, _ = u.shape
    H, QB = MLA_N_HEADS, MLA_Q_BLOCK
    nb = S // QB
    proj = u @ in_w
    q_c, kv_c, k_rope, z = jnp.split(
        proj, [MLA_Q_RANK, MLA_Q_RANK + MLA_KV_RANK, MLA_Q_RANK + MLA_KV_RANK + MLA_ROPE], axis=-1)
    q = (_rmsnorm(q_c, q_norm_w) @ q_up_w).astype(f32).reshape(Bsz, S, H, MLA_NOPE + MLA_ROPE)
    kv = (_rmsnorm(kv_c, kv_norm_w) @ kv_up_w).astype(f32).reshape(Bsz, S, H, MLA_NOPE + MLA_V)
    q_nope, q_rope = q[..., :MLA_NOPE], q[..., MLA_NOPE:]
    k_nope, v = kv[..., :MLA_NOPE], kv[..., MLA_NOPE:]
    inv_freq = ROPE_THETA ** (-jnp.arange(0, MLA_ROPE, 2, dtype=f32) / MLA_ROPE)
    ang = positions.astype(f32)[..., None] * inv_freq
    cos, sin = jnp.cos(ang), jnp.sin(ang)
    q_rope = _rope(q_rope, cos[:, :, None], sin[:, :, None])
    k_rope = _rope(k_rope.astype(f32), cos, sin)
    scale = (MLA_NOPE + MLA_ROPE) ** -0.5
    qn_b = q_nope.reshape(Bsz, nb, QB, H, MLA_NOPE).transpose(1, 0, 3, 2, 4)
    qr_b = q_rope.reshape(Bsz, nb, QB, H, MLA_ROPE).transpose(1, 0, 3, 2, 4)
    kpos = jnp.arange(S)

    def attend(args):
        qn, qr, blk = args
        s = (jnp.einsum("bhqd,bkhd->bhqk", qn, k_nope)
             + jnp.einsum("bhqr,bkr->bhqk", qr, k_rope)) * scale
        qpos = blk * QB + jnp.arange(QB)
        s = jnp.where(kpos[None, :] <= qpos[:, None], s, -jnp.inf)
        p = jax.nn.softmax(s, axis=-1)
        return jnp.einsum("bhqk,bkhd->bqhd", p, v)

    o = lax.map(attend, (qn_b, qr_b, jnp.arange(nb)))
    o = o.transpose(1, 0, 2, 3, 4).reshape(Bsz, S, H * MLA_V)
    o = o * jax.nn.silu(z.astype(f32))
    return o.astype(u.dtype) @ out_w


def _chunk_gated_delta(q, k, v, g, beta):
    Bsz, S, H, DK = q.shape
    DV = v.shape[-1]
    L = GDN_CHUNK
    nc = S // L

    def chunks(t):
        return jnp.swapaxes(t.reshape(Bsz, nc, L, H, *t.shape[3:]), 2, 3)

    q, k, v, g, beta = chunks(q), chunks(k), chunks(v), chunks(g), chunks(beta)
    g_cs = jnp.cumsum(g, axis=-1)
    incl = jnp.tril(jnp.ones((L, L), dtype=bool))
    strict = jnp.tril(jnp.ones((L, L), dtype=bool), -1)
    decay = jnp.exp(jnp.where(incl, g_cs[..., :, None] - g_cs[..., None, :], -jnp.inf))
    k_beta = k * beta[..., None]
    v_beta = v * beta[..., None]
    a_mat = jnp.where(strict, jnp.einsum("bchld,bchsd->bchls", k_beta, k) * decay, 0.0)
    eye = jnp.eye(L, dtype=q.dtype)
    rhs = jnp.concatenate([v_beta, k_beta * jnp.exp(g_cs)[..., None]], axis=-1)
    sol = lax.linalg.triangular_solve(eye + a_mat, rhs, left_side=True, lower=True, unit_diagonal=True)
    u_val, w_dec = sol[..., :DV], sol[..., DV:]
    qk = jnp.where(incl, jnp.einsum("bchld,bchsd->bchls", q, k) * decay, 0.0)
    q_dec = q * jnp.exp(g_cs)[..., None]
    g_last = g_cs[..., -1]
    k_dec = k * jnp.exp(g_last[..., None] - g_cs)[..., None]

    def step(state, inp):
        qk_c, q_dec_c, u_c, w_c, k_dec_c, gl_c = inp
        v_new = u_c - jnp.einsum("bhld,bhdv->bhlv", w_c, state)
        o_c = jnp.einsum("bhld,bhdv->bhlv", q_dec_c, state) + jnp.einsum("bhls,bhsv->bhlv", qk_c, v_new)
        state = state * jnp.exp(gl_c)[..., None, None] + jnp.einsum("bhld,bhlv->bhdv", k_dec_c, v_new)
        return state, o_c

    xs = tuple(jnp.moveaxis(t, 1, 0) for t in (qk, q_dec, u_val, w_dec, k_dec, g_last))
    _, o = lax.scan(step, jnp.zeros((Bsz, H, DK, DV), q.dtype), xs)
    o = jnp.swapaxes(jnp.moveaxis(o, 0, 1), 2, 3)
    return o.reshape(Bsz, S, H, DV)


def _gdn_mixer(u, in_w, conv_w, a_log, dt_bias, norm_w, out_w):
    f32 = jnp.float32
    Bsz, S, _ = u.shape
    HK, HV, DK, DV = GDN_N_QK_HEADS, GDN_N_V_HEADS, GDN_DK, GDN_DV
    proj = u @ in_w
    qkv, z, b, a = jnp.split(
        proj, [GDN_CONV_DIM, GDN_CONV_DIM + GDN_VAL_DIM, GDN_CONV_DIM + GDN_VAL_DIM + HV], axis=-1)
    qkv = jax.nn.silu(_causal_dwconv(qkv, conv_w))
    q, k, v = jnp.split(qkv, [GDN_KEY_DIM, 2 * GDN_KEY_DIM], axis=-1)
    rep = HV // HK
    q = jnp.repeat(_l2norm(q.reshape(Bsz, S, HK, DK)), rep, axis=2) * (DK ** -0.5)
    k = jnp.repeat(_l2norm(k.reshape(Bsz, S, HK, DK)), rep, axis=2)
    v = v.astype(f32).reshape(Bsz, S, HV, DV)
    beta = jax.nn.sigmoid(b.astype(f32))
    g = -jnp.exp(a_log.astype(f32)) * jax.nn.softplus(a.astype(f32) + dt_bias.astype(f32))
    o = _chunk_gated_delta(q, k, v, g, beta)
    o = _rmsnorm(o, norm_w) * jax.nn.silu(z.astype(f32).reshape(Bsz, S, HV, DV))
    return o.reshape(Bsz, S, GDN_VAL_DIM).astype(u.dtype) @ out_w


def _dt_bias(key, shape):
    dt = jnp.exp(jax.random.uniform(key, shape, jnp.float32, math.log(1e-3), math.log(1e-1)))
    return dt + jnp.log(-jnp.expm1(-dt))


def _fwd_setup_inputs(seed: int = 0) -> dict:
    key = jax.random.key(seed)
    ks = jax.random.split(key, 32)
    f32 = jnp.float32

    def nrm(k, shape, scale):
        return jax.random.normal(k, shape, f32) * scale

    def gain(k, shape):
        return 1.0 + 0.02 * jax.random.normal(k, shape, f32)

    nA, nB, nC = N_SSD_LAYERS, N_MLA_LAYERS, N_GDN_LAYERS
    x = jax.random.normal(ks[0], (BATCH, SEQ, D_MODEL), f32)
    offset = jax.random.randint(ks[1], (BATCH, 1), 0, 1024, dtype=jnp.int32)
    positions = offset + jnp.arange(SEQ, dtype=jnp.int32)[None, :]
    return {
        "x": x,
        "positions": positions,
        "ssd_in_w": nrm(ks[2], (nA, D_MODEL, SSD_PROJ), D_MODEL ** -0.5),
        "ssd_conv_w": nrm(ks[3], (nA, SSD_CONV, SSD_CONV_DIM), SSD_CONV ** -0.5),
        "ssd_conv_b": nrm(ks[4], (nA, SSD_CONV_DIM), 0.02),
        "ssd_dt_bias": _dt_bias(ks[5], (nA, SSD_N_HEADS)),
        "ssd_a_log": jnp.log(jax.random.uniform(ks[6], (nA, SSD_N_HEADS), f32, 1.0, 16.0)),
        "ssd_d": gain(ks[7], (nA, SSD_N_HEADS)),
        "ssd_norm_w": gain(ks[8], (nA, SSD_D_INNER)),
        "ssd_out_w": nrm(ks[9], (nA, SSD_D_INNER, D_MODEL), SSD_D_INNER ** -0.5 * DEEPNORM_BETA),
        "mla_in_w": nrm(ks[10], (nB, D_MODEL, MLA_PROJ), D_MODEL ** -0.5),
        "mla_q_norm_w": gain(ks[11], (nB, MLA_Q_RANK)),
        "mla_q_up_w": nrm(ks[12], (nB, MLA_Q_RANK, MLA_N_HEADS * (MLA_NOPE + MLA_ROPE)), MLA_Q_RANK ** -0.5),
        "mla_kv_norm_w": gain(ks[13], (nB, MLA_KV_RANK)),
        "mla_kv_up_w": nrm(ks[14], (nB, MLA_KV_RANK, MLA_N_HEADS * (MLA_NOPE + MLA_V)), MLA_KV_RANK ** -0.5),
        "mla_out_w": nrm(ks[15], (nB, MLA_N_HEADS * MLA_V, D_MODEL), (MLA_N_HEADS * MLA_V) ** -0.5 * DEEPNORM_BETA),
        "gdn_in_w": nrm(ks[16], (nC, D_MODEL, GDN_PROJ), D_MODEL ** -0.5),
        "gdn_conv_w": nrm(ks[17], (nC, GDN_CONV, GDN_CONV_DIM), GDN_CONV ** -0.5),
        "gdn_a_log": jnp.log(jax.random.uniform(ks[18], (nC, GDN_N_V_HEADS), f32, 1.0, 16.0)),
        "gdn_dt_bias": _dt_bias(ks[19], (nC, GDN_N_V_HEADS)),
        "gdn_norm_w": gain(ks[20], (nC, GDN_DV)),
        "gdn_out_w": nrm(ks[21], (nC, GDN_VAL_DIM, D_MODEL), GDN_VAL_DIM ** -0.5 * DEEPNORM_BETA),
        "ln_g": gain(ks[22], (DEPTH, D_MODEL)),
        "ln_b": nrm(ks[23], (DEPTH, D_MODEL), 0.02),
    }


def _fwd_reference(x, positions, ssd_in_w, ssd_conv_w, ssd_conv_b, ssd_dt_bias, ssd_a_log, ssd_d,
              ssd_norm_w, ssd_out_w, mla_in_w, mla_q_norm_w, mla_q_up_w, mla_kv_norm_w, mla_kv_up_w,
              mla_out_w, gdn_in_w, gdn_conv_w, gdn_a_log, gdn_dt_bias, gdn_norm_w, gdn_out_w, ln_g, ln_b):
    h = x
    for i in range(DEPTH):
        kind, j = i % N_MIXERS, i // N_MIXERS
        if kind == 0:
            y = _ssd_mixer(h, ssd_in_w[j], ssd_conv_w[j], ssd_conv_b[j], ssd_dt_bias[j], ssd_a_log[j],
                           ssd_d[j], ssd_norm_w[j], ssd_out_w[j])
        elif kind == 1:
            y = _mla_mixer(h, positions, mla_in_w[j], mla_q_norm_w[j], mla_q_up_w[j], mla_kv_norm_w[j],
                           mla_kv_up_w[j], mla_out_w[j])
        else:
            y = _gdn_mixer(h, gdn_in_w[j], gdn_conv_w[j], gdn_a_log[j], gdn_dt_bias[j], gdn_norm_w[j],
                           gdn_out_w[j])
        h = _layernorm(DEEPNORM_ALPHA * h + y.astype(h.dtype), ln_g[i], ln_b[i])
    return h


import jax as _jax
import jax.numpy as _jnp

TWIN_FORMAT = 'train_step'
FWD_PARAMS = ['x', 'positions', 'ssd_in_w', 'ssd_conv_w', 'ssd_conv_b', 'ssd_dt_bias', 'ssd_a_log', 'ssd_d', 'ssd_norm_w', 'ssd_out_w', 'mla_in_w', 'mla_q_norm_w', 'mla_q_up_w', 'mla_kv_norm_w', 'mla_kv_up_w', 'mla_out_w', 'gdn_in_w', 'gdn_conv_w', 'gdn_a_log', 'gdn_dt_bias', 'gdn_norm_w', 'gdn_out_w', 'ln_g', 'ln_b']
TWIN_WEIGHTS = ['ssd_in_w', 'ssd_conv_w', 'ssd_conv_b', 'ssd_dt_bias', 'ssd_a_log', 'ssd_d', 'ssd_norm_w', 'ssd_out_w', 'mla_in_w', 'mla_q_norm_w', 'mla_q_up_w', 'mla_kv_norm_w', 'mla_kv_up_w', 'mla_out_w', 'gdn_in_w', 'gdn_conv_w', 'gdn_a_log', 'gdn_dt_bias', 'gdn_norm_w', 'gdn_out_w', 'ln_g', 'ln_b']
TWIN_DIFF_INPUT = 'x'
TWIN_INPUTS = ['x', 'positions', 'ssd_in_w', 'ssd_conv_w', 'ssd_conv_b', 'ssd_dt_bias', 'ssd_a_log', 'ssd_d', 'ssd_norm_w', 'ssd_out_w', 'mla_in_w', 'mla_q_norm_w', 'mla_q_up_w', 'mla_kv_norm_w', 'mla_kv_up_w', 'mla_out_w', 'gdn_in_w', 'gdn_conv_w', 'gdn_a_log', 'gdn_dt_bias', 'gdn_norm_w', 'gdn_out_w', 'ln_g', 'ln_b', 'loss_target', 'm_ssd_in_w', 'm_ssd_conv_w', 'm_ssd_conv_b', 'm_ssd_dt_bias', 'm_ssd_a_log', 'm_ssd_d', 'm_ssd_norm_w', 'm_ssd_out_w', 'm_mla_in_w', 'm_mla_q_norm_w', 'm_mla_q_up_w', 'm_mla_kv_norm_w', 'm_mla_kv_up_w', 'm_mla_out_w', 'm_gdn_in_w', 'm_gdn_conv_w', 'm_gdn_a_log', 'm_gdn_dt_bias', 'm_gdn_norm_w', 'm_gdn_out_w', 'm_ln_g', 'm_ln_b', 'v_ssd_in_w', 'v_ssd_conv_w', 'v_ssd_conv_b', 'v_ssd_dt_bias', 'v_ssd_a_log', 'v_ssd_d', 'v_ssd_norm_w', 'v_ssd_out_w', 'v_mla_in_w', 'v_mla_q_norm_w', 'v_mla_q_up_w', 'v_mla_kv_norm_w', 'v_mla_kv_up_w', 'v_mla_out_w', 'v_gdn_in_w', 'v_gdn_conv_w', 'v_gdn_a_log', 'v_gdn_dt_bias', 'v_gdn_norm_w', 'v_gdn_out_w', 'v_ln_g', 'v_ln_b']
TWIN_OUTPUTS = ['loss', 'grad_x', 'grad_ssd_in_w', 'grad_ssd_conv_w', 'grad_ssd_conv_b', 'grad_ssd_dt_bias', 'grad_ssd_a_log', 'grad_ssd_d', 'grad_ssd_norm_w', 'grad_ssd_out_w', 'grad_mla_in_w', 'grad_mla_q_norm_w', 'grad_mla_q_up_w', 'grad_mla_kv_norm_w', 'grad_mla_kv_up_w', 'grad_mla_out_w', 'grad_gdn_in_w', 'grad_gdn_conv_w', 'grad_gdn_a_log', 'grad_gdn_dt_bias', 'grad_gdn_norm_w', 'grad_gdn_out_w', 'grad_ln_g', 'grad_ln_b', 'delta_ssd_in_w', 'delta_ssd_conv_w', 'delta_ssd_conv_b', 'delta_ssd_dt_bias', 'delta_ssd_a_log', 'delta_ssd_d', 'delta_ssd_norm_w', 'delta_ssd_out_w', 'delta_mla_in_w', 'delta_mla_q_norm_w', 'delta_mla_q_up_w', 'delta_mla_kv_norm_w', 'delta_mla_kv_up_w', 'delta_mla_out_w', 'delta_gdn_in_w', 'delta_gdn_conv_w', 'delta_gdn_a_log', 'delta_gdn_dt_bias', 'delta_gdn_norm_w', 'delta_gdn_out_w', 'delta_ln_g', 'delta_ln_b', 'new_m_ssd_in_w', 'new_m_ssd_conv_w', 'new_m_ssd_conv_b', 'new_m_ssd_dt_bias', 'new_m_ssd_a_log', 'new_m_ssd_d', 'new_m_ssd_norm_w', 'new_m_ssd_out_w', 'new_m_mla_in_w', 'new_m_mla_q_norm_w', 'new_m_mla_q_up_w', 'new_m_mla_kv_norm_w', 'new_m_mla_kv_up_w', 'new_m_mla_out_w', 'new_m_gdn_in_w', 'new_m_gdn_conv_w', 'new_m_gdn_a_log', 'new_m_gdn_dt_bias', 'new_m_gdn_norm_w', 'new_m_gdn_out_w', 'new_m_ln_g', 'new_m_ln_b', 'new_v_ssd_in_w', 'new_v_ssd_conv_w', 'new_v_ssd_conv_b', 'new_v_ssd_dt_bias', 'new_v_ssd_a_log', 'new_v_ssd_d', 'new_v_ssd_norm_w', 'new_v_ssd_out_w', 'new_v_mla_in_w', 'new_v_mla_q_norm_w', 'new_v_mla_q_up_w', 'new_v_mla_kv_norm_w', 'new_v_mla_kv_up_w', 'new_v_mla_out_w', 'new_v_gdn_in_w', 'new_v_gdn_conv_w', 'new_v_gdn_a_log', 'new_v_gdn_dt_bias', 'new_v_gdn_norm_w', 'new_v_gdn_out_w', 'new_v_ln_g', 'new_v_ln_b']
TWIN_LEAF_KINDS = {'loss': 'loss', 'grad_x': 'grad_x', 'grad_ssd_in_w': 'grad_w', 'grad_ssd_conv_w': 'grad_w', 'grad_ssd_conv_b': 'grad_w', 'grad_ssd_dt_bias': 'grad_w', 'grad_ssd_a_log': 'grad_w', 'grad_ssd_d': 'grad_w', 'grad_ssd_norm_w': 'grad_w', 'grad_ssd_out_w': 'grad_w', 'grad_mla_in_w': 'grad_w', 'grad_mla_q_norm_w': 'grad_w', 'grad_mla_q_up_w': 'grad_w', 'grad_mla_kv_norm_w': 'grad_w', 'grad_mla_kv_up_w': 'grad_w', 'grad_mla_out_w': 'grad_w', 'grad_gdn_in_w': 'grad_w', 'grad_gdn_conv_w': 'grad_w', 'grad_gdn_a_log': 'grad_w', 'grad_gdn_dt_bias': 'grad_w', 'grad_gdn_norm_w': 'grad_w', 'grad_gdn_out_w': 'grad_w', 'grad_ln_g': 'grad_w', 'grad_ln_b': 'grad_w', 'delta_ssd_in_w': 'delta_w', 'delta_ssd_conv_w': 'delta_w', 'delta_ssd_conv_b': 'delta_w', 'delta_ssd_dt_bias': 'delta_w', 'delta_ssd_a_log': 'delta_w', 'delta_ssd_d': 'delta_w', 'delta_ssd_norm_w': 'delta_w', 'delta_ssd_out_w': 'delta_w', 'delta_mla_in_w': 'delta_w', 'delta_mla_q_norm_w': 'delta_w', 'delta_mla_q_up_w': 'delta_w', 'delta_mla_kv_norm_w': 'delta_w', 'delta_mla_kv_up_w': 'delta_w', 'delta_mla_out_w': 'delta_w', 'delta_gdn_in_w': 'delta_w', 'delta_gdn_conv_w': 'delta_w', 'delta_gdn_a_log': 'delta_w', 'delta_gdn_dt_bias': 'delta_w', 'delta_gdn_norm_w': 'delta_w', 'delta_gdn_out_w': 'delta_w', 'delta_ln_g': 'delta_w', 'delta_ln_b': 'delta_w', 'new_m_ssd_in_w': 'new_m', 'new_m_ssd_conv_w': 'new_m', 'new_m_ssd_conv_b': 'new_m', 'new_m_ssd_dt_bias': 'new_m', 'new_m_ssd_a_log': 'new_m', 'new_m_ssd_d': 'new_m', 'new_m_ssd_norm_w': 'new_m', 'new_m_ssd_out_w': 'new_m', 'new_m_mla_in_w': 'new_m', 'new_m_mla_q_norm_w': 'new_m', 'new_m_mla_q_up_w': 'new_m', 'new_m_mla_kv_norm_w': 'new_m', 'new_m_mla_kv_up_w': 'new_m', 'new_m_mla_out_w': 'new_m', 'new_m_gdn_in_w': 'new_m', 'new_m_gdn_conv_w': 'new_m', 'new_m_gdn_a_log': 'new_m', 'new_m_gdn_dt_bias': 'new_m', 'new_m_gdn_norm_w': 'new_m', 'new_m_gdn_out_w': 'new_m', 'new_m_ln_g': 'new_m', 'new_m_ln_b': 'new_m', 'new_v_ssd_in_w': 'new_v', 'new_v_ssd_conv_w': 'new_v', 'new_v_ssd_conv_b': 'new_v', 'new_v_ssd_dt_bias': 'new_v', 'new_v_ssd_a_log': 'new_v', 'new_v_ssd_d': 'new_v', 'new_v_ssd_norm_w': 'new_v', 'new_v_ssd_out_w': 'new_v', 'new_v_mla_in_w': 'new_v', 'new_v_mla_q_norm_w': 'new_v', 'new_v_mla_q_up_w': 'new_v', 'new_v_mla_kv_norm_w': 'new_v', 'new_v_mla_kv_up_w': 'new_v', 'new_v_mla_out_w': 'new_v', 'new_v_gdn_in_w': 'new_v', 'new_v_gdn_conv_w': 'new_v', 'new_v_gdn_a_log': 'new_v', 'new_v_gdn_dt_bias': 'new_v', 'new_v_gdn_norm_w': 'new_v', 'new_v_gdn_out_w': 'new_v', 'new_v_ln_g': 'new_v', 'new_v_ln_b': 'new_v'}


def _forward(args):
    return _fwd_reference(*[args[k] for k in FWD_PARAMS])


def _output_shape():
    def fwd():
        inp = _fwd_setup_inputs(0)
        return _fwd_reference(*[inp[k] for k in FWD_PARAMS])
    out = _jax.eval_shape(fwd)
    return out.shape, out.dtype

N_MICROBATCH = 1
ADAM_LR = 0.001
ADAM_B1 = 0.9
ADAM_B2 = 0.999
ADAM_EPS = 1e-08
ADAM_WD = 0.01
ADAM_STEP = 10
PER_EXAMPLE_BATCH_AXIS = {'x': 0, 'positions': 0, 'loss_target': 0}
SHARED_INPUTS = []
_WEIGHT_DTYPES = {'ssd_in_w': _jnp.float32, 'ssd_conv_w': _jnp.float32, 'ssd_conv_b': _jnp.float32, 'ssd_dt_bias': _jnp.float32, 'ssd_a_log': _jnp.float32, 'ssd_d': _jnp.float32, 'ssd_norm_w': _jnp.float32, 'ssd_out_w': _jnp.float32, 'mla_in_w': _jnp.float32, 'mla_q_norm_w': _jnp.float32, 'mla_q_up_w': _jnp.float32, 'mla_kv_norm_w': _jnp.float32, 'mla_kv_up_w': _jnp.float32, 'mla_out_w': _jnp.float32, 'gdn_in_w': _jnp.float32, 'gdn_conv_w': _jnp.float32, 'gdn_a_log': _jnp.float32, 'gdn_dt_bias': _jnp.float32, 'gdn_norm_w': _jnp.float32, 'gdn_out_w': _jnp.float32, 'ln_g': _jnp.float32, 'ln_b': _jnp.float32}
MOMENT_SCALE = {'ssd_in_w': 1.483566e-02, 'ssd_conv_w': 1.410618e-02, 'ssd_conv_b': 2.348229e-02, 'ssd_dt_bias': 3.561381e-02, 'ssd_a_log': 7.017392e-02, 'ssd_d': 8.546830e-02, 'ssd_norm_w': 1.761427e-02, 'ssd_out_w': 5.827296e-02, 'mla_in_w': 4.396101e-03, 'mla_q_norm_w': 3.914608e-03, 'mla_q_up_w': 2.020383e-03, 'mla_kv_norm_w': 7.387160e-03, 'mla_kv_up_w': 2.650176e-03, 'mla_out_w': 7.421602e-03, 'gdn_in_w': 9.078454e-03, 'gdn_conv_w': 9.371503e-03, 'gdn_a_log': 5.132408e-02, 'gdn_dt_bias': 4.986350e-02, 'gdn_norm_w': 5.857022e-02, 'gdn_out_w': 3.546295e-02, 'ln_g': 8.034699e+00, 'ln_b': 6.817138e-01}


def _to_microbatches(a, axis):
    t = _jnp.moveaxis(a, axis, 0)
    t = t.reshape((N_MICROBATCH, t.shape[0] // N_MICROBATCH) + t.shape[1:])
    return _jnp.moveaxis(t, 1, axis + 1)


def setup_inputs(seed: int = 0) -> dict:
    inp = _fwd_setup_inputs(seed)
    key = _jax.random.fold_in(_jax.random.key(seed), 7919)
    shape, _ = _output_shape()
    out = dict(inp)
    out["loss_target"] = _jax.random.normal(_jax.random.fold_in(key, 0), shape, _jnp.float32)
    for i, name in enumerate(TWIN_WEIGHTS):
        w = inp[name].astype(_jnp.float32)
        if MOMENT_SCALE is None:
            s = _jnp.sqrt(_jnp.mean(_jnp.square(w)) + 1e-30)
        else:
            s = MOMENT_SCALE[name]
        km, kv = _jax.random.split(_jax.random.fold_in(key, i + 1))
        out[name] = w
        out["m_" + name] = s * _jax.random.normal(km, w.shape, _jnp.float32)
        out["v_" + name] = (s * s) * _jax.random.uniform(kv, w.shape, _jnp.float32, 0.5, 1.5)
    if N_MICROBATCH > 1:
        for name, axis in PER_EXAMPLE_BATCH_AXIS.items():
            out[name] = _to_microbatches(out[name], axis)
    return {'x': out['x'], 'positions': out['positions'], 'ssd_in_w': out['ssd_in_w'], 'ssd_conv_w': out['ssd_conv_w'], 'ssd_conv_b': out['ssd_conv_b'], 'ssd_dt_bias': out['ssd_dt_bias'], 'ssd_a_log': out['ssd_a_log'], 'ssd_d': out['ssd_d'], 'ssd_norm_w': out['ssd_norm_w'], 'ssd_out_w': out['ssd_out_w'], 'mla_in_w': out['mla_in_w'], 'mla_q_norm_w': out['mla_q_norm_w'], 'mla_q_up_w': out['mla_q_up_w'], 'mla_kv_norm_w': out['mla_kv_norm_w'], 'mla_kv_up_w': out['mla_kv_up_w'], 'mla_out_w': out['mla_out_w'], 'gdn_in_w': out['gdn_in_w'], 'gdn_conv_w': out['gdn_conv_w'], 'gdn_a_log': out['gdn_a_log'], 'gdn_dt_bias': out['gdn_dt_bias'], 'gdn_norm_w': out['gdn_norm_w'], 'gdn_out_w': out['gdn_out_w'], 'ln_g': out['ln_g'], 'ln_b': out['ln_b'], 'loss_target': out['loss_target'], 'm_ssd_in_w': out['m_ssd_in_w'], 'm_ssd_conv_w': out['m_ssd_conv_w'], 'm_ssd_conv_b': out['m_ssd_conv_b'], 'm_ssd_dt_bias': out['m_ssd_dt_bias'], 'm_ssd_a_log': out['m_ssd_a_log'], 'm_ssd_d': out['m_ssd_d'], 'm_ssd_norm_w': out['m_ssd_norm_w'], 'm_ssd_out_w': out['m_ssd_out_w'], 'm_mla_in_w': out['m_mla_in_w'], 'm_mla_q_norm_w': out['m_mla_q_norm_w'], 'm_mla_q_up_w': out['m_mla_q_up_w'], 'm_mla_kv_norm_w': out['m_mla_kv_norm_w'], 'm_mla_kv_up_w': out['m_mla_kv_up_w'], 'm_mla_out_w': out['m_mla_out_w'], 'm_gdn_in_w': out['m_gdn_in_w'], 'm_gdn_conv_w': out['m_gdn_conv_w'], 'm_gdn_a_log': out['m_gdn_a_log'], 'm_gdn_dt_bias': out['m_gdn_dt_bias'], 'm_gdn_norm_w': out['m_gdn_norm_w'], 'm_gdn_out_w': out['m_gdn_out_w'], 'm_ln_g': out['m_ln_g'], 'm_ln_b': out['m_ln_b'], 'v_ssd_in_w': out['v_ssd_in_w'], 'v_ssd_conv_w': out['v_ssd_conv_w'], 'v_ssd_conv_b': out['v_ssd_conv_b'], 'v_ssd_dt_bias': out['v_ssd_dt_bias'], 'v_ssd_a_log': out['v_ssd_a_log'], 'v_ssd_d': out['v_ssd_d'], 'v_ssd_norm_w': out['v_ssd_norm_w'], 'v_ssd_out_w': out['v_ssd_out_w'], 'v_mla_in_w': out['v_mla_in_w'], 'v_mla_q_norm_w': out['v_mla_q_norm_w'], 'v_mla_q_up_w': out['v_mla_q_up_w'], 'v_mla_kv_norm_w': out['v_mla_kv_norm_w'], 'v_mla_kv_up_w': out['v_mla_kv_up_w'], 'v_mla_out_w': out['v_mla_out_w'], 'v_gdn_in_w': out['v_gdn_in_w'], 'v_gdn_conv_w': out['v_gdn_conv_w'], 'v_gdn_a_log': out['v_gdn_a_log'], 'v_gdn_dt_bias': out['v_gdn_dt_bias'], 'v_gdn_norm_w': out['v_gdn_norm_w'], 'v_gdn_out_w': out['v_gdn_out_w'], 'v_ln_g': out['v_ln_g'], 'v_ln_b': out['v_ln_b']}


def _loss(weights, diff, rest, loss_target):
    with _jax.named_scope("forward"):
        args = {**rest, TWIN_DIFF_INPUT: diff, **{k: w.astype(_WEIGHT_DTYPES[k]) for k, w in weights.items()}}
        y = _forward(args)
    with _jax.named_scope("loss_head"):
        err = _jnp.square(y.astype(_jnp.float32) - loss_target)
        return 0.5 * _jnp.sum(_jnp.mean(err, axis=-1)) if err.ndim else 0.5 * err


def _adamw(w, g, m, v):
    m = ADAM_B1 * m + (1.0 - ADAM_B1) * g
    v = ADAM_B2 * v + (1.0 - ADAM_B2) * _jnp.square(g)
    m_hat = m / (1.0 - ADAM_B1 ** ADAM_STEP)
    v_hat = v / (1.0 - ADAM_B2 ** ADAM_STEP)
    delta = -ADAM_LR * (m_hat / (_jnp.sqrt(v_hat) + ADAM_EPS) + ADAM_WD * w)
    return delta, m, v


def reference(x, positions, ssd_in_w, ssd_conv_w, ssd_conv_b, ssd_dt_bias, ssd_a_log, ssd_d, ssd_norm_w, ssd_out_w, mla_in_w, mla_q_norm_w, mla_q_up_w, mla_kv_norm_w, mla_kv_up_w, mla_out_w, gdn_in_w, gdn_conv_w, gdn_a_log, gdn_dt_bias, gdn_norm_w, gdn_out_w, ln_g, ln_b, loss_target, m_ssd_in_w, m_ssd_conv_w, m_ssd_conv_b, m_ssd_dt_bias, m_ssd_a_log, m_ssd_d, m_ssd_norm_w, m_ssd_out_w, m_mla_in_w, m_mla_q_norm_w, m_mla_q_up_w, m_mla_kv_norm_w, m_mla_kv_up_w, m_mla_out_w, m_gdn_in_w, m_gdn_conv_w, m_gdn_a_log, m_gdn_dt_bias, m_gdn_norm_w, m_gdn_out_w, m_ln_g, m_ln_b, v_ssd_in_w, v_ssd_conv_w, v_ssd_conv_b, v_ssd_dt_bias, v_ssd_a_log, v_ssd_d, v_ssd_norm_w, v_ssd_out_w, v_mla_in_w, v_mla_q_norm_w, v_mla_q_up_w, v_mla_kv_norm_w, v_mla_kv_up_w, v_mla_out_w, v_gdn_in_w, v_gdn_conv_w, v_gdn_a_log, v_gdn_dt_bias, v_gdn_norm_w, v_gdn_out_w, v_ln_g, v_ln_b):
    given = dict(x=x, positions=positions, ssd_in_w=ssd_in_w, ssd_conv_w=ssd_conv_w, ssd_conv_b=ssd_conv_b, ssd_dt_bias=ssd_dt_bias, ssd_a_log=ssd_a_log, ssd_d=ssd_d, ssd_norm_w=ssd_norm_w, ssd_out_w=ssd_out_w, mla_in_w=mla_in_w, mla_q_norm_w=mla_q_norm_w, mla_q_up_w=mla_q_up_w, mla_kv_norm_w=mla_kv_norm_w, mla_kv_up_w=mla_kv_up_w, mla_out_w=mla_out_w, gdn_in_w=gdn_in_w, gdn_conv_w=gdn_conv_w, gdn_a_log=gdn_a_log, gdn_dt_bias=gdn_dt_bias, gdn_norm_w=gdn_norm_w, gdn_out_w=gdn_out_w, ln_g=ln_g, ln_b=ln_b, loss_target=loss_target, m_ssd_in_w=m_ssd_in_w, m_ssd_conv_w=m_ssd_conv_w, m_ssd_conv_b=m_ssd_conv_b, m_ssd_dt_bias=m_ssd_dt_bias, m_ssd_a_log=m_ssd_a_log, m_ssd_d=m_ssd_d, m_ssd_norm_w=m_ssd_norm_w, m_ssd_out_w=m_ssd_out_w, m_mla_in_w=m_mla_in_w, m_mla_q_norm_w=m_mla_q_norm_w, m_mla_q_up_w=m_mla_q_up_w, m_mla_kv_norm_w=m_mla_kv_norm_w, m_mla_kv_up_w=m_mla_kv_up_w, m_mla_out_w=m_mla_out_w, m_gdn_in_w=m_gdn_in_w, m_gdn_conv_w=m_gdn_conv_w, m_gdn_a_log=m_gdn_a_log, m_gdn_dt_bias=m_gdn_dt_bias, m_gdn_norm_w=m_gdn_norm_w, m_gdn_out_w=m_gdn_out_w, m_ln_g=m_ln_g, m_ln_b=m_ln_b, v_ssd_in_w=v_ssd_in_w, v_ssd_conv_w=v_ssd_conv_w, v_ssd_conv_b=v_ssd_conv_b, v_ssd_dt_bias=v_ssd_dt_bias, v_ssd_a_log=v_ssd_a_log, v_ssd_d=v_ssd_d, v_ssd_norm_w=v_ssd_norm_w, v_ssd_out_w=v_ssd_out_w, v_mla_in_w=v_mla_in_w, v_mla_q_norm_w=v_mla_q_norm_w, v_mla_q_up_w=v_mla_q_up_w, v_mla_kv_norm_w=v_mla_kv_norm_w, v_mla_kv_up_w=v_mla_kv_up_w, v_mla_out_w=v_mla_out_w, v_gdn_in_w=v_gdn_in_w, v_gdn_conv_w=v_gdn_conv_w, v_gdn_a_log=v_gdn_a_log, v_gdn_dt_bias=v_gdn_dt_bias, v_gdn_norm_w=v_gdn_norm_w, v_gdn_out_w=v_gdn_out_w, v_ln_g=v_ln_g, v_ln_b=v_ln_b)
    weights = {n: given[n] for n in TWIN_WEIGHTS}
    shared = {n: given[n] for n in SHARED_INPUTS}
    per_example = {n: given[n] for n in ['x', 'positions']}
    grad_fn = _jax.value_and_grad(_loss, argnums=(0, 1))

    def one_microbatch(ex, loss_target):
        ex = dict(ex)
        diff = ex.pop(TWIN_DIFF_INPUT)
        return grad_fn(weights, diff, {**shared, **ex}, loss_target)

    if N_MICROBATCH == 1:
        loss, (grad_w, grad_x) = one_microbatch(per_example, given["loss_target"])
    else:
        def body(carry, xs):
            loss_sum, grad_sum = carry
            l_k, (gw_k, gx_k) = one_microbatch(xs[0], xs[1])
            with _jax.named_scope("update"):
                return (loss_sum + l_k, _jax.tree.map(_jnp.add, grad_sum, gw_k)), gx_k

        init = (_jnp.zeros((), _jnp.float32), _jax.tree.map(_jnp.zeros_like, weights))
        (loss, grad_w), grad_x = _jax.lax.scan(body, init, (per_example, given["loss_target"]))
    with _jax.named_scope("update"):
        delta_w, new_m, new_v = {}, {}, {}
        for n in TWIN_WEIGHTS:
            delta_w[n], new_m[n], new_v[n] = _adamw(weights[n], grad_w[n], given["m_" + n], given["v_" + n])
    return (loss, grad_x, *[grad_w[n] for n in TWIN_WEIGHTS], *[delta_w[n] for n in TWIN_WEIGHTS],
            *[new_m[n] for n in TWIN_WEIGHTS], *[new_v[n] for n in TWIN_WEIGHTS])
```

```python
import functools

import jax
import jax.numpy as jnp
from jax import lax
from jax.experimental import pallas as pl
from jax.experimental.pallas import tpu as pltpu

F32 = jnp.float32
MXU = jnp.bfloat16
WIRE = jnp.bfloat16
HI = lax.Precision.HIGHEST

N_DEV = 8
LANES = 128
VMEM_LIMIT = 56 * 1024 * 1024

D_MODEL = 2048
DEPTH = 4
ALPHA = (2.0 * DEPTH) ** 0.25
LN_EPS = 1e-5
RMS_EPS = 1e-6

SSD_DI = 4096
SSD_P = 64
SSD_H = 64
SSD_G = 8
SSD_N = 128
SSD_L = 128
SSD_GS = SSD_DI // SSD_G
SSD_CONV_DIM = SSD_DI + 2 * SSD_G * SSD_N
SSD_PROJ_PAD = SSD_DI + SSD_CONV_DIM + LANES

MLA_H = 16
MLA_QR = 768
MLA_KVR = 512
MLA_NOPE = 128
MLA_ROPE = 64
MLA_V = 128
MLA_GATE = MLA_H * MLA_V
MLA_PROJ_PAD = MLA_QR + MLA_KVR + LANES + MLA_GATE
MLA_SCALE = (MLA_NOPE + MLA_ROPE) ** -0.5
ROPE_THETA = 10000.0
ATT_BLK = 512

GDN_HK = 16
GDN_HV = 32
GDN_DK = 128
GDN_DV = 128
GDN_KEY = GDN_HK * GDN_DK
GDN_VAL = GDN_HV * GDN_DV
GDN_L = 64
GDN_CONV_DIM = 2 * GDN_KEY + GDN_VAL
GDN_PROJ_PAD = 12800

ADAM_LR = 0.001
ADAM_B1 = 0.9
ADAM_B2 = 0.999
ADAM_EPS = 1e-08
ADAM_WD = 0.01
ADAM_STEP = 10


def _cparams(sem=None):
    return pltpu.CompilerParams(dimension_semantics=sem, vmem_limit_bytes=VMEM_LIMIT)


def _tile(n, cap):
    if n <= cap:
        return n
    best = None
    for d in range(LANES, cap + 1, LANES):
        if n % d == 0:
            best = d
    assert best is not None, (n, cap)
    return best


def _dg(a, b, ca, cb, prec=None):
    return lax.dot_general(a, b, (((ca,), (cb,)), ((), ())), preferred_element_type=F32, precision=prec)


def _mx(a):
    return a.astype(MXU)


def _hd(a, b):
    return _dg(a, b, 1, 0, HI)


@jax.custom_vjp
def _nn(a, b):
    return _dg(_mx(a), _mx(b), 1, 0)


def _nn_f(a, b):
    return _nn(a, b), (a, b)


def _nn_b(res, ct):
    a, b = res
    return _dg(_mx(ct), _mx(b), 1, 1), _dg(_mx(a), _mx(ct), 0, 0)


_nn.defvjp(_nn_f, _nn_b)


@jax.custom_vjp
def _nt(a, b):
    return _dg(_mx(a), _mx(b), 1, 1)


def _nt_f(a, b):
    return _nt(a, b), (a, b)


def _nt_b(res, ct):
    a, b = res
    return _dg(_mx(ct), _mx(b), 1, 0), _dg(_mx(ct), _mx(a), 0, 0)


_nt.defvjp(_nt_f, _nt_b)


@jax.custom_vjp
def _tn(a, b):
    return _dg(_mx(a), _mx(b), 0, 0)


def _tn_f(a, b):
    return _tn(a, b), (a, b)


def _tn_b(res, ct):
    a, b = res
    return _dg(_mx(b), _mx(ct), 1, 1), _dg(_mx(a), _mx(ct), 1, 0)


_tn.defvjp(_tn_f, _tn_b)


def _softplus(x):
    return jnp.maximum(x, 0.0) + jnp.log(1.0 + jnp.exp(-jnp.abs(x)))


def _silu(x):
    return x * jax.nn.sigmoid(x)


def _mm(a, b, *, ta=False, tb=False, out_dtype=F32, name):
    if ta:
        kdim, m = a.shape
    else:
        m, kdim = a.shape
    if tb:
        n, kb = b.shape
    else:
        kb, n = b.shape
    assert kdim == kb, (a.shape, b.shape, ta, tb)
    tm, tn, tk = _tile(m, 512), _tile(n, 1280), _tile(kdim, 512)
    nk = kdim // tk
    ca, cb = (0 if ta else 1), (1 if tb else 0)

    def body(a_ref, b_ref, o_ref, acc_ref):
        k = pl.program_id(2)

        @pl.when(k == 0)
        def _():
            acc_ref[...] = jnp.zeros_like(acc_ref)

        acc_ref[...] += _dg(_mx(a_ref[...]), _mx(b_ref[...]), ca, cb)

        @pl.when(k == nk - 1)
        def _():
            o_ref[...] = acc_ref[...].astype(out_dtype)

    a_spec = pl.BlockSpec((tk, tm), lambda i, j, k: (k, i)) if ta else pl.BlockSpec((tm, tk), lambda i, j, k: (i, k))
    b_spec = pl.BlockSpec((tn, tk), lambda i, j, k: (j, k)) if tb else pl.BlockSpec((tk, tn), lambda i, j, k: (k, j))
    return pl.pallas_call(
        body, name=name, grid=(m // tm, n // tn, nk),
        in_specs=[a_spec, b_spec], out_specs=pl.BlockSpec((tm, tn), lambda i, j, k: (i, j)),
        out_shape=jax.ShapeDtypeStruct((m, n), out_dtype),
        scratch_shapes=[pltpu.VMEM((tm, tn), F32)],
        compiler_params=_cparams(("parallel", "parallel", "arbitrary")),
    )(a, b)


def _ln_fwd(h, y, g, b, name):
    t, d = h.shape
    tr = _tile(t, 256)

    def body(h_ref, y_ref, g_ref, b_ref, o_ref):
        r = ALPHA * h_ref[...] + y_ref[...]
        mu = jnp.mean(r, -1, keepdims=True)
        xc = r - mu
        var = jnp.mean(xc * xc, -1, keepdims=True)
        o_ref[...] = xc * lax.rsqrt(var + LN_EPS) * g_ref[...] + b_ref[...]

    row = pl.BlockSpec((tr, d), lambda i: (i, 0))
    par = pl.BlockSpec((1, d), lambda i: (0, 0))
    return pl.pallas_call(
        body, name=name, grid=(t // tr,), in_specs=[row, row, par, par], out_specs=row,
        out_shape=jax.ShapeDtypeStruct((t, d), F32), compiler_params=_cparams(("parallel",)),
    )(h, y, g, b)


def _ln_bwd(h, y, g, dr_up, du_up, name):
    t, d = h.shape
    tr = _tile(t, 256)

    def body(h_ref, y_ref, g_ref, dr_ref, du_ref, o_ref, dg_ref, db_ref):
        i = pl.program_id(0)

        @pl.when(i == 0)
        def _():
            dg_ref[...] = jnp.zeros_like(dg_ref)
            db_ref[...] = jnp.zeros_like(db_ref)

        dout = ALPHA * dr_ref[...] + du_ref[...]
        r = ALPHA * h_ref[...] + y_ref[...]
        mu = jnp.mean(r, -1, keepdims=True)
        xc = r - mu
        rstd = lax.rsqrt(jnp.mean(xc * xc, -1, keepdims=True) + LN_EPS)
        xh = xc * rstd
        dxh = dout * g_ref[...]
        o_ref[...] = rstd * (dxh - jnp.mean(dxh, -1, keepdims=True) - xh * jnp.mean(dxh * xh, -1, keepdims=True))
        dg_ref[...] += jnp.sum(dout * xh, 0, keepdims=True)
        db_ref[...] += jnp.sum(dout, 0, keepdims=True)

    row = pl.BlockSpec((tr, d), lambda i: (i, 0))
    par = pl.BlockSpec((1, d), lambda i: (0, 0))
    return pl.pallas_call(
        body, name=name, grid=(t // tr,), in_specs=[row, row, par, row, row], out_specs=[row, par, par],
        out_shape=[jax.ShapeDtypeStruct((t, d), F32), jax.ShapeDtypeStruct((1, d), F32), jax.ShapeDtypeStruct((1, d), F32)],
        compiler_params=_cparams(("arbitrary",)),
    )(h, y, g, dr_up, du_up)


def _loss_head(h, tgt, name):
    t, d = h.shape
    tr = _tile(t, 256)

    def body(h_ref, t_ref, l_ref, d_ref):
        i = pl.program_id(0)

        @pl.when(i == 0)
        def _():
            l_ref[...] = jnp.zeros_like(l_ref)

        e = h_ref[...] - t_ref[...]
        d_ref[...] = e * (1.0 / d)
        l_ref[...] += 0.5 * jnp.sum(jnp.mean(e * e, -1, keepdims=True))

    row = pl.BlockSpec((tr, d), lambda i: (i, 0))
    return pl.pallas_call(
        body, name=name, grid=(t // tr,), in_specs=[row, row],
        out_specs=[pl.BlockSpec((8, LANES), lambda i: (0, 0)), row],
        out_shape=[jax.ShapeDtypeStruct((8, LANES), F32), jax.ShapeDtypeStruct((t, d), F32)],
        compiler_params=_cparams(("arbitrary",)),
    )(h, tgt)


def _axpy(dr, du, name):
    t, d = dr.shape
    tr = _tile(t, 256)

    def body(a_ref, b_ref, o_ref):
        o_ref[...] = ALPHA * a_ref[...] + b_ref[...]

    row = pl.BlockSpec((tr, d), lambda i: (i, 0))
    return pl.pallas_call(
        body, name=name, grid=(t // tr,), in_specs=[row, row], out_specs=row,
        out_shape=jax.ShapeDtypeStruct((t, d), F32), compiler_params=_cparams(("parallel",)),
    )(dr, du)


CONV_TT = 512
CONV_TC = 512


def _shift_down(cur, prev, s, row):
    if s == 0:
        return cur
    return jnp.where(row >= s, pltpu.roll(cur, s, 0), pltpu.roll(prev, s, 0))


def _shift_up(cur, nxt, s, row, tt):
    if s == 0:
        return cur
    return jnp.where(row < tt - s, pltpu.roll(cur, tt - s, 0), pltpu.roll(nxt, tt - s, 0))


def _conv_fwd(proj, col0, w, b, name):
    t = proj.shape[0]
    c = w.shape[1]
    tt = _tile(t, CONV_TT)
    cb0 = col0 // CONV_TC

    def body(x_ref, p_ref, w_ref, b_ref, o_ref):
        i = pl.program_id(1)
        x = x_ref[...]
        p = jnp.where(i > 0, p_ref[...], 0.0)
        row = lax.broadcasted_iota(jnp.int32, x.shape, 0)
        pre = b_ref[...] + w_ref[3:4, :] * x
        for s in (1, 2, 3):
            pre = pre + w_ref[3 - s:4 - s, :] * _shift_down(x, p, s, row)
        o_ref[...] = _silu(pre)

    return pl.pallas_call(
        body, name=name, grid=(c // CONV_TC, t // tt),
        in_specs=[pl.BlockSpec((tt, CONV_TC), lambda j, i: (i, cb0 + j)),
                  pl.BlockSpec((tt, CONV_TC), lambda j, i: (jnp.maximum(i - 1, 0), cb0 + j)),
                  pl.BlockSpec((4, CONV_TC), lambda j, i: (0, j)),
                  pl.BlockSpec((1, CONV_TC), lambda j, i: (0, j))],
        out_specs=pl.BlockSpec((tt, CONV_TC), lambda j, i: (i, j)),
        out_shape=jax.ShapeDtypeStruct((t, c), F32), compiler_params=_cparams(("parallel", "parallel")),
    )(proj, proj, w, b)


def _conv_bwd_pre(proj, col0, w, b, dact, name):
    t = proj.shape[0]
    c = w.shape[1]
    tt = _tile(t, CONV_TT)
    cb0 = col0 // CONV_TC

    def body(x_ref, p_ref, w_ref, b_ref, d_ref, dpre_ref, dw_ref, db_ref):
        i = pl.program_id(1)

        @pl.when(i == 0)
        def _():
            dw_ref[...] = jnp.zeros_like(dw_ref)
            db_ref[...] = jnp.zeros_like(db_ref)

        x = x_ref[...]
        p = jnp.where(i > 0, p_ref[...], 0.0)
        row = lax.broadcasted_iota(jnp.int32, x.shape, 0)
        sh = [_shift_down(x, p, s, row) for s in range(4)]
        pre = b_ref[...] + w_ref[3:4, :] * sh[0]
        for s in (1, 2, 3):
            pre = pre + w_ref[3 - s:4 - s, :] * sh[s]
        sg = jax.nn.sigmoid(pre)
        dpre = d_ref[...] * (sg * (1.0 + pre * (1.0 - sg)))
        dpre_ref[...] = dpre
        for s in range(4):
            dw_ref[3 - s:4 - s, :] += jnp.sum(dpre * sh[s], 0, keepdims=True)
        db_ref[...] += jnp.sum(dpre, 0, keepdims=True)

    return pl.pallas_call(
        body, name=name, grid=(c // CONV_TC, t // tt),
        in_specs=[pl.BlockSpec((tt, CONV_TC), lambda j, i: (i, cb0 + j)),
                  pl.BlockSpec((tt, CONV_TC), lambda j, i: (jnp.maximum(i - 1, 0), cb0 + j)),
                  pl.BlockSpec((4, CONV_TC), lambda j, i: (0, j)),
                  pl.BlockSpec((1, CONV_TC), lambda j, i: (0, j)),
                  pl.BlockSpec((tt, CONV_TC), lambda j, i: (i, j))],
        out_specs=[pl.BlockSpec((tt, CONV_TC), lambda j, i: (i, j)),
                   pl.BlockSpec((4, CONV_TC), lambda j, i: (0, j)),
                   pl.BlockSpec((1, CONV_TC), lambda j, i: (0, j))],
        out_shape=[jax.ShapeDtypeStruct((t, c), F32), jax.ShapeDtypeStruct((4, c), F32), jax.ShapeDtypeStruct((1, c), F32)],
        compiler_params=_cparams(("parallel", "arbitrary")),
    )(proj, proj, w, b, dact)


def _conv_bwd_x(dpre, w, name):
    t, c = dpre.shape
    tt = _tile(t, CONV_TT)
    nt = t // tt

    def body(d_ref, n_ref, w_ref, o_ref):
        i = pl.program_id(1)
        d = d_ref[...]
        nx = jnp.where(i < nt - 1, n_ref[...], 0.0)
        row = lax.broadcasted_iota(jnp.int32, d.shape, 0)
        acc = w_ref[3:4, :] * d
        for s in (1, 2, 3):
            acc = acc + w_ref[3 - s:4 - s, :] * _shift_up(d, nx, s, row, tt)
        o_ref[...] = acc

    return pl.pallas_call(
        body, name=name, grid=(c // CONV_TC, nt),
        in_specs=[pl.BlockSpec((tt, CONV_TC), lambda j, i: (i, j)),
                  pl.BlockSpec((tt, CONV_TC), lambda j, i: (jnp.minimum(i + 1, nt - 1), j)),
                  pl.BlockSpec((4, CONV_TC), lambda j, i: (0, j))],
        out_specs=pl.BlockSpec((tt, CONV_TC), lambda j, i: (i, j)),
        out_shape=jax.ShapeDtypeStruct((t, c), F32), compiler_params=_cparams(("parallel", "parallel")),
    )(dpre, dpre, w)


def _ssd_chunk(g, x, z, bm, cm, dtraw, dtb, alog, dsk, nw, prev):
    L = x.shape[0]
    r_i = lax.broadcasted_iota(jnp.int32, (L, L), 0)
    c_i = lax.broadcasted_iota(jnp.int32, (L, L), 1)
    causal = r_i >= c_i
    dt = _softplus(dtraw + dtb)
    a = dt * (-jnp.exp(alog))
    acs = _hd(causal.astype(F32), a)
    e_r = lax.broadcasted_iota(jnp.int32, (LANES, SSD_GS), 0)
    e_c = lax.broadcasted_iota(jnp.int32, (LANES, SSD_GS), 1)
    sel = (e_r == g * (SSD_H // SSD_G) + jnp.right_shift(e_c, 6)).astype(F32)
    dt_x = _hd(dt, sel)
    acs_x = _hd(acs, sel)
    d_x = _hd(jnp.broadcast_to(dsk, (8, LANES)), sel)[0:1]
    alast = acs_x[L - 1:L]
    xdt = x * dt_x
    cb = _nt(cm, bm)
    lane = lax.broadcasted_iota(jnp.int32, (L, LANES), 1)
    ys = []
    for j in range(SSD_GS // LANES):
        xp = xdt[:, j * LANES:(j + 1) * LANES]
        yp = None
        for hh in range(2):
            c0 = (2 * j + hh) * SSD_P
            cmx = jnp.broadcast_to(acs_x[:, c0:c0 + 1], (L, L))
            dec = jnp.exp(jnp.where(causal, cmx - cmx.T, -jnp.inf))
            half = (lane < SSD_P) if hh == 0 else (lane >= SSD_P)
            t = _nn(cb * dec, jnp.where(half, xp, 0.0))
            yp = t if yp is None else yp + t
        ys.append(yp)
    y_diag = jnp.concatenate(ys, axis=1)
    st = _tn(bm, xdt * jnp.exp(alast - acs_x))
    new = prev * jnp.exp(alast) + st
    y_off = _nn(cm, prev) * jnp.exp(acs_x)
    y = y_diag + y_off + x * d_x
    yg = y * _silu(z)
    yn = yg * lax.rsqrt(jnp.mean(yg * yg, -1, keepdims=True) + RMS_EPS) * nw
    return yn, new


def _ssd_specs(nc, rev):
    cc = (lambda c: nc - 1 - c) if rev else (lambda c: c)
    zb = 0
    dtb = (SSD_DI + SSD_CONV_DIM) // LANES
    bb = SSD_DI // LANES
    cbk = (SSD_DI + SSD_G * SSD_N) // LANES
    par = pl.BlockSpec((1, LANES), lambda c, g: (0, 0))
    return dict(
        z=pl.BlockSpec((SSD_L, SSD_GS), lambda c, g: (cc(c), zb + g)),
        dt=pl.BlockSpec((SSD_L, LANES), lambda c, g: (cc(c), dtb)),
        x=pl.BlockSpec((SSD_L, SSD_GS), lambda c, g: (cc(c), g)),
        bm=pl.BlockSpec((SSD_L, LANES), lambda c, g: (cc(c), bb + g)),
        cm=pl.BlockSpec((SSD_L, LANES), lambda c, g: (cc(c), cbk + g)),
        par=par,
        nw=pl.BlockSpec((1, SSD_GS), lambda c, g: (0, g)),
        st=pl.BlockSpec((1, 1, SSD_N, SSD_GS), lambda c, g: (cc(c), g, 0, 0)),
        y=pl.BlockSpec((SSD_L, SSD_GS), lambda c, g: (cc(c), g)),
        bc=pl.BlockSpec((SSD_L, LANES), lambda c, g: (cc(c), g)),
        dtout=pl.BlockSpec((SSD_L, LANES), lambda c, g: (cc(c), 0)),
    )


def _ssd_fwd(proj, act, dtb, alog, dsk, nw, name):
    t = proj.shape[0]
    nc = t // SSD_L
    sp = _ssd_specs(nc, False)

    def body(z_ref, dt_ref, x_ref, bm_ref, cm_ref, dtb_ref, alog_ref, dsk_ref, nw_ref, y_ref, st_ref, state):
        c, g = pl.program_id(0), pl.program_id(1)

        @pl.when(c == 0)
        def _():
            state[g] = jnp.zeros((SSD_N, SSD_GS), F32)

        prev = state[g]
        st_ref[0, 0] = prev
        yn, new = _ssd_chunk(g, x_ref[...], z_ref[...], bm_ref[...], cm_ref[...], dt_ref[...], dtb_ref[...],
                             alog_ref[...], dsk_ref[...], nw_ref[...], prev)
        y_ref[...] = yn.astype(y_ref.dtype)
        state[g] = new

    return pl.pallas_call(
        body, name=name, grid=(nc, SSD_G),
        in_specs=[sp["z"], sp["dt"], sp["x"], sp["bm"], sp["cm"], sp["par"], sp["par"], sp["par"], sp["nw"]],
        out_specs=[sp["y"], sp["st"]],
        out_shape=[jax.ShapeDtypeStruct((t, SSD_DI), F32), jax.ShapeDtypeStruct((nc, SSD_G, SSD_N, SSD_GS), F32)],
        scratch_shapes=[pltpu.VMEM((SSD_G, SSD_N, SSD_GS), F32)],
        compiler_params=_cparams(("arbitrary", "arbitrary")),
    )(proj, proj, act, act, act, dtb, alog, dsk, nw)


def _ssd_bwd(proj, act, dtb, alog, dsk, nw, states, dyn, name):
    t = proj.shape[0]
    nc = t // SSD_L
    sp = _ssd_specs(nc, True)

    def body(z_ref, dt_ref, x_ref, bm_ref, cm_ref, dtb_ref, alog_ref, dsk_ref, nw_ref, st_ref, dy_ref,
             dx_ref, dz_ref, dbm_ref, dcm_ref, ddt_ref, ddtb_ref, dalog_ref, ddsk_ref, dnw_ref, dstate):
        c, g = pl.program_id(0), pl.program_id(1)

        @pl.when(c == 0)
        def _():
            dstate[g] = jnp.zeros((SSD_N, SSD_GS), F32)

        @pl.when((c == 0) & (g == 0))
        def _():
            ddtb_ref[...] = jnp.zeros_like(ddtb_ref)
            dalog_ref[...] = jnp.zeros_like(dalog_ref)
            ddsk_ref[...] = jnp.zeros_like(ddsk_ref)
            dnw_ref[...] = jnp.zeros_like(dnw_ref)

        @pl.when(g == 0)
        def _():
            ddt_ref[...] = jnp.zeros_like(ddt_ref)

        _, vjp = jax.vjp(functools.partial(_ssd_chunk, g), x_ref[...], z_ref[...], bm_ref[...], cm_ref[...],
                         dt_ref[...], dtb_ref[...], alog_ref[...], dsk_ref[...], nw_ref[...], st_ref[0, 0])
        dx, dz, dbm, dcm, ddt, ddtb, dalog, ddsk, dnw, dprev = vjp((dy_ref[...].astype(F32), dstate[g]))
        dx_ref[...] = dx
        dz_ref[...] = dz
        dbm_ref[...] = dbm
        dcm_ref[...] = dcm
        ddt_ref[...] += ddt
        ddtb_ref[...] += ddtb
        dalog_ref[...] += dalog
        ddsk_ref[...] += ddsk
        dnw_ref[g] += dnw
        dstate[g] = dprev

    par_out = pl.BlockSpec((1, LANES), lambda c, g: (0, 0))
    sds = jax.ShapeDtypeStruct
    return pl.pallas_call(
        body, name=name, grid=(nc, SSD_G),
        in_specs=[sp["z"], sp["dt"], sp["x"], sp["bm"], sp["cm"], sp["par"], sp["par"], sp["par"], sp["nw"],
                  sp["st"], sp["y"]],
        out_specs=[sp["y"], sp["y"], sp["bc"], sp["bc"], sp["dtout"], par_out, par_out, par_out,
                   pl.BlockSpec((SSD_G, 1, SSD_GS), lambda c, g: (0, 0, 0))],
        out_shape=[sds((t, SSD_DI), F32), sds((t, SSD_DI), F32), sds((t, SSD_G * SSD_N), F32), sds((t, SSD_G * SSD_N), F32),
                   sds((t, LANES), F32), sds((1, LANES), F32), sds((1, LANES), F32), sds((1, LANES), F32),
                   sds((SSD_G, 1, SSD_GS), F32)],
        scratch_shapes=[pltpu.VMEM((SSD_G, SSD_N, SSD_GS), F32)],
        compiler_params=_cparams(("arbitrary", "arbitrary")),
    )(proj, proj, act, act, act, dtb, alog, dsk, nw, states, dyn)


def _pad_lanes(v, width=LANES, offset=0):
    return jnp.zeros((1, width), F32).at[0, offset:offset + v.shape[0]].set(v)


def _ssd_layer_fwd(u, w, ln_g, ln_b, tag):
    w_in = jnp.concatenate([w["in_w"], jnp.zeros((D_MODEL, LANES - SSD_H), w["in_w"].dtype)], axis=1)
    dtb, alog, dsk = _pad_lanes(w["dt_bias"]), _pad_lanes(w["a_log"]), _pad_lanes(w["d"])
    nw = w["norm_w"].reshape(1, SSD_DI)
    cb = w["conv_b"].reshape(1, SSD_CONV_DIM)
    proj = _mm(u, w_in, name=tag + "_in")
    act = _conv_fwd(proj, SSD_DI, w["conv_w"], cb, name=tag + "_conv")
    yn, states = _ssd_fwd(proj, act, dtb, alog, dsk, nw, name=tag + "_scan")
    y = _mm(yn, w["out_w"], name=tag + "_out")
    h = _ln_fwd(u, y, ln_g, ln_b, name=tag + "_ln")
    saved = dict(u=u, w_in=w_in, proj=proj, act=act, states=states, yn=yn, y=y, dtb=dtb, alog=alog, dsk=dsk, nw=nw, cb=cb)
    return h, saved


def _ssd_layer_bwd(s, w, ln_g, dr_up, du_up, tag):
    dr, dg, db = _ln_bwd(s["u"], s["y"], ln_g, dr_up, du_up, name=tag + "_ln_b")
    dyn = _mm(dr, w["out_w"], tb=True, name=tag + "_out_bx")
    d_out_w = _mm(s["yn"], dr, ta=True, out_dtype=WIRE, name=tag + "_out_bw")
    dx, dz, dbm, dcm, ddt, ddtb, dalog, ddsk, dnw = _ssd_bwd(
        s["proj"], s["act"], s["dtb"], s["alog"], s["dsk"], s["nw"], s["states"], dyn, name=tag + "_scan_b")
    dact = jnp.concatenate([dx, dbm, dcm], axis=1)
    dpre, d_conv_w, d_conv_b = _conv_bwd_pre(s["proj"], SSD_DI, w["conv_w"], s["cb"], dact, name=tag + "_conv_bp")
    dxbc = _conv_bwd_x(dpre, w["conv_w"], name=tag + "_conv_bx")
    dproj = jnp.concatenate([dz, dxbc, ddt], axis=1)
    d_in_w = _mm(s["u"], dproj, ta=True, out_dtype=WIRE, name=tag + "_in_bw")
    du = _mm(dproj, s["w_in"], tb=True, name=tag + "_in_bx")
    grads = dict(in_w=d_in_w[:, :SSD_DI + SSD_CONV_DIM + SSD_H], conv_w=d_conv_w, conv_b=d_conv_b.reshape(-1),
                 dt_bias=ddtb[0, :SSD_H], a_log=dalog[0, :SSD_H], d=ddsk[0, :SSD_H], norm_w=dnw.reshape(-1),
                 out_w=d_out_w, ln_g=dg[0], ln_b=db[0])
    return dr, du, grads


MLA_LOW = MLA_QR + MLA_KVR + LANES
MLA_ZB = MLA_LOW // LANES


def _rope_mat():
    r = lax.broadcasted_iota(jnp.int32, (LANES, LANES), 0)
    c = lax.broadcasted_iota(jnp.int32, (LANES, LANES), 1)
    hf = MLA_ROPE // 2
    return jnp.where((c < hf) & (r == c + hf), -1.0, 0.0) + jnp.where((c >= hf) & (c < 2 * hf) & (r == c - hf), 1.0, 0.0)


def _rope(x, cosf, sinf):
    return x * cosf + _hd(x, _rope_mat()) * sinf


def _rope_adj(d, cosf, sinf):
    return d * cosf - _hd(d * sinf, _rope_mat())


def _mla_low_fn(low, qnw, kvnw, cosf, sinf):
    qc, kvc, kr = low[:, :MLA_QR], low[:, MLA_QR:MLA_QR + MLA_KVR], low[:, MLA_QR + MLA_KVR:]
    qn = qc * lax.rsqrt(jnp.mean(qc * qc, -1, keepdims=True) + RMS_EPS) * qnw
    kvn = kvc * lax.rsqrt(jnp.mean(kvc * kvc, -1, keepdims=True) + RMS_EPS) * kvnw
    return qn, kvn, _rope(kr, cosf, sinf)


def _mla_low_fwd(proj, qnw, kvnw, cosf, sinf, name):
    t = proj.shape[0]
    tr = _tile(t, 256)

    def body(low_ref, qnw_ref, kvnw_ref, cos_ref, sin_ref, qn_ref, kvn_ref, kr_ref):
        qn, kvn, kr = _mla_low_fn(low_ref[...], qnw_ref[...], kvnw_ref[...], cos_ref[...], sin_ref[...])
        qn_ref[...] = qn
        kvn_ref[...] = kvn
        kr_ref[...] = kr

    row = lambda wdt: pl.BlockSpec((tr, wdt), lambda i: (i, 0))
    par = lambda wdt: pl.BlockSpec((1, wdt), lambda i: (0, 0))
    sds = jax.ShapeDtypeStruct
    return pl.pallas_call(
        body, name=name, grid=(t // tr,),
        in_specs=[row(MLA_LOW), par(MLA_QR), par(MLA_KVR), row(LANES), row(LANES)],
        out_specs=[row(MLA_QR), row(MLA_KVR), row(LANES)],
        out_shape=[sds((t, MLA_QR), F32), sds((t, MLA_KVR), F32), sds((t, LANES), F32)],
        compiler_params=_cparams(("parallel",)),
    )(proj, qnw, kvnw, cosf, sinf)


def _mla_low_bwd(proj, qnw, kvnw, cosf, sinf, dqn, dkvn, dkr, name):
    t = proj.shape[0]
    tr = _tile(t, 256)

    def body(low_ref, qnw_ref, kvnw_ref, cos_ref, sin_ref, dqn_ref, dkvn_ref, dkr_ref, dlow_ref, dqnw_ref, dkvnw_ref):
        i = pl.program_id(0)

        @pl.when(i == 0)
        def _():
            dqnw_ref[...] = jnp.zeros_like(dqnw_ref)
            dkvnw_ref[...] = jnp.zeros_like(dkvnw_ref)

        cosf, sinf = cos_ref[...], sin_ref[...]
        _, vjp = jax.vjp(lambda a, b, c: _mla_low_fn(a, b, c, cosf, sinf), low_ref[...], qnw_ref[...], kvnw_ref[...])
        dlow, dq, dk = vjp((dqn_ref[...], dkvn_ref[...], dkr_ref[...]))
        dlow_ref[...] = dlow
        dqnw_ref[...] += dq
        dkvnw_ref[...] += dk

    row = lambda wdt: pl.BlockSpec((tr, wdt), lambda i: (i, 0))
    par = lambda wdt: pl.BlockSpec((1, wdt), lambda i: (0, 0))
    sds = jax.ShapeDtypeStruct
    return pl.pallas_call(
        body, name=name, grid=(t // tr,),
        in_specs=[row(MLA_LOW), par(MLA_QR), par(MLA_KVR), row(LANES), row(LANES), row(MLA_QR), row(MLA_KVR), row(LANES)],
        out_specs=[row(MLA_LOW), par(MLA_QR), par(MLA_KVR)],
        out_shape=[sds((t, MLA_LOW), F32), sds((1, MLA_QR), F32), sds((1, MLA_KVR), F32)],
        compiler_params=_cparams(("arbitrary",)),
    )(proj, qnw, kvnw, cosf, sinf, dqn, dkvn, dkr)


def _rope_heads(x, col_blk0, cosf, sinf, adjoint, name):
    t = x.shape[0]
    tr = _tile(t, 512)

    def body(x_ref, cos_ref, sin_ref, o_ref):
        f = _rope_adj if adjoint else _rope
        o_ref[...] = f(x_ref[...], cos_ref[...], sin_ref[...])

    tab = pl.BlockSpec((tr, LANES), lambda i, h: (i, 0))
    return pl.pallas_call(
        body, name=name, grid=(t // tr, MLA_H),
        in_specs=[pl.BlockSpec((tr, LANES), lambda i, h: (i, col_blk0 + h)), tab, tab],
        out_specs=pl.BlockSpec((tr, LANES), lambda i, h: (i, h)),
        out_shape=jax.ShapeDtypeStruct((t, MLA_H * LANES), F32), compiler_params=_cparams(("parallel", "parallel")),
    )(x, cosf, sinf)


def _att_scores(qn, qr, kn, kr, i, j, tq, tk):
    s = (_dg(_mx(qn), _mx(kn), 1, 1) + _dg(_mx(qr), _mx(kr), 1, 1)) * MLA_SCALE
    qpos = i * tq + lax.broadcasted_iota(jnp.int32, (tq, tk), 0)
    kpos = j * tk + lax.broadcasted_iota(jnp.int32, (tq, tk), 1)
    return jnp.where(kpos <= qpos, s, -jnp.inf)


def _attn_fwd(q, qr, kv, kr, proj, name):
    t = q.shape[0]
    tq = tk = _tile(t, ATT_BLK)
    nq = nk = t // tq

    def body(qn_ref, qr_ref, kn_ref, kr_ref, v_ref, z_ref, o_ref, og_ref, lse_ref, m_s, l_s, acc_s):
        i, j = pl.program_id(1), pl.program_id(2)

        @pl.when(j == 0)
        def _():
            m_s[...] = jnp.full_like(m_s, -jnp.inf)
            l_s[...] = jnp.zeros_like(l_s)
            acc_s[...] = jnp.zeros_like(acc_s)

        @pl.when(j <= i)
        def _():
            s = _att_scores(qn_ref[...], qr_ref[...], kn_ref[...], kr_ref[...], i, j, tq, tk)
            m_new = jnp.maximum(m_s[...], jnp.max(s, -1, keepdims=True))
            p = jnp.exp(s - m_new)
            corr = jnp.exp(m_s[...] - m_new)
            l_s[...] = corr * l_s[...] + jnp.sum(p, -1, keepdims=True)
            acc_s[...] = corr * acc_s[...] + _dg(_mx(p), _mx(v_ref[...]), 1, 0)
            m_s[...] = m_new

        @pl.when(j == nk - 1)
        def _():
            o = acc_s[...] / l_s[...]
            o_ref[...] = o
            og_ref[...] = (o * _silu(z_ref[...])).astype(og_ref.dtype)
            lse_ref[...] = jnp.broadcast_to(m_s[...] + jnp.log(l_s[...]), (tq, LANES))

    qs = lambda off: pl.BlockSpec((tq, LANES), lambda h, i, j: (i, off + h))
    ks = lambda off: pl.BlockSpec((tk, LANES), lambda h, i, j: (jnp.minimum(j, i), off + h))
    sds = jax.ShapeDtypeStruct
    return pl.pallas_call(
        body, name=name, grid=(MLA_H, nq, nk),
        in_specs=[qs(0), qs(0), ks(0), pl.BlockSpec((tk, LANES), lambda h, i, j: (jnp.minimum(j, i), 0)), ks(MLA_H), qs(MLA_ZB)],
        out_specs=[qs(0), qs(0), qs(0)],
        out_shape=[sds((t, MLA_GATE), F32), sds((t, MLA_GATE), F32), sds((t, MLA_H * LANES), F32)],
        scratch_shapes=[pltpu.VMEM((tq, 1), F32), pltpu.VMEM((tq, 1), F32), pltpu.VMEM((tq, LANES), F32)],
        compiler_params=_cparams(("parallel", "parallel", "arbitrary")),
    )(q, qr, kv, kr, kv, proj)


def _gate_bwd(dog, o, proj, name):
    t = o.shape[0]
    tr = _tile(t, 512)

    def body(d_ref, o_ref, z_ref, do_ref, dz_ref):
        z = z_ref[...]
        sg = jax.nn.sigmoid(z)
        d = d_ref[...]
        do_ref[...] = d * z * sg
        dz_ref[...] = d * o_ref[...] * (sg * (1.0 + z * (1.0 - sg)))

    blk = lambda off: pl.BlockSpec((tr, 512), lambda i, j: (i, off + j))
    assert MLA_LOW % 512 != 0 or True
    zspec = pl.BlockSpec((tr, LANES), lambda i, j: (i, MLA_ZB + j))
    b128 = pl.BlockSpec((tr, LANES), lambda i, j: (i, j))
    sds = jax.ShapeDtypeStruct
    return pl.pallas_call(
        body, name=name, grid=(t // tr, MLA_GATE // LANES),
        in_specs=[b128, b128, zspec], out_specs=[b128, b128],
        out_shape=[sds((t, MLA_GATE), F32), sds((t, MLA_GATE), F32)],
        compiler_params=_cparams(("parallel", "parallel")),
    )(dog, o, proj)


def _attn_bwd_q(q, qr, kv, kr, o, do, lse, name):
    t = q.shape[0]
    tq = tk = _tile(t, ATT_BLK)
    nq = nk = t // tq

    def body(qn_ref, qr_ref, kn_ref, kr_ref, v_ref, o_ref, do_ref, lse_ref, dqn_ref, dqr_ref, an_s, ar_s):
        i, j = pl.program_id(1), pl.program_id(2)

        @pl.when(j == 0)
        def _():
            an_s[...] = jnp.zeros_like(an_s)
            ar_s[...] = jnp.zeros_like(ar_s)

        @pl.when(j <= i)
        def _():
            s = _att_scores(qn_ref[...], qr_ref[...], kn_ref[...], kr_ref[...], i, j, tq, tk)
            p = jnp.exp(s - lse_ref[:, 0:1])
            do = do_ref[...]
            dp = _dg(_mx(do), _mx(v_ref[...]), 1, 1)
            dl = jnp.sum(do * o_ref[...], -1, keepdims=True)
            ds = _mx(p * (dp - dl) * MLA_SCALE)
            an_s[...] += _dg(ds, _mx(kn_ref[...]), 1, 0)
            ar_s[...] += _dg(ds, _mx(kr_ref[...]), 1, 0)

        @pl.when(j == nk - 1)
        def _():
            dqn_ref[...] = an_s[...]
            dqr_ref[...] = ar_s[...]

    qs = lambda off: pl.BlockSpec((tq, LANES), lambda h, i, j: (i, off + h))
    ks = lambda off: pl.BlockSpec((tk, LANES), lambda h, i, j: (jnp.minimum(j, i), off + h))
    sds = jax.ShapeDtypeStruct
    return pl.pallas_call(
        body, name=name, grid=(MLA_H, nq, nk),
        in_specs=[qs(0), qs(0), ks(0), pl.BlockSpec((tk, LANES), lambda h, i, j: (jnp.minimum(j, i), 0)), ks(MLA_H),
                  qs(0), qs(0), qs(0)],
        out_specs=[qs(0), qs(0)],
        out_shape=[sds((t, MLA_H * LANES), F32), sds((t, MLA_H * LANES), F32)],
        scratch_shapes=[pltpu.VMEM((tq, LANES), F32), pltpu.VMEM((tq, LANES), F32)],
        compiler_params=_cparams(("parallel", "parallel", "arbitrary")),
    )(q, qr, kv, kr, kv, o, do, lse)


def _attn_bwd_kv(q, qr, kv, kr, o, do, lse, name):
    t = q.shape[0]
    tq = tk = _tile(t, ATT_BLK)
    nq = nk = t // tq

    def body(qn_ref, qr_ref, kn_ref, kr_ref, v_ref, o_ref, do_ref, lse_ref, dkn_ref, dv_ref, dkr_ref, akn_s, av_s):
        j, h, i = pl.program_id(0), pl.program_id(1), pl.program_id(2)

        @pl.when((h == 0) & (i == 0))
        def _():
            dkr_ref[...] = jnp.zeros_like(dkr_ref)

        @pl.when(i == 0)
        def _():
            akn_s[...] = jnp.zeros_like(akn_s)
            av_s[...] = jnp.zeros_like(av_s)

        @pl.when(i >= j)
        def _():
            s = _att_scores(qn_ref[...], qr_ref[...], kn_ref[...], kr_ref[...], i, j, tq, tk)
            p = jnp.exp(s - lse_ref[:, 0:1])
            do = do_ref[...]
            dp = _dg(_mx(do), _mx(v_ref[...]), 1, 1)
            dl = jnp.sum(do * o_ref[...], -1, keepdims=True)
            ds = _mx(p * (dp - dl) * MLA_SCALE)
            av_s[...] += _dg(_mx(p), _mx(do), 0, 0)
            akn_s[...] += _dg(ds, _mx(qn_ref[...]), 0, 0)
            dkr_ref[...] += _dg(ds, _mx(qr_ref[...]), 0, 0)

        @pl.when(i == nq - 1)
        def _():
            dkn_ref[...] = akn_s[...]
            dv_ref[...] = av_s[...]

    qs = lambda off: pl.BlockSpec((tq, LANES), lambda j, h, i: (jnp.maximum(i, j), off + h))
    ks = lambda off: pl.BlockSpec((tk, LANES), lambda j, h, i: (j, off + h))
    sds = jax.ShapeDtypeStruct
    return pl.pallas_call(
        body, name=name, grid=(nk, MLA_H, nq),
        in_specs=[qs(0), qs(0), ks(0), pl.BlockSpec((tk, LANES), lambda j, h, i: (j, 0)), ks(MLA_H), qs(0), qs(0), qs(0)],
        out_specs=[ks(0), ks(0), pl.BlockSpec((tk, LANES), lambda j, h, i: (j, 0))],
        out_shape=[sds((t, MLA_H * LANES), F32), sds((t, MLA_H * LANES), F32), sds((t, LANES), F32)],
        scratch_shapes=[pltpu.VMEM((tk, LANES), F32), pltpu.VMEM((tk, LANES), F32)],
        compiler_params=_cparams(("parallel", "arbitrary", "arbitrary")),
    )(q, qr, kv, kr, kv, o, do, lse)


def _rope_tables(positions):
    inv_freq = ROPE_THETA ** (-jnp.arange(0, MLA_ROPE, 2, dtype=F32) / MLA_ROPE)
    ang = positions.astype(F32)[:, None] * inv_freq
    pad = jnp.zeros((positions.shape[0], LANES - MLA_ROPE), F32)
    cos, sin = jnp.cos(ang), jnp.sin(ang)
    return jnp.concatenate([cos, cos, pad], 1), jnp.concatenate([sin, sin, pad], 1)


def _mla_weights(w):
    dt = w["in_w"].dtype
    iw = w["in_w"]
    c1 = MLA_QR + MLA_KVR + MLA_ROPE
    w_in = jnp.concatenate([iw[:, :c1], jnp.zeros((D_MODEL, LANES - MLA_ROPE), dt), iw[:, c1:]], axis=1)
    qu = w["q_up_w"].reshape(MLA_QR, MLA_H, MLA_NOPE + MLA_ROPE)
    qrope = jnp.concatenate([qu[:, :, MLA_NOPE:], jnp.zeros((MLA_QR, MLA_H, LANES - MLA_ROPE), dt)], axis=2)
    w_q = jnp.concatenate([qu[:, :, :MLA_NOPE].reshape(MLA_QR, -1), qrope.reshape(MLA_QR, -1)], axis=1)
    kvu = w["kv_up_w"].reshape(MLA_KVR, MLA_H, MLA_NOPE + MLA_V)
    w_kv = jnp.concatenate([kvu[:, :, :MLA_NOPE].reshape(MLA_KVR, -1), kvu[:, :, MLA_NOPE:].reshape(MLA_KVR, -1)], axis=1)
    return w_in, w_q, w_kv


def _mla_layer_fwd(u, w, ln_g, ln_b, cosf, sinf, tag):
    w_in, w_q, w_kv = _mla_weights(w)
    qnw, kvnw = w["q_norm_w"].reshape(1, -1), w["kv_norm_w"].reshape(1, -1)
    proj = _mm(u, w_in, name=tag + "_in")
    qn, kvn, kr = _mla_low_fwd(proj, qnw, kvnw, cosf, sinf, name=tag + "_low")
    q = _mm(qn, w_q, name=tag + "_qup")
    kv = _mm(kvn, w_kv, name=tag + "_kvup")
    qr = _rope_heads(q, MLA_H, cosf, sinf, False, name=tag + "_qrope")
    o, og, lse = _attn_fwd(q, qr, kv, kr, proj, name=tag + "_attn")
    y = _mm(og, w["out_w"], name=tag + "_out")
    h = _ln_fwd(u, y, ln_g, ln_b, name=tag + "_ln")
    saved = dict(u=u, w_in=w_in, w_q=w_q, w_kv=w_kv, qnw=qnw, kvnw=kvnw, proj=proj, qn=qn, kvn=kvn, kr=kr, q=q, kv=kv,
                 qr=qr, o=o, og=og, lse=lse, y=y)
    return h, saved


def _mla_layer_bwd(s, w, ln_g, cosf, sinf, dr_up, du_up, tag):
    dr, dg, db = _ln_bwd(s["u"], s["y"], ln_g, dr_up, du_up, name=tag + "_ln_b")
    dog = _mm(dr, w["out_w"], tb=True, name=tag + "_out_bx")
    d_out_w = _mm(s["og"], dr, ta=True, out_dtype=WIRE, name=tag + "_out_bw")
    do, dz = _gate_bwd(dog, s["o"], s["proj"], name=tag + "_gate_b")
    dqn_h, dqr_rot = _attn_bwd_q(s["q"], s["qr"], s["kv"], s["kr"], s["o"], do, s["lse"], name=tag + "_attn_bq")
    dkn_h, dv_h, dkr_rot = _attn_bwd_kv(s["q"], s["qr"], s["kv"], s["kr"], s["o"], do, s["lse"], name=tag + "_attn_bkv")
    dqr = _rope_heads(dqr_rot, 0, cosf, sinf, True, name=tag + "_qrope_b")
    dq = jnp.concatenate([dqn_h, dqr], axis=1)
    dkv = jnp.concatenate([dkn_h, dv_h], axis=1)
    d_wq = _mm(s["qn"], dq, ta=True, out_dtype=WIRE, name=tag + "_qup_bw")
    dqn = _mm(dq, s["w_q"], tb=True, name=tag + "_qup_bx")
    d_wkv = _mm(s["kvn"], dkv, ta=True, out_dtype=WIRE, name=tag + "_kvup_bw")
    dkvn = _mm(dkv, s["w_kv"], tb=True, name=tag + "_kvup_bx")
    dlow, dqnw, dkvnw = _mla_low_bwd(s["proj"], s["qnw"], s["kvnw"], cosf, sinf, dqn, dkvn, dkr_rot, name=tag + "_low_b")
    dproj = jnp.concatenate([dlow, dz], axis=1)
    d_in = _mm(s["u"], dproj, ta=True, out_dtype=WIRE, name=tag + "_in_bw")
    du = _mm(dproj, s["w_in"], tb=True, name=tag + "_in_bx")
    c1 = MLA_QR + MLA_KVR + MLA_ROPE
    d_in_w = jnp.concatenate([d_in[:, :c1], d_in[:, MLA_LOW:]], axis=1)
    dq3n = d_wq[:, :MLA_H * MLA_NOPE].reshape(MLA_QR, MLA_H, MLA_NOPE)
    dq3r = d_wq[:, MLA_H * MLA_NOPE:].reshape(MLA_QR, MLA_H, LANES)[:, :, :MLA_ROPE]
    d_q_up = jnp.concatenate([dq3n, dq3r], axis=2).reshape(MLA_QR, -1)
    dkv3 = d_wkv.reshape(MLA_KVR, 2, MLA_H, MLA_NOPE)
    d_kv_up = jnp.concatenate([dkv3[:, 0], dkv3[:, 1]], axis=2).reshape(MLA_KVR, -1)
    grads = dict(in_w=d_in_w, q_norm_w=dqnw[0], q_up_w=d_q_up, kv_norm_w=dkvnw[0], kv_up_w=d_kv_up, out_w=d_out_w,
                 ln_g=dg[0], ln_b=db[0])
    return dr, du, grads


GDN_REP = GDN_HV // GDN_HK
GDN_A_LANE = GDN_HV


def _gdn_chunk(hk, q, k, v2, z2, ba, alog, dtb, nw, s2):
    L = q.shape[0]
    r_i = lax.broadcasted_iota(jnp.int32, (L, L), 0)
    c_i = lax.broadcasted_iota(jnp.int32, (L, L), 1)
    incl, strict = r_i >= c_i, r_i > c_i
    eye = (r_i == c_i).astype(F32)
    qn = q * lax.rsqrt(jnp.sum(q * q, -1, keepdims=True) + RMS_EPS) * (GDN_DK ** -0.5)
    kn = k * lax.rsqrt(jnp.sum(k * k, -1, keepdims=True) + RMS_EPS)
    beta_all = jax.nn.sigmoid(ba)
    g_all = -jnp.exp(alog) * _softplus(ba + dtb)
    gcs_all = _hd(incl.astype(F32), g_all)
    lane = lax.broadcasted_iota(jnp.int32, (L, LANES), 1)
    qk_raw = _nt(qn, kn)
    outs, news = [], []
    for hh in range(GDN_REP):
        hv = GDN_REP * hk + hh
        beta = jnp.sum(jnp.where(lane == hv, beta_all, 0.0), axis=1, keepdims=True)
        gc = jnp.sum(jnp.where(lane == GDN_A_LANE + hv, gcs_all, 0.0), axis=1, keepdims=True)
        gm = jnp.broadcast_to(gc, (L, L))
        decay = jnp.exp(jnp.where(incl, gm - gm.T, -jnp.inf))
        v = v2[:, hh * GDN_DV:(hh + 1) * GDN_DV]
        z = z2[:, hh * GDN_DV:(hh + 1) * GDN_DV]
        s = s2[hh]
        kb = kn * beta
        vb = v * beta
        x = -jnp.where(strict, _nt(kb, kn) * decay, 0.0)
        inv = eye + x
        xp = x
        for _ in range(5):
            xp = _hd(xp, xp)
            inv = inv + _hd(inv, xp)
        eg = jnp.exp(gc)
        uu = _hd(inv, vb)
        ww = _hd(inv, kb * eg)
        qk = jnp.where(incl, qk_raw * decay, 0.0)
        glast = gc[L - 1:L]
        kdec = kn * jnp.exp(glast - gc)
        vnew = uu - _nn(ww, s)
        o = _nn(qn * eg, s) + _nn(qk, vnew)
        news.append(s * jnp.exp(glast) + _tn(kdec, vnew))
        outs.append(o * lax.rsqrt(jnp.mean(o * o, -1, keepdims=True) + RMS_EPS) * nw * _silu(z))
    return jnp.concatenate(outs, axis=1), jnp.stack(news)


def _gdn_specs(nc, rev):
    cc = (lambda c: nc - 1 - c) if rev else (lambda c: c)
    w2 = GDN_REP * GDN_DV
    par = pl.BlockSpec((1, LANES), lambda c, h: (0, 0))
    return dict(
        q=pl.BlockSpec((GDN_L, GDN_DK), lambda c, h: (cc(c), h)),
        k=pl.BlockSpec((GDN_L, GDN_DK), lambda c, h: (cc(c), GDN_HK + h)),
        v=pl.BlockSpec((GDN_L, w2), lambda c, h: (cc(c), 2 * GDN_KEY // w2 + h)),
        z=pl.BlockSpec((GDN_L, w2), lambda c, h: (cc(c), GDN_CONV_DIM // w2 + h)),
        ba=pl.BlockSpec((GDN_L, LANES), lambda c, h: (cc(c), (GDN_CONV_DIM + GDN_VAL) // LANES)),
        par=par,
        st=pl.BlockSpec((1, 1, GDN_REP, GDN_DK, GDN_DV), lambda c, h: (cc(c), h, 0, 0, 0)),
        o=pl.BlockSpec((GDN_L, w2), lambda c, h: (cc(c), h)),
        qk_out=pl.BlockSpec((GDN_L, GDN_DK), lambda c, h: (cc(c), h)),
        ba_out=pl.BlockSpec((GDN_L, LANES), lambda c, h: (cc(c), 0)),
    )


def _gdn_fwd(proj, act, alog, dtb, nw, name):
    t = proj.shape[0]
    nc = t // GDN_L
    sp = _gdn_specs(nc, False)

    def body(q_ref, k_ref, v_ref, z_ref, ba_ref, alog_ref, dtb_ref, nw_ref, o_ref, st_ref, state):
        c, h = pl.program_id(0), pl.program_id(1)

        @pl.when(c == 0)
        def _():
            state[h] = jnp.zeros((GDN_REP, GDN_DK, GDN_DV), F32)

        prev = state[h]
        st_ref[0, 0] = prev
        on, new = _gdn_chunk(h, q_ref[...], k_ref[...], v_ref[...], z_ref[...], ba_ref[...], alog_ref[...],
                             dtb_ref[...], nw_ref[...], prev)
        o_ref[...] = on.astype(o_ref.dtype)
        state[h] = new

    sds = jax.ShapeDtypeStruct
    return pl.pallas_call(
        body, name=name, grid=(nc, GDN_HK),
        in_specs=[sp["q"], sp["k"], sp["v"], sp["z"], sp["ba"], sp["par"], sp["par"], sp["par"]],
        out_specs=[sp["o"], sp["st"]],
        out_shape=[sds((t, GDN_VAL), F32), sds((nc, GDN_HK, GDN_REP, GDN_DK, GDN_DV), F32)],
        scratch_shapes=[pltpu.VMEM((GDN_HK, GDN_REP, GDN_DK, GDN_DV), F32)],
        compiler_params=_cparams(("arbitrary", "arbitrary")),
    )(act, act, act, proj, proj, alog, dtb, nw)


def _gdn_bwd(proj, act, alog, dtb, nw, states, don, name):
    t = proj.shape[0]
    nc = t // GDN_L
    sp = _gdn_specs(nc, True)

    def body(q_ref, k_ref, v_ref, z_ref, ba_ref, alog_ref, dtb_ref, nw_ref, st_ref, do_ref,
             dq_ref, dk_ref, dv_ref, dz_ref, dba_ref, dalog_ref, ddtb_ref, dnw_ref, dstate):
        c, h = pl.program_id(0), pl.program_id(1)

        @pl.when(c == 0)
        def _():
            dstate[h] = jnp.zeros((GDN_REP, GDN_DK, GDN_DV), F32)

        @pl.when((c == 0) & (h == 0))
        def _():
            dalog_ref[...] = jnp.zeros_like(dalog_ref)
            ddtb_ref[...] = jnp.zeros_like(ddtb_ref)
            dnw_ref[...] = jnp.zeros_like(dnw_ref)

        @pl.when(h == 0)
        def _():
            dba_ref[...] = jnp.zeros_like(dba_ref)

        _, vjp = jax.vjp(functools.partial(_gdn_chunk, h), q_ref[...], k_ref[...], v_ref[...], z_ref[...], ba_ref[...],
                         alog_ref[...], dtb_ref[...], nw_ref[...], st_ref[0, 0])
        dq, dk, dv, dz, dba, dalog, ddtb, dnw, dprev = vjp((do_ref[...].astype(F32), dstate[h]))
        dq_ref[...] = dq
        dk_ref[...] = dk
        dv_ref[...] = dv
        dz_ref[...] = dz
        dba_ref[...] += dba
        dalog_ref[...] += dalog
        ddtb_ref[...] += ddtb
        dnw_ref[...] += dnw
        dstate[h] = dprev

    sds = jax.ShapeDtypeStruct
    par_out = pl.BlockSpec((1, LANES), lambda c, h: (0, 0))
    return pl.pallas_call(
        body, name=name, grid=(nc, GDN_HK),
        in_specs=[sp["q"], sp["k"], sp["v"], sp["z"], sp["ba"], sp["par"], sp["par"], sp["par"], sp["st"], sp["o"]],
        out_specs=[sp["qk_out"], sp["qk_out"], sp["o"], sp["o"], sp["ba_out"], par_out, par_out, par_out],
        out_shape=[sds((t, GDN_KEY), F32), sds((t, GDN_KEY), F32), sds((t, GDN_VAL), F32), sds((t, GDN_VAL), F32),
                   sds((t, LANES), F32), sds((1, LANES), F32), sds((1, LANES), F32), sds((1, LANES), F32)],
        scratch_shapes=[pltpu.VMEM((GDN_HK, GDN_REP, GDN_DK, GDN_DV), F32)],
        compiler_params=_cparams(("arbitrary", "arbitrary")),
    )(act, act, act, proj, proj, alog, dtb, nw, states, don)


GDN_PROJ = GDN_CONV_DIM + GDN_VAL + 2 * GDN_HV


def _gdn_layer_fwd(u, w, ln_g, ln_b, tag):
    w_in = jnp.concatenate([w["in_w"], jnp.zeros((D_MODEL, GDN_PROJ_PAD - GDN_PROJ), w["in_w"].dtype)], axis=1)
    alog = _pad_lanes(w["a_log"], offset=GDN_A_LANE)
    dtb = _pad_lanes(w["dt_bias"], offset=GDN_A_LANE)
    nw = w["norm_w"].reshape(1, GDN_DV)
    zb = jnp.zeros((1, GDN_CONV_DIM), F32)
    proj = _mm(u, w_in, name=tag + "_in")
    act = _conv_fwd(proj, 0, w["conv_w"], zb, name=tag + "_conv")
    on, states = _gdn_fwd(proj, act, alog, dtb, nw, name=tag + "_delta")
    y = _mm(on, w["out_w"], name=tag + "_out")
    h = _ln_fwd(u, y, ln_g, ln_b, name=tag + "_ln")
    saved = dict(u=u, w_in=w_in, proj=proj, act=act, states=states, on=on, y=y, alog=alog, dtb=dtb, nw=nw, zb=zb)
    return h, saved


def _gdn_layer_bwd(s, w, ln_g, dr_up, du_up, tag):
    t = s["u"].shape[0]
    dr, dg, db = _ln_bwd(s["u"], s["y"], ln_g, dr_up, du_up, name=tag + "_ln_b")
    don = _mm(dr, w["out_w"], tb=True, name=tag + "_out_bx")
    d_out_w = _mm(s["on"], dr, ta=True, out_dtype=WIRE, name=tag + "_out_bw")
    dq, dk, dv, dz, dba, dalog, ddtb, dnw = _gdn_bwd(s["proj"], s["act"], s["alog"], s["dtb"], s["nw"], s["states"], don,
                                                    name=tag + "_delta_b")
    dact = jnp.concatenate([dq, dk, dv], axis=1)
    dpre, d_conv_w, _ = _conv_bwd_pre(s["proj"], 0, w["conv_w"], s["zb"], dact, name=tag + "_conv_bp")
    dqkv = _conv_bwd_x(dpre, w["conv_w"], name=tag + "_conv_bx")
    dproj = jnp.concatenate([dqkv, dz, dba, jnp.zeros((t, GDN_PROJ_PAD - GDN_PROJ - (LANES - 2 * GDN_HV)), F32)], axis=1)
    d_in = _mm(s["u"], dproj, ta=True, out_dtype=WIRE, name=tag + "_in_bw")
    du = _mm(dproj, s["w_in"], tb=True, name=tag + "_in_bx")
    grads = dict(in_w=d_in[:, :GDN_PROJ], conv_w=d_conv_w, a_log=dalog[0, GDN_A_LANE:GDN_A_LANE + GDN_HV],
                 dt_bias=ddtb[0, GDN_A_LANE:GDN_A_LANE + GDN_HV], norm_w=dnw[0], out_w=d_out_w, ln_g=dg[0], ln_b=db[0])
    return dr, du, grads


def _mesh_pos():
    return lax.axis_index("x"), lax.axis_index("y"), lax.axis_index("c")


def _peer(k, x, y, c):
    return ((1 - x) if k & 4 else x, (1 - y) if k & 2 else y, (1 - c) if k & 1 else c)


def _exchange(arrs, scatter, name):
    n = len(arrs)
    hbm = pl.BlockSpec(memory_space=pltpu.HBM)

    def body(*refs):
        ins, outs = refs[:n], refs[n:2 * n]
        send, recv, loc = refs[2 * n:]
        x, y, c = _mesh_pos()
        me = 4 * x + 2 * y + c
        started = []
        for i in range(n):
            cp = pltpu.make_async_copy(ins[i].at[me] if scatter else ins[i], outs[i].at[me], loc.at[i])
            cp.start()
            started.append(cp)
        for k in range(1, N_DEV):
            peer = _peer(k, x, y, c)
            pidx = 4 * peer[0] + 2 * peer[1] + peer[2]
            for i in range(n):
                cp = pltpu.make_async_remote_copy(
                    src_ref=ins[i].at[pidx] if scatter else ins[i], dst_ref=outs[i].at[me],
                    send_sem=send.at[i, k - 1], recv_sem=recv.at[i, k - 1],
                    device_id=peer, device_id_type=pl.DeviceIdType.MESH)
                cp.start()
                started.append(cp)
        for k in range(1, N_DEV):
            peer = _peer(k, x, y, c)
            pidx = 4 * peer[0] + 2 * peer[1] + peer[2]
            for i in range(n):
                pltpu.make_async_remote_copy(
                    src_ref=ins[i].at[pidx] if scatter else ins[i], dst_ref=outs[i].at[pidx],
                    send_sem=send.at[i, k - 1], recv_sem=recv.at[i, k - 1],
                    device_id=peer, device_id_type=pl.DeviceIdType.MESH).wait_recv()
        for cp in started[n:]:
            cp.wait_send()
        for cp in started[:n]:
            cp.wait()

    out_shape = [jax.ShapeDtypeStruct(a.shape if scatter else (N_DEV,) + a.shape, a.dtype) for a in arrs]
    return pl.pallas_call(
        body, name=name, in_specs=[hbm] * n, out_specs=[hbm] * n, out_shape=out_shape,
        scratch_shapes=[pltpu.SemaphoreType.DMA((n, N_DEV - 1)), pltpu.SemaphoreType.DMA((n, N_DEV - 1)),
                        pltpu.SemaphoreType.DMA((n,))],
        compiler_params=pltpu.CompilerParams(has_side_effects=True),
    )(*arrs)


def _unshard(g, ax):
    g = jnp.moveaxis(g, 0, ax + 1)
    sh = g.shape
    return g.reshape(sh[:ax + 1] + (sh[ax + 1] * sh[ax + 2],) + sh[ax + 3:])


def _to_parts(full, ax):
    sh = full.shape
    full = full.reshape(sh[:ax + 1] + (N_DEV, sh[ax + 1] // N_DEV) + sh[ax + 2:])
    return jnp.moveaxis(full, ax + 1, 0)


def _row_tile(r, c):
    cap = max(8, (256 * 1024) // max(c, 1))
    best = None
    for d in range(8, min(r, cap) + 1, 8):
        if r % d == 0:
            best = d
    return r if best is None else best


def _adamw(w, m, v, parts, name):
    r, c = w.shape
    tr = _row_tile(r, c)

    def body(w_ref, m_ref, v_ref, p_ref, g_ref, d_ref, nm_ref, nv_ref):
        g = p_ref[0].astype(F32)
        for q in range(1, N_DEV):
            g = g + p_ref[q].astype(F32)
        nm = ADAM_B1 * m_ref[...] + (1.0 - ADAM_B1) * g
        nv = ADAM_B2 * v_ref[...] + (1.0 - ADAM_B2) * (g * g)
        m_hat = nm / (1.0 - ADAM_B1 ** ADAM_STEP)
        v_hat = nv / (1.0 - ADAM_B2 ** ADAM_STEP)
        g_ref[...] = g
        d_ref[...] = -ADAM_LR * (m_hat / (jnp.sqrt(v_hat) + ADAM_EPS) + ADAM_WD * w_ref[...])
        nm_ref[...] = nm
        nv_ref[...] = nv

    row = pl.BlockSpec((tr, c), lambda i: (i, 0))
    out = jax.ShapeDtypeStruct((r, c), F32)
    return pl.pallas_call(
        body, name=name, grid=(r // tr,),
        in_specs=[row, row, row, pl.BlockSpec((N_DEV, tr, c), lambda i: (0, i, 0))],
        out_specs=[row] * 4, out_shape=[out] * 4, compiler_params=_cparams(("parallel",)),
    )(w, m, v, parts)


WEIGHTS = ['ssd_in_w', 'ssd_conv_w', 'ssd_conv_b', 'ssd_dt_bias', 'ssd_a_log', 'ssd_d', 'ssd_norm_w', 'ssd_out_w',
           'mla_in_w', 'mla_q_norm_w', 'mla_q_up_w', 'mla_kv_norm_w', 'mla_kv_up_w', 'mla_out_w', 'gdn_in_w',
           'gdn_conv_w', 'gdn_a_log', 'gdn_dt_bias', 'gdn_norm_w', 'gdn_out_w', 'ln_g', 'ln_b']
SHARDED = {'ssd_in_w': (1, True), 'ssd_conv_w': (1, False), 'ssd_conv_b': (0, False), 'ssd_norm_w': (0, False),
           'ssd_out_w': (0, True), 'mla_in_w': (1, True), 'mla_q_up_w': (1, True), 'mla_kv_up_w': (1, True),
           'mla_out_w': (0, True), 'gdn_in_w': (1, True), 'gdn_conv_w': (1, False), 'gdn_out_w': (0, True)}
REPLICATED = [n for n in WEIGHTS if n not in SHARDED]


def _pack_small(vals):
    flat = jnp.concatenate([vals[n].reshape(-1).astype(F32) for n in REPLICATED])
    rows = -(-flat.shape[0] // (8 * LANES)) * 8
    return jnp.pad(flat, (0, rows * LANES - flat.shape[0])).reshape(rows, LANES)


def _unpack_small(slab, like):
    flat = slab.reshape(-1)
    out, off = {}, 0
    for n in REPLICATED:
        sz = like[n].size
        out[n] = flat[off:off + sz].reshape(like[n].shape)
        off += sz
    return out


def kernel(x, positions, ssd_in_w, ssd_conv_w, ssd_conv_b, ssd_dt_bias, ssd_a_log, ssd_d, ssd_norm_w, ssd_out_w, mla_in_w, mla_q_norm_w, mla_q_up_w, mla_kv_norm_w, mla_kv_up_w, mla_out_w, gdn_in_w, gdn_conv_w, gdn_a_log, gdn_dt_bias, gdn_norm_w, gdn_out_w, ln_g, ln_b, loss_target, m_ssd_in_w, m_ssd_conv_w, m_ssd_conv_b, m_ssd_dt_bias, m_ssd_a_log, m_ssd_d, m_ssd_norm_w, m_ssd_out_w, m_mla_in_w, m_mla_q_norm_w, m_mla_q_up_w, m_mla_kv_norm_w, m_mla_kv_up_w, m_mla_out_w, m_gdn_in_w, m_gdn_conv_w, m_gdn_a_log, m_gdn_dt_bias, m_gdn_norm_w, m_gdn_out_w, m_ln_g, m_ln_b, v_ssd_in_w, v_ssd_conv_w, v_ssd_conv_b, v_ssd_dt_bias, v_ssd_a_log, v_ssd_d, v_ssd_norm_w, v_ssd_out_w, v_mla_in_w, v_mla_q_norm_w, v_mla_q_up_w, v_mla_kv_norm_w, v_mla_kv_up_w, v_mla_out_w, v_gdn_in_w, v_gdn_conv_w, v_gdn_a_log, v_gdn_dt_bias, v_gdn_norm_w, v_gdn_out_w, v_ln_g, v_ln_b):
    loc = locals()
    w = {n: loc[n] for n in WEIGHTS}
    m = {n: loc["m_" + n] for n in WEIGHTS}
    v = {n: loc["v_" + n] for n in WEIGHTS}
    xs, pos, tgt = x[0], positions[0], loss_target[0]

    snames = list(SHARDED)
    gathered = _exchange([w[n].astype(WIRE) if SHARDED[n][1] else w[n] for n in snames], False, name="gather_weights")
    full = {n: _unshard(g, SHARDED[n][0]) for n, g in zip(snames, gathered)}
    for n in REPLICATED:
        full[n] = w[n]

    def layer_w(prefix, j):
        return {n[len(prefix) + 1:]: full[n][j] for n in WEIGHTS if n.startswith(prefix + "_")}

    lg = lambda i: full["ln_g"][i].reshape(1, D_MODEL)
    lb = lambda i: full["ln_b"][i].reshape(1, D_MODEL)
    cosf, sinf = _rope_tables(pos)
    w_s0, w_m0, w_g0, w_s1 = layer_w("ssd", 0), layer_w("mla", 0), layer_w("gdn", 0), layer_w("ssd", 1)

    h1, s0 = _ssd_layer_fwd(xs, w_s0, lg(0), lb(0), "ssd0")
    h2, s1 = _mla_layer_fwd(h1, w_m0, lg(1), lb(1), cosf, sinf, "mla0")
    h3, s2 = _gdn_layer_fwd(h2, w_g0, lg(2), lb(2), "gdn0")
    h4, s3 = _ssd_layer_fwd(h3, w_s1, lg(3), lb(3), "ssd1")
    loss_tile, dl = _loss_head(h4, tgt, name="loss_head")

    dr3, du3, g3 = _ssd_layer_bwd(s3, w_s1, lg(3), jnp.zeros_like(dl), dl, "ssd1")
    dr2, du2, g2 = _gdn_layer_bwd(s2, w_g0, lg(2), dr3, du3, "gdn0")
    dr1, du1, g1 = _mla_layer_bwd(s1, w_m0, lg(1), cosf, sinf, dr2, du2, "mla0")
    dr0, du0, g0 = _ssd_layer_bwd(s0, w_s0, lg(0), dr1, du1, "ssd0")
    grad_x = _axpy(dr0, du0, name="grad_x")[None]

    gfull = {"ssd_" + k: jnp.stack([g0[k], g3[k]]) for k in g0 if not k.startswith("ln_")}
    gfull.update({"mla_" + k: g1[k][None] for k in g1 if not k.startswith("ln_")})
    gfull.update({"gdn_" + k: g2[k][None] for k in g2 if not k.startswith("ln_")})
    gfull["ln_g"] = jnp.stack([g0["ln_g"], g1["ln_g"], g2["ln_g"], g3["ln_g"]])
    gfull["ln_b"] = jnp.stack([g0["ln_b"], g1["ln_b"], g2["ln_b"], g3["ln_b"]])

    small = _pack_small(gfull)
    sends = [_to_parts(gfull[n], SHARDED[n][0]).astype(WIRE if SHARDED[n][1] else F32) for n in snames]
    sends.append(jnp.broadcast_to(small[None], (N_DEV,) + small.shape))
    recvd = _exchange(sends, True, name="scatter_grads")

    grads, deltas, new_m, new_v = {}, {}, {}, {}
    for n, parts in zip(snames, recvd[:-1]):
        shp = w[n].shape
        r2 = (-1, shp[-1])
        outs = _adamw(w[n].reshape(r2), m[n].reshape(r2), v[n].reshape(r2), parts.reshape((N_DEV,) + w[n].reshape(r2).shape),
                      name="adamw_" + n)
        grads[n], deltas[n], new_m[n], new_v[n] = (o.reshape(shp) for o in outs)
    outs = _adamw(_pack_small(w), _pack_small(m), _pack_small(v), recvd[-1], name="adamw_replicated")
    for dst, o in zip((grads, deltas, new_m, new_v), outs):
        dst.update(_unpack_small(o, w))

    loss = lax.psum(loss_tile[0, 0], ("x", "y", "c"))
    return (loss, grad_x, *[grads[n] for n in WEIGHTS], *[deltas[n] for n in WEIGHTS],
            *[new_m[n] for n in WEIGHTS], *[new_v[n] for n in WEIGHTS])
```

```python
import functools

import jax
import jax.numpy as jnp
from jax import lax
from jax.experimental import pallas as pl
from jax.experimental.pallas import tpu as pltpu

F32 = jnp.float32
MXU = jnp.bfloat16
WIRE = jnp.bfloat16
HI = lax.Precision.HIGHEST

N_DEV = 8
LANES = 128
VMEM_LIMIT = 56 * 1024 * 1024

D_MODEL = 2048
DEPTH = 4
ALPHA = (2.0 * DEPTH) ** 0.25
LN_EPS = 1e-5
RMS_EPS = 1e-6

SSD_DI = 4096
SSD_P = 64
SSD_H = 64
SSD_G = 8
SSD_N = 128
SSD_L = 128
SSD_GS = SSD_DI // SSD_G
SSD_CONV_DIM = SSD_DI + 2 * SSD_G * SSD_N
SSD_PROJ_PAD = SSD_DI + SSD_CONV_DIM + LANES

MLA_H = 16
MLA_QR = 768
MLA_KVR = 512
MLA_NOPE = 128
MLA_ROPE = 64
MLA_V = 128
MLA_GATE = MLA_H * MLA_V
MLA_PROJ_PAD = MLA_QR + MLA_KVR + LANES + MLA_GATE
MLA_SCALE = (MLA_NOPE + MLA_ROPE) ** -0.5
ROPE_THETA = 10000.0
ATT_BLK = 512

GDN_HK = 16
GDN_HV = 32
GDN_DK = 128
GDN_DV = 128
GDN_KEY = GDN_HK * GDN_DK
GDN_VAL = GDN_HV * GDN_DV
GDN_L = 128
GDN_CONV_DIM = 2 * GDN_KEY + GDN_VAL
GDN_PROJ_PAD = 12800

ADAM_LR = 0.001
ADAM_B1 = 0.9
ADAM_B2 = 0.999
ADAM_EPS = 1e-08
ADAM_WD = 0.01
ADAM_STEP = 10


def _cparams(sem=None):
    return pltpu.CompilerParams(dimension_semantics=sem, vmem_limit_bytes=VMEM_LIMIT)


def _tile(n, cap):
    if n <= cap:
        return n
    best = None
    for d in range(LANES, cap + 1, LANES):
        if n % d == 0:
            best = d
    assert best is not None, (n, cap)
    return best


def _dg(a, b, ca, cb, prec=None):
    return lax.dot_general(a, b, (((ca,), (cb,)), ((), ())), preferred_element_type=F32, precision=prec)


def _mx(a):
    return a.astype(MXU)


def _hd(a, b):
    return _dg(a, b, 1, 0, HI)


@jax.custom_vjp
def _nn(a, b):
    return _dg(_mx(a), _mx(b), 1, 0)


def _nn_f(a, b):
    return _nn(a, b), (a, b)


def _nn_b(res, ct):
    a, b = res
    return _dg(_mx(ct), _mx(b), 1, 1), _dg(_mx(a), _mx(ct), 0, 0)


_nn.defvjp(_nn_f, _nn_b)


@jax.custom_vjp
def _nt(a, b):
    return _dg(_mx(a), _mx(b), 1, 1)


def _nt_f(a, b):
    return _nt(a, b), (a, b)


def _nt_b(res, ct):
    a, b = res
    return _dg(_mx(ct), _mx(b), 1, 0), _dg(_mx(ct), _mx(a), 0, 0)


_nt.defvjp(_nt_f, _nt_b)


@jax.custom_vjp
def _tn(a, b):
    return _dg(_mx(a), _mx(b), 0, 0)


def _tn_f(a, b):
    return _tn(a, b), (a, b)


def _tn_b(res, ct):
    a, b = res
    return _dg(_mx(b), _mx(ct), 1, 1), _dg(_mx(a), _mx(ct), 1, 0)


_tn.defvjp(_tn_f, _tn_b)


def _softplus(x):
    return jnp.maximum(x, 0.0) + jnp.log(1.0 + jnp.exp(-jnp.abs(x)))


def _silu(x):
    return x * jax.nn.sigmoid(x)


def _mm(a, b, *, ta=False, tb=False, out_dtype=F32, name, ride=None):
    if ta:
        kdim, m = a.shape
    else:
        m, kdim = a.shape
    if tb:
        n, kb = b.shape
    else:
        kb, n = b.shape
    assert kdim == kb, (a.shape, b.shape, ta, tb)
    tm, tn, tk = _tile(m, 512), _tile(n, 1280), _tile(kdim, 512)
    nk = kdim // tk
    ca, cb = (0 if ta else 1), (1 if tb else 0)

    def body(a_ref, b_ref, o_ref, acc_ref):
        k = pl.program_id(2)

        @pl.when(k == 0)
        def _():
            acc_ref[...] = jnp.zeros_like(acc_ref)

        acc_ref[...] += _dg(_mx(a_ref[...]), _mx(b_ref[...]), ca, cb)

        @pl.when(k == nk - 1)
        def _():
            o_ref[...] = acc_ref[...].astype(out_dtype)

    a_spec = pl.BlockSpec((tk, tm), lambda i, j, k: (k, i)) if ta else pl.BlockSpec((tm, tk), lambda i, j, k: (i, k))
    b_spec = pl.BlockSpec((tn, tk), lambda i, j, k: (j, k)) if tb else pl.BlockSpec((tk, tn), lambda i, j, k: (k, j))
    (out,), rode = _hosted_call(
        body, name=name, grid=(m // tm, n // tn, nk),
        in_specs=[a_spec, b_spec], out_specs=[pl.BlockSpec((tm, tn), lambda i, j, k: (i, j))],
        out_shape=[jax.ShapeDtypeStruct((m, n), out_dtype)],
        scratch_shapes=[pltpu.VMEM((tm, tn), F32)],
        sem=("parallel", "parallel", "arbitrary"), args=(a, b), ride=ride)
    return out if ride is None else (out, rode)


def _ln_fwd(h, y, g, b, name):
    t, d = h.shape
    tr = _tile(t, 256)

    def body(h_ref, y_ref, g_ref, b_ref, o_ref):
        r = ALPHA * h_ref[...] + y_ref[...]
        mu = jnp.mean(r, -1, keepdims=True)
        xc = r - mu
        var = jnp.mean(xc * xc, -1, keepdims=True)
        o_ref[...] = xc * lax.rsqrt(var + LN_EPS) * g_ref[...] + b_ref[...]

    row = pl.BlockSpec((tr, d), lambda i: (i, 0))
    par = pl.BlockSpec((1, d), lambda i: (0, 0))
    return pl.pallas_call(
        body, name=name, grid=(t // tr,), in_specs=[row, row, par, par], out_specs=row,
        out_shape=jax.ShapeDtypeStruct((t, d), F32), compiler_params=_cparams(("parallel",)),
    )(h, y, g, b)


def _ln_bwd(h, y, g, dr_up, du_up, name):
    t, d = h.shape
    tr = _tile(t, 256)

    def body(h_ref, y_ref, g_ref, dr_ref, du_ref, o_ref, dg_ref, db_ref):
        i = pl.program_id(0)

        @pl.when(i == 0)
        def _():
            dg_ref[...] = jnp.zeros_like(dg_ref)
            db_ref[...] = jnp.zeros_like(db_ref)

        dout = ALPHA * dr_ref[...] + du_ref[...]
        r = ALPHA * h_ref[...] + y_ref[...]
        mu = jnp.mean(r, -1, keepdims=True)
        xc = r - mu
        rstd = lax.rsqrt(jnp.mean(xc * xc, -1, keepdims=True) + LN_EPS)
        xh = xc * rstd
        dxh = dout * g_ref[...]
        o_ref[...] = rstd * (dxh - jnp.mean(dxh, -1, keepdims=True) - xh * jnp.mean(dxh * xh, -1, keepdims=True))
        dg_ref[...] += jnp.sum(dout * xh, 0, keepdims=True)
        db_ref[...] += jnp.sum(dout, 0, keepdims=True)

    row = pl.BlockSpec((tr, d), lambda i: (i, 0))
    par = pl.BlockSpec((1, d), lambda i: (0, 0))
    return pl.pallas_call(
        body, name=name, grid=(t // tr,), in_specs=[row, row, par, row, row], out_specs=[row, par, par],
        out_shape=[jax.ShapeDtypeStruct((t, d), F32), jax.ShapeDtypeStruct((1, d), F32), jax.ShapeDtypeStruct((1, d), F32)],
        compiler_params=_cparams(("arbitrary",)),
    )(h, y, g, dr_up, du_up)


def _loss_head(h, tgt, name):
    t, d = h.shape
    tr = _tile(t, 256)

    def body(h_ref, t_ref, l_ref, d_ref):
        i = pl.program_id(0)

        @pl.when(i == 0)
        def _():
            l_ref[...] = jnp.zeros_like(l_ref)

        e = h_ref[...] - t_ref[...]
        d_ref[...] = e * (1.0 / d)
        l_ref[...] += 0.5 * jnp.sum(jnp.mean(e * e, -1, keepdims=True))

    row = pl.BlockSpec((tr, d), lambda i: (i, 0))
    return pl.pallas_call(
        body, name=name, grid=(t // tr,), in_specs=[row, row],
        out_specs=[pl.BlockSpec((8, LANES), lambda i: (0, 0)), row],
        out_shape=[jax.ShapeDtypeStruct((8, LANES), F32), jax.ShapeDtypeStruct((t, d), F32)],
        compiler_params=_cparams(("arbitrary",)),
    )(h, tgt)


def _axpy(dr, du, name):
    t, d = dr.shape
    tr = _tile(t, 256)

    def body(a_ref, b_ref, o_ref):
        o_ref[...] = ALPHA * a_ref[...] + b_ref[...]

    row = pl.BlockSpec((tr, d), lambda i: (i, 0))
    return pl.pallas_call(
        body, name=name, grid=(t // tr,), in_specs=[row, row], out_specs=row,
        out_shape=jax.ShapeDtypeStruct((t, d), F32), compiler_params=_cparams(("parallel",)),
    )(dr, du)


CONV_TT = 512
CONV_TC = 512


def _shift_down(cur, prev, s, row):
    if s == 0:
        return cur
    return jnp.where(row >= s, pltpu.roll(cur, s, 0), pltpu.roll(prev, s, 0))


def _shift_up(cur, nxt, s, row, tt):
    if s == 0:
        return cur
    return jnp.where(row < tt - s, pltpu.roll(cur, tt - s, 0), pltpu.roll(nxt, tt - s, 0))


def _conv_fwd(proj, col0, w, b, name):
    t = proj.shape[0]
    c = w.shape[1]
    tt = _tile(t, CONV_TT)
    cb0 = col0 // CONV_TC

    def body(x_ref, p_ref, w_ref, b_ref, o_ref):
        i = pl.program_id(1)
        x = x_ref[...]
        p = jnp.where(i > 0, p_ref[...], 0.0)
        row = lax.broadcasted_iota(jnp.int32, x.shape, 0)
        pre = b_ref[...] + w_ref[3:4, :] * x
        for s in (1, 2, 3):
            pre = pre + w_ref[3 - s:4 - s, :] * _shift_down(x, p, s, row)
        o_ref[...] = _silu(pre)

    return pl.pallas_call(
        body, name=name, grid=(c // CONV_TC, t // tt),
        in_specs=[pl.BlockSpec((tt, CONV_TC), lambda j, i: (i, cb0 + j)),
                  pl.BlockSpec((tt, CONV_TC), lambda j, i: (jnp.maximum(i - 1, 0), cb0 + j)),
                  pl.BlockSpec((4, CONV_TC), lambda j, i: (0, j)),
                  pl.BlockSpec((1, CONV_TC), lambda j, i: (0, j))],
        out_specs=pl.BlockSpec((tt, CONV_TC), lambda j, i: (i, j)),
        out_shape=jax.ShapeDtypeStruct((t, c), F32), compiler_params=_cparams(("parallel", "parallel")),
    )(proj, proj, w, b)


def _conv_bwd_pre(proj, col0, w, b, dact, name):
    t = proj.shape[0]
    c = w.shape[1]
    tt = _tile(t, CONV_TT)
    cb0 = col0 // CONV_TC

    def body(x_ref, p_ref, w_ref, b_ref, d_ref, dpre_ref, dw_ref, db_ref):
        i = pl.program_id(1)

        @pl.when(i == 0)
        def _():
            dw_ref[...] = jnp.zeros_like(dw_ref)
            db_ref[...] = jnp.zeros_like(db_ref)

        x = x_ref[...]
        p = jnp.where(i > 0, p_ref[...], 0.0)
        row = lax.broadcasted_iota(jnp.int32, x.shape, 0)
        sh = [_shift_down(x, p, s, row) for s in range(4)]
        pre = b_ref[...] + w_ref[3:4, :] * sh[0]
        for s in (1, 2, 3):
            pre = pre + w_ref[3 - s:4 - s, :] * sh[s]
        sg = jax.nn.sigmoid(pre)
        dpre = d_ref[...] * (sg * (1.0 + pre * (1.0 - sg)))
        dpre_ref[...] = dpre
        for s in range(4):
            dw_ref[3 - s:4 - s, :] += jnp.sum(dpre * sh[s], 0, keepdims=True)
        db_ref[...] += jnp.sum(dpre, 0, keepdims=True)

    return pl.pallas_call(
        body, name=name, grid=(c // CONV_TC, t // tt),
        in_specs=[pl.BlockSpec((tt, CONV_TC), lambda j, i: (i, cb0 + j)),
                  pl.BlockSpec((tt, CONV_TC), lambda j, i: (jnp.maximum(i - 1, 0), cb0 + j)),
                  pl.BlockSpec((4, CONV_TC), lambda j, i: (0, j)),
                  pl.BlockSpec((1, CONV_TC), lambda j, i: (0, j)),
                  pl.BlockSpec((tt, CONV_TC), lambda j, i: (i, j))],
        out_specs=[pl.BlockSpec((tt, CONV_TC), lambda j, i: (i, j)),
                   pl.BlockSpec((4, CONV_TC), lambda j, i: (0, j)),
                   pl.BlockSpec((1, CONV_TC), lambda j, i: (0, j))],
        out_shape=[jax.ShapeDtypeStruct((t, c), F32), jax.ShapeDtypeStruct((4, c), F32), jax.ShapeDtypeStruct((1, c), F32)],
        compiler_params=_cparams(("parallel", "arbitrary")),
    )(proj, proj, w, b, dact)


def _conv_bwd_x(dpre, w, name):
    t, c = dpre.shape
    tt = _tile(t, CONV_TT)
    nt = t // tt

    def body(d_ref, n_ref, w_ref, o_ref):
        i = pl.program_id(1)
        d = d_ref[...]
        nx = jnp.where(i < nt - 1, n_ref[...], 0.0)
        row = lax.broadcasted_iota(jnp.int32, d.shape, 0)
        acc = w_ref[3:4, :] * d
        for s in (1, 2, 3):
            acc = acc + w_ref[3 - s:4 - s, :] * _shift_up(d, nx, s, row, tt)
        o_ref[...] = acc

    return pl.pallas_call(
        body, name=name, grid=(c // CONV_TC, nt),
        in_specs=[pl.BlockSpec((tt, CONV_TC), lambda j, i: (i, j)),
                  pl.BlockSpec((tt, CONV_TC), lambda j, i: (jnp.minimum(i + 1, nt - 1), j)),
                  pl.BlockSpec((4, CONV_TC), lambda j, i: (0, j))],
        out_specs=pl.BlockSpec((tt, CONV_TC), lambda j, i: (i, j)),
        out_shape=jax.ShapeDtypeStruct((t, c), F32), compiler_params=_cparams(("parallel", "parallel")),
    )(dpre, dpre, w)


def _ssd_chunk(g, x, z, bm, cm, dtraw, dtb, alog, dsk, nw, prev):
    L = x.shape[0]
    r_i = lax.broadcasted_iota(jnp.int32, (L, L), 0)
    c_i = lax.broadcasted_iota(jnp.int32, (L, L), 1)
    causal = r_i >= c_i
    dt = _softplus(dtraw + dtb)
    a = dt * (-jnp.exp(alog))
    acs = _hd(causal.astype(F32), a)
    e_r = lax.broadcasted_iota(jnp.int32, (LANES, SSD_GS), 0)
    e_c = lax.broadcasted_iota(jnp.int32, (LANES, SSD_GS), 1)
    sel = (e_r == g * (SSD_H // SSD_G) + jnp.right_shift(e_c, 6)).astype(F32)
    dt_x = _hd(dt, sel)
    acs_x = _hd(acs, sel)
    d_x = _hd(jnp.broadcast_to(dsk, (8, LANES)), sel)[0:1]
    alast = acs_x[L - 1:L]
    xdt = x * dt_x
    cb = _nt(cm, bm)
    lane = lax.broadcasted_iota(jnp.int32, (L, LANES), 1)
    ys = []
    for j in range(SSD_GS // LANES):
        xp = xdt[:, j * LANES:(j + 1) * LANES]
        yp = None
        for hh in range(2):
            c0 = (2 * j + hh) * SSD_P
            cmx = jnp.broadcast_to(acs_x[:, c0:c0 + 1], (L, L))
            dec = jnp.exp(jnp.where(causal, cmx - cmx.T, -jnp.inf))
            half = (lane < SSD_P) if hh == 0 else (lane >= SSD_P)
            t = _nn(cb * dec, jnp.where(half, xp, 0.0))
            yp = t if yp is None else yp + t
        ys.append(yp)
    y_diag = jnp.concatenate(ys, axis=1)
    st = _tn(bm, xdt * jnp.exp(alast - acs_x))
    new = prev * jnp.exp(alast) + st
    y_off = _nn(cm, prev) * jnp.exp(acs_x)
    y = y_diag + y_off + x * d_x
    yg = y * _silu(z)
    yn = yg * lax.rsqrt(jnp.mean(yg * yg, -1, keepdims=True) + RMS_EPS) * nw
    return yn, new


def _ssd_specs(nc, rev):
    cc = (lambda c: nc - 1 - c) if rev else (lambda c: c)
    zb = 0
    dtb = (SSD_DI + SSD_CONV_DIM) // LANES
    bb = SSD_DI // LANES
    cbk = (SSD_DI + SSD_G * SSD_N) // LANES
    par = pl.BlockSpec((1, LANES), lambda c, g: (0, 0))
    return dict(
        z=pl.BlockSpec((SSD_L, SSD_GS), lambda c, g: (cc(c), zb + g)),
        dt=pl.BlockSpec((SSD_L, LANES), lambda c, g: (cc(c), dtb)),
        x=pl.BlockSpec((SSD_L, SSD_GS), lambda c, g: (cc(c), g)),
        bm=pl.BlockSpec((SSD_L, LANES), lambda c, g: (cc(c), bb + g)),
        cm=pl.BlockSpec((SSD_L, LANES), lambda c, g: (cc(c), cbk + g)),
        par=par,
        nw=pl.BlockSpec((1, SSD_GS), lambda c, g: (0, g)),
        st=pl.BlockSpec((1, 1, SSD_N, SSD_GS), lambda c, g: (cc(c), g, 0, 0)),
        y=pl.BlockSpec((SSD_L, SSD_GS), lambda c, g: (cc(c), g)),
        bc=pl.BlockSpec((SSD_L, LANES), lambda c, g: (cc(c), g)),
        dtout=pl.BlockSpec((SSD_L, LANES), lambda c, g: (cc(c), 0)),
    )


def _ssd_fwd(proj, act, dtb, alog, dsk, nw, name):
    t = proj.shape[0]
    nc = t // SSD_L
    sp = _ssd_specs(nc, False)

    def body(z_ref, dt_ref, x_ref, bm_ref, cm_ref, dtb_ref, alog_ref, dsk_ref, nw_ref, y_ref, st_ref, state):
        c, g = pl.program_id(0), pl.program_id(1)

        @pl.when(c == 0)
        def _():
            state[g] = jnp.zeros((SSD_N, SSD_GS), F32)

        prev = state[g]
        st_ref[0, 0] = prev
        yn, new = _ssd_chunk(g, x_ref[...], z_ref[...], bm_ref[...], cm_ref[...], dt_ref[...], dtb_ref[...],
                             alog_ref[...], dsk_ref[...], nw_ref[...], prev)
        y_ref[...] = yn.astype(y_ref.dtype)
        state[g] = new

    return pl.pallas_call(
        body, name=name, grid=(nc, SSD_G),
        in_specs=[sp["z"], sp["dt"], sp["x"], sp["bm"], sp["cm"], sp["par"], sp["par"], sp["par"], sp["nw"]],
        out_specs=[sp["y"], sp["st"]],
        out_shape=[jax.ShapeDtypeStruct((t, SSD_DI), F32), jax.ShapeDtypeStruct((nc, SSD_G, SSD_N, SSD_GS), F32)],
        scratch_shapes=[pltpu.VMEM((SSD_G, SSD_N, SSD_GS), F32)],
        compiler_params=_cparams(("arbitrary", "arbitrary")),
    )(proj, proj, act, act, act, dtb, alog, dsk, nw)


def _ssd_bwd(proj, act, dtb, alog, dsk, nw, states, dyn, name):
    t = proj.shape[0]
    nc = t // SSD_L
    sp = _ssd_specs(nc, True)

    def body(z_ref, dt_ref, x_ref, bm_ref, cm_ref, dtb_ref, alog_ref, dsk_ref, nw_ref, st_ref, dy_ref,
             dx_ref, dz_ref, dbm_ref, dcm_ref, ddt_ref, ddtb_ref, dalog_ref, ddsk_ref, dnw_ref, dstate):
        c, g = pl.program_id(0), pl.program_id(1)

        @pl.when(c == 0)
        def _():
            dstate[g] = jnp.zeros((SSD_N, SSD_GS), F32)

        @pl.when((c == 0) & (g == 0))
        def _():
            ddtb_ref[...] = jnp.zeros_like(ddtb_ref)
            dalog_ref[...] = jnp.zeros_like(dalog_ref)
            ddsk_ref[...] = jnp.zeros_like(ddsk_ref)
            dnw_ref[...] = jnp.zeros_like(dnw_ref)

        @pl.when(g == 0)
        def _():
            ddt_ref[...] = jnp.zeros_like(ddt_ref)

        _, vjp = jax.vjp(functools.partial(_ssd_chunk, g), x_ref[...], z_ref[...], bm_ref[...], cm_ref[...],
                         dt_ref[...], dtb_ref[...], alog_ref[...], dsk_ref[...], nw_ref[...], st_ref[0, 0])
        dx, dz, dbm, dcm, ddt, ddtb, dalog, ddsk, dnw, dprev = vjp((dy_ref[...].astype(F32), dstate[g]))
        dx_ref[...] = dx
        dz_ref[...] = dz
        dbm_ref[...] = dbm
        dcm_ref[...] = dcm
        ddt_ref[...] += ddt
        ddtb_ref[...] += ddtb
        dalog_ref[...] += dalog
        ddsk_ref[...] += ddsk
        dnw_ref[g] += dnw
        dstate[g] = dprev

    par_out = pl.BlockSpec((1, LANES), lambda c, g: (0, 0))
    sds = jax.ShapeDtypeStruct
    return pl.pallas_call(
        body, name=name, grid=(nc, SSD_G),
        in_specs=[sp["z"], sp["dt"], sp["x"], sp["bm"], sp["cm"], sp["par"], sp["par"], sp["par"], sp["nw"],
                  sp["st"], sp["y"]],
        out_specs=[sp["y"], sp["y"], sp["bc"], sp["bc"], sp["dtout"], par_out, par_out, par_out,
                   pl.BlockSpec((SSD_G, 1, SSD_GS), lambda c, g: (0, 0, 0))],
        out_shape=[sds((t, SSD_DI), F32), sds((t, SSD_DI), F32), sds((t, SSD_G * SSD_N), F32), sds((t, SSD_G * SSD_N), F32),
                   sds((t, LANES), F32), sds((1, LANES), F32), sds((1, LANES), F32), sds((1, LANES), F32),
                   sds((SSD_G, 1, SSD_GS), F32)],
        scratch_shapes=[pltpu.VMEM((SSD_G, SSD_N, SSD_GS), F32)],
        compiler_params=_cparams(("arbitrary", "arbitrary")),
    )(proj, proj, act, act, act, dtb, alog, dsk, nw, states, dyn)


def _pad_lanes(v, width=LANES, offset=0):
    return jnp.pad(v.astype(F32), (offset, width - offset - v.shape[0])).reshape(1, width)


def _ssd_layer_fwd(u, w, ln_g, ln_b, tag, ride=None):
    w_in = jnp.concatenate([w["in_w"], jnp.zeros((D_MODEL, LANES - SSD_H), w["in_w"].dtype)], axis=1)
    dtb, alog, dsk = _pad_lanes(w["dt_bias"]), _pad_lanes(w["a_log"]), _pad_lanes(w["d"])
    nw = w["norm_w"].reshape(1, SSD_DI)
    cb = w["conv_b"].reshape(1, SSD_CONV_DIM)
    proj = _mm(u, w_in, name=tag + "_in", ride=ride)
    proj, rode = (proj, []) if ride is None else proj
    act = _conv_fwd(proj, SSD_DI, w["conv_w"], cb, name=tag + "_conv")
    yn, states = _ssd_fwd(proj, act, dtb, alog, dsk, nw, name=tag + "_scan")
    y = _mm(yn, w["out_w"], name=tag + "_out")
    h = _ln_fwd(u, y, ln_g, ln_b, name=tag + "_ln")
    saved = dict(u=u, w_in=w_in, proj=proj, act=act, states=states, yn=yn, y=y, dtb=dtb, alog=alog, dsk=dsk, nw=nw, cb=cb)
    return h, saved, rode


def _ssd_layer_bwd(s, w, ln_g, dr_up, du_up, tag, ride=None):
    dr, dg, db = _ln_bwd(s["u"], s["y"], ln_g, dr_up, du_up, name=tag + "_ln_b")
    dyn = _mm(dr, w["out_w"], tb=True, name=tag + "_out_bx")
    d_out_w = _mm(s["yn"], dr, ta=True, out_dtype=WIRE, name=tag + "_out_bw")
    dx, dz, dbm, dcm, ddt, ddtb, dalog, ddsk, dnw = _ssd_bwd(
        s["proj"], s["act"], s["dtb"], s["alog"], s["dsk"], s["nw"], s["states"], dyn, name=tag + "_scan_b")
    dact = jnp.concatenate([dx, dbm, dcm], axis=1)
    dpre, d_conv_w, d_conv_b = _conv_bwd_pre(s["proj"], SSD_DI, w["conv_w"], s["cb"], dact, name=tag + "_conv_bp")
    dxbc = _conv_bwd_x(dpre, w["conv_w"], name=tag + "_conv_bx")
    dproj = jnp.concatenate([dz, dxbc, ddt], axis=1)
    d_in_w = _mm(s["u"], dproj, ta=True, out_dtype=WIRE, name=tag + "_in_bw", ride=ride)
    d_in_w, rode = (d_in_w, []) if ride is None else d_in_w
    du = _mm(dproj, s["w_in"], tb=True, name=tag + "_in_bx")
    grads = dict(in_w=d_in_w[:, :SSD_DI + SSD_CONV_DIM + SSD_H], conv_w=d_conv_w, conv_b=d_conv_b.reshape(-1),
                 dt_bias=ddtb[0, :SSD_H], a_log=dalog[0, :SSD_H], d=ddsk[0, :SSD_H], norm_w=dnw.reshape(-1),
                 out_w=d_out_w, ln_g=dg[0], ln_b=db[0])
    return dr, du, grads, rode


MLA_LOW = MLA_QR + MLA_KVR + LANES
MLA_ZB = MLA_LOW // LANES


def _rope_mat():
    r = lax.broadcasted_iota(jnp.int32, (LANES, LANES), 0)
    c = lax.broadcasted_iota(jnp.int32, (LANES, LANES), 1)
    hf = MLA_ROPE // 2
    return jnp.where((c < hf) & (r == c + hf), -1.0, 0.0) + jnp.where((c >= hf) & (c < 2 * hf) & (r == c - hf), 1.0, 0.0)


def _rope(x, cosf, sinf):
    return x * cosf + _hd(x, _rope_mat()) * sinf


def _rope_adj(d, cosf, sinf):
    return d * cosf - _hd(d * sinf, _rope_mat())


def _mla_low_fn(low, qnw, kvnw, cosf, sinf):
    qc, kvc, kr = low[:, :MLA_QR], low[:, MLA_QR:MLA_QR + MLA_KVR], low[:, MLA_QR + MLA_KVR:]
    qn = qc * lax.rsqrt(jnp.mean(qc * qc, -1, keepdims=True) + RMS_EPS) * qnw
    kvn = kvc * lax.rsqrt(jnp.mean(kvc * kvc, -1, keepdims=True) + RMS_EPS) * kvnw
    return qn, kvn, _rope(kr, cosf, sinf)


def _mla_low_fwd(proj, qnw, kvnw, cosf, sinf, name):
    t = proj.shape[0]
    tr = _tile(t, 256)

    def body(low_ref, qnw_ref, kvnw_ref, cos_ref, sin_ref, qn_ref, kvn_ref, kr_ref):
        qn, kvn, kr = _mla_low_fn(low_ref[...], qnw_ref[...], kvnw_ref[...], cos_ref[...], sin_ref[...])
        qn_ref[...] = qn
        kvn_ref[...] = kvn
        kr_ref[...] = kr

    row = lambda wdt: pl.BlockSpec((tr, wdt), lambda i: (i, 0))
    par = lambda wdt: pl.BlockSpec((1, wdt), lambda i: (0, 0))
    sds = jax.ShapeDtypeStruct
    return pl.pallas_call(
        body, name=name, grid=(t // tr,),
        in_specs=[row(MLA_LOW), par(MLA_QR), par(MLA_KVR), row(LANES), row(LANES)],
        out_specs=[row(MLA_QR), row(MLA_KVR), row(LANES)],
        out_shape=[sds((t, MLA_QR), F32), sds((t, MLA_KVR), F32), sds((t, LANES), F32)],
        compiler_params=_cparams(("parallel",)),
    )(proj, qnw, kvnw, cosf, sinf)


def _mla_low_bwd(proj, qnw, kvnw, cosf, sinf, dqn, dkvn, dkr, name):
    t = proj.shape[0]
    tr = _tile(t, 256)

    def body(low_ref, qnw_ref, kvnw_ref, cos_ref, sin_ref, dqn_ref, dkvn_ref, dkr_ref, dlow_ref, dqnw_ref, dkvnw_ref):
        i = pl.program_id(0)

        @pl.when(i == 0)
        def _():
            dqnw_ref[...] = jnp.zeros_like(dqnw_ref)
            dkvnw_ref[...] = jnp.zeros_like(dkvnw_ref)

        cosf, sinf = cos_ref[...], sin_ref[...]
        _, vjp = jax.vjp(lambda a, b, c: _mla_low_fn(a, b, c, cosf, sinf), low_ref[...], qnw_ref[...], kvnw_ref[...])
        dlow, dq, dk = vjp((dqn_ref[...], dkvn_ref[...], dkr_ref[...]))
        dlow_ref[...] = dlow
        dqnw_ref[...] += dq
        dkvnw_ref[...] += dk

    row = lambda wdt: pl.BlockSpec((tr, wdt), lambda i: (i, 0))
    par = lambda wdt: pl.BlockSpec((1, wdt), lambda i: (0, 0))
    sds = jax.ShapeDtypeStruct
    return pl.pallas_call(
        body, name=name, grid=(t // tr,),
        in_specs=[row(MLA_LOW), par(MLA_QR), par(MLA_KVR), row(LANES), row(LANES), row(MLA_QR), row(MLA_KVR), row(LANES)],
        out_specs=[row(MLA_LOW), par(MLA_QR), par(MLA_KVR)],
        out_shape=[sds((t, MLA_LOW), F32), sds((1, MLA_QR), F32), sds((1, MLA_KVR), F32)],
        compiler_params=_cparams(("arbitrary",)),
    )(proj, qnw, kvnw, cosf, sinf, dqn, dkvn, dkr)


def _rope_heads(x, col_blk0, cosf, sinf, adjoint, name):
    t = x.shape[0]
    tr = _tile(t, 512)

    def body(x_ref, cos_ref, sin_ref, o_ref):
        f = _rope_adj if adjoint else _rope
        o_ref[...] = f(x_ref[...], cos_ref[...], sin_ref[...])

    tab = pl.BlockSpec((tr, LANES), lambda i, h: (i, 0))
    return pl.pallas_call(
        body, name=name, grid=(t // tr, MLA_H),
        in_specs=[pl.BlockSpec((tr, LANES), lambda i, h: (i, col_blk0 + h)), tab, tab],
        out_specs=pl.BlockSpec((tr, LANES), lambda i, h: (i, h)),
        out_shape=jax.ShapeDtypeStruct((t, MLA_H * LANES), F32), compiler_params=_cparams(("parallel", "parallel")),
    )(x, cosf, sinf)


def _att_scores(qn, qr, kn, kr, i, j, tq, tk):
    s = (_dg(_mx(qn), _mx(kn), 1, 1) + _dg(_mx(qr), _mx(kr), 1, 1)) * MLA_SCALE
    qpos = i * tq + lax.broadcasted_iota(jnp.int32, (tq, tk), 0)
    kpos = j * tk + lax.broadcasted_iota(jnp.int32, (tq, tk), 1)
    return jnp.where(kpos <= qpos, s, -jnp.inf)


def _attn_fwd(q, qr, kv, kr, proj, name, ride=None):
    t = q.shape[0]
    tq = tk = _tile(t, ATT_BLK)
    nq = nk = t // tq

    def body(qn_ref, qr_ref, kn_ref, kr_ref, v_ref, z_ref, o_ref, og_ref, lse_ref, m_s, l_s, acc_s):
        i, j = pl.program_id(1), pl.program_id(2)

        @pl.when(j == 0)
        def _():
            m_s[...] = jnp.full_like(m_s, -jnp.inf)
            l_s[...] = jnp.zeros_like(l_s)
            acc_s[...] = jnp.zeros_like(acc_s)

        @pl.when(j <= i)
        def _():
            s = _att_scores(qn_ref[...], qr_ref[...], kn_ref[...], kr_ref[...], i, j, tq, tk)
            m_new = jnp.maximum(m_s[...], jnp.max(s, -1, keepdims=True))
            p = jnp.exp(s - m_new)
            corr = jnp.exp(m_s[...] - m_new)
            l_s[...] = corr * l_s[...] + jnp.sum(p, -1, keepdims=True)
            acc_s[...] = corr * acc_s[...] + _dg(_mx(p), _mx(v_ref[...]), 1, 0)
            m_s[...] = m_new

        @pl.when(j == nk - 1)
        def _():
            o = acc_s[...] / l_s[...]
            o_ref[...] = o
            og_ref[...] = (o * _silu(z_ref[...])).astype(og_ref.dtype)
            lse_ref[...] = jnp.broadcast_to(m_s[...] + jnp.log(l_s[...]), (tq, LANES))

    qs = lambda off: pl.BlockSpec((tq, LANES), lambda h, i, j: (i, off + h))
    ks = lambda off: pl.BlockSpec((tk, LANES), lambda h, i, j: (jnp.minimum(j, i), off + h))
    sds = jax.ShapeDtypeStruct
    return _hosted_call(
        body, name=name, grid=(MLA_H, nq, nk),
        in_specs=[qs(0), qs(0), ks(0), pl.BlockSpec((tk, LANES), lambda h, i, j: (jnp.minimum(j, i), 0)), ks(MLA_H), qs(MLA_ZB)],
        out_specs=[qs(0), qs(0), qs(0)],
        out_shape=[sds((t, MLA_GATE), F32), sds((t, MLA_GATE), F32), sds((t, MLA_H * LANES), F32)],
        scratch_shapes=[pltpu.VMEM((tq, 1), F32), pltpu.VMEM((tq, 1), F32), pltpu.VMEM((tq, LANES), F32)],
        sem=("parallel", "parallel", "arbitrary"), args=(q, qr, kv, kr, kv, proj), ride=ride)


def _gate_bwd(dog, o, proj, name):
    t = o.shape[0]
    tr = _tile(t, 512)

    def body(d_ref, o_ref, z_ref, do_ref, dz_ref):
        z = z_ref[...]
        sg = jax.nn.sigmoid(z)
        d = d_ref[...]
        do_ref[...] = d * z * sg
        dz_ref[...] = d * o_ref[...] * (sg * (1.0 + z * (1.0 - sg)))

    blk = lambda off: pl.BlockSpec((tr, 512), lambda i, j: (i, off + j))
    assert MLA_LOW % 512 != 0 or True
    zspec = pl.BlockSpec((tr, LANES), lambda i, j: (i, MLA_ZB + j))
    b128 = pl.BlockSpec((tr, LANES), lambda i, j: (i, j))
    sds = jax.ShapeDtypeStruct
    return pl.pallas_call(
        body, name=name, grid=(t // tr, MLA_GATE // LANES),
        in_specs=[b128, b128, zspec], out_specs=[b128, b128],
        out_shape=[sds((t, MLA_GATE), F32), sds((t, MLA_GATE), F32)],
        compiler_params=_cparams(("parallel", "parallel")),
    )(dog, o, proj)


def _attn_bwd_q(q, qr, kv, kr, o, do, lse, name, ride=None):
    t = q.shape[0]
    tq = tk = _tile(t, ATT_BLK)
    nq = nk = t // tq

    def body(qn_ref, qr_ref, kn_ref, kr_ref, v_ref, o_ref, do_ref, lse_ref, dqn_ref, dqr_ref, an_s, ar_s):
        i, j = pl.program_id(1), pl.program_id(2)

        @pl.when(j == 0)
        def _():
            an_s[...] = jnp.zeros_like(an_s)
            ar_s[...] = jnp.zeros_like(ar_s)

        @pl.when(j <= i)
        def _():
            s = _att_scores(qn_ref[...], qr_ref[...], kn_ref[...], kr_ref[...], i, j, tq, tk)
            p = jnp.exp(s - lse_ref[:, 0:1])
            do = do_ref[...]
            dp = _dg(_mx(do), _mx(v_ref[...]), 1, 1)
            dl = jnp.sum(do * o_ref[...], -1, keepdims=True)
            ds = _mx(p * (dp - dl) * MLA_SCALE)
            an_s[...] += _dg(ds, _mx(kn_ref[...]), 1, 0)
            ar_s[...] += _dg(ds, _mx(kr_ref[...]), 1, 0)

        @pl.when(j == nk - 1)
        def _():
            dqn_ref[...] = an_s[...]
            dqr_ref[...] = ar_s[...]

    qs = lambda off: pl.BlockSpec((tq, LANES), lambda h, i, j: (i, off + h))
    ks = lambda off: pl.BlockSpec((tk, LANES), lambda h, i, j: (jnp.minimum(j, i), off + h))
    sds = jax.ShapeDtypeStruct
    return _hosted_call(
        body, name=name, grid=(MLA_H, nq, nk),
        in_specs=[qs(0), qs(0), ks(0), pl.BlockSpec((tk, LANES), lambda h, i, j: (jnp.minimum(j, i), 0)), ks(MLA_H),
                  qs(0), qs(0), qs(0)],
        out_specs=[qs(0), qs(0)],
        out_shape=[sds((t, MLA_H * LANES), F32), sds((t, MLA_H * LANES), F32)],
        scratch_shapes=[pltpu.VMEM((tq, LANES), F32), pltpu.VMEM((tq, LANES), F32)],
        sem=("parallel", "parallel", "arbitrary"), args=(q, qr, kv, kr, kv, o, do, lse), ride=ride)


def _attn_bwd_kv(q, qr, kv, kr, o, do, lse, name):
    t = q.shape[0]
    tq = tk = _tile(t, ATT_BLK)
    nq = nk = t // tq

    def body(qn_ref, qr_ref, kn_ref, kr_ref, v_ref, o_ref, do_ref, lse_ref, dkn_ref, dv_ref, dkr_ref, akn_s, av_s):
        j, h, i = pl.program_id(0), pl.program_id(1), pl.program_id(2)

        @pl.when((h == 0) & (i == 0))
        def _():
            dkr_ref[...] = jnp.zeros_like(dkr_ref)

        @pl.when(i == 0)
        def _():
            akn_s[...] = jnp.zeros_like(akn_s)
            av_s[...] = jnp.zeros_like(av_s)

        @pl.when(i >= j)
        def _():
            s = _att_scores(qn_ref[...], qr_ref[...], kn_ref[...], kr_ref[...], i, j, tq, tk)
            p = jnp.exp(s - lse_ref[:, 0:1])
            do = do_ref[...]
            dp = _dg(_mx(do), _mx(v_ref[...]), 1, 1)
            dl = jnp.sum(do * o_ref[...], -1, keepdims=True)
            ds = _mx(p * (dp - dl) * MLA_SCALE)
            av_s[...] += _dg(_mx(p), _mx(do), 0, 0)
            akn_s[...] += _dg(ds, _mx(qn_ref[...]), 0, 0)
            dkr_ref[...] += _dg(ds, _mx(qr_ref[...]), 0, 0)

        @pl.when(i == nq - 1)
        def _():
            dkn_ref[...] = akn_s[...]
            dv_ref[...] = av_s[...]

    qs = lambda off: pl.BlockSpec((tq, LANES), lambda j, h, i: (jnp.maximum(i, j), off + h))
    ks = lambda off: pl.BlockSpec((tk, LANES), lambda j, h, i: (j, off + h))
    sds = jax.ShapeDtypeStruct
    return pl.pallas_call(
        body, name=name, grid=(nk, MLA_H, nq),
        in_specs=[qs(0), qs(0), ks(0), pl.BlockSpec((tk, LANES), lambda j, h, i: (j, 0)), ks(MLA_H), qs(0), qs(0), qs(0)],
        out_specs=[ks(0), ks(0), pl.BlockSpec((tk, LANES), lambda j, h, i: (j, 0))],
        out_shape=[sds((t, MLA_H * LANES), F32), sds((t, MLA_H * LANES), F32), sds((t, LANES), F32)],
        scratch_shapes=[pltpu.VMEM((tk, LANES), F32), pltpu.VMEM((tk, LANES), F32)],
        compiler_params=_cparams(("parallel", "arbitrary", "arbitrary")),
    )(q, qr, kv, kr, kv, o, do, lse)


def _rope_tables(positions):
    inv_freq = ROPE_THETA ** (-jnp.arange(0, MLA_ROPE, 2, dtype=F32) / MLA_ROPE)
    ang = positions.astype(F32)[:, None] * inv_freq
    pad = jnp.zeros((positions.shape[0], LANES - MLA_ROPE), F32)
    cos, sin = jnp.cos(ang), jnp.sin(ang)
    return jnp.concatenate([cos, cos, pad], 1), jnp.concatenate([sin, sin, pad], 1)


def _mla_weights(w):
    dt = w["in_w"].dtype
    iw = w["in_w"]
    c1 = MLA_QR + MLA_KVR + MLA_ROPE
    w_in = jnp.concatenate([iw[:, :c1], jnp.zeros((D_MODEL, LANES - MLA_ROPE), dt), iw[:, c1:]], axis=1)
    qu = w["q_up_w"].reshape(MLA_QR, MLA_H, MLA_NOPE + MLA_ROPE)
    qrope = jnp.concatenate([qu[:, :, MLA_NOPE:], jnp.zeros((MLA_QR, MLA_H, LANES - MLA_ROPE), dt)], axis=2)
    w_q = jnp.concatenate([qu[:, :, :MLA_NOPE].reshape(MLA_QR, -1), qrope.reshape(MLA_QR, -1)], axis=1)
    kvu = w["kv_up_w"].reshape(MLA_KVR, MLA_H, MLA_NOPE + MLA_V)
    w_kv = jnp.concatenate([kvu[:, :, :MLA_NOPE].reshape(MLA_KVR, -1), kvu[:, :, MLA_NOPE:].reshape(MLA_KVR, -1)], axis=1)
    return w_in, w_q, w_kv


def _mla_layer_fwd(u, w, ln_g, ln_b, cosf, sinf, tag, ride=None):
    w_in, w_q, w_kv = _mla_weights(w)
    qnw, kvnw = w["q_norm_w"].reshape(1, -1), w["kv_norm_w"].reshape(1, -1)
    proj = _mm(u, w_in, name=tag + "_in")
    qn, kvn, kr = _mla_low_fwd(proj, qnw, kvnw, cosf, sinf, name=tag + "_low")
    q = _mm(qn, w_q, name=tag + "_qup")
    kv = _mm(kvn, w_kv, name=tag + "_kvup")
    qr = _rope_heads(q, MLA_H, cosf, sinf, False, name=tag + "_qrope")
    (o, og, lse), rode = _attn_fwd(q, qr, kv, kr, proj, name=tag + "_attn", ride=ride)
    y = _mm(og, w["out_w"], name=tag + "_out")
    h = _ln_fwd(u, y, ln_g, ln_b, name=tag + "_ln")
    saved = dict(u=u, w_in=w_in, w_q=w_q, w_kv=w_kv, qnw=qnw, kvnw=kvnw, proj=proj, qn=qn, kvn=kvn, kr=kr, q=q, kv=kv,
                 qr=qr, o=o, og=og, lse=lse, y=y)
    return h, saved, rode


def _mla_layer_bwd(s, w, ln_g, cosf, sinf, dr_up, du_up, tag, ride=None):
    dr, dg, db = _ln_bwd(s["u"], s["y"], ln_g, dr_up, du_up, name=tag + "_ln_b")
    dog = _mm(dr, w["out_w"], tb=True, name=tag + "_out_bx")
    d_out_w = _mm(s["og"], dr, ta=True, out_dtype=WIRE, name=tag + "_out_bw")
    do, dz = _gate_bwd(dog, s["o"], s["proj"], name=tag + "_gate_b")
    (dqn_h, dqr_rot), rode = _attn_bwd_q(s["q"], s["qr"], s["kv"], s["kr"], s["o"], do, s["lse"], name=tag + "_attn_bq",
                                         ride=ride)
    dkn_h, dv_h, dkr_rot = _attn_bwd_kv(s["q"], s["qr"], s["kv"], s["kr"], s["o"], do, s["lse"], name=tag + "_attn_bkv")
    dqr = _rope_heads(dqr_rot, 0, cosf, sinf, True, name=tag + "_qrope_b")
    dq = jnp.concatenate([dqn_h, dqr], axis=1)
    dkv = jnp.concatenate([dkn_h, dv_h], axis=1)
    d_wq = _mm(s["qn"], dq, ta=True, out_dtype=WIRE, name=tag + "_qup_bw")
    dqn = _mm(dq, s["w_q"], tb=True, name=tag + "_qup_bx")
    d_wkv = _mm(s["kvn"], dkv, ta=True, out_dtype=WIRE, name=tag + "_kvup_bw")
    dkvn = _mm(dkv, s["w_kv"], tb=True, name=tag + "_kvup_bx")
    dlow, dqnw, dkvnw = _mla_low_bwd(s["proj"], s["qnw"], s["kvnw"], cosf, sinf, dqn, dkvn, dkr_rot, name=tag + "_low_b")
    dproj = jnp.concatenate([dlow, dz], axis=1)
    d_in = _mm(s["u"], dproj, ta=True, out_dtype=WIRE, name=tag + "_in_bw")
    du = _mm(dproj, s["w_in"], tb=True, name=tag + "_in_bx")
    c1 = MLA_QR + MLA_KVR + MLA_ROPE
    d_in_w = jnp.concatenate([d_in[:, :c1], d_in[:, MLA_LOW:]], axis=1)
    dq3n = d_wq[:, :MLA_H * MLA_NOPE].reshape(MLA_QR, MLA_H, MLA_NOPE)
    dq3r = d_wq[:, MLA_H * MLA_NOPE:].reshape(MLA_QR, MLA_H, LANES)[:, :, :MLA_ROPE]
    d_q_up = jnp.concatenate([dq3n, dq3r], axis=2).reshape(MLA_QR, -1)
    dkv3 = d_wkv.reshape(MLA_KVR, 2, MLA_H, MLA_NOPE)
    d_kv_up = jnp.concatenate([dkv3[:, 0], dkv3[:, 1]], axis=2).reshape(MLA_KVR, -1)
    grads = dict(in_w=d_in_w, q_norm_w=dqnw[0], q_up_w=d_q_up, kv_norm_w=dkvnw[0], kv_up_w=d_kv_up, out_w=d_out_w,
                 ln_g=dg[0], ln_b=db[0])
    return dr, du, grads, rode


GDN_REP = GDN_HV // GDN_HK
GDN_A_LANE = GDN_HV
GDN_HPB = 2
GDN_VPB = GDN_HPB * GDN_REP


def _h3(a, b, ca=1, cb=0):
    ah, bh = _mx(a), _mx(b)
    al, bl = _mx(a - ah.astype(F32)), _mx(b - bh.astype(F32))
    return _dg(ah, bh, ca, cb) + (_dg(ah, bl, ca, cb) + _dg(al, bh, ca, cb))


@jax.custom_vjp
def _neumann_inverse(x):
    L = x.shape[0]
    eye = (lax.broadcasted_iota(jnp.int32, (L, L), 0) == lax.broadcasted_iota(jnp.int32, (L, L), 1)).astype(F32)
    inv = eye + x
    xp = x
    for _ in range(L.bit_length() - 2):
        xp = _h3(xp, xp)
        inv = inv + _h3(inv, xp)
    return inv


def _neumann_f(x):
    inv = _neumann_inverse(x)
    return inv, inv


def _neumann_b(inv, ct):
    return (_h3(_h3(inv, ct, 0, 0), inv, 1, 1),)


_neumann_inverse.defvjp(_neumann_f, _neumann_b)


def _gdn_chunk(hb, q, k, v, z, ba, alog, dtb, nw, states):
    L = q.shape[0]
    r_i = lax.broadcasted_iota(jnp.int32, (L, L), 0)
    c_i = lax.broadcasted_iota(jnp.int32, (L, L), 1)
    incl, strict = r_i >= c_i, r_i > c_i
    beta_all = jax.nn.sigmoid(ba)
    g_all = -jnp.exp(alog) * _softplus(ba + dtb)
    gcs_all = _hd(incl.astype(F32), g_all)
    lane = lax.broadcasted_iota(jnp.int32, (L, LANES), 1)
    outs, news = [], []
    for i in range(GDN_HPB):
        qi, ki = q[:, i * GDN_DK:(i + 1) * GDN_DK], k[:, i * GDN_DK:(i + 1) * GDN_DK]
        qn = qi * lax.rsqrt(jnp.sum(qi * qi, -1, keepdims=True) + RMS_EPS) * (GDN_DK ** -0.5)
        kn = ki * lax.rsqrt(jnp.sum(ki * ki, -1, keepdims=True) + RMS_EPS)
        qk_raw = _nt(qn, kn)
        for hh in range(GDN_REP):
            idx = GDN_REP * i + hh
            hv = GDN_VPB * hb + idx
            beta = jnp.sum(jnp.where(lane == hv, beta_all, 0.0), axis=1, keepdims=True)
            gc = jnp.sum(jnp.where(lane == GDN_A_LANE + hv, gcs_all, 0.0), axis=1, keepdims=True)
            gm = jnp.broadcast_to(gc, (L, L))
            decay = jnp.exp(jnp.where(incl, gm - gm.T, -jnp.inf))
            vv = v[:, idx * GDN_DV:(idx + 1) * GDN_DV]
            zz = z[:, idx * GDN_DV:(idx + 1) * GDN_DV]
            s = states[idx]
            kb = kn * beta
            eg = jnp.exp(gc)
            inv = _neumann_inverse(-jnp.where(strict, _nt(kb, kn) * decay, 0.0))
            uw = _nn(inv, jnp.concatenate([vv * beta, kb * eg], axis=1))
            uu, ww = uw[:, :GDN_DV], uw[:, GDN_DV:]
            qk = jnp.where(incl, qk_raw * decay, 0.0)
            glast = gc[L - 1:L]
            kdec = kn * jnp.exp(glast - gc)
            vnew = uu - _nn(ww, s)
            o = _nn(qn * eg, s) + _nn(qk, vnew)
            news.append(s * jnp.exp(glast) + _tn(kdec, vnew))
            outs.append(o * lax.rsqrt(jnp.mean(o * o, -1, keepdims=True) + RMS_EPS) * nw * _silu(zz))
    return jnp.concatenate(outs, axis=1), tuple(news)


def _gdn_specs(nc, rev):
    cc = (lambda c: nc - 1 - c) if rev else (lambda c: c)
    wq, wv = GDN_HPB * GDN_DK, GDN_VPB * GDN_DV
    par = pl.BlockSpec((1, LANES), lambda c, h: (0, 0))
    return dict(
        q=pl.BlockSpec((GDN_L, wq), lambda c, h: (cc(c), h)),
        k=pl.BlockSpec((GDN_L, wq), lambda c, h: (cc(c), GDN_KEY // wq + h)),
        v=pl.BlockSpec((GDN_L, wv), lambda c, h: (cc(c), 2 * GDN_KEY // wv + h)),
        z=pl.BlockSpec((GDN_L, wv), lambda c, h: (cc(c), GDN_CONV_DIM // wv + h)),
        ba=pl.BlockSpec((GDN_L, LANES), lambda c, h: (cc(c), (GDN_CONV_DIM + GDN_VAL) // LANES)),
        par=par,
        st=pl.BlockSpec((1, GDN_VPB, GDN_DK, GDN_DV), lambda c, h: (cc(c), h, 0, 0)),
        o=pl.BlockSpec((GDN_L, wv), lambda c, h: (cc(c), h)),
        qk_out=pl.BlockSpec((GDN_L, wq), lambda c, h: (cc(c), h)),
        ba_out=pl.BlockSpec((GDN_L, LANES), lambda c, h: (cc(c), 0)),
    )


def _gdn_fwd(proj, act, alog, dtb, nw, name, ride=None):
    t = proj.shape[0]
    nc = t // GDN_L
    sp = _gdn_specs(nc, False)

    def body(q_ref, k_ref, v_ref, z_ref, ba_ref, alog_ref, dtb_ref, nw_ref, o_ref, st_ref, state):
        c, h = pl.program_id(0), pl.program_id(1)

        @pl.when(c == 0)
        def _():
            for i in range(GDN_VPB):
                state[h * GDN_VPB + i] = jnp.zeros((GDN_DK, GDN_DV), F32)

        prev = tuple(state[h * GDN_VPB + i] for i in range(GDN_VPB))
        for i in range(GDN_VPB):
            st_ref[0, i] = prev[i]
        on, new = _gdn_chunk(h, q_ref[...], k_ref[...], v_ref[...], z_ref[...], ba_ref[...], alog_ref[...],
                             dtb_ref[...], nw_ref[...], prev)
        o_ref[...] = on.astype(o_ref.dtype)
        for i in range(GDN_VPB):
            state[h * GDN_VPB + i] = new[i]

    sds = jax.ShapeDtypeStruct
    return _hosted_call(
        body, name=name, grid=(nc, GDN_HK // GDN_HPB),
        in_specs=[sp["q"], sp["k"], sp["v"], sp["z"], sp["ba"], sp["par"], sp["par"], sp["par"]],
        out_specs=[sp["o"], sp["st"]],
        out_shape=[sds((t, GDN_VAL), F32), sds((nc, GDN_HV, GDN_DK, GDN_DV), F32)],
        scratch_shapes=[pltpu.VMEM((GDN_HV, GDN_DK, GDN_DV), F32)],
        sem=("arbitrary", "arbitrary"), args=(act, act, act, proj, proj, alog, dtb, nw), ride=ride)


def _gdn_bwd(proj, act, alog, dtb, nw, states, don, name, ride=None):
    t = proj.shape[0]
    nc = t // GDN_L
    sp = _gdn_specs(nc, True)

    def body(q_ref, k_ref, v_ref, z_ref, ba_ref, alog_ref, dtb_ref, nw_ref, st_ref, do_ref,
             dq_ref, dk_ref, dv_ref, dz_ref, dba_ref, dalog_ref, ddtb_ref, dnw_ref, dstate):
        c, h = pl.program_id(0), pl.program_id(1)

        @pl.when(c == 0)
        def _():
            for i in range(GDN_VPB):
                dstate[h * GDN_VPB + i] = jnp.zeros((GDN_DK, GDN_DV), F32)

        @pl.when((c == 0) & (h == 0))
        def _():
            dalog_ref[...] = jnp.zeros_like(dalog_ref)
            ddtb_ref[...] = jnp.zeros_like(ddtb_ref)
            dnw_ref[...] = jnp.zeros_like(dnw_ref)

        @pl.when(h == 0)
        def _():
            dba_ref[...] = jnp.zeros_like(dba_ref)

        prev = tuple(st_ref[0, i] for i in range(GDN_VPB))
        _, vjp = jax.vjp(functools.partial(_gdn_chunk, h), q_ref[...], k_ref[...], v_ref[...], z_ref[...], ba_ref[...],
                         alog_ref[...], dtb_ref[...], nw_ref[...], prev)
        ct_state = tuple(dstate[h * GDN_VPB + i] for i in range(GDN_VPB))
        dq, dk, dv, dz, dba, dalog, ddtb, dnw, dprev = vjp((do_ref[...].astype(F32), ct_state))
        dq_ref[...] = dq
        dk_ref[...] = dk
        dv_ref[...] = dv
        dz_ref[...] = dz
        dba_ref[...] += dba
        dalog_ref[...] += dalog
        ddtb_ref[...] += ddtb
        dnw_ref[...] += dnw
        for i in range(GDN_VPB):
            dstate[h * GDN_VPB + i] = dprev[i]

    sds = jax.ShapeDtypeStruct
    par_out = pl.BlockSpec((1, LANES), lambda c, h: (0, 0))
    return _hosted_call(
        body, name=name, grid=(nc, GDN_HK // GDN_HPB),
        in_specs=[sp["q"], sp["k"], sp["v"], sp["z"], sp["ba"], sp["par"], sp["par"], sp["par"], sp["st"], sp["o"]],
        out_specs=[sp["qk_out"], sp["qk_out"], sp["o"], sp["o"], sp["ba_out"], par_out, par_out, par_out],
        out_shape=[sds((t, GDN_KEY), F32), sds((t, GDN_KEY), F32), sds((t, GDN_VAL), F32), sds((t, GDN_VAL), F32),
                   sds((t, LANES), F32), sds((1, LANES), F32), sds((1, LANES), F32), sds((1, LANES), F32)],
        scratch_shapes=[pltpu.VMEM((GDN_HV, GDN_DK, GDN_DV), F32)],
        sem=("arbitrary", "arbitrary"), args=(act, act, act, proj, proj, alog, dtb, nw, states, don), ride=ride)


GDN_PROJ = GDN_CONV_DIM + GDN_VAL + 2 * GDN_HV


def _gdn_layer_fwd(u, w, ln_g, ln_b, tag, ride=None):
    w_in = jnp.concatenate([w["in_w"], jnp.zeros((D_MODEL, GDN_PROJ_PAD - GDN_PROJ), w["in_w"].dtype)], axis=1)
    alog = _pad_lanes(w["a_log"], offset=GDN_A_LANE)
    dtb = _pad_lanes(w["dt_bias"], offset=GDN_A_LANE)
    nw = w["norm_w"].reshape(1, GDN_DV)
    zb = jnp.zeros((1, GDN_CONV_DIM), F32)
    proj = _mm(u, w_in, name=tag + "_in")
    act = _conv_fwd(proj, 0, w["conv_w"], zb, name=tag + "_conv")
    (on, states), rode = _gdn_fwd(proj, act, alog, dtb, nw, name=tag + "_delta", ride=ride)
    y = _mm(on, w["out_w"], name=tag + "_out")
    h = _ln_fwd(u, y, ln_g, ln_b, name=tag + "_ln")
    saved = dict(u=u, w_in=w_in, proj=proj, act=act, states=states, on=on, y=y, alog=alog, dtb=dtb, nw=nw, zb=zb)
    return h, saved, rode


def _gdn_layer_bwd(s, w, ln_g, dr_up, du_up, tag, ride=None):
    t = s["u"].shape[0]
    dr, dg, db = _ln_bwd(s["u"], s["y"], ln_g, dr_up, du_up, name=tag + "_ln_b")
    don = _mm(dr, w["out_w"], tb=True, name=tag + "_out_bx")
    d_out_w = _mm(s["on"], dr, ta=True, out_dtype=WIRE, name=tag + "_out_bw")
    (dq, dk, dv, dz, dba, dalog, ddtb, dnw), rode = _gdn_bwd(s["proj"], s["act"], s["alog"], s["dtb"], s["nw"], s["states"],
                                                            don, name=tag + "_delta_b", ride=ride)
    dact = jnp.concatenate([dq, dk, dv], axis=1)
    dpre, d_conv_w, _ = _conv_bwd_pre(s["proj"], 0, w["conv_w"], s["zb"], dact, name=tag + "_conv_bp")
    dqkv = _conv_bwd_x(dpre, w["conv_w"], name=tag + "_conv_bx")
    dproj = jnp.concatenate([dqkv, dz, dba, jnp.zeros((t, GDN_PROJ_PAD - GDN_PROJ - (LANES - 2 * GDN_HV)), F32)], axis=1)
    d_in = _mm(s["u"], dproj, ta=True, out_dtype=WIRE, name=tag + "_in_bw")
    du = _mm(dproj, s["w_in"], tb=True, name=tag + "_in_bx")
    grads = dict(in_w=d_in[:, :GDN_PROJ], conv_w=d_conv_w, a_log=dalog[0, GDN_A_LANE:GDN_A_LANE + GDN_HV],
                 dt_bias=ddtb[0, GDN_A_LANE:GDN_A_LANE + GDN_HV], norm_w=dnw[0], out_w=d_out_w, ln_g=dg[0], ln_b=db[0])
    return dr, du, grads, rode


def _mesh_pos():
    return lax.axis_index("x"), lax.axis_index("y"), lax.axis_index("c")


def _peer(k, x, y, c):
    return ((1 - x) if k & 4 else x, (1 - y) if k & 2 else y, (1 - c) if k & 1 else c)


def _ride_copies(ins, outs, send, recv, loc, scatter):
    n = len(ins)
    x, y, c = _mesh_pos()
    me = 4 * x + 2 * y + c
    local = [pltpu.make_async_copy(ins[i].at[me] if scatter else ins[i], outs[i].at[me], loc.at[i]) for i in range(n)]
    sends, arrivals = [], []
    for k in range(1, N_DEV):
        peer = _peer(k, x, y, c)
        pidx = 4 * peer[0] + 2 * peer[1] + peer[2]
        for i in range(n):
            src = ins[i].at[pidx] if scatter else ins[i]
            sems = dict(send_sem=send.at[i, k - 1], recv_sem=recv.at[i, k - 1], device_id=peer,
                        device_id_type=pl.DeviceIdType.MESH)
            sends.append(pltpu.make_async_remote_copy(src_ref=src, dst_ref=outs[i].at[me], **sems))
            arrivals.append(pltpu.make_async_remote_copy(src_ref=src, dst_ref=outs[i].at[pidx], **sems))
    return local, sends, arrivals


def _ride_start(ins, outs, send, recv, loc, scatter):
    local, sends, _ = _ride_copies(ins, outs, send, recv, loc, scatter)
    for cp in local + sends:
        cp.start()


def _ride_wait(ins, outs, send, recv, loc, scatter):
    local, sends, arrivals = _ride_copies(ins, outs, send, recv, loc, scatter)
    for cp in arrivals:
        cp.wait_recv()
    for cp in sends:
        cp.wait_send()
    for cp in local:
        cp.wait()


def _ride_shapes(arrs, scatter):
    n = len(arrs)
    out_shape = [jax.ShapeDtypeStruct(a.shape if scatter else (N_DEV,) + a.shape, a.dtype) for a in arrs]
    scratch = [pltpu.SemaphoreType.DMA((n, N_DEV - 1)), pltpu.SemaphoreType.DMA((n, N_DEV - 1)), pltpu.SemaphoreType.DMA((n,))]
    return out_shape, scratch


def _exchange(arrs, scatter, name):
    n = len(arrs)
    hbm = pl.BlockSpec(memory_space=pltpu.HBM)

    def body(*refs):
        ins, outs = refs[:n], refs[n:2 * n]
        _ride_start(ins, outs, *refs[2 * n:], scatter)
        _ride_wait(ins, outs, *refs[2 * n:], scatter)

    out_shape, scratch = _ride_shapes(arrs, scatter)
    return pl.pallas_call(
        body, name=name, in_specs=[hbm] * n, out_specs=[hbm] * n, out_shape=out_shape, scratch_shapes=scratch,
        compiler_params=pltpu.CompilerParams(has_side_effects=True),
    )(*arrs)


def _hosted_call(body, *, name, grid, in_specs, out_specs, out_shape, scratch_shapes, sem, args, ride=None):
    if ride is None:
        return pl.pallas_call(body, name=name, grid=grid, in_specs=in_specs, out_specs=out_specs, out_shape=out_shape,
                              scratch_shapes=scratch_shapes, compiler_params=_cparams(sem))(*args), []
    arrs, scatter = ride
    n, ni, no, ns = len(arrs), len(in_specs), len(out_specs), len(scratch_shapes)
    hbm = pl.BlockSpec(memory_space=pltpu.HBM)
    r_shape, r_scratch = _ride_shapes(arrs, scatter)

    def full(*refs):
        a, ri = refs[:ni], refs[ni:ni + n]
        o, ro = refs[ni + n:ni + n + no], refs[ni + n + no:ni + 2 * n + no]
        s, rs = refs[ni + 2 * n + no:ni + 2 * n + no + ns], refs[ni + 2 * n + no + ns:]
        ids = [pl.program_id(d) for d in range(len(grid))]
        first, last = ids[0] == 0, ids[0] == grid[0] - 1
        for d in range(1, len(grid)):
            first, last = first & (ids[d] == 0), last & (ids[d] == grid[d] - 1)

        @pl.when(first)
        def _():
            _ride_start(ri, ro, *rs, scatter)

        body(*a, *o, *s)

        @pl.when(last)
        def _():
            _ride_wait(ri, ro, *rs, scatter)

    outs = pl.pallas_call(
        full, name=name, grid=grid, in_specs=list(in_specs) + [hbm] * n, out_specs=list(out_specs) + [hbm] * n,
        out_shape=list(out_shape) + r_shape, scratch_shapes=list(scratch_shapes) + r_scratch,
        compiler_params=pltpu.CompilerParams(dimension_semantics=("arbitrary",) * len(grid), vmem_limit_bytes=VMEM_LIMIT,
                                             has_side_effects=True),
    )(*args, *arrs)
    return outs[:no], list(outs[no:])


def _unshard(g, ax):
    g = jnp.moveaxis(g, 0, ax)
    sh = g.shape
    return g.reshape(sh[:ax] + (sh[ax] * sh[ax + 1],) + sh[ax + 2:])


def _to_parts(full, ax):
    sh = full.shape
    full = full.reshape(sh[:ax] + (N_DEV, sh[ax] // N_DEV) + sh[ax + 1:])
    return jnp.moveaxis(full, ax, 0)


def _row_tile(r, c):
    cap = max(8, (256 * 1024) // max(c, 1))
    best = None
    for d in range(8, min(r, cap) + 1, 8):
        if r % d == 0:
            best = d
    return r if best is None else best


def _adamw(w, m, v, parts, name):
    r, c = w.shape
    tr = _row_tile(r, c)

    def body(w_ref, m_ref, v_ref, p_ref, g_ref, d_ref, nm_ref, nv_ref):
        g = p_ref[0].astype(F32)
        for q in range(1, N_DEV):
            g = g + p_ref[q].astype(F32)
        nm = ADAM_B1 * m_ref[...] + (1.0 - ADAM_B1) * g
        nv = ADAM_B2 * v_ref[...] + (1.0 - ADAM_B2) * (g * g)
        m_hat = nm / (1.0 - ADAM_B1 ** ADAM_STEP)
        v_hat = nv / (1.0 - ADAM_B2 ** ADAM_STEP)
        g_ref[...] = g
        d_ref[...] = -ADAM_LR * (m_hat / (jnp.sqrt(v_hat) + ADAM_EPS) + ADAM_WD * w_ref[...])
        nm_ref[...] = nm
        nv_ref[...] = nv

    row = pl.BlockSpec((tr, c), lambda i: (i, 0))
    out = jax.ShapeDtypeStruct((r, c), F32)
    return pl.pallas_call(
        body, name=name, grid=(r // tr,),
        in_specs=[row, row, row, pl.BlockSpec((N_DEV, tr, c), lambda i: (0, i, 0))],
        out_specs=[row] * 4, out_shape=[out] * 4, compiler_params=_cparams(("parallel",)),
    )(w, m, v, parts)


WEIGHTS = ['ssd_in_w', 'ssd_conv_w', 'ssd_conv_b', 'ssd_dt_bias', 'ssd_a_log', 'ssd_d', 'ssd_norm_w', 'ssd_out_w',
           'mla_in_w', 'mla_q_norm_w', 'mla_q_up_w', 'mla_kv_norm_w', 'mla_kv_up_w', 'mla_out_w', 'gdn_in_w',
           'gdn_conv_w', 'gdn_a_log', 'gdn_dt_bias', 'gdn_norm_w', 'gdn_out_w', 'ln_g', 'ln_b']
SHARDED = {'ssd_in_w': (1, True), 'ssd_conv_w': (1, False), 'ssd_conv_b': (0, False), 'ssd_norm_w': (0, False),
           'ssd_out_w': (0, True), 'mla_in_w': (1, True), 'mla_q_up_w': (1, True), 'mla_kv_up_w': (1, True),
           'mla_out_w': (0, True), 'gdn_in_w': (1, True), 'gdn_conv_w': (1, False), 'gdn_out_w': (0, True)}
REPLICATED = [n for n in WEIGHTS if n not in SHARDED]


def _pack_small(vals):
    flat = jnp.concatenate([vals[n].reshape(-1).astype(F32) for n in REPLICATED])
    rows = -(-flat.shape[0] // (8 * LANES)) * 8
    return jnp.pad(flat, (0, rows * LANES - flat.shape[0])).reshape(rows, LANES)


def _unpack_small(slab, like):
    flat = slab.reshape(-1)
    out, off = {}, 0
    for n in REPLICATED:
        sz = like[n].size
        out[n] = flat[off:off + sz].reshape(like[n].shape)
        off += sz
    return out


def kernel(x, positions, ssd_in_w, ssd_conv_w, ssd_conv_b, ssd_dt_bias, ssd_a_log, ssd_d, ssd_norm_w, ssd_out_w, mla_in_w, mla_q_norm_w, mla_q_up_w, mla_kv_norm_w, mla_kv_up_w, mla_out_w, gdn_in_w, gdn_conv_w, gdn_a_log, gdn_dt_bias, gdn_norm_w, gdn_out_w, ln_g, ln_b, loss_target, m_ssd_in_w, m_ssd_conv_w, m_ssd_conv_b, m_ssd_dt_bias, m_ssd_a_log, m_ssd_d, m_ssd_norm_w, m_ssd_out_w, m_mla_in_w, m_mla_q_norm_w, m_mla_q_up_w, m_mla_kv_norm_w, m_mla_kv_up_w, m_mla_out_w, m_gdn_in_w, m_gdn_conv_w, m_gdn_a_log, m_gdn_dt_bias, m_gdn_norm_w, m_gdn_out_w, m_ln_g, m_ln_b, v_ssd_in_w, v_ssd_conv_w, v_ssd_conv_b, v_ssd_dt_bias, v_ssd_a_log, v_ssd_d, v_ssd_norm_w, v_ssd_out_w, v_mla_in_w, v_mla_q_norm_w, v_mla_q_up_w, v_mla_kv_norm_w, v_mla_kv_up_w, v_mla_out_w, v_gdn_in_w, v_gdn_conv_w, v_gdn_a_log, v_gdn_dt_bias, v_gdn_norm_w, v_gdn_out_w, v_ln_g, v_ln_b):
    loc = locals()
    w = {n: loc[n] for n in WEIGHTS}
    m = {n: loc["m_" + n] for n in WEIGHTS}
    v = {n: loc["v_" + n] for n in WEIGHTS}
    xs, pos, tgt = x[0], positions[0], loss_target[0]

    def names_of(prefix):
        return [n for n in SHARDED if n.startswith(prefix + "_")]

    def shards(prefix, j):
        return [w[n][j].astype(WIRE) if SHARDED[n][1] else w[n][j] for n in names_of(prefix)]

    def assemble(prefix, j, gathered):
        lw = {n[len(prefix) + 1:]: _unshard(g, SHARDED[n][0]) for n, g in zip(names_of(prefix), gathered)}
        lw.update({n[len(prefix) + 1:]: w[n][j] for n in REPLICATED if n.startswith(prefix + "_")})
        return lw

    def parts(prefix, g):
        return [_to_parts(g[n[len(prefix) + 1:]], SHARDED[n][0]).astype(WIRE if SHARDED[n][1] else F32)
                for n in names_of(prefix)]

    lg = lambda i: w["ln_g"][i].reshape(1, D_MODEL)
    lb = lambda i: w["ln_b"][i].reshape(1, D_MODEL)
    cosf, sinf = _rope_tables(pos)

    w_s0 = assemble("ssd", 0, _exchange(shards("ssd", 0), False, name="gather_ssd0"))
    h1, s0, got = _ssd_layer_fwd(xs, w_s0, lg(0), lb(0), "ssd0", ride=(shards("mla", 0), False))
    w_m0 = assemble("mla", 0, got)
    h2, s1, got = _mla_layer_fwd(h1, w_m0, lg(1), lb(1), cosf, sinf, "mla0", ride=(shards("gdn", 0), False))
    w_g0 = assemble("gdn", 0, got)
    h3, s2, got = _gdn_layer_fwd(h2, w_g0, lg(2), lb(2), "gdn0", ride=(shards("ssd", 1), False))
    w_s1 = assemble("ssd", 1, got)
    h4, s3, _ = _ssd_layer_fwd(h3, w_s1, lg(3), lb(3), "ssd1")
    loss_tile, dl = _loss_head(h4, tgt, name="loss_head")

    dr3, du3, g3, _ = _ssd_layer_bwd(s3, w_s1, lg(3), jnp.zeros_like(dl), dl, "ssd1")
    dr2, du2, g2, r3 = _gdn_layer_bwd(s2, w_g0, lg(2), dr3, du3, "gdn0", ride=(parts("ssd", g3), True))
    dr1, du1, g1, r2 = _mla_layer_bwd(s1, w_m0, lg(1), cosf, sinf, dr2, du2, "mla0", ride=(parts("gdn", g2), True))
    dr0, du0, g0, r1 = _ssd_layer_bwd(s0, w_s0, lg(0), dr1, du1, "ssd0", ride=(parts("mla", g1), True))
    grad_x = _axpy(dr0, du0, name="grad_x")[None]

    gsmall = {"ssd_" + k: jnp.stack([g0[k], g3[k]]) for k in ("dt_bias", "a_log", "d")}
    gsmall.update({"mla_" + k: g1[k][None] for k in ("q_norm_w", "kv_norm_w")})
    gsmall.update({"gdn_" + k: g2[k][None] for k in ("a_log", "dt_bias", "norm_w")})
    gsmall["ln_g"] = jnp.stack([g0["ln_g"], g1["ln_g"], g2["ln_g"], g3["ln_g"]])
    gsmall["ln_b"] = jnp.stack([g0["ln_b"], g1["ln_b"], g2["ln_b"], g3["ln_b"]])
    small = _pack_small(gsmall)
    r0 = _exchange(parts("ssd", g0) + [jnp.broadcast_to(small[None], (N_DEV,) + small.shape)], True, name="scatter_ssd0")

    recvd = {n: jnp.stack([a, b], axis=1) for n, a, b in zip(names_of("ssd"), r0[:-1], r3)}
    recvd.update({n: a[:, None] for n, a in zip(names_of("mla"), r1)})
    recvd.update({n: a[:, None] for n, a in zip(names_of("gdn"), r2)})
    recvd = [recvd[n] for n in SHARDED] + [r0[-1]]

    grads, deltas, new_m, new_v = {}, {}, {}, {}
    for n, pt in zip(SHARDED, recvd[:-1]):
        shp = w[n].shape
        r2d = (-1, shp[-1])
        outs = _adamw(w[n].reshape(r2d), m[n].reshape(r2d), v[n].reshape(r2d), pt.reshape((N_DEV,) + w[n].reshape(r2d).shape),
                      name="adamw_" + n)
        grads[n], deltas[n], new_m[n], new_v[n] = (o.reshape(shp) for o in outs)
    outs = _adamw(_pack_small(w), _pack_small(m), _pack_small(v), recvd[-1], name="adamw_replicated")
    for dst, o in zip((grads, deltas, new_m, new_v), outs):
        dst.update(_unpack_small(o, w))

    loss = lax.psum(loss_tile[0, 0], ("x", "y", "c"))
    return (loss, grad_x, *[grads[n] for n in WEIGHTS], *[deltas[n] for n in WEIGHTS],
            *[new_m[n] for n in WEIGHTS], *[new_v[n] for n in WEIGHTS])
```

```python
import functools

import jax
import jax.numpy as jnp
from jax import lax
from jax.experimental import pallas as pl
from jax.experimental.pallas import tpu as pltpu

F32 = jnp.float32
MXU = jnp.bfloat16
WIRE = jnp.bfloat16
HI = lax.Precision.HIGHEST

N_DEV = 8
LANES = 128
VMEM_LIMIT = 56 * 1024 * 1024

D_MODEL = 2048
DEPTH = 4
ALPHA = (2.0 * DEPTH) ** 0.25
LN_EPS = 1e-5
RMS_EPS = 1e-6

SSD_DI = 4096
SSD_P = 64
SSD_H = 64
SSD_G = 8
SSD_N = 128
SSD_L = 128
SSD_GS = SSD_DI // SSD_G
SSD_CONV_DIM = SSD_DI + 2 * SSD_G * SSD_N
SSD_PROJ_PAD = SSD_DI + SSD_CONV_DIM + LANES

MLA_H = 16
MLA_QR = 768
MLA_KVR = 512
MLA_NOPE = 128
MLA_ROPE = 64
MLA_V = 128
MLA_GATE = MLA_H * MLA_V
MLA_PROJ_PAD = MLA_QR + MLA_KVR + LANES + MLA_GATE
MLA_SCALE = (MLA_NOPE + MLA_ROPE) ** -0.5
ROPE_THETA = 10000.0
ATT_BLK = 512

GDN_HK = 16
GDN_HV = 32
GDN_DK = 128
GDN_DV = 128
GDN_KEY = GDN_HK * GDN_DK
GDN_VAL = GDN_HV * GDN_DV
GDN_L = 128
GDN_CONV_DIM = 2 * GDN_KEY + GDN_VAL
GDN_PROJ_PAD = 12800

ADAM_LR = 0.001
ADAM_B1 = 0.9
ADAM_B2 = 0.999
ADAM_EPS = 1e-08
ADAM_WD = 0.01
ADAM_STEP = 10


def _cparams(sem=None):
    return pltpu.CompilerParams(dimension_semantics=sem, vmem_limit_bytes=VMEM_LIMIT)


def _tile(n, cap):
    if n <= cap:
        return n
    best = None
    for d in range(LANES, cap + 1, LANES):
        if n % d == 0:
            best = d
    assert best is not None, (n, cap)
    return best


def _dg(a, b, ca, cb, prec=None):
    return lax.dot_general(a, b, (((ca,), (cb,)), ((), ())), preferred_element_type=F32, precision=prec)


def _mx(a):
    return a.astype(MXU)


def _hd(a, b):
    return _dg(a, b, 1, 0, HI)


@jax.custom_vjp
def _nn(a, b):
    return _dg(_mx(a), _mx(b), 1, 0)


def _nn_f(a, b):
    return _nn(a, b), (a, b)


def _nn_b(res, ct):
    a, b = res
    return _dg(_mx(ct), _mx(b), 1, 1), _dg(_mx(a), _mx(ct), 0, 0)


_nn.defvjp(_nn_f, _nn_b)


@jax.custom_vjp
def _nt(a, b):
    return _dg(_mx(a), _mx(b), 1, 1)


def _nt_f(a, b):
    return _nt(a, b), (a, b)


def _nt_b(res, ct):
    a, b = res
    return _dg(_mx(ct), _mx(b), 1, 0), _dg(_mx(ct), _mx(a), 0, 0)


_nt.defvjp(_nt_f, _nt_b)


@jax.custom_vjp
def _tn(a, b):
    return _dg(_mx(a), _mx(b), 0, 0)


def _tn_f(a, b):
    return _tn(a, b), (a, b)


def _tn_b(res, ct):
    a, b = res
    return _dg(_mx(b), _mx(ct), 1, 1), _dg(_mx(a), _mx(ct), 1, 0)


_tn.defvjp(_tn_f, _tn_b)


def _softplus(x):
    return jnp.maximum(x, 0.0) + jnp.log(1.0 + jnp.exp(-jnp.abs(x)))


def _silu(x):
    return x * jax.nn.sigmoid(x)


MM_TM = 1024
MM_TN = 1280
MM_VMEM_BUDGET = 40 * 1024 * 1024


def _mm(a, b, *, ta=False, tb=False, out_dtype=F32, name, ride=None):
    if ta:
        kdim, m = a.shape
    else:
        m, kdim = a.shape
    if tb:
        n, kb = b.shape
    else:
        kb, n = b.shape
    assert kdim == kb, (a.shape, b.shape, ta, tb)
    tm, tn = _tile(m, MM_TM), _tile(n, MM_TN)
    abytes, bbytes, obytes = a.dtype.itemsize, b.dtype.itemsize, jnp.dtype(out_dtype).itemsize
    tk = LANES
    for d in range(LANES, kdim + 1, LANES):
        if kdim % d == 0 and 2 * d * (tm * abytes + tn * bbytes) + tm * tn * (2 * obytes + 4) <= MM_VMEM_BUDGET:
            tk = d
    nk = kdim // tk
    ca, cb = (0 if ta else 1), (1 if tb else 0)

    def body(a_ref, b_ref, o_ref, *acc):
        part = _dg(_mx(a_ref[...]), _mx(b_ref[...]), ca, cb)
        if nk == 1:
            o_ref[...] = part.astype(out_dtype)
            return
        acc_ref, = acc
        k = pl.program_id(2)

        @pl.when(k == 0)
        def _():
            acc_ref[...] = part

        @pl.when(k > 0)
        def _():
            acc_ref[...] += part

        @pl.when(k == nk - 1)
        def _():
            o_ref[...] = acc_ref[...].astype(out_dtype)

    a_spec = pl.BlockSpec((tk, tm), lambda i, j, k: (k, i)) if ta else pl.BlockSpec((tm, tk), lambda i, j, k: (i, k))
    b_spec = pl.BlockSpec((tn, tk), lambda i, j, k: (j, k)) if tb else pl.BlockSpec((tk, tn), lambda i, j, k: (k, j))
    (out,), rode = _hosted_call(
        body, name=name, grid=(m // tm, n // tn, nk),
        in_specs=[a_spec, b_spec], out_specs=[pl.BlockSpec((tm, tn), lambda i, j, k: (i, j))],
        out_shape=[jax.ShapeDtypeStruct((m, n), out_dtype)],
        scratch_shapes=[pltpu.VMEM((tm, tn), F32)] if nk > 1 else [],
        sem=("parallel", "parallel", "arbitrary"), args=(a, b), ride=ride)
    return out if ride is None else (out, rode)


def _ln_fwd(h, y, g, b, name):
    t, d = h.shape
    tr = _tile(t, 256)

    def body(h_ref, y_ref, g_ref, b_ref, o_ref):
        r = ALPHA * h_ref[...] + y_ref[...]
        mu = jnp.mean(r, -1, keepdims=True)
        xc = r - mu
        var = jnp.mean(xc * xc, -1, keepdims=True)
        o_ref[...] = xc * lax.rsqrt(var + LN_EPS) * g_ref[...] + b_ref[...]

    row = pl.BlockSpec((tr, d), lambda i: (i, 0))
    par = pl.BlockSpec((1, d), lambda i: (0, 0))
    return pl.pallas_call(
        body, name=name, grid=(t // tr,), in_specs=[row, row, par, par], out_specs=row,
        out_shape=jax.ShapeDtypeStruct((t, d), F32), compiler_params=_cparams(("parallel",)),
    )(h, y, g, b)


def _ln_bwd(h, y, g, dr_up, du_up, name):
    t, d = h.shape
    tr = _tile(t, 256)

    def body(h_ref, y_ref, g_ref, dr_ref, du_ref, o_ref, dg_ref, db_ref):
        i = pl.program_id(0)

        @pl.when(i == 0)
        def _():
            dg_ref[...] = jnp.zeros_like(dg_ref)
            db_ref[...] = jnp.zeros_like(db_ref)

        dout = ALPHA * dr_ref[...] + du_ref[...]
        r = ALPHA * h_ref[...] + y_ref[...]
        mu = jnp.mean(r, -1, keepdims=True)
        xc = r - mu
        rstd = lax.rsqrt(jnp.mean(xc * xc, -1, keepdims=True) + LN_EPS)
        xh = xc * rstd
        dxh = dout * g_ref[...]
        o_ref[...] = rstd * (dxh - jnp.mean(dxh, -1, keepdims=True) - xh * jnp.mean(dxh * xh, -1, keepdims=True))
        dg_ref[...] += jnp.sum(dout * xh, 0, keepdims=True)
        db_ref[...] += jnp.sum(dout, 0, keepdims=True)

    row = pl.BlockSpec((tr, d), lambda i: (i, 0))
    par = pl.BlockSpec((1, d), lambda i: (0, 0))
    return pl.pallas_call(
        body, name=name, grid=(t // tr,), in_specs=[row, row, par, row, row], out_specs=[row, par, par],
        out_shape=[jax.ShapeDtypeStruct((t, d), F32), jax.ShapeDtypeStruct((1, d), F32), jax.ShapeDtypeStruct((1, d), F32)],
        compiler_params=_cparams(("arbitrary",)),
    )(h, y, g, dr_up, du_up)


def _loss_head(h, tgt, name):
    t, d = h.shape
    tr = _tile(t, 256)

    def body(h_ref, t_ref, l_ref, d_ref):
        i = pl.program_id(0)

        @pl.when(i == 0)
        def _():
            l_ref[...] = jnp.zeros_like(l_ref)

        e = h_ref[...] - t_ref[...]
        d_ref[...] = e * (1.0 / d)
        l_ref[...] += 0.5 * jnp.sum(jnp.mean(e * e, -1, keepdims=True))

    row = pl.BlockSpec((tr, d), lambda i: (i, 0))
    return pl.pallas_call(
        body, name=name, grid=(t // tr,), in_specs=[row, row],
        out_specs=[pl.BlockSpec((8, LANES), lambda i: (0, 0)), row],
        out_shape=[jax.ShapeDtypeStruct((8, LANES), F32), jax.ShapeDtypeStruct((t, d), F32)],
        compiler_params=_cparams(("arbitrary",)),
    )(h, tgt)


def _axpy(dr, du, name):
    t, d = dr.shape
    tr = _tile(t, 256)

    def body(a_ref, b_ref, o_ref):
        o_ref[...] = ALPHA * a_ref[...] + b_ref[...]

    row = pl.BlockSpec((tr, d), lambda i: (i, 0))
    return pl.pallas_call(
        body, name=name, grid=(t // tr,), in_specs=[row, row], out_specs=row,
        out_shape=jax.ShapeDtypeStruct((t, d), F32), compiler_params=_cparams(("parallel",)),
    )(dr, du)


CONV_TT = 512
CONV_TC = 512


def _shift_down(cur, prev, s, row):
    if s == 0:
        return cur
    return jnp.where(row >= s, pltpu.roll(cur, s, 0), pltpu.roll(prev, s, 0))


def _shift_up(cur, nxt, s, row, tt):
    if s == 0:
        return cur
    return jnp.where(row < tt - s, pltpu.roll(cur, tt - s, 0), pltpu.roll(nxt, tt - s, 0))


def _conv_fwd(proj, col0, w, b, name):
    t = proj.shape[0]
    c = w.shape[1]
    tt = _tile(t, CONV_TT)
    cb0 = col0 // CONV_TC

    def body(x_ref, p_ref, w_ref, b_ref, o_ref):
        i = pl.program_id(1)
        x = x_ref[...]
        p = jnp.where(i > 0, p_ref[...], 0.0)
        row = lax.broadcasted_iota(jnp.int32, x.shape, 0)
        pre = b_ref[...] + w_ref[3:4, :] * x
        for s in (1, 2, 3):
            pre = pre + w_ref[3 - s:4 - s, :] * _shift_down(x, p, s, row)
        o_ref[...] = _silu(pre)

    return pl.pallas_call(
        body, name=name, grid=(c // CONV_TC, t // tt),
        in_specs=[pl.BlockSpec((tt, CONV_TC), lambda j, i: (i, cb0 + j)),
                  pl.BlockSpec((tt, CONV_TC), lambda j, i: (jnp.maximum(i - 1, 0), cb0 + j)),
                  pl.BlockSpec((4, CONV_TC), lambda j, i: (0, j)),
                  pl.BlockSpec((1, CONV_TC), lambda j, i: (0, j))],
        out_specs=pl.BlockSpec((tt, CONV_TC), lambda j, i: (i, j)),
        out_shape=jax.ShapeDtypeStruct((t, c), F32), compiler_params=_cparams(("parallel", "parallel")),
    )(proj, proj, w, b)


def _conv_bwd_pre(proj, col0, w, b, dact, name):
    t = proj.shape[0]
    c = w.shape[1]
    tt = _tile(t, CONV_TT)
    cb0 = col0 // CONV_TC

    def body(x_ref, p_ref, w_ref, b_ref, d_ref, dpre_ref, dw_ref, db_ref):
        i = pl.program_id(1)

        @pl.when(i == 0)
        def _():
            dw_ref[...] = jnp.zeros_like(dw_ref)
            db_ref[...] = jnp.zeros_like(db_ref)

        x = x_ref[...]
        p = jnp.where(i > 0, p_ref[...], 0.0)
        row = lax.broadcasted_iota(jnp.int32, x.shape, 0)
        sh = [_shift_down(x, p, s, row) for s in range(4)]
        pre = b_ref[...] + w_ref[3:4, :] * sh[0]
        for s in (1, 2, 3):
            pre = pre + w_ref[3 - s:4 - s, :] * sh[s]
        sg = jax.nn.sigmoid(pre)
        dpre = d_ref[...] * (sg * (1.0 + pre * (1.0 - sg)))
        dpre_ref[...] = dpre
        for s in range(4):
            dw_ref[3 - s:4 - s, :] += jnp.sum(dpre * sh[s], 0, keepdims=True)
        db_ref[...] += jnp.sum(dpre, 0, keepdims=True)

    return pl.pallas_call(
        body, name=name, grid=(c // CONV_TC, t // tt),
        in_specs=[pl.BlockSpec((tt, CONV_TC), lambda j, i: (i, cb0 + j)),
                  pl.BlockSpec((tt, CONV_TC), lambda j, i: (jnp.maximum(i - 1, 0), cb0 + j)),
                  pl.BlockSpec((4, CONV_TC), lambda j, i: (0, j)),
                  pl.BlockSpec((1, CONV_TC), lambda j, i: (0, j)),
                  pl.BlockSpec((tt, CONV_TC), lambda j, i: (i, j))],
        out_specs=[pl.BlockSpec((tt, CONV_TC), lambda j, i: (i, j)),
                   pl.BlockSpec((4, CONV_TC), lambda j, i: (0, j)),
                   pl.BlockSpec((1, CONV_TC), lambda j, i: (0, j))],
        out_shape=[jax.ShapeDtypeStruct((t, c), F32), jax.ShapeDtypeStruct((4, c), F32), jax.ShapeDtypeStruct((1, c), F32)],
        compiler_params=_cparams(("parallel", "arbitrary")),
    )(proj, proj, w, b, dact)


def _conv_bwd_x(dpre, w, name):
    t, c = dpre.shape
    tt = _tile(t, CONV_TT)
    nt = t // tt

    def body(d_ref, n_ref, w_ref, o_ref):
        i = pl.program_id(1)
        d = d_ref[...]
        nx = jnp.where(i < nt - 1, n_ref[...], 0.0)
        row = lax.broadcasted_iota(jnp.int32, d.shape, 0)
        acc = w_ref[3:4, :] * d
        for s in (1, 2, 3):
            acc = acc + w_ref[3 - s:4 - s, :] * _shift_up(d, nx, s, row, tt)
        o_ref[...] = acc

    return pl.pallas_call(
        body, name=name, grid=(c // CONV_TC, nt),
        in_specs=[pl.BlockSpec((tt, CONV_TC), lambda j, i: (i, j)),
                  pl.BlockSpec((tt, CONV_TC), lambda j, i: (jnp.minimum(i + 1, nt - 1), j)),
                  pl.BlockSpec((4, CONV_TC), lambda j, i: (0, j))],
        out_specs=pl.BlockSpec((tt, CONV_TC), lambda j, i: (i, j)),
        out_shape=jax.ShapeDtypeStruct((t, c), F32), compiler_params=_cparams(("parallel", "parallel")),
    )(dpre, dpre, w)


def _ssd_chunk(g, x, z, bm, cm, dtraw, dtb, alog, dsk, nw, prev):
    L = x.shape[0]
    r_i = lax.broadcasted_iota(jnp.int32, (L, L), 0)
    c_i = lax.broadcasted_iota(jnp.int32, (L, L), 1)
    causal = r_i >= c_i
    dt = _softplus(dtraw + dtb)
    a = dt * (-jnp.exp(alog))
    acs = _hd(causal.astype(F32), a)
    e_r = lax.broadcasted_iota(jnp.int32, (LANES, SSD_GS), 0)
    e_c = lax.broadcasted_iota(jnp.int32, (LANES, SSD_GS), 1)
    sel = (e_r == g * (SSD_H // SSD_G) + jnp.right_shift(e_c, 6)).astype(F32)
    dt_x = _hd(dt, sel)
    acs_x = _hd(acs, sel)
    d_x = _hd(jnp.broadcast_to(dsk, (8, LANES)), sel)[0:1]
    alast = acs_x[L - 1:L]
    xdt = x * dt_x
    cb = _nt(cm, bm)
    lane = lax.broadcasted_iota(jnp.int32, (L, LANES), 1)
    ys = []
    for j in range(SSD_GS // LANES):
        xp = xdt[:, j * LANES:(j + 1) * LANES]
        yp = None
        for hh in range(2):
            c0 = (2 * j + hh) * SSD_P
            cmx = jnp.broadcast_to(acs_x[:, c0:c0 + 1], (L, L))
            dec = jnp.exp(jnp.where(causal, cmx - cmx.T, -jnp.inf))
            half = (lane < SSD_P) if hh == 0 else (lane >= SSD_P)
            t = _nn(cb * dec, jnp.where(half, xp, 0.0))
            yp = t if yp is None else yp + t
        ys.append(yp)
    y_diag = jnp.concatenate(ys, axis=1)
    st = _tn(bm, xdt * jnp.exp(alast - acs_x))
    new = prev * jnp.exp(alast) + st
    y_off = _nn(cm, prev) * jnp.exp(acs_x)
    y = y_diag + y_off + x * d_x
    yg = y * _silu(z)
    yn = yg * lax.rsqrt(jnp.mean(yg * yg, -1, keepdims=True) + RMS_EPS) * nw
    return yn, new


def _ssd_specs(nc, rev):
    cc = (lambda c: nc - 1 - c) if rev else (lambda c: c)
    zb = 0
    dtb = (SSD_DI + SSD_CONV_DIM) // LANES
    bb = SSD_DI // LANES
    cbk = (SSD_DI + SSD_G * SSD_N) // LANES
    par = pl.BlockSpec((1, LANES), lambda c, g: (0, 0))
    return dict(
        z=pl.BlockSpec((SSD_L, SSD_GS), lambda c, g: (cc(c), zb + g)),
        dt=pl.BlockSpec((SSD_L, LANES), lambda c, g: (cc(c), dtb)),
        x=pl.BlockSpec((SSD_L, SSD_GS), lambda c, g: (cc(c), g)),
        bm=pl.BlockSpec((SSD_L, LANES), lambda c, g: (cc(c), bb + g)),
        cm=pl.BlockSpec((SSD_L, LANES), lambda c, g: (cc(c), cbk + g)),
        par=par,
        nw=pl.BlockSpec((1, SSD_GS), lambda c, g: (0, g)),
        st=pl.BlockSpec((1, 1, SSD_N, SSD_GS), lambda c, g: (cc(c), g, 0, 0)),
        y=pl.BlockSpec((SSD_L, SSD_GS), lambda c, g: (cc(c), g)),
        bc=pl.BlockSpec((SSD_L, LANES), lambda c, g: (cc(c), g)),
        dtout=pl.BlockSpec((SSD_L, LANES), lambda c, g: (cc(c), 0)),
    )


def _ssd_fwd(proj, act, dtb, alog, dsk, nw, name):
    t = proj.shape[0]
    nc = t // SSD_L
    sp = _ssd_specs(nc, False)

    def body(z_ref, dt_ref, x_ref, bm_ref, cm_ref, dtb_ref, alog_ref, dsk_ref, nw_ref, y_ref, st_ref, state):
        c, g = pl.program_id(0), pl.program_id(1)

        @pl.when(c == 0)
        def _():
            state[g] = jnp.zeros((SSD_N, SSD_GS), F32)

        prev = state[g]
        st_ref[0, 0] = prev
        yn, new = _ssd_chunk(g, x_ref[...], z_ref[...], bm_ref[...], cm_ref[...], dt_ref[...], dtb_ref[...],
                             alog_ref[...], dsk_ref[...], nw_ref[...], prev)
        y_ref[...] = yn.astype(y_ref.dtype)
        state[g] = new

    return pl.pallas_call(
        body, name=name, grid=(nc, SSD_G),
        in_specs=[sp["z"], sp["dt"], sp["x"], sp["bm"], sp["cm"], sp["par"], sp["par"], sp["par"], sp["nw"]],
        out_specs=[sp["y"], sp["st"]],
        out_shape=[jax.ShapeDtypeStruct((t, SSD_DI), F32), jax.ShapeDtypeStruct((nc, SSD_G, SSD_N, SSD_GS), F32)],
        scratch_shapes=[pltpu.VMEM((SSD_G, SSD_N, SSD_GS), F32)],
        compiler_params=_cparams(("arbitrary", "arbitrary")),
    )(proj, proj, act, act, act, dtb, alog, dsk, nw)


def _ssd_bwd(proj, act, dtb, alog, dsk, nw, states, dyn, name):
    t = proj.shape[0]
    nc = t // SSD_L
    sp = _ssd_specs(nc, True)

    def body(z_ref, dt_ref, x_ref, bm_ref, cm_ref, dtb_ref, alog_ref, dsk_ref, nw_ref, st_ref, dy_ref,
             dx_ref, dz_ref, dbm_ref, dcm_ref, ddt_ref, ddtb_ref, dalog_ref, ddsk_ref, dnw_ref, dstate):
        c, g = pl.program_id(0), pl.program_id(1)

        @pl.when(c == 0)
        def _():
            dstate[g] = jnp.zeros((SSD_N, SSD_GS), F32)

        @pl.when((c == 0) & (g == 0))
        def _():
            ddtb_ref[...] = jnp.zeros_like(ddtb_ref)
            dalog_ref[...] = jnp.zeros_like(dalog_ref)
            ddsk_ref[...] = jnp.zeros_like(ddsk_ref)
            dnw_ref[...] = jnp.zeros_like(dnw_ref)

        @pl.when(g == 0)
        def _():
            ddt_ref[...] = jnp.zeros_like(ddt_ref)

        _, vjp = jax.vjp(functools.partial(_ssd_chunk, g), x_ref[...], z_ref[...], bm_ref[...], cm_ref[...],
                         dt_ref[...], dtb_ref[...], alog_ref[...], dsk_ref[...], nw_ref[...], st_ref[0, 0])
        dx, dz, dbm, dcm, ddt, ddtb, dalog, ddsk, dnw, dprev = vjp((dy_ref[...].astype(F32), dstate[g]))
        dx_ref[...] = dx
        dz_ref[...] = dz
        dbm_ref[...] = dbm
        dcm_ref[...] = dcm
        ddt_ref[...] += ddt
        ddtb_ref[...] += ddtb
        dalog_ref[...] += dalog
        ddsk_ref[...] += ddsk
        dnw_ref[g] += dnw
        dstate[g] = dprev

    par_out = pl.BlockSpec((1, LANES), lambda c, g: (0, 0))
    sds = jax.ShapeDtypeStruct
    return pl.pallas_call(
        body, name=name, grid=(nc, SSD_G),
        in_specs=[sp["z"], sp["dt"], sp["x"], sp["bm"], sp["cm"], sp["par"], sp["par"], sp["par"], sp["nw"],
                  sp["st"], sp["y"]],
        out_specs=[sp["y"], sp["y"], sp["bc"], sp["bc"], sp["dtout"], par_out, par_out, par_out,
                   pl.BlockSpec((SSD_G, 1, SSD_GS), lambda c, g: (0, 0, 0))],
        out_shape=[sds((t, SSD_DI), F32), sds((t, SSD_DI), F32), sds((t, SSD_G * SSD_N), F32), sds((t, SSD_G * SSD_N), F32),
                   sds((t, LANES), F32), sds((1, LANES), F32), sds((1, LANES), F32), sds((1, LANES), F32),
                   sds((SSD_G, 1, SSD_GS), F32)],
        scratch_shapes=[pltpu.VMEM((SSD_G, SSD_N, SSD_GS), F32)],
        compiler_params=_cparams(("arbitrary", "arbitrary")),
    )(proj, proj, act, act, act, dtb, alog, dsk, nw, states, dyn)


def _pad_lanes(v, width=LANES, offset=0):
    return jnp.pad(v.astype(F32), (offset, width - offset - v.shape[0])).reshape(1, width)


def _ssd_layer_fwd(u, w, ln_g, ln_b, tag, ride=None, late=None):
    w_in = jnp.concatenate([w["in_w"], jnp.zeros((D_MODEL, LANES - SSD_H), w["in_w"].dtype)], axis=1)
    dtb, alog, dsk = _pad_lanes(w["dt_bias"]), _pad_lanes(w["a_log"]), _pad_lanes(w["d"])
    proj = _mm(u, w_in, name=tag + "_in", ride=ride)
    proj, rode = (proj, []) if ride is None else proj
    if late is not None:
        more, rode = late(rode)
        w.update(more)
    nw = w["norm_w"].reshape(1, SSD_DI)
    cb = w["conv_b"].reshape(1, SSD_CONV_DIM)
    act = _conv_fwd(proj, SSD_DI, w["conv_w"], cb, name=tag + "_conv")
    yn, states = _ssd_fwd(proj, act, dtb, alog, dsk, nw, name=tag + "_scan")
    y = _mm(yn, w["out_w"], name=tag + "_out")
    h = _ln_fwd(u, y, ln_g, ln_b, name=tag + "_ln")
    saved = dict(u=u, w_in=w_in, proj=proj, act=act, states=states, yn=yn, y=y, dtb=dtb, alog=alog, dsk=dsk, nw=nw, cb=cb)
    return h, saved, rode


def _ssd_layer_bwd(s, w, ln_g, dr_up, du_up, tag, ride=None, own=None):
    dr, dg, db = _ln_bwd(s["u"], s["y"], ln_g, dr_up, du_up, name=tag + "_ln_b")
    dyn = _mm(dr, w["out_w"], tb=True, name=tag + "_out_bx")
    d_out_w = _mm(s["yn"], dr, ta=True, out_dtype=WIRE, name=tag + "_out_bw")
    dx, dz, dbm, dcm, ddt, ddtb, dalog, ddsk, dnw = _ssd_bwd(
        s["proj"], s["act"], s["dtb"], s["alog"], s["dsk"], s["nw"], s["states"], dyn, name=tag + "_scan_b")
    dact = jnp.concatenate([dx, dbm, dcm], axis=1)
    dpre, d_conv_w, d_conv_b = _conv_bwd_pre(s["proj"], SSD_DI, w["conv_w"], s["cb"], dact, name=tag + "_conv_bp")
    dxbc = _conv_bwd_x(dpre, w["conv_w"], name=tag + "_conv_bx")
    dproj = jnp.concatenate([dz, dxbc, ddt], axis=1)
    grads = dict(conv_w=d_conv_w, conv_b=d_conv_b.reshape(-1), dt_bias=ddtb[0, :SSD_H], a_log=dalog[0, :SSD_H],
                 d=ddsk[0, :SSD_H], norm_w=dnw.reshape(-1), out_w=d_out_w, ln_g=dg[0], ln_b=db[0])
    late_names = ("conv_w", "conv_b", "norm_w", "out_w")
    if own is not None:
        ride = ((ride[0] if ride else []) + [own(k, grads[k]) for k in late_names], True)
    d_in_w = _mm(s["u"], dproj, ta=True, out_dtype=WIRE, name=tag + "_in_bw", ride=ride)
    d_in_w, rode = (d_in_w, []) if ride is None else d_in_w
    grads["in_w"] = d_in_w[:, :SSD_DI + SSD_CONV_DIM + SSD_H]
    got = {}
    if own is not None:
        got = dict(zip(late_names, rode[len(rode) - len(late_names):]))
        rode = rode[:len(rode) - len(late_names)]
        du, (got["in_w"],) = _mm(dproj, s["w_in"], tb=True, name=tag + "_in_bx", ride=([own("in_w", grads["in_w"])], True))
    else:
        du = _mm(dproj, s["w_in"], tb=True, name=tag + "_in_bx")
    return dr, du, grads, rode, got


MLA_LOW = MLA_QR + MLA_KVR + LANES
MLA_ZB = MLA_LOW // LANES


def _rope_mat():
    r = lax.broadcasted_iota(jnp.int32, (LANES, LANES), 0)
    c = lax.broadcasted_iota(jnp.int32, (LANES, LANES), 1)
    hf = MLA_ROPE // 2
    return jnp.where((c < hf) & (r == c + hf), -1.0, 0.0) + jnp.where((c >= hf) & (c < 2 * hf) & (r == c - hf), 1.0, 0.0)


def _rope(x, cosf, sinf):
    return x * cosf + _hd(x, _rope_mat()) * sinf


def _rope_adj(d, cosf, sinf):
    return d * cosf - _hd(d * sinf, _rope_mat())


def _mla_low_fn(low, qnw, kvnw, cosf, sinf):
    qc, kvc, kr = low[:, :MLA_QR], low[:, MLA_QR:MLA_QR + MLA_KVR], low[:, MLA_QR + MLA_KVR:]
    qn = qc * lax.rsqrt(jnp.mean(qc * qc, -1, keepdims=True) + RMS_EPS) * qnw
    kvn = kvc * lax.rsqrt(jnp.mean(kvc * kvc, -1, keepdims=True) + RMS_EPS) * kvnw
    return qn, kvn, _rope(kr, cosf, sinf)


def _mla_low_fwd(proj, qnw, kvnw, cosf, sinf, name):
    t = proj.shape[0]
    tr = _tile(t, 256)

    def body(low_ref, qnw_ref, kvnw_ref, cos_ref, sin_ref, qn_ref, kvn_ref, kr_ref):
        qn, kvn, kr = _mla_low_fn(low_ref[...], qnw_ref[...], kvnw_ref[...], cos_ref[...], sin_ref[...])
        qn_ref[...] = qn
        kvn_ref[...] = kvn
        kr_ref[...] = kr

    row = lambda wdt: pl.BlockSpec((tr, wdt), lambda i: (i, 0))
    par = lambda wdt: pl.BlockSpec((1, wdt), lambda i: (0, 0))
    sds = jax.ShapeDtypeStruct
    return pl.pallas_call(
        body, name=name, grid=(t // tr,),
        in_specs=[row(MLA_LOW), par(MLA_QR), par(MLA_KVR), row(LANES), row(LANES)],
        out_specs=[row(MLA_QR), row(MLA_KVR), row(LANES)],
        out_shape=[sds((t, MLA_QR), F32), sds((t, MLA_KVR), F32), sds((t, LANES), F32)],
        compiler_params=_cparams(("parallel",)),
    )(proj, qnw, kvnw, cosf, sinf)


def _mla_low_bwd(proj, qnw, kvnw, cosf, sinf, dqn, dkvn, dkr, name):
    t = proj.shape[0]
    tr = _tile(t, 256)

    def body(low_ref, qnw_ref, kvnw_ref, cos_ref, sin_ref, dqn_ref, dkvn_ref, dkr_ref, dlow_ref, dqnw_ref, dkvnw_ref):
        i = pl.program_id(0)

        @pl.when(i == 0)
        def _():
            dqnw_ref[...] = jnp.zeros_like(dqnw_ref)
            dkvnw_ref[...] = jnp.zeros_like(dkvnw_ref)

        cosf, sinf = cos_ref[...], sin_ref[...]
        _, vjp = jax.vjp(lambda a, b, c: _mla_low_fn(a, b, c, cosf, sinf), low_ref[...], qnw_ref[...], kvnw_ref[...])
        dlow, dq, dk = vjp((dqn_ref[...], dkvn_ref[...], dkr_ref[...]))
        dlow_ref[...] = dlow
        dqnw_ref[...] += dq
        dkvnw_ref[...] += dk

    row = lambda wdt: pl.BlockSpec((tr, wdt), lambda i: (i, 0))
    par = lambda wdt: pl.BlockSpec((1, wdt), lambda i: (0, 0))
    sds = jax.ShapeDtypeStruct
    return pl.pallas_call(
        body, name=name, grid=(t // tr,),
        in_specs=[row(MLA_LOW), par(MLA_QR), par(MLA_KVR), row(LANES), row(LANES), row(MLA_QR), row(MLA_KVR), row(LANES)],
        out_specs=[row(MLA_LOW), par(MLA_QR), par(MLA_KVR)],
        out_shape=[sds((t, MLA_LOW), F32), sds((1, MLA_QR), F32), sds((1, MLA_KVR), F32)],
        compiler_params=_cparams(("arbitrary",)),
    )(proj, qnw, kvnw, cosf, sinf, dqn, dkvn, dkr)


def _rope_heads(x, col_blk0, cosf, sinf, adjoint, name):
    t = x.shape[0]
    tr = _tile(t, 512)

    def body(x_ref, cos_ref, sin_ref, o_ref):
        f = _rope_adj if adjoint else _rope
        o_ref[...] = f(x_ref[...], cos_ref[...], sin_ref[...])

    tab = pl.BlockSpec((tr, LANES), lambda i, h: (i, 0))
    return pl.pallas_call(
        body, name=name, grid=(t // tr, MLA_H),
        in_specs=[pl.BlockSpec((tr, LANES), lambda i, h: (i, col_blk0 + h)), tab, tab],
        out_specs=pl.BlockSpec((tr, LANES), lambda i, h: (i, h)),
        out_shape=jax.ShapeDtypeStruct((t, MLA_H * LANES), F32), compiler_params=_cparams(("parallel", "parallel")),
    )(x, cosf, sinf)


def _att_scores(qn, qr, kn, kr, i, j, tq, tk):
    s = (_dg(_mx(qn), _mx(kn), 1, 1) + _dg(_mx(qr), _mx(kr), 1, 1)) * MLA_SCALE
    qpos = i * tq + lax.broadcasted_iota(jnp.int32, (tq, tk), 0)
    kpos = j * tk + lax.broadcasted_iota(jnp.int32, (tq, tk), 1)
    return jnp.where(kpos <= qpos, s, -jnp.inf)


def _attn_fwd(q, qr, kv, kr, proj, name, ride=None):
    t = q.shape[0]
    tq = tk = _tile(t, ATT_BLK)
    nq = nk = t // tq

    def body(qn_ref, qr_ref, kn_ref, kr_ref, v_ref, z_ref, o_ref, og_ref, lse_ref, m_s, l_s, acc_s):
        i, j = pl.program_id(1), pl.program_id(2)

        @pl.when(j == 0)
        def _():
            m_s[...] = jnp.full_like(m_s, -jnp.inf)
            l_s[...] = jnp.zeros_like(l_s)
            acc_s[...] = jnp.zeros_like(acc_s)

        @pl.when(j <= i)
        def _():
            s = _att_scores(qn_ref[...], qr_ref[...], kn_ref[...], kr_ref[...], i, j, tq, tk)
            m_new = jnp.maximum(m_s[...], jnp.max(s, -1, keepdims=True))
            p = jnp.exp(s - m_new)
            corr = jnp.exp(m_s[...] - m_new)
            l_s[...] = corr * l_s[...] + jnp.sum(p, -1, keepdims=True)
            acc_s[...] = corr * acc_s[...] + _dg(_mx(p), _mx(v_ref[...]), 1, 0)
            m_s[...] = m_new

        @pl.when(j == nk - 1)
        def _():
            o = acc_s[...] / l_s[...]
            o_ref[...] = o
            og_ref[...] = (o * _silu(z_ref[...])).astype(og_ref.dtype)
            lse_ref[...] = jnp.broadcast_to(m_s[...] + jnp.log(l_s[...]), (tq, LANES))

    qs = lambda off: pl.BlockSpec((tq, LANES), lambda h, i, j: (i, off + h))
    ks = lambda off: pl.BlockSpec((tk, LANES), lambda h, i, j: (jnp.minimum(j, i), off + h))
    sds = jax.ShapeDtypeStruct
    return _hosted_call(
        body, name=name, grid=(MLA_H, nq, nk),
        in_specs=[qs(0), qs(0), ks(0), pl.BlockSpec((tk, LANES), lambda h, i, j: (jnp.minimum(j, i), 0)), ks(MLA_H), qs(MLA_ZB)],
        out_specs=[qs(0), qs(0), qs(0)],
        out_shape=[sds((t, MLA_GATE), F32), sds((t, MLA_GATE), F32), sds((t, MLA_H * LANES), F32)],
        scratch_shapes=[pltpu.VMEM((tq, 1), F32), pltpu.VMEM((tq, 1), F32), pltpu.VMEM((tq, LANES), F32)],
        sem=("parallel", "parallel", "arbitrary"), args=(q, qr, kv, kr, kv, proj), ride=ride)


def _gate_bwd(dog, o, proj, name):
    t = o.shape[0]
    tr = _tile(t, 512)

    def body(d_ref, o_ref, z_ref, do_ref, dz_ref):
        z = z_ref[...]
        sg = jax.nn.sigmoid(z)
        d = d_ref[...]
        do_ref[...] = d * z * sg
        dz_ref[...] = d * o_ref[...] * (sg * (1.0 + z * (1.0 - sg)))

    blk = lambda off: pl.BlockSpec((tr, 512), lambda i, j: (i, off + j))
    assert MLA_LOW % 512 != 0 or True
    zspec = pl.BlockSpec((tr, LANES), lambda i, j: (i, MLA_ZB + j))
    b128 = pl.BlockSpec((tr, LANES), lambda i, j: (i, j))
    sds = jax.ShapeDtypeStruct
    return pl.pallas_call(
        body, name=name, grid=(t // tr, MLA_GATE // LANES),
        in_specs=[b128, b128, zspec], out_specs=[b128, b128],
        out_shape=[sds((t, MLA_GATE), F32), sds((t, MLA_GATE), F32)],
        compiler_params=_cparams(("parallel", "parallel")),
    )(dog, o, proj)


def _attn_bwd_q(q, qr, kv, kr, o, do, lse, name, ride=None):
    t = q.shape[0]
    tq = tk = _tile(t, ATT_BLK)
    nq = nk = t // tq

    def body(qn_ref, qr_ref, kn_ref, kr_ref, v_ref, o_ref, do_ref, lse_ref, dqn_ref, dqr_ref, an_s, ar_s):
        i, j = pl.program_id(1), pl.program_id(2)

        @pl.when(j == 0)
        def _():
            an_s[...] = jnp.zeros_like(an_s)
            ar_s[...] = jnp.zeros_like(ar_s)

        @pl.when(j <= i)
        def _():
            s = _att_scores(qn_ref[...], qr_ref[...], kn_ref[...], kr_ref[...], i, j, tq, tk)
            p = jnp.exp(s - lse_ref[:, 0:1])
            do = do_ref[...]
            dp = _dg(_mx(do), _mx(v_ref[...]), 1, 1)
            dl = jnp.sum(do * o_ref[...], -1, keepdims=True)
            ds = _mx(p * (dp - dl) * MLA_SCALE)
            an_s[...] += _dg(ds, _mx(kn_ref[...]), 1, 0)
            ar_s[...] += _dg(ds, _mx(kr_ref[...]), 1, 0)

        @pl.when(j == nk - 1)
        def _():
            dqn_ref[...] = an_s[...]
            dqr_ref[...] = ar_s[...]

    qs = lambda off: pl.BlockSpec((tq, LANES), lambda h, i, j: (i, off + h))
    ks = lambda off: pl.BlockSpec((tk, LANES), lambda h, i, j: (jnp.minimum(j, i), off + h))
    sds = jax.ShapeDtypeStruct
    return _hosted_call(
        body, name=name, grid=(MLA_H, nq, nk),
        in_specs=[qs(0), qs(0), ks(0), pl.BlockSpec((tk, LANES), lambda h, i, j: (jnp.minimum(j, i), 0)), ks(MLA_H),
                  qs(0), qs(0), qs(0)],
        out_specs=[qs(0), qs(0)],
        out_shape=[sds((t, MLA_H * LANES), F32), sds((t, MLA_H * LANES), F32)],
        scratch_shapes=[pltpu.VMEM((tq, LANES), F32), pltpu.VMEM((tq, LANES), F32)],
        sem=("parallel", "parallel", "arbitrary"), args=(q, qr, kv, kr, kv, o, do, lse), ride=ride)


def _attn_bwd_kv(q, qr, kv, kr, o, do, lse, name):
    t = q.shape[0]
    tq = tk = _tile(t, ATT_BLK)
    nq = nk = t // tq

    def body(qn_ref, qr_ref, kn_ref, kr_ref, v_ref, o_ref, do_ref, lse_ref, dkn_ref, dv_ref, dkr_ref, akn_s, av_s):
        j, h, i = pl.program_id(0), pl.program_id(1), pl.program_id(2)

        @pl.when((h == 0) & (i == 0))
        def _():
            dkr_ref[...] = jnp.zeros_like(dkr_ref)

        @pl.when(i == 0)
        def _():
            akn_s[...] = jnp.zeros_like(akn_s)
            av_s[...] = jnp.zeros_like(av_s)

        @pl.when(i >= j)
        def _():
            s = _att_scores(qn_ref[...], qr_ref[...], kn_ref[...], kr_ref[...], i, j, tq, tk)
            p = jnp.exp(s - lse_ref[:, 0:1])
            do = do_ref[...]
            dp = _dg(_mx(do), _mx(v_ref[...]), 1, 1)
            dl = jnp.sum(do * o_ref[...], -1, keepdims=True)
            ds = _mx(p * (dp - dl) * MLA_SCALE)
            av_s[...] += _dg(_mx(p), _mx(do), 0, 0)
            akn_s[...] += _dg(ds, _mx(qn_ref[...]), 0, 0)
            dkr_ref[...] += _dg(ds, _mx(qr_ref[...]), 0, 0)

        @pl.when(i == nq - 1)
        def _():
            dkn_ref[...] = akn_s[...]
            dv_ref[...] = av_s[...]

    qs = lambda off: pl.BlockSpec((tq, LANES), lambda j, h, i: (jnp.maximum(i, j), off + h))
    ks = lambda off: pl.BlockSpec((tk, LANES), lambda j, h, i: (j, off + h))
    sds = jax.ShapeDtypeStruct
    return pl.pallas_call(
        body, name=name, grid=(nk, MLA_H, nq),
        in_specs=[qs(0), qs(0), ks(0), pl.BlockSpec((tk, LANES), lambda j, h, i: (j, 0)), ks(MLA_H), qs(0), qs(0), qs(0)],
        out_specs=[ks(0), ks(0), pl.BlockSpec((tk, LANES), lambda j, h, i: (j, 0))],
        out_shape=[sds((t, MLA_H * LANES), F32), sds((t, MLA_H * LANES), F32), sds((t, LANES), F32)],
        scratch_shapes=[pltpu.VMEM((tk, LANES), F32), pltpu.VMEM((tk, LANES), F32)],
        compiler_params=_cparams(("parallel", "arbitrary", "arbitrary")),
    )(q, qr, kv, kr, kv, o, do, lse)


def _rope_tables(positions):
    inv_freq = ROPE_THETA ** (-jnp.arange(0, MLA_ROPE, 2, dtype=F32) / MLA_ROPE)
    ang = positions.astype(F32)[:, None] * inv_freq
    pad = jnp.zeros((positions.shape[0], LANES - MLA_ROPE), F32)
    cos, sin = jnp.cos(ang), jnp.sin(ang)
    return jnp.concatenate([cos, cos, pad], 1), jnp.concatenate([sin, sin, pad], 1)


def _mla_weights(w):
    dt = w["in_w"].dtype
    iw = w["in_w"]
    c1 = MLA_QR + MLA_KVR + MLA_ROPE
    w_in = jnp.concatenate([iw[:, :c1], jnp.zeros((D_MODEL, LANES - MLA_ROPE), dt), iw[:, c1:]], axis=1)
    qu = w["q_up_w"].reshape(MLA_QR, MLA_H, MLA_NOPE + MLA_ROPE)
    qrope = jnp.concatenate([qu[:, :, MLA_NOPE:], jnp.zeros((MLA_QR, MLA_H, LANES - MLA_ROPE), dt)], axis=2)
    w_q = jnp.concatenate([qu[:, :, :MLA_NOPE].reshape(MLA_QR, -1), qrope.reshape(MLA_QR, -1)], axis=1)
    kvu = w["kv_up_w"].reshape(MLA_KVR, MLA_H, MLA_NOPE + MLA_V)
    w_kv = jnp.concatenate([kvu[:, :, :MLA_NOPE].reshape(MLA_KVR, -1), kvu[:, :, MLA_NOPE:].reshape(MLA_KVR, -1)], axis=1)
    return w_in, w_q, w_kv


def _mla_layer_fwd(u, w, ln_g, ln_b, cosf, sinf, tag, ride=None):
    w_in, w_q, w_kv = _mla_weights(w)
    qnw, kvnw = w["q_norm_w"].reshape(1, -1), w["kv_norm_w"].reshape(1, -1)
    proj = _mm(u, w_in, name=tag + "_in")
    qn, kvn, kr = _mla_low_fwd(proj, qnw, kvnw, cosf, sinf, name=tag + "_low")
    q = _mm(qn, w_q, name=tag + "_qup")
    kv = _mm(kvn, w_kv, name=tag + "_kvup")
    qr = _rope_heads(q, MLA_H, cosf, sinf, False, name=tag + "_qrope")
    (o, og, lse), rode = _attn_fwd(q, qr, kv, kr, proj, name=tag + "_attn", ride=ride)
    y = _mm(og, w["out_w"], name=tag + "_out")
    h = _ln_fwd(u, y, ln_g, ln_b, name=tag + "_ln")
    saved = dict(u=u, w_in=w_in, w_q=w_q, w_kv=w_kv, qnw=qnw, kvnw=kvnw, proj=proj, qn=qn, kvn=kvn, kr=kr, q=q, kv=kv,
                 qr=qr, o=o, og=og, lse=lse, y=y)
    return h, saved, rode


def _mla_layer_bwd(s, w, ln_g, cosf, sinf, dr_up, du_up, tag, ride=None):
    dr, dg, db = _ln_bwd(s["u"], s["y"], ln_g, dr_up, du_up, name=tag + "_ln_b")
    dog = _mm(dr, w["out_w"], tb=True, name=tag + "_out_bx")
    d_out_w = _mm(s["og"], dr, ta=True, out_dtype=WIRE, name=tag + "_out_bw")
    do, dz = _gate_bwd(dog, s["o"], s["proj"], name=tag + "_gate_b")
    (dqn_h, dqr_rot), rode = _attn_bwd_q(s["q"], s["qr"], s["kv"], s["kr"], s["o"], do, s["lse"], name=tag + "_attn_bq",
                                         ride=ride)
    dkn_h, dv_h, dkr_rot = _attn_bwd_kv(s["q"], s["qr"], s["kv"], s["kr"], s["o"], do, s["lse"], name=tag + "_attn_bkv")
    dqr = _rope_heads(dqr_rot, 0, cosf, sinf, True, name=tag + "_qrope_b")
    dq = jnp.concatenate([dqn_h, dqr], axis=1)
    dkv = jnp.concatenate([dkn_h, dv_h], axis=1)
    d_wq = _mm(s["qn"], dq, ta=True, out_dtype=WIRE, name=tag + "_qup_bw")
    dqn = _mm(dq, s["w_q"], tb=True, name=tag + "_qup_bx")
    d_wkv = _mm(s["kvn"], dkv, ta=True, out_dtype=WIRE, name=tag + "_kvup_bw")
    dkvn = _mm(dkv, s["w_kv"], tb=True, name=tag + "_kvup_bx")
    dlow, dqnw, dkvnw = _mla_low_bwd(s["proj"], s["qnw"], s["kvnw"], cosf, sinf, dqn, dkvn, dkr_rot, name=tag + "_low_b")
    dproj = jnp.concatenate([dlow, dz], axis=1)
    d_in = _mm(s["u"], dproj, ta=True, out_dtype=WIRE, name=tag + "_in_bw")
    du = _mm(dproj, s["w_in"], tb=True, name=tag + "_in_bx")
    c1 = MLA_QR + MLA_KVR + MLA_ROPE
    d_in_w = jnp.concatenate([d_in[:, :c1], d_in[:, MLA_LOW:]], axis=1)
    dq3n = d_wq[:, :MLA_H * MLA_NOPE].reshape(MLA_QR, MLA_H, MLA_NOPE)
    dq3r = d_wq[:, MLA_H * MLA_NOPE:].reshape(MLA_QR, MLA_H, LANES)[:, :, :MLA_ROPE]
    d_q_up = jnp.concatenate([dq3n, dq3r], axis=2).reshape(MLA_QR, -1)
    dkv3 = d_wkv.reshape(MLA_KVR, 2, MLA_H, MLA_NOPE)
    d_kv_up = jnp.concatenate([dkv3[:, 0], dkv3[:, 1]], axis=2).reshape(MLA_KVR, -1)
    grads = dict(in_w=d_in_w, q_norm_w=dqnw[0], q_up_w=d_q_up, kv_norm_w=dkvnw[0], kv_up_w=d_kv_up, out_w=d_out_w,
                 ln_g=dg[0], ln_b=db[0])
    return dr, du, grads, rode


GDN_REP = GDN_HV // GDN_HK
GDN_A_LANE = GDN_HV
GDN_HPB = 2
GDN_VPB = GDN_HPB * GDN_REP


def _h3(a, b, ca=1, cb=0):
    ah, bh = _mx(a), _mx(b)
    al, bl = _mx(a - ah.astype(F32)), _mx(b - bh.astype(F32))
    return _dg(ah, bh, ca, cb) + (_dg(ah, bl, ca, cb) + _dg(al, bh, ca, cb))


@jax.custom_vjp
def _neumann_inverse(x):
    L = x.shape[0]
    eye = (lax.broadcasted_iota(jnp.int32, (L, L), 0) == lax.broadcasted_iota(jnp.int32, (L, L), 1)).astype(F32)
    inv = eye + x
    xp = x
    for _ in range(L.bit_length() - 2):
        xp = _h3(xp, xp)
        inv = inv + _h3(inv, xp)
    return inv


def _neumann_f(x):
    inv = _neumann_inverse(x)
    return inv, inv


def _neumann_b(inv, ct):
    return (_h3(_h3(inv, ct, 0, 0), inv, 1, 1),)


_neumann_inverse.defvjp(_neumann_f, _neumann_b)


def _gdn_chunk(hb, q, k, v, z, ba, alog, dtb, nw, states):
    L = q.shape[0]
    r_i = lax.broadcasted_iota(jnp.int32, (L, L), 0)
    c_i = lax.broadcasted_iota(jnp.int32, (L, L), 1)
    incl, strict = r_i >= c_i, r_i > c_i
    beta_all = jax.nn.sigmoid(ba)
    g_all = -jnp.exp(alog) * _softplus(ba + dtb)
    gcs_all = _hd(incl.astype(F32), g_all)
    lane = lax.broadcasted_iota(jnp.int32, (L, LANES), 1)
    outs, news = [], []
    for i in range(GDN_HPB):
        qi, ki = q[:, i * GDN_DK:(i + 1) * GDN_DK], k[:, i * GDN_DK:(i + 1) * GDN_DK]
        qn = qi * lax.rsqrt(jnp.sum(qi * qi, -1, keepdims=True) + RMS_EPS) * (GDN_DK ** -0.5)
        kn = ki * lax.rsqrt(jnp.sum(ki * ki, -1, keepdims=True) + RMS_EPS)
        qk_raw = _nt(qn, kn)
        for hh in range(GDN_REP):
            idx = GDN_REP * i + hh
            hv = GDN_VPB * hb + idx
            beta = jnp.sum(jnp.where(lane == hv, beta_all, 0.0), axis=1, keepdims=True)
            gc = jnp.sum(jnp.where(lane == GDN_A_LANE + hv, gcs_all, 0.0), axis=1, keepdims=True)
            gm = jnp.broadcast_to(gc, (L, L))
            decay = jnp.exp(jnp.where(incl, gm - gm.T, -jnp.inf))
            vv = v[:, idx * GDN_DV:(idx + 1) * GDN_DV]
            zz = z[:, idx * GDN_DV:(idx + 1) * GDN_DV]
            s = states[idx]
            kb = kn * beta
            eg = jnp.exp(gc)
            inv = _neumann_inverse(-jnp.where(strict, _nt(kb, kn) * decay, 0.0))
            uw = _nn(inv, jnp.concatenate([vv * beta, kb * eg], axis=1))
            uu, ww = uw[:, :GDN_DV], uw[:, GDN_DV:]
            qk = jnp.where(incl, qk_raw * decay, 0.0)
            glast = gc[L - 1:L]
            kdec = kn * jnp.exp(glast - gc)
            vnew = uu - _nn(ww, s)
            o = _nn(qn * eg, s) + _nn(qk, vnew)
            news.append(s * jnp.exp(glast) + _tn(kdec, vnew))
            outs.append(o * lax.rsqrt(jnp.mean(o * o, -1, keepdims=True) + RMS_EPS) * nw * _silu(zz))
    return jnp.concatenate(outs, axis=1), tuple(news)


def _gdn_specs(nc, rev):
    cc = (lambda c: nc - 1 - c) if rev else (lambda c: c)
    wq, wv = GDN_HPB * GDN_DK, GDN_VPB * GDN_DV
    par = pl.BlockSpec((1, LANES), lambda c, h: (0, 0))
    return dict(
        q=pl.BlockSpec((GDN_L, wq), lambda c, h: (cc(c), h)),
        k=pl.BlockSpec((GDN_L, wq), lambda c, h: (cc(c), GDN_KEY // wq + h)),
        v=pl.BlockSpec((GDN_L, wv), lambda c, h: (cc(c), 2 * GDN_KEY // wv + h)),
        z=pl.BlockSpec((GDN_L, wv), lambda c, h: (cc(c), GDN_CONV_DIM // wv + h)),
        ba=pl.BlockSpec((GDN_L, LANES), lambda c, h: (cc(c), (GDN_CONV_DIM + GDN_VAL) // LANES)),
        par=par,
        st=pl.BlockSpec((1, GDN_VPB, GDN_DK, GDN_DV), lambda c, h: (cc(c), h, 0, 0)),
        o=pl.BlockSpec((GDN_L, wv), lambda c, h: (cc(c), h)),
        qk_out=pl.BlockSpec((GDN_L, wq), lambda c, h: (cc(c), h)),
        ba_out=pl.BlockSpec((GDN_L, LANES), lambda c, h: (cc(c), 0)),
    )


def _gdn_fwd(proj, act, alog, dtb, nw, name, ride=None):
    t = proj.shape[0]
    nc = t // GDN_L
    sp = _gdn_specs(nc, False)

    def body(q_ref, k_ref, v_ref, z_ref, ba_ref, alog_ref, dtb_ref, nw_ref, o_ref, st_ref, state):
        c, h = pl.program_id(0), pl.program_id(1)

        @pl.when(c == 0)
        def _():
            for i in range(GDN_VPB):
                state[h * GDN_VPB + i] = jnp.zeros((GDN_DK, GDN_DV), F32)

        prev = tuple(state[h * GDN_VPB + i] for i in range(GDN_VPB))
        for i in range(GDN_VPB):
            st_ref[0, i] = prev[i]
        on, new = _gdn_chunk(h, q_ref[...], k_ref[...], v_ref[...], z_ref[...], ba_ref[...], alog_ref[...],
                             dtb_ref[...], nw_ref[...], prev)
        o_ref[...] = on.astype(o_ref.dtype)
        for i in range(GDN_VPB):
            state[h * GDN_VPB + i] = new[i]

    sds = jax.ShapeDtypeStruct
    return _hosted_call(
        body, name=name, grid=(nc, GDN_HK // GDN_HPB),
        in_specs=[sp["q"], sp["k"], sp["v"], sp["z"], sp["ba"], sp["par"], sp["par"], sp["par"]],
        out_specs=[sp["o"], sp["st"]],
        out_shape=[sds((t, GDN_VAL), F32), sds((nc, GDN_HV, GDN_DK, GDN_DV), F32)],
        scratch_shapes=[pltpu.VMEM((GDN_HV, GDN_DK, GDN_DV), F32)],
        sem=("arbitrary", "arbitrary"), args=(act, act, act, proj, proj, alog, dtb, nw), ride=ride)


def _gdn_bwd(proj, act, alog, dtb, nw, states, don, name, ride=None):
    t = proj.shape[0]
    nc = t // GDN_L
    sp = _gdn_specs(nc, True)

    def body(q_ref, k_ref, v_ref, z_ref, ba_ref, alog_ref, dtb_ref, nw_ref, st_ref, do_ref,
             dq_ref, dk_ref, dv_ref, dz_ref, dba_ref, dalog_ref, ddtb_ref, dnw_ref, dstate):
        c, h = pl.program_id(0), pl.program_id(1)

        @pl.when(c == 0)
        def _():
            for i in range(GDN_VPB):
                dstate[h * GDN_VPB + i] = jnp.zeros((GDN_DK, GDN_DV), F32)

        @pl.when((c == 0) & (h == 0))
        def _():
            dalog_ref[...] = jnp.zeros_like(dalog_ref)
            ddtb_ref[...] = jnp.zeros_like(ddtb_ref)
            dnw_ref[...] = jnp.zeros_like(dnw_ref)

        @pl.when(h == 0)
        def _():
            dba_ref[...] = jnp.zeros_like(dba_ref)

        prev = tuple(st_ref[0, i] for i in range(GDN_VPB))
        _, vjp = jax.vjp(functools.partial(_gdn_chunk, h), q_ref[...], k_ref[...], v_ref[...], z_ref[...], ba_ref[...],
                         alog_ref[...], dtb_ref[...], nw_ref[...], prev)
        ct_state = tuple(dstate[h * GDN_VPB + i] for i in range(GDN_VPB))
        dq, dk, dv, dz, dba, dalog, ddtb, dnw, dprev = vjp((do_ref[...].astype(F32), ct_state))
        dq_ref[...] = dq
        dk_ref[...] = dk
        dv_ref[...] = dv
        dz_ref[...] = dz
        dba_ref[...] += dba
        dalog_ref[...] += dalog
        ddtb_ref[...] += ddtb
        dnw_ref[...] += dnw
        for i in range(GDN_VPB):
            dstate[h * GDN_VPB + i] = dprev[i]

    sds = jax.ShapeDtypeStruct
    par_out = pl.BlockSpec((1, LANES), lambda c, h: (0, 0))
    return _hosted_call(
        body, name=name, grid=(nc, GDN_HK // GDN_HPB),
        in_specs=[sp["q"], sp["k"], sp["v"], sp["z"], sp["ba"], sp["par"], sp["par"], sp["par"], sp["st"], sp["o"]],
        out_specs=[sp["qk_out"], sp["qk_out"], sp["o"], sp["o"], sp["ba_out"], par_out, par_out, par_out],
        out_shape=[sds((t, GDN_KEY), F32), sds((t, GDN_KEY), F32), sds((t, GDN_VAL), F32), sds((t, GDN_VAL), F32),
                   sds((t, LANES), F32), sds((1, LANES), F32), sds((1, LANES), F32), sds((1, LANES), F32)],
        scratch_shapes=[pltpu.VMEM((GDN_HV, GDN_DK, GDN_DV), F32)],
        sem=("arbitrary", "arbitrary"), args=(act, act, act, proj, proj, alog, dtb, nw, states, don), ride=ride)


GDN_PROJ = GDN_CONV_DIM + GDN_VAL + 2 * GDN_HV


def _gdn_layer_fwd(u, w, ln_g, ln_b, tag, ride=None):
    w_in = jnp.concatenate([w["in_w"], jnp.zeros((D_MODEL, GDN_PROJ_PAD - GDN_PROJ), w["in_w"].dtype)], axis=1)
    alog = _pad_lanes(w["a_log"], offset=GDN_A_LANE)
    dtb = _pad_lanes(w["dt_bias"], offset=GDN_A_LANE)
    nw = w["norm_w"].reshape(1, GDN_DV)
    zb = jnp.zeros((1, GDN_CONV_DIM), F32)
    proj = _mm(u, w_in, name=tag + "_in")
    act = _conv_fwd(proj, 0, w["conv_w"], zb, name=tag + "_conv")
    (on, states), rode = _gdn_fwd(proj, act, alog, dtb, nw, name=tag + "_delta", ride=ride)
    y = _mm(on, w["out_w"], name=tag + "_out")
    h = _ln_fwd(u, y, ln_g, ln_b, name=tag + "_ln")
    saved = dict(u=u, w_in=w_in, proj=proj, act=act, states=states, on=on, y=y, alog=alog, dtb=dtb, nw=nw, zb=zb)
    return h, saved, rode


def _gdn_layer_bwd(s, w, ln_g, dr_up, du_up, tag, ride=None):
    t = s["u"].shape[0]
    dr, dg, db = _ln_bwd(s["u"], s["y"], ln_g, dr_up, du_up, name=tag + "_ln_b")
    don = _mm(dr, w["out_w"], tb=True, name=tag + "_out_bx")
    d_out_w = _mm(s["on"], dr, ta=True, out_dtype=WIRE, name=tag + "_out_bw")
    (dq, dk, dv, dz, dba, dalog, ddtb, dnw), rode = _gdn_bwd(s["proj"], s["act"], s["alog"], s["dtb"], s["nw"], s["states"],
                                                            don, name=tag + "_delta_b", ride=ride)
    dact = jnp.concatenate([dq, dk, dv], axis=1)
    dpre, d_conv_w, _ = _conv_bwd_pre(s["proj"], 0, w["conv_w"], s["zb"], dact, name=tag + "_conv_bp")
    dqkv = _conv_bwd_x(dpre, w["conv_w"], name=tag + "_conv_bx")
    dproj = jnp.concatenate([dqkv, dz, dba, jnp.zeros((t, GDN_PROJ_PAD - GDN_PROJ - (LANES - 2 * GDN_HV)), F32)], axis=1)
    d_in = _mm(s["u"], dproj, ta=True, out_dtype=WIRE, name=tag + "_in_bw")
    du = _mm(dproj, s["w_in"], tb=True, name=tag + "_in_bx")
    grads = dict(in_w=d_in[:, :GDN_PROJ], conv_w=d_conv_w, a_log=dalog[0, GDN_A_LANE:GDN_A_LANE + GDN_HV],
                 dt_bias=ddtb[0, GDN_A_LANE:GDN_A_LANE + GDN_HV], norm_w=dnw[0], out_w=d_out_w, ln_g=dg[0], ln_b=db[0])
    return dr, du, grads, rode


def _mesh_pos():
    return lax.axis_index("x"), lax.axis_index("y"), lax.axis_index("c")


def _peer(k, x, y, c):
    return ((1 - x) if k & 4 else x, (1 - y) if k & 2 else y, (1 - c) if k & 1 else c)


def _ride_copies(ins, outs, send, recv, loc, scatter, with_arrivals):
    n = len(ins)
    x, y, c = _mesh_pos()
    me = 4 * x + 2 * y + c
    local = [pltpu.make_async_copy(ins[i].at[me] if scatter else ins[i], outs[i].at[me], loc.at[i]) for i in range(n)]
    sends, arrivals = [], []
    for k in range(1, N_DEV):
        peer = _peer(k, x, y, c)
        pidx = 4 * peer[0] + 2 * peer[1] + peer[2]
        for i in range(n):
            src = ins[i].at[pidx] if scatter else ins[i]
            sems = dict(send_sem=send.at[i, k - 1], recv_sem=recv.at[i, k - 1], device_id=peer,
                        device_id_type=pl.DeviceIdType.MESH)
            sends.append(pltpu.make_async_remote_copy(src_ref=src, dst_ref=outs[i].at[me], **sems))
            if with_arrivals:
                arrivals.append(pltpu.make_async_remote_copy(src_ref=src, dst_ref=outs[i].at[pidx], **sems))
    return local, sends, arrivals


def _ride_start(ins, outs, send, recv, loc, scatter):
    local, sends, _ = _ride_copies(ins, outs, send, recv, loc, scatter, False)
    for cp in local + sends:
        cp.start()


def _ride_wait(ins, outs, send, recv, loc, scatter):
    local, sends, arrivals = _ride_copies(ins, outs, send, recv, loc, scatter, True)
    for cp in arrivals:
        cp.wait_recv()
    for cp in sends:
        cp.wait_send()
    for cp in local:
        cp.wait()


def _ride_shapes(arrs, scatter):
    n = len(arrs)
    out_shape = [jax.ShapeDtypeStruct(a.shape if scatter else (N_DEV,) + a.shape, a.dtype) for a in arrs]
    scratch = [pltpu.SemaphoreType.DMA((n, N_DEV - 1)), pltpu.SemaphoreType.DMA((n, N_DEV - 1)), pltpu.SemaphoreType.DMA((n,))]
    return out_shape, scratch


def _exchange(arrs, scatter, name):
    n = len(arrs)
    hbm = pl.BlockSpec(memory_space=pltpu.HBM)

    def body(*refs):
        ins, outs = refs[:n], refs[n:2 * n]
        _ride_start(ins, outs, *refs[2 * n:], scatter)
        _ride_wait(ins, outs, *refs[2 * n:], scatter)

    out_shape, scratch = _ride_shapes(arrs, scatter)
    return pl.pallas_call(
        body, name=name, in_specs=[hbm] * n, out_specs=[hbm] * n, out_shape=out_shape, scratch_shapes=scratch,
        compiler_params=pltpu.CompilerParams(has_side_effects=True),
    )(*arrs)


def _hosted_call(body, *, name, grid, in_specs, out_specs, out_shape, scratch_shapes, sem, args, ride=None):
    if ride is None:
        return pl.pallas_call(body, name=name, grid=grid, in_specs=in_specs, out_specs=out_specs, out_shape=out_shape,
                              scratch_shapes=scratch_shapes, compiler_params=_cparams(sem))(*args), []
    arrs, scatter = ride
    n, ni, no, ns = len(arrs), len(in_specs), len(out_specs), len(scratch_shapes)
    hbm = pl.BlockSpec(memory_space=pltpu.HBM)
    r_shape, r_scratch = _ride_shapes(arrs, scatter)

    def full(*refs):
        a, ri = refs[:ni], refs[ni:ni + n]
        o, ro = refs[ni + n:ni + n + no], refs[ni + n + no:ni + 2 * n + no]
        s, rs = refs[ni + 2 * n + no:ni + 2 * n + no + ns], refs[ni + 2 * n + no + ns:]
        ids = [pl.program_id(d) for d in range(len(grid))]
        first, last = ids[0] == 0, ids[0] == grid[0] - 1
        for d in range(1, len(grid)):
            first, last = first & (ids[d] == 0), last & (ids[d] == grid[d] - 1)

        @pl.when(first)
        def _():
            _ride_start(ri, ro, *rs, scatter)

        body(*a, *o, *s)

        @pl.when(last)
        def _():
            _ride_wait(ri, ro, *rs, scatter)

    outs = pl.pallas_call(
        full, name=name, grid=grid, in_specs=list(in_specs) + [hbm] * n, out_specs=list(out_specs) + [hbm] * n,
        out_shape=list(out_shape) + r_shape, scratch_shapes=list(scratch_shapes) + r_scratch,
        compiler_params=pltpu.CompilerParams(dimension_semantics=("arbitrary",) * len(grid), vmem_limit_bytes=VMEM_LIMIT,
                                             has_side_effects=True),
    )(*args, *arrs)
    return outs[:no], list(outs[no:])


def _unshard(g, ax):
    g = jnp.moveaxis(g, 0, ax)
    sh = g.shape
    return g.reshape(sh[:ax] + (sh[ax] * sh[ax + 1],) + sh[ax + 2:])


def _to_parts(full, ax):
    sh = full.shape
    full = full.reshape(sh[:ax] + (N_DEV, sh[ax] // N_DEV) + sh[ax + 1:])
    return jnp.moveaxis(full, ax, 0)


def _row_tile(r, c):
    cap = max(8, (256 * 1024) // max(c, 1))
    best = None
    for d in range(8, min(r, cap) + 1, 8):
        if r % d == 0:
            best = d
    return r if best is None else best


def _adamw(w, m, v, parts, name):
    r, c = w.shape
    tr = _row_tile(r, c)

    def body(w_ref, m_ref, v_ref, p_ref, g_ref, d_ref, nm_ref, nv_ref):
        g = p_ref[0].astype(F32)
        for q in range(1, N_DEV):
            g = g + p_ref[q].astype(F32)
        nm = ADAM_B1 * m_ref[...] + (1.0 - ADAM_B1) * g
        nv = ADAM_B2 * v_ref[...] + (1.0 - ADAM_B2) * (g * g)
        m_hat = nm / (1.0 - ADAM_B1 ** ADAM_STEP)
        v_hat = nv / (1.0 - ADAM_B2 ** ADAM_STEP)
        g_ref[...] = g
        d_ref[...] = -ADAM_LR * (m_hat / (jnp.sqrt(v_hat) + ADAM_EPS) + ADAM_WD * w_ref[...])
        nm_ref[...] = nm
        nv_ref[...] = nv

    row = pl.BlockSpec((tr, c), lambda i: (i, 0))
    out = jax.ShapeDtypeStruct((r, c), F32)
    return pl.pallas_call(
        body, name=name, grid=(r // tr,),
        in_specs=[row, row, row, pl.BlockSpec((N_DEV, tr, c), lambda i: (0, i, 0))],
        out_specs=[row] * 4, out_shape=[out] * 4, compiler_params=_cparams(("parallel",)),
    )(w, m, v, parts)


WEIGHTS = ['ssd_in_w', 'ssd_conv_w', 'ssd_conv_b', 'ssd_dt_bias', 'ssd_a_log', 'ssd_d', 'ssd_norm_w', 'ssd_out_w',
           'mla_in_w', 'mla_q_norm_w', 'mla_q_up_w', 'mla_kv_norm_w', 'mla_kv_up_w', 'mla_out_w', 'gdn_in_w',
           'gdn_conv_w', 'gdn_a_log', 'gdn_dt_bias', 'gdn_norm_w', 'gdn_out_w', 'ln_g', 'ln_b']
SHARDED = {'ssd_in_w': (1, True), 'ssd_conv_w': (1, False), 'ssd_conv_b': (0, False), 'ssd_norm_w': (0, False),
           'ssd_out_w': (0, True), 'mla_in_w': (1, True), 'mla_q_up_w': (1, True), 'mla_kv_up_w': (1, True),
           'mla_out_w': (0, True), 'gdn_in_w': (1, True), 'gdn_conv_w': (1, False), 'gdn_out_w': (0, True)}
REPLICATED = [n for n in WEIGHTS if n not in SHARDED]


def _pack_small(vals):
    flat = jnp.concatenate([vals[n].reshape(-1).astype(F32) for n in REPLICATED])
    rows = -(-flat.shape[0] // (8 * LANES)) * 8
    return jnp.pad(flat, (0, rows * LANES - flat.shape[0])).reshape(rows, LANES)


def _unpack_small(slab, like):
    flat = slab.reshape(-1)
    out, off = {}, 0
    for n in REPLICATED:
        sz = like[n].size
        out[n] = flat[off:off + sz].reshape(like[n].shape)
        off += sz
    return out


def kernel(x, positions, ssd_in_w, ssd_conv_w, ssd_conv_b, ssd_dt_bias, ssd_a_log, ssd_d, ssd_norm_w, ssd_out_w, mla_in_w, mla_q_norm_w, mla_q_up_w, mla_kv_norm_w, mla_kv_up_w, mla_out_w, gdn_in_w, gdn_conv_w, gdn_a_log, gdn_dt_bias, gdn_norm_w, gdn_out_w, ln_g, ln_b, loss_target, m_ssd_in_w, m_ssd_conv_w, m_ssd_conv_b, m_ssd_dt_bias, m_ssd_a_log, m_ssd_d, m_ssd_norm_w, m_ssd_out_w, m_mla_in_w, m_mla_q_norm_w, m_mla_q_up_w, m_mla_kv_norm_w, m_mla_kv_up_w, m_mla_out_w, m_gdn_in_w, m_gdn_conv_w, m_gdn_a_log, m_gdn_dt_bias, m_gdn_norm_w, m_gdn_out_w, m_ln_g, m_ln_b, v_ssd_in_w, v_ssd_conv_w, v_ssd_conv_b, v_ssd_dt_bias, v_ssd_a_log, v_ssd_d, v_ssd_norm_w, v_ssd_out_w, v_mla_in_w, v_mla_q_norm_w, v_mla_q_up_w, v_mla_kv_norm_w, v_mla_kv_up_w, v_mla_out_w, v_gdn_in_w, v_gdn_conv_w, v_gdn_a_log, v_gdn_dt_bias, v_gdn_norm_w, v_gdn_out_w, v_ln_g, v_ln_b):
    loc = locals()
    w = {n: loc[n] for n in WEIGHTS}
    m = {n: loc["m_" + n] for n in WEIGHTS}
    v = {n: loc["v_" + n] for n in WEIGHTS}
    xs, pos, tgt = x[0], positions[0], loss_target[0]

    def names_of(prefix):
        return [n for n in SHARDED if n.startswith(prefix + "_")]

    def shards(prefix, j):
        return [w[n][j].astype(WIRE) if SHARDED[n][1] else w[n][j] for n in names_of(prefix)]

    def assemble(prefix, j, gathered):
        lw = {n[len(prefix) + 1:]: _unshard(g, SHARDED[n][0]) for n, g in zip(names_of(prefix), gathered) if g is not None}
        lw.update({n[len(prefix) + 1:]: w[n][j] for n in REPLICATED if n.startswith(prefix + "_")})
        return lw

    def parts(prefix, g):
        return [_to_parts(g[n[len(prefix) + 1:]], SHARDED[n][0]).astype(WIRE if SHARDED[n][1] else F32)
                for n in names_of(prefix)]

    lg = lambda i: w["ln_g"][i].reshape(1, D_MODEL)
    lb = lambda i: w["ln_b"][i].reshape(1, D_MODEL)
    cosf, sinf = _rope_tables(pos)

    sh0 = shards("ssd", 0)
    w_s0 = assemble("ssd", 0, list(_exchange(sh0[:1], False, name="gather_ssd0")) + [None] * (len(sh0) - 1))

    def late0(got):
        rest = {n[4:]: _unshard(g, SHARDED[n][0]) for n, g in zip(names_of("ssd")[1:], got)}
        return rest, got[len(sh0) - 1:]

    h1, s0, got = _ssd_layer_fwd(xs, w_s0, lg(0), lb(0), "ssd0", ride=(sh0[1:] + shards("mla", 0), False), late=late0)
    w_m0 = assemble("mla", 0, got)
    h2, s1, got = _mla_layer_fwd(h1, w_m0, lg(1), lb(1), cosf, sinf, "mla0", ride=(shards("gdn", 0), False))
    w_g0 = assemble("gdn", 0, got)
    h3, s2, got = _gdn_layer_fwd(h2, w_g0, lg(2), lb(2), "gdn0", ride=(shards("ssd", 1), False))
    w_s1 = assemble("ssd", 1, got)
    h4, s3, _ = _ssd_layer_fwd(h3, w_s1, lg(3), lb(3), "ssd1")
    loss_tile, dl = _loss_head(h4, tgt, name="loss_head")

    def own0(k, g):
        n = "ssd_" + k
        return _to_parts(g, SHARDED[n][0]).astype(WIRE if SHARDED[n][1] else F32)

    dr3, du3, g3, _, _ = _ssd_layer_bwd(s3, w_s1, lg(3), jnp.zeros_like(dl), dl, "ssd1")
    dr2, du2, g2, r3 = _gdn_layer_bwd(s2, w_g0, lg(2), dr3, du3, "gdn0", ride=(parts("ssd", g3), True))
    dr1, du1, g1, r2 = _mla_layer_bwd(s1, w_m0, lg(1), cosf, sinf, dr2, du2, "mla0", ride=(parts("gdn", g2), True))
    dr0, du0, g0, r1, r0 = _ssd_layer_bwd(s0, w_s0, lg(0), dr1, du1, "ssd0", ride=(parts("mla", g1), True), own=own0)
    grad_x = _axpy(dr0, du0, name="grad_x")[None]

    gsmall = {"ssd_" + k: jnp.stack([g0[k], g3[k]]) for k in ("dt_bias", "a_log", "d")}
    gsmall.update({"mla_" + k: g1[k][None] for k in ("q_norm_w", "kv_norm_w")})
    gsmall.update({"gdn_" + k: g2[k][None] for k in ("a_log", "dt_bias", "norm_w")})
    gsmall["ln_g"] = jnp.stack([g0["ln_g"], g1["ln_g"], g2["ln_g"], g3["ln_g"]])
    gsmall["ln_b"] = jnp.stack([g0["ln_b"], g1["ln_b"], g2["ln_b"], g3["ln_b"]])
    small = _pack_small(gsmall)
    rsmall, = _exchange([jnp.broadcast_to(small[None], (N_DEV,) + small.shape)], True, name="gather_small_grads")

    recvd = {n: jnp.stack([r0[n[4:]], b], axis=1) for n, b in zip(names_of("ssd"), r3)}
    recvd.update({n: a[:, None] for n, a in zip(names_of("mla"), r1)})
    recvd.update({n: a[:, None] for n, a in zip(names_of("gdn"), r2)})
    recvd = [recvd[n] for n in SHARDED] + [rsmall]

    grads, deltas, new_m, new_v = {}, {}, {}, {}
    for n, pt in zip(SHARDED, recvd[:-1]):
        shp = w[n].shape
        r2d = (-1, shp[-1])
        outs = _adamw(w[n].reshape(r2d), m[n].reshape(r2d), v[n].reshape(r2d), pt.reshape((N_DEV,) + w[n].reshape(r2d).shape),
                      name="adamw_" + n)
        grads[n], deltas[n], new_m[n], new_v[n] = (o.reshape(shp) for o in outs)
    outs = _adamw(_pack_small(w), _pack_small(m), _pack_small(v), recvd[-1], name="adamw_replicated")
    for dst, o in zip((grads, deltas, new_m, new_v), outs):
        dst.update(_unpack_small(o, w))

    loss = lax.psum(loss_tile[0, 0], ("x", "y", "c"))
    return (loss, grad_x, *[grads[n] for n in WEIGHTS], *[deltas[n] for n in WEIGHTS],
            *[new_m[n] for n in WEIGHTS], *[new_v[n] for n in WEIGHTS])
```

```python
import functools

import jax
import jax.numpy as jnp
from jax import lax
from jax.experimental import pallas as pl
from jax.experimental.pallas import tpu as pltpu

F32 = jnp.float32
MXU = jnp.bfloat16
WIRE = jnp.bfloat16
HI = lax.Precision.HIGHEST

N_DEV = 8
LANES = 128
VMEM_LIMIT = 56 * 1024 * 1024

D_MODEL = 2048
DEPTH = 4
ALPHA = (2.0 * DEPTH) ** 0.25
LN_EPS = 1e-5
RMS_EPS = 1e-6

SSD_DI = 4096
SSD_P = 64
SSD_H = 64
SSD_G = 8
SSD_N = 128
SSD_L = 128
SSD_GS = SSD_DI // SSD_G
SSD_CONV_DIM = SSD_DI + 2 * SSD_G * SSD_N
SSD_PROJ_PAD = SSD_DI + SSD_CONV_DIM + LANES

MLA_H = 16
MLA_QR = 768
MLA_KVR = 512
MLA_NOPE = 128
MLA_ROPE = 64
MLA_V = 128
MLA_GATE = MLA_H * MLA_V
MLA_PROJ_PAD = MLA_QR + MLA_KVR + LANES + MLA_GATE
MLA_SCALE = (MLA_NOPE + MLA_ROPE) ** -0.5
ROPE_THETA = 10000.0
ATT_BLK = 512

GDN_HK = 16
GDN_HV = 32
GDN_DK = 128
GDN_DV = 128
GDN_KEY = GDN_HK * GDN_DK
GDN_VAL = GDN_HV * GDN_DV
GDN_L = 128
GDN_CONV_DIM = 2 * GDN_KEY + GDN_VAL
GDN_PROJ_PAD = 12800

ADAM_LR = 0.001
ADAM_B1 = 0.9
ADAM_B2 = 0.999
ADAM_EPS = 1e-08
ADAM_WD = 0.01
ADAM_STEP = 10


def _cparams(sem=None):
    return pltpu.CompilerParams(dimension_semantics=sem, vmem_limit_bytes=VMEM_LIMIT)


def _tile(n, cap):
    if n <= cap:
        return n
    best = None
    for d in range(LANES, cap + 1, LANES):
        if n % d == 0:
            best = d
    assert best is not None, (n, cap)
    return best


def _dg(a, b, ca, cb, prec=None):
    return lax.dot_general(a, b, (((ca,), (cb,)), ((), ())), preferred_element_type=F32, precision=prec)


def _mx(a):
    return a.astype(MXU)


def _hd(a, b):
    return _dg(a, b, 1, 0, HI)


@jax.custom_vjp
def _nn(a, b):
    return _dg(_mx(a), _mx(b), 1, 0)


def _nn_f(a, b):
    return _nn(a, b), (a, b)


def _nn_b(res, ct):
    a, b = res
    return _dg(_mx(ct), _mx(b), 1, 1), _dg(_mx(a), _mx(ct), 0, 0)


_nn.defvjp(_nn_f, _nn_b)


@jax.custom_vjp
def _nt(a, b):
    return _dg(_mx(a), _mx(b), 1, 1)


def _nt_f(a, b):
    return _nt(a, b), (a, b)


def _nt_b(res, ct):
    a, b = res
    return _dg(_mx(ct), _mx(b), 1, 0), _dg(_mx(ct), _mx(a), 0, 0)


_nt.defvjp(_nt_f, _nt_b)


@jax.custom_vjp
def _tn(a, b):
    return _dg(_mx(a), _mx(b), 0, 0)


def _tn_f(a, b):
    return _tn(a, b), (a, b)


def _tn_b(res, ct):
    a, b = res
    return _dg(_mx(b), _mx(ct), 1, 1), _dg(_mx(a), _mx(ct), 1, 0)


_tn.defvjp(_tn_f, _tn_b)


def _softplus(x):
    return jnp.maximum(x, 0.0) + jnp.log(1.0 + jnp.exp(-jnp.abs(x)))


def _silu(x):
    return x * jax.nn.sigmoid(x)


MM_TM = 1024
MM_TN = 1280
MM_VMEM_BUDGET = 40 * 1024 * 1024


def _mm(a, b, *, ta=False, tb=False, out_dtype=F32, name, ride=None):
    if ta:
        kdim, m = a.shape
    else:
        m, kdim = a.shape
    if tb:
        n, kb = b.shape
    else:
        kb, n = b.shape
    assert kdim == kb, (a.shape, b.shape, ta, tb)
    tm, tn = _tile(m, MM_TM), _tile(n, MM_TN)
    abytes, bbytes, obytes = a.dtype.itemsize, b.dtype.itemsize, jnp.dtype(out_dtype).itemsize
    tk = LANES
    for d in range(LANES, kdim + 1, LANES):
        if kdim % d == 0 and 2 * d * (tm * abytes + tn * bbytes) + tm * tn * (2 * obytes + 4) <= MM_VMEM_BUDGET:
            tk = d
    nk = kdim // tk
    ca, cb = (0 if ta else 1), (1 if tb else 0)

    def body(a_ref, b_ref, o_ref, *acc):
        part = _dg(_mx(a_ref[...]), _mx(b_ref[...]), ca, cb)
        if nk == 1:
            o_ref[...] = part.astype(out_dtype)
            return
        acc_ref, = acc
        k = pl.program_id(2)

        @pl.when(k == 0)
        def _():
            acc_ref[...] = part

        @pl.when(k > 0)
        def _():
            acc_ref[...] += part

        @pl.when(k == nk - 1)
        def _():
            o_ref[...] = acc_ref[...].astype(out_dtype)

    a_spec = pl.BlockSpec((tk, tm), lambda i, j, k: (k, i)) if ta else pl.BlockSpec((tm, tk), lambda i, j, k: (i, k))
    b_spec = pl.BlockSpec((tn, tk), lambda i, j, k: (j, k)) if tb else pl.BlockSpec((tk, tn), lambda i, j, k: (k, j))
    (out,), rode = _hosted_call(
        body, name=name, grid=(m // tm, n // tn, nk),
        in_specs=[a_spec, b_spec], out_specs=[pl.BlockSpec((tm, tn), lambda i, j, k: (i, j))],
        out_shape=[jax.ShapeDtypeStruct((m, n), out_dtype)],
        scratch_shapes=[pltpu.VMEM((tm, tn), F32)] if nk > 1 else [],
        sem=("parallel", "parallel", "arbitrary"), args=(a, b), ride=ride)
    return out if ride is None else (out, rode)


def _ln_fwd(h, y, g, b, name):
    t, d = h.shape
    tr = _tile(t, 256)

    def body(h_ref, y_ref, g_ref, b_ref, o_ref):
        r = ALPHA * h_ref[...] + y_ref[...]
        mu = jnp.mean(r, -1, keepdims=True)
        xc = r - mu
        var = jnp.mean(xc * xc, -1, keepdims=True)
        o_ref[...] = xc * lax.rsqrt(var + LN_EPS) * g_ref[...] + b_ref[...]

    row = pl.BlockSpec((tr, d), lambda i: (i, 0))
    par = pl.BlockSpec((1, d), lambda i: (0, 0))
    return pl.pallas_call(
        body, name=name, grid=(t // tr,), in_specs=[row, row, par, par], out_specs=row,
        out_shape=jax.ShapeDtypeStruct((t, d), F32), compiler_params=_cparams(("parallel",)),
    )(h, y, g, b)


def _ln_bwd(h, y, g, dr_up, du_up, name):
    t, d = h.shape
    tr = _tile(t, 256)

    def body(h_ref, y_ref, g_ref, dr_ref, du_ref, o_ref, dg_ref, db_ref):
        i = pl.program_id(0)

        @pl.when(i == 0)
        def _():
            dg_ref[...] = jnp.zeros_like(dg_ref)
            db_ref[...] = jnp.zeros_like(db_ref)

        dout = ALPHA * dr_ref[...] + du_ref[...]
        r = ALPHA * h_ref[...] + y_ref[...]
        mu = jnp.mean(r, -1, keepdims=True)
        xc = r - mu
        rstd = lax.rsqrt(jnp.mean(xc * xc, -1, keepdims=True) + LN_EPS)
        xh = xc * rstd
        dxh = dout * g_ref[...]
        o_ref[...] = rstd * (dxh - jnp.mean(dxh, -1, keepdims=True) - xh * jnp.mean(dxh * xh, -1, keepdims=True))
        dg_ref[...] += jnp.sum(dout * xh, 0, keepdims=True)
        db_ref[...] += jnp.sum(dout, 0, keepdims=True)

    row = pl.BlockSpec((tr, d), lambda i: (i, 0))
    par = pl.BlockSpec((1, d), lambda i: (0, 0))
    return pl.pallas_call(
        body, name=name, grid=(t // tr,), in_specs=[row, row, par, row, row], out_specs=[row, par, par],
        out_shape=[jax.ShapeDtypeStruct((t, d), F32), jax.ShapeDtypeStruct((1, d), F32), jax.ShapeDtypeStruct((1, d), F32)],
        compiler_params=_cparams(("arbitrary",)),
    )(h, y, g, dr_up, du_up)


def _loss_head(h, tgt, name):
    t, d = h.shape
    tr = _tile(t, 256)

    def body(h_ref, t_ref, l_ref, d_ref):
        i = pl.program_id(0)

        @pl.when(i == 0)
        def _():
            l_ref[...] = jnp.zeros_like(l_ref)

        e = h_ref[...] - t_ref[...]
        d_ref[...] = e * (1.0 / d)
        l_ref[...] += 0.5 * jnp.sum(jnp.mean(e * e, -1, keepdims=True))

    row = pl.BlockSpec((tr, d), lambda i: (i, 0))
    return pl.pallas_call(
        body, name=name, grid=(t // tr,), in_specs=[row, row],
        out_specs=[pl.BlockSpec((8, LANES), lambda i: (0, 0)), row],
        out_shape=[jax.ShapeDtypeStruct((8, LANES), F32), jax.ShapeDtypeStruct((t, d), F32)],
        compiler_params=_cparams(("arbitrary",)),
    )(h, tgt)


def _axpy(dr, du, name):
    t, d = dr.shape
    tr = _tile(t, 256)

    def body(a_ref, b_ref, o_ref):
        o_ref[...] = ALPHA * a_ref[...] + b_ref[...]

    row = pl.BlockSpec((tr, d), lambda i: (i, 0))
    return pl.pallas_call(
        body, name=name, grid=(t // tr,), in_specs=[row, row], out_specs=row,
        out_shape=jax.ShapeDtypeStruct((t, d), F32), compiler_params=_cparams(("parallel",)),
    )(dr, du)


CONV_TT = 512
CONV_TC = 512


def _shift_down(cur, prev, s, row):
    if s == 0:
        return cur
    return jnp.where(row >= s, pltpu.roll(cur, s, 0), pltpu.roll(prev, s, 0))


def _shift_up(cur, nxt, s, row, tt):
    if s == 0:
        return cur
    return jnp.where(row < tt - s, pltpu.roll(cur, tt - s, 0), pltpu.roll(nxt, tt - s, 0))


def _conv_fwd(proj, col0, w, b, name):
    t = proj.shape[0]
    c = w.shape[1]
    tt = _tile(t, CONV_TT)
    cb0 = col0 // CONV_TC

    def body(x_ref, p_ref, w_ref, b_ref, o_ref):
        i = pl.program_id(1)
        x = x_ref[...]
        p = jnp.where(i > 0, p_ref[...], 0.0)
        row = lax.broadcasted_iota(jnp.int32, x.shape, 0)
        pre = b_ref[...] + w_ref[3:4, :] * x
        for s in (1, 2, 3):
            pre = pre + w_ref[3 - s:4 - s, :] * _shift_down(x, p, s, row)
        o_ref[...] = _silu(pre)

    return pl.pallas_call(
        body, name=name, grid=(c // CONV_TC, t // tt),
        in_specs=[pl.BlockSpec((tt, CONV_TC), lambda j, i: (i, cb0 + j)),
                  pl.BlockSpec((tt, CONV_TC), lambda j, i: (jnp.maximum(i - 1, 0), cb0 + j)),
                  pl.BlockSpec((4, CONV_TC), lambda j, i: (0, j)),
                  pl.BlockSpec((1, CONV_TC), lambda j, i: (0, j))],
        out_specs=pl.BlockSpec((tt, CONV_TC), lambda j, i: (i, j)),
        out_shape=jax.ShapeDtypeStruct((t, c), F32), compiler_params=_cparams(("parallel", "parallel")),
    )(proj, proj, w, b)


def _conv_bwd_pre(proj, col0, w, b, dact, name):
    t = proj.shape[0]
    c = w.shape[1]
    tt = _tile(t, CONV_TT)
    cb0 = col0 // CONV_TC

    def body(x_ref, p_ref, w_ref, b_ref, d_ref, dpre_ref, dw_ref, db_ref):
        i = pl.program_id(1)

        @pl.when(i == 0)
        def _():
            dw_ref[...] = jnp.zeros_like(dw_ref)
            db_ref[...] = jnp.zeros_like(db_ref)

        x = x_ref[...]
        p = jnp.where(i > 0, p_ref[...], 0.0)
        row = lax.broadcasted_iota(jnp.int32, x.shape, 0)
        sh = [_shift_down(x, p, s, row) for s in range(4)]
        pre = b_ref[...] + w_ref[3:4, :] * sh[0]
        for s in (1, 2, 3):
            pre = pre + w_ref[3 - s:4 - s, :] * sh[s]
        sg = jax.nn.sigmoid(pre)
        dpre = d_ref[...] * (sg * (1.0 + pre * (1.0 - sg)))
        dpre_ref[...] = dpre
        for s in range(4):
            dw_ref[3 - s:4 - s, :] += jnp.sum(dpre * sh[s], 0, keepdims=True)
        db_ref[...] += jnp.sum(dpre, 0, keepdims=True)

    return pl.pallas_call(
        body, name=name, grid=(c // CONV_TC, t // tt),
        in_specs=[pl.BlockSpec((tt, CONV_TC), lambda j, i: (i, cb0 + j)),
                  pl.BlockSpec((tt, CONV_TC), lambda j, i: (jnp.maximum(i - 1, 0), cb0 + j)),
                  pl.BlockSpec((4, CONV_TC), lambda j, i: (0, j)),
                  pl.BlockSpec((1, CONV_TC), lambda j, i: (0, j)),
                  pl.BlockSpec((tt, CONV_TC), lambda j, i: (i, j))],
        out_specs=[pl.BlockSpec((tt, CONV_TC), lambda j, i: (i, j)),
                   pl.BlockSpec((4, CONV_TC), lambda j, i: (0, j)),
                   pl.BlockSpec((1, CONV_TC), lambda j, i: (0, j))],
        out_shape=[jax.ShapeDtypeStruct((t, c), F32), jax.ShapeDtypeStruct((4, c), F32), jax.ShapeDtypeStruct((1, c), F32)],
        compiler_params=_cparams(("parallel", "arbitrary")),
    )(proj, proj, w, b, dact)


def _conv_bwd_x(dpre, w, name):
    t, c = dpre.shape
    tt = _tile(t, CONV_TT)
    nt = t // tt

    def body(d_ref, n_ref, w_ref, o_ref):
        i = pl.program_id(1)
        d = d_ref[...]
        nx = jnp.where(i < nt - 1, n_ref[...], 0.0)
        row = lax.broadcasted_iota(jnp.int32, d.shape, 0)
        acc = w_ref[3:4, :] * d
        for s in (1, 2, 3):
            acc = acc + w_ref[3 - s:4 - s, :] * _shift_up(d, nx, s, row, tt)
        o_ref[...] = acc

    return pl.pallas_call(
        body, name=name, grid=(c // CONV_TC, nt),
        in_specs=[pl.BlockSpec((tt, CONV_TC), lambda j, i: (i, j)),
                  pl.BlockSpec((tt, CONV_TC), lambda j, i: (jnp.minimum(i + 1, nt - 1), j)),
                  pl.BlockSpec((4, CONV_TC), lambda j, i: (0, j))],
        out_specs=pl.BlockSpec((tt, CONV_TC), lambda j, i: (i, j)),
        out_shape=jax.ShapeDtypeStruct((t, c), F32), compiler_params=_cparams(("parallel", "parallel")),
    )(dpre, dpre, w)


def _ssd_chunk(g, x, z, bm, cm, dtraw, dtb, alog, dsk, nw, prev):
    L = x.shape[0]
    r_i = lax.broadcasted_iota(jnp.int32, (L, L), 0)
    c_i = lax.broadcasted_iota(jnp.int32, (L, L), 1)
    causal = r_i >= c_i
    dt = _softplus(dtraw + dtb)
    a = dt * (-jnp.exp(alog))
    acs = _hd(causal.astype(F32), a)
    e_r = lax.broadcasted_iota(jnp.int32, (LANES, SSD_GS), 0)
    e_c = lax.broadcasted_iota(jnp.int32, (LANES, SSD_GS), 1)
    sel = (e_r == g * (SSD_H // SSD_G) + jnp.right_shift(e_c, 6)).astype(F32)
    dt_x = _hd(dt, sel)
    acs_x = _hd(acs, sel)
    d_x = _hd(jnp.broadcast_to(dsk, (8, LANES)), sel)[0:1]
    alast = acs_x[L - 1:L]
    xdt = x * dt_x
    cb = _nt(cm, bm)
    lane = lax.broadcasted_iota(jnp.int32, (L, LANES), 1)
    ys = []
    for j in range(SSD_GS // LANES):
        xp = xdt[:, j * LANES:(j + 1) * LANES]
        yp = None
        for hh in range(2):
            c0 = (2 * j + hh) * SSD_P
            cmx = jnp.broadcast_to(acs_x[:, c0:c0 + 1], (L, L))
            dec = jnp.exp(jnp.where(causal, cmx - cmx.T, -jnp.inf))
            half = (lane < SSD_P) if hh == 0 else (lane >= SSD_P)
            t = _nn(cb * dec, jnp.where(half, xp, 0.0))
            yp = t if yp is None else yp + t
        ys.append(yp)
    y_diag = jnp.concatenate(ys, axis=1)
    st = _tn(bm, xdt * jnp.exp(alast - acs_x))
    new = prev * jnp.exp(alast) + st
    y_off = _nn(cm, prev) * jnp.exp(acs_x)
    y = y_diag + y_off + x * d_x
    yg = y * _silu(z)
    yn = yg * lax.rsqrt(jnp.mean(yg * yg, -1, keepdims=True) + RMS_EPS) * nw
    return yn, new


def _ssd_specs(nc, rev):
    cc = (lambda c: nc - 1 - c) if rev else (lambda c: c)
    zb = 0
    dtb = (SSD_DI + SSD_CONV_DIM) // LANES
    bb = SSD_DI // LANES
    cbk = (SSD_DI + SSD_G * SSD_N) // LANES
    par = pl.BlockSpec((1, LANES), lambda c, g: (0, 0))
    return dict(
        z=pl.BlockSpec((SSD_L, SSD_GS), lambda c, g: (cc(c), zb + g)),
        dt=pl.BlockSpec((SSD_L, LANES), lambda c, g: (cc(c), dtb)),
        x=pl.BlockSpec((SSD_L, SSD_GS), lambda c, g: (cc(c), g)),
        bm=pl.BlockSpec((SSD_L, LANES), lambda c, g: (cc(c), bb + g)),
        cm=pl.BlockSpec((SSD_L, LANES), lambda c, g: (cc(c), cbk + g)),
        par=par,
        nw=pl.BlockSpec((1, SSD_GS), lambda c, g: (0, g)),
        st=pl.BlockSpec((1, 1, SSD_N, SSD_GS), lambda c, g: (cc(c), g, 0, 0)),
        y=pl.BlockSpec((SSD_L, SSD_GS), lambda c, g: (cc(c), g)),
        bc=pl.BlockSpec((SSD_L, LANES), lambda c, g: (cc(c), g)),
        dtout=pl.BlockSpec((SSD_L, LANES), lambda c, g: (cc(c), 0)),
    )


def _ssd_fwd(proj, act, dtb, alog, dsk, nw, name):
    t = proj.shape[0]
    nc = t // SSD_L
    sp = _ssd_specs(nc, False)

    def body(z_ref, dt_ref, x_ref, bm_ref, cm_ref, dtb_ref, alog_ref, dsk_ref, nw_ref, y_ref, st_ref, state):
        c, g = pl.program_id(0), pl.program_id(1)

        @pl.when(c == 0)
        def _():
            state[g] = jnp.zeros((SSD_N, SSD_GS), F32)

        prev = state[g]
        st_ref[0, 0] = prev
        yn, new = _ssd_chunk(g, x_ref[...], z_ref[...], bm_ref[...], cm_ref[...], dt_ref[...], dtb_ref[...],
                             alog_ref[...], dsk_ref[...], nw_ref[...], prev)
        y_ref[...] = yn.astype(y_ref.dtype)
        state[g] = new

    return pl.pallas_call(
        body, name=name, grid=(nc, SSD_G),
        in_specs=[sp["z"], sp["dt"], sp["x"], sp["bm"], sp["cm"], sp["par"], sp["par"], sp["par"], sp["nw"]],
        out_specs=[sp["y"], sp["st"]],
        out_shape=[jax.ShapeDtypeStruct((t, SSD_DI), F32), jax.ShapeDtypeStruct((nc, SSD_G, SSD_N, SSD_GS), F32)],
        scratch_shapes=[pltpu.VMEM((SSD_G, SSD_N, SSD_GS), F32)],
        compiler_params=_cparams(("arbitrary", "arbitrary")),
    )(proj, proj, act, act, act, dtb, alog, dsk, nw)


def _ssd_bwd(proj, act, dtb, alog, dsk, nw, states, dyn, name):
    t = proj.shape[0]
    nc = t // SSD_L
    sp = _ssd_specs(nc, True)

    def body(z_ref, dt_ref, x_ref, bm_ref, cm_ref, dtb_ref, alog_ref, dsk_ref, nw_ref, st_ref, dy_ref,
             dx_ref, dz_ref, dbm_ref, dcm_ref, ddt_ref, ddtb_ref, dalog_ref, ddsk_ref, dnw_ref, dstate):
        c, g = pl.program_id(0), pl.program_id(1)

        @pl.when(c == 0)
        def _():
            dstate[g] = jnp.zeros((SSD_N, SSD_GS), F32)

        @pl.when((c == 0) & (g == 0))
        def _():
            ddtb_ref[...] = jnp.zeros_like(ddtb_ref)
            dalog_ref[...] = jnp.zeros_like(dalog_ref)
            ddsk_ref[...] = jnp.zeros_like(ddsk_ref)
            dnw_ref[...] = jnp.zeros_like(dnw_ref)

        @pl.when(g == 0)
        def _():
            ddt_ref[...] = jnp.zeros_like(ddt_ref)

        _, vjp = jax.vjp(functools.partial(_ssd_chunk, g), x_ref[...], z_ref[...], bm_ref[...], cm_ref[...],
                         dt_ref[...], dtb_ref[...], alog_ref[...], dsk_ref[...], nw_ref[...], st_ref[0, 0])
        dx, dz, dbm, dcm, ddt, ddtb, dalog, ddsk, dnw, dprev = vjp((dy_ref[...].astype(F32), dstate[g]))
        dx_ref[...] = dx
        dz_ref[...] = dz
        dbm_ref[...] = dbm
        dcm_ref[...] = dcm
        ddt_ref[...] += ddt
        ddtb_ref[...] += ddtb
        dalog_ref[...] += dalog
        ddsk_ref[...] += ddsk
        dnw_ref[g] += dnw
        dstate[g] = dprev

    par_out = pl.BlockSpec((1, LANES), lambda c, g: (0, 0))
    sds = jax.ShapeDtypeStruct
    return pl.pallas_call(
        body, name=name, grid=(nc, SSD_G),
        in_specs=[sp["z"], sp["dt"], sp["x"], sp["bm"], sp["cm"], sp["par"], sp["par"], sp["par"], sp["nw"],
                  sp["st"], sp["y"]],
        out_specs=[sp["y"], sp["y"], sp["bc"], sp["bc"], sp["dtout"], par_out, par_out, par_out,
                   pl.BlockSpec((SSD_G, 1, SSD_GS), lambda c, g: (0, 0, 0))],
        out_shape=[sds((t, SSD_DI), F32), sds((t, SSD_DI), F32), sds((t, SSD_G * SSD_N), F32), sds((t, SSD_G * SSD_N), F32),
                   sds((t, LANES), F32), sds((1, LANES), F32), sds((1, LANES), F32), sds((1, LANES), F32),
                   sds((SSD_G, 1, SSD_GS), F32)],
        scratch_shapes=[pltpu.VMEM((SSD_G, SSD_N, SSD_GS), F32)],
        compiler_params=_cparams(("arbitrary", "arbitrary")),
    )(proj, proj, act, act, act, dtb, alog, dsk, nw, states, dyn)


def _pad_lanes(v, width=LANES, offset=0):
    return jnp.pad(v.astype(F32), (offset, width - offset - v.shape[0])).reshape(1, width)


def _ssd_layer_fwd(u, w, ln_g, ln_b, tag, ride=None, late=None):
    w_in = jnp.concatenate([w["in_w"], jnp.zeros((D_MODEL, LANES - SSD_H), w["in_w"].dtype)], axis=1)
    dtb, alog, dsk = _pad_lanes(w["dt_bias"]), _pad_lanes(w["a_log"]), _pad_lanes(w["d"])
    proj = _mm(u, w_in, name=tag + "_in", ride=ride)
    proj, rode = (proj, []) if ride is None else proj
    if late is not None:
        more, rode = late(rode)
        w.update(more)
    nw = w["norm_w"].reshape(1, SSD_DI)
    cb = w["conv_b"].reshape(1, SSD_CONV_DIM)
    act = _conv_fwd(proj, SSD_DI, w["conv_w"], cb, name=tag + "_conv")
    yn, states = _ssd_fwd(proj, act, dtb, alog, dsk, nw, name=tag + "_scan")
    y = _mm(yn, w["out_w"], name=tag + "_out")
    h = _ln_fwd(u, y, ln_g, ln_b, name=tag + "_ln")
    saved = dict(u=u, w_in=w_in, proj=proj, act=act, states=states, yn=yn, y=y, dtb=dtb, alog=alog, dsk=dsk, nw=nw, cb=cb)
    return h, saved, rode


def _ssd_layer_bwd(s, w, ln_g, dr_up, du_up, tag, ride=None, own=None):
    dr, dg, db = _ln_bwd(s["u"], s["y"], ln_g, dr_up, du_up, name=tag + "_ln_b")
    dyn = _mm(dr, w["out_w"], tb=True, name=tag + "_out_bx")
    d_out_w = _mm(s["yn"], dr, ta=True, out_dtype=WIRE, name=tag + "_out_bw")
    dx, dz, dbm, dcm, ddt, ddtb, dalog, ddsk, dnw = _ssd_bwd(
        s["proj"], s["act"], s["dtb"], s["alog"], s["dsk"], s["nw"], s["states"], dyn, name=tag + "_scan_b")
    dact = jnp.concatenate([dx, dbm, dcm], axis=1)
    dpre, d_conv_w, d_conv_b = _conv_bwd_pre(s["proj"], SSD_DI, w["conv_w"], s["cb"], dact, name=tag + "_conv_bp")
    dxbc = _conv_bwd_x(dpre, w["conv_w"], name=tag + "_conv_bx")
    dproj = jnp.concatenate([dz, dxbc, ddt], axis=1)
    grads = dict(conv_w=d_conv_w, conv_b=d_conv_b.reshape(-1), dt_bias=ddtb[0, :SSD_H], a_log=dalog[0, :SSD_H],
                 d=ddsk[0, :SSD_H], norm_w=dnw.reshape(-1), out_w=d_out_w, ln_g=dg[0], ln_b=db[0])
    late_names = ("conv_w", "conv_b", "norm_w", "out_w")
    if own is not None:
        ride = ((ride[0] if ride else []) + [own(k, grads[k]) for k in late_names], True)
    d_in_w = _mm(s["u"], dproj, ta=True, out_dtype=WIRE, name=tag + "_in_bw", ride=ride)
    d_in_w, rode = (d_in_w, []) if ride is None else d_in_w
    grads["in_w"] = d_in_w[:, :SSD_DI + SSD_CONV_DIM + SSD_H]
    got = {}
    if own is not None:
        got = dict(zip(late_names, rode[len(rode) - len(late_names):]))
        rode = rode[:len(rode) - len(late_names)]
        du, (got["in_w"],) = _mm(dproj, s["w_in"], tb=True, name=tag + "_in_bx", ride=([own("in_w", grads["in_w"])], True))
    else:
        du = _mm(dproj, s["w_in"], tb=True, name=tag + "_in_bx")
    return dr, du, grads, rode, got


MLA_LOW = MLA_QR + MLA_KVR + LANES
MLA_ZB = MLA_LOW // LANES


def _rope_mat():
    r = lax.broadcasted_iota(jnp.int32, (LANES, LANES), 0)
    c = lax.broadcasted_iota(jnp.int32, (LANES, LANES), 1)
    hf = MLA_ROPE // 2
    return jnp.where((c < hf) & (r == c + hf), -1.0, 0.0) + jnp.where((c >= hf) & (c < 2 * hf) & (r == c - hf), 1.0, 0.0)


def _rope(x, cosf, sinf):
    return x * cosf + _hd(x, _rope_mat()) * sinf


def _rope_adj(d, cosf, sinf):
    return d * cosf - _hd(d * sinf, _rope_mat())


def _mla_low_fn(low, qnw, kvnw, cosf, sinf):
    qc, kvc, kr = low[:, :MLA_QR], low[:, MLA_QR:MLA_QR + MLA_KVR], low[:, MLA_QR + MLA_KVR:]
    qn = qc * lax.rsqrt(jnp.mean(qc * qc, -1, keepdims=True) + RMS_EPS) * qnw
    kvn = kvc * lax.rsqrt(jnp.mean(kvc * kvc, -1, keepdims=True) + RMS_EPS) * kvnw
    return qn, kvn, _rope(kr, cosf, sinf)


def _mla_low_fwd(proj, qnw, kvnw, cosf, sinf, name):
    t = proj.shape[0]
    tr = _tile(t, 256)

    def body(low_ref, qnw_ref, kvnw_ref, cos_ref, sin_ref, qn_ref, kvn_ref, kr_ref):
        qn, kvn, kr = _mla_low_fn(low_ref[...], qnw_ref[...], kvnw_ref[...], cos_ref[...], sin_ref[...])
        qn_ref[...] = qn
        kvn_ref[...] = kvn
        kr_ref[...] = kr

    row = lambda wdt: pl.BlockSpec((tr, wdt), lambda i: (i, 0))
    par = lambda wdt: pl.BlockSpec((1, wdt), lambda i: (0, 0))
    sds = jax.ShapeDtypeStruct
    return pl.pallas_call(
        body, name=name, grid=(t // tr,),
        in_specs=[row(MLA_LOW), par(MLA_QR), par(MLA_KVR), row(LANES), row(LANES)],
        out_specs=[row(MLA_QR), row(MLA_KVR), row(LANES)],
        out_shape=[sds((t, MLA_QR), F32), sds((t, MLA_KVR), F32), sds((t, LANES), F32)],
        compiler_params=_cparams(("parallel",)),
    )(proj, qnw, kvnw, cosf, sinf)


def _mla_low_bwd(proj, qnw, kvnw, cosf, sinf, dqn, dkvn, dkr, name):
    t = proj.shape[0]
    tr = _tile(t, 256)

    def body(low_ref, qnw_ref, kvnw_ref, cos_ref, sin_ref, dqn_ref, dkvn_ref, dkr_ref, dlow_ref, dqnw_ref, dkvnw_ref):
        i = pl.program_id(0)

        @pl.when(i == 0)
        def _():
            dqnw_ref[...] = jnp.zeros_like(dqnw_ref)
            dkvnw_ref[...] = jnp.zeros_like(dkvnw_ref)

        cosf, sinf = cos_ref[...], sin_ref[...]
        _, vjp = jax.vjp(lambda a, b, c: _mla_low_fn(a, b, c, cosf, sinf), low_ref[...], qnw_ref[...], kvnw_ref[...])
        dlow, dq, dk = vjp((dqn_ref[...], dkvn_ref[...], dkr_ref[...]))
        dlow_ref[...] = dlow
        dqnw_ref[...] += dq
        dkvnw_ref[...] += dk

    row = lambda wdt: pl.BlockSpec((tr, wdt), lambda i: (i, 0))
    par = lambda wdt: pl.BlockSpec((1, wdt), lambda i: (0, 0))
    sds = jax.ShapeDtypeStruct
    return pl.pallas_call(
        body, name=name, grid=(t // tr,),
        in_specs=[row(MLA_LOW), par(MLA_QR), par(MLA_KVR), row(LANES), row(LANES), row(MLA_QR), row(MLA_KVR), row(LANES)],
        out_specs=[row(MLA_LOW), par(MLA_QR), par(MLA_KVR)],
        out_shape=[sds((t, MLA_LOW), F32), sds((1, MLA_QR), F32), sds((1, MLA_KVR), F32)],
        compiler_params=_cparams(("arbitrary",)),
    )(proj, qnw, kvnw, cosf, sinf, dqn, dkvn, dkr)


def _rope_heads(x, col_blk0, cosf, sinf, adjoint, name):
    t = x.shape[0]
    tr = _tile(t, 512)

    def body(x_ref, cos_ref, sin_ref, o_ref):
        f = _rope_adj if adjoint else _rope
        o_ref[...] = f(x_ref[...], cos_ref[...], sin_ref[...])

    tab = pl.BlockSpec((tr, LANES), lambda i, h: (i, 0))
    return pl.pallas_call(
        body, name=name, grid=(t // tr, MLA_H),
        in_specs=[pl.BlockSpec((tr, LANES), lambda i, h: (i, col_blk0 + h)), tab, tab],
        out_specs=pl.BlockSpec((tr, LANES), lambda i, h: (i, h)),
        out_shape=jax.ShapeDtypeStruct((t, MLA_H * LANES), F32), compiler_params=_cparams(("parallel", "parallel")),
    )(x, cosf, sinf)


ATT_HB = 2
ATT_W = ATT_HB * LANES


def _att_scores(qn_ref, qr_ref, kn_ref, kr_ref, i, j, tq, tk):
    qn, qr, kn = _heads_of(qn_ref, ATT_HB), _heads_of(qr_ref, ATT_HB), _heads_of(kn_ref, ATT_HB)
    s_rope = _dg(_mx(qr.reshape(ATT_HB * tq, LANES)), _mx(kr_ref[...]), 1, 1).reshape(ATT_HB, tq, tk)
    s = (_bdg(_mx(qn), _mx(kn), 2, 2) + s_rope) * MLA_SCALE
    qpos = i * tq + lax.broadcasted_iota(jnp.int32, (tq, tk), 0)
    kpos = j * tk + lax.broadcasted_iota(jnp.int32, (tq, tk), 1)
    return jnp.where(kpos <= qpos, s, -jnp.inf), qn, qr, kn


def _put_heads(ref, val):
    for b in range(val.shape[0]):
        ref[:, b * LANES:(b + 1) * LANES] = val[b].astype(ref.dtype)


def _att_ds(s, v_ref, o_ref, do_ref, lse_ref):
    do = _heads_of(do_ref, ATT_HB)
    p = jnp.exp(s - _heads_of(lse_ref, ATT_HB)[:, :, 0:1])
    dp = _bdg(_mx(do), _mx(_heads_of(v_ref, ATT_HB)), 2, 2)
    dl = jnp.sum(do * _heads_of(o_ref, ATT_HB), -1, keepdims=True)
    return _mx(p), _mx(p * (dp - dl) * MLA_SCALE), do


def _attn_fwd(q, qr, kv, kr, proj, name, ride=None):
    t = q.shape[0]
    tq = tk = _tile(t, ATT_BLK)
    nq = nk = t // tq

    def body(qn_ref, qr_ref, kn_ref, kr_ref, v_ref, *rest):
        z_refs, (o_ref, og_ref, lse_ref, m_s, l_s, acc_s) = rest[:ATT_HB], rest[ATT_HB:]
        i, j = pl.program_id(1), pl.program_id(2)

        @pl.when(j == 0)
        def _():
            m_s[...] = jnp.full_like(m_s, -jnp.inf)
            l_s[...] = jnp.zeros_like(l_s)
            acc_s[...] = jnp.zeros_like(acc_s)

        @pl.when(j <= i)
        def _():
            s, _, _, _ = _att_scores(qn_ref, qr_ref, kn_ref, kr_ref, i, j, tq, tk)
            m_new = jnp.maximum(m_s[...], jnp.max(s, -1, keepdims=True))
            p = jnp.exp(s - m_new)
            corr = jnp.exp(m_s[...] - m_new)
            l_s[...] = corr * l_s[...] + jnp.sum(p, -1, keepdims=True)
            acc_s[...] = corr * acc_s[...] + _bdg(_mx(p), _mx(_heads_of(v_ref, ATT_HB)), 2, 1)
            m_s[...] = m_new

        @pl.when(j == nk - 1)
        def _():
            o = acc_s[...] / l_s[...]
            _put_heads(o_ref, o)
            lse = m_s[...] + jnp.log(l_s[...])
            for b in range(ATT_HB):
                og_ref[:, b * LANES:(b + 1) * LANES] = (o[b] * _silu(z_refs[b][...])).astype(og_ref.dtype)
                lse_ref[:, b * LANES:(b + 1) * LANES] = jnp.broadcast_to(lse[b], (tq, LANES))

    qs = lambda off: pl.BlockSpec((tq, ATT_W), lambda h, i, j: (i, off // ATT_HB + h))
    ks = lambda off: pl.BlockSpec((tk, ATT_W), lambda h, i, j: (jnp.minimum(j, i), off // ATT_HB + h))
    zs = [pl.BlockSpec((tq, LANES), functools.partial(lambda b, h, i, j: (i, MLA_ZB + h * ATT_HB + b), b)) for b in range(ATT_HB)]
    sds = jax.ShapeDtypeStruct
    return _hosted_call(
        body, name=name, grid=(MLA_H // ATT_HB, nq, nk),
        in_specs=[qs(0), qs(0), ks(0), pl.BlockSpec((tk, LANES), lambda h, i, j: (jnp.minimum(j, i), 0)), ks(MLA_H)] + zs,
        out_specs=[qs(0), qs(0), qs(0)],
        out_shape=[sds((t, MLA_GATE), F32), sds((t, MLA_GATE), F32), sds((t, MLA_H * LANES), F32)],
        scratch_shapes=[pltpu.VMEM((ATT_HB, tq, 1), F32), pltpu.VMEM((ATT_HB, tq, 1), F32), pltpu.VMEM((ATT_HB, tq, LANES), F32)],
        sem=("parallel", "parallel", "arbitrary"), args=(q, qr, kv, kr, kv) + (proj,) * ATT_HB, ride=ride)


def _gate_bwd(dog, o, proj, name):
    t = o.shape[0]
    tr = _tile(t, 512)

    def body(d_ref, o_ref, z_ref, do_ref, dz_ref):
        z = z_ref[...]
        sg = jax.nn.sigmoid(z)
        d = d_ref[...]
        do_ref[...] = d * z * sg
        dz_ref[...] = d * o_ref[...] * (sg * (1.0 + z * (1.0 - sg)))

    blk = lambda off: pl.BlockSpec((tr, 512), lambda i, j: (i, off + j))
    assert MLA_LOW % 512 != 0 or True
    zspec = pl.BlockSpec((tr, LANES), lambda i, j: (i, MLA_ZB + j))
    b128 = pl.BlockSpec((tr, LANES), lambda i, j: (i, j))
    sds = jax.ShapeDtypeStruct
    return pl.pallas_call(
        body, name=name, grid=(t // tr, MLA_GATE // LANES),
        in_specs=[b128, b128, zspec], out_specs=[b128, b128],
        out_shape=[sds((t, MLA_GATE), F32), sds((t, MLA_GATE), F32)],
        compiler_params=_cparams(("parallel", "parallel")),
    )(dog, o, proj)


def _attn_bwd_q(q, qr, kv, kr, o, do, lse, name, ride=None):
    t = q.shape[0]
    tq = tk = _tile(t, ATT_BLK)
    nq = nk = t // tq

    def body(qn_ref, qr_ref, kn_ref, kr_ref, v_ref, o_ref, do_ref, lse_ref, dqn_ref, dqr_ref, an_s, ar_s):
        i, j = pl.program_id(1), pl.program_id(2)

        @pl.when(j == 0)
        def _():
            an_s[...] = jnp.zeros_like(an_s)
            ar_s[...] = jnp.zeros_like(ar_s)

        @pl.when(j <= i)
        def _():
            s, _, _, kn = _att_scores(qn_ref, qr_ref, kn_ref, kr_ref, i, j, tq, tk)
            ds = _att_ds(s, v_ref, o_ref, do_ref, lse_ref)[1]
            an_s[...] += _bdg(ds, _mx(kn), 2, 1)
            ar_s[...] += _dg(ds.reshape(ATT_HB * tq, tk), _mx(kr_ref[...]), 1, 0).reshape(ATT_HB, tq, LANES)

        @pl.when(j == nk - 1)
        def _():
            _put_heads(dqn_ref, an_s[...])
            _put_heads(dqr_ref, ar_s[...])

    qs = lambda off: pl.BlockSpec((tq, ATT_W), lambda h, i, j: (i, off // ATT_HB + h))
    ks = lambda off: pl.BlockSpec((tk, ATT_W), lambda h, i, j: (jnp.minimum(j, i), off // ATT_HB + h))
    sds = jax.ShapeDtypeStruct
    return _hosted_call(
        body, name=name, grid=(MLA_H // ATT_HB, nq, nk),
        in_specs=[qs(0), qs(0), ks(0), pl.BlockSpec((tk, LANES), lambda h, i, j: (jnp.minimum(j, i), 0)), ks(MLA_H),
                  qs(0), qs(0), qs(0)],
        out_specs=[qs(0), qs(0)],
        out_shape=[sds((t, MLA_H * LANES), F32), sds((t, MLA_H * LANES), F32)],
        scratch_shapes=[pltpu.VMEM((ATT_HB, tq, LANES), F32), pltpu.VMEM((ATT_HB, tq, LANES), F32)],
        sem=("parallel", "parallel", "arbitrary"), args=(q, qr, kv, kr, kv, o, do, lse), ride=ride)


def _attn_bwd_kv(q, qr, kv, kr, o, do, lse, name):
    t = q.shape[0]
    tq = tk = _tile(t, ATT_BLK)
    nq = nk = t // tq

    def body(qn_ref, qr_ref, kn_ref, kr_ref, v_ref, o_ref, do_ref, lse_ref, dkn_ref, dv_ref, dkr_ref, akn_s, av_s):
        j, h, i = pl.program_id(0), pl.program_id(1), pl.program_id(2)

        @pl.when((h == 0) & (i == 0))
        def _():
            dkr_ref[...] = jnp.zeros_like(dkr_ref)

        @pl.when(i == 0)
        def _():
            akn_s[...] = jnp.zeros_like(akn_s)
            av_s[...] = jnp.zeros_like(av_s)

        @pl.when(i >= j)
        def _():
            s, qn, qr, _ = _att_scores(qn_ref, qr_ref, kn_ref, kr_ref, i, j, tq, tk)
            p, ds, do = _att_ds(s, v_ref, o_ref, do_ref, lse_ref)
            av_s[...] += _bdg(p, _mx(do), 1, 1)
            akn_s[...] += _bdg(ds, _mx(qn), 1, 1)
            dkr_ref[...] += _dg(ds.reshape(ATT_HB * tq, tk), _mx(qr.reshape(ATT_HB * tq, LANES)), 0, 0)

        @pl.when(i == nq - 1)
        def _():
            _put_heads(dkn_ref, akn_s[...])
            _put_heads(dv_ref, av_s[...])

    qs = lambda off: pl.BlockSpec((tq, ATT_W), lambda j, h, i: (jnp.maximum(i, j), off // ATT_HB + h))
    ks = lambda off: pl.BlockSpec((tk, ATT_W), lambda j, h, i: (j, off // ATT_HB + h))
    sds = jax.ShapeDtypeStruct
    return pl.pallas_call(
        body, name=name, grid=(nk, MLA_H // ATT_HB, nq),
        in_specs=[qs(0), qs(0), ks(0), pl.BlockSpec((tk, LANES), lambda j, h, i: (j, 0)), ks(MLA_H), qs(0), qs(0), qs(0)],
        out_specs=[ks(0), ks(0), pl.BlockSpec((tk, LANES), lambda j, h, i: (j, 0))],
        out_shape=[sds((t, MLA_H * LANES), F32), sds((t, MLA_H * LANES), F32), sds((t, LANES), F32)],
        scratch_shapes=[pltpu.VMEM((ATT_HB, tk, LANES), F32), pltpu.VMEM((ATT_HB, tk, LANES), F32)],
        compiler_params=_cparams(("parallel", "arbitrary", "arbitrary")),
    )(q, qr, kv, kr, kv, o, do, lse)


def _rope_tables(positions):
    inv_freq = ROPE_THETA ** (-jnp.arange(0, MLA_ROPE, 2, dtype=F32) / MLA_ROPE)
    ang = positions.astype(F32)[:, None] * inv_freq
    pad = jnp.zeros((positions.shape[0], LANES - MLA_ROPE), F32)
    cos, sin = jnp.cos(ang), jnp.sin(ang)
    return jnp.concatenate([cos, cos, pad], 1), jnp.concatenate([sin, sin, pad], 1)


def _mla_weights(w):
    dt = w["in_w"].dtype
    iw = w["in_w"]
    c1 = MLA_QR + MLA_KVR + MLA_ROPE
    w_in = jnp.concatenate([iw[:, :c1], jnp.zeros((D_MODEL, LANES - MLA_ROPE), dt), iw[:, c1:]], axis=1)
    qu = w["q_up_w"].reshape(MLA_QR, MLA_H, MLA_NOPE + MLA_ROPE)
    qrope = jnp.concatenate([qu[:, :, MLA_NOPE:], jnp.zeros((MLA_QR, MLA_H, LANES - MLA_ROPE), dt)], axis=2)
    w_q = jnp.concatenate([qu[:, :, :MLA_NOPE].reshape(MLA_QR, -1), qrope.reshape(MLA_QR, -1)], axis=1)
    kvu = w["kv_up_w"].reshape(MLA_KVR, MLA_H, MLA_NOPE + MLA_V)
    w_kv = jnp.concatenate([kvu[:, :, :MLA_NOPE].reshape(MLA_KVR, -1), kvu[:, :, MLA_NOPE:].reshape(MLA_KVR, -1)], axis=1)
    return w_in, w_q, w_kv


def _mla_layer_fwd(u, w, ln_g, ln_b, cosf, sinf, tag, ride=None):
    w_in, w_q, w_kv = _mla_weights(w)
    qnw, kvnw = w["q_norm_w"].reshape(1, -1), w["kv_norm_w"].reshape(1, -1)
    proj = _mm(u, w_in, name=tag + "_in")
    qn, kvn, kr = _mla_low_fwd(proj, qnw, kvnw, cosf, sinf, name=tag + "_low")
    q = _mm(qn, w_q, name=tag + "_qup")
    kv = _mm(kvn, w_kv, name=tag + "_kvup")
    qr = _rope_heads(q, MLA_H, cosf, sinf, False, name=tag + "_qrope")
    (o, og, lse), rode = _attn_fwd(q, qr, kv, kr, proj, name=tag + "_attn", ride=ride)
    y = _mm(og, w["out_w"], name=tag + "_out")
    h = _ln_fwd(u, y, ln_g, ln_b, name=tag + "_ln")
    saved = dict(u=u, w_in=w_in, w_q=w_q, w_kv=w_kv, qnw=qnw, kvnw=kvnw, proj=proj, qn=qn, kvn=kvn, kr=kr, q=q, kv=kv,
                 qr=qr, o=o, og=og, lse=lse, y=y)
    return h, saved, rode


def _mla_layer_bwd(s, w, ln_g, cosf, sinf, dr_up, du_up, tag, ride=None):
    dr, dg, db = _ln_bwd(s["u"], s["y"], ln_g, dr_up, du_up, name=tag + "_ln_b")
    dog = _mm(dr, w["out_w"], tb=True, name=tag + "_out_bx")
    d_out_w = _mm(s["og"], dr, ta=True, out_dtype=WIRE, name=tag + "_out_bw")
    do, dz = _gate_bwd(dog, s["o"], s["proj"], name=tag + "_gate_b")
    (dqn_h, dqr_rot), rode = _attn_bwd_q(s["q"], s["qr"], s["kv"], s["kr"], s["o"], do, s["lse"], name=tag + "_attn_bq",
                                         ride=ride)
    dkn_h, dv_h, dkr_rot = _attn_bwd_kv(s["q"], s["qr"], s["kv"], s["kr"], s["o"], do, s["lse"], name=tag + "_attn_bkv")
    dqr = _rope_heads(dqr_rot, 0, cosf, sinf, True, name=tag + "_qrope_b")
    dq = jnp.concatenate([dqn_h, dqr], axis=1)
    dkv = jnp.concatenate([dkn_h, dv_h], axis=1)
    d_wq = _mm(s["qn"], dq, ta=True, out_dtype=WIRE, name=tag + "_qup_bw")
    dqn = _mm(dq, s["w_q"], tb=True, name=tag + "_qup_bx")
    d_wkv = _mm(s["kvn"], dkv, ta=True, out_dtype=WIRE, name=tag + "_kvup_bw")
    dkvn = _mm(dkv, s["w_kv"], tb=True, name=tag + "_kvup_bx")
    dlow, dqnw, dkvnw = _mla_low_bwd(s["proj"], s["qnw"], s["kvnw"], cosf, sinf, dqn, dkvn, dkr_rot, name=tag + "_low_b")
    dproj = jnp.concatenate([dlow, dz], axis=1)
    d_in = _mm(s["u"], dproj, ta=True, out_dtype=WIRE, name=tag + "_in_bw")
    du = _mm(dproj, s["w_in"], tb=True, name=tag + "_in_bx")
    c1 = MLA_QR + MLA_KVR + MLA_ROPE
    d_in_w = jnp.concatenate([d_in[:, :c1], d_in[:, MLA_LOW:]], axis=1)
    dq3n = d_wq[:, :MLA_H * MLA_NOPE].reshape(MLA_QR, MLA_H, MLA_NOPE)
    dq3r = d_wq[:, MLA_H * MLA_NOPE:].reshape(MLA_QR, MLA_H, LANES)[:, :, :MLA_ROPE]
    d_q_up = jnp.concatenate([dq3n, dq3r], axis=2).reshape(MLA_QR, -1)
    dkv3 = d_wkv.reshape(MLA_KVR, 2, MLA_H, MLA_NOPE)
    d_kv_up = jnp.concatenate([dkv3[:, 0], dkv3[:, 1]], axis=2).reshape(MLA_KVR, -1)
    grads = dict(in_w=d_in_w, q_norm_w=dqnw[0], q_up_w=d_q_up, kv_norm_w=dkvnw[0], kv_up_w=d_kv_up, out_w=d_out_w,
                 ln_g=dg[0], ln_b=db[0])
    return dr, du, grads, rode


GDN_REP = GDN_HV // GDN_HK
GDN_A_LANE = GDN_HV
GDN_HPB = 4
GDN_VPB = GDN_HPB * GDN_REP


def _bdg(a, b, ca, cb):
    return lax.dot_general(a, b, (((ca,), (cb,)), ((0,), (0,))), preferred_element_type=F32)


@jax.custom_vjp
def _bnn(a, b):
    return _bdg(_mx(a), _mx(b), 2, 1)


def _bnn_f(a, b):
    return _bnn(a, b), (a, b)


def _bnn_b(res, ct):
    a, b = res
    return _bdg(_mx(ct), _mx(b), 2, 2), _bdg(_mx(a), _mx(ct), 1, 1)


_bnn.defvjp(_bnn_f, _bnn_b)


@jax.custom_vjp
def _bnt(a, b):
    return _bdg(_mx(a), _mx(b), 2, 2)


def _bnt_f(a, b):
    return _bnt(a, b), (a, b)


def _bnt_b(res, ct):
    a, b = res
    return _bdg(_mx(ct), _mx(b), 2, 1), _bdg(_mx(ct), _mx(a), 1, 1)


_bnt.defvjp(_bnt_f, _bnt_b)


@jax.custom_vjp
def _btn(a, b):
    return _bdg(_mx(a), _mx(b), 1, 1)


def _btn_f(a, b):
    return _btn(a, b), (a, b)


def _btn_b(res, ct):
    a, b = res
    return _bdg(_mx(b), _mx(ct), 2, 2), _bdg(_mx(a), _mx(ct), 2, 1)


_btn.defvjp(_btn_f, _btn_b)


def _h3(a, b, ca=2, cb=1):
    ah, bh = _mx(a), _mx(b)
    al, bl = _mx(a - ah.astype(F32)), _mx(b - bh.astype(F32))
    return _bdg(ah, bh, ca, cb) + (_bdg(ah, bl, ca, cb) + _bdg(al, bh, ca, cb))


@jax.custom_vjp
def _neumann_inverse(x):
    L = x.shape[-1]
    eye = (lax.broadcasted_iota(jnp.int32, (L, L), 0) == lax.broadcasted_iota(jnp.int32, (L, L), 1)).astype(F32)
    inv = eye + x
    xp = x
    for _ in range(L.bit_length() - 2):
        xp = _h3(xp, xp)
        inv = inv + _h3(inv, xp)
    return inv


def _neumann_f(x):
    inv = _neumann_inverse(x)
    return inv, inv


def _neumann_b(inv, ct):
    return (_h3(_h3(inv, ct, 1, 1), inv, 2, 2),)


_neumann_inverse.defvjp(_neumann_f, _neumann_b)


def _cat0(parts):
    return jnp.concatenate([p[None] for p in parts], axis=0)


def _gdn_chunk(hb, q, k, v, z, ba, alog, dtb, nw, s):
    L = q.shape[1]
    r_i = lax.broadcasted_iota(jnp.int32, (L, L), 0)
    c_i = lax.broadcasted_iota(jnp.int32, (L, L), 1)
    incl, strict = r_i >= c_i, r_i > c_i
    rep = lambda t: jnp.broadcast_to(t[:, None], (GDN_HPB, GDN_REP) + t.shape[1:]).reshape((GDN_VPB,) + t.shape[1:])
    qn = rep(q * lax.rsqrt(jnp.sum(q * q, -1, keepdims=True) + RMS_EPS) * (GDN_DK ** -0.5))
    kn = rep(k * lax.rsqrt(jnp.sum(k * k, -1, keepdims=True) + RMS_EPS))
    beta_all = jax.nn.sigmoid(ba)
    g_all = -jnp.exp(alog) * _softplus(ba + dtb)
    gcs_all = _hd(incl.astype(F32), g_all)
    lane = lax.broadcasted_iota(jnp.int32, (L, LANES), 1)
    pick = lambda mat, idx: jnp.sum(jnp.where(lane == idx, mat, 0.0), axis=1, keepdims=True)
    beta = _cat0([pick(beta_all, GDN_VPB * hb + b) for b in range(GDN_VPB)])
    gc = _cat0([pick(gcs_all, GDN_A_LANE + GDN_VPB * hb + b) for b in range(GDN_VPB)])
    gm = jnp.broadcast_to(gc, (GDN_VPB, L, L))
    decay = jnp.exp(jnp.where(incl, gm - jnp.swapaxes(gm, 1, 2), -jnp.inf))
    kb = kn * beta
    eg = jnp.exp(gc)
    inv = _neumann_inverse(-jnp.where(strict, _bnt(kb, kn) * decay, 0.0))
    uw = _bnn(inv, jnp.concatenate([v * beta, kb * eg], axis=2))
    uu, ww = uw[:, :, :GDN_DV], uw[:, :, GDN_DV:]
    qk = jnp.where(incl, _bnt(qn, kn) * decay, 0.0)
    last = lax.broadcasted_iota(jnp.int32, (L, 1), 0) == L - 1
    glast = jnp.sum(jnp.where(last, gc, 0.0), axis=1, keepdims=True)
    kdec = kn * jnp.exp(glast - gc)
    vnew = uu - _bnn(ww, s)
    o = _bnn(qn * eg, s) + _bnn(qk, vnew)
    new = s * jnp.exp(glast) + _btn(kdec, vnew)
    on = o * lax.rsqrt(jnp.mean(o * o, -1, keepdims=True) + RMS_EPS) * nw * _silu(z)
    return on, new


def _heads_of(ref, n):
    return _cat0([ref[:, i * LANES:(i + 1) * LANES] for i in range(n)])


def _gdn_specs(nc, rev):
    cc = (lambda c: nc - 1 - c) if rev else (lambda c: c)
    wq, wv = GDN_HPB * GDN_DK, GDN_VPB * GDN_DV
    par = pl.BlockSpec((1, LANES), lambda c, h: (0, 0))
    return dict(
        q=pl.BlockSpec((GDN_L, wq), lambda c, h: (cc(c), h)),
        k=pl.BlockSpec((GDN_L, wq), lambda c, h: (cc(c), GDN_KEY // wq + h)),
        v=pl.BlockSpec((GDN_L, wv), lambda c, h: (cc(c), 2 * GDN_KEY // wv + h)),
        z=pl.BlockSpec((GDN_L, wv), lambda c, h: (cc(c), GDN_CONV_DIM // wv + h)),
        ba=pl.BlockSpec((GDN_L, LANES), lambda c, h: (cc(c), (GDN_CONV_DIM + GDN_VAL) // LANES)),
        par=par,
        st=pl.BlockSpec((1, GDN_VPB, GDN_DK, GDN_DV), lambda c, h: (cc(c), h, 0, 0)),
        o=pl.BlockSpec((GDN_L, wv), lambda c, h: (cc(c), h)),
        qk_out=pl.BlockSpec((GDN_L, wq), lambda c, h: (cc(c), h)),
        ba_out=pl.BlockSpec((GDN_L, LANES), lambda c, h: (cc(c), 0)),
    )


def _gdn_fwd(proj, act, alog, dtb, nw, name, ride=None):
    t = proj.shape[0]
    nc = t // GDN_L
    sp = _gdn_specs(nc, False)

    def body(q_ref, k_ref, v_ref, z_ref, ba_ref, alog_ref, dtb_ref, nw_ref, o_ref, st_ref, state):
        c, h = pl.program_id(0), pl.program_id(1)

        mine = pl.ds(h * GDN_VPB, GDN_VPB)

        @pl.when(c == 0)
        def _():
            state[mine] = jnp.zeros((GDN_VPB, GDN_DK, GDN_DV), F32)

        prev = state[mine]
        st_ref[0] = prev
        on, new = _gdn_chunk(h, _heads_of(q_ref, GDN_HPB), _heads_of(k_ref, GDN_HPB), _heads_of(v_ref, GDN_VPB),
                             _heads_of(z_ref, GDN_VPB), ba_ref[...], alog_ref[...], dtb_ref[...], nw_ref[...], prev)
        for b in range(GDN_VPB):
            o_ref[:, b * LANES:(b + 1) * LANES] = on[b].astype(o_ref.dtype)
        state[mine] = new

    sds = jax.ShapeDtypeStruct
    return _hosted_call(
        body, name=name, grid=(nc, GDN_HK // GDN_HPB),
        in_specs=[sp["q"], sp["k"], sp["v"], sp["z"], sp["ba"], sp["par"], sp["par"], sp["par"]],
        out_specs=[sp["o"], sp["st"]],
        out_shape=[sds((t, GDN_VAL), F32), sds((nc, GDN_HV, GDN_DK, GDN_DV), F32)],
        scratch_shapes=[pltpu.VMEM((GDN_HV, GDN_DK, GDN_DV), F32)],
        sem=("arbitrary", "arbitrary"), args=(act, act, act, proj, proj, alog, dtb, nw), ride=ride)


def _gdn_bwd(proj, act, alog, dtb, nw, states, don, name, ride=None):
    t = proj.shape[0]
    nc = t // GDN_L
    sp = _gdn_specs(nc, True)

    def body(q_ref, k_ref, v_ref, z_ref, ba_ref, alog_ref, dtb_ref, nw_ref, st_ref, do_ref,
             dq_ref, dk_ref, dv_ref, dz_ref, dba_ref, dalog_ref, ddtb_ref, dnw_ref, dstate):
        c, h = pl.program_id(0), pl.program_id(1)

        mine = pl.ds(h * GDN_VPB, GDN_VPB)

        @pl.when(c == 0)
        def _():
            dstate[mine] = jnp.zeros((GDN_VPB, GDN_DK, GDN_DV), F32)

        @pl.when((c == 0) & (h == 0))
        def _():
            dalog_ref[...] = jnp.zeros_like(dalog_ref)
            ddtb_ref[...] = jnp.zeros_like(ddtb_ref)
            dnw_ref[...] = jnp.zeros_like(dnw_ref)

        @pl.when(h == 0)
        def _():
            dba_ref[...] = jnp.zeros_like(dba_ref)

        _, vjp = jax.vjp(functools.partial(_gdn_chunk, h), _heads_of(q_ref, GDN_HPB), _heads_of(k_ref, GDN_HPB),
                         _heads_of(v_ref, GDN_VPB), _heads_of(z_ref, GDN_VPB), ba_ref[...], alog_ref[...], dtb_ref[...],
                         nw_ref[...], st_ref[0])
        dq, dk, dv, dz, dba, dalog, ddtb, dnw, dprev = vjp((_heads_of(do_ref, GDN_VPB), dstate[mine]))
        for i in range(GDN_HPB):
            dq_ref[:, i * LANES:(i + 1) * LANES] = dq[i]
            dk_ref[:, i * LANES:(i + 1) * LANES] = dk[i]
        for b in range(GDN_VPB):
            dv_ref[:, b * LANES:(b + 1) * LANES] = dv[b]
            dz_ref[:, b * LANES:(b + 1) * LANES] = dz[b]
        dba_ref[...] += dba
        dalog_ref[...] += dalog
        ddtb_ref[...] += ddtb
        dnw_ref[...] += dnw
        dstate[mine] = dprev

    sds = jax.ShapeDtypeStruct
    par_out = pl.BlockSpec((1, LANES), lambda c, h: (0, 0))
    return _hosted_call(
        body, name=name, grid=(nc, GDN_HK // GDN_HPB),
        in_specs=[sp["q"], sp["k"], sp["v"], sp["z"], sp["ba"], sp["par"], sp["par"], sp["par"], sp["st"], sp["o"]],
        out_specs=[sp["qk_out"], sp["qk_out"], sp["o"], sp["o"], sp["ba_out"], par_out, par_out, par_out],
        out_shape=[sds((t, GDN_KEY), F32), sds((t, GDN_KEY), F32), sds((t, GDN_VAL), F32), sds((t, GDN_VAL), F32),
                   sds((t, LANES), F32), sds((1, LANES), F32), sds((1, LANES), F32), sds((1, LANES), F32)],
        scratch_shapes=[pltpu.VMEM((GDN_HV, GDN_DK, GDN_DV), F32)],
        sem=("arbitrary", "arbitrary"), args=(act, act, act, proj, proj, alog, dtb, nw, states, don), ride=ride)


GDN_PROJ = GDN_CONV_DIM + GDN_VAL + 2 * GDN_HV


def _gdn_layer_fwd(u, w, ln_g, ln_b, tag, ride=None):
    w_in = jnp.concatenate([w["in_w"], jnp.zeros((D_MODEL, GDN_PROJ_PAD - GDN_PROJ), w["in_w"].dtype)], axis=1)
    alog = _pad_lanes(w["a_log"], offset=GDN_A_LANE)
    dtb = _pad_lanes(w["dt_bias"], offset=GDN_A_LANE)
    nw = w["norm_w"].reshape(1, GDN_DV)
    zb = jnp.zeros((1, GDN_CONV_DIM), F32)
    proj = _mm(u, w_in, name=tag + "_in")
    act = _conv_fwd(proj, 0, w["conv_w"], zb, name=tag + "_conv")
    (on, states), rode = _gdn_fwd(proj, act, alog, dtb, nw, name=tag + "_delta", ride=ride)
    y = _mm(on, w["out_w"], name=tag + "_out")
    h = _ln_fwd(u, y, ln_g, ln_b, name=tag + "_ln")
    saved = dict(u=u, w_in=w_in, proj=proj, act=act, states=states, on=on, y=y, alog=alog, dtb=dtb, nw=nw, zb=zb)
    return h, saved, rode


def _gdn_layer_bwd(s, w, ln_g, dr_up, du_up, tag, ride=None):
    t = s["u"].shape[0]
    dr, dg, db = _ln_bwd(s["u"], s["y"], ln_g, dr_up, du_up, name=tag + "_ln_b")
    don = _mm(dr, w["out_w"], tb=True, name=tag + "_out_bx")
    d_out_w = _mm(s["on"], dr, ta=True, out_dtype=WIRE, name=tag + "_out_bw")
    (dq, dk, dv, dz, dba, dalog, ddtb, dnw), rode = _gdn_bwd(s["proj"], s["act"], s["alog"], s["dtb"], s["nw"], s["states"],
                                                            don, name=tag + "_delta_b", ride=ride)
    dact = jnp.concatenate([dq, dk, dv], axis=1)
    dpre, d_conv_w, _ = _conv_bwd_pre(s["proj"], 0, w["conv_w"], s["zb"], dact, name=tag + "_conv_bp")
    dqkv = _conv_bwd_x(dpre, w["conv_w"], name=tag + "_conv_bx")
    dproj = jnp.concatenate([dqkv, dz, dba, jnp.zeros((t, GDN_PROJ_PAD - GDN_PROJ - (LANES - 2 * GDN_HV)), F32)], axis=1)
    d_in = _mm(s["u"], dproj, ta=True, out_dtype=WIRE, name=tag + "_in_bw")
    du = _mm(dproj, s["w_in"], tb=True, name=tag + "_in_bx")
    grads = dict(in_w=d_in[:, :GDN_PROJ], conv_w=d_conv_w, a_log=dalog[0, GDN_A_LANE:GDN_A_LANE + GDN_HV],
                 dt_bias=ddtb[0, GDN_A_LANE:GDN_A_LANE + GDN_HV], norm_w=dnw[0], out_w=d_out_w, ln_g=dg[0], ln_b=db[0])
    return dr, du, grads, rode


def _mesh_pos():
    return lax.axis_index("x"), lax.axis_index("y"), lax.axis_index("c")


def _peer(k, x, y, c):
    return ((1 - x) if k & 4 else x, (1 - y) if k & 2 else y, (1 - c) if k & 1 else c)


def _ride_copies(ins, outs, send, recv, loc, scatter, with_arrivals):
    n = len(ins)
    x, y, c = _mesh_pos()
    me = 4 * x + 2 * y + c
    local = [pltpu.make_async_copy(ins[i].at[me] if scatter else ins[i], outs[i].at[me], loc.at[i]) for i in range(n)]
    sends, arrivals = [], []
    for k in range(1, N_DEV):
        peer = _peer(k, x, y, c)
        pidx = 4 * peer[0] + 2 * peer[1] + peer[2]
        for i in range(n):
            src = ins[i].at[pidx] if scatter else ins[i]
            sems = dict(send_sem=send.at[i, k - 1], recv_sem=recv.at[i, k - 1], device_id=peer,
                        device_id_type=pl.DeviceIdType.MESH)
            sends.append(pltpu.make_async_remote_copy(src_ref=src, dst_ref=outs[i].at[me], **sems))
            if with_arrivals:
                arrivals.append(pltpu.make_async_remote_copy(src_ref=src, dst_ref=outs[i].at[pidx], **sems))
    return local, sends, arrivals


def _ride_start(ins, outs, send, recv, loc, scatter):
    local, sends, _ = _ride_copies(ins, outs, send, recv, loc, scatter, False)
    for cp in local + sends:
        cp.start()


def _ride_wait(ins, outs, send, recv, loc, scatter):
    local, sends, arrivals = _ride_copies(ins, outs, send, recv, loc, scatter, True)
    for cp in arrivals:
        cp.wait_recv()
    for cp in sends:
        cp.wait_send()
    for cp in local:
        cp.wait()


def _ride_shapes(arrs, scatter):
    n = len(arrs)
    out_shape = [jax.ShapeDtypeStruct(a.shape if scatter else (N_DEV,) + a.shape, a.dtype) for a in arrs]
    scratch = [pltpu.SemaphoreType.DMA((n, N_DEV - 1)), pltpu.SemaphoreType.DMA((n, N_DEV - 1)), pltpu.SemaphoreType.DMA((n,))]
    return out_shape, scratch


def _exchange(arrs, scatter, name):
    n = len(arrs)
    hbm = pl.BlockSpec(memory_space=pltpu.HBM)

    def body(*refs):
        ins, outs = refs[:n], refs[n:2 * n]
        _ride_start(ins, outs, *refs[2 * n:], scatter)
        _ride_wait(ins, outs, *refs[2 * n:], scatter)

    out_shape, scratch = _ride_shapes(arrs, scatter)
    return pl.pallas_call(
        body, name=name, in_specs=[hbm] * n, out_specs=[hbm] * n, out_shape=out_shape, scratch_shapes=scratch,
        compiler_params=pltpu.CompilerParams(has_side_effects=True),
    )(*arrs)


def _hosted_call(body, *, name, grid, in_specs, out_specs, out_shape, scratch_shapes, sem, args, ride=None):
    if ride is None:
        return pl.pallas_call(body, name=name, grid=grid, in_specs=in_specs, out_specs=out_specs, out_shape=out_shape,
                              scratch_shapes=scratch_shapes, compiler_params=_cparams(sem))(*args), []
    arrs, scatter = ride
    n, ni, no, ns = len(arrs), len(in_specs), len(out_specs), len(scratch_shapes)
    hbm = pl.BlockSpec(memory_space=pltpu.HBM)
    r_shape, r_scratch = _ride_shapes(arrs, scatter)

    def full(*refs):
        a, ri = refs[:ni], refs[ni:ni + n]
        o, ro = refs[ni + n:ni + n + no], refs[ni + n + no:ni + 2 * n + no]
        s, rs = refs[ni + 2 * n + no:ni + 2 * n + no + ns], refs[ni + 2 * n + no + ns:]
        ids = [pl.program_id(d) for d in range(len(grid))]
        first, last = ids[0] == 0, ids[0] == grid[0] - 1
        for d in range(1, len(grid)):
            first, last = first & (ids[d] == 0), last & (ids[d] == grid[d] - 1)

        @pl.when(first)
        def _():
            _ride_start(ri, ro, *rs, scatter)

        body(*a, *o, *s)

        @pl.when(last)
        def _():
            _ride_wait(ri, ro, *rs, scatter)

    outs = pl.pallas_call(
        full, name=name, grid=grid, in_specs=list(in_specs) + [hbm] * n, out_specs=list(out_specs) + [hbm] * n,
        out_shape=list(out_shape) + r_shape, scratch_shapes=list(scratch_shapes) + r_scratch,
        compiler_params=pltpu.CompilerParams(dimension_semantics=("arbitrary",) * len(grid), vmem_limit_bytes=VMEM_LIMIT,
                                             has_side_effects=True),
    )(*args, *arrs)
    return outs[:no], list(outs[no:])


def _unshard(g, ax):
    g = jnp.moveaxis(g, 0, ax)
    sh = g.shape
    return g.reshape(sh[:ax] + (sh[ax] * sh[ax + 1],) + sh[ax + 2:])


def _to_parts(full, ax):
    sh = full.shape
    full = full.reshape(sh[:ax] + (N_DEV, sh[ax] // N_DEV) + sh[ax + 1:])
    return jnp.moveaxis(full, ax, 0)


def _row_tile(r, c):
    cap = max(8, (256 * 1024) // max(c, 1))
    best = None
    for d in range(8, min(r, cap) + 1, 8):
        if r % d == 0:
            best = d
    return r if best is None else best


def _adamw(w, m, v, parts, name):
    r, c = w.shape
    tr = _row_tile(r, c)

    def body(w_ref, m_ref, v_ref, p_ref, g_ref, d_ref, nm_ref, nv_ref):
        g = p_ref[0].astype(F32)
        for q in range(1, N_DEV):
            g = g + p_ref[q].astype(F32)
        nm = ADAM_B1 * m_ref[...] + (1.0 - ADAM_B1) * g
        nv = ADAM_B2 * v_ref[...] + (1.0 - ADAM_B2) * (g * g)
        m_hat = nm / (1.0 - ADAM_B1 ** ADAM_STEP)
        v_hat = nv / (1.0 - ADAM_B2 ** ADAM_STEP)
        g_ref[...] = g
        d_ref[...] = -ADAM_LR * (m_hat / (jnp.sqrt(v_hat) + ADAM_EPS) + ADAM_WD * w_ref[...])
        nm_ref[...] = nm
        nv_ref[...] = nv

    row = pl.BlockSpec((tr, c), lambda i: (i, 0))
    out = jax.ShapeDtypeStruct((r, c), F32)
    return pl.pallas_call(
        body, name=name, grid=(r // tr,),
        in_specs=[row, row, row, pl.BlockSpec((N_DEV, tr, c), lambda i: (0, i, 0))],
        out_specs=[row] * 4, out_shape=[out] * 4, compiler_params=_cparams(("parallel",)),
    )(w, m, v, parts)


WEIGHTS = ['ssd_in_w', 'ssd_conv_w', 'ssd_conv_b', 'ssd_dt_bias', 'ssd_a_log', 'ssd_d', 'ssd_norm_w', 'ssd_out_w',
           'mla_in_w', 'mla_q_norm_w', 'mla_q_up_w', 'mla_kv_norm_w', 'mla_kv_up_w', 'mla_out_w', 'gdn_in_w',
           'gdn_conv_w', 'gdn_a_log', 'gdn_dt_bias', 'gdn_norm_w', 'gdn_out_w', 'ln_g', 'ln_b']
SHARDED = {'ssd_in_w': (1, True), 'ssd_conv_w': (1, False), 'ssd_conv_b': (0, False), 'ssd_norm_w': (0, False),
           'ssd_out_w': (0, True), 'mla_in_w': (1, True), 'mla_q_up_w': (1, True), 'mla_kv_up_w': (1, True),
           'mla_out_w': (0, True), 'gdn_in_w': (1, True), 'gdn_conv_w': (1, False), 'gdn_out_w': (0, True)}
REPLICATED = [n for n in WEIGHTS if n not in SHARDED]


def _pack_small(vals):
    flat = jnp.concatenate([vals[n].reshape(-1).astype(F32) for n in REPLICATED])
    rows = -(-flat.shape[0] // (8 * LANES)) * 8
    return jnp.pad(flat, (0, rows * LANES - flat.shape[0])).reshape(rows, LANES)


def _unpack_small(slab, like):
    flat = slab.reshape(-1)
    out, off = {}, 0
    for n in REPLICATED:
        sz = like[n].size
        out[n] = flat[off:off + sz].reshape(like[n].shape)
        off += sz
    return out


def kernel(x, positions, ssd_in_w, ssd_conv_w, ssd_conv_b, ssd_dt_bias, ssd_a_log, ssd_d, ssd_norm_w, ssd_out_w, mla_in_w, mla_q_norm_w, mla_q_up_w, mla_kv_norm_w, mla_kv_up_w, mla_out_w, gdn_in_w, gdn_conv_w, gdn_a_log, gdn_dt_bias, gdn_norm_w, gdn_out_w, ln_g, ln_b, loss_target, m_ssd_in_w, m_ssd_conv_w, m_ssd_conv_b, m_ssd_dt_bias, m_ssd_a_log, m_ssd_d, m_ssd_norm_w, m_ssd_out_w, m_mla_in_w, m_mla_q_norm_w, m_mla_q_up_w, m_mla_kv_norm_w, m_mla_kv_up_w, m_mla_out_w, m_gdn_in_w, m_gdn_conv_w, m_gdn_a_log, m_gdn_dt_bias, m_gdn_norm_w, m_gdn_out_w, m_ln_g, m_ln_b, v_ssd_in_w, v_ssd_conv_w, v_ssd_conv_b, v_ssd_dt_bias, v_ssd_a_log, v_ssd_d, v_ssd_norm_w, v_ssd_out_w, v_mla_in_w, v_mla_q_norm_w, v_mla_q_up_w, v_mla_kv_norm_w, v_mla_kv_up_w, v_mla_out_w, v_gdn_in_w, v_gdn_conv_w, v_gdn_a_log, v_gdn_dt_bias, v_gdn_norm_w, v_gdn_out_w, v_ln_g, v_ln_b):
    loc = locals()
    w = {n: loc[n] for n in WEIGHTS}
    m = {n: loc["m_" + n] for n in WEIGHTS}
    v = {n: loc["v_" + n] for n in WEIGHTS}
    xs, pos, tgt = x[0], positions[0], loss_target[0]

    def names_of(prefix):
        return [n for n in SHARDED if n.startswith(prefix + "_")]

    def shards(prefix, j):
        return [w[n][j].astype(WIRE) if SHARDED[n][1] else w[n][j] for n in names_of(prefix)]

    def assemble(prefix, j, gathered):
        lw = {n[len(prefix) + 1:]: _unshard(g, SHARDED[n][0]) for n, g in zip(names_of(prefix), gathered) if g is not None}
        lw.update({n[len(prefix) + 1:]: w[n][j] for n in REPLICATED if n.startswith(prefix + "_")})
        return lw

    def parts(prefix, g):
        return [_to_parts(g[n[len(prefix) + 1:]], SHARDED[n][0]).astype(WIRE if SHARDED[n][1] else F32)
                for n in names_of(prefix)]

    lg = lambda i: w["ln_g"][i].reshape(1, D_MODEL)
    lb = lambda i: w["ln_b"][i].reshape(1, D_MODEL)
    cosf, sinf = _rope_tables(pos)

    sh0 = shards("ssd", 0)
    w_s0 = assemble("ssd", 0, list(_exchange(sh0[:1], False, name="gather_ssd0")) + [None] * (len(sh0) - 1))

    def late0(got):
        rest = {n[4:]: _unshard(g, SHARDED[n][0]) for n, g in zip(names_of("ssd")[1:], got)}
        return rest, got[len(sh0) - 1:]

    h1, s0, got = _ssd_layer_fwd(xs, w_s0, lg(0), lb(0), "ssd0", ride=(sh0[1:] + shards("mla", 0), False), late=late0)
    w_m0 = assemble("mla", 0, got)
    h2, s1, got = _mla_layer_fwd(h1, w_m0, lg(1), lb(1), cosf, sinf, "mla0", ride=(shards("gdn", 0), False))
    w_g0 = assemble("gdn", 0, got)
    h3, s2, got = _gdn_layer_fwd(h2, w_g0, lg(2), lb(2), "gdn0", ride=(shards("ssd", 1), False))
    w_s1 = assemble("ssd", 1, got)
    h4, s3, _ = _ssd_layer_fwd(h3, w_s1, lg(3), lb(3), "ssd1")
    loss_tile, dl = _loss_head(h4, tgt, name="loss_head")

    def own0(k, g):
        n = "ssd_" + k
        return _to_parts(g, SHARDED[n][0]).astype(WIRE if SHARDED[n][1] else F32)

    dr3, du3, g3, _, _ = _ssd_layer_bwd(s3, w_s1, lg(3), jnp.zeros_like(dl), dl, "ssd1")
    dr2, du2, g2, r3 = _gdn_layer_bwd(s2, w_g0, lg(2), dr3, du3, "gdn0", ride=(parts("ssd", g3), True))
    dr1, du1, g1, r2 = _mla_layer_bwd(s1, w_m0, lg(1), cosf, sinf, dr2, du2, "mla0", ride=(parts("gdn", g2), True))
    dr0, du0, g0, r1, r0 = _ssd_layer_bwd(s0, w_s0, lg(0), dr1, du1, "ssd0", ride=(parts("mla", g1), True), own=own0)
    grad_x = _axpy(dr0, du0, name="grad_x")[None]

    gsmall = {"ssd_" + k: jnp.stack([g0[k], g3[k]]) for k in ("dt_bias", "a_log", "d")}
    gsmall.update({"mla_" + k: g1[k][None] for k in ("q_norm_w", "kv_norm_w")})
    gsmall.update({"gdn_" + k: g2[k][None] for k in ("a_log", "dt_bias", "norm_w")})
    gsmall["ln_g"] = jnp.stack([g0["ln_g"], g1["ln_g"], g2["ln_g"], g3["ln_g"]])
    gsmall["ln_b"] = jnp.stack([g0["ln_b"], g1["ln_b"], g2["ln_b"], g3["ln_b"]])
    small = _pack_small(gsmall)
    rsmall, = _exchange([jnp.broadcast_to(small[None], (N_DEV,) + small.shape)], True, name="gather_small_grads")

    recvd = {n: jnp.stack([r0[n[4:]], b], axis=1) for n, b in zip(names_of("ssd"), r3)}
    recvd.update({n: a[:, None] for n, a in zip(names_of("mla"), r1)})
    recvd.update({n: a[:, None] for n, a in zip(names_of("gdn"), r2)})
    recvd = [recvd[n] for n in SHARDED] + [rsmall]

    grads, deltas, new_m, new_v = {}, {}, {}, {}
    for n, pt in zip(SHARDED, recvd[:-1]):
        shp = w[n].shape
        r2d = (-1, shp[-1])
        outs = _adamw(w[n].reshape(r2d), m[n].reshape(r2d), v[n].reshape(r2d), pt.reshape((N_DEV,) + w[n].reshape(r2d).shape),
                      name="adamw_" + n)
        grads[n], deltas[n], new_m[n], new_v[n] = (o.reshape(shp) for o in outs)
    outs = _adamw(_pack_small(w), _pack_small(m), _pack_small(v), recvd[-1], name="adamw_replicated")
    for dst, o in zip((grads, deltas, new_m, new_v), outs):
        dst.update(_unpack_small(o, w))

    loss = lax.psum(loss_tile[0, 0], ("x", "y", "c"))
    return (loss, grad_x, *[grads[n] for n in WEIGHTS], *[deltas[n] for n in WEIGHTS],
            *[new_m[n] for n in WEIGHTS], *[new_v[n] for n in WEIGHTS])
```

```python
import functools

import jax
import jax.numpy as jnp
from jax import lax
from jax.experimental import pallas as pl
from jax.experimental.pallas import tpu as pltpu

F32 = jnp.float32
MXU = jnp.bfloat16
WIRE = jnp.bfloat16
HI = lax.Precision.HIGHEST

N_DEV = 8
LANES = 128
VMEM_LIMIT = 56 * 1024 * 1024

D_MODEL = 2048
DEPTH = 4
ALPHA = (2.0 * DEPTH) ** 0.25
LN_EPS = 1e-5
RMS_EPS = 1e-6

SSD_DI = 4096
SSD_P = 64
SSD_H = 64
SSD_G = 8
SSD_N = 128
SSD_L = 128
SSD_GS = SSD_DI // SSD_G
SSD_CONV_DIM = SSD_DI + 2 * SSD_G * SSD_N
SSD_PROJ_PAD = SSD_DI + SSD_CONV_DIM + LANES

MLA_H = 16
MLA_QR = 768
MLA_KVR = 512
MLA_NOPE = 128
MLA_ROPE = 64
MLA_V = 128
MLA_GATE = MLA_H * MLA_V
MLA_PROJ_PAD = MLA_QR + MLA_KVR + LANES + MLA_GATE
MLA_SCALE = (MLA_NOPE + MLA_ROPE) ** -0.5
ROPE_THETA = 10000.0
ATT_BLK = 512

GDN_HK = 16
GDN_HV = 32
GDN_DK = 128
GDN_DV = 128
GDN_KEY = GDN_HK * GDN_DK
GDN_VAL = GDN_HV * GDN_DV
GDN_L = 128
GDN_CONV_DIM = 2 * GDN_KEY + GDN_VAL
GDN_PROJ_PAD = 12800

ADAM_LR = 0.001
ADAM_B1 = 0.9
ADAM_B2 = 0.999
ADAM_EPS = 1e-08
ADAM_WD = 0.01
ADAM_STEP = 10


def _cparams(sem=None):
    return pltpu.CompilerParams(dimension_semantics=sem, vmem_limit_bytes=VMEM_LIMIT)


def _tile(n, cap):
    if n <= cap:
        return n
    best = None
    for d in range(LANES, cap + 1, LANES):
        if n % d == 0:
            best = d
    assert best is not None, (n, cap)
    return best


def _dg(a, b, ca, cb, prec=None):
    return lax.dot_general(a, b, (((ca,), (cb,)), ((), ())), preferred_element_type=F32, precision=prec)


def _mx(a):
    return a.astype(MXU)


@jax.custom_vjp
def _nn(a, b):
    return _dg(_mx(a), _mx(b), 1, 0)


def _nn_f(a, b):
    return _nn(a, b), (a, b)


def _nn_b(res, ct):
    a, b = res
    return _dg(_mx(ct), _mx(b), 1, 1), _dg(_mx(a), _mx(ct), 0, 0)


_nn.defvjp(_nn_f, _nn_b)


@jax.custom_vjp
def _nt(a, b):
    return _dg(_mx(a), _mx(b), 1, 1)


def _nt_f(a, b):
    return _nt(a, b), (a, b)


def _nt_b(res, ct):
    a, b = res
    return _dg(_mx(ct), _mx(b), 1, 0), _dg(_mx(ct), _mx(a), 0, 0)


_nt.defvjp(_nt_f, _nt_b)


@jax.custom_vjp
def _tn(a, b):
    return _dg(_mx(a), _mx(b), 0, 0)


def _tn_f(a, b):
    return _tn(a, b), (a, b)


def _tn_b(res, ct):
    a, b = res
    return _dg(_mx(b), _mx(ct), 1, 1), _dg(_mx(a), _mx(ct), 1, 0)


_tn.defvjp(_tn_f, _tn_b)


def _softplus(x):
    return jnp.maximum(x, 0.0) + jnp.log(1.0 + jnp.exp(-jnp.abs(x)))


def _silu(x):
    return x * jax.nn.sigmoid(x)


MM_TM = 1024
MM_TN = 1280
MM_VMEM_BUDGET = 40 * 1024 * 1024


def _mm(a, b, *, ta=False, tb=False, out_dtype=F32, name, ride=None):
    if ta:
        kdim, m = a.shape
    else:
        m, kdim = a.shape
    if tb:
        n, kb = b.shape
    else:
        kb, n = b.shape
    assert kdim == kb, (a.shape, b.shape, ta, tb)
    tm, tn = _tile(m, MM_TM), _tile(n, MM_TN)
    abytes, bbytes, obytes = a.dtype.itemsize, b.dtype.itemsize, jnp.dtype(out_dtype).itemsize
    tk = LANES
    for d in range(LANES, kdim + 1, LANES):
        if kdim % d == 0 and 2 * d * (tm * abytes + tn * bbytes) + tm * tn * (2 * obytes + 4) <= MM_VMEM_BUDGET:
            tk = d
    nk = kdim // tk
    ca, cb = (0 if ta else 1), (1 if tb else 0)

    def body(a_ref, b_ref, o_ref, *acc):
        part = _dg(_mx(a_ref[...]), _mx(b_ref[...]), ca, cb)
        if nk == 1:
            o_ref[...] = part.astype(out_dtype)
            return
        acc_ref, = acc
        k = pl.program_id(2)

        @pl.when(k == 0)
        def _():
            acc_ref[...] = part

        @pl.when(k > 0)
        def _():
            acc_ref[...] += part

        @pl.when(k == nk - 1)
        def _():
            o_ref[...] = acc_ref[...].astype(out_dtype)

    a_spec = pl.BlockSpec((tk, tm), lambda i, j, k: (k, i)) if ta else pl.BlockSpec((tm, tk), lambda i, j, k: (i, k))
    b_spec = pl.BlockSpec((tn, tk), lambda i, j, k: (j, k)) if tb else pl.BlockSpec((tk, tn), lambda i, j, k: (k, j))
    (out,), rode = _hosted_call(
        body, name=name, grid=(m // tm, n // tn, nk),
        in_specs=[a_spec, b_spec], out_specs=[pl.BlockSpec((tm, tn), lambda i, j, k: (i, j))],
        out_shape=[jax.ShapeDtypeStruct((m, n), out_dtype)],
        scratch_shapes=[pltpu.VMEM((tm, tn), F32)] if nk > 1 else [],
        sem=("parallel", "parallel", "arbitrary"), args=(a, b), ride=ride)
    return out if ride is None else (out, rode)


def _ln_fwd(h, y, g, b, name):
    t, d = h.shape
    tr = _tile(t, 256)

    def body(h_ref, y_ref, g_ref, b_ref, o_ref):
        r = ALPHA * h_ref[...] + y_ref[...]
        mu = jnp.mean(r, -1, keepdims=True)
        xc = r - mu
        var = jnp.mean(xc * xc, -1, keepdims=True)
        o_ref[...] = xc * lax.rsqrt(var + LN_EPS) * g_ref[...] + b_ref[...]

    row = pl.BlockSpec((tr, d), lambda i: (i, 0))
    par = pl.BlockSpec((1, d), lambda i: (0, 0))
    return pl.pallas_call(
        body, name=name, grid=(t // tr,), in_specs=[row, row, par, par], out_specs=row,
        out_shape=jax.ShapeDtypeStruct((t, d), F32), compiler_params=_cparams(("parallel",)),
    )(h, y, g, b)


def _ln_bwd(h, y, g, dr_up, du_up, name):
    t, d = h.shape
    tr = _tile(t, 256)

    def body(h_ref, y_ref, g_ref, dr_ref, du_ref, o_ref, dg_ref, db_ref):
        i = pl.program_id(0)

        @pl.when(i == 0)
        def _():
            dg_ref[...] = jnp.zeros_like(dg_ref)
            db_ref[...] = jnp.zeros_like(db_ref)

        dout = ALPHA * dr_ref[...] + du_ref[...]
        r = ALPHA * h_ref[...] + y_ref[...]
        mu = jnp.mean(r, -1, keepdims=True)
        xc = r - mu
        rstd = lax.rsqrt(jnp.mean(xc * xc, -1, keepdims=True) + LN_EPS)
        xh = xc * rstd
        dxh = dout * g_ref[...]
        o_ref[...] = rstd * (dxh - jnp.mean(dxh, -1, keepdims=True) - xh * jnp.mean(dxh * xh, -1, keepdims=True))
        dg_ref[...] += jnp.sum(dout * xh, 0, keepdims=True)
        db_ref[...] += jnp.sum(dout, 0, keepdims=True)

    row = pl.BlockSpec((tr, d), lambda i: (i, 0))
    par = pl.BlockSpec((1, d), lambda i: (0, 0))
    return pl.pallas_call(
        body, name=name, grid=(t // tr,), in_specs=[row, row, par, row, row], out_specs=[row, par, par],
        out_shape=[jax.ShapeDtypeStruct((t, d), F32), jax.ShapeDtypeStruct((1, d), F32), jax.ShapeDtypeStruct((1, d), F32)],
        compiler_params=_cparams(("arbitrary",)),
    )(h, y, g, dr_up, du_up)


def _loss_head(h, tgt, name):
    t, d = h.shape
    tr = _tile(t, 256)

    def body(h_ref, t_ref, l_ref, d_ref):
        i = pl.program_id(0)

        @pl.when(i == 0)
        def _():
            l_ref[...] = jnp.zeros_like(l_ref)

        e = h_ref[...] - t_ref[...]
        d_ref[...] = e * (1.0 / d)
        l_ref[...] += 0.5 * jnp.sum(jnp.mean(e * e, -1, keepdims=True))

    row = pl.BlockSpec((tr, d), lambda i: (i, 0))
    return pl.pallas_call(
        body, name=name, grid=(t // tr,), in_specs=[row, row],
        out_specs=[pl.BlockSpec((8, LANES), lambda i: (0, 0)), row],
        out_shape=[jax.ShapeDtypeStruct((8, LANES), F32), jax.ShapeDtypeStruct((t, d), F32)],
        compiler_params=_cparams(("arbitrary",)),
    )(h, tgt)


def _axpy(dr, du, name):
    t, d = dr.shape
    tr = _tile(t, 256)

    def body(a_ref, b_ref, o_ref):
        o_ref[...] = ALPHA * a_ref[...] + b_ref[...]

    row = pl.BlockSpec((tr, d), lambda i: (i, 0))
    return pl.pallas_call(
        body, name=name, grid=(t // tr,), in_specs=[row, row], out_specs=row,
        out_shape=jax.ShapeDtypeStruct((t, d), F32), compiler_params=_cparams(("parallel",)),
    )(dr, du)


CONV_TT = 512
CONV_TC = 512


def _shift_down(cur, prev, s, row):
    if s == 0:
        return cur
    return jnp.where(row >= s, pltpu.roll(cur, s, 0), pltpu.roll(prev, s, 0))


def _shift_up(cur, nxt, s, row, tt):
    if s == 0:
        return cur
    return jnp.where(row < tt - s, pltpu.roll(cur, tt - s, 0), pltpu.roll(nxt, tt - s, 0))


def _conv_fwd(proj, col0, w, b, name):
    t = proj.shape[0]
    c = w.shape[1]
    tt = _tile(t, CONV_TT)
    cb0 = col0 // CONV_TC

    def body(x_ref, p_ref, w_ref, b_ref, o_ref):
        i = pl.program_id(1)
        x = x_ref[...]
        p = jnp.where(i > 0, p_ref[...], 0.0)
        row = lax.broadcasted_iota(jnp.int32, x.shape, 0)
        pre = b_ref[...] + w_ref[3:4, :] * x
        for s in (1, 2, 3):
            pre = pre + w_ref[3 - s:4 - s, :] * _shift_down(x, p, s, row)
        o_ref[...] = _silu(pre)

    return pl.pallas_call(
        body, name=name, grid=(c // CONV_TC, t // tt),
        in_specs=[pl.BlockSpec((tt, CONV_TC), lambda j, i: (i, cb0 + j)),
                  pl.BlockSpec((tt, CONV_TC), lambda j, i: (jnp.maximum(i - 1, 0), cb0 + j)),
                  pl.BlockSpec((4, CONV_TC), lambda j, i: (0, j)),
                  pl.BlockSpec((1, CONV_TC), lambda j, i: (0, j))],
        out_specs=pl.BlockSpec((tt, CONV_TC), lambda j, i: (i, j)),
        out_shape=jax.ShapeDtypeStruct((t, c), F32), compiler_params=_cparams(("parallel", "parallel")),
    )(proj, proj, w, b)


def _conv_bwd_pre(proj, col0, w, b, dact, name):
    t = proj.shape[0]
    c = w.shape[1]
    tt = _tile(t, CONV_TT)
    cb0 = col0 // CONV_TC

    def body(x_ref, p_ref, w_ref, b_ref, d_ref, dpre_ref, dw_ref, db_ref):
        i = pl.program_id(1)

        @pl.when(i == 0)
        def _():
            dw_ref[...] = jnp.zeros_like(dw_ref)
            db_ref[...] = jnp.zeros_like(db_ref)

        x = x_ref[...]
        p = jnp.where(i > 0, p_ref[...], 0.0)
        row = lax.broadcasted_iota(jnp.int32, x.shape, 0)
        sh = [_shift_down(x, p, s, row) for s in range(4)]
        pre = b_ref[...] + w_ref[3:4, :] * sh[0]
        for s in (1, 2, 3):
            pre = pre + w_ref[3 - s:4 - s, :] * sh[s]
        sg = jax.nn.sigmoid(pre)
        dpre = d_ref[...] * (sg * (1.0 + pre * (1.0 - sg)))
        dpre_ref[...] = dpre
        for s in range(4):
            dw_ref[3 - s:4 - s, :] += jnp.sum(dpre * sh[s], 0, keepdims=True)
        db_ref[...] += jnp.sum(dpre, 0, keepdims=True)

    return pl.pallas_call(
        body, name=name, grid=(c // CONV_TC, t // tt),
        in_specs=[pl.BlockSpec((tt, CONV_TC), lambda j, i: (i, cb0 + j)),
                  pl.BlockSpec((tt, CONV_TC), lambda j, i: (jnp.maximum(i - 1, 0), cb0 + j)),
                  pl.BlockSpec((4, CONV_TC), lambda j, i: (0, j)),
                  pl.BlockSpec((1, CONV_TC), lambda j, i: (0, j)),
                  pl.BlockSpec((tt, CONV_TC), lambda j, i: (i, j))],
        out_specs=[pl.BlockSpec((tt, CONV_TC), lambda j, i: (i, j)),
                   pl.BlockSpec((4, CONV_TC), lambda j, i: (0, j)),
                   pl.BlockSpec((1, CONV_TC), lambda j, i: (0, j))],
        out_shape=[jax.ShapeDtypeStruct((t, c), F32), jax.ShapeDtypeStruct((4, c), F32), jax.ShapeDtypeStruct((1, c), F32)],
        compiler_params=_cparams(("parallel", "arbitrary")),
    )(proj, proj, w, b, dact)


def _conv_bwd_x(dpre, w, name):
    t, c = dpre.shape
    tt = _tile(t, CONV_TT)
    nt = t // tt

    def body(d_ref, n_ref, w_ref, o_ref):
        i = pl.program_id(1)
        d = d_ref[...]
        nx = jnp.where(i < nt - 1, n_ref[...], 0.0)
        row = lax.broadcasted_iota(jnp.int32, d.shape, 0)
        acc = w_ref[3:4, :] * d
        for s in (1, 2, 3):
            acc = acc + w_ref[3 - s:4 - s, :] * _shift_up(d, nx, s, row, tt)
        o_ref[...] = acc

    return pl.pallas_call(
        body, name=name, grid=(c // CONV_TC, nt),
        in_specs=[pl.BlockSpec((tt, CONV_TC), lambda j, i: (i, j)),
                  pl.BlockSpec((tt, CONV_TC), lambda j, i: (jnp.minimum(i + 1, nt - 1), j)),
                  pl.BlockSpec((4, CONV_TC), lambda j, i: (0, j))],
        out_specs=pl.BlockSpec((tt, CONV_TC), lambda j, i: (i, j)),
        out_shape=jax.ShapeDtypeStruct((t, c), F32), compiler_params=_cparams(("parallel", "parallel")),
    )(dpre, dpre, w)


SSD_GB = 4


def _split3(a):
    a1 = _mx(a)
    r = a - a1.astype(F32)
    a2 = _mx(r)
    return a1, a2, _mx(r - a2.astype(F32))


@jax.custom_vjp
def _sel_r(a, c):
    cm = _mx(c)
    p1, p2, p3 = _split3(a)
    return _dg(p1, cm, 1, 0) + (_dg(p2, cm, 1, 0) + _dg(p3, cm, 1, 0))


def _sel_r_f(a, c):
    return _sel_r(a, c), c


def _sel_r_b(c, ct):
    cm = _mx(c)
    p1, p2, p3 = _split3(ct)
    return _dg(p1, cm, 1, 1) + (_dg(p2, cm, 1, 1) + _dg(p3, cm, 1, 1)), jnp.zeros_like(c)


_sel_r.defvjp(_sel_r_f, _sel_r_b)


@jax.custom_vjp
def _sel_l(c, a):
    cm = _mx(c)
    p1, p2, p3 = _split3(a)
    return _dg(cm, p1, 1, 0) + (_dg(cm, p2, 1, 0) + _dg(cm, p3, 1, 0))


def _sel_l_f(c, a):
    return _sel_l(c, a), c


def _sel_l_b(c, ct):
    cm = _mx(c)
    p1, p2, p3 = _split3(ct)
    return jnp.zeros_like(c), _dg(cm, p1, 0, 0) + (_dg(cm, p2, 0, 0) + _dg(cm, p3, 0, 0))


_sel_l.defvjp(_sel_l_f, _sel_l_b)


def _ssd_chunk(gb, x, z, bm, cm, dtraw, dtb, alog, dsk, nw, prev):
    L = x.shape[1]
    r_i = lax.broadcasted_iota(jnp.int32, (L, L), 0)
    c_i = lax.broadcasted_iota(jnp.int32, (L, L), 1)
    causal = r_i >= c_i
    dt = _softplus(dtraw + dtb)
    a = dt * (-jnp.exp(alog))
    acs = _sel_l(causal.astype(F32), a)
    e_r = lax.broadcasted_iota(jnp.int32, (LANES, SSD_GS), 0)
    e_c = lax.broadcasted_iota(jnp.int32, (LANES, SSD_GS), 1)
    hpg = SSD_H // SSD_G
    sels = [(e_r == (gb * SSD_GB + i) * hpg + jnp.right_shift(e_c, 6)).astype(F32) for i in range(SSD_GB)]
    dt_x = _cat0([_sel_r(dt, s) for s in sels])
    acs_x = _cat0([_sel_r(acs, s) for s in sels])
    d_x = _cat0([_sel_r(jnp.broadcast_to(dsk, (8, LANES)), s)[0:1] for s in sels])
    last = lax.broadcasted_iota(jnp.int32, (L, 1), 0) == L - 1
    alast = jnp.sum(jnp.where(last, acs_x, 0.0), axis=1, keepdims=True)
    xdt = x * dt_x
    cb = _bnt(cm, bm)
    lane = lax.broadcasted_iota(jnp.int32, (L, LANES), 1)
    ys = []
    for j in range(SSD_GS // LANES):
        xp = xdt[:, :, j * LANES:(j + 1) * LANES]
        yp = None
        for hh in range(2):
            c0 = (2 * j + hh) * SSD_P
            cmx = jnp.broadcast_to(acs_x[:, :, c0:c0 + 1], (SSD_GB, L, L))
            dec = jnp.exp(jnp.where(causal, cmx - jnp.swapaxes(cmx, 1, 2), -jnp.inf))
            half = (lane < SSD_P) if hh == 0 else (lane >= SSD_P)
            t = _bnn(cb * dec, jnp.where(half, xp, 0.0))
            yp = t if yp is None else yp + t
        ys.append(yp)
    y_diag = jnp.concatenate(ys, axis=2)
    st = _btn(bm, xdt * jnp.exp(alast - acs_x))
    new = prev * jnp.exp(alast) + st
    y_off = _bnn(cm, prev) * jnp.exp(acs_x)
    y = y_diag + y_off + x * d_x
    yg = y * _silu(z)
    yn = yg * lax.rsqrt(jnp.mean(yg * yg, -1, keepdims=True) + RMS_EPS) * nw
    return yn, new


def _groups_of(ref, width):
    return _cat0([ref[:, i * width:(i + 1) * width] for i in range(SSD_GB)])


def _put_groups(ref, val, width):
    for i in range(SSD_GB):
        ref[:, i * width:(i + 1) * width] = val[i].astype(ref.dtype)


def _ssd_specs(nc, rev):
    cc = (lambda c: nc - 1 - c) if rev else (lambda c: c)
    wx, wb = SSD_GB * SSD_GS, SSD_GB * SSD_N
    dtb = (SSD_DI + SSD_CONV_DIM) // LANES
    bb = SSD_DI // wb
    cbk = (SSD_DI + SSD_G * SSD_N) // wb
    par = pl.BlockSpec((1, LANES), lambda c, g: (0, 0))
    return dict(
        z=pl.BlockSpec((SSD_L, wx), lambda c, g: (cc(c), g)),
        dt=pl.BlockSpec((SSD_L, LANES), lambda c, g: (cc(c), dtb)),
        x=pl.BlockSpec((SSD_L, wx), lambda c, g: (cc(c), g)),
        bm=pl.BlockSpec((SSD_L, wb), lambda c, g: (cc(c), bb + g)),
        cm=pl.BlockSpec((SSD_L, wb), lambda c, g: (cc(c), cbk + g)),
        par=par,
        nw=pl.BlockSpec((1, wx), lambda c, g: (0, g)),
        st=pl.BlockSpec((1, SSD_GB, SSD_N, SSD_GS), lambda c, g: (cc(c), g, 0, 0)),
        y=pl.BlockSpec((SSD_L, wx), lambda c, g: (cc(c), g)),
        bc=pl.BlockSpec((SSD_L, wb), lambda c, g: (cc(c), g)),
        dtout=pl.BlockSpec((SSD_L, LANES), lambda c, g: (cc(c), 0)),
    )


def _ssd_fwd(proj, act, dtb, alog, dsk, nw, name, ride=None):
    t = proj.shape[0]
    nc = t // SSD_L
    sp = _ssd_specs(nc, False)

    def body(z_ref, dt_ref, x_ref, bm_ref, cm_ref, dtb_ref, alog_ref, dsk_ref, nw_ref, y_ref, st_ref, state):
        c, g = pl.program_id(0), pl.program_id(1)
        mine = pl.ds(g * SSD_GB, SSD_GB)

        @pl.when(c == 0)
        def _():
            state[mine] = jnp.zeros((SSD_GB, SSD_N, SSD_GS), F32)

        prev = state[mine]
        st_ref[0] = prev
        yn, new = _ssd_chunk(g, _groups_of(x_ref, SSD_GS), _groups_of(z_ref, SSD_GS), _groups_of(bm_ref, SSD_N),
                             _groups_of(cm_ref, SSD_N), dt_ref[...], dtb_ref[...], alog_ref[...], dsk_ref[...],
                             _groups_of(nw_ref, SSD_GS), prev)
        _put_groups(y_ref, yn, SSD_GS)
        state[mine] = new

    return _hosted_call(
        body, name=name, grid=(nc, SSD_G // SSD_GB),
        in_specs=[sp["z"], sp["dt"], sp["x"], sp["bm"], sp["cm"], sp["par"], sp["par"], sp["par"], sp["nw"]],
        out_specs=[sp["y"], sp["st"]],
        out_shape=[jax.ShapeDtypeStruct((t, SSD_DI), F32), jax.ShapeDtypeStruct((nc, SSD_G, SSD_N, SSD_GS), F32)],
        scratch_shapes=[pltpu.VMEM((SSD_G, SSD_N, SSD_GS), F32)],
        sem=("arbitrary", "arbitrary"), args=(proj, proj, act, act, act, dtb, alog, dsk, nw), ride=ride)


def _ssd_bwd(proj, act, dtb, alog, dsk, nw, states, dyn, name, ride=None):
    t = proj.shape[0]
    nc = t // SSD_L
    sp = _ssd_specs(nc, True)

    def body(z_ref, dt_ref, x_ref, bm_ref, cm_ref, dtb_ref, alog_ref, dsk_ref, nw_ref, st_ref, dy_ref,
             dx_ref, dz_ref, dbm_ref, dcm_ref, ddt_ref, ddtb_ref, dalog_ref, ddsk_ref, dnw_ref, dstate):
        c, g = pl.program_id(0), pl.program_id(1)
        mine = pl.ds(g * SSD_GB, SSD_GB)

        @pl.when(c == 0)
        def _():
            dstate[mine] = jnp.zeros((SSD_GB, SSD_N, SSD_GS), F32)

        @pl.when((c == 0) & (g == 0))
        def _():
            ddtb_ref[...] = jnp.zeros_like(ddtb_ref)
            dalog_ref[...] = jnp.zeros_like(dalog_ref)
            ddsk_ref[...] = jnp.zeros_like(ddsk_ref)
            dnw_ref[...] = jnp.zeros_like(dnw_ref)

        @pl.when(g == 0)
        def _():
            ddt_ref[...] = jnp.zeros_like(ddt_ref)

        _, vjp = jax.vjp(functools.partial(_ssd_chunk, g), _groups_of(x_ref, SSD_GS), _groups_of(z_ref, SSD_GS),
                         _groups_of(bm_ref, SSD_N), _groups_of(cm_ref, SSD_N), dt_ref[...], dtb_ref[...], alog_ref[...],
                         dsk_ref[...], _groups_of(nw_ref, SSD_GS), st_ref[0])
        dx, dz, dbm, dcm, ddt, ddtb, dalog, ddsk, dnw, dprev = vjp((_groups_of(dy_ref, SSD_GS), dstate[mine]))
        _put_groups(dx_ref, dx, SSD_GS)
        _put_groups(dz_ref, dz, SSD_GS)
        _put_groups(dbm_ref, dbm, SSD_N)
        _put_groups(dcm_ref, dcm, SSD_N)
        ddt_ref[...] += ddt
        ddtb_ref[...] += ddtb
        dalog_ref[...] += dalog
        ddsk_ref[...] += ddsk
        dnw_ref[mine] += dnw
        dstate[mine] = dprev

    par_out = pl.BlockSpec((1, LANES), lambda c, g: (0, 0))
    sds = jax.ShapeDtypeStruct
    return _hosted_call(
        body, name=name, grid=(nc, SSD_G // SSD_GB),
        in_specs=[sp["z"], sp["dt"], sp["x"], sp["bm"], sp["cm"], sp["par"], sp["par"], sp["par"], sp["nw"],
                  sp["st"], sp["y"]],
        out_specs=[sp["y"], sp["y"], sp["bc"], sp["bc"], sp["dtout"], par_out, par_out, par_out,
                   pl.BlockSpec((SSD_G, 1, SSD_GS), lambda c, g: (0, 0, 0))],
        out_shape=[sds((t, SSD_DI), F32), sds((t, SSD_DI), F32), sds((t, SSD_G * SSD_N), F32), sds((t, SSD_G * SSD_N), F32),
                   sds((t, LANES), F32), sds((1, LANES), F32), sds((1, LANES), F32), sds((1, LANES), F32),
                   sds((SSD_G, 1, SSD_GS), F32)],
        scratch_shapes=[pltpu.VMEM((SSD_G, SSD_N, SSD_GS), F32)],
        sem=("arbitrary", "arbitrary"), args=(proj, proj, act, act, act, dtb, alog, dsk, nw, states, dyn), ride=ride)


def _pad_lanes(v, width=LANES, offset=0):
    return jnp.pad(v.astype(F32), (offset, width - offset - v.shape[0])).reshape(1, width)


def _ssd_layer_fwd(u, w, ln_g, ln_b, tag, ride=None, late=None):
    w_in = jnp.concatenate([w["in_w"], jnp.zeros((D_MODEL, LANES - SSD_H), w["in_w"].dtype)], axis=1)
    dtb, alog, dsk = _pad_lanes(w["dt_bias"]), _pad_lanes(w["a_log"]), _pad_lanes(w["d"])
    if late is None:
        proj = _mm(u, w_in, name=tag + "_in")
    else:
        proj, got = _mm(u, w_in, name=tag + "_in", ride=(late[0], False))
        w.update(late[1](got))
    nw = w["norm_w"].reshape(1, SSD_DI)
    cb = w["conv_b"].reshape(1, SSD_CONV_DIM)
    act = _conv_fwd(proj, SSD_DI, w["conv_w"], cb, name=tag + "_conv")
    (yn, states), rode = _ssd_fwd(proj, act, dtb, alog, dsk, nw, name=tag + "_scan", ride=ride)
    y = _mm(yn, w["out_w"], name=tag + "_out")
    h = _ln_fwd(u, y, ln_g, ln_b, name=tag + "_ln")
    saved = dict(u=u, w_in=w_in, proj=proj, act=act, states=states, yn=yn, y=y, dtb=dtb, alog=alog, dsk=dsk, nw=nw, cb=cb)
    return h, saved, rode


def _ssd_layer_bwd(s, w, ln_g, dr_up, du_up, tag, ride=None, own=None):
    dr, dg, db = _ln_bwd(s["u"], s["y"], ln_g, dr_up, du_up, name=tag + "_ln_b")
    dyn = _mm(dr, w["out_w"], tb=True, name=tag + "_out_bx")
    d_out_w = _mm(s["yn"], dr, ta=True, out_dtype=WIRE, name=tag + "_out_bw")
    (dx, dz, dbm, dcm, ddt, ddtb, dalog, ddsk, dnw), rode = _ssd_bwd(
        s["proj"], s["act"], s["dtb"], s["alog"], s["dsk"], s["nw"], s["states"], dyn, name=tag + "_scan_b", ride=ride)
    dact = jnp.concatenate([dx, dbm, dcm], axis=1)
    dpre, d_conv_w, d_conv_b = _conv_bwd_pre(s["proj"], SSD_DI, w["conv_w"], s["cb"], dact, name=tag + "_conv_bp")
    dxbc = _conv_bwd_x(dpre, w["conv_w"], name=tag + "_conv_bx")
    dproj = jnp.concatenate([dz, dxbc, ddt], axis=1)
    grads = dict(conv_w=d_conv_w, conv_b=d_conv_b.reshape(-1), dt_bias=ddtb[0, :SSD_H], a_log=dalog[0, :SSD_H],
                 d=ddsk[0, :SSD_H], norm_w=dnw.reshape(-1), out_w=d_out_w, ln_g=dg[0], ln_b=db[0])
    late_names = ("conv_w", "conv_b", "norm_w", "out_w")
    got = {}
    if own is None:
        d_in_w = _mm(s["u"], dproj, ta=True, out_dtype=WIRE, name=tag + "_in_bw")
        du = _mm(dproj, s["w_in"], tb=True, name=tag + "_in_bx")
        grads["in_w"] = d_in_w[:, :SSD_DI + SSD_CONV_DIM + SSD_H]
    else:
        d_in_w, late = _mm(s["u"], dproj, ta=True, out_dtype=WIRE, name=tag + "_in_bw",
                           ride=([own(k, grads[k]) for k in late_names], True))
        got = dict(zip(late_names, late))
        grads["in_w"] = d_in_w[:, :SSD_DI + SSD_CONV_DIM + SSD_H]
        du, (got["in_w"],) = _mm(dproj, s["w_in"], tb=True, name=tag + "_in_bx", ride=([own("in_w", grads["in_w"])], True))
    return dr, du, grads, rode, got


MLA_LOW = MLA_QR + MLA_KVR + LANES
MLA_ZB = MLA_LOW // LANES


def _rope_mat():
    r = lax.broadcasted_iota(jnp.int32, (LANES, LANES), 0)
    c = lax.broadcasted_iota(jnp.int32, (LANES, LANES), 1)
    hf = MLA_ROPE // 2
    return jnp.where((c < hf) & (r == c + hf), -1.0, 0.0) + jnp.where((c >= hf) & (c < 2 * hf) & (r == c - hf), 1.0, 0.0)


def _rope(x, cosf, sinf):
    return x * cosf + _sel_r(x, _rope_mat()) * sinf


def _rope_adj(d, cosf, sinf):
    return d * cosf - _sel_r(d * sinf, _rope_mat())


def _mla_low_fn(low, qnw, kvnw, cosf, sinf):
    qc, kvc, kr = low[:, :MLA_QR], low[:, MLA_QR:MLA_QR + MLA_KVR], low[:, MLA_QR + MLA_KVR:]
    qn = qc * lax.rsqrt(jnp.mean(qc * qc, -1, keepdims=True) + RMS_EPS) * qnw
    kvn = kvc * lax.rsqrt(jnp.mean(kvc * kvc, -1, keepdims=True) + RMS_EPS) * kvnw
    return qn, kvn, _rope(kr, cosf, sinf)


def _mla_low_fwd(proj, qnw, kvnw, cosf, sinf, name):
    t = proj.shape[0]
    tr = _tile(t, 256)

    def body(low_ref, qnw_ref, kvnw_ref, cos_ref, sin_ref, qn_ref, kvn_ref, kr_ref):
        qn, kvn, kr = _mla_low_fn(low_ref[...], qnw_ref[...], kvnw_ref[...], cos_ref[...], sin_ref[...])
        qn_ref[...] = qn
        kvn_ref[...] = kvn
        kr_ref[...] = kr

    row = lambda wdt: pl.BlockSpec((tr, wdt), lambda i: (i, 0))
    par = lambda wdt: pl.BlockSpec((1, wdt), lambda i: (0, 0))
    sds = jax.ShapeDtypeStruct
    return pl.pallas_call(
        body, name=name, grid=(t // tr,),
        in_specs=[row(MLA_LOW), par(MLA_QR), par(MLA_KVR), row(LANES), row(LANES)],
        out_specs=[row(MLA_QR), row(MLA_KVR), row(LANES)],
        out_shape=[sds((t, MLA_QR), F32), sds((t, MLA_KVR), F32), sds((t, LANES), F32)],
        compiler_params=_cparams(("parallel",)),
    )(proj, qnw, kvnw, cosf, sinf)


def _mla_low_bwd(proj, qnw, kvnw, cosf, sinf, dqn, dkvn, dkr, name):
    t = proj.shape[0]
    tr = _tile(t, 256)

    def body(low_ref, qnw_ref, kvnw_ref, cos_ref, sin_ref, dqn_ref, dkvn_ref, dkr_ref, dlow_ref, dqnw_ref, dkvnw_ref):
        i = pl.program_id(0)

        @pl.when(i == 0)
        def _():
            dqnw_ref[...] = jnp.zeros_like(dqnw_ref)
            dkvnw_ref[...] = jnp.zeros_like(dkvnw_ref)

        cosf, sinf = cos_ref[...], sin_ref[...]
        _, vjp = jax.vjp(lambda a, b, c: _mla_low_fn(a, b, c, cosf, sinf), low_ref[...], qnw_ref[...], kvnw_ref[...])
        dlow, dq, dk = vjp((dqn_ref[...], dkvn_ref[...], dkr_ref[...]))
        dlow_ref[...] = dlow
        dqnw_ref[...] += dq
        dkvnw_ref[...] += dk

    row = lambda wdt: pl.BlockSpec((tr, wdt), lambda i: (i, 0))
    par = lambda wdt: pl.BlockSpec((1, wdt), lambda i: (0, 0))
    sds = jax.ShapeDtypeStruct
    return pl.pallas_call(
        body, name=name, grid=(t // tr,),
        in_specs=[row(MLA_LOW), par(MLA_QR), par(MLA_KVR), row(LANES), row(LANES), row(MLA_QR), row(MLA_KVR), row(LANES)],
        out_specs=[row(MLA_LOW), par(MLA_QR), par(MLA_KVR)],
        out_shape=[sds((t, MLA_LOW), F32), sds((1, MLA_QR), F32), sds((1, MLA_KVR), F32)],
        compiler_params=_cparams(("arbitrary",)),
    )(proj, qnw, kvnw, cosf, sinf, dqn, dkvn, dkr)


def _rope_heads(x, col_blk0, cosf, sinf, adjoint, name):
    t = x.shape[0]
    tr = _tile(t, 512)

    def body(x_ref, cos_ref, sin_ref, o_ref):
        f = _rope_adj if adjoint else _rope
        o_ref[...] = f(x_ref[...], cos_ref[...], sin_ref[...])

    tab = pl.BlockSpec((tr, LANES), lambda i, h: (i, 0))
    return pl.pallas_call(
        body, name=name, grid=(t // tr, MLA_H),
        in_specs=[pl.BlockSpec((tr, LANES), lambda i, h: (i, col_blk0 + h)), tab, tab],
        out_specs=pl.BlockSpec((tr, LANES), lambda i, h: (i, h)),
        out_shape=jax.ShapeDtypeStruct((t, MLA_H * LANES), F32), compiler_params=_cparams(("parallel", "parallel")),
    )(x, cosf, sinf)


ATT_HB = 2
ATT_W = ATT_HB * LANES


def _att_scores(qn_ref, qr_ref, kn_ref, kr_ref, i, j, tq, tk):
    qn, qr, kn = _heads_of(qn_ref, ATT_HB), _heads_of(qr_ref, ATT_HB), _heads_of(kn_ref, ATT_HB)
    s_rope = _dg(_mx(qr.reshape(ATT_HB * tq, LANES)), _mx(kr_ref[...]), 1, 1).reshape(ATT_HB, tq, tk)
    s = (_bdg(_mx(qn), _mx(kn), 2, 2) + s_rope) * MLA_SCALE
    qpos = i * tq + lax.broadcasted_iota(jnp.int32, (tq, tk), 0)
    kpos = j * tk + lax.broadcasted_iota(jnp.int32, (tq, tk), 1)
    return jnp.where(kpos <= qpos, s, -jnp.inf), qn, qr, kn


def _put_heads(ref, val):
    for b in range(val.shape[0]):
        ref[:, b * LANES:(b + 1) * LANES] = val[b].astype(ref.dtype)


def _att_ds(s, v_ref, o_ref, do_ref, lse_ref):
    do = _heads_of(do_ref, ATT_HB)
    p = jnp.exp(s - _heads_of(lse_ref, ATT_HB)[:, :, 0:1])
    dp = _bdg(_mx(do), _mx(_heads_of(v_ref, ATT_HB)), 2, 2)
    dl = jnp.sum(do * _heads_of(o_ref, ATT_HB), -1, keepdims=True)
    return _mx(p), _mx(p * (dp - dl) * MLA_SCALE), do


def _attn_fwd(q, qr, kv, kr, proj, name, ride=None):
    t = q.shape[0]
    tq = tk = _tile(t, ATT_BLK)
    nq = nk = t // tq

    def body(qn_ref, qr_ref, kn_ref, kr_ref, v_ref, *rest):
        z_refs, (o_ref, og_ref, lse_ref, m_s, l_s, acc_s) = rest[:ATT_HB], rest[ATT_HB:]
        i, j = pl.program_id(1), pl.program_id(2)

        @pl.when(j == 0)
        def _():
            m_s[...] = jnp.full_like(m_s, -jnp.inf)
            l_s[...] = jnp.zeros_like(l_s)
            acc_s[...] = jnp.zeros_like(acc_s)

        @pl.when(j <= i)
        def _():
            s, _, _, _ = _att_scores(qn_ref, qr_ref, kn_ref, kr_ref, i, j, tq, tk)
            m_new = jnp.maximum(m_s[...], jnp.max(s, -1, keepdims=True))
            p = jnp.exp(s - m_new)
            corr = jnp.exp(m_s[...] - m_new)
            l_s[...] = corr * l_s[...] + jnp.sum(p, -1, keepdims=True)
            acc_s[...] = corr * acc_s[...] + _bdg(_mx(p), _mx(_heads_of(v_ref, ATT_HB)), 2, 1)
            m_s[...] = m_new

        @pl.when(j == nk - 1)
        def _():
            o = acc_s[...] / l_s[...]
            _put_heads(o_ref, o)
            lse = m_s[...] + jnp.log(l_s[...])
            for b in range(ATT_HB):
                og_ref[:, b * LANES:(b + 1) * LANES] = (o[b] * _silu(z_refs[b][...])).astype(og_ref.dtype)
                lse_ref[:, b * LANES:(b + 1) * LANES] = jnp.broadcast_to(lse[b], (tq, LANES))

    qs = lambda off: pl.BlockSpec((tq, ATT_W), lambda h, i, j: (i, off // ATT_HB + h))
    ks = lambda off: pl.BlockSpec((tk, ATT_W), lambda h, i, j: (jnp.minimum(j, i), off // ATT_HB + h))
    zs = [pl.BlockSpec((tq, LANES), functools.partial(lambda b, h, i, j: (i, MLA_ZB + h * ATT_HB + b), b)) for b in range(ATT_HB)]
    sds = jax.ShapeDtypeStruct
    return _hosted_call(
        body, name=name, grid=(MLA_H // ATT_HB, nq, nk),
        in_specs=[qs(0), qs(0), ks(0), pl.BlockSpec((tk, LANES), lambda h, i, j: (jnp.minimum(j, i), 0)), ks(MLA_H)] + zs,
        out_specs=[qs(0), qs(0), qs(0)],
        out_shape=[sds((t, MLA_GATE), F32), sds((t, MLA_GATE), F32), sds((t, MLA_H * LANES), F32)],
        scratch_shapes=[pltpu.VMEM((ATT_HB, tq, 1), F32), pltpu.VMEM((ATT_HB, tq, 1), F32), pltpu.VMEM((ATT_HB, tq, LANES), F32)],
        sem=("parallel", "parallel", "arbitrary"), args=(q, qr, kv, kr, kv) + (proj,) * ATT_HB, ride=ride)


def _gate_bwd(dog, o, proj, name):
    t = o.shape[0]
    tr = _tile(t, 512)

    def body(d_ref, o_ref, z_ref, do_ref, dz_ref):
        z = z_ref[...]
        sg = jax.nn.sigmoid(z)
        d = d_ref[...]
        do_ref[...] = d * z * sg
        dz_ref[...] = d * o_ref[...] * (sg * (1.0 + z * (1.0 - sg)))

    blk = lambda off: pl.BlockSpec((tr, 512), lambda i, j: (i, off + j))
    assert MLA_LOW % 512 != 0 or True
    zspec = pl.BlockSpec((tr, LANES), lambda i, j: (i, MLA_ZB + j))
    b128 = pl.BlockSpec((tr, LANES), lambda i, j: (i, j))
    sds = jax.ShapeDtypeStruct
    return pl.pallas_call(
        body, name=name, grid=(t // tr, MLA_GATE // LANES),
        in_specs=[b128, b128, zspec], out_specs=[b128, b128],
        out_shape=[sds((t, MLA_GATE), F32), sds((t, MLA_GATE), F32)],
        compiler_params=_cparams(("parallel", "parallel")),
    )(dog, o, proj)


def _attn_bwd_q(q, qr, kv, kr, o, do, lse, name, ride=None):
    t = q.shape[0]
    tq = tk = _tile(t, ATT_BLK)
    nq = nk = t // tq

    def body(qn_ref, qr_ref, kn_ref, kr_ref, v_ref, o_ref, do_ref, lse_ref, dqn_ref, dqr_ref, an_s, ar_s):
        i, j = pl.program_id(1), pl.program_id(2)

        @pl.when(j == 0)
        def _():
            an_s[...] = jnp.zeros_like(an_s)
            ar_s[...] = jnp.zeros_like(ar_s)

        @pl.when(j <= i)
        def _():
            s, _, _, kn = _att_scores(qn_ref, qr_ref, kn_ref, kr_ref, i, j, tq, tk)
            ds = _att_ds(s, v_ref, o_ref, do_ref, lse_ref)[1]
            an_s[...] += _bdg(ds, _mx(kn), 2, 1)
            ar_s[...] += _dg(ds.reshape(ATT_HB * tq, tk), _mx(kr_ref[...]), 1, 0).reshape(ATT_HB, tq, LANES)

        @pl.when(j == nk - 1)
        def _():
            _put_heads(dqn_ref, an_s[...])
            _put_heads(dqr_ref, ar_s[...])

    qs = lambda off: pl.BlockSpec((tq, ATT_W), lambda h, i, j: (i, off // ATT_HB + h))
    ks = lambda off: pl.BlockSpec((tk, ATT_W), lambda h, i, j: (jnp.minimum(j, i), off // ATT_HB + h))
    sds = jax.ShapeDtypeStruct
    return _hosted_call(
        body, name=name, grid=(MLA_H // ATT_HB, nq, nk),
        in_specs=[qs(0), qs(0), ks(0), pl.BlockSpec((tk, LANES), lambda h, i, j: (jnp.minimum(j, i), 0)), ks(MLA_H),
                  qs(0), qs(0), qs(0)],
        out_specs=[qs(0), qs(0)],
        out_shape=[sds((t, MLA_H * LANES), F32), sds((t, MLA_H * LANES), F32)],
        scratch_shapes=[pltpu.VMEM((ATT_HB, tq, LANES), F32), pltpu.VMEM((ATT_HB, tq, LANES), F32)],
        sem=("parallel", "parallel", "arbitrary"), args=(q, qr, kv, kr, kv, o, do, lse), ride=ride)


def _attn_bwd_kv(q, qr, kv, kr, o, do, lse, name):
    t = q.shape[0]
    tq = tk = _tile(t, ATT_BLK)
    nq = nk = t // tq

    def body(qn_ref, qr_ref, kn_ref, kr_ref, v_ref, o_ref, do_ref, lse_ref, dkn_ref, dv_ref, dkr_ref, akn_s, av_s):
        j, h, i = pl.program_id(0), pl.program_id(1), pl.program_id(2)

        @pl.when((h == 0) & (i == 0))
        def _():
            dkr_ref[...] = jnp.zeros_like(dkr_ref)

        @pl.when(i == 0)
        def _():
            akn_s[...] = jnp.zeros_like(akn_s)
            av_s[...] = jnp.zeros_like(av_s)

        @pl.when(i >= j)
        def _():
            s, qn, qr, _ = _att_scores(qn_ref, qr_ref, kn_ref, kr_ref, i, j, tq, tk)
            p, ds, do = _att_ds(s, v_ref, o_ref, do_ref, lse_ref)
            av_s[...] += _bdg(p, _mx(do), 1, 1)
            akn_s[...] += _bdg(ds, _mx(qn), 1, 1)
            dkr_ref[...] += _dg(ds.reshape(ATT_HB * tq, tk), _mx(qr.reshape(ATT_HB * tq, LANES)), 0, 0)

        @pl.when(i == nq - 1)
        def _():
            _put_heads(dkn_ref, akn_s[...])
            _put_heads(dv_ref, av_s[...])

    qs = lambda off: pl.BlockSpec((tq, ATT_W), lambda j, h, i: (jnp.maximum(i, j), off // ATT_HB + h))
    ks = lambda off: pl.BlockSpec((tk, ATT_W), lambda j, h, i: (j, off // ATT_HB + h))
    sds = jax.ShapeDtypeStruct
    return pl.pallas_call(
        body, name=name, grid=(nk, MLA_H // ATT_HB, nq),
        in_specs=[qs(0), qs(0), ks(0), pl.BlockSpec((tk, LANES), lambda j, h, i: (j, 0)), ks(MLA_H), qs(0), qs(0), qs(0)],
        out_specs=[ks(0), ks(0), pl.BlockSpec((tk, LANES), lambda j, h, i: (j, 0))],
        out_shape=[sds((t, MLA_H * LANES), F32), sds((t, MLA_H * LANES), F32), sds((t, LANES), F32)],
        scratch_shapes=[pltpu.VMEM((ATT_HB, tk, LANES), F32), pltpu.VMEM((ATT_HB, tk, LANES), F32)],
        compiler_params=_cparams(("parallel", "arbitrary", "arbitrary")),
    )(q, qr, kv, kr, kv, o, do, lse)


def _rope_tables(positions):
    inv_freq = ROPE_THETA ** (-jnp.arange(0, MLA_ROPE, 2, dtype=F32) / MLA_ROPE)
    ang = positions.astype(F32)[:, None] * inv_freq
    pad = jnp.zeros((positions.shape[0], LANES - MLA_ROPE), F32)
    cos, sin = jnp.cos(ang), jnp.sin(ang)
    return jnp.concatenate([cos, cos, pad], 1), jnp.concatenate([sin, sin, pad], 1)


def _mla_weights(w):
    dt = w["in_w"].dtype
    iw = w["in_w"]
    c1 = MLA_QR + MLA_KVR + MLA_ROPE
    w_in = jnp.concatenate([iw[:, :c1], jnp.zeros((D_MODEL, LANES - MLA_ROPE), dt), iw[:, c1:]], axis=1)
    qu = w["q_up_w"].reshape(MLA_QR, MLA_H, MLA_NOPE + MLA_ROPE)
    qrope = jnp.concatenate([qu[:, :, MLA_NOPE:], jnp.zeros((MLA_QR, MLA_H, LANES - MLA_ROPE), dt)], axis=2)
    w_q = jnp.concatenate([qu[:, :, :MLA_NOPE].reshape(MLA_QR, -1), qrope.reshape(MLA_QR, -1)], axis=1)
    kvu = w["kv_up_w"].reshape(MLA_KVR, MLA_H, MLA_NOPE + MLA_V)
    w_kv = jnp.concatenate([kvu[:, :, :MLA_NOPE].reshape(MLA_KVR, -1), kvu[:, :, MLA_NOPE:].reshape(MLA_KVR, -1)], axis=1)
    return w_in, w_q, w_kv


def _mla_layer_fwd(u, w, ln_g, ln_b, cosf, sinf, tag, ride=None):
    w_in, w_q, w_kv = _mla_weights(w)
    qnw, kvnw = w["q_norm_w"].reshape(1, -1), w["kv_norm_w"].reshape(1, -1)
    proj = _mm(u, w_in, name=tag + "_in")
    qn, kvn, kr = _mla_low_fwd(proj, qnw, kvnw, cosf, sinf, name=tag + "_low")
    q = _mm(qn, w_q, name=tag + "_qup")
    kv = _mm(kvn, w_kv, name=tag + "_kvup")
    qr = _rope_heads(q, MLA_H, cosf, sinf, False, name=tag + "_qrope")
    (o, og, lse), rode = _attn_fwd(q, qr, kv, kr, proj, name=tag + "_attn", ride=ride)
    y = _mm(og, w["out_w"], name=tag + "_out")
    h = _ln_fwd(u, y, ln_g, ln_b, name=tag + "_ln")
    saved = dict(u=u, w_in=w_in, w_q=w_q, w_kv=w_kv, qnw=qnw, kvnw=kvnw, proj=proj, qn=qn, kvn=kvn, kr=kr, q=q, kv=kv,
                 qr=qr, o=o, og=og, lse=lse, y=y)
    return h, saved, rode


def _mla_layer_bwd(s, w, ln_g, cosf, sinf, dr_up, du_up, tag, ride=None):
    dr, dg, db = _ln_bwd(s["u"], s["y"], ln_g, dr_up, du_up, name=tag + "_ln_b")
    dog = _mm(dr, w["out_w"], tb=True, name=tag + "_out_bx")
    d_out_w = _mm(s["og"], dr, ta=True, out_dtype=WIRE, name=tag + "_out_bw")
    do, dz = _gate_bwd(dog, s["o"], s["proj"], name=tag + "_gate_b")
    (dqn_h, dqr_rot), rode = _attn_bwd_q(s["q"], s["qr"], s["kv"], s["kr"], s["o"], do, s["lse"], name=tag + "_attn_bq",
                                         ride=ride)
    dkn_h, dv_h, dkr_rot = _attn_bwd_kv(s["q"], s["qr"], s["kv"], s["kr"], s["o"], do, s["lse"], name=tag + "_attn_bkv")
    dqr = _rope_heads(dqr_rot, 0, cosf, sinf, True, name=tag + "_qrope_b")
    dq = jnp.concatenate([dqn_h, dqr], axis=1)
    dkv = jnp.concatenate([dkn_h, dv_h], axis=1)
    d_wq = _mm(s["qn"], dq, ta=True, out_dtype=WIRE, name=tag + "_qup_bw")
    dqn = _mm(dq, s["w_q"], tb=True, name=tag + "_qup_bx")
    d_wkv = _mm(s["kvn"], dkv, ta=True, out_dtype=WIRE, name=tag + "_kvup_bw")
    dkvn = _mm(dkv, s["w_kv"], tb=True, name=tag + "_kvup_bx")
    dlow, dqnw, dkvnw = _mla_low_bwd(s["proj"], s["qnw"], s["kvnw"], cosf, sinf, dqn, dkvn, dkr_rot, name=tag + "_low_b")
    dproj = jnp.concatenate([dlow, dz], axis=1)
    d_in = _mm(s["u"], dproj, ta=True, out_dtype=WIRE, name=tag + "_in_bw")
    du = _mm(dproj, s["w_in"], tb=True, name=tag + "_in_bx")
    c1 = MLA_QR + MLA_KVR + MLA_ROPE
    d_in_w = jnp.concatenate([d_in[:, :c1], d_in[:, MLA_LOW:]], axis=1)
    dq3n = d_wq[:, :MLA_H * MLA_NOPE].reshape(MLA_QR, MLA_H, MLA_NOPE)
    dq3r = d_wq[:, MLA_H * MLA_NOPE:].reshape(MLA_QR, MLA_H, LANES)[:, :, :MLA_ROPE]
    d_q_up = jnp.concatenate([dq3n, dq3r], axis=2).reshape(MLA_QR, -1)
    dkv3 = d_wkv.reshape(MLA_KVR, 2, MLA_H, MLA_NOPE)
    d_kv_up = jnp.concatenate([dkv3[:, 0], dkv3[:, 1]], axis=2).reshape(MLA_KVR, -1)
    grads = dict(in_w=d_in_w, q_norm_w=dqnw[0], q_up_w=d_q_up, kv_norm_w=dkvnw[0], kv_up_w=d_kv_up, out_w=d_out_w,
                 ln_g=dg[0], ln_b=db[0])
    return dr, du, grads, rode


GDN_REP = GDN_HV // GDN_HK
GDN_A_LANE = GDN_HV
GDN_HPB = 4
GDN_VPB = GDN_HPB * GDN_REP


def _bdg(a, b, ca, cb):
    return lax.dot_general(a, b, (((ca,), (cb,)), ((0,), (0,))), preferred_element_type=F32)


@jax.custom_vjp
def _bnn(a, b):
    return _bdg(_mx(a), _mx(b), 2, 1)


def _bnn_f(a, b):
    return _bnn(a, b), (a, b)


def _bnn_b(res, ct):
    a, b = res
    return _bdg(_mx(ct), _mx(b), 2, 2), _bdg(_mx(a), _mx(ct), 1, 1)


_bnn.defvjp(_bnn_f, _bnn_b)


@jax.custom_vjp
def _bnt(a, b):
    return _bdg(_mx(a), _mx(b), 2, 2)


def _bnt_f(a, b):
    return _bnt(a, b), (a, b)


def _bnt_b(res, ct):
    a, b = res
    return _bdg(_mx(ct), _mx(b), 2, 1), _bdg(_mx(ct), _mx(a), 1, 1)


_bnt.defvjp(_bnt_f, _bnt_b)


@jax.custom_vjp
def _btn(a, b):
    return _bdg(_mx(a), _mx(b), 1, 1)


def _btn_f(a, b):
    return _btn(a, b), (a, b)


def _btn_b(res, ct):
    a, b = res
    return _bdg(_mx(b), _mx(ct), 2, 2), _bdg(_mx(a), _mx(ct), 2, 1)


_btn.defvjp(_btn_f, _btn_b)


def _h3(a, b, ca=2, cb=1):
    ah, bh = _mx(a), _mx(b)
    al, bl = _mx(a - ah.astype(F32)), _mx(b - bh.astype(F32))
    return _bdg(ah, bh, ca, cb) + (_bdg(ah, bl, ca, cb) + _bdg(al, bh, ca, cb))


@jax.custom_vjp
def _neumann_inverse(x):
    L = x.shape[-1]
    eye = (lax.broadcasted_iota(jnp.int32, (L, L), 0) == lax.broadcasted_iota(jnp.int32, (L, L), 1)).astype(F32)
    inv = eye + x
    xp = x
    for _ in range(L.bit_length() - 2):
        xp = _h3(xp, xp)
        inv = inv + _h3(inv, xp)
    return inv


def _neumann_f(x):
    inv = _neumann_inverse(x)
    return inv, inv


def _neumann_b(inv, ct):
    return (_h3(_h3(inv, ct, 1, 1), inv, 2, 2),)


_neumann_inverse.defvjp(_neumann_f, _neumann_b)


def _cat0(parts):
    return jnp.concatenate([p[None] for p in parts], axis=0)


def _gdn_chunk(hb, q, k, v, z, ba, alog, dtb, nw, s):
    L = q.shape[1]
    r_i = lax.broadcasted_iota(jnp.int32, (L, L), 0)
    c_i = lax.broadcasted_iota(jnp.int32, (L, L), 1)
    incl, strict = r_i >= c_i, r_i > c_i
    rep = lambda t: jnp.broadcast_to(t[:, None], (GDN_HPB, GDN_REP) + t.shape[1:]).reshape((GDN_VPB,) + t.shape[1:])
    qn = rep(q * lax.rsqrt(jnp.sum(q * q, -1, keepdims=True) + RMS_EPS) * (GDN_DK ** -0.5))
    kn = rep(k * lax.rsqrt(jnp.sum(k * k, -1, keepdims=True) + RMS_EPS))
    beta_all = jax.nn.sigmoid(ba)
    g_all = -jnp.exp(alog) * _softplus(ba + dtb)
    gcs_all = _sel_l(incl.astype(F32), g_all)
    lane = lax.broadcasted_iota(jnp.int32, (L, LANES), 1)
    pick = lambda mat, idx: jnp.sum(jnp.where(lane == idx, mat, 0.0), axis=1, keepdims=True)
    beta = _cat0([pick(beta_all, GDN_VPB * hb + b) for b in range(GDN_VPB)])
    gc = _cat0([pick(gcs_all, GDN_A_LANE + GDN_VPB * hb + b) for b in range(GDN_VPB)])
    gm = jnp.broadcast_to(gc, (GDN_VPB, L, L))
    decay = jnp.exp(jnp.where(incl, gm - jnp.swapaxes(gm, 1, 2), -jnp.inf))
    kb = kn * beta
    eg = jnp.exp(gc)
    inv = _neumann_inverse(-jnp.where(strict, _bnt(kb, kn) * decay, 0.0))
    uw = _bnn(inv, jnp.concatenate([v * beta, kb * eg], axis=2))
    uu, ww = uw[:, :, :GDN_DV], uw[:, :, GDN_DV:]
    qk = jnp.where(incl, _bnt(qn, kn) * decay, 0.0)
    last = lax.broadcasted_iota(jnp.int32, (L, 1), 0) == L - 1
    glast = jnp.sum(jnp.where(last, gc, 0.0), axis=1, keepdims=True)
    kdec = kn * jnp.exp(glast - gc)
    vnew = uu - _bnn(ww, s)
    o = _bnn(qn * eg, s) + _bnn(qk, vnew)
    new = s * jnp.exp(glast) + _btn(kdec, vnew)
    on = o * lax.rsqrt(jnp.mean(o * o, -1, keepdims=True) + RMS_EPS) * nw * _silu(z)
    return on, new


def _heads_of(ref, n):
    return _cat0([ref[:, i * LANES:(i + 1) * LANES] for i in range(n)])


def _gdn_specs(nc, rev):
    cc = (lambda c: nc - 1 - c) if rev else (lambda c: c)
    wq, wv = GDN_HPB * GDN_DK, GDN_VPB * GDN_DV
    par = pl.BlockSpec((1, LANES), lambda c, h: (0, 0))
    return dict(
        q=pl.BlockSpec((GDN_L, wq), lambda c, h: (cc(c), h)),
        k=pl.BlockSpec((GDN_L, wq), lambda c, h: (cc(c), GDN_KEY // wq + h)),
        v=pl.BlockSpec((GDN_L, wv), lambda c, h: (cc(c), 2 * GDN_KEY // wv + h)),
        z=pl.BlockSpec((GDN_L, wv), lambda c, h: (cc(c), GDN_CONV_DIM // wv + h)),
        ba=pl.BlockSpec((GDN_L, LANES), lambda c, h: (cc(c), (GDN_CONV_DIM + GDN_VAL) // LANES)),
        par=par,
        st=pl.BlockSpec((1, GDN_VPB, GDN_DK, GDN_DV), lambda c, h: (cc(c), h, 0, 0)),
        o=pl.BlockSpec((GDN_L, wv), lambda c, h: (cc(c), h)),
        qk_out=pl.BlockSpec((GDN_L, wq), lambda c, h: (cc(c), h)),
        ba_out=pl.BlockSpec((GDN_L, LANES), lambda c, h: (cc(c), 0)),
    )


def _gdn_fwd(proj, act, alog, dtb, nw, name, ride=None):
    t = proj.shape[0]
    nc = t // GDN_L
    sp = _gdn_specs(nc, False)

    def body(q_ref, k_ref, v_ref, z_ref, ba_ref, alog_ref, dtb_ref, nw_ref, o_ref, st_ref, state):
        c, h = pl.program_id(0), pl.program_id(1)

        mine = pl.ds(h * GDN_VPB, GDN_VPB)

        @pl.when(c == 0)
        def _():
            state[mine] = jnp.zeros((GDN_VPB, GDN_DK, GDN_DV), F32)

        prev = state[mine]
        st_ref[0] = prev
        on, new = _gdn_chunk(h, _heads_of(q_ref, GDN_HPB), _heads_of(k_ref, GDN_HPB), _heads_of(v_ref, GDN_VPB),
                             _heads_of(z_ref, GDN_VPB), ba_ref[...], alog_ref[...], dtb_ref[...], nw_ref[...], prev)
        for b in range(GDN_VPB):
            o_ref[:, b * LANES:(b + 1) * LANES] = on[b].astype(o_ref.dtype)
        state[mine] = new

    sds = jax.ShapeDtypeStruct
    return _hosted_call(
        body, name=name, grid=(nc, GDN_HK // GDN_HPB),
        in_specs=[sp["q"], sp["k"], sp["v"], sp["z"], sp["ba"], sp["par"], sp["par"], sp["par"]],
        out_specs=[sp["o"], sp["st"]],
        out_shape=[sds((t, GDN_VAL), F32), sds((nc, GDN_HV, GDN_DK, GDN_DV), F32)],
        scratch_shapes=[pltpu.VMEM((GDN_HV, GDN_DK, GDN_DV), F32)],
        sem=("arbitrary", "arbitrary"), args=(act, act, act, proj, proj, alog, dtb, nw), ride=ride)


def _gdn_bwd(proj, act, alog, dtb, nw, states, don, name, ride=None):
    t = proj.shape[0]
    nc = t // GDN_L
    sp = _gdn_specs(nc, True)

    def body(q_ref, k_ref, v_ref, z_ref, ba_ref, alog_ref, dtb_ref, nw_ref, st_ref, do_ref,
             dq_ref, dk_ref, dv_ref, dz_ref, dba_ref, dalog_ref, ddtb_ref, dnw_ref, dstate):
        c, h = pl.program_id(0), pl.program_id(1)

        mine = pl.ds(h * GDN_VPB, GDN_VPB)

        @pl.when(c == 0)
        def _():
            dstate[mine] = jnp.zeros((GDN_VPB, GDN_DK, GDN_DV), F32)

        @pl.when((c == 0) & (h == 0))
        def _():
            dalog_ref[...] = jnp.zeros_like(dalog_ref)
            ddtb_ref[...] = jnp.zeros_like(ddtb_ref)
            dnw_ref[...] = jnp.zeros_like(dnw_ref)

        @pl.when(h == 0)
        def _():
            dba_ref[...] = jnp.zeros_like(dba_ref)

        _, vjp = jax.vjp(functools.partial(_gdn_chunk, h), _heads_of(q_ref, GDN_HPB), _heads_of(k_ref, GDN_HPB),
                         _heads_of(v_ref, GDN_VPB), _heads_of(z_ref, GDN_VPB), ba_ref[...], alog_ref[...], dtb_ref[...],
                         nw_ref[...], st_ref[0])
        dq, dk, dv, dz, dba, dalog, ddtb, dnw, dprev = vjp((_heads_of(do_ref, GDN_VPB), dstate[mine]))
        for i in range(GDN_HPB):
            dq_ref[:, i * LANES:(i + 1) * LANES] = dq[i]
            dk_ref[:, i * LANES:(i + 1) * LANES] = dk[i]
        for b in range(GDN_VPB):
            dv_ref[:, b * LANES:(b + 1) * LANES] = dv[b]
            dz_ref[:, b * LANES:(b + 1) * LANES] = dz[b]
        dba_ref[...] += dba
        dalog_ref[...] += dalog
        ddtb_ref[...] += ddtb
        dnw_ref[...] += dnw
        dstate[mine] = dprev

    sds = jax.ShapeDtypeStruct
    par_out = pl.BlockSpec((1, LANES), lambda c, h: (0, 0))
    return _hosted_call(
        body, name=name, grid=(nc, GDN_HK // GDN_HPB),
        in_specs=[sp["q"], sp["k"], sp["v"], sp["z"], sp["ba"], sp["par"], sp["par"], sp["par"], sp["st"], sp["o"]],
        out_specs=[sp["qk_out"], sp["qk_out"], sp["o"], sp["o"], sp["ba_out"], par_out, par_out, par_out],
        out_shape=[sds((t, GDN_KEY), F32), sds((t, GDN_KEY), F32), sds((t, GDN_VAL), F32), sds((t, GDN_VAL), F32),
                   sds((t, LANES), F32), sds((1, LANES), F32), sds((1, LANES), F32), sds((1, LANES), F32)],
        scratch_shapes=[pltpu.VMEM((GDN_HV, GDN_DK, GDN_DV), F32)],
        sem=("arbitrary", "arbitrary"), args=(act, act, act, proj, proj, alog, dtb, nw, states, don), ride=ride)


GDN_PROJ = GDN_CONV_DIM + GDN_VAL + 2 * GDN_HV


def _gdn_layer_fwd(u, w, ln_g, ln_b, tag, ride=None):
    w_in = jnp.concatenate([w["in_w"], jnp.zeros((D_MODEL, GDN_PROJ_PAD - GDN_PROJ), w["in_w"].dtype)], axis=1)
    alog = _pad_lanes(w["a_log"], offset=GDN_A_LANE)
    dtb = _pad_lanes(w["dt_bias"], offset=GDN_A_LANE)
    nw = w["norm_w"].reshape(1, GDN_DV)
    zb = jnp.zeros((1, GDN_CONV_DIM), F32)
    proj = _mm(u, w_in, name=tag + "_in")
    act = _conv_fwd(proj, 0, w["conv_w"], zb, name=tag + "_conv")
    (on, states), rode = _gdn_fwd(proj, act, alog, dtb, nw, name=tag + "_delta", ride=ride)
    y = _mm(on, w["out_w"], name=tag + "_out")
    h = _ln_fwd(u, y, ln_g, ln_b, name=tag + "_ln")
    saved = dict(u=u, w_in=w_in, proj=proj, act=act, states=states, on=on, y=y, alog=alog, dtb=dtb, nw=nw, zb=zb)
    return h, saved, rode


def _gdn_layer_bwd(s, w, ln_g, dr_up, du_up, tag, ride=None):
    t = s["u"].shape[0]
    dr, dg, db = _ln_bwd(s["u"], s["y"], ln_g, dr_up, du_up, name=tag + "_ln_b")
    don = _mm(dr, w["out_w"], tb=True, name=tag + "_out_bx")
    d_out_w = _mm(s["on"], dr, ta=True, out_dtype=WIRE, name=tag + "_out_bw")
    (dq, dk, dv, dz, dba, dalog, ddtb, dnw), rode = _gdn_bwd(s["proj"], s["act"], s["alog"], s["dtb"], s["nw"], s["states"],
                                                            don, name=tag + "_delta_b", ride=ride)
    dact = jnp.concatenate([dq, dk, dv], axis=1)
    dpre, d_conv_w, _ = _conv_bwd_pre(s["proj"], 0, w["conv_w"], s["zb"], dact, name=tag + "_conv_bp")
    dqkv = _conv_bwd_x(dpre, w["conv_w"], name=tag + "_conv_bx")
    dproj = jnp.concatenate([dqkv, dz, dba, jnp.zeros((t, GDN_PROJ_PAD - GDN_PROJ - (LANES - 2 * GDN_HV)), F32)], axis=1)
    d_in = _mm(s["u"], dproj, ta=True, out_dtype=WIRE, name=tag + "_in_bw")
    du = _mm(dproj, s["w_in"], tb=True, name=tag + "_in_bx")
    grads = dict(in_w=d_in[:, :GDN_PROJ], conv_w=d_conv_w, a_log=dalog[0, GDN_A_LANE:GDN_A_LANE + GDN_HV],
                 dt_bias=ddtb[0, GDN_A_LANE:GDN_A_LANE + GDN_HV], norm_w=dnw[0], out_w=d_out_w, ln_g=dg[0], ln_b=db[0])
    return dr, du, grads, rode


def _mesh_pos():
    return lax.axis_index("x"), lax.axis_index("y"), lax.axis_index("c")


def _peer(k, x, y, c):
    return ((1 - x) if k & 4 else x, (1 - y) if k & 2 else y, (1 - c) if k & 1 else c)


def _ride_copies(ins, outs, send, recv, loc, scatter, with_arrivals):
    n = len(ins)
    x, y, c = _mesh_pos()
    me = 4 * x + 2 * y + c
    local = [pltpu.make_async_copy(ins[i].at[me] if scatter else ins[i], outs[i].at[me], loc.at[i]) for i in range(n)]
    sends, arrivals = [], []
    for k in range(1, N_DEV):
        peer = _peer(k, x, y, c)
        pidx = 4 * peer[0] + 2 * peer[1] + peer[2]
        for i in range(n):
            src = ins[i].at[pidx] if scatter else ins[i]
            sems = dict(send_sem=send.at[i, k - 1], recv_sem=recv.at[i, k - 1], device_id=peer,
                        device_id_type=pl.DeviceIdType.MESH)
            sends.append(pltpu.make_async_remote_copy(src_ref=src, dst_ref=outs[i].at[me], **sems))
            if with_arrivals:
                arrivals.append(pltpu.make_async_remote_copy(src_ref=src, dst_ref=outs[i].at[pidx], **sems))
    return local, sends, arrivals


def _ride_start(ins, outs, send, recv, loc, scatter):
    local, sends, _ = _ride_copies(ins, outs, send, recv, loc, scatter, False)
    for cp in local + sends:
        cp.start()


def _ride_wait(ins, outs, send, recv, loc, scatter):
    local, sends, arrivals = _ride_copies(ins, outs, send, recv, loc, scatter, True)
    for cp in arrivals:
        cp.wait_recv()
    for cp in sends:
        cp.wait_send()
    for cp in local:
        cp.wait()


def _ride_shapes(arrs, scatter):
    n = len(arrs)
    out_shape = [jax.ShapeDtypeStruct(a.shape if scatter else (N_DEV,) + a.shape, a.dtype) for a in arrs]
    scratch = [pltpu.SemaphoreType.DMA((n, N_DEV - 1)), pltpu.SemaphoreType.DMA((n, N_DEV - 1)), pltpu.SemaphoreType.DMA((n,))]
    return out_shape, scratch


def _exchange(arrs, scatter, name):
    n = len(arrs)
    hbm = pl.BlockSpec(memory_space=pltpu.HBM)

    def body(*refs):
        ins, outs = refs[:n], refs[n:2 * n]
        _ride_start(ins, outs, *refs[2 * n:], scatter)
        _ride_wait(ins, outs, *refs[2 * n:], scatter)

    out_shape, scratch = _ride_shapes(arrs, scatter)
    return pl.pallas_call(
        body, name=name, in_specs=[hbm] * n, out_specs=[hbm] * n, out_shape=out_shape, scratch_shapes=scratch,
        compiler_params=pltpu.CompilerParams(has_side_effects=True),
    )(*arrs)


def _hosted_call(body, *, name, grid, in_specs, out_specs, out_shape, scratch_shapes, sem, args, ride=None):
    if ride is None:
        return pl.pallas_call(body, name=name, grid=grid, in_specs=in_specs, out_specs=out_specs, out_shape=out_shape,
                              scratch_shapes=scratch_shapes, compiler_params=_cparams(sem))(*args), []
    arrs, scatter = ride
    n, ni, no, ns = len(arrs), len(in_specs), len(out_specs), len(scratch_shapes)
    hbm = pl.BlockSpec(memory_space=pltpu.HBM)
    r_shape, r_scratch = _ride_shapes(arrs, scatter)

    def full(*refs):
        a, ri = refs[:ni], refs[ni:ni + n]
        o, ro = refs[ni + n:ni + n + no], refs[ni + n + no:ni + 2 * n + no]
        s, rs = refs[ni + 2 * n + no:ni + 2 * n + no + ns], refs[ni + 2 * n + no + ns:]
        ids = [pl.program_id(d) for d in range(len(grid))]
        first, last = ids[0] == 0, ids[0] == grid[0] - 1
        for d in range(1, len(grid)):
            first, last = first & (ids[d] == 0), last & (ids[d] == grid[d] - 1)

        @pl.when(first)
        def _():
            _ride_start(ri, ro, *rs, scatter)

        body(*a, *o, *s)

        @pl.when(last)
        def _():
            _ride_wait(ri, ro, *rs, scatter)

    outs = pl.pallas_call(
        full, name=name, grid=grid, in_specs=list(in_specs) + [hbm] * n, out_specs=list(out_specs) + [hbm] * n,
        out_shape=list(out_shape) + r_shape, scratch_shapes=list(scratch_shapes) + r_scratch,
        compiler_params=pltpu.CompilerParams(dimension_semantics=("arbitrary",) * len(grid), vmem_limit_bytes=VMEM_LIMIT,
                                             has_side_effects=True),
    )(*args, *arrs)
    return outs[:no], list(outs[no:])


def _unshard(g, ax):
    g = jnp.moveaxis(g, 0, ax)
    sh = g.shape
    return g.reshape(sh[:ax] + (sh[ax] * sh[ax + 1],) + sh[ax + 2:])


def _to_parts(full, ax):
    sh = full.shape
    full = full.reshape(sh[:ax] + (N_DEV, sh[ax] // N_DEV) + sh[ax + 1:])
    return jnp.moveaxis(full, ax, 0)


def _row_tile(r, c):
    cap = max(8, (256 * 1024) // max(c, 1))
    best = None
    for d in range(8, min(r, cap) + 1, 8):
        if r % d == 0:
            best = d
    return r if best is None else best


def _adamw(w, m, v, parts, name):
    r, c = w.shape
    tr = _row_tile(r, c)

    def body(w_ref, m_ref, v_ref, p_ref, g_ref, d_ref, nm_ref, nv_ref):
        g = p_ref[0].astype(F32)
        for q in range(1, N_DEV):
            g = g + p_ref[q].astype(F32)
        nm = ADAM_B1 * m_ref[...] + (1.0 - ADAM_B1) * g
        nv = ADAM_B2 * v_ref[...] + (1.0 - ADAM_B2) * (g * g)
        m_hat = nm / (1.0 - ADAM_B1 ** ADAM_STEP)
        v_hat = nv / (1.0 - ADAM_B2 ** ADAM_STEP)
        g_ref[...] = g
        d_ref[...] = -ADAM_LR * (m_hat / (jnp.sqrt(v_hat) + ADAM_EPS) + ADAM_WD * w_ref[...])
        nm_ref[...] = nm
        nv_ref[...] = nv

    row = pl.BlockSpec((tr, c), lambda i: (i, 0))
    out = jax.ShapeDtypeStruct((r, c), F32)
    return pl.pallas_call(
        body, name=name, grid=(r // tr,),
        in_specs=[row, row, row, pl.BlockSpec((N_DEV, tr, c), lambda i: (0, i, 0))],
        out_specs=[row] * 4, out_shape=[out] * 4, compiler_params=_cparams(("parallel",)),
    )(w, m, v, parts)


WEIGHTS = ['ssd_in_w', 'ssd_conv_w', 'ssd_conv_b', 'ssd_dt_bias', 'ssd_a_log', 'ssd_d', 'ssd_norm_w', 'ssd_out_w',
           'mla_in_w', 'mla_q_norm_w', 'mla_q_up_w', 'mla_kv_norm_w', 'mla_kv_up_w', 'mla_out_w', 'gdn_in_w',
           'gdn_conv_w', 'gdn_a_log', 'gdn_dt_bias', 'gdn_norm_w', 'gdn_out_w', 'ln_g', 'ln_b']
SHARDED = {'ssd_in_w': (1, True), 'ssd_conv_w': (1, False), 'ssd_conv_b': (0, False), 'ssd_norm_w': (0, False),
           'ssd_out_w': (0, True), 'mla_in_w': (1, True), 'mla_q_up_w': (1, True), 'mla_kv_up_w': (1, True),
           'mla_out_w': (0, True), 'gdn_in_w': (1, True), 'gdn_conv_w': (1, False), 'gdn_out_w': (0, True)}
REPLICATED = [n for n in WEIGHTS if n not in SHARDED]


def _pack_small(vals):
    flat = jnp.concatenate([vals[n].reshape(-1).astype(F32) for n in REPLICATED])
    rows = -(-flat.shape[0] // (8 * LANES)) * 8
    return jnp.pad(flat, (0, rows * LANES - flat.shape[0])).reshape(rows, LANES)


def _unpack_small(slab, like):
    flat = slab.reshape(-1)
    out, off = {}, 0
    for n in REPLICATED:
        sz = like[n].size
        out[n] = flat[off:off + sz].reshape(like[n].shape)
        off += sz
    return out


def kernel(x, positions, ssd_in_w, ssd_conv_w, ssd_conv_b, ssd_dt_bias, ssd_a_log, ssd_d, ssd_norm_w, ssd_out_w, mla_in_w, mla_q_norm_w, mla_q_up_w, mla_kv_norm_w, mla_kv_up_w, mla_out_w, gdn_in_w, gdn_conv_w, gdn_a_log, gdn_dt_bias, gdn_norm_w, gdn_out_w, ln_g, ln_b, loss_target, m_ssd_in_w, m_ssd_conv_w, m_ssd_conv_b, m_ssd_dt_bias, m_ssd_a_log, m_ssd_d, m_ssd_norm_w, m_ssd_out_w, m_mla_in_w, m_mla_q_norm_w, m_mla_q_up_w, m_mla_kv_norm_w, m_mla_kv_up_w, m_mla_out_w, m_gdn_in_w, m_gdn_conv_w, m_gdn_a_log, m_gdn_dt_bias, m_gdn_norm_w, m_gdn_out_w, m_ln_g, m_ln_b, v_ssd_in_w, v_ssd_conv_w, v_ssd_conv_b, v_ssd_dt_bias, v_ssd_a_log, v_ssd_d, v_ssd_norm_w, v_ssd_out_w, v_mla_in_w, v_mla_q_norm_w, v_mla_q_up_w, v_mla_kv_norm_w, v_mla_kv_up_w, v_mla_out_w, v_gdn_in_w, v_gdn_conv_w, v_gdn_a_log, v_gdn_dt_bias, v_gdn_norm_w, v_gdn_out_w, v_ln_g, v_ln_b):
    loc = locals()
    w = {n: loc[n] for n in WEIGHTS}
    m = {n: loc["m_" + n] for n in WEIGHTS}
    v = {n: loc["v_" + n] for n in WEIGHTS}
    xs, pos, tgt = x[0], positions[0], loss_target[0]

    def names_of(prefix):
        return [n for n in SHARDED if n.startswith(prefix + "_")]

    def shards(prefix, j):
        return [w[n][j].astype(WIRE) if SHARDED[n][1] else w[n][j] for n in names_of(prefix)]

    def assemble(prefix, j, gathered):
        lw = {n[len(prefix) + 1:]: _unshard(g, SHARDED[n][0]) for n, g in zip(names_of(prefix), gathered) if g is not None}
        lw.update({n[len(prefix) + 1:]: w[n][j] for n in REPLICATED if n.startswith(prefix + "_")})
        return lw

    def parts(prefix, g):
        return [_to_parts(g[n[len(prefix) + 1:]], SHARDED[n][0]).astype(WIRE if SHARDED[n][1] else F32)
                for n in names_of(prefix)]

    lg = lambda i: w["ln_g"][i].reshape(1, D_MODEL)
    lb = lambda i: w["ln_b"][i].reshape(1, D_MODEL)
    cosf, sinf = _rope_tables(pos)

    sh0 = shards("ssd", 0)
    w_s0 = assemble("ssd", 0, list(_exchange(sh0[:1], False, name="gather_ssd0")) + [None] * (len(sh0) - 1))

    def late0(got):
        return {n[4:]: _unshard(g, SHARDED[n][0]) for n, g in zip(names_of("ssd")[1:], got)}

    h1, s0, got = _ssd_layer_fwd(xs, w_s0, lg(0), lb(0), "ssd0", ride=(shards("mla", 0), False), late=(sh0[1:], late0))
    w_m0 = assemble("mla", 0, got)
    h2, s1, got = _mla_layer_fwd(h1, w_m0, lg(1), lb(1), cosf, sinf, "mla0", ride=(shards("gdn", 0), False))
    w_g0 = assemble("gdn", 0, got)
    h3, s2, got = _gdn_layer_fwd(h2, w_g0, lg(2), lb(2), "gdn0", ride=(shards("ssd", 1), False))
    w_s1 = assemble("ssd", 1, got)
    h4, s3, _ = _ssd_layer_fwd(h3, w_s1, lg(3), lb(3), "ssd1")
    loss_tile, dl = _loss_head(h4, tgt, name="loss_head")

    def own0(k, g):
        n = "ssd_" + k
        return _to_parts(g, SHARDED[n][0]).astype(WIRE if SHARDED[n][1] else F32)

    dr3, du3, g3, _, _ = _ssd_layer_bwd(s3, w_s1, lg(3), jnp.zeros_like(dl), dl, "ssd1")
    dr2, du2, g2, r3 = _gdn_layer_bwd(s2, w_g0, lg(2), dr3, du3, "gdn0", ride=(parts("ssd", g3), True))
    dr1, du1, g1, r2 = _mla_layer_bwd(s1, w_m0, lg(1), cosf, sinf, dr2, du2, "mla0", ride=(parts("gdn", g2), True))
    dr0, du0, g0, r1, r0 = _ssd_layer_bwd(s0, w_s0, lg(0), dr1, du1, "ssd0", ride=(parts("mla", g1), True), own=own0)
    grad_x = _axpy(dr0, du0, name="grad_x")[None]

    gsmall = {"ssd_" + k: jnp.stack([g0[k], g3[k]]) for k in ("dt_bias", "a_log", "d")}
    gsmall.update({"mla_" + k: g1[k][None] for k in ("q_norm_w", "kv_norm_w")})
    gsmall.update({"gdn_" + k: g2[k][None] for k in ("a_log", "dt_bias", "norm_w")})
    gsmall["ln_g"] = jnp.stack([g0["ln_g"], g1["ln_g"], g2["ln_g"], g3["ln_g"]])
    gsmall["ln_b"] = jnp.stack([g0["ln_b"], g1["ln_b"], g2["ln_b"], g3["ln_b"]])
    small = _pack_small(gsmall)
    rsmall, = _exchange([jnp.broadcast_to(small[None], (N_DEV,) + small.shape)], True, name="gather_small_grads")

    recvd = {n: jnp.stack([r0[n[4:]], b], axis=1) for n, b in zip(names_of("ssd"), r3)}
    recvd.update({n: a[:, None] for n, a in zip(names_of("mla"), r1)})
    recvd.update({n: a[:, None] for n, a in zip(names_of("gdn"), r2)})
    recvd = [recvd[n] for n in SHARDED] + [rsmall]

    grads, deltas, new_m, new_v = {}, {}, {}, {}
    for n, pt in zip(SHARDED, recvd[:-1]):
        shp = w[n].shape
        r2d = (-1, shp[-1])
        outs = _adamw(w[n].reshape(r2d), m[n].reshape(r2d), v[n].reshape(r2d), pt.reshape((N_DEV,) + w[n].reshape(r2d).shape),
                      name="adamw_" + n)
        grads[n], deltas[n], new_m[n], new_v[n] = (o.reshape(shp) for o in outs)
    outs = _adamw(_pack_small(w), _pack_small(m), _pack_small(v), recvd[-1], name="adamw_replicated")
    for dst, o in zip((grads, deltas, new_m, new_v), outs):
        dst.update(_unpack_small(o, w))

    loss = lax.psum(loss_tile[0, 0], ("x", "y", "c"))
    return (loss, grad_x, *[grads[n] for n in WEIGHTS], *[deltas[n] for n in WEIGHTS],
            *[new_m[n] for n in WEIGHTS], *[new_v[n] for n in WEIGHTS])
```

```python
import functools

import jax
import jax.numpy as jnp
from jax import lax
from jax.experimental import pallas as pl
from jax.experimental.pallas import tpu as pltpu

F32 = jnp.float32
MXU = jnp.bfloat16
WIRE = jnp.bfloat16
HI = lax.Precision.HIGHEST

N_DEV = 8
LANES = 128
VMEM_LIMIT = 56 * 1024 * 1024

D_MODEL = 2048
DEPTH = 4
ALPHA = (2.0 * DEPTH) ** 0.25
LN_EPS = 1e-5
RMS_EPS = 1e-6

SSD_DI = 4096
SSD_P = 64
SSD_H = 64
SSD_G = 8
SSD_N = 128
SSD_L = 128
SSD_GS = SSD_DI // SSD_G
SSD_CONV_DIM = SSD_DI + 2 * SSD_G * SSD_N
SSD_PROJ_PAD = SSD_DI + SSD_CONV_DIM + LANES

MLA_H = 16
MLA_QR = 768
MLA_KVR = 512
MLA_NOPE = 128
MLA_ROPE = 64
MLA_V = 128
MLA_GATE = MLA_H * MLA_V
MLA_PROJ_PAD = MLA_QR + MLA_KVR + LANES + MLA_GATE
MLA_SCALE = (MLA_NOPE + MLA_ROPE) ** -0.5
ROPE_THETA = 10000.0
ATT_BLK = 512

GDN_HK = 16
GDN_HV = 32
GDN_DK = 128
GDN_DV = 128
GDN_KEY = GDN_HK * GDN_DK
GDN_VAL = GDN_HV * GDN_DV
GDN_L = 128
GDN_CONV_DIM = 2 * GDN_KEY + GDN_VAL
GDN_PROJ_PAD = 12800

ADAM_LR = 0.001
ADAM_B1 = 0.9
ADAM_B2 = 0.999
ADAM_EPS = 1e-08
ADAM_WD = 0.01
ADAM_STEP = 10


def _cparams(sem=None):
    return pltpu.CompilerParams(dimension_semantics=sem, vmem_limit_bytes=VMEM_LIMIT)


def _tile(n, cap):
    if n <= cap:
        return n
    best = None
    for d in range(LANES, cap + 1, LANES):
        if n % d == 0:
            best = d
    assert best is not None, (n, cap)
    return best


def _dg(a, b, ca, cb, prec=None):
    return lax.dot_general(a, b, (((ca,), (cb,)), ((), ())), preferred_element_type=F32, precision=prec)


def _mx(a):
    return a.astype(MXU)


@jax.custom_vjp
def _nn(a, b):
    return _dg(_mx(a), _mx(b), 1, 0)


def _nn_f(a, b):
    return _nn(a, b), (a, b)


def _nn_b(res, ct):
    a, b = res
    return _dg(_mx(ct), _mx(b), 1, 1), _dg(_mx(a), _mx(ct), 0, 0)


_nn.defvjp(_nn_f, _nn_b)


@jax.custom_vjp
def _nt(a, b):
    return _dg(_mx(a), _mx(b), 1, 1)


def _nt_f(a, b):
    return _nt(a, b), (a, b)


def _nt_b(res, ct):
    a, b = res
    return _dg(_mx(ct), _mx(b), 1, 0), _dg(_mx(ct), _mx(a), 0, 0)


_nt.defvjp(_nt_f, _nt_b)


@jax.custom_vjp
def _tn(a, b):
    return _dg(_mx(a), _mx(b), 0, 0)


def _tn_f(a, b):
    return _tn(a, b), (a, b)


def _tn_b(res, ct):
    a, b = res
    return _dg(_mx(b), _mx(ct), 1, 1), _dg(_mx(a), _mx(ct), 1, 0)


_tn.defvjp(_tn_f, _tn_b)


def _softplus(x):
    return jnp.maximum(x, 0.0) + jnp.log(1.0 + jnp.exp(-jnp.abs(x)))


def _silu(x):
    return x * jax.nn.sigmoid(x)


MM_TM = 1024
MM_TN = 1280
MM_VMEM_BUDGET = 40 * 1024 * 1024


def _mm(a, b, *, ta=False, tb=False, out_dtype=F32, name, ride=None):
    if ta:
        kdim, m = a.shape
    else:
        m, kdim = a.shape
    if tb:
        n, kb = b.shape
    else:
        kb, n = b.shape
    assert kdim == kb, (a.shape, b.shape, ta, tb)
    tm, tn = _tile(m, MM_TM), _tile(n, MM_TN)
    abytes, bbytes, obytes = a.dtype.itemsize, b.dtype.itemsize, jnp.dtype(out_dtype).itemsize
    tk = LANES
    for d in range(LANES, kdim + 1, LANES):
        if kdim % d == 0 and 2 * d * (tm * abytes + tn * bbytes) + tm * tn * (2 * obytes + 4) <= MM_VMEM_BUDGET:
            tk = d
    nk = kdim // tk
    ca, cb = (0 if ta else 1), (1 if tb else 0)

    def body(a_ref, b_ref, o_ref, *acc):
        part = _dg(_mx(a_ref[...]), _mx(b_ref[...]), ca, cb)
        if nk == 1:
            o_ref[...] = part.astype(out_dtype)
            return
        acc_ref, = acc
        k = pl.program_id(2)

        @pl.when(k == 0)
        def _():
            acc_ref[...] = part

        @pl.when(k > 0)
        def _():
            acc_ref[...] += part

        @pl.when(k == nk - 1)
        def _():
            o_ref[...] = acc_ref[...].astype(out_dtype)

    a_spec = pl.BlockSpec((tk, tm), lambda i, j, k: (k, i)) if ta else pl.BlockSpec((tm, tk), lambda i, j, k: (i, k))
    b_spec = pl.BlockSpec((tn, tk), lambda i, j, k: (j, k)) if tb else pl.BlockSpec((tk, tn), lambda i, j, k: (k, j))
    (out,), rode = _hosted_call(
        body, name=name, grid=(m // tm, n // tn, nk),
        in_specs=[a_spec, b_spec], out_specs=[pl.BlockSpec((tm, tn), lambda i, j, k: (i, j))],
        out_shape=[jax.ShapeDtypeStruct((m, n), out_dtype)],
        scratch_shapes=[pltpu.VMEM((tm, tn), F32)] if nk > 1 else [],
        sem=("parallel", "parallel", "arbitrary"), args=(a, b), ride=ride)
    return out if ride is None else (out, rode)


def _ln_fwd(h, y, g, b, name):
    t, d = h.shape
    tr = _tile(t, 256)

    def body(h_ref, y_ref, g_ref, b_ref, o_ref):
        r = ALPHA * h_ref[...] + y_ref[...]
        mu = jnp.mean(r, -1, keepdims=True)
        xc = r - mu
        var = jnp.mean(xc * xc, -1, keepdims=True)
        o_ref[...] = xc * lax.rsqrt(var + LN_EPS) * g_ref[...] + b_ref[...]

    row = pl.BlockSpec((tr, d), lambda i: (i, 0))
    par = pl.BlockSpec((1, d), lambda i: (0, 0))
    return pl.pallas_call(
        body, name=name, grid=(t // tr,), in_specs=[row, row, par, par], out_specs=row,
        out_shape=jax.ShapeDtypeStruct((t, d), F32), compiler_params=_cparams(("parallel",)),
    )(h, y, g, b)


def _ln_bwd(h, y, g, dr_up, du_up, name):
    t, d = h.shape
    tr = _tile(t, 256)

    def body(h_ref, y_ref, g_ref, dr_ref, du_ref, o_ref, dg_ref, db_ref):
        i = pl.program_id(0)

        @pl.when(i == 0)
        def _():
            dg_ref[...] = jnp.zeros_like(dg_ref)
            db_ref[...] = jnp.zeros_like(db_ref)

        dout = ALPHA * dr_ref[...] + du_ref[...]
        r = ALPHA * h_ref[...] + y_ref[...]
        mu = jnp.mean(r, -1, keepdims=True)
        xc = r - mu
        rstd = lax.rsqrt(jnp.mean(xc * xc, -1, keepdims=True) + LN_EPS)
        xh = xc * rstd
        dxh = dout * g_ref[...]
        o_ref[...] = rstd * (dxh - jnp.mean(dxh, -1, keepdims=True) - xh * jnp.mean(dxh * xh, -1, keepdims=True))
        dg_ref[...] += jnp.sum(dout * xh, 0, keepdims=True)
        db_ref[...] += jnp.sum(dout, 0, keepdims=True)

    row = pl.BlockSpec((tr, d), lambda i: (i, 0))
    par = pl.BlockSpec((1, d), lambda i: (0, 0))
    return pl.pallas_call(
        body, name=name, grid=(t // tr,), in_specs=[row, row, par, row, row], out_specs=[row, par, par],
        out_shape=[jax.ShapeDtypeStruct((t, d), F32), jax.ShapeDtypeStruct((1, d), F32), jax.ShapeDtypeStruct((1, d), F32)],
        compiler_params=_cparams(("arbitrary",)),
    )(h, y, g, dr_up, du_up)


def _loss_head(h, tgt, name):
    t, d = h.shape
    tr = _tile(t, 256)

    def body(h_ref, t_ref, l_ref, d_ref):
        i = pl.program_id(0)

        @pl.when(i == 0)
        def _():
            l_ref[...] = jnp.zeros_like(l_ref)

        e = h_ref[...] - t_ref[...]
        d_ref[...] = e * (1.0 / d)
        l_ref[...] += 0.5 * jnp.sum(jnp.mean(e * e, -1, keepdims=True))

    row = pl.BlockSpec((tr, d), lambda i: (i, 0))
    return pl.pallas_call(
        body, name=name, grid=(t // tr,), in_specs=[row, row],
        out_specs=[pl.BlockSpec((8, LANES), lambda i: (0, 0)), row],
        out_shape=[jax.ShapeDtypeStruct((8, LANES), F32), jax.ShapeDtypeStruct((t, d), F32)],
        compiler_params=_cparams(("arbitrary",)),
    )(h, tgt)


def _axpy(dr, du, name):
    t, d = dr.shape
    tr = _tile(t, 256)

    def body(a_ref, b_ref, o_ref):
        o_ref[...] = ALPHA * a_ref[...] + b_ref[...]

    row = pl.BlockSpec((tr, d), lambda i: (i, 0))
    return pl.pallas_call(
        body, name=name, grid=(t // tr,), in_specs=[row, row], out_specs=row,
        out_shape=jax.ShapeDtypeStruct((t, d), F32), compiler_params=_cparams(("parallel",)),
    )(dr, du)


CONV_TT = 512
CONV_TC = 512


HALO = 8


def _shift_down(cur, halo, s, row):
    if s == 0:
        return cur
    tt = cur.shape[0]
    edge = jnp.concatenate([pltpu.roll(halo, s, 0), jnp.zeros((tt - HALO, cur.shape[1]), cur.dtype)], axis=0)
    return jnp.where(row >= s, pltpu.roll(cur, s, 0), edge)


def _shift_up(cur, halo, s, row, tt):
    if s == 0:
        return cur
    edge = jnp.concatenate([jnp.zeros((tt - HALO, cur.shape[1]), cur.dtype), pltpu.roll(halo, HALO - s, 0)], axis=0)
    return jnp.where(row < tt - s, pltpu.roll(cur, tt - s, 0), edge)


def _conv_fwd(proj, col0, w, b, name):
    t = proj.shape[0]
    c = w.shape[1]
    tt = _tile(t, CONV_TT)
    cb0 = col0 // CONV_TC

    def body(x_ref, p_ref, w_ref, b_ref, o_ref):
        i = pl.program_id(1)
        x = x_ref[...]
        p = jnp.where(i > 0, p_ref[...], 0.0)
        row = lax.broadcasted_iota(jnp.int32, x.shape, 0)
        pre = b_ref[...] + w_ref[3:4, :] * x
        for s in (1, 2, 3):
            pre = pre + w_ref[3 - s:4 - s, :] * _shift_down(x, p, s, row)
        o_ref[...] = _silu(pre)

    return pl.pallas_call(
        body, name=name, grid=(c // CONV_TC, t // tt),
        in_specs=[pl.BlockSpec((tt, CONV_TC), lambda j, i: (i, cb0 + j)),
                  pl.BlockSpec((HALO, CONV_TC), lambda j, i: (jnp.maximum(i * (tt // HALO) - 1, 0), cb0 + j)),
                  pl.BlockSpec((4, CONV_TC), lambda j, i: (0, j)),
                  pl.BlockSpec((1, CONV_TC), lambda j, i: (0, j))],
        out_specs=pl.BlockSpec((tt, CONV_TC), lambda j, i: (i, j)),
        out_shape=jax.ShapeDtypeStruct((t, c), F32), compiler_params=_cparams(("parallel", "parallel")),
    )(proj, proj, w, b)


def _conv_bwd_pre(proj, col0, w, b, dact, name):
    t = proj.shape[0]
    c = w.shape[1]
    tt = _tile(t, CONV_TT)
    cb0 = col0 // CONV_TC

    def body(x_ref, p_ref, w_ref, b_ref, d_ref, dpre_ref, dw_ref, db_ref):
        i = pl.program_id(1)

        @pl.when(i == 0)
        def _():
            dw_ref[...] = jnp.zeros_like(dw_ref)
            db_ref[...] = jnp.zeros_like(db_ref)

        x = x_ref[...]
        p = jnp.where(i > 0, p_ref[...], 0.0)
        row = lax.broadcasted_iota(jnp.int32, x.shape, 0)
        sh = [_shift_down(x, p, s, row) for s in range(4)]
        pre = b_ref[...] + w_ref[3:4, :] * sh[0]
        for s in (1, 2, 3):
            pre = pre + w_ref[3 - s:4 - s, :] * sh[s]
        sg = jax.nn.sigmoid(pre)
        dpre = d_ref[...] * (sg * (1.0 + pre * (1.0 - sg)))
        dpre_ref[...] = dpre
        for s in range(4):
            dw_ref[3 - s:4 - s, :] += jnp.sum(dpre * sh[s], 0, keepdims=True)
        db_ref[...] += jnp.sum(dpre, 0, keepdims=True)

    return pl.pallas_call(
        body, name=name, grid=(c // CONV_TC, t // tt),
        in_specs=[pl.BlockSpec((tt, CONV_TC), lambda j, i: (i, cb0 + j)),
                  pl.BlockSpec((HALO, CONV_TC), lambda j, i: (jnp.maximum(i * (tt // HALO) - 1, 0), cb0 + j)),
                  pl.BlockSpec((4, CONV_TC), lambda j, i: (0, j)),
                  pl.BlockSpec((1, CONV_TC), lambda j, i: (0, j)),
                  pl.BlockSpec((tt, CONV_TC), lambda j, i: (i, j))],
        out_specs=[pl.BlockSpec((tt, CONV_TC), lambda j, i: (i, j)),
                   pl.BlockSpec((4, CONV_TC), lambda j, i: (0, j)),
                   pl.BlockSpec((1, CONV_TC), lambda j, i: (0, j))],
        out_shape=[jax.ShapeDtypeStruct((t, c), F32), jax.ShapeDtypeStruct((4, c), F32), jax.ShapeDtypeStruct((1, c), F32)],
        compiler_params=_cparams(("parallel", "arbitrary")),
    )(proj, proj, w, b, dact)


def _conv_bwd_x(dpre, w, name):
    t, c = dpre.shape
    tt = _tile(t, CONV_TT)
    nt = t // tt

    def body(d_ref, n_ref, w_ref, o_ref):
        i = pl.program_id(1)
        d = d_ref[...]
        nx = jnp.where(i < nt - 1, n_ref[...], 0.0)
        row = lax.broadcasted_iota(jnp.int32, d.shape, 0)
        acc = w_ref[3:4, :] * d
        for s in (1, 2, 3):
            acc = acc + w_ref[3 - s:4 - s, :] * _shift_up(d, nx, s, row, tt)
        o_ref[...] = acc

    return pl.pallas_call(
        body, name=name, grid=(c // CONV_TC, nt),
        in_specs=[pl.BlockSpec((tt, CONV_TC), lambda j, i: (i, j)),
                  pl.BlockSpec((HALO, CONV_TC), lambda j, i: (jnp.minimum((i + 1) * (tt // HALO), t // HALO - 1), j)),
                  pl.BlockSpec((4, CONV_TC), lambda j, i: (0, j))],
        out_specs=pl.BlockSpec((tt, CONV_TC), lambda j, i: (i, j)),
        out_shape=jax.ShapeDtypeStruct((t, c), F32), compiler_params=_cparams(("parallel", "parallel")),
    )(dpre, dpre, w)


SSD_GB = 4


def _split3(a):
    a1 = _mx(a)
    r = a - a1.astype(F32)
    a2 = _mx(r)
    return a1, a2, _mx(r - a2.astype(F32))


@jax.custom_vjp
def _sel_r(a, c):
    cm = _mx(c)
    p1, p2, p3 = _split3(a)
    return _dg(p1, cm, 1, 0) + (_dg(p2, cm, 1, 0) + _dg(p3, cm, 1, 0))


def _sel_r_f(a, c):
    return _sel_r(a, c), c


def _sel_r_b(c, ct):
    cm = _mx(c)
    p1, p2, p3 = _split3(ct)
    return _dg(p1, cm, 1, 1) + (_dg(p2, cm, 1, 1) + _dg(p3, cm, 1, 1)), jnp.zeros_like(c)


_sel_r.defvjp(_sel_r_f, _sel_r_b)


@jax.custom_vjp
def _sel_l(c, a):
    cm = _mx(c)
    p1, p2, p3 = _split3(a)
    return _dg(cm, p1, 1, 0) + (_dg(cm, p2, 1, 0) + _dg(cm, p3, 1, 0))


def _sel_l_f(c, a):
    return _sel_l(c, a), c


def _sel_l_b(c, ct):
    cm = _mx(c)
    p1, p2, p3 = _split3(ct)
    return jnp.zeros_like(c), _dg(cm, p1, 0, 0) + (_dg(cm, p2, 0, 0) + _dg(cm, p3, 0, 0))


_sel_l.defvjp(_sel_l_f, _sel_l_b)


def _ssd_chunk(gb, x, z, bm, cm, dtraw, dtb, alog, dsk, nw, prev):
    L = x.shape[1]
    r_i = lax.broadcasted_iota(jnp.int32, (L, L), 0)
    c_i = lax.broadcasted_iota(jnp.int32, (L, L), 1)
    causal = r_i >= c_i
    dt = _softplus(dtraw + dtb)
    a = dt * (-jnp.exp(alog))
    acs = _sel_l(causal.astype(F32), a)
    e_r = lax.broadcasted_iota(jnp.int32, (LANES, SSD_GS), 0)
    e_c = lax.broadcasted_iota(jnp.int32, (LANES, SSD_GS), 1)
    hpg = SSD_H // SSD_G
    sels = [(e_r == (gb * SSD_GB + i) * hpg + jnp.right_shift(e_c, 6)).astype(F32) for i in range(SSD_GB)]
    dt_x = _cat0([_sel_r(dt, s) for s in sels])
    acs_x = _cat0([_sel_r(acs, s) for s in sels])
    d_x = _cat0([_sel_r(jnp.broadcast_to(dsk, (8, LANES)), s)[0:1] for s in sels])
    last = lax.broadcasted_iota(jnp.int32, (L, 1), 0) == L - 1
    alast = jnp.sum(jnp.where(last, acs_x, 0.0), axis=1, keepdims=True)
    xdt = x * dt_x
    cb = _bnt(cm, bm)
    lane = lax.broadcasted_iota(jnp.int32, (L, LANES), 1)
    ys = []
    for j in range(SSD_GS // LANES):
        xp = xdt[:, :, j * LANES:(j + 1) * LANES]
        yp = None
        for hh in range(2):
            c0 = (2 * j + hh) * SSD_P
            cmx = jnp.broadcast_to(acs_x[:, :, c0:c0 + 1], (SSD_GB, L, L))
            dec = jnp.exp(jnp.where(causal, cmx - jnp.swapaxes(cmx, 1, 2), -jnp.inf))
            half = (lane < SSD_P) if hh == 0 else (lane >= SSD_P)
            t = _bnn(cb * dec, jnp.where(half, xp, 0.0))
            yp = t if yp is None else yp + t
        ys.append(yp)
    y_diag = jnp.concatenate(ys, axis=2)
    st = _btn(bm, xdt * jnp.exp(alast - acs_x))
    new = prev * jnp.exp(alast) + st
    y_off = _bnn(cm, prev) * jnp.exp(acs_x)
    y = y_diag + y_off + x * d_x
    yg = y * _silu(z)
    yn = yg * lax.rsqrt(jnp.mean(yg * yg, -1, keepdims=True) + RMS_EPS) * nw
    return yn, new


def _groups_of(ref, width):
    return _cat0([ref[:, i * width:(i + 1) * width] for i in range(SSD_GB)])


def _put_groups(ref, val, width):
    for i in range(SSD_GB):
        ref[:, i * width:(i + 1) * width] = val[i].astype(ref.dtype)


def _ssd_specs(nc, rev):
    cc = (lambda c: nc - 1 - c) if rev else (lambda c: c)
    wx, wb = SSD_GB * SSD_GS, SSD_GB * SSD_N
    dtb = (SSD_DI + SSD_CONV_DIM) // LANES
    bb = SSD_DI // wb
    cbk = (SSD_DI + SSD_G * SSD_N) // wb
    par = pl.BlockSpec((1, LANES), lambda c, g: (0, 0))
    return dict(
        z=pl.BlockSpec((SSD_L, wx), lambda c, g: (cc(c), g)),
        dt=pl.BlockSpec((SSD_L, LANES), lambda c, g: (cc(c), dtb)),
        x=pl.BlockSpec((SSD_L, wx), lambda c, g: (cc(c), g)),
        bm=pl.BlockSpec((SSD_L, wb), lambda c, g: (cc(c), bb + g)),
        cm=pl.BlockSpec((SSD_L, wb), lambda c, g: (cc(c), cbk + g)),
        par=par,
        nw=pl.BlockSpec((1, wx), lambda c, g: (0, g)),
        st=pl.BlockSpec((1, SSD_GB, SSD_N, SSD_GS), lambda c, g: (cc(c), g, 0, 0)),
        y=pl.BlockSpec((SSD_L, wx), lambda c, g: (cc(c), g)),
        bc=pl.BlockSpec((SSD_L, wb), lambda c, g: (cc(c), g)),
        dtout=pl.BlockSpec((SSD_L, LANES), lambda c, g: (cc(c), 0)),
    )


def _ssd_fwd(proj, act, dtb, alog, dsk, nw, name, ride=None):
    t = proj.shape[0]
    nc = t // SSD_L
    sp = _ssd_specs(nc, False)

    def body(z_ref, dt_ref, x_ref, bm_ref, cm_ref, dtb_ref, alog_ref, dsk_ref, nw_ref, y_ref, st_ref, state):
        c, g = pl.program_id(0), pl.program_id(1)
        mine = pl.ds(g * SSD_GB, SSD_GB)

        @pl.when(c == 0)
        def _():
            state[mine] = jnp.zeros((SSD_GB, SSD_N, SSD_GS), F32)

        prev = state[mine]
        st_ref[0] = prev
        yn, new = _ssd_chunk(g, _groups_of(x_ref, SSD_GS), _groups_of(z_ref, SSD_GS), _groups_of(bm_ref, SSD_N),
                             _groups_of(cm_ref, SSD_N), dt_ref[...], dtb_ref[...], alog_ref[...], dsk_ref[...],
                             _groups_of(nw_ref, SSD_GS), prev)
        _put_groups(y_ref, yn, SSD_GS)
        state[mine] = new

    return _hosted_call(
        body, name=name, grid=(nc, SSD_G // SSD_GB),
        in_specs=[sp["z"], sp["dt"], sp["x"], sp["bm"], sp["cm"], sp["par"], sp["par"], sp["par"], sp["nw"]],
        out_specs=[sp["y"], sp["st"]],
        out_shape=[jax.ShapeDtypeStruct((t, SSD_DI), F32), jax.ShapeDtypeStruct((nc, SSD_G, SSD_N, SSD_GS), F32)],
        scratch_shapes=[pltpu.VMEM((SSD_G, SSD_N, SSD_GS), F32)],
        sem=("arbitrary", "arbitrary"), args=(proj, proj, act, act, act, dtb, alog, dsk, nw), ride=ride)


def _ssd_bwd(proj, act, dtb, alog, dsk, nw, states, dyn, name, ride=None):
    t = proj.shape[0]
    nc = t // SSD_L
    sp = _ssd_specs(nc, True)

    def body(z_ref, dt_ref, x_ref, bm_ref, cm_ref, dtb_ref, alog_ref, dsk_ref, nw_ref, st_ref, dy_ref,
             dx_ref, dz_ref, dbm_ref, dcm_ref, ddt_ref, ddtb_ref, dalog_ref, ddsk_ref, dnw_ref, dstate):
        c, g = pl.program_id(0), pl.program_id(1)
        mine = pl.ds(g * SSD_GB, SSD_GB)

        @pl.when(c == 0)
        def _():
            dstate[mine] = jnp.zeros((SSD_GB, SSD_N, SSD_GS), F32)

        @pl.when((c == 0) & (g == 0))
        def _():
            ddtb_ref[...] = jnp.zeros_like(ddtb_ref)
            dalog_ref[...] = jnp.zeros_like(dalog_ref)
            ddsk_ref[...] = jnp.zeros_like(ddsk_ref)
            dnw_ref[...] = jnp.zeros_like(dnw_ref)

        @pl.when(g == 0)
        def _():
            ddt_ref[...] = jnp.zeros_like(ddt_ref)

        _, vjp = jax.vjp(functools.partial(_ssd_chunk, g), _groups_of(x_ref, SSD_GS), _groups_of(z_ref, SSD_GS),
                         _groups_of(bm_ref, SSD_N), _groups_of(cm_ref, SSD_N), dt_ref[...], dtb_ref[...], alog_ref[...],
                         dsk_ref[...], _groups_of(nw_ref, SSD_GS), st_ref[0])
        dx, dz, dbm, dcm, ddt, ddtb, dalog, ddsk, dnw, dprev = vjp((_groups_of(dy_ref, SSD_GS), dstate[mine]))
        _put_groups(dx_ref, dx, SSD_GS)
        _put_groups(dz_ref, dz, SSD_GS)
        _put_groups(dbm_ref, dbm, SSD_N)
        _put_groups(dcm_ref, dcm, SSD_N)
        ddt_ref[...] += ddt
        ddtb_ref[...] += ddtb
        dalog_ref[...] += dalog
        ddsk_ref[...] += ddsk
        dnw_ref[mine] += dnw
        dstate[mine] = dprev

    par_out = pl.BlockSpec((1, LANES), lambda c, g: (0, 0))
    sds = jax.ShapeDtypeStruct
    return _hosted_call(
        body, name=name, grid=(nc, SSD_G // SSD_GB),
        in_specs=[sp["z"], sp["dt"], sp["x"], sp["bm"], sp["cm"], sp["par"], sp["par"], sp["par"], sp["nw"],
                  sp["st"], sp["y"]],
        out_specs=[sp["y"], sp["y"], sp["bc"], sp["bc"], sp["dtout"], par_out, par_out, par_out,
                   pl.BlockSpec((SSD_G, 1, SSD_GS), lambda c, g: (0, 0, 0))],
        out_shape=[sds((t, SSD_DI), F32), sds((t, SSD_DI), F32), sds((t, SSD_G * SSD_N), F32), sds((t, SSD_G * SSD_N), F32),
                   sds((t, LANES), F32), sds((1, LANES), F32), sds((1, LANES), F32), sds((1, LANES), F32),
                   sds((SSD_G, 1, SSD_GS), F32)],
        scratch_shapes=[pltpu.VMEM((SSD_G, SSD_N, SSD_GS), F32)],
        sem=("arbitrary", "arbitrary"), args=(proj, proj, act, act, act, dtb, alog, dsk, nw, states, dyn), ride=ride)


def _pad_lanes(v, width=LANES, offset=0):
    return jnp.pad(v.astype(F32), (offset, width - offset - v.shape[0])).reshape(1, width)


def _ssd_layer_fwd(u, w, ln_g, ln_b, tag, ride=None, late=None):
    w_in = jnp.concatenate([w["in_w"], jnp.zeros((D_MODEL, LANES - SSD_H), w["in_w"].dtype)], axis=1)
    dtb, alog, dsk = _pad_lanes(w["dt_bias"]), _pad_lanes(w["a_log"]), _pad_lanes(w["d"])
    if late is None:
        proj = _mm(u, w_in, name=tag + "_in")
    else:
        proj, got = _mm(u, w_in, name=tag + "_in", ride=(late[0], False))
        w.update(late[1](got))
    nw = w["norm_w"].reshape(1, SSD_DI)
    cb = w["conv_b"].reshape(1, SSD_CONV_DIM)
    act = _conv_fwd(proj, SSD_DI, w["conv_w"], cb, name=tag + "_conv")
    (yn, states), rode = _ssd_fwd(proj, act, dtb, alog, dsk, nw, name=tag + "_scan", ride=ride)
    y = _mm(yn, w["out_w"], name=tag + "_out")
    h = _ln_fwd(u, y, ln_g, ln_b, name=tag + "_ln")
    saved = dict(u=u, w_in=w_in, proj=proj, act=act, states=states, yn=yn, y=y, dtb=dtb, alog=alog, dsk=dsk, nw=nw, cb=cb)
    return h, saved, rode


def _ssd_layer_bwd(s, w, ln_g, dr_up, du_up, tag, ride=None, own=None):
    dr, dg, db = _ln_bwd(s["u"], s["y"], ln_g, dr_up, du_up, name=tag + "_ln_b")
    dyn = _mm(dr, w["out_w"], tb=True, name=tag + "_out_bx")
    d_out_w = _mm(s["yn"], dr, ta=True, out_dtype=WIRE, name=tag + "_out_bw")
    (dx, dz, dbm, dcm, ddt, ddtb, dalog, ddsk, dnw), rode = _ssd_bwd(
        s["proj"], s["act"], s["dtb"], s["alog"], s["dsk"], s["nw"], s["states"], dyn, name=tag + "_scan_b", ride=ride)
    dact = jnp.concatenate([dx, dbm, dcm], axis=1)
    dpre, d_conv_w, d_conv_b = _conv_bwd_pre(s["proj"], SSD_DI, w["conv_w"], s["cb"], dact, name=tag + "_conv_bp")
    dxbc = _conv_bwd_x(dpre, w["conv_w"], name=tag + "_conv_bx")
    dproj = jnp.concatenate([dz, dxbc, ddt], axis=1)
    grads = dict(conv_w=d_conv_w, conv_b=d_conv_b.reshape(-1), dt_bias=ddtb[0, :SSD_H], a_log=dalog[0, :SSD_H],
                 d=ddsk[0, :SSD_H], norm_w=dnw.reshape(-1), out_w=d_out_w, ln_g=dg[0], ln_b=db[0])
    late_names = ("conv_w", "conv_b", "norm_w", "out_w")
    got = {}
    if own is None:
        d_in_w = _mm(s["u"], dproj, ta=True, out_dtype=WIRE, name=tag + "_in_bw")
        du = _mm(dproj, s["w_in"], tb=True, name=tag + "_in_bx")
        grads["in_w"] = d_in_w[:, :SSD_DI + SSD_CONV_DIM + SSD_H]
    else:
        d_in_w, late = _mm(s["u"], dproj, ta=True, out_dtype=WIRE, name=tag + "_in_bw",
                           ride=([own(k, grads[k]) for k in late_names], True))
        got = dict(zip(late_names, late))
        grads["in_w"] = d_in_w[:, :SSD_DI + SSD_CONV_DIM + SSD_H]
        du, (got["in_w"],) = _mm(dproj, s["w_in"], tb=True, name=tag + "_in_bx", ride=([own("in_w", grads["in_w"])], True))
    return dr, du, grads, rode, got


MLA_LOW = MLA_QR + MLA_KVR + LANES
MLA_ZB = MLA_LOW // LANES


def _rope_mat():
    r = lax.broadcasted_iota(jnp.int32, (LANES, LANES), 0)
    c = lax.broadcasted_iota(jnp.int32, (LANES, LANES), 1)
    hf = MLA_ROPE // 2
    return jnp.where((c < hf) & (r == c + hf), -1.0, 0.0) + jnp.where((c >= hf) & (c < 2 * hf) & (r == c - hf), 1.0, 0.0)


def _rope(x, cosf, sinf):
    return x * cosf + _sel_r(x, _rope_mat()) * sinf


def _rope_adj(d, cosf, sinf):
    return d * cosf - _sel_r(d * sinf, _rope_mat())


def _mla_low_fn(low, qnw, kvnw, cosf, sinf):
    qc, kvc, kr = low[:, :MLA_QR], low[:, MLA_QR:MLA_QR + MLA_KVR], low[:, MLA_QR + MLA_KVR:]
    qn = qc * lax.rsqrt(jnp.mean(qc * qc, -1, keepdims=True) + RMS_EPS) * qnw
    kvn = kvc * lax.rsqrt(jnp.mean(kvc * kvc, -1, keepdims=True) + RMS_EPS) * kvnw
    return qn, kvn, _rope(kr, cosf, sinf)


def _mla_low_fwd(proj, qnw, kvnw, cosf, sinf, name):
    t = proj.shape[0]
    tr = _tile(t, 256)

    def body(low_ref, qnw_ref, kvnw_ref, cos_ref, sin_ref, qn_ref, kvn_ref, kr_ref):
        qn, kvn, kr = _mla_low_fn(low_ref[...], qnw_ref[...], kvnw_ref[...], cos_ref[...], sin_ref[...])
        qn_ref[...] = qn
        kvn_ref[...] = kvn
        kr_ref[...] = kr

    row = lambda wdt: pl.BlockSpec((tr, wdt), lambda i: (i, 0))
    par = lambda wdt: pl.BlockSpec((1, wdt), lambda i: (0, 0))
    sds = jax.ShapeDtypeStruct
    return pl.pallas_call(
        body, name=name, grid=(t // tr,),
        in_specs=[row(MLA_LOW), par(MLA_QR), par(MLA_KVR), row(LANES), row(LANES)],
        out_specs=[row(MLA_QR), row(MLA_KVR), row(LANES)],
        out_shape=[sds((t, MLA_QR), F32), sds((t, MLA_KVR), F32), sds((t, LANES), F32)],
        compiler_params=_cparams(("parallel",)),
    )(proj, qnw, kvnw, cosf, sinf)


def _mla_low_bwd(proj, qnw, kvnw, cosf, sinf, dqn, dkvn, dkr, name):
    t = proj.shape[0]
    tr = _tile(t, 256)

    def body(low_ref, qnw_ref, kvnw_ref, cos_ref, sin_ref, dqn_ref, dkvn_ref, dkr_ref, dlow_ref, dqnw_ref, dkvnw_ref):
        i = pl.program_id(0)

        @pl.when(i == 0)
        def _():
            dqnw_ref[...] = jnp.zeros_like(dqnw_ref)
            dkvnw_ref[...] = jnp.zeros_like(dkvnw_ref)

        cosf, sinf = cos_ref[...], sin_ref[...]
        _, vjp = jax.vjp(lambda a, b, c: _mla_low_fn(a, b, c, cosf, sinf), low_ref[...], qnw_ref[...], kvnw_ref[...])
        dlow, dq, dk = vjp((dqn_ref[...], dkvn_ref[...], dkr_ref[...]))
        dlow_ref[...] = dlow
        dqnw_ref[...] += dq
        dkvnw_ref[...] += dk

    row = lambda wdt: pl.BlockSpec((tr, wdt), lambda i: (i, 0))
    par = lambda wdt: pl.BlockSpec((1, wdt), lambda i: (0, 0))
    sds = jax.ShapeDtypeStruct
    return pl.pallas_call(
        body, name=name, grid=(t // tr,),
        in_specs=[row(MLA_LOW), par(MLA_QR), par(MLA_KVR), row(LANES), row(LANES), row(MLA_QR), row(MLA_KVR), row(LANES)],
        out_specs=[row(MLA_LOW), par(MLA_QR), par(MLA_KVR)],
        out_shape=[sds((t, MLA_LOW), F32), sds((1, MLA_QR), F32), sds((1, MLA_KVR), F32)],
        compiler_params=_cparams(("arbitrary",)),
    )(proj, qnw, kvnw, cosf, sinf, dqn, dkvn, dkr)


def _rope_heads(x, col_blk0, cosf, sinf, adjoint, name):
    t = x.shape[0]
    tr = _tile(t, 512)

    def body(x_ref, cos_ref, sin_ref, o_ref):
        f = _rope_adj if adjoint else _rope
        o_ref[...] = f(x_ref[...], cos_ref[...], sin_ref[...])

    tab = pl.BlockSpec((tr, LANES), lambda i, h: (i, 0))
    return pl.pallas_call(
        body, name=name, grid=(t // tr, MLA_H),
        in_specs=[pl.BlockSpec((tr, LANES), lambda i, h: (i, col_blk0 + h)), tab, tab],
        out_specs=pl.BlockSpec((tr, LANES), lambda i, h: (i, h)),
        out_shape=jax.ShapeDtypeStruct((t, MLA_H * LANES), F32), compiler_params=_cparams(("parallel", "parallel")),
    )(x, cosf, sinf)


ATT_HB = 4
ATT_W = ATT_HB * LANES


def _att_qk(qn_ref, qr_ref, kn_ref, kr_ref):
    q2 = jnp.concatenate([_heads_of(qn_ref, ATT_HB), _heads_of(qr_ref, ATT_HB)], axis=2)
    kr = kr_ref[...]
    k2 = jnp.concatenate([_heads_of(kn_ref, ATT_HB), jnp.broadcast_to(kr[None], (ATT_HB,) + kr.shape)], axis=2)
    return q2, k2


def _att_scores(q2, k2, masked):
    s = _bdg(_mx(q2 * MLA_SCALE), _mx(k2), 2, 2)
    if masked:
        tq, tk = s.shape[1:]
        s = jnp.where(lax.broadcasted_iota(jnp.int32, (tq, tk), 1) <= lax.broadcasted_iota(jnp.int32, (tq, tk), 0), s, -jnp.inf)
    return s


def _on_causal_blocks(q_blk, k_blk, step):
    @pl.when(k_blk < q_blk)
    def _():
        step(False)

    @pl.when(k_blk == q_blk)
    def _():
        step(True)


def _put_heads(ref, val):
    for b in range(val.shape[0]):
        ref[:, b * LANES:(b + 1) * LANES] = val[b].astype(ref.dtype)


def _att_ds(s, v_ref, o_ref, do_ref, lse_ref):
    do = _heads_of(do_ref, ATT_HB)
    p = jnp.exp(s - _heads_of(lse_ref, ATT_HB)[:, :, 0:1])
    dp = _bdg(_mx(do), _mx(_heads_of(v_ref, ATT_HB)), 2, 2)
    dl = jnp.sum(do * _heads_of(o_ref, ATT_HB), -1, keepdims=True)
    return _mx(p), _mx(p * (dp - dl) * MLA_SCALE), do


def _attn_fwd(q, qr, kv, kr, proj, name, ride=None):
    t = q.shape[0]
    tq = tk = _tile(t, ATT_BLK)
    nq = nk = t // tq

    def body(qn_ref, qr_ref, kn_ref, kr_ref, v_ref, *rest):
        z_refs, (o_ref, og_ref, lse_ref, m_s, l_s, acc_s) = rest[:ATT_HB], rest[ATT_HB:]
        i, j = pl.program_id(1), pl.program_id(2)

        @pl.when(j == 0)
        def _():
            m_s[...] = jnp.full_like(m_s, -jnp.inf)
            l_s[...] = jnp.zeros_like(l_s)
            acc_s[...] = jnp.zeros_like(acc_s)

        def step(masked):
            s = _att_scores(*_att_qk(qn_ref, qr_ref, kn_ref, kr_ref), masked)
            m_new = jnp.maximum(m_s[...], jnp.max(s, -1, keepdims=True))
            p = jnp.exp(s - m_new)
            corr = jnp.exp(m_s[...] - m_new)
            l_s[...] = corr * l_s[...] + jnp.sum(p, -1, keepdims=True)
            acc_s[...] = corr * acc_s[...] + _bdg(_mx(p), _mx(_heads_of(v_ref, ATT_HB)), 2, 1)
            m_s[...] = m_new

        _on_causal_blocks(i, j, step)

        @pl.when(j == nk - 1)
        def _():
            o = acc_s[...] / l_s[...]
            _put_heads(o_ref, o)
            lse = m_s[...] + jnp.log(l_s[...])
            for b in range(ATT_HB):
                og_ref[:, b * LANES:(b + 1) * LANES] = (o[b] * _silu(z_refs[b][...])).astype(og_ref.dtype)
                lse_ref[:, b * LANES:(b + 1) * LANES] = jnp.broadcast_to(lse[b], (tq, LANES))

    qs = lambda off: pl.BlockSpec((tq, ATT_W), lambda h, i, j: (i, off // ATT_HB + h))
    ks = lambda off: pl.BlockSpec((tk, ATT_W), lambda h, i, j: (jnp.minimum(j, i), off // ATT_HB + h))
    zs = [pl.BlockSpec((tq, LANES), functools.partial(lambda b, h, i, j: (i, MLA_ZB + h * ATT_HB + b), b)) for b in range(ATT_HB)]
    sds = jax.ShapeDtypeStruct
    return _hosted_call(
        body, name=name, grid=(MLA_H // ATT_HB, nq, nk),
        in_specs=[qs(0), qs(0), ks(0), pl.BlockSpec((tk, LANES), lambda h, i, j: (jnp.minimum(j, i), 0)), ks(MLA_H)] + zs,
        out_specs=[qs(0), qs(0), qs(0)],
        out_shape=[sds((t, MLA_GATE), F32), sds((t, MLA_GATE), F32), sds((t, MLA_H * LANES), F32)],
        scratch_shapes=[pltpu.VMEM((ATT_HB, tq, 1), F32), pltpu.VMEM((ATT_HB, tq, 1), F32), pltpu.VMEM((ATT_HB, tq, LANES), F32)],
        sem=("parallel", "parallel", "arbitrary"), args=(q, qr, kv, kr, kv) + (proj,) * ATT_HB, ride=ride)


def _gate_bwd(dog, o, proj, name):
    t = o.shape[0]
    tr = _tile(t, 512)

    def body(d_ref, o_ref, z_ref, do_ref, dz_ref):
        z = z_ref[...]
        sg = jax.nn.sigmoid(z)
        d = d_ref[...]
        do_ref[...] = d * z * sg
        dz_ref[...] = d * o_ref[...] * (sg * (1.0 + z * (1.0 - sg)))

    blk = lambda off: pl.BlockSpec((tr, 512), lambda i, j: (i, off + j))
    assert MLA_LOW % 512 != 0 or True
    zspec = pl.BlockSpec((tr, LANES), lambda i, j: (i, MLA_ZB + j))
    b128 = pl.BlockSpec((tr, LANES), lambda i, j: (i, j))
    sds = jax.ShapeDtypeStruct
    return pl.pallas_call(
        body, name=name, grid=(t // tr, MLA_GATE // LANES),
        in_specs=[b128, b128, zspec], out_specs=[b128, b128],
        out_shape=[sds((t, MLA_GATE), F32), sds((t, MLA_GATE), F32)],
        compiler_params=_cparams(("parallel", "parallel")),
    )(dog, o, proj)


def _attn_bwd_q(q, qr, kv, kr, o, do, lse, name, ride=None):
    t = q.shape[0]
    tq = tk = _tile(t, ATT_BLK)
    nq = nk = t // tq

    def body(qn_ref, qr_ref, kn_ref, kr_ref, v_ref, o_ref, do_ref, lse_ref, dqn_ref, dqr_ref, an_s, ar_s):
        i, j = pl.program_id(1), pl.program_id(2)

        @pl.when(j == 0)
        def _():
            an_s[...] = jnp.zeros_like(an_s)
            ar_s[...] = jnp.zeros_like(ar_s)

        def step(masked):
            q2, k2 = _att_qk(qn_ref, qr_ref, kn_ref, kr_ref)
            ds = _att_ds(_att_scores(q2, k2, masked), v_ref, o_ref, do_ref, lse_ref)[1]
            dq2 = _bdg(ds, _mx(k2), 2, 1)
            an_s[...] += dq2[:, :, :LANES]
            ar_s[...] += dq2[:, :, LANES:]

        _on_causal_blocks(i, j, step)

        @pl.when(j == nk - 1)
        def _():
            _put_heads(dqn_ref, an_s[...])
            _put_heads(dqr_ref, ar_s[...])

    qs = lambda off: pl.BlockSpec((tq, ATT_W), lambda h, i, j: (i, off // ATT_HB + h))
    ks = lambda off: pl.BlockSpec((tk, ATT_W), lambda h, i, j: (jnp.minimum(j, i), off // ATT_HB + h))
    sds = jax.ShapeDtypeStruct
    return _hosted_call(
        body, name=name, grid=(MLA_H // ATT_HB, nq, nk),
        in_specs=[qs(0), qs(0), ks(0), pl.BlockSpec((tk, LANES), lambda h, i, j: (jnp.minimum(j, i), 0)), ks(MLA_H),
                  qs(0), qs(0), qs(0)],
        out_specs=[qs(0), qs(0)],
        out_shape=[sds((t, MLA_H * LANES), F32), sds((t, MLA_H * LANES), F32)],
        scratch_shapes=[pltpu.VMEM((ATT_HB, tq, LANES), F32), pltpu.VMEM((ATT_HB, tq, LANES), F32)],
        sem=("parallel", "parallel", "arbitrary"), args=(q, qr, kv, kr, kv, o, do, lse), ride=ride)


def _attn_bwd_kv(q, qr, kv, kr, o, do, lse, name):
    t = q.shape[0]
    tq = tk = _tile(t, ATT_BLK)
    nq = nk = t // tq

    def body(qn_ref, qr_ref, kn_ref, kr_ref, v_ref, o_ref, do_ref, lse_ref, dkn_ref, dv_ref, dkr_ref, akn_s, av_s):
        j, h, i = pl.program_id(0), pl.program_id(1), pl.program_id(2)

        @pl.when((h == 0) & (i == 0))
        def _():
            dkr_ref[...] = jnp.zeros_like(dkr_ref)

        @pl.when(i == 0)
        def _():
            akn_s[...] = jnp.zeros_like(akn_s)
            av_s[...] = jnp.zeros_like(av_s)

        def step(masked):
            q2, k2 = _att_qk(qn_ref, qr_ref, kn_ref, kr_ref)
            p, ds, do = _att_ds(_att_scores(q2, k2, masked), v_ref, o_ref, do_ref, lse_ref)
            av_s[...] += _bdg(p, _mx(do), 1, 1)
            dk2 = _bdg(ds, _mx(q2), 1, 1)
            akn_s[...] += dk2[:, :, :LANES]
            dkr_ref[...] += jnp.sum(dk2[:, :, LANES:], axis=0)

        _on_causal_blocks(i, j, step)

        @pl.when(i == nq - 1)
        def _():
            _put_heads(dkn_ref, akn_s[...])
            _put_heads(dv_ref, av_s[...])

    qs = lambda off: pl.BlockSpec((tq, ATT_W), lambda j, h, i: (jnp.maximum(i, j), off // ATT_HB + h))
    ks = lambda off: pl.BlockSpec((tk, ATT_W), lambda j, h, i: (j, off // ATT_HB + h))
    sds = jax.ShapeDtypeStruct
    return pl.pallas_call(
        body, name=name, grid=(nk, MLA_H // ATT_HB, nq),
        in_specs=[qs(0), qs(0), ks(0), pl.BlockSpec((tk, LANES), lambda j, h, i: (j, 0)), ks(MLA_H), qs(0), qs(0), qs(0)],
        out_specs=[ks(0), ks(0), pl.BlockSpec((tk, LANES), lambda j, h, i: (j, 0))],
        out_shape=[sds((t, MLA_H * LANES), F32), sds((t, MLA_H * LANES), F32), sds((t, LANES), F32)],
        scratch_shapes=[pltpu.VMEM((ATT_HB, tk, LANES), F32), pltpu.VMEM((ATT_HB, tk, LANES), F32)],
        compiler_params=_cparams(("parallel", "arbitrary", "arbitrary")),
    )(q, qr, kv, kr, kv, o, do, lse)


def _rope_tables(positions):
    inv_freq = ROPE_THETA ** (-jnp.arange(0, MLA_ROPE, 2, dtype=F32) / MLA_ROPE)
    ang = positions.astype(F32)[:, None] * inv_freq
    pad = jnp.zeros((positions.shape[0], LANES - MLA_ROPE), F32)
    cos, sin = jnp.cos(ang), jnp.sin(ang)
    return jnp.concatenate([cos, cos, pad], 1), jnp.concatenate([sin, sin, pad], 1)


def _mla_weights(w):
    dt = w["in_w"].dtype
    iw = w["in_w"]
    c1 = MLA_QR + MLA_KVR + MLA_ROPE
    w_in = jnp.concatenate([iw[:, :c1], jnp.zeros((D_MODEL, LANES - MLA_ROPE), dt), iw[:, c1:]], axis=1)
    qu = w["q_up_w"].reshape(MLA_QR, MLA_H, MLA_NOPE + MLA_ROPE)
    qrope = jnp.concatenate([qu[:, :, MLA_NOPE:], jnp.zeros((MLA_QR, MLA_H, LANES - MLA_ROPE), dt)], axis=2)
    w_q = jnp.concatenate([qu[:, :, :MLA_NOPE].reshape(MLA_QR, -1), qrope.reshape(MLA_QR, -1)], axis=1)
    kvu = w["kv_up_w"].reshape(MLA_KVR, MLA_H, MLA_NOPE + MLA_V)
    w_kv = jnp.concatenate([kvu[:, :, :MLA_NOPE].reshape(MLA_KVR, -1), kvu[:, :, MLA_NOPE:].reshape(MLA_KVR, -1)], axis=1)
    return w_in, w_q, w_kv


def _mla_layer_fwd(u, w, ln_g, ln_b, cosf, sinf, tag, ride=None):
    w_in, w_q, w_kv = _mla_weights(w)
    qnw, kvnw = w["q_norm_w"].reshape(1, -1), w["kv_norm_w"].reshape(1, -1)
    proj = _mm(u, w_in, name=tag + "_in")
    qn, kvn, kr = _mla_low_fwd(proj, qnw, kvnw, cosf, sinf, name=tag + "_low")
    q = _mm(qn, w_q, name=tag + "_qup")
    kv = _mm(kvn, w_kv, name=tag + "_kvup")
    qr = _rope_heads(q, MLA_H, cosf, sinf, False, name=tag + "_qrope")
    (o, og, lse), rode = _attn_fwd(q, qr, kv, kr, proj, name=tag + "_attn", ride=ride)
    y = _mm(og, w["out_w"], name=tag + "_out")
    h = _ln_fwd(u, y, ln_g, ln_b, name=tag + "_ln")
    saved = dict(u=u, w_in=w_in, w_q=w_q, w_kv=w_kv, qnw=qnw, kvnw=kvnw, proj=proj, qn=qn, kvn=kvn, kr=kr, q=q, kv=kv,
                 qr=qr, o=o, og=og, lse=lse, y=y)
    return h, saved, rode


def _mla_layer_bwd(s, w, ln_g, cosf, sinf, dr_up, du_up, tag, ride=None):
    dr, dg, db = _ln_bwd(s["u"], s["y"], ln_g, dr_up, du_up, name=tag + "_ln_b")
    dog = _mm(dr, w["out_w"], tb=True, name=tag + "_out_bx")
    d_out_w = _mm(s["og"], dr, ta=True, out_dtype=WIRE, name=tag + "_out_bw")
    do, dz = _gate_bwd(dog, s["o"], s["proj"], name=tag + "_gate_b")
    (dqn_h, dqr_rot), rode = _attn_bwd_q(s["q"], s["qr"], s["kv"], s["kr"], s["o"], do, s["lse"], name=tag + "_attn_bq",
                                         ride=ride)
    dkn_h, dv_h, dkr_rot = _attn_bwd_kv(s["q"], s["qr"], s["kv"], s["kr"], s["o"], do, s["lse"], name=tag + "_attn_bkv")
    dqr = _rope_heads(dqr_rot, 0, cosf, sinf, True, name=tag + "_qrope_b")
    dq = jnp.concatenate([dqn_h, dqr], axis=1)
    dkv = jnp.concatenate([dkn_h, dv_h], axis=1)
    d_wq = _mm(s["qn"], dq, ta=True, out_dtype=WIRE, name=tag + "_qup_bw")
    dqn = _mm(dq, s["w_q"], tb=True, name=tag + "_qup_bx")
    d_wkv = _mm(s["kvn"], dkv, ta=True, out_dtype=WIRE, name=tag + "_kvup_bw")
    dkvn = _mm(dkv, s["w_kv"], tb=True, name=tag + "_kvup_bx")
    dlow, dqnw, dkvnw = _mla_low_bwd(s["proj"], s["qnw"], s["kvnw"], cosf, sinf, dqn, dkvn, dkr_rot, name=tag + "_low_b")
    dproj = jnp.concatenate([dlow, dz], axis=1)
    d_in = _mm(s["u"], dproj, ta=True, out_dtype=WIRE, name=tag + "_in_bw")
    du = _mm(dproj, s["w_in"], tb=True, name=tag + "_in_bx")
    c1 = MLA_QR + MLA_KVR + MLA_ROPE
    d_in_w = jnp.concatenate([d_in[:, :c1], d_in[:, MLA_LOW:]], axis=1)
    dq3n = d_wq[:, :MLA_H * MLA_NOPE].reshape(MLA_QR, MLA_H, MLA_NOPE)
    dq3r = d_wq[:, MLA_H * MLA_NOPE:].reshape(MLA_QR, MLA_H, LANES)[:, :, :MLA_ROPE]
    d_q_up = jnp.concatenate([dq3n, dq3r], axis=2).reshape(MLA_QR, -1)
    dkv3 = d_wkv.reshape(MLA_KVR, 2, MLA_H, MLA_NOPE)
    d_kv_up = jnp.concatenate([dkv3[:, 0], dkv3[:, 1]], axis=2).reshape(MLA_KVR, -1)
    grads = dict(in_w=d_in_w, q_norm_w=dqnw[0], q_up_w=d_q_up, kv_norm_w=dkvnw[0], kv_up_w=d_kv_up, out_w=d_out_w,
                 ln_g=dg[0], ln_b=db[0])
    return dr, du, grads, rode


GDN_REP = GDN_HV // GDN_HK
GDN_A_LANE = GDN_HV
GDN_HPB = 4
GDN_VPB = GDN_HPB * GDN_REP


def _bdg(a, b, ca, cb):
    return lax.dot_general(a, b, (((ca,), (cb,)), ((0,), (0,))), preferred_element_type=F32)


@jax.custom_vjp
def _bnn(a, b):
    return _bdg(_mx(a), _mx(b), 2, 1)


def _bnn_f(a, b):
    return _bnn(a, b), (a, b)


def _bnn_b(res, ct):
    a, b = res
    return _bdg(_mx(ct), _mx(b), 2, 2), _bdg(_mx(a), _mx(ct), 1, 1)


_bnn.defvjp(_bnn_f, _bnn_b)


@jax.custom_vjp
def _bnt(a, b):
    return _bdg(_mx(a), _mx(b), 2, 2)


def _bnt_f(a, b):
    return _bnt(a, b), (a, b)


def _bnt_b(res, ct):
    a, b = res
    return _bdg(_mx(ct), _mx(b), 2, 1), _bdg(_mx(ct), _mx(a), 1, 1)


_bnt.defvjp(_bnt_f, _bnt_b)


@jax.custom_vjp
def _btn(a, b):
    return _bdg(_mx(a), _mx(b), 1, 1)


def _btn_f(a, b):
    return _btn(a, b), (a, b)


def _btn_b(res, ct):
    a, b = res
    return _bdg(_mx(b), _mx(ct), 2, 2), _bdg(_mx(a), _mx(ct), 2, 1)


_btn.defvjp(_btn_f, _btn_b)


def _h3(a, b, ca=2, cb=1):
    ah, bh = _mx(a), _mx(b)
    al, bl = _mx(a - ah.astype(F32)), _mx(b - bh.astype(F32))
    return _bdg(ah, bh, ca, cb) + (_bdg(ah, bl, ca, cb) + _bdg(al, bh, ca, cb))


@jax.custom_vjp
def _neumann_inverse(x):
    L = x.shape[-1]
    eye = (lax.broadcasted_iota(jnp.int32, (L, L), 0) == lax.broadcasted_iota(jnp.int32, (L, L), 1)).astype(F32)
    inv = eye + x
    xp = x
    for _ in range(L.bit_length() - 2):
        xp = _h3(xp, xp)
        inv = inv + _h3(inv, xp)
    return inv


def _neumann_f(x):
    inv = _neumann_inverse(x)
    return inv, inv


def _neumann_b(inv, ct):
    return (_h3(_h3(inv, ct, 1, 1), inv, 2, 2),)


_neumann_inverse.defvjp(_neumann_f, _neumann_b)


def _cat0(parts):
    return jnp.concatenate([p[None] for p in parts], axis=0)


def _gdn_chunk(hb, q, k, v, z, ba, alog, dtb, nw, s):
    L = q.shape[1]
    r_i = lax.broadcasted_iota(jnp.int32, (L, L), 0)
    c_i = lax.broadcasted_iota(jnp.int32, (L, L), 1)
    incl, strict = r_i >= c_i, r_i > c_i
    rep = lambda t: jnp.broadcast_to(t[:, None], (GDN_HPB, GDN_REP) + t.shape[1:]).reshape((GDN_VPB,) + t.shape[1:])
    qn = rep(q * lax.rsqrt(jnp.sum(q * q, -1, keepdims=True) + RMS_EPS) * (GDN_DK ** -0.5))
    kn = rep(k * lax.rsqrt(jnp.sum(k * k, -1, keepdims=True) + RMS_EPS))
    beta_all = jax.nn.sigmoid(ba)
    g_all = -jnp.exp(alog) * _softplus(ba + dtb)
    gcs_all = _sel_l(incl.astype(F32), g_all)
    lane = lax.broadcasted_iota(jnp.int32, (L, LANES), 1)
    pick = lambda mat, idx: jnp.sum(jnp.where(lane == idx, mat, 0.0), axis=1, keepdims=True)
    beta = _cat0([pick(beta_all, GDN_VPB * hb + b) for b in range(GDN_VPB)])
    gc = _cat0([pick(gcs_all, GDN_A_LANE + GDN_VPB * hb + b) for b in range(GDN_VPB)])
    gm = jnp.broadcast_to(gc, (GDN_VPB, L, L))
    decay = jnp.exp(jnp.where(incl, gm - jnp.swapaxes(gm, 1, 2), -jnp.inf))
    kb = kn * beta
    eg = jnp.exp(gc)
    inv = _neumann_inverse(-jnp.where(strict, _bnt(kb, kn) * decay, 0.0))
    uw = _bnn(inv, jnp.concatenate([v * beta, kb * eg], axis=2))
    uu, ww = uw[:, :, :GDN_DV], uw[:, :, GDN_DV:]
    qk = jnp.where(incl, _bnt(qn, kn) * decay, 0.0)
    last = lax.broadcasted_iota(jnp.int32, (L, 1), 0) == L - 1
    glast = jnp.sum(jnp.where(last, gc, 0.0), axis=1, keepdims=True)
    kdec = kn * jnp.exp(glast - gc)
    vnew = uu - _bnn(ww, s)
    o = _bnn(qn * eg, s) + _bnn(qk, vnew)
    new = s * jnp.exp(glast) + _btn(kdec, vnew)
    on = o * lax.rsqrt(jnp.mean(o * o, -1, keepdims=True) + RMS_EPS) * nw * _silu(z)
    return on, new


def _heads_of(ref, n):
    return _cat0([ref[:, i * LANES:(i + 1) * LANES] for i in range(n)])


def _gdn_specs(nc, rev):
    cc = (lambda c: nc - 1 - c) if rev else (lambda c: c)
    wq, wv = GDN_HPB * GDN_DK, GDN_VPB * GDN_DV
    par = pl.BlockSpec((1, LANES), lambda c, h: (0, 0))
    return dict(
        q=pl.BlockSpec((GDN_L, wq), lambda c, h: (cc(c), h)),
        k=pl.BlockSpec((GDN_L, wq), lambda c, h: (cc(c), GDN_KEY // wq + h)),
        v=pl.BlockSpec((GDN_L, wv), lambda c, h: (cc(c), 2 * GDN_KEY // wv + h)),
        z=pl.BlockSpec((GDN_L, wv), lambda c, h: (cc(c), GDN_CONV_DIM // wv + h)),
        ba=pl.BlockSpec((GDN_L, LANES), lambda c, h: (cc(c), (GDN_CONV_DIM + GDN_VAL) // LANES)),
        par=par,
        st=pl.BlockSpec((1, GDN_VPB, GDN_DK, GDN_DV), lambda c, h: (cc(c), h, 0, 0)),
        o=pl.BlockSpec((GDN_L, wv), lambda c, h: (cc(c), h)),
        qk_out=pl.BlockSpec((GDN_L, wq), lambda c, h: (cc(c), h)),
        ba_out=pl.BlockSpec((GDN_L, LANES), lambda c, h: (cc(c), 0)),
    )


def _gdn_fwd(proj, act, alog, dtb, nw, name, ride=None):
    t = proj.shape[0]
    nc = t // GDN_L
    sp = _gdn_specs(nc, False)

    def body(q_ref, k_ref, v_ref, z_ref, ba_ref, alog_ref, dtb_ref, nw_ref, o_ref, st_ref, state):
        c, h = pl.program_id(0), pl.program_id(1)

        mine = pl.ds(h * GDN_VPB, GDN_VPB)

        @pl.when(c == 0)
        def _():
            state[mine] = jnp.zeros((GDN_VPB, GDN_DK, GDN_DV), F32)

        prev = state[mine]
        st_ref[0] = prev
        on, new = _gdn_chunk(h, _heads_of(q_ref, GDN_HPB), _heads_of(k_ref, GDN_HPB), _heads_of(v_ref, GDN_VPB),
                             _heads_of(z_ref, GDN_VPB), ba_ref[...], alog_ref[...], dtb_ref[...], nw_ref[...], prev)
        for b in range(GDN_VPB):
            o_ref[:, b * LANES:(b + 1) * LANES] = on[b].astype(o_ref.dtype)
        state[mine] = new

    sds = jax.ShapeDtypeStruct
    return _hosted_call(
        body, name=name, grid=(nc, GDN_HK // GDN_HPB),
        in_specs=[sp["q"], sp["k"], sp["v"], sp["z"], sp["ba"], sp["par"], sp["par"], sp["par"]],
        out_specs=[sp["o"], sp["st"]],
        out_shape=[sds((t, GDN_VAL), F32), sds((nc, GDN_HV, GDN_DK, GDN_DV), F32)],
        scratch_shapes=[pltpu.VMEM((GDN_HV, GDN_DK, GDN_DV), F32)],
        sem=("arbitrary", "arbitrary"), args=(act, act, act, proj, proj, alog, dtb, nw), ride=ride)


def _gdn_bwd(proj, act, alog, dtb, nw, states, don, name, ride=None):
    t = proj.shape[0]
    nc = t // GDN_L
    sp = _gdn_specs(nc, True)

    def body(q_ref, k_ref, v_ref, z_ref, ba_ref, alog_ref, dtb_ref, nw_ref, st_ref, do_ref,
             dq_ref, dk_ref, dv_ref, dz_ref, dba_ref, dalog_ref, ddtb_ref, dnw_ref, dstate):
        c, h = pl.program_id(0), pl.program_id(1)

        mine = pl.ds(h * GDN_VPB, GDN_VPB)

        @pl.when(c == 0)
        def _():
            dstate[mine] = jnp.zeros((GDN_VPB, GDN_DK, GDN_DV), F32)

        @pl.when((c == 0) & (h == 0))
        def _():
            dalog_ref[...] = jnp.zeros_like(dalog_ref)
            ddtb_ref[...] = jnp.zeros_like(ddtb_ref)
            dnw_ref[...] = jnp.zeros_like(dnw_ref)

        @pl.when(h == 0)
        def _():
            dba_ref[...] = jnp.zeros_like(dba_ref)

        _, vjp = jax.vjp(functools.partial(_gdn_chunk, h), _heads_of(q_ref, GDN_HPB), _heads_of(k_ref, GDN_HPB),
                         _heads_of(v_ref, GDN_VPB), _heads_of(z_ref, GDN_VPB), ba_ref[...], alog_ref[...], dtb_ref[...],
                         nw_ref[...], st_ref[0])
        dq, dk, dv, dz, dba, dalog, ddtb, dnw, dprev = vjp((_heads_of(do_ref, GDN_VPB), dstate[mine]))
        for i in range(GDN_HPB):
            dq_ref[:, i * LANES:(i + 1) * LANES] = dq[i]
            dk_ref[:, i * LANES:(i + 1) * LANES] = dk[i]
        for b in range(GDN_VPB):
            dv_ref[:, b * LANES:(b + 1) * LANES] = dv[b]
            dz_ref[:, b * LANES:(b + 1) * LANES] = dz[b]
        dba_ref[...] += dba
        dalog_ref[...] += dalog
        ddtb_ref[...] += ddtb
        dnw_ref[...] += dnw
        dstate[mine] = dprev

    sds = jax.ShapeDtypeStruct
    par_out = pl.BlockSpec((1, LANES), lambda c, h: (0, 0))
    return _hosted_call(
        body, name=name, grid=(nc, GDN_HK // GDN_HPB),
        in_specs=[sp["q"], sp["k"], sp["v"], sp["z"], sp["ba"], sp["par"], sp["par"], sp["par"], sp["st"], sp["o"]],
        out_specs=[sp["qk_out"], sp["qk_out"], sp["o"], sp["o"], sp["ba_out"], par_out, par_out, par_out],
        out_shape=[sds((t, GDN_KEY), F32), sds((t, GDN_KEY), F32), sds((t, GDN_VAL), F32), sds((t, GDN_VAL), F32),
                   sds((t, LANES), F32), sds((1, LANES), F32), sds((1, LANES), F32), sds((1, LANES), F32)],
        scratch_shapes=[pltpu.VMEM((GDN_HV, GDN_DK, GDN_DV), F32)],
        sem=("arbitrary", "arbitrary"), args=(act, act, act, proj, proj, alog, dtb, nw, states, don), ride=ride)


GDN_PROJ = GDN_CONV_DIM + GDN_VAL + 2 * GDN_HV


def _gdn_layer_fwd(u, w, ln_g, ln_b, tag, ride=None):
    w_in = jnp.concatenate([w["in_w"], jnp.zeros((D_MODEL, GDN_PROJ_PAD - GDN_PROJ), w["in_w"].dtype)], axis=1)
    alog = _pad_lanes(w["a_log"], offset=GDN_A_LANE)
    dtb = _pad_lanes(w["dt_bias"], offset=GDN_A_LANE)
    nw = w["norm_w"].reshape(1, GDN_DV)
    zb = jnp.zeros((1, GDN_CONV_DIM), F32)
    proj = _mm(u, w_in, name=tag + "_in")
    act = _conv_fwd(proj, 0, w["conv_w"], zb, name=tag + "_conv")
    (on, states), rode = _gdn_fwd(proj, act, alog, dtb, nw, name=tag + "_delta", ride=ride)
    y = _mm(on, w["out_w"], name=tag + "_out")
    h = _ln_fwd(u, y, ln_g, ln_b, name=tag + "_ln")
    saved = dict(u=u, w_in=w_in, proj=proj, act=act, states=states, on=on, y=y, alog=alog, dtb=dtb, nw=nw, zb=zb)
    return h, saved, rode


def _gdn_layer_bwd(s, w, ln_g, dr_up, du_up, tag, ride=None):
    t = s["u"].shape[0]
    dr, dg, db = _ln_bwd(s["u"], s["y"], ln_g, dr_up, du_up, name=tag + "_ln_b")
    don = _mm(dr, w["out_w"], tb=True, name=tag + "_out_bx")
    d_out_w = _mm(s["on"], dr, ta=True, out_dtype=WIRE, name=tag + "_out_bw")
    (dq, dk, dv, dz, dba, dalog, ddtb, dnw), rode = _gdn_bwd(s["proj"], s["act"], s["alog"], s["dtb"], s["nw"], s["states"],
                                                            don, name=tag + "_delta_b", ride=ride)
    dact = jnp.concatenate([dq, dk, dv], axis=1)
    dpre, d_conv_w, _ = _conv_bwd_pre(s["proj"], 0, w["conv_w"], s["zb"], dact, name=tag + "_conv_bp")
    dqkv = _conv_bwd_x(dpre, w["conv_w"], name=tag + "_conv_bx")
    dproj = jnp.concatenate([dqkv, dz, dba, jnp.zeros((t, GDN_PROJ_PAD - GDN_PROJ - (LANES - 2 * GDN_HV)), F32)], axis=1)
    d_in = _mm(s["u"], dproj, ta=True, out_dtype=WIRE, name=tag + "_in_bw")
    du = _mm(dproj, s["w_in"], tb=True, name=tag + "_in_bx")
    grads = dict(in_w=d_in[:, :GDN_PROJ], conv_w=d_conv_w, a_log=dalog[0, GDN_A_LANE:GDN_A_LANE + GDN_HV],
                 dt_bias=ddtb[0, GDN_A_LANE:GDN_A_LANE + GDN_HV], norm_w=dnw[0], out_w=d_out_w, ln_g=dg[0], ln_b=db[0])
    return dr, du, grads, rode


def _mesh_pos():
    return lax.axis_index("x"), lax.axis_index("y"), lax.axis_index("c")


def _peer(k, x, y, c):
    return ((1 - x) if k & 4 else x, (1 - y) if k & 2 else y, (1 - c) if k & 1 else c)


def _ride_copies(ins, outs, send, recv, loc, scatter, with_arrivals):
    n = len(ins)
    x, y, c = _mesh_pos()
    me = 4 * x + 2 * y + c
    local = [pltpu.make_async_copy(ins[i].at[me] if scatter else ins[i], outs[i].at[me], loc.at[i]) for i in range(n)]
    sends, arrivals = [], []
    for k in range(1, N_DEV):
        peer = _peer(k, x, y, c)
        pidx = 4 * peer[0] + 2 * peer[1] + peer[2]
        for i in range(n):
            src = ins[i].at[pidx] if scatter else ins[i]
            sems = dict(send_sem=send.at[i, k - 1], recv_sem=recv.at[i, k - 1], device_id=peer,
                        device_id_type=pl.DeviceIdType.MESH)
            sends.append(pltpu.make_async_remote_copy(src_ref=src, dst_ref=outs[i].at[me], **sems))
            if with_arrivals:
                arrivals.append(pltpu.make_async_remote_copy(src_ref=src, dst_ref=outs[i].at[pidx], **sems))
    return local, sends, arrivals


def _ride_start(ins, outs, send, recv, loc, scatter):
    local, sends, _ = _ride_copies(ins, outs, send, recv, loc, scatter, False)
    for cp in local + sends:
        cp.start()


def _ride_wait(ins, outs, send, recv, loc, scatter):
    local, sends, arrivals = _ride_copies(ins, outs, send, recv, loc, scatter, True)
    for cp in arrivals:
        cp.wait_recv()
    for cp in sends:
        cp.wait_send()
    for cp in local:
        cp.wait()


def _ride_shapes(arrs, scatter):
    n = len(arrs)
    out_shape = [jax.ShapeDtypeStruct(a.shape if scatter else (N_DEV,) + a.shape, a.dtype) for a in arrs]
    scratch = [pltpu.SemaphoreType.DMA((n, N_DEV - 1)), pltpu.SemaphoreType.DMA((n, N_DEV - 1)), pltpu.SemaphoreType.DMA((n,))]
    return out_shape, scratch


def _exchange(arrs, scatter, name):
    n = len(arrs)
    hbm = pl.BlockSpec(memory_space=pltpu.HBM)

    def body(*refs):
        ins, outs = refs[:n], refs[n:2 * n]
        _ride_start(ins, outs, *refs[2 * n:], scatter)
        _ride_wait(ins, outs, *refs[2 * n:], scatter)

    out_shape, scratch = _ride_shapes(arrs, scatter)
    return pl.pallas_call(
        body, name=name, in_specs=[hbm] * n, out_specs=[hbm] * n, out_shape=out_shape, scratch_shapes=scratch,
        compiler_params=pltpu.CompilerParams(has_side_effects=True),
    )(*arrs)


def _hosted_call(body, *, name, grid, in_specs, out_specs, out_shape, scratch_shapes, sem, args, ride=None):
    if ride is None:
        return pl.pallas_call(body, name=name, grid=grid, in_specs=in_specs, out_specs=out_specs, out_shape=out_shape,
                              scratch_shapes=scratch_shapes, compiler_params=_cparams(sem))(*args), []
    arrs, scatter = ride
    n, ni, no, ns = len(arrs), len(in_specs), len(out_specs), len(scratch_shapes)
    hbm = pl.BlockSpec(memory_space=pltpu.HBM)
    r_shape, r_scratch = _ride_shapes(arrs, scatter)

    def full(*refs):
        a, ri = refs[:ni], refs[ni:ni + n]
        o, ro = refs[ni + n:ni + n + no], refs[ni + n + no:ni + 2 * n + no]
        s, rs = refs[ni + 2 * n + no:ni + 2 * n + no + ns], refs[ni + 2 * n + no + ns:]
        ids = [pl.program_id(d) for d in range(len(grid))]
        first, last = ids[0] == 0, ids[0] == grid[0] - 1
        for d in range(1, len(grid)):
            first, last = first & (ids[d] == 0), last & (ids[d] == grid[d] - 1)

        @pl.when(first)
        def _():
            _ride_start(ri, ro, *rs, scatter)

        body(*a, *o, *s)

        @pl.when(last)
        def _():
            _ride_wait(ri, ro, *rs, scatter)

    outs = pl.pallas_call(
        full, name=name, grid=grid, in_specs=list(in_specs) + [hbm] * n, out_specs=list(out_specs) + [hbm] * n,
        out_shape=list(out_shape) + r_shape, scratch_shapes=list(scratch_shapes) + r_scratch,
        compiler_params=pltpu.CompilerParams(dimension_semantics=("arbitrary",) * len(grid), vmem_limit_bytes=VMEM_LIMIT,
                                             has_side_effects=True),
    )(*args, *arrs)
    return outs[:no], list(outs[no:])


def _unshard(g, ax):
    g = jnp.moveaxis(g, 0, ax)
    sh = g.shape
    return g.reshape(sh[:ax] + (sh[ax] * sh[ax + 1],) + sh[ax + 2:])


def _to_parts(full, ax):
    sh = full.shape
    full = full.reshape(sh[:ax] + (N_DEV, sh[ax] // N_DEV) + sh[ax + 1:])
    return jnp.moveaxis(full, ax, 0)


def _row_tile(r, c):
    cap = max(8, (256 * 1024) // max(c, 1))
    best = None
    for d in range(8, min(r, cap) + 1, 8):
        if r % d == 0:
            best = d
    return r if best is None else best


def _adamw(w, m, v, parts, name):
    r, c = w.shape
    tr = _row_tile(r, c)

    def body(w_ref, m_ref, v_ref, p_ref, g_ref, d_ref, nm_ref, nv_ref):
        g = p_ref[0].astype(F32)
        for q in range(1, N_DEV):
            g = g + p_ref[q].astype(F32)
        nm = ADAM_B1 * m_ref[...] + (1.0 - ADAM_B1) * g
        nv = ADAM_B2 * v_ref[...] + (1.0 - ADAM_B2) * (g * g)
        m_hat = nm / (1.0 - ADAM_B1 ** ADAM_STEP)
        v_hat = nv / (1.0 - ADAM_B2 ** ADAM_STEP)
        g_ref[...] = g
        d_ref[...] = -ADAM_LR * (m_hat / (jnp.sqrt(v_hat) + ADAM_EPS) + ADAM_WD * w_ref[...])
        nm_ref[...] = nm
        nv_ref[...] = nv

    row = pl.BlockSpec((tr, c), lambda i: (i, 0))
    out = jax.ShapeDtypeStruct((r, c), F32)
    return pl.pallas_call(
        body, name=name, grid=(r // tr,),
        in_specs=[row, row, row, pl.BlockSpec((N_DEV, tr, c), lambda i: (0, i, 0))],
        out_specs=[row] * 4, out_shape=[out] * 4, compiler_params=_cparams(("parallel",)),
    )(w, m, v, parts)


WEIGHTS = ['ssd_in_w', 'ssd_conv_w', 'ssd_conv_b', 'ssd_dt_bias', 'ssd_a_log', 'ssd_d', 'ssd_norm_w', 'ssd_out_w',
           'mla_in_w', 'mla_q_norm_w', 'mla_q_up_w', 'mla_kv_norm_w', 'mla_kv_up_w', 'mla_out_w', 'gdn_in_w',
           'gdn_conv_w', 'gdn_a_log', 'gdn_dt_bias', 'gdn_norm_w', 'gdn_out_w', 'ln_g', 'ln_b']
SHARDED = {'ssd_in_w': (1, True), 'ssd_conv_w': (1, False), 'ssd_conv_b': (0, False), 'ssd_norm_w': (0, False),
           'ssd_out_w': (0, True), 'mla_in_w': (1, True), 'mla_q_up_w': (1, True), 'mla_kv_up_w': (1, True),
           'mla_out_w': (0, True), 'gdn_in_w': (1, True), 'gdn_conv_w': (1, False), 'gdn_out_w': (0, True)}
REPLICATED = [n for n in WEIGHTS if n not in SHARDED]


def _pack_small(vals):
    flat = jnp.concatenate([vals[n].reshape(-1).astype(F32) for n in REPLICATED])
    rows = -(-flat.shape[0] // (8 * LANES)) * 8
    return jnp.pad(flat, (0, rows * LANES - flat.shape[0])).reshape(rows, LANES)


def _unpack_small(slab, like):
    flat = slab.reshape(-1)
    out, off = {}, 0
    for n in REPLICATED:
        sz = like[n].size
        out[n] = flat[off:off + sz].reshape(like[n].shape)
        off += sz
    return out


def kernel(x, positions, ssd_in_w, ssd_conv_w, ssd_conv_b, ssd_dt_bias, ssd_a_log, ssd_d, ssd_norm_w, ssd_out_w, mla_in_w, mla_q_norm_w, mla_q_up_w, mla_kv_norm_w, mla_kv_up_w, mla_out_w, gdn_in_w, gdn_conv_w, gdn_a_log, gdn_dt_bias, gdn_norm_w, gdn_out_w, ln_g, ln_b, loss_target, m_ssd_in_w, m_ssd_conv_w, m_ssd_conv_b, m_ssd_dt_bias, m_ssd_a_log, m_ssd_d, m_ssd_norm_w, m_ssd_out_w, m_mla_in_w, m_mla_q_norm_w, m_mla_q_up_w, m_mla_kv_norm_w, m_mla_kv_up_w, m_mla_out_w, m_gdn_in_w, m_gdn_conv_w, m_gdn_a_log, m_gdn_dt_bias, m_gdn_norm_w, m_gdn_out_w, m_ln_g, m_ln_b, v_ssd_in_w, v_ssd_conv_w, v_ssd_conv_b, v_ssd_dt_bias, v_ssd_a_log, v_ssd_d, v_ssd_norm_w, v_ssd_out_w, v_mla_in_w, v_mla_q_norm_w, v_mla_q_up_w, v_mla_kv_norm_w, v_mla_kv_up_w, v_mla_out_w, v_gdn_in_w, v_gdn_conv_w, v_gdn_a_log, v_gdn_dt_bias, v_gdn_norm_w, v_gdn_out_w, v_ln_g, v_ln_b):
    loc = locals()
    w = {n: loc[n] for n in WEIGHTS}
    m = {n: loc["m_" + n] for n in WEIGHTS}
    v = {n: loc["v_" + n] for n in WEIGHTS}
    xs, pos, tgt = x[0], positions[0], loss_target[0]

    def names_of(prefix):
        return [n for n in SHARDED if n.startswith(prefix + "_")]

    def shards(prefix, j):
        return [w[n][j].astype(WIRE) if SHARDED[n][1] else w[n][j] for n in names_of(prefix)]

    def assemble(prefix, j, gathered):
        lw = {n[len(prefix) + 1:]: _unshard(g, SHARDED[n][0]) for n, g in zip(names_of(prefix), gathered) if g is not None}
        lw.update({n[len(prefix) + 1:]: w[n][j] for n in REPLICATED if n.startswith(prefix + "_")})
        return lw

    def parts(prefix, g):
        return [_to_parts(g[n[len(prefix) + 1:]], SHARDED[n][0]).astype(WIRE if SHARDED[n][1] else F32)
                for n in names_of(prefix)]

    lg = lambda i: w["ln_g"][i].reshape(1, D_MODEL)
    lb = lambda i: w["ln_b"][i].reshape(1, D_MODEL)
    cosf, sinf = _rope_tables(pos)

    sh0 = shards("ssd", 0)
    w_s0 = assemble("ssd", 0, list(_exchange(sh0[:1], False, name="gather_ssd0")) + [None] * (len(sh0) - 1))

    def late0(got):
        return {n[4:]: _unshard(g, SHARDED[n][0]) for n, g in zip(names_of("ssd")[1:], got)}

    h1, s0, got = _ssd_layer_fwd(xs, w_s0, lg(0), lb(0), "ssd0", ride=(shards("mla", 0), False), late=(sh0[1:], late0))
    w_m0 = assemble("mla", 0, got)
    h2, s1, got = _mla_layer_fwd(h1, w_m0, lg(1), lb(1), cosf, sinf, "mla0", ride=(shards("gdn", 0), False))
    w_g0 = assemble("gdn", 0, got)
    h3, s2, got = _gdn_layer_fwd(h2, w_g0, lg(2), lb(2), "gdn0", ride=(shards("ssd", 1), False))
    w_s1 = assemble("ssd", 1, got)
    h4, s3, _ = _ssd_layer_fwd(h3, w_s1, lg(3), lb(3), "ssd1")
    loss_tile, dl = _loss_head(h4, tgt, name="loss_head")

    def own0(k, g):
        n = "ssd_" + k
        return _to_parts(g, SHARDED[n][0]).astype(WIRE if SHARDED[n][1] else F32)

    dr3, du3, g3, _, _ = _ssd_layer_bwd(s3, w_s1, lg(3), jnp.zeros_like(dl), dl, "ssd1")
    dr2, du2, g2, r3 = _gdn_layer_bwd(s2, w_g0, lg(2), dr3, du3, "gdn0", ride=(parts("ssd", g3), True))
    dr1, du1, g1, r2 = _mla_layer_bwd(s1, w_m0, lg(1), cosf, sinf, dr2, du2, "mla0", ride=(parts("gdn", g2), True))
    dr0, du0, g0, r1, r0 = _ssd_layer_bwd(s0, w_s0, lg(0), dr1, du1, "ssd0", ride=(parts("mla", g1), True), own=own0)
    grad_x = _axpy(dr0, du0, name="grad_x")[None]

    gsmall = {"ssd_" + k: jnp.stack([g0[k], g3[k]]) for k in ("dt_bias", "a_log", "d")}
    gsmall.update({"mla_" + k: g1[k][None] for k in ("q_norm_w", "kv_norm_w")})
    gsmall.update({"gdn_" + k: g2[k][None] for k in ("a_log", "dt_bias", "norm_w")})
    gsmall["ln_g"] = jnp.stack([g0["ln_g"], g1["ln_g"], g2["ln_g"], g3["ln_g"]])
    gsmall["ln_b"] = jnp.stack([g0["ln_b"], g1["ln_b"], g2["ln_b"], g3["ln_b"]])
    small = _pack_small(gsmall)
    rsmall, = _exchange([jnp.broadcast_to(small[None], (N_DEV,) + small.shape)], True, name="gather_small_grads")

    recvd = {n: jnp.stack([r0[n[4:]], b], axis=1) for n, b in zip(names_of("ssd"), r3)}
    recvd.update({n: a[:, None] for n, a in zip(names_of("mla"), r1)})
    recvd.update({n: a[:, None] for n, a in zip(names_of("gdn"), r2)})
    recvd = [recvd[n] for n in SHARDED] + [rsmall]

    grads, deltas, new_m, new_v = {}, {}, {}, {}
    for n, pt in zip(SHARDED, recvd[:-1]):
        shp = w[n].shape
        r2d = (-1, shp[-1])
        outs = _adamw(w[n].reshape(r2d), m[n].reshape(r2d), v[n].reshape(r2d), pt.reshape((N_DEV,) + w[n].reshape(r2d).shape),
                      name="adamw_" + n)
        grads[n], deltas[n], new_m[n], new_v[n] = (o.reshape(shp) for o in outs)
    outs = _adamw(_pack_small(w), _pack_small(m), _pack_small(v), recvd[-1], name="adamw_replicated")
    for dst, o in zip((grads, deltas, new_m, new_v), outs):
        dst.update(_unpack_small(o, w))

    loss = lax.psum(loss_tile[0, 0], ("x", "y", "c"))
    return (loss, grad_x, *[grads[n] for n in WEIGHTS], *[deltas[n] for n in WEIGHTS],
            *[new_m[n] for n in WEIGHTS], *[new_v[n] for n in WEIGHTS])
```

```python
import functools

import jax
import jax.numpy as jnp
from jax import lax
from jax.experimental import pallas as pl
from jax.experimental.pallas import tpu as pltpu

F32 = jnp.float32
MXU = jnp.bfloat16
WIRE = jnp.bfloat16
HI = lax.Precision.HIGHEST

N_DEV = 8
LANES = 128
VMEM_LIMIT = 56 * 1024 * 1024

D_MODEL = 2048
DEPTH = 4
ALPHA = (2.0 * DEPTH) ** 0.25
LN_EPS = 1e-5
RMS_EPS = 1e-6

SSD_DI = 4096
SSD_P = 64
SSD_H = 64
SSD_G = 8
SSD_N = 128
SSD_L = 128
SSD_GS = SSD_DI // SSD_G
SSD_CONV_DIM = SSD_DI + 2 * SSD_G * SSD_N
SSD_PROJ_PAD = SSD_DI + SSD_CONV_DIM + LANES

MLA_H = 16
MLA_QR = 768
MLA_KVR = 512
MLA_NOPE = 128
MLA_ROPE = 64
MLA_V = 128
MLA_GATE = MLA_H * MLA_V
MLA_PROJ_PAD = MLA_QR + MLA_KVR + LANES + MLA_GATE
MLA_SCALE = (MLA_NOPE + MLA_ROPE) ** -0.5
ROPE_THETA = 10000.0
ATT_BLK = 512

GDN_HK = 16
GDN_HV = 32
GDN_DK = 128
GDN_DV = 128
GDN_KEY = GDN_HK * GDN_DK
GDN_VAL = GDN_HV * GDN_DV
GDN_L = 128
GDN_CONV_DIM = 2 * GDN_KEY + GDN_VAL
GDN_PROJ_PAD = 12800

ADAM_LR = 0.001
ADAM_B1 = 0.9
ADAM_B2 = 0.999
ADAM_EPS = 1e-08
ADAM_WD = 0.01
ADAM_STEP = 10


def _cparams(sem=None):
    return pltpu.CompilerParams(dimension_semantics=sem, vmem_limit_bytes=VMEM_LIMIT)


def _tile(n, cap):
    if n <= cap:
        return n
    best = None
    for d in range(LANES, cap + 1, LANES):
        if n % d == 0:
            best = d
    assert best is not None, (n, cap)
    return best


def _dg(a, b, ca, cb, prec=None):
    return lax.dot_general(a, b, (((ca,), (cb,)), ((), ())), preferred_element_type=F32, precision=prec)


def _mx(a):
    return a.astype(MXU)


@jax.custom_vjp
def _nn(a, b):
    return _dg(_mx(a), _mx(b), 1, 0)


def _nn_f(a, b):
    return _nn(a, b), (a, b)


def _nn_b(res, ct):
    a, b = res
    return _dg(_mx(ct), _mx(b), 1, 1), _dg(_mx(a), _mx(ct), 0, 0)


_nn.defvjp(_nn_f, _nn_b)


@jax.custom_vjp
def _nt(a, b):
    return _dg(_mx(a), _mx(b), 1, 1)


def _nt_f(a, b):
    return _nt(a, b), (a, b)


def _nt_b(res, ct):
    a, b = res
    return _dg(_mx(ct), _mx(b), 1, 0), _dg(_mx(ct), _mx(a), 0, 0)


_nt.defvjp(_nt_f, _nt_b)


@jax.custom_vjp
def _tn(a, b):
    return _dg(_mx(a), _mx(b), 0, 0)


def _tn_f(a, b):
    return _tn(a, b), (a, b)


def _tn_b(res, ct):
    a, b = res
    return _dg(_mx(b), _mx(ct), 1, 1), _dg(_mx(a), _mx(ct), 1, 0)


_tn.defvjp(_tn_f, _tn_b)


def _softplus(x):
    return jnp.maximum(x, 0.0) + jnp.log(1.0 + jnp.exp(-jnp.abs(x)))


def _silu(x):
    return x * jax.nn.sigmoid(x)


MM_TM = 1024
MM_TN = 1280
MM_VMEM_BUDGET = 40 * 1024 * 1024


def _mm(a, b, *, ta=False, tb=False, out_dtype=F32, name, ride=None):
    if ta:
        kdim, m = a.shape
    else:
        m, kdim = a.shape
    if tb:
        n, kb = b.shape
    else:
        kb, n = b.shape
    assert kdim == kb, (a.shape, b.shape, ta, tb)
    tm, tn = _tile(m, MM_TM), _tile(n, MM_TN)
    abytes, bbytes, obytes = a.dtype.itemsize, b.dtype.itemsize, jnp.dtype(out_dtype).itemsize
    tk = LANES
    for d in range(LANES, kdim + 1, LANES):
        if kdim % d == 0 and 2 * d * (tm * abytes + tn * bbytes) + tm * tn * (2 * obytes + 4) <= MM_VMEM_BUDGET:
            tk = d
    nk = kdim // tk
    ca, cb = (0 if ta else 1), (1 if tb else 0)

    def body(a_ref, b_ref, o_ref, *acc):
        part = _dg(_mx(a_ref[...]), _mx(b_ref[...]), ca, cb)
        if nk == 1:
            o_ref[...] = part.astype(out_dtype)
            return
        acc_ref, = acc
        k = pl.program_id(2)

        @pl.when(k == 0)
        def _():
            acc_ref[...] = part

        @pl.when(k > 0)
        def _():
            acc_ref[...] += part

        @pl.when(k == nk - 1)
        def _():
            o_ref[...] = acc_ref[...].astype(out_dtype)

    a_spec = pl.BlockSpec((tk, tm), lambda i, j, k: (k, i)) if ta else pl.BlockSpec((tm, tk), lambda i, j, k: (i, k))
    b_spec = pl.BlockSpec((tn, tk), lambda i, j, k: (j, k)) if tb else pl.BlockSpec((tk, tn), lambda i, j, k: (k, j))
    (out,), rode = _hosted_call(
        body, name=name, grid=(m // tm, n // tn, nk),
        in_specs=[a_spec, b_spec], out_specs=[pl.BlockSpec((tm, tn), lambda i, j, k: (i, j))],
        out_shape=[jax.ShapeDtypeStruct((m, n), out_dtype)],
        scratch_shapes=[pltpu.VMEM((tm, tn), F32)] if nk > 1 else [],
        sem=("parallel", "parallel", "arbitrary"), args=(a, b), ride=ride)
    return out if ride is None else (out, rode)


def _ln_fwd(h, y, g, b, name):
    t, d = h.shape
    tr = _tile(t, 256)

    def body(h_ref, y_ref, g_ref, b_ref, o_ref):
        r = ALPHA * h_ref[...] + y_ref[...]
        mu = jnp.mean(r, -1, keepdims=True)
        xc = r - mu
        var = jnp.mean(xc * xc, -1, keepdims=True)
        o_ref[...] = xc * lax.rsqrt(var + LN_EPS) * g_ref[...] + b_ref[...]

    row = pl.BlockSpec((tr, d), lambda i: (i, 0))
    par = pl.BlockSpec((1, d), lambda i: (0, 0))
    return pl.pallas_call(
        body, name=name, grid=(t // tr,), in_specs=[row, row, par, par], out_specs=row,
        out_shape=jax.ShapeDtypeStruct((t, d), F32), compiler_params=_cparams(("parallel",)),
    )(h, y, g, b)


def _ln_bwd(h, y, g, dr_up, du_up, name):
    t, d = h.shape
    tr = _tile(t, 256)

    def body(h_ref, y_ref, g_ref, dr_ref, du_ref, o_ref, dg_ref, db_ref):
        i = pl.program_id(0)

        @pl.when(i == 0)
        def _():
            dg_ref[...] = jnp.zeros_like(dg_ref)
            db_ref[...] = jnp.zeros_like(db_ref)

        dout = ALPHA * dr_ref[...] + du_ref[...]
        r = ALPHA * h_ref[...] + y_ref[...]
        mu = jnp.mean(r, -1, keepdims=True)
        xc = r - mu
        rstd = lax.rsqrt(jnp.mean(xc * xc, -1, keepdims=True) + LN_EPS)
        xh = xc * rstd
        dxh = dout * g_ref[...]
        o_ref[...] = rstd * (dxh - jnp.mean(dxh, -1, keepdims=True) - xh * jnp.mean(dxh * xh, -1, keepdims=True))
        dg_ref[...] += jnp.sum(dout * xh, 0, keepdims=True)
        db_ref[...] += jnp.sum(dout, 0, keepdims=True)

    row = pl.BlockSpec((tr, d), lambda i: (i, 0))
    par = pl.BlockSpec((1, d), lambda i: (0, 0))
    return pl.pallas_call(
        body, name=name, grid=(t // tr,), in_specs=[row, row, par, row, row], out_specs=[row, par, par],
        out_shape=[jax.ShapeDtypeStruct((t, d), F32), jax.ShapeDtypeStruct((1, d), F32), jax.ShapeDtypeStruct((1, d), F32)],
        compiler_params=_cparams(("arbitrary",)),
    )(h, y, g, dr_up, du_up)


def _loss_head(h, tgt, name):
    t, d = h.shape
    tr = _tile(t, 256)

    def body(h_ref, t_ref, l_ref, d_ref):
        i = pl.program_id(0)

        @pl.when(i == 0)
        def _():
            l_ref[...] = jnp.zeros_like(l_ref)

        e = h_ref[...] - t_ref[...]
        d_ref[...] = e * (1.0 / d)
        l_ref[...] += 0.5 * jnp.sum(jnp.mean(e * e, -1, keepdims=True))

    row = pl.BlockSpec((tr, d), lambda i: (i, 0))
    return pl.pallas_call(
        body, name=name, grid=(t // tr,), in_specs=[row, row],
        out_specs=[pl.BlockSpec((8, LANES), lambda i: (0, 0)), row],
        out_shape=[jax.ShapeDtypeStruct((8, LANES), F32), jax.ShapeDtypeStruct((t, d), F32)],
        compiler_params=_cparams(("arbitrary",)),
    )(h, tgt)


def _axpy(dr, du, name):
    t, d = dr.shape
    tr = _tile(t, 256)

    def body(a_ref, b_ref, o_ref):
        o_ref[...] = ALPHA * a_ref[...] + b_ref[...]

    row = pl.BlockSpec((tr, d), lambda i: (i, 0))
    return pl.pallas_call(
        body, name=name, grid=(t // tr,), in_specs=[row, row], out_specs=row,
        out_shape=jax.ShapeDtypeStruct((t, d), F32), compiler_params=_cparams(("parallel",)),
    )(dr, du)


CONV_TT = 512
CONV_TC = 512


HALO = 8


def _shift_down(cur, halo, s, row):
    if s == 0:
        return cur
    tt = cur.shape[0]
    edge = jnp.concatenate([pltpu.roll(halo, s, 0), jnp.zeros((tt - HALO, cur.shape[1]), cur.dtype)], axis=0)
    return jnp.where(row >= s, pltpu.roll(cur, s, 0), edge)


def _shift_up(cur, halo, s, row, tt):
    if s == 0:
        return cur
    edge = jnp.concatenate([jnp.zeros((tt - HALO, cur.shape[1]), cur.dtype), pltpu.roll(halo, HALO - s, 0)], axis=0)
    return jnp.where(row < tt - s, pltpu.roll(cur, tt - s, 0), edge)


def _conv_fwd(proj, col0, w, b, name, ride=None):
    t = proj.shape[0]
    c = w.shape[1]
    tt = _tile(t, CONV_TT)
    cb0 = col0 // CONV_TC

    def body(x_ref, p_ref, w_ref, b_ref, o_ref):
        i = pl.program_id(1)
        x = x_ref[...]
        p = jnp.where(i > 0, p_ref[...], 0.0)
        row = lax.broadcasted_iota(jnp.int32, x.shape, 0)
        pre = b_ref[...] + w_ref[3:4, :] * x
        for s in (1, 2, 3):
            pre = pre + w_ref[3 - s:4 - s, :] * _shift_down(x, p, s, row)
        o_ref[...] = _silu(pre)

    (act,), rode = _hosted_call(
        body, name=name, grid=(c // CONV_TC, t // tt),
        in_specs=[pl.BlockSpec((tt, CONV_TC), lambda j, i: (i, cb0 + j)),
                  pl.BlockSpec((HALO, CONV_TC), lambda j, i: (jnp.maximum(i * (tt // HALO) - 1, 0), cb0 + j)),
                  pl.BlockSpec((4, CONV_TC), lambda j, i: (0, j)),
                  pl.BlockSpec((1, CONV_TC), lambda j, i: (0, j))],
        out_specs=[pl.BlockSpec((tt, CONV_TC), lambda j, i: (i, j))],
        out_shape=[jax.ShapeDtypeStruct((t, c), F32)], scratch_shapes=[],
        sem=("parallel", "parallel"), args=(proj, proj, w, b), ride=ride)
    return act, rode


def _conv_bwd_pre(proj, col0, w, b, dact, name):
    t = proj.shape[0]
    c = w.shape[1]
    tt = _tile(t, CONV_TT)
    cb0 = col0 // CONV_TC

    def body(x_ref, p_ref, w_ref, b_ref, d_ref, dpre_ref, dw_ref, db_ref):
        i = pl.program_id(1)

        @pl.when(i == 0)
        def _():
            dw_ref[...] = jnp.zeros_like(dw_ref)
            db_ref[...] = jnp.zeros_like(db_ref)

        x = x_ref[...]
        p = jnp.where(i > 0, p_ref[...], 0.0)
        row = lax.broadcasted_iota(jnp.int32, x.shape, 0)
        sh = [_shift_down(x, p, s, row) for s in range(4)]
        pre = b_ref[...] + w_ref[3:4, :] * sh[0]
        for s in (1, 2, 3):
            pre = pre + w_ref[3 - s:4 - s, :] * sh[s]
        sg = jax.nn.sigmoid(pre)
        dpre = d_ref[...] * (sg * (1.0 + pre * (1.0 - sg)))
        dpre_ref[...] = dpre
        for s in range(4):
            dw_ref[3 - s:4 - s, :] += jnp.sum(dpre * sh[s], 0, keepdims=True)
        db_ref[...] += jnp.sum(dpre, 0, keepdims=True)

    return pl.pallas_call(
        body, name=name, grid=(c // CONV_TC, t // tt),
        in_specs=[pl.BlockSpec((tt, CONV_TC), lambda j, i: (i, cb0 + j)),
                  pl.BlockSpec((HALO, CONV_TC), lambda j, i: (jnp.maximum(i * (tt // HALO) - 1, 0), cb0 + j)),
                  pl.BlockSpec((4, CONV_TC), lambda j, i: (0, j)),
                  pl.BlockSpec((1, CONV_TC), lambda j, i: (0, j)),
                  pl.BlockSpec((tt, CONV_TC), lambda j, i: (i, j))],
        out_specs=[pl.BlockSpec((tt, CONV_TC), lambda j, i: (i, j)),
                   pl.BlockSpec((4, CONV_TC), lambda j, i: (0, j)),
                   pl.BlockSpec((1, CONV_TC), lambda j, i: (0, j))],
        out_shape=[jax.ShapeDtypeStruct((t, c), F32), jax.ShapeDtypeStruct((4, c), F32), jax.ShapeDtypeStruct((1, c), F32)],
        compiler_params=_cparams(("parallel", "arbitrary")),
    )(proj, proj, w, b, dact)


def _conv_bwd_x(dpre, w, name):
    t, c = dpre.shape
    tt = _tile(t, CONV_TT)
    nt = t // tt

    def body(d_ref, n_ref, w_ref, o_ref):
        i = pl.program_id(1)
        d = d_ref[...]
        nx = jnp.where(i < nt - 1, n_ref[...], 0.0)
        row = lax.broadcasted_iota(jnp.int32, d.shape, 0)
        acc = w_ref[3:4, :] * d
        for s in (1, 2, 3):
            acc = acc + w_ref[3 - s:4 - s, :] * _shift_up(d, nx, s, row, tt)
        o_ref[...] = acc

    return pl.pallas_call(
        body, name=name, grid=(c // CONV_TC, nt),
        in_specs=[pl.BlockSpec((tt, CONV_TC), lambda j, i: (i, j)),
                  pl.BlockSpec((HALO, CONV_TC), lambda j, i: (jnp.minimum((i + 1) * (tt // HALO), t // HALO - 1), j)),
                  pl.BlockSpec((4, CONV_TC), lambda j, i: (0, j))],
        out_specs=pl.BlockSpec((tt, CONV_TC), lambda j, i: (i, j)),
        out_shape=jax.ShapeDtypeStruct((t, c), F32), compiler_params=_cparams(("parallel", "parallel")),
    )(dpre, dpre, w)


SSD_GB = 4


def _split3(a):
    a1 = _mx(a)
    r = a - a1.astype(F32)
    a2 = _mx(r)
    return a1, a2, _mx(r - a2.astype(F32))


@jax.custom_vjp
def _sel_r(a, c):
    cm = _mx(c)
    p1, p2, p3 = _split3(a)
    return _dg(p1, cm, 1, 0) + (_dg(p2, cm, 1, 0) + _dg(p3, cm, 1, 0))


def _sel_r_f(a, c):
    return _sel_r(a, c), c


def _sel_r_b(c, ct):
    cm = _mx(c)
    p1, p2, p3 = _split3(ct)
    return _dg(p1, cm, 1, 1) + (_dg(p2, cm, 1, 1) + _dg(p3, cm, 1, 1)), jnp.zeros_like(c)


_sel_r.defvjp(_sel_r_f, _sel_r_b)


@jax.custom_vjp
def _sel_l(c, a):
    cm = _mx(c)
    p1, p2, p3 = _split3(a)
    return _dg(cm, p1, 1, 0) + (_dg(cm, p2, 1, 0) + _dg(cm, p3, 1, 0))


def _sel_l_f(c, a):
    return _sel_l(c, a), c


def _sel_l_b(c, ct):
    cm = _mx(c)
    p1, p2, p3 = _split3(ct)
    return jnp.zeros_like(c), _dg(cm, p1, 0, 0) + (_dg(cm, p2, 0, 0) + _dg(cm, p3, 0, 0))


_sel_l.defvjp(_sel_l_f, _sel_l_b)


def _ssd_chunk(gb, x, z, bm, cm, dtraw, dtb, alog, dsk, nw, prev):
    L = x.shape[1]
    r_i = lax.broadcasted_iota(jnp.int32, (L, L), 0)
    c_i = lax.broadcasted_iota(jnp.int32, (L, L), 1)
    causal = r_i >= c_i
    dt = _softplus(dtraw + dtb)
    a = dt * (-jnp.exp(alog))
    acs = _sel_l(causal.astype(F32), a)
    e_r = lax.broadcasted_iota(jnp.int32, (LANES, SSD_GS), 0)
    e_c = lax.broadcasted_iota(jnp.int32, (LANES, SSD_GS), 1)
    hpg = SSD_H // SSD_G
    sels = [(e_r == (gb * SSD_GB + i) * hpg + jnp.right_shift(e_c, 6)).astype(F32) for i in range(SSD_GB)]
    dt_x = _cat0([_sel_r(dt, s) for s in sels])
    acs_x = _cat0([_sel_r(acs, s) for s in sels])
    d_x = _cat0([_sel_r(jnp.broadcast_to(dsk, (8, LANES)), s)[0:1] for s in sels])
    last = lax.broadcasted_iota(jnp.int32, (L, 1), 0) == L - 1
    alast = jnp.sum(jnp.where(last, acs_x, 0.0), axis=1, keepdims=True)
    xdt = x * dt_x
    cb = _bnt(cm, bm)
    lane = lax.broadcasted_iota(jnp.int32, (L, LANES), 1)
    ys = []
    for j in range(SSD_GS // LANES):
        xp = xdt[:, :, j * LANES:(j + 1) * LANES]
        yp = None
        for hh in range(2):
            c0 = (2 * j + hh) * SSD_P
            cmx = jnp.broadcast_to(acs_x[:, :, c0:c0 + 1], (SSD_GB, L, L))
            dec = jnp.exp(jnp.where(causal, cmx - jnp.swapaxes(cmx, 1, 2), -jnp.inf))
            half = (lane < SSD_P) if hh == 0 else (lane >= SSD_P)
            t = _bnn(cb * dec, jnp.where(half, xp, 0.0))
            yp = t if yp is None else yp + t
        ys.append(yp)
    y_diag = jnp.concatenate(ys, axis=2)
    st = _btn(bm, xdt * jnp.exp(alast - acs_x))
    new = prev * jnp.exp(alast) + st
    y_off = _bnn(cm, prev) * jnp.exp(acs_x)
    y = y_diag + y_off + x * d_x
    yg = y * _silu(z)
    yn = yg * lax.rsqrt(jnp.mean(yg * yg, -1, keepdims=True) + RMS_EPS) * nw
    return yn, new


def _groups_of(ref, width):
    return _cat0([ref[:, i * width:(i + 1) * width] for i in range(SSD_GB)])


def _put_groups(ref, val, width):
    for i in range(SSD_GB):
        ref[:, i * width:(i + 1) * width] = val[i].astype(ref.dtype)


def _ssd_specs(nc, rev):
    cc = (lambda c: nc - 1 - c) if rev else (lambda c: c)
    wx, wb = SSD_GB * SSD_GS, SSD_GB * SSD_N
    dtb = (SSD_DI + SSD_CONV_DIM) // LANES
    bb = SSD_DI // wb
    cbk = (SSD_DI + SSD_G * SSD_N) // wb
    par = pl.BlockSpec((1, LANES), lambda c, g: (0, 0))
    return dict(
        z=pl.BlockSpec((SSD_L, wx), lambda c, g: (cc(c), g)),
        dt=pl.BlockSpec((SSD_L, LANES), lambda c, g: (cc(c), dtb)),
        x=pl.BlockSpec((SSD_L, wx), lambda c, g: (cc(c), g)),
        bm=pl.BlockSpec((SSD_L, wb), lambda c, g: (cc(c), bb + g)),
        cm=pl.BlockSpec((SSD_L, wb), lambda c, g: (cc(c), cbk + g)),
        par=par,
        nw=pl.BlockSpec((1, wx), lambda c, g: (0, g)),
        st=pl.BlockSpec((1, SSD_GB, SSD_N, SSD_GS), lambda c, g: (cc(c), g, 0, 0)),
        y=pl.BlockSpec((SSD_L, wx), lambda c, g: (cc(c), g)),
        bc=pl.BlockSpec((SSD_L, wb), lambda c, g: (cc(c), g)),
        dtout=pl.BlockSpec((SSD_L, LANES), lambda c, g: (cc(c), 0)),
    )


def _ssd_fwd(proj, act, dtb, alog, dsk, nw, name, ride=None):
    t = proj.shape[0]
    nc = t // SSD_L
    sp = _ssd_specs(nc, False)

    def body(z_ref, dt_ref, x_ref, bm_ref, cm_ref, dtb_ref, alog_ref, dsk_ref, nw_ref, y_ref, st_ref, state):
        c, g = pl.program_id(0), pl.program_id(1)
        mine = pl.ds(g * SSD_GB, SSD_GB)

        @pl.when(c == 0)
        def _():
            state[mine] = jnp.zeros((SSD_GB, SSD_N, SSD_GS), F32)

        prev = state[mine]
        st_ref[0] = prev
        yn, new = _ssd_chunk(g, _groups_of(x_ref, SSD_GS), _groups_of(z_ref, SSD_GS), _groups_of(bm_ref, SSD_N),
                             _groups_of(cm_ref, SSD_N), dt_ref[...], dtb_ref[...], alog_ref[...], dsk_ref[...],
                             _groups_of(nw_ref, SSD_GS), prev)
        _put_groups(y_ref, yn, SSD_GS)
        state[mine] = new

    return _hosted_call(
        body, name=name, grid=(nc, SSD_G // SSD_GB),
        in_specs=[sp["z"], sp["dt"], sp["x"], sp["bm"], sp["cm"], sp["par"], sp["par"], sp["par"], sp["nw"]],
        out_specs=[sp["y"], sp["st"]],
        out_shape=[jax.ShapeDtypeStruct((t, SSD_DI), F32), jax.ShapeDtypeStruct((nc, SSD_G, SSD_N, SSD_GS), F32)],
        scratch_shapes=[pltpu.VMEM((SSD_G, SSD_N, SSD_GS), F32)],
        sem=("arbitrary", "arbitrary"), args=(proj, proj, act, act, act, dtb, alog, dsk, nw), ride=ride)


def _ssd_bwd(proj, act, dtb, alog, dsk, nw, states, dyn, name, ride=None):
    t = proj.shape[0]
    nc = t // SSD_L
    sp = _ssd_specs(nc, True)

    def body(z_ref, dt_ref, x_ref, bm_ref, cm_ref, dtb_ref, alog_ref, dsk_ref, nw_ref, st_ref, dy_ref,
             dx_ref, dz_ref, dbm_ref, dcm_ref, ddt_ref, ddtb_ref, dalog_ref, ddsk_ref, dnw_ref, dstate):
        c, g = pl.program_id(0), pl.program_id(1)
        mine = pl.ds(g * SSD_GB, SSD_GB)

        @pl.when(c == 0)
        def _():
            dstate[mine] = jnp.zeros((SSD_GB, SSD_N, SSD_GS), F32)

        @pl.when((c == 0) & (g == 0))
        def _():
            ddtb_ref[...] = jnp.zeros_like(ddtb_ref)
            dalog_ref[...] = jnp.zeros_like(dalog_ref)
            ddsk_ref[...] = jnp.zeros_like(ddsk_ref)
            dnw_ref[...] = jnp.zeros_like(dnw_ref)

        @pl.when(g == 0)
        def _():
            ddt_ref[...] = jnp.zeros_like(ddt_ref)

        _, vjp = jax.vjp(functools.partial(_ssd_chunk, g), _groups_of(x_ref, SSD_GS), _groups_of(z_ref, SSD_GS),
                         _groups_of(bm_ref, SSD_N), _groups_of(cm_ref, SSD_N), dt_ref[...], dtb_ref[...], alog_ref[...],
                         dsk_ref[...], _groups_of(nw_ref, SSD_GS), st_ref[0])
        dx, dz, dbm, dcm, ddt, ddtb, dalog, ddsk, dnw, dprev = vjp((_groups_of(dy_ref, SSD_GS), dstate[mine]))
        _put_groups(dx_ref, dx, SSD_GS)
        _put_groups(dz_ref, dz, SSD_GS)
        _put_groups(dbm_ref, dbm, SSD_N)
        _put_groups(dcm_ref, dcm, SSD_N)
        ddt_ref[...] += ddt
        ddtb_ref[...] += ddtb
        dalog_ref[...] += dalog
        ddsk_ref[...] += ddsk
        dnw_ref[mine] += dnw
        dstate[mine] = dprev

    par_out = pl.BlockSpec((1, LANES), lambda c, g: (0, 0))
    sds = jax.ShapeDtypeStruct
    return _hosted_call(
        body, name=name, grid=(nc, SSD_G // SSD_GB),
        in_specs=[sp["z"], sp["dt"], sp["x"], sp["bm"], sp["cm"], sp["par"], sp["par"], sp["par"], sp["nw"],
                  sp["st"], sp["y"]],
        out_specs=[sp["y"], sp["y"], sp["bc"], sp["bc"], sp["dtout"], par_out, par_out, par_out,
                   pl.BlockSpec((SSD_G, 1, SSD_GS), lambda c, g: (0, 0, 0))],
        out_shape=[sds((t, SSD_DI), F32), sds((t, SSD_DI), F32), sds((t, SSD_G * SSD_N), F32), sds((t, SSD_G * SSD_N), F32),
                   sds((t, LANES), F32), sds((1, LANES), F32), sds((1, LANES), F32), sds((1, LANES), F32),
                   sds((SSD_G, 1, SSD_GS), F32)],
        scratch_shapes=[pltpu.VMEM((SSD_G, SSD_N, SSD_GS), F32)],
        sem=("arbitrary", "arbitrary"), args=(proj, proj, act, act, act, dtb, alog, dsk, nw, states, dyn), ride=ride)


def _pad_lanes(v, width=LANES, offset=0):
    return jnp.pad(v.astype(F32), (offset, width - offset - v.shape[0])).reshape(1, width)


def _ride_of(rides, key, scatter):
    arrs = (rides or {}).get(key)
    return (arrs, scatter) if arrs else None


def _mm_r(a, b, ride, **kw):
    out = _mm(a, b, ride=ride, **kw)
    return (out, []) if ride is None else out


def _ssd_layer_fwd(u, w, ln_g, ln_b, tag, rides=None, late=None):
    w_in = jnp.concatenate([w["in_w"], jnp.zeros((D_MODEL, LANES - SSD_H), w["in_w"].dtype)], axis=1)
    dtb, alog, dsk = _pad_lanes(w["dt_bias"]), _pad_lanes(w["a_log"]), _pad_lanes(w["d"])
    if late is None:
        proj = _mm(u, w_in, name=tag + "_in")
    else:
        proj, got = _mm(u, w_in, name=tag + "_in", ride=(late[0], False))
        w.update(late[1](got))
    nw = w["norm_w"].reshape(1, SSD_DI)
    cb = w["conv_b"].reshape(1, SSD_CONV_DIM)
    rode = {}
    act, rode["conv"] = _conv_fwd(proj, SSD_DI, w["conv_w"], cb, name=tag + "_conv", ride=_ride_of(rides, "conv", False))
    (yn, states), rode["scan"] = _ssd_fwd(proj, act, dtb, alog, dsk, nw, name=tag + "_scan", ride=_ride_of(rides, "scan", False))
    y, rode["out"] = _mm_r(yn, w["out_w"], _ride_of(rides, "out", False), name=tag + "_out")
    h = _ln_fwd(u, y, ln_g, ln_b, name=tag + "_ln")
    saved = dict(u=u, w_in=w_in, proj=proj, act=act, states=states, yn=yn, y=y, dtb=dtb, alog=alog, dsk=dsk, nw=nw, cb=cb)
    return h, saved, rode


def _ssd_layer_bwd(s, w, ln_g, dr_up, du_up, tag, rides=None, own=None):
    dr, dg, db = _ln_bwd(s["u"], s["y"], ln_g, dr_up, du_up, name=tag + "_ln_b")
    dyn = _mm(dr, w["out_w"], tb=True, name=tag + "_out_bx")
    d_out_w = _mm(s["yn"], dr, ta=True, out_dtype=WIRE, name=tag + "_out_bw")
    rode = {}
    (dx, dz, dbm, dcm, ddt, ddtb, dalog, ddsk, dnw), rode["scan_b"] = _ssd_bwd(
        s["proj"], s["act"], s["dtb"], s["alog"], s["dsk"], s["nw"], s["states"], dyn, name=tag + "_scan_b",
        ride=_ride_of(rides, "scan_b", True))
    dact = jnp.concatenate([dx, dbm, dcm], axis=1)
    dpre, d_conv_w, d_conv_b = _conv_bwd_pre(s["proj"], SSD_DI, w["conv_w"], s["cb"], dact, name=tag + "_conv_bp")
    dxbc = _conv_bwd_x(dpre, w["conv_w"], name=tag + "_conv_bx")
    dproj = jnp.concatenate([dz, dxbc, ddt], axis=1)
    grads = dict(conv_w=d_conv_w, conv_b=d_conv_b.reshape(-1), dt_bias=ddtb[0, :SSD_H], a_log=dalog[0, :SSD_H],
                 d=ddsk[0, :SSD_H], norm_w=dnw.reshape(-1), out_w=d_out_w, ln_g=dg[0], ln_b=db[0])
    late_names = ("conv_w", "conv_b", "norm_w", "out_w")
    got = {}
    if own is None:
        d_in_w = _mm(s["u"], dproj, ta=True, out_dtype=WIRE, name=tag + "_in_bw")
        du = _mm(dproj, s["w_in"], tb=True, name=tag + "_in_bx")
        grads["in_w"] = d_in_w[:, :SSD_DI + SSD_CONV_DIM + SSD_H]
    else:
        d_in_w, late = _mm(s["u"], dproj, ta=True, out_dtype=WIRE, name=tag + "_in_bw",
                           ride=([own(k, grads[k]) for k in late_names], True))
        got = dict(zip(late_names, late))
        grads["in_w"] = d_in_w[:, :SSD_DI + SSD_CONV_DIM + SSD_H]
        du, (got["in_w"],) = _mm(dproj, s["w_in"], tb=True, name=tag + "_in_bx", ride=([own("in_w", grads["in_w"])], True))
    return dr, du, grads, rode, got


MLA_LOW = MLA_QR + MLA_KVR + LANES
MLA_ZB = MLA_LOW // LANES


def _rope_mat():
    r = lax.broadcasted_iota(jnp.int32, (LANES, LANES), 0)
    c = lax.broadcasted_iota(jnp.int32, (LANES, LANES), 1)
    hf = MLA_ROPE // 2
    return jnp.where((c < hf) & (r == c + hf), -1.0, 0.0) + jnp.where((c >= hf) & (c < 2 * hf) & (r == c - hf), 1.0, 0.0)


def _rope(x, cosf, sinf):
    return x * cosf + _sel_r(x, _rope_mat()) * sinf


def _rope_adj(d, cosf, sinf):
    return d * cosf - _sel_r(d * sinf, _rope_mat())


def _mla_low_fn(low, qnw, kvnw, cosf, sinf):
    qc, kvc, kr = low[:, :MLA_QR], low[:, MLA_QR:MLA_QR + MLA_KVR], low[:, MLA_QR + MLA_KVR:]
    qn = qc * lax.rsqrt(jnp.mean(qc * qc, -1, keepdims=True) + RMS_EPS) * qnw
    kvn = kvc * lax.rsqrt(jnp.mean(kvc * kvc, -1, keepdims=True) + RMS_EPS) * kvnw
    return qn, kvn, _rope(kr, cosf, sinf)


def _mla_low_fwd(proj, qnw, kvnw, cosf, sinf, name):
    t = proj.shape[0]
    tr = _tile(t, 256)

    def body(low_ref, qnw_ref, kvnw_ref, cos_ref, sin_ref, qn_ref, kvn_ref, kr_ref):
        qn, kvn, kr = _mla_low_fn(low_ref[...], qnw_ref[...], kvnw_ref[...], cos_ref[...], sin_ref[...])
        qn_ref[...] = qn
        kvn_ref[...] = kvn
        kr_ref[...] = kr

    row = lambda wdt: pl.BlockSpec((tr, wdt), lambda i: (i, 0))
    par = lambda wdt: pl.BlockSpec((1, wdt), lambda i: (0, 0))
    sds = jax.ShapeDtypeStruct
    return pl.pallas_call(
        body, name=name, grid=(t // tr,),
        in_specs=[row(MLA_LOW), par(MLA_QR), par(MLA_KVR), row(LANES), row(LANES)],
        out_specs=[row(MLA_QR), row(MLA_KVR), row(LANES)],
        out_shape=[sds((t, MLA_QR), F32), sds((t, MLA_KVR), F32), sds((t, LANES), F32)],
        compiler_params=_cparams(("parallel",)),
    )(proj, qnw, kvnw, cosf, sinf)


def _mla_low_bwd(proj, qnw, kvnw, cosf, sinf, dqn, dkvn, dkr, name):
    t = proj.shape[0]
    tr = _tile(t, 256)

    def body(low_ref, qnw_ref, kvnw_ref, cos_ref, sin_ref, dqn_ref, dkvn_ref, dkr_ref, dlow_ref, dqnw_ref, dkvnw_ref):
        i = pl.program_id(0)

        @pl.when(i == 0)
        def _():
            dqnw_ref[...] = jnp.zeros_like(dqnw_ref)
            dkvnw_ref[...] = jnp.zeros_like(dkvnw_ref)

        cosf, sinf = cos_ref[...], sin_ref[...]
        _, vjp = jax.vjp(lambda a, b, c: _mla_low_fn(a, b, c, cosf, sinf), low_ref[...], qnw_ref[...], kvnw_ref[...])
        dlow, dq, dk = vjp((dqn_ref[...], dkvn_ref[...], dkr_ref[...]))
        dlow_ref[...] = dlow
        dqnw_ref[...] += dq
        dkvnw_ref[...] += dk

    row = lambda wdt: pl.BlockSpec((tr, wdt), lambda i: (i, 0))
    par = lambda wdt: pl.BlockSpec((1, wdt), lambda i: (0, 0))
    sds = jax.ShapeDtypeStruct
    return pl.pallas_call(
        body, name=name, grid=(t // tr,),
        in_specs=[row(MLA_LOW), par(MLA_QR), par(MLA_KVR), row(LANES), row(LANES), row(MLA_QR), row(MLA_KVR), row(LANES)],
        out_specs=[row(MLA_LOW), par(MLA_QR), par(MLA_KVR)],
        out_shape=[sds((t, MLA_LOW), F32), sds((1, MLA_QR), F32), sds((1, MLA_KVR), F32)],
        compiler_params=_cparams(("arbitrary",)),
    )(proj, qnw, kvnw, cosf, sinf, dqn, dkvn, dkr)


def _rope_heads(x, col_blk0, cosf, sinf, adjoint, name):
    t = x.shape[0]
    tr = _tile(t, 512)

    def body(x_ref, cos_ref, sin_ref, o_ref):
        f = _rope_adj if adjoint else _rope
        o_ref[...] = f(x_ref[...], cos_ref[...], sin_ref[...])

    tab = pl.BlockSpec((tr, LANES), lambda i, h: (i, 0))
    return pl.pallas_call(
        body, name=name, grid=(t // tr, MLA_H),
        in_specs=[pl.BlockSpec((tr, LANES), lambda i, h: (i, col_blk0 + h)), tab, tab],
        out_specs=pl.BlockSpec((tr, LANES), lambda i, h: (i, h)),
        out_shape=jax.ShapeDtypeStruct((t, MLA_H * LANES), F32), compiler_params=_cparams(("parallel", "parallel")),
    )(x, cosf, sinf)


ATT_HB = 4
ATT_W = ATT_HB * LANES


def _att_qk(qn_ref, qr_ref, kn_ref, kr_ref):
    q2 = jnp.concatenate([_heads_of(qn_ref, ATT_HB), _heads_of(qr_ref, ATT_HB)], axis=2)
    kr = kr_ref[...]
    k2 = jnp.concatenate([_heads_of(kn_ref, ATT_HB), jnp.broadcast_to(kr[None], (ATT_HB,) + kr.shape)], axis=2)
    return q2, k2


def _att_scores(q2, k2, masked):
    s = _bdg(_mx(q2 * MLA_SCALE), _mx(k2), 2, 2)
    if masked:
        tq, tk = s.shape[1:]
        s = jnp.where(lax.broadcasted_iota(jnp.int32, (tq, tk), 1) <= lax.broadcasted_iota(jnp.int32, (tq, tk), 0), s, -jnp.inf)
    return s


def _on_causal_blocks(q_blk, k_blk, step):
    @pl.when(k_blk < q_blk)
    def _():
        step(False)

    @pl.when(k_blk == q_blk)
    def _():
        step(True)


def _put_heads(ref, val):
    for b in range(val.shape[0]):
        ref[:, b * LANES:(b + 1) * LANES] = val[b].astype(ref.dtype)


def _att_ds(s, v_ref, o_ref, do_ref, lse_ref):
    do = _heads_of(do_ref, ATT_HB)
    p = jnp.exp(s - _heads_of(lse_ref, ATT_HB)[:, :, 0:1])
    dp = _bdg(_mx(do), _mx(_heads_of(v_ref, ATT_HB)), 2, 2)
    dl = jnp.sum(do * _heads_of(o_ref, ATT_HB), -1, keepdims=True)
    return _mx(p), _mx(p * (dp - dl) * MLA_SCALE), do


def _attn_fwd(q, qr, kv, kr, proj, name, ride=None):
    t = q.shape[0]
    tq = tk = _tile(t, ATT_BLK)
    nq = nk = t // tq

    def body(qn_ref, qr_ref, kn_ref, kr_ref, v_ref, *rest):
        z_refs, (o_ref, og_ref, lse_ref, m_s, l_s, acc_s) = rest[:ATT_HB], rest[ATT_HB:]
        i, j = pl.program_id(1), pl.program_id(2)

        @pl.when(j == 0)
        def _():
            m_s[...] = jnp.full_like(m_s, -jnp.inf)
            l_s[...] = jnp.zeros_like(l_s)
            acc_s[...] = jnp.zeros_like(acc_s)

        def step(masked):
            s = _att_scores(*_att_qk(qn_ref, qr_ref, kn_ref, kr_ref), masked)
            m_new = jnp.maximum(m_s[...], jnp.max(s, -1, keepdims=True))
            p = jnp.exp(s - m_new)
            corr = jnp.exp(m_s[...] - m_new)
            l_s[...] = corr * l_s[...] + jnp.sum(p, -1, keepdims=True)
            acc_s[...] = corr * acc_s[...] + _bdg(_mx(p), _mx(_heads_of(v_ref, ATT_HB)), 2, 1)
            m_s[...] = m_new

        _on_causal_blocks(i, j, step)

        @pl.when(j == nk - 1)
        def _():
            o = acc_s[...] / l_s[...]
            _put_heads(o_ref, o)
            lse = m_s[...] + jnp.log(l_s[...])
            for b in range(ATT_HB):
                og_ref[:, b * LANES:(b + 1) * LANES] = (o[b] * _silu(z_refs[b][...])).astype(og_ref.dtype)
                lse_ref[:, b * LANES:(b + 1) * LANES] = jnp.broadcast_to(lse[b], (tq, LANES))

    qs = lambda off: pl.BlockSpec((tq, ATT_W), lambda h, i, j: (i, off // ATT_HB + h))
    ks = lambda off: pl.BlockSpec((tk, ATT_W), lambda h, i, j: (jnp.minimum(j, i), off // ATT_HB + h))
    zs = [pl.BlockSpec((tq, LANES), functools.partial(lambda b, h, i, j: (i, MLA_ZB + h * ATT_HB + b), b)) for b in range(ATT_HB)]
    sds = jax.ShapeDtypeStruct
    return _hosted_call(
        body, name=name, grid=(MLA_H // ATT_HB, nq, nk),
        in_specs=[qs(0), qs(0), ks(0), pl.BlockSpec((tk, LANES), lambda h, i, j: (jnp.minimum(j, i), 0)), ks(MLA_H)] + zs,
        out_specs=[qs(0), qs(0), qs(0)],
        out_shape=[sds((t, MLA_GATE), F32), sds((t, MLA_GATE), F32), sds((t, MLA_H * LANES), F32)],
        scratch_shapes=[pltpu.VMEM((ATT_HB, tq, 1), F32), pltpu.VMEM((ATT_HB, tq, 1), F32), pltpu.VMEM((ATT_HB, tq, LANES), F32)],
        sem=("parallel", "parallel", "arbitrary"), args=(q, qr, kv, kr, kv) + (proj,) * ATT_HB, ride=ride)


def _gate_bwd(dog, o, proj, name):
    t = o.shape[0]
    tr = _tile(t, 512)

    def body(d_ref, o_ref, z_ref, do_ref, dz_ref):
        z = z_ref[...]
        sg = jax.nn.sigmoid(z)
        d = d_ref[...]
        do_ref[...] = d * z * sg
        dz_ref[...] = d * o_ref[...] * (sg * (1.0 + z * (1.0 - sg)))

    blk = lambda off: pl.BlockSpec((tr, 512), lambda i, j: (i, off + j))
    assert MLA_LOW % 512 != 0 or True
    zspec = pl.BlockSpec((tr, LANES), lambda i, j: (i, MLA_ZB + j))
    b128 = pl.BlockSpec((tr, LANES), lambda i, j: (i, j))
    sds = jax.ShapeDtypeStruct
    return pl.pallas_call(
        body, name=name, grid=(t // tr, MLA_GATE // LANES),
        in_specs=[b128, b128, zspec], out_specs=[b128, b128],
        out_shape=[sds((t, MLA_GATE), F32), sds((t, MLA_GATE), F32)],
        compiler_params=_cparams(("parallel", "parallel")),
    )(dog, o, proj)


def _attn_bwd_q(q, qr, kv, kr, o, do, lse, name, ride=None):
    t = q.shape[0]
    tq = tk = _tile(t, ATT_BLK)
    nq = nk = t // tq

    def body(qn_ref, qr_ref, kn_ref, kr_ref, v_ref, o_ref, do_ref, lse_ref, dqn_ref, dqr_ref, an_s, ar_s):
        i, j = pl.program_id(1), pl.program_id(2)

        @pl.when(j == 0)
        def _():
            an_s[...] = jnp.zeros_like(an_s)
            ar_s[...] = jnp.zeros_like(ar_s)

        def step(masked):
            q2, k2 = _att_qk(qn_ref, qr_ref, kn_ref, kr_ref)
            ds = _att_ds(_att_scores(q2, k2, masked), v_ref, o_ref, do_ref, lse_ref)[1]
            dq2 = _bdg(ds, _mx(k2), 2, 1)
            an_s[...] += dq2[:, :, :LANES]
            ar_s[...] += dq2[:, :, LANES:]

        _on_causal_blocks(i, j, step)

        @pl.when(j == nk - 1)
        def _():
            _put_heads(dqn_ref, an_s[...])
            _put_heads(dqr_ref, ar_s[...])

    qs = lambda off: pl.BlockSpec((tq, ATT_W), lambda h, i, j: (i, off // ATT_HB + h))
    ks = lambda off: pl.BlockSpec((tk, ATT_W), lambda h, i, j: (jnp.minimum(j, i), off // ATT_HB + h))
    sds = jax.ShapeDtypeStruct
    return _hosted_call(
        body, name=name, grid=(MLA_H // ATT_HB, nq, nk),
        in_specs=[qs(0), qs(0), ks(0), pl.BlockSpec((tk, LANES), lambda h, i, j: (jnp.minimum(j, i), 0)), ks(MLA_H),
                  qs(0), qs(0), qs(0)],
        out_specs=[qs(0), qs(0)],
        out_shape=[sds((t, MLA_H * LANES), F32), sds((t, MLA_H * LANES), F32)],
        scratch_shapes=[pltpu.VMEM((ATT_HB, tq, LANES), F32), pltpu.VMEM((ATT_HB, tq, LANES), F32)],
        sem=("parallel", "parallel", "arbitrary"), args=(q, qr, kv, kr, kv, o, do, lse), ride=ride)


def _attn_bwd_kv(q, qr, kv, kr, o, do, lse, name, ride=None):
    t = q.shape[0]
    tq = tk = _tile(t, ATT_BLK)
    nq = nk = t // tq

    def body(qn_ref, qr_ref, kn_ref, kr_ref, v_ref, o_ref, do_ref, lse_ref, dkn_ref, dv_ref, dkr_ref, akn_s, av_s):
        j, h, i = pl.program_id(0), pl.program_id(1), pl.program_id(2)

        @pl.when((h == 0) & (i == 0))
        def _():
            dkr_ref[...] = jnp.zeros_like(dkr_ref)

        @pl.when(i == 0)
        def _():
            akn_s[...] = jnp.zeros_like(akn_s)
            av_s[...] = jnp.zeros_like(av_s)

        def step(masked):
            q2, k2 = _att_qk(qn_ref, qr_ref, kn_ref, kr_ref)
            p, ds, do = _att_ds(_att_scores(q2, k2, masked), v_ref, o_ref, do_ref, lse_ref)
            av_s[...] += _bdg(p, _mx(do), 1, 1)
            dk2 = _bdg(ds, _mx(q2), 1, 1)
            akn_s[...] += dk2[:, :, :LANES]
            dkr_ref[...] += jnp.sum(dk2[:, :, LANES:], axis=0)

        _on_causal_blocks(i, j, step)

        @pl.when(i == nq - 1)
        def _():
            _put_heads(dkn_ref, akn_s[...])
            _put_heads(dv_ref, av_s[...])

    qs = lambda off: pl.BlockSpec((tq, ATT_W), lambda j, h, i: (jnp.maximum(i, j), off // ATT_HB + h))
    ks = lambda off: pl.BlockSpec((tk, ATT_W), lambda j, h, i: (j, off // ATT_HB + h))
    sds = jax.ShapeDtypeStruct
    return _hosted_call(
        body, name=name, grid=(nk, MLA_H // ATT_HB, nq),
        in_specs=[qs(0), qs(0), ks(0), pl.BlockSpec((tk, LANES), lambda j, h, i: (j, 0)), ks(MLA_H), qs(0), qs(0), qs(0)],
        out_specs=[ks(0), ks(0), pl.BlockSpec((tk, LANES), lambda j, h, i: (j, 0))],
        out_shape=[sds((t, MLA_H * LANES), F32), sds((t, MLA_H * LANES), F32), sds((t, LANES), F32)],
        scratch_shapes=[pltpu.VMEM((ATT_HB, tk, LANES), F32), pltpu.VMEM((ATT_HB, tk, LANES), F32)],
        sem=("parallel", "arbitrary", "arbitrary"), args=(q, qr, kv, kr, kv, o, do, lse), ride=ride)


def _rope_tables(positions):
    inv_freq = ROPE_THETA ** (-jnp.arange(0, MLA_ROPE, 2, dtype=F32) / MLA_ROPE)
    ang = positions.astype(F32)[:, None] * inv_freq
    pad = jnp.zeros((positions.shape[0], LANES - MLA_ROPE), F32)
    cos, sin = jnp.cos(ang), jnp.sin(ang)
    return jnp.concatenate([cos, cos, pad], 1), jnp.concatenate([sin, sin, pad], 1)


def _mla_weights(w):
    dt = w["in_w"].dtype
    iw = w["in_w"]
    c1 = MLA_QR + MLA_KVR + MLA_ROPE
    w_in = jnp.concatenate([iw[:, :c1], jnp.zeros((D_MODEL, LANES - MLA_ROPE), dt), iw[:, c1:]], axis=1)
    qu = w["q_up_w"].reshape(MLA_QR, MLA_H, MLA_NOPE + MLA_ROPE)
    qrope = jnp.concatenate([qu[:, :, MLA_NOPE:], jnp.zeros((MLA_QR, MLA_H, LANES - MLA_ROPE), dt)], axis=2)
    w_q = jnp.concatenate([qu[:, :, :MLA_NOPE].reshape(MLA_QR, -1), qrope.reshape(MLA_QR, -1)], axis=1)
    kvu = w["kv_up_w"].reshape(MLA_KVR, MLA_H, MLA_NOPE + MLA_V)
    w_kv = jnp.concatenate([kvu[:, :, :MLA_NOPE].reshape(MLA_KVR, -1), kvu[:, :, MLA_NOPE:].reshape(MLA_KVR, -1)], axis=1)
    return w_in, w_q, w_kv


def _mla_layer_fwd(u, w, ln_g, ln_b, cosf, sinf, tag, rides=None):
    w_in, w_q, w_kv = _mla_weights(w)
    qnw, kvnw = w["q_norm_w"].reshape(1, -1), w["kv_norm_w"].reshape(1, -1)
    proj = _mm(u, w_in, name=tag + "_in")
    qn, kvn, kr = _mla_low_fwd(proj, qnw, kvnw, cosf, sinf, name=tag + "_low")
    q = _mm(qn, w_q, name=tag + "_qup")
    kv = _mm(kvn, w_kv, name=tag + "_kvup")
    qr = _rope_heads(q, MLA_H, cosf, sinf, False, name=tag + "_qrope")
    rode = {}
    (o, og, lse), rode["attn"] = _attn_fwd(q, qr, kv, kr, proj, name=tag + "_attn", ride=_ride_of(rides, "attn", False))
    y = _mm(og, w["out_w"], name=tag + "_out")
    h = _ln_fwd(u, y, ln_g, ln_b, name=tag + "_ln")
    saved = dict(u=u, w_in=w_in, w_q=w_q, w_kv=w_kv, qnw=qnw, kvnw=kvnw, proj=proj, qn=qn, kvn=kvn, kr=kr, q=q, kv=kv,
                 qr=qr, o=o, og=og, lse=lse, y=y)
    return h, saved, rode


def _mla_layer_bwd(s, w, ln_g, cosf, sinf, dr_up, du_up, tag, rides=None):
    dr, dg, db = _ln_bwd(s["u"], s["y"], ln_g, dr_up, du_up, name=tag + "_ln_b")
    dog = _mm(dr, w["out_w"], tb=True, name=tag + "_out_bx")
    d_out_w = _mm(s["og"], dr, ta=True, out_dtype=WIRE, name=tag + "_out_bw")
    do, dz = _gate_bwd(dog, s["o"], s["proj"], name=tag + "_gate_b")
    rode = {}
    (dqn_h, dqr_rot), rode["attn_bq"] = _attn_bwd_q(s["q"], s["qr"], s["kv"], s["kr"], s["o"], do, s["lse"],
                                                    name=tag + "_attn_bq", ride=_ride_of(rides, "attn_bq", True))
    (dkn_h, dv_h, dkr_rot), rode["attn_bkv"] = _attn_bwd_kv(s["q"], s["qr"], s["kv"], s["kr"], s["o"], do, s["lse"],
                                                           name=tag + "_attn_bkv", ride=_ride_of(rides, "attn_bkv", True))
    dqr = _rope_heads(dqr_rot, 0, cosf, sinf, True, name=tag + "_qrope_b")
    dq = jnp.concatenate([dqn_h, dqr], axis=1)
    dkv = jnp.concatenate([dkn_h, dv_h], axis=1)
    d_wq = _mm(s["qn"], dq, ta=True, out_dtype=WIRE, name=tag + "_qup_bw")
    dqn = _mm(dq, s["w_q"], tb=True, name=tag + "_qup_bx")
    d_wkv = _mm(s["kvn"], dkv, ta=True, out_dtype=WIRE, name=tag + "_kvup_bw")
    dkvn = _mm(dkv, s["w_kv"], tb=True, name=tag + "_kvup_bx")
    dlow, dqnw, dkvnw = _mla_low_bwd(s["proj"], s["qnw"], s["kvnw"], cosf, sinf, dqn, dkvn, dkr_rot, name=tag + "_low_b")
    dproj = jnp.concatenate([dlow, dz], axis=1)
    d_in = _mm(s["u"], dproj, ta=True, out_dtype=WIRE, name=tag + "_in_bw")
    du = _mm(dproj, s["w_in"], tb=True, name=tag + "_in_bx")
    c1 = MLA_QR + MLA_KVR + MLA_ROPE
    d_in_w = jnp.concatenate([d_in[:, :c1], d_in[:, MLA_LOW:]], axis=1)
    dq3n = d_wq[:, :MLA_H * MLA_NOPE].reshape(MLA_QR, MLA_H, MLA_NOPE)
    dq3r = d_wq[:, MLA_H * MLA_NOPE:].reshape(MLA_QR, MLA_H, LANES)[:, :, :MLA_ROPE]
    d_q_up = jnp.concatenate([dq3n, dq3r], axis=2).reshape(MLA_QR, -1)
    dkv3 = d_wkv.reshape(MLA_KVR, 2, MLA_H, MLA_NOPE)
    d_kv_up = jnp.concatenate([dkv3[:, 0], dkv3[:, 1]], axis=2).reshape(MLA_KVR, -1)
    grads = dict(in_w=d_in_w, q_norm_w=dqnw[0], q_up_w=d_q_up, kv_norm_w=dkvnw[0], kv_up_w=d_kv_up, out_w=d_out_w,
                 ln_g=dg[0], ln_b=db[0])
    return dr, du, grads, rode


GDN_REP = GDN_HV // GDN_HK
GDN_A_LANE = GDN_HV
GDN_HPB = 4
GDN_VPB = GDN_HPB * GDN_REP


def _bdg(a, b, ca, cb):
    return lax.dot_general(a, b, (((ca,), (cb,)), ((0,), (0,))), preferred_element_type=F32)


@jax.custom_vjp
def _bnn(a, b):
    return _bdg(_mx(a), _mx(b), 2, 1)


def _bnn_f(a, b):
    return _bnn(a, b), (a, b)


def _bnn_b(res, ct):
    a, b = res
    return _bdg(_mx(ct), _mx(b), 2, 2), _bdg(_mx(a), _mx(ct), 1, 1)


_bnn.defvjp(_bnn_f, _bnn_b)


@jax.custom_vjp
def _bnt(a, b):
    return _bdg(_mx(a), _mx(b), 2, 2)


def _bnt_f(a, b):
    return _bnt(a, b), (a, b)


def _bnt_b(res, ct):
    a, b = res
    return _bdg(_mx(ct), _mx(b), 2, 1), _bdg(_mx(ct), _mx(a), 1, 1)


_bnt.defvjp(_bnt_f, _bnt_b)


@jax.custom_vjp
def _btn(a, b):
    return _bdg(_mx(a), _mx(b), 1, 1)


def _btn_f(a, b):
    return _btn(a, b), (a, b)


def _btn_b(res, ct):
    a, b = res
    return _bdg(_mx(b), _mx(ct), 2, 2), _bdg(_mx(a), _mx(ct), 2, 1)


_btn.defvjp(_btn_f, _btn_b)


def _h3(a, b, ca=2, cb=1):
    ah, bh = _mx(a), _mx(b)
    al, bl = _mx(a - ah.astype(F32)), _mx(b - bh.astype(F32))
    return _bdg(ah, bh, ca, cb) + (_bdg(ah, bl, ca, cb) + _bdg(al, bh, ca, cb))


@jax.custom_vjp
def _neumann_inverse(x):
    L = x.shape[-1]
    eye = (lax.broadcasted_iota(jnp.int32, (L, L), 0) == lax.broadcasted_iota(jnp.int32, (L, L), 1)).astype(F32)
    inv = eye + x
    xp = x
    for _ in range(L.bit_length() - 2):
        xp = _h3(xp, xp)
        inv = inv + _h3(inv, xp)
    return inv


def _neumann_f(x):
    inv = _neumann_inverse(x)
    return inv, inv


def _neumann_b(inv, ct):
    return (_h3(_h3(inv, ct, 1, 1), inv, 2, 2),)


_neumann_inverse.defvjp(_neumann_f, _neumann_b)


def _cat0(parts):
    return jnp.concatenate([p[None] for p in parts], axis=0)


def _gdn_chunk(hb, q, k, v, z, ba, alog, dtb, nw, s):
    L = q.shape[1]
    r_i = lax.broadcasted_iota(jnp.int32, (L, L), 0)
    c_i = lax.broadcasted_iota(jnp.int32, (L, L), 1)
    incl, strict = r_i >= c_i, r_i > c_i
    rep = lambda t: jnp.broadcast_to(t[:, None], (GDN_HPB, GDN_REP) + t.shape[1:]).reshape((GDN_VPB,) + t.shape[1:])
    qn = rep(q * lax.rsqrt(jnp.sum(q * q, -1, keepdims=True) + RMS_EPS) * (GDN_DK ** -0.5))
    kn = rep(k * lax.rsqrt(jnp.sum(k * k, -1, keepdims=True) + RMS_EPS))
    beta_all = jax.nn.sigmoid(ba)
    g_all = -jnp.exp(alog) * _softplus(ba + dtb)
    gcs_all = _sel_l(incl.astype(F32), g_all)
    lane = lax.broadcasted_iota(jnp.int32, (L, LANES), 1)
    pick = lambda mat, idx: jnp.sum(jnp.where(lane == idx, mat, 0.0), axis=1, keepdims=True)
    beta = _cat0([pick(beta_all, GDN_VPB * hb + b) for b in range(GDN_VPB)])
    gc = _cat0([pick(gcs_all, GDN_A_LANE + GDN_VPB * hb + b) for b in range(GDN_VPB)])
    gm = jnp.broadcast_to(gc, (GDN_VPB, L, L))
    decay = jnp.exp(jnp.where(incl, gm - jnp.swapaxes(gm, 1, 2), -jnp.inf))
    kb = kn * beta
    eg = jnp.exp(gc)
    inv = _neumann_inverse(-jnp.where(strict, _bnt(kb, kn) * decay, 0.0))
    uw = _bnn(inv, jnp.concatenate([v * beta, kb * eg], axis=2))
    uu, ww = uw[:, :, :GDN_DV], uw[:, :, GDN_DV:]
    qk = jnp.where(incl, _bnt(qn, kn) * decay, 0.0)
    last = lax.broadcasted_iota(jnp.int32, (L, 1), 0) == L - 1
    glast = jnp.sum(jnp.where(last, gc, 0.0), axis=1, keepdims=True)
    kdec = kn * jnp.exp(glast - gc)
    vnew = uu - _bnn(ww, s)
    o = _bnn(qn * eg, s) + _bnn(qk, vnew)
    new = s * jnp.exp(glast) + _btn(kdec, vnew)
    on = o * lax.rsqrt(jnp.mean(o * o, -1, keepdims=True) + RMS_EPS) * nw * _silu(z)
    return on, new


def _heads_of(ref, n):
    return _cat0([ref[:, i * LANES:(i + 1) * LANES] for i in range(n)])


def _gdn_specs(nc, rev):
    cc = (lambda c: nc - 1 - c) if rev else (lambda c: c)
    wq, wv = GDN_HPB * GDN_DK, GDN_VPB * GDN_DV
    par = pl.BlockSpec((1, LANES), lambda c, h: (0, 0))
    return dict(
        q=pl.BlockSpec((GDN_L, wq), lambda c, h: (cc(c), h)),
        k=pl.BlockSpec((GDN_L, wq), lambda c, h: (cc(c), GDN_KEY // wq + h)),
        v=pl.BlockSpec((GDN_L, wv), lambda c, h: (cc(c), 2 * GDN_KEY // wv + h)),
        z=pl.BlockSpec((GDN_L, wv), lambda c, h: (cc(c), GDN_CONV_DIM // wv + h)),
        ba=pl.BlockSpec((GDN_L, LANES), lambda c, h: (cc(c), (GDN_CONV_DIM + GDN_VAL) // LANES)),
        par=par,
        st=pl.BlockSpec((1, GDN_VPB, GDN_DK, GDN_DV), lambda c, h: (cc(c), h, 0, 0)),
        o=pl.BlockSpec((GDN_L, wv), lambda c, h: (cc(c), h)),
        qk_out=pl.BlockSpec((GDN_L, wq), lambda c, h: (cc(c), h)),
        ba_out=pl.BlockSpec((GDN_L, LANES), lambda c, h: (cc(c), 0)),
    )


def _gdn_fwd(proj, act, alog, dtb, nw, name, ride=None):
    t = proj.shape[0]
    nc = t // GDN_L
    sp = _gdn_specs(nc, False)

    def body(q_ref, k_ref, v_ref, z_ref, ba_ref, alog_ref, dtb_ref, nw_ref, o_ref, st_ref, state):
        c, h = pl.program_id(0), pl.program_id(1)

        mine = pl.ds(h * GDN_VPB, GDN_VPB)

        @pl.when(c == 0)
        def _():
            state[mine] = jnp.zeros((GDN_VPB, GDN_DK, GDN_DV), F32)

        prev = state[mine]
        st_ref[0] = prev
        on, new = _gdn_chunk(h, _heads_of(q_ref, GDN_HPB), _heads_of(k_ref, GDN_HPB), _heads_of(v_ref, GDN_VPB),
                             _heads_of(z_ref, GDN_VPB), ba_ref[...], alog_ref[...], dtb_ref[...], nw_ref[...], prev)
        for b in range(GDN_VPB):
            o_ref[:, b * LANES:(b + 1) * LANES] = on[b].astype(o_ref.dtype)
        state[mine] = new

    sds = jax.ShapeDtypeStruct
    return _hosted_call(
        body, name=name, grid=(nc, GDN_HK // GDN_HPB),
        in_specs=[sp["q"], sp["k"], sp["v"], sp["z"], sp["ba"], sp["par"], sp["par"], sp["par"]],
        out_specs=[sp["o"], sp["st"]],
        out_shape=[sds((t, GDN_VAL), F32), sds((nc, GDN_HV, GDN_DK, GDN_DV), F32)],
        scratch_shapes=[pltpu.VMEM((GDN_HV, GDN_DK, GDN_DV), F32)],
        sem=("arbitrary", "arbitrary"), args=(act, act, act, proj, proj, alog, dtb, nw), ride=ride)


def _gdn_bwd(proj, act, alog, dtb, nw, states, don, name, ride=None):
    t = proj.shape[0]
    nc = t // GDN_L
    sp = _gdn_specs(nc, True)

    def body(q_ref, k_ref, v_ref, z_ref, ba_ref, alog_ref, dtb_ref, nw_ref, st_ref, do_ref,
             dq_ref, dk_ref, dv_ref, dz_ref, dba_ref, dalog_ref, ddtb_ref, dnw_ref, dstate):
        c, h = pl.program_id(0), pl.program_id(1)

        mine = pl.ds(h * GDN_VPB, GDN_VPB)

        @pl.when(c == 0)
        def _():
            dstate[mine] = jnp.zeros((GDN_VPB, GDN_DK, GDN_DV), F32)

        @pl.when((c == 0) & (h == 0))
        def _():
            dalog_ref[...] = jnp.zeros_like(dalog_ref)
            ddtb_ref[...] = jnp.zeros_like(ddtb_ref)
            dnw_ref[...] = jnp.zeros_like(dnw_ref)

        @pl.when(h == 0)
        def _():
            dba_ref[...] = jnp.zeros_like(dba_ref)

        _, vjp = jax.vjp(functools.partial(_gdn_chunk, h), _heads_of(q_ref, GDN_HPB), _heads_of(k_ref, GDN_HPB),
                         _heads_of(v_ref, GDN_VPB), _heads_of(z_ref, GDN_VPB), ba_ref[...], alog_ref[...], dtb_ref[...],
                         nw_ref[...], st_ref[0])
        dq, dk, dv, dz, dba, dalog, ddtb, dnw, dprev = vjp((_heads_of(do_ref, GDN_VPB), dstate[mine]))
        for i in range(GDN_HPB):
            dq_ref[:, i * LANES:(i + 1) * LANES] = dq[i]
            dk_ref[:, i * LANES:(i + 1) * LANES] = dk[i]
        for b in range(GDN_VPB):
            dv_ref[:, b * LANES:(b + 1) * LANES] = dv[b]
            dz_ref[:, b * LANES:(b + 1) * LANES] = dz[b]
        dba_ref[...] += dba
        dalog_ref[...] += dalog
        ddtb_ref[...] += ddtb
        dnw_ref[...] += dnw
        dstate[mine] = dprev

    sds = jax.ShapeDtypeStruct
    par_out = pl.BlockSpec((1, LANES), lambda c, h: (0, 0))
    return _hosted_call(
        body, name=name, grid=(nc, GDN_HK // GDN_HPB),
        in_specs=[sp["q"], sp["k"], sp["v"], sp["z"], sp["ba"], sp["par"], sp["par"], sp["par"], sp["st"], sp["o"]],
        out_specs=[sp["qk_out"], sp["qk_out"], sp["o"], sp["o"], sp["ba_out"], par_out, par_out, par_out],
        out_shape=[sds((t, GDN_KEY), F32), sds((t, GDN_KEY), F32), sds((t, GDN_VAL), F32), sds((t, GDN_VAL), F32),
                   sds((t, LANES), F32), sds((1, LANES), F32), sds((1, LANES), F32), sds((1, LANES), F32)],
        scratch_shapes=[pltpu.VMEM((GDN_HV, GDN_DK, GDN_DV), F32)],
        sem=("arbitrary", "arbitrary"), args=(act, act, act, proj, proj, alog, dtb, nw, states, don), ride=ride)


GDN_PROJ = GDN_CONV_DIM + GDN_VAL + 2 * GDN_HV


def _gdn_layer_fwd(u, w, ln_g, ln_b, tag, rides=None, late=None):
    w_in = jnp.concatenate([w["in_w"], jnp.zeros((D_MODEL, GDN_PROJ_PAD - GDN_PROJ), w["in_w"].dtype)], axis=1)
    alog = _pad_lanes(w["a_log"], offset=GDN_A_LANE)
    dtb = _pad_lanes(w["dt_bias"], offset=GDN_A_LANE)
    nw = w["norm_w"].reshape(1, GDN_DV)
    zb = jnp.zeros((1, GDN_CONV_DIM), F32)
    if late is None:
        proj = _mm(u, w_in, name=tag + "_in")
    else:
        proj, got = _mm(u, w_in, name=tag + "_in", ride=(late[0], False))
        w.update(late[1](got))
    act, _ = _conv_fwd(proj, 0, w["conv_w"], zb, name=tag + "_conv")
    rode = {}
    (on, states), rode["delta"] = _gdn_fwd(proj, act, alog, dtb, nw, name=tag + "_delta", ride=_ride_of(rides, "delta", False))
    y = _mm(on, w["out_w"], name=tag + "_out")
    h = _ln_fwd(u, y, ln_g, ln_b, name=tag + "_ln")
    saved = dict(u=u, w_in=w_in, proj=proj, act=act, states=states, on=on, y=y, alog=alog, dtb=dtb, nw=nw, zb=zb)
    return h, saved, rode


def _gdn_layer_bwd(s, w, ln_g, dr_up, du_up, tag, rides=None, own=None):
    t = s["u"].shape[0]
    dr, dg, db = _ln_bwd(s["u"], s["y"], ln_g, dr_up, du_up, name=tag + "_ln_b")
    don = _mm(dr, w["out_w"], tb=True, name=tag + "_out_bx")
    d_out_w = _mm(s["on"], dr, ta=True, out_dtype=WIRE, name=tag + "_out_bw")
    carried = list((rides or {}).get("delta_b", [])) + ([own("out_w", d_out_w)] if own else [])
    rode = {}
    (dq, dk, dv, dz, dba, dalog, ddtb, dnw), rode["delta_b"] = _gdn_bwd(
        s["proj"], s["act"], s["alog"], s["dtb"], s["nw"], s["states"], don, name=tag + "_delta_b",
        ride=(carried, True) if carried else None)
    dact = jnp.concatenate([dq, dk, dv], axis=1)
    dpre, d_conv_w, _ = _conv_bwd_pre(s["proj"], 0, w["conv_w"], s["zb"], dact, name=tag + "_conv_bp")
    dqkv = _conv_bwd_x(dpre, w["conv_w"], name=tag + "_conv_bx")
    dproj = jnp.concatenate([dqkv, dz, dba, jnp.zeros((t, GDN_PROJ_PAD - GDN_PROJ - (LANES - 2 * GDN_HV)), F32)], axis=1)
    d_in = _mm(s["u"], dproj, ta=True, out_dtype=WIRE, name=tag + "_in_bw")
    du = _mm(dproj, s["w_in"], tb=True, name=tag + "_in_bx")
    grads = dict(in_w=d_in[:, :GDN_PROJ], conv_w=d_conv_w, a_log=dalog[0, GDN_A_LANE:GDN_A_LANE + GDN_HV],
                 dt_bias=ddtb[0, GDN_A_LANE:GDN_A_LANE + GDN_HV], norm_w=dnw[0], out_w=d_out_w, ln_g=dg[0], ln_b=db[0])
    return dr, du, grads, rode


def _mesh_pos():
    return lax.axis_index("x"), lax.axis_index("y"), lax.axis_index("c")


def _peer(k, x, y, c):
    return ((1 - x) if k & 4 else x, (1 - y) if k & 2 else y, (1 - c) if k & 1 else c)


def _ride_copies(ins, outs, send, recv, loc, scatter, with_arrivals):
    n = len(ins)
    x, y, c = _mesh_pos()
    me = 4 * x + 2 * y + c
    local = [pltpu.make_async_copy(ins[i].at[me] if scatter else ins[i], outs[i].at[me], loc.at[i]) for i in range(n)]
    sends, arrivals = [], []
    for k in range(1, N_DEV):
        peer = _peer(k, x, y, c)
        pidx = 4 * peer[0] + 2 * peer[1] + peer[2]
        for i in range(n):
            src = ins[i].at[pidx] if scatter else ins[i]
            sems = dict(send_sem=send.at[i, k - 1], recv_sem=recv.at[i, k - 1], device_id=peer,
                        device_id_type=pl.DeviceIdType.MESH)
            sends.append(pltpu.make_async_remote_copy(src_ref=src, dst_ref=outs[i].at[me], **sems))
            if with_arrivals:
                arrivals.append(pltpu.make_async_remote_copy(src_ref=src, dst_ref=outs[i].at[pidx], **sems))
    return local, sends, arrivals


def _ride_start(ins, outs, send, recv, loc, scatter):
    local, sends, _ = _ride_copies(ins, outs, send, recv, loc, scatter, False)
    for cp in local + sends:
        cp.start()


def _ride_wait(ins, outs, send, recv, loc, scatter):
    local, sends, arrivals = _ride_copies(ins, outs, send, recv, loc, scatter, True)
    for cp in arrivals:
        cp.wait_recv()
    for cp in sends:
        cp.wait_send()
    for cp in local:
        cp.wait()


def _ride_shapes(arrs, scatter):
    n = len(arrs)
    out_shape = [jax.ShapeDtypeStruct(a.shape if scatter else (N_DEV,) + a.shape, a.dtype) for a in arrs]
    scratch = [pltpu.SemaphoreType.DMA((n, N_DEV - 1)), pltpu.SemaphoreType.DMA((n, N_DEV - 1)), pltpu.SemaphoreType.DMA((n,))]
    return out_shape, scratch


def _exchange(arrs, scatter, name):
    n = len(arrs)
    hbm = pl.BlockSpec(memory_space=pltpu.HBM)

    def body(*refs):
        ins, outs = refs[:n], refs[n:2 * n]
        _ride_start(ins, outs, *refs[2 * n:], scatter)
        _ride_wait(ins, outs, *refs[2 * n:], scatter)

    out_shape, scratch = _ride_shapes(arrs, scatter)
    return pl.pallas_call(
        body, name=name, in_specs=[hbm] * n, out_specs=[hbm] * n, out_shape=out_shape, scratch_shapes=scratch,
        compiler_params=pltpu.CompilerParams(has_side_effects=True),
    )(*arrs)


def _hosted_call(body, *, name, grid, in_specs, out_specs, out_shape, scratch_shapes, sem, args, ride=None):
    if ride is None:
        return pl.pallas_call(body, name=name, grid=grid, in_specs=in_specs, out_specs=out_specs, out_shape=out_shape,
                              scratch_shapes=scratch_shapes, compiler_params=_cparams(sem))(*args), []
    arrs, scatter = ride
    n, ni, no, ns = len(arrs), len(in_specs), len(out_specs), len(scratch_shapes)
    hbm = pl.BlockSpec(memory_space=pltpu.HBM)
    r_shape, r_scratch = _ride_shapes(arrs, scatter)

    def full(*refs):
        a, ri = refs[:ni], refs[ni:ni + n]
        o, ro = refs[ni + n:ni + n + no], refs[ni + n + no:ni + 2 * n + no]
        s, rs = refs[ni + 2 * n + no:ni + 2 * n + no + ns], refs[ni + 2 * n + no + ns:]
        ids = [pl.program_id(d) for d in range(len(grid))]
        first, last = ids[0] == 0, ids[0] == grid[0] - 1
        for d in range(1, len(grid)):
            first, last = first & (ids[d] == 0), last & (ids[d] == grid[d] - 1)

        @pl.when(first)
        def _():
            _ride_start(ri, ro, *rs, scatter)

        body(*a, *o, *s)

        @pl.when(last)
        def _():
            _ride_wait(ri, ro, *rs, scatter)

    outs = pl.pallas_call(
        full, name=name, grid=grid, in_specs=list(in_specs) + [hbm] * n, out_specs=list(out_specs) + [hbm] * n,
        out_shape=list(out_shape) + r_shape, scratch_shapes=list(scratch_shapes) + r_scratch,
        compiler_params=pltpu.CompilerParams(dimension_semantics=("arbitrary",) * len(grid), vmem_limit_bytes=VMEM_LIMIT,
                                             has_side_effects=True),
    )(*args, *arrs)
    return outs[:no], list(outs[no:])


def _unshard(g, ax):
    g = jnp.moveaxis(g, 0, ax)
    sh = g.shape
    return g.reshape(sh[:ax] + (sh[ax] * sh[ax + 1],) + sh[ax + 2:])


def _to_parts(full, ax):
    sh = full.shape
    full = full.reshape(sh[:ax] + (N_DEV, sh[ax] // N_DEV) + sh[ax + 1:])
    return jnp.moveaxis(full, ax, 0)


def _row_tile(r, c):
    cap = max(8, (256 * 1024) // max(c, 1))
    best = None
    for d in range(8, min(r, cap) + 1, 8):
        if r % d == 0:
            best = d
    return r if best is None else best


def _adamw(w, m, v, parts, name):
    r, c = w.shape
    tr = _row_tile(r, c)

    def body(w_ref, m_ref, v_ref, p_ref, g_ref, d_ref, nm_ref, nv_ref):
        g = p_ref[0].astype(F32)
        for q in range(1, N_DEV):
            g = g + p_ref[q].astype(F32)
        nm = ADAM_B1 * m_ref[...] + (1.0 - ADAM_B1) * g
        nv = ADAM_B2 * v_ref[...] + (1.0 - ADAM_B2) * (g * g)
        m_hat = nm / (1.0 - ADAM_B1 ** ADAM_STEP)
        v_hat = nv / (1.0 - ADAM_B2 ** ADAM_STEP)
        g_ref[...] = g
        d_ref[...] = -ADAM_LR * (m_hat / (jnp.sqrt(v_hat) + ADAM_EPS) + ADAM_WD * w_ref[...])
        nm_ref[...] = nm
        nv_ref[...] = nv

    row = pl.BlockSpec((tr, c), lambda i: (i, 0))
    out = jax.ShapeDtypeStruct((r, c), F32)
    return pl.pallas_call(
        body, name=name, grid=(r // tr,),
        in_specs=[row, row, row, pl.BlockSpec((N_DEV, tr, c), lambda i: (0, i, 0))],
        out_specs=[row] * 4, out_shape=[out] * 4, compiler_params=_cparams(("parallel",)),
    )(w, m, v, parts)


WEIGHTS = ['ssd_in_w', 'ssd_conv_w', 'ssd_conv_b', 'ssd_dt_bias', 'ssd_a_log', 'ssd_d', 'ssd_norm_w', 'ssd_out_w',
           'mla_in_w', 'mla_q_norm_w', 'mla_q_up_w', 'mla_kv_norm_w', 'mla_kv_up_w', 'mla_out_w', 'gdn_in_w',
           'gdn_conv_w', 'gdn_a_log', 'gdn_dt_bias', 'gdn_norm_w', 'gdn_out_w', 'ln_g', 'ln_b']
SHARDED = {'ssd_in_w': (1, True), 'ssd_conv_w': (1, False), 'ssd_conv_b': (0, False), 'ssd_norm_w': (0, False),
           'ssd_out_w': (0, True), 'mla_in_w': (1, True), 'mla_q_up_w': (1, True), 'mla_kv_up_w': (1, True),
           'mla_out_w': (0, True), 'gdn_in_w': (1, True), 'gdn_conv_w': (1, False), 'gdn_out_w': (0, True)}
REPLICATED = [n for n in WEIGHTS if n not in SHARDED]


def _pack_small(vals):
    flat = jnp.concatenate([vals[n].reshape(-1).astype(F32) for n in REPLICATED])
    rows = -(-flat.shape[0] // (8 * LANES)) * 8
    return jnp.pad(flat, (0, rows * LANES - flat.shape[0])).reshape(rows, LANES)


def _unpack_small(slab, like):
    flat = slab.reshape(-1)
    out, off = {}, 0
    for n in REPLICATED:
        sz = like[n].size
        out[n] = flat[off:off + sz].reshape(like[n].shape)
        off += sz
    return out


def kernel(x, positions, ssd_in_w, ssd_conv_w, ssd_conv_b, ssd_dt_bias, ssd_a_log, ssd_d, ssd_norm_w, ssd_out_w, mla_in_w, mla_q_norm_w, mla_q_up_w, mla_kv_norm_w, mla_kv_up_w, mla_out_w, gdn_in_w, gdn_conv_w, gdn_a_log, gdn_dt_bias, gdn_norm_w, gdn_out_w, ln_g, ln_b, loss_target, m_ssd_in_w, m_ssd_conv_w, m_ssd_conv_b, m_ssd_dt_bias, m_ssd_a_log, m_ssd_d, m_ssd_norm_w, m_ssd_out_w, m_mla_in_w, m_mla_q_norm_w, m_mla_q_up_w, m_mla_kv_norm_w, m_mla_kv_up_w, m_mla_out_w, m_gdn_in_w, m_gdn_conv_w, m_gdn_a_log, m_gdn_dt_bias, m_gdn_norm_w, m_gdn_out_w, m_ln_g, m_ln_b, v_ssd_in_w, v_ssd_conv_w, v_ssd_conv_b, v_ssd_dt_bias, v_ssd_a_log, v_ssd_d, v_ssd_norm_w, v_ssd_out_w, v_mla_in_w, v_mla_q_norm_w, v_mla_q_up_w, v_mla_kv_norm_w, v_mla_kv_up_w, v_mla_out_w, v_gdn_in_w, v_gdn_conv_w, v_gdn_a_log, v_gdn_dt_bias, v_gdn_norm_w, v_gdn_out_w, v_ln_g, v_ln_b):
    loc = locals()
    w = {n: loc[n] for n in WEIGHTS}
    m = {n: loc["m_" + n] for n in WEIGHTS}
    v = {n: loc["v_" + n] for n in WEIGHTS}
    xs, pos, tgt = x[0], positions[0], loss_target[0]

    def names_of(prefix):
        return [n for n in SHARDED if n.startswith(prefix + "_")]

    def shard(n, j):
        return w[n][j].astype(WIRE) if SHARDED[n][1] else w[n][j]

    def full(n, gathered):
        return _unshard(gathered, SHARDED[n][0])

    def slots(n, g):
        return _to_parts(g, SHARDED[n][0]).astype(WIRE if SHARDED[n][1] else F32)

    def replicated(prefix, j):
        return {n[len(prefix) + 1:]: w[n][j] for n in REPLICATED if n.startswith(prefix + "_")}

    def late_weights(prefix, j, keys):
        return [shard(prefix + "_" + k, j) for k in keys], lambda got: {k: full(prefix + "_" + k, g) for k, g in zip(keys, got)}

    lg = lambda i: w["ln_g"][i].reshape(1, D_MODEL)
    lb = lambda i: w["ln_b"][i].reshape(1, D_MODEL)
    cosf, sinf = _rope_tables(pos)
    ssd_late = ("conv_w", "conv_b", "norm_w", "out_w")

    w_s0 = dict(replicated("ssd", 0), in_w=full("ssd_in_w", _exchange([shard("ssd_in_w", 0)], False, name="gather_ssd0")[0]))
    h1, s0, got = _ssd_layer_fwd(
        xs, w_s0, lg(0), lb(0), "ssd0", late=late_weights("ssd", 0, ssd_late),
        rides={"conv": [shard("mla_q_up_w", 0), shard("mla_kv_up_w", 0)], "scan": [shard("mla_in_w", 0)],
               "out": [shard("mla_out_w", 0)]})
    w_m0 = dict(replicated("mla", 0), q_up_w=full("mla_q_up_w", got["conv"][0]), kv_up_w=full("mla_kv_up_w", got["conv"][1]),
                in_w=full("mla_in_w", got["scan"][0]), out_w=full("mla_out_w", got["out"][0]))
    h2, s1, got = _mla_layer_fwd(h1, w_m0, lg(1), lb(1), cosf, sinf, "mla0", rides={"attn": [shard("gdn_in_w", 0)]})
    w_g0 = dict(replicated("gdn", 0), in_w=full("gdn_in_w", got["attn"][0]))
    h3, s2, got = _gdn_layer_fwd(h2, w_g0, lg(2), lb(2), "gdn0", late=late_weights("gdn", 0, ("conv_w", "out_w")),
                                 rides={"delta": [shard("ssd_in_w", 1)]})
    w_s1 = dict(replicated("ssd", 1), in_w=full("ssd_in_w", got["delta"][0]))
    h4, s3, _ = _ssd_layer_fwd(h3, w_s1, lg(3), lb(3), "ssd1", late=late_weights("ssd", 1, ssd_late))
    loss_tile, dl = _loss_head(h4, tgt, name="loss_head")

    dr3, du3, g3, _, _ = _ssd_layer_bwd(s3, w_s1, lg(3), jnp.zeros_like(dl), dl, "ssd1")
    dr2, du2, g2, got = _gdn_layer_bwd(s2, w_g0, lg(2), dr3, du3, "gdn0", own=lambda k, g: slots("gdn_" + k, g),
                                       rides={"delta_b": [slots(n, g3[n[4:]]) for n in names_of("ssd")]})
    r3, r2_out = got["delta_b"][:-1], got["delta_b"][-1]
    gin = slots("gdn_in_w", g2["in_w"])
    half = gin.shape[1] // 2
    dr1, du1, g1, got = _mla_layer_bwd(
        s1, w_m0, lg(1), cosf, sinf, dr2, du2, "mla0",
        rides={"attn_bq": [gin[:, :half]], "attn_bkv": [gin[:, half:], slots("gdn_conv_w", g2["conv_w"])]})
    r2 = {"gdn_in_w": jnp.concatenate([got["attn_bq"][0], got["attn_bkv"][0]], axis=1), "gdn_conv_w": got["attn_bkv"][1],
          "gdn_out_w": r2_out}
    dr0, du0, g0, got, r0 = _ssd_layer_bwd(s0, w_s0, lg(0), dr1, du1, "ssd0", own=lambda k, g: slots("ssd_" + k, g),
                                           rides={"scan_b": [slots(n, g1[n[4:]]) for n in names_of("mla")]})
    r1 = got["scan_b"]
    grad_x = _axpy(dr0, du0, name="grad_x")[None]

    gsmall = {"ssd_" + k: jnp.stack([g0[k], g3[k]]) for k in ("dt_bias", "a_log", "d")}
    gsmall.update({"mla_" + k: g1[k][None] for k in ("q_norm_w", "kv_norm_w")})
    gsmall.update({"gdn_" + k: g2[k][None] for k in ("a_log", "dt_bias", "norm_w")})
    gsmall["ln_g"] = jnp.stack([g0["ln_g"], g1["ln_g"], g2["ln_g"], g3["ln_g"]])
    gsmall["ln_b"] = jnp.stack([g0["ln_b"], g1["ln_b"], g2["ln_b"], g3["ln_b"]])
    small = _pack_small(gsmall)
    rsmall, = _exchange([jnp.broadcast_to(small[None], (N_DEV,) + small.shape)], True, name="gather_small_grads")

    recvd = {n: jnp.stack([r0[n[4:]], b], axis=1) for n, b in zip(names_of("ssd"), r3)}
    recvd.update({n: a[:, None] for n, a in zip(names_of("mla"), r1)})
    recvd.update({n: a[:, None] for n, a in r2.items()})
    recvd = [recvd[n] for n in SHARDED] + [rsmall]

    grads, deltas, new_m, new_v = {}, {}, {}, {}
    for n, pt in zip(SHARDED, recvd[:-1]):
        shp = w[n].shape
        r2d = (-1, shp[-1])
        outs = _adamw(w[n].reshape(r2d), m[n].reshape(r2d), v[n].reshape(r2d), pt.reshape((N_DEV,) + w[n].reshape(r2d).shape),
                      name="adamw_" + n)
        grads[n], deltas[n], new_m[n], new_v[n] = (o.reshape(shp) for o in outs)
    outs = _adamw(_pack_small(w), _pack_small(m), _pack_small(v), recvd[-1], name="adamw_replicated")
    for dst, o in zip((grads, deltas, new_m, new_v), outs):
        dst.update(_unpack_small(o, w))

    loss = lax.psum(loss_tile[0, 0], ("x", "y", "c"))
    return (loss, grad_x, *[grads[n] for n in WEIGHTS], *[deltas[n] for n in WEIGHTS],
            *[new_m[n] for n in WEIGHTS], *[new_v[n] for n in WEIGHTS])
```

```python
import functools

import jax
import jax.numpy as jnp
from jax import lax
from jax.experimental import pallas as pl
from jax.experimental.pallas import tpu as pltpu

F32 = jnp.float32
MXU = jnp.bfloat16
WIRE = jnp.bfloat16
HI = lax.Precision.HIGHEST

N_DEV = 8
LANES = 128
VMEM_LIMIT = 56 * 1024 * 1024

D_MODEL = 2048
DEPTH = 4
ALPHA = (2.0 * DEPTH) ** 0.25
LN_EPS = 1e-5
RMS_EPS = 1e-6

SSD_DI = 4096
SSD_P = 64
SSD_H = 64
SSD_G = 8
SSD_N = 128
SSD_L = 128
SSD_GS = SSD_DI // SSD_G
SSD_CONV_DIM = SSD_DI + 2 * SSD_G * SSD_N
SSD_PROJ_PAD = SSD_DI + SSD_CONV_DIM + LANES

MLA_H = 16
MLA_QR = 768
MLA_KVR = 512
MLA_NOPE = 128
MLA_ROPE = 64
MLA_V = 128
MLA_GATE = MLA_H * MLA_V
MLA_PROJ_PAD = MLA_QR + MLA_KVR + LANES + MLA_GATE
MLA_SCALE = (MLA_NOPE + MLA_ROPE) ** -0.5
ROPE_THETA = 10000.0
ATT_BLK = 512

GDN_HK = 16
GDN_HV = 32
GDN_DK = 128
GDN_DV = 128
GDN_KEY = GDN_HK * GDN_DK
GDN_VAL = GDN_HV * GDN_DV
GDN_L = 128
GDN_CONV_DIM = 2 * GDN_KEY + GDN_VAL
GDN_PROJ_PAD = 12800

ADAM_LR = 0.001
ADAM_B1 = 0.9
ADAM_B2 = 0.999
ADAM_EPS = 1e-08
ADAM_WD = 0.01
ADAM_STEP = 10


def _cparams(sem=None):
    return pltpu.CompilerParams(dimension_semantics=sem, vmem_limit_bytes=VMEM_LIMIT)


def _tile(n, cap):
    if n <= cap:
        return n
    best = None
    for d in range(LANES, cap + 1, LANES):
        if n % d == 0:
            best = d
    assert best is not None, (n, cap)
    return best


def _dg(a, b, ca, cb, prec=None):
    return lax.dot_general(a, b, (((ca,), (cb,)), ((), ())), preferred_element_type=F32, precision=prec)


def _mx(a):
    return a.astype(MXU)


@jax.custom_vjp
def _nn(a, b):
    return _dg(_mx(a), _mx(b), 1, 0)


def _nn_f(a, b):
    return _nn(a, b), (a, b)


def _nn_b(res, ct):
    a, b = res
    return _dg(_mx(ct), _mx(b), 1, 1), _dg(_mx(a), _mx(ct), 0, 0)


_nn.defvjp(_nn_f, _nn_b)


@jax.custom_vjp
def _nt(a, b):
    return _dg(_mx(a), _mx(b), 1, 1)


def _nt_f(a, b):
    return _nt(a, b), (a, b)


def _nt_b(res, ct):
    a, b = res
    return _dg(_mx(ct), _mx(b), 1, 0), _dg(_mx(ct), _mx(a), 0, 0)


_nt.defvjp(_nt_f, _nt_b)


@jax.custom_vjp
def _tn(a, b):
    return _dg(_mx(a), _mx(b), 0, 0)


def _tn_f(a, b):
    return _tn(a, b), (a, b)


def _tn_b(res, ct):
    a, b = res
    return _dg(_mx(b), _mx(ct), 1, 1), _dg(_mx(a), _mx(ct), 1, 0)


_tn.defvjp(_tn_f, _tn_b)


def _softplus(x):
    return jnp.maximum(x, 0.0) + jnp.log(1.0 + jnp.exp(-jnp.abs(x)))


def _silu(x):
    return x * jax.nn.sigmoid(x)


MM_TM = 1024
MM_TN = 1280
MM_VMEM_BUDGET = 40 * 1024 * 1024


def _mm(a, b, *, ta=False, tb=False, out_dtype=F32, name, ride=None):
    if ta:
        kdim, m = a.shape
    else:
        m, kdim = a.shape
    if tb:
        n, kb = b.shape
    else:
        kb, n = b.shape
    assert kdim == kb, (a.shape, b.shape, ta, tb)
    tm, tn = _tile(m, MM_TM), _tile(n, MM_TN)
    abytes, bbytes, obytes = a.dtype.itemsize, b.dtype.itemsize, jnp.dtype(out_dtype).itemsize
    tk = LANES
    for d in range(LANES, kdim + 1, LANES):
        if kdim % d == 0 and 2 * d * (tm * abytes + tn * bbytes) + tm * tn * (2 * obytes + 4) <= MM_VMEM_BUDGET:
            tk = d
    nk = kdim // tk
    ca, cb = (0 if ta else 1), (1 if tb else 0)

    def body(a_ref, b_ref, o_ref, *acc):
        part = _dg(_mx(a_ref[...]), _mx(b_ref[...]), ca, cb)
        if nk == 1:
            o_ref[...] = part.astype(out_dtype)
            return
        acc_ref, = acc
        k = pl.program_id(2)

        @pl.when(k == 0)
        def _():
            acc_ref[...] = part

        @pl.when(k > 0)
        def _():
            acc_ref[...] += part

        @pl.when(k == nk - 1)
        def _():
            o_ref[...] = acc_ref[...].astype(out_dtype)

    a_spec = pl.BlockSpec((tk, tm), lambda i, j, k: (k, i)) if ta else pl.BlockSpec((tm, tk), lambda i, j, k: (i, k))
    b_spec = pl.BlockSpec((tn, tk), lambda i, j, k: (j, k)) if tb else pl.BlockSpec((tk, tn), lambda i, j, k: (k, j))
    (out,), rode = _hosted_call(
        body, name=name, grid=(m // tm, n // tn, nk),
        in_specs=[a_spec, b_spec], out_specs=[pl.BlockSpec((tm, tn), lambda i, j, k: (i, j))],
        out_shape=[jax.ShapeDtypeStruct((m, n), out_dtype)],
        scratch_shapes=[pltpu.VMEM((tm, tn), F32)] if nk > 1 else [],
        sem=("parallel", "parallel", "arbitrary"), args=(a, b), ride=ride)
    return out if ride is None else (out, rode)


def _ln_fwd(h, y, g, b, name):
    t, d = h.shape
    tr = _tile(t, 256)

    def body(h_ref, y_ref, g_ref, b_ref, o_ref):
        r = ALPHA * h_ref[...] + y_ref[...]
        mu = jnp.mean(r, -1, keepdims=True)
        xc = r - mu
        var = jnp.mean(xc * xc, -1, keepdims=True)
        o_ref[...] = xc * lax.rsqrt(var + LN_EPS) * g_ref[...] + b_ref[...]

    row = pl.BlockSpec((tr, d), lambda i: (i, 0))
    par = pl.BlockSpec((1, d), lambda i: (0, 0))
    return pl.pallas_call(
        body, name=name, grid=(t // tr,), in_specs=[row, row, par, par], out_specs=row,
        out_shape=jax.ShapeDtypeStruct((t, d), F32), compiler_params=_cparams(("parallel",)),
    )(h, y, g, b)


def _ln_bwd(h, y, g, dr_up, du_up, name):
    t, d = h.shape
    tr = _tile(t, 256)

    def body(h_ref, y_ref, g_ref, dr_ref, du_ref, o_ref, dg_ref, db_ref):
        i = pl.program_id(0)

        @pl.when(i == 0)
        def _():
            dg_ref[...] = jnp.zeros_like(dg_ref)
            db_ref[...] = jnp.zeros_like(db_ref)

        dout = ALPHA * dr_ref[...] + du_ref[...]
        r = ALPHA * h_ref[...] + y_ref[...]
        mu = jnp.mean(r, -1, keepdims=True)
        xc = r - mu
        rstd = lax.rsqrt(jnp.mean(xc * xc, -1, keepdims=True) + LN_EPS)
        xh = xc * rstd
        dxh = dout * g_ref[...]
        o_ref[...] = rstd * (dxh - jnp.mean(dxh, -1, keepdims=True) - xh * jnp.mean(dxh * xh, -1, keepdims=True))
        dg_ref[...] += jnp.sum(dout * xh, 0, keepdims=True)
        db_ref[...] += jnp.sum(dout, 0, keepdims=True)

    row = pl.BlockSpec((tr, d), lambda i: (i, 0))
    par = pl.BlockSpec((1, d), lambda i: (0, 0))
    return pl.pallas_call(
        body, name=name, grid=(t // tr,), in_specs=[row, row, par, row, row], out_specs=[row, par, par],
        out_shape=[jax.ShapeDtypeStruct((t, d), F32), jax.ShapeDtypeStruct((1, d), F32), jax.ShapeDtypeStruct((1, d), F32)],
        compiler_params=_cparams(("arbitrary",)),
    )(h, y, g, dr_up, du_up)


def _loss_head(h, tgt, name):
    t, d = h.shape
    tr = _tile(t, 256)

    def body(h_ref, t_ref, l_ref, d_ref):
        i = pl.program_id(0)

        @pl.when(i == 0)
        def _():
            l_ref[...] = jnp.zeros_like(l_ref)

        e = h_ref[...] - t_ref[...]
        d_ref[...] = e * (1.0 / d)
        l_ref[...] += 0.5 * jnp.sum(jnp.mean(e * e, -1, keepdims=True))

    row = pl.BlockSpec((tr, d), lambda i: (i, 0))
    return pl.pallas_call(
        body, name=name, grid=(t // tr,), in_specs=[row, row],
        out_specs=[pl.BlockSpec((8, LANES), lambda i: (0, 0)), row],
        out_shape=[jax.ShapeDtypeStruct((8, LANES), F32), jax.ShapeDtypeStruct((t, d), F32)],
        compiler_params=_cparams(("arbitrary",)),
    )(h, tgt)


def _axpy(dr, du, name):
    t, d = dr.shape
    tr = _tile(t, 256)

    def body(a_ref, b_ref, o_ref):
        o_ref[...] = ALPHA * a_ref[...] + b_ref[...]

    row = pl.BlockSpec((tr, d), lambda i: (i, 0))
    return pl.pallas_call(
        body, name=name, grid=(t // tr,), in_specs=[row, row], out_specs=row,
        out_shape=jax.ShapeDtypeStruct((t, d), F32), compiler_params=_cparams(("parallel",)),
    )(dr, du)


CONV_TT = 512
CONV_TC = 512


HALO = 8


def _shift_down(cur, halo, s, row):
    if s == 0:
        return cur
    tt = cur.shape[0]
    edge = pltpu.roll(halo, s, 0)
    if tt > HALO:
        edge = jnp.concatenate([edge, jnp.zeros((tt - HALO, cur.shape[1]), cur.dtype)], axis=0)
    return jnp.where(row >= s, pltpu.roll(cur, s, 0), edge)


def _shift_up(cur, halo, s, row, tt):
    if s == 0:
        return cur
    edge = jnp.concatenate([jnp.zeros((tt - HALO, cur.shape[1]), cur.dtype), pltpu.roll(halo, HALO - s, 0)], axis=0)
    return jnp.where(row < tt - s, pltpu.roll(cur, tt - s, 0), edge)


def _conv_fwd(proj, col0, w, b, name, ride=None):
    t = proj.shape[0]
    c = w.shape[1]
    tt = _tile(t, CONV_TT)
    cb0 = col0 // CONV_TC

    def body(x_ref, p_ref, w_ref, b_ref, o_ref):
        i = pl.program_id(1)
        x = x_ref[...]
        p = jnp.where(i > 0, p_ref[...], 0.0)
        row = lax.broadcasted_iota(jnp.int32, x.shape, 0)
        pre = b_ref[...] + w_ref[3:4, :] * x
        for s in (1, 2, 3):
            pre = pre + w_ref[3 - s:4 - s, :] * _shift_down(x, p, s, row)
        o_ref[...] = _silu(pre)

    (act,), rode = _hosted_call(
        body, name=name, grid=(c // CONV_TC, t // tt),
        in_specs=[pl.BlockSpec((tt, CONV_TC), lambda j, i: (i, cb0 + j)),
                  pl.BlockSpec((HALO, CONV_TC), lambda j, i: (jnp.maximum(i * (tt // HALO) - 1, 0), cb0 + j)),
                  pl.BlockSpec((4, CONV_TC), lambda j, i: (0, j)),
                  pl.BlockSpec((1, CONV_TC), lambda j, i: (0, j))],
        out_specs=[pl.BlockSpec((tt, CONV_TC), lambda j, i: (i, j))],
        out_shape=[jax.ShapeDtypeStruct((t, c), F32)], scratch_shapes=[],
        sem=("parallel", "parallel"), args=(proj, proj, w, b), ride=ride)
    return act, rode


def _conv_bwd(proj, col0, w, b, dact, name):
    t = proj.shape[0]
    c = w.shape[1]
    tt = _tile(t, CONV_TT)
    nt = t // tt
    cb0 = col0 // CONV_TC
    per = tt // HALO

    def dpre_of(x, halo, d, w_ref, b_ref):
        row = lax.broadcasted_iota(jnp.int32, x.shape, 0)
        sh = [_shift_down(x, halo, s, row) for s in range(4)]
        pre = b_ref[...] + w_ref[3:4, :] * sh[0]
        for s in (1, 2, 3):
            pre = pre + w_ref[3 - s:4 - s, :] * sh[s]
        sg = jax.nn.sigmoid(pre)
        return d * (sg * (1.0 + pre * (1.0 - sg))), sh

    def body(x_ref, p_ref, n_ref, w_ref, b_ref, d_ref, dn_ref, dx_ref, dw_ref, db_ref):
        i = pl.program_id(1)

        @pl.when(i == 0)
        def _():
            dw_ref[...] = jnp.zeros_like(dw_ref)
            db_ref[...] = jnp.zeros_like(db_ref)

        x = x_ref[...]
        dpre, sh = dpre_of(x, jnp.where(i > 0, p_ref[...], 0.0), d_ref[...], w_ref, b_ref)
        dnext, _ = dpre_of(n_ref[...], x[tt - HALO:, :], jnp.where(i < nt - 1, dn_ref[...], 0.0), w_ref, b_ref)
        row = lax.broadcasted_iota(jnp.int32, x.shape, 0)
        acc = w_ref[3:4, :] * dpre
        for s in (1, 2, 3):
            acc = acc + w_ref[3 - s:4 - s, :] * _shift_up(dpre, dnext, s, row, tt)
        dx_ref[...] = acc
        for s in range(4):
            dw_ref[3 - s:4 - s, :] += jnp.sum(dpre * sh[s], 0, keepdims=True)
        db_ref[...] += jnp.sum(dpre, 0, keepdims=True)

    tile = lambda off: pl.BlockSpec((tt, CONV_TC), lambda j, i: (i, off + j))
    before = lambda off: pl.BlockSpec((HALO, CONV_TC), lambda j, i: (jnp.maximum(i * per - 1, 0), off + j))
    after = lambda off: pl.BlockSpec((HALO, CONV_TC), lambda j, i: (jnp.minimum((i + 1) * per, t // HALO - 1), off + j))
    wspec = lambda rows: pl.BlockSpec((rows, CONV_TC), lambda j, i: (0, j))
    return pl.pallas_call(
        body, name=name, grid=(c // CONV_TC, nt),
        in_specs=[tile(cb0), before(cb0), after(cb0), wspec(4), wspec(1), tile(0), after(0)],
        out_specs=[tile(0), wspec(4), wspec(1)],
        out_shape=[jax.ShapeDtypeStruct((t, c), F32), jax.ShapeDtypeStruct((4, c), F32), jax.ShapeDtypeStruct((1, c), F32)],
        compiler_params=_cparams(("parallel", "arbitrary")),
    )(proj, proj, proj, w, b, dact, dact)


SSD_GB = 4


def _split3(a):
    a1 = _mx(a)
    r = a - a1.astype(F32)
    a2 = _mx(r)
    return a1, a2, _mx(r - a2.astype(F32))


@jax.custom_vjp
def _sel_r(a, c):
    cm = _mx(c)
    p1, p2, p3 = _split3(a)
    return _dg(p1, cm, 1, 0) + (_dg(p2, cm, 1, 0) + _dg(p3, cm, 1, 0))


def _sel_r_f(a, c):
    return _sel_r(a, c), c


def _sel_r_b(c, ct):
    cm = _mx(c)
    p1, p2, p3 = _split3(ct)
    return _dg(p1, cm, 1, 1) + (_dg(p2, cm, 1, 1) + _dg(p3, cm, 1, 1)), jnp.zeros_like(c)


_sel_r.defvjp(_sel_r_f, _sel_r_b)


@jax.custom_vjp
def _sel_l(c, a):
    cm = _mx(c)
    p1, p2, p3 = _split3(a)
    return _dg(cm, p1, 1, 0) + (_dg(cm, p2, 1, 0) + _dg(cm, p3, 1, 0))


def _sel_l_f(c, a):
    return _sel_l(c, a), c


def _sel_l_b(c, ct):
    cm = _mx(c)
    p1, p2, p3 = _split3(ct)
    return jnp.zeros_like(c), _dg(cm, p1, 0, 0) + (_dg(cm, p2, 0, 0) + _dg(cm, p3, 0, 0))


_sel_l.defvjp(_sel_l_f, _sel_l_b)


def _ssd_chunk(gb, x, z, bm, cm, dtraw, dtb, alog, dsk, nw, prev):
    L = x.shape[1]
    r_i = lax.broadcasted_iota(jnp.int32, (L, L), 0)
    c_i = lax.broadcasted_iota(jnp.int32, (L, L), 1)
    causal = r_i >= c_i
    dt = _softplus(dtraw + dtb)
    a = dt * (-jnp.exp(alog))
    acs = _sel_l(causal.astype(F32), a)
    e_r = lax.broadcasted_iota(jnp.int32, (LANES, SSD_GS), 0)
    e_c = lax.broadcasted_iota(jnp.int32, (LANES, SSD_GS), 1)
    hpg = SSD_H // SSD_G
    sels = [(e_r == (gb * SSD_GB + i) * hpg + jnp.right_shift(e_c, 6)).astype(F32) for i in range(SSD_GB)]
    dt_x = _cat0([_sel_r(dt, s) for s in sels])
    acs_x = _cat0([_sel_r(acs, s) for s in sels])
    d_x = _cat0([_sel_r(jnp.broadcast_to(dsk, (8, LANES)), s)[0:1] for s in sels])
    last = lax.broadcasted_iota(jnp.int32, (L, 1), 0) == L - 1
    alast = jnp.sum(jnp.where(last, acs_x, 0.0), axis=1, keepdims=True)
    xdt = x * dt_x
    cb = _bnt(cm, bm)
    lane = lax.broadcasted_iota(jnp.int32, (L, LANES), 1)
    ys = []
    for j in range(SSD_GS // LANES):
        xp = xdt[:, :, j * LANES:(j + 1) * LANES]
        yp = None
        for hh in range(2):
            c0 = (2 * j + hh) * SSD_P
            cmx = jnp.broadcast_to(acs_x[:, :, c0:c0 + 1], (SSD_GB, L, L))
            dec = jnp.exp(jnp.where(causal, cmx - jnp.swapaxes(cmx, 1, 2), -jnp.inf))
            half = (lane < SSD_P) if hh == 0 else (lane >= SSD_P)
            t = _bnn(cb * dec, jnp.where(half, xp, 0.0))
            yp = t if yp is None else yp + t
        ys.append(yp)
    y_diag = jnp.concatenate(ys, axis=2)
    st = _btn(bm, xdt * jnp.exp(alast - acs_x))
    new = prev * jnp.exp(alast) + st
    y_off = _bnn(cm, prev) * jnp.exp(acs_x)
    y = y_diag + y_off + x * d_x
    yg = y * _silu(z)
    yn = yg * lax.rsqrt(jnp.mean(yg * yg, -1, keepdims=True) + RMS_EPS) * nw
    return yn, new


def _groups_of(ref, width):
    return _cat0([ref[:, i * width:(i + 1) * width] for i in range(SSD_GB)])


def _put_groups(ref, val, width):
    for i in range(SSD_GB):
        ref[:, i * width:(i + 1) * width] = val[i].astype(ref.dtype)


def _ssd_specs(nc, rev):
    cc = (lambda c: nc - 1 - c) if rev else (lambda c: c)
    wx, wb = SSD_GB * SSD_GS, SSD_GB * SSD_N
    dtb = (SSD_DI + SSD_CONV_DIM) // LANES
    bb = SSD_DI // wb
    cbk = (SSD_DI + SSD_G * SSD_N) // wb
    par = pl.BlockSpec((1, LANES), lambda c, g: (0, 0))
    return dict(
        z=pl.BlockSpec((SSD_L, wx), lambda c, g: (cc(c), g)),
        dt=pl.BlockSpec((SSD_L, LANES), lambda c, g: (cc(c), dtb)),
        x=pl.BlockSpec((SSD_L, wx), lambda c, g: (cc(c), g)),
        bm=pl.BlockSpec((SSD_L, wb), lambda c, g: (cc(c), bb + g)),
        cm=pl.BlockSpec((SSD_L, wb), lambda c, g: (cc(c), cbk + g)),
        par=par,
        nw=pl.BlockSpec((1, wx), lambda c, g: (0, g)),
        st=pl.BlockSpec((1, SSD_GB, SSD_N, SSD_GS), lambda c, g: (cc(c), g, 0, 0)),
        y=pl.BlockSpec((SSD_L, wx), lambda c, g: (cc(c), g)),
        bc=pl.BlockSpec((SSD_L, wb), lambda c, g: (cc(c), g)),
        dtout=pl.BlockSpec((SSD_L, LANES), lambda c, g: (cc(c), 0)),
    )


def _ssd_fwd(proj, act, dtb, alog, dsk, nw, name, ride=None):
    t = proj.shape[0]
    nc = t // SSD_L
    sp = _ssd_specs(nc, False)

    def body(z_ref, dt_ref, x_ref, bm_ref, cm_ref, dtb_ref, alog_ref, dsk_ref, nw_ref, y_ref, st_ref, state):
        c, g = pl.program_id(0), pl.program_id(1)
        mine = pl.ds(g * SSD_GB, SSD_GB)

        @pl.when(c == 0)
        def _():
            state[mine] = jnp.zeros((SSD_GB, SSD_N, SSD_GS), F32)

        prev = state[mine]
        st_ref[0] = prev
        yn, new = _ssd_chunk(g, _groups_of(x_ref, SSD_GS), _groups_of(z_ref, SSD_GS), _groups_of(bm_ref, SSD_N),
                             _groups_of(cm_ref, SSD_N), dt_ref[...], dtb_ref[...], alog_ref[...], dsk_ref[...],
                             _groups_of(nw_ref, SSD_GS), prev)
        _put_groups(y_ref, yn, SSD_GS)
        state[mine] = new

    return _hosted_call(
        body, name=name, grid=(nc, SSD_G // SSD_GB),
        in_specs=[sp["z"], sp["dt"], sp["x"], sp["bm"], sp["cm"], sp["par"], sp["par"], sp["par"], sp["nw"]],
        out_specs=[sp["y"], sp["st"]],
        out_shape=[jax.ShapeDtypeStruct((t, SSD_DI), F32), jax.ShapeDtypeStruct((nc, SSD_G, SSD_N, SSD_GS), F32)],
        scratch_shapes=[pltpu.VMEM((SSD_G, SSD_N, SSD_GS), F32)],
        sem=("arbitrary", "arbitrary"), args=(proj, proj, act, act, act, dtb, alog, dsk, nw), ride=ride)


def _ssd_bwd(proj, act, dtb, alog, dsk, nw, states, dyn, name, ride=None):
    t = proj.shape[0]
    nc = t // SSD_L
    sp = _ssd_specs(nc, True)

    def body(z_ref, dt_ref, x_ref, bm_ref, cm_ref, dtb_ref, alog_ref, dsk_ref, nw_ref, st_ref, dy_ref,
             dx_ref, dz_ref, dbm_ref, dcm_ref, ddt_ref, ddtb_ref, dalog_ref, ddsk_ref, dnw_ref, dstate):
        c, g = pl.program_id(0), pl.program_id(1)
        mine = pl.ds(g * SSD_GB, SSD_GB)

        @pl.when(c == 0)
        def _():
            dstate[mine] = jnp.zeros((SSD_GB, SSD_N, SSD_GS), F32)

        @pl.when((c == 0) & (g == 0))
        def _():
            ddtb_ref[...] = jnp.zeros_like(ddtb_ref)
            dalog_ref[...] = jnp.zeros_like(dalog_ref)
            ddsk_ref[...] = jnp.zeros_like(ddsk_ref)
            dnw_ref[...] = jnp.zeros_like(dnw_ref)

        @pl.when(g == 0)
        def _():
            ddt_ref[...] = jnp.zeros_like(ddt_ref)

        _, vjp = jax.vjp(functools.partial(_ssd_chunk, g), _groups_of(x_ref, SSD_GS), _groups_of(z_ref, SSD_GS),
                         _groups_of(bm_ref, SSD_N), _groups_of(cm_ref, SSD_N), dt_ref[...], dtb_ref[...], alog_ref[...],
                         dsk_ref[...], _groups_of(nw_ref, SSD_GS), st_ref[0])
        dx, dz, dbm, dcm, ddt, ddtb, dalog, ddsk, dnw, dprev = vjp((_groups_of(dy_ref, SSD_GS), dstate[mine]))
        _put_groups(dx_ref, dx, SSD_GS)
        _put_groups(dz_ref, dz, SSD_GS)
        _put_groups(dbm_ref, dbm, SSD_N)
        _put_groups(dcm_ref, dcm, SSD_N)
        ddt_ref[...] += ddt
        ddtb_ref[...] += ddtb
        dalog_ref[...] += dalog
        ddsk_ref[...] += ddsk
        dnw_ref[mine] += dnw
        dstate[mine] = dprev

    par_out = pl.BlockSpec((1, LANES), lambda c, g: (0, 0))
    sds = jax.ShapeDtypeStruct
    return _hosted_call(
        body, name=name, grid=(nc, SSD_G // SSD_GB),
        in_specs=[sp["z"], sp["dt"], sp["x"], sp["bm"], sp["cm"], sp["par"], sp["par"], sp["par"], sp["nw"],
                  sp["st"], sp["y"]],
        out_specs=[sp["y"], sp["y"], sp["bc"], sp["bc"], sp["dtout"], par_out, par_out, par_out,
                   pl.BlockSpec((SSD_G, 1, SSD_GS), lambda c, g: (0, 0, 0))],
        out_shape=[sds((t, SSD_DI), F32), sds((t, SSD_DI), F32), sds((t, SSD_G * SSD_N), F32), sds((t, SSD_G * SSD_N), F32),
                   sds((t, LANES), F32), sds((1, LANES), F32), sds((1, LANES), F32), sds((1, LANES), F32),
                   sds((SSD_G, 1, SSD_GS), F32)],
        scratch_shapes=[pltpu.VMEM((SSD_G, SSD_N, SSD_GS), F32)],
        sem=("arbitrary", "arbitrary"), args=(proj, proj, act, act, act, dtb, alog, dsk, nw, states, dyn), ride=ride)


def _pad_lanes(v, width=LANES, offset=0):
    return jnp.pad(v.astype(F32), (offset, width - offset - v.shape[0])).reshape(1, width)


def _ride_of(rides, key, scatter):
    arrs = (rides or {}).get(key)
    return (arrs, scatter) if arrs else None


def _mm_r(a, b, ride, **kw):
    out = _mm(a, b, ride=ride, **kw)
    return (out, []) if ride is None else out


def _ssd_layer_fwd(u, w, ln_g, ln_b, tag, rides=None, late=None):
    w_in = jnp.concatenate([w["in_w"], jnp.zeros((D_MODEL, LANES - SSD_H), w["in_w"].dtype)], axis=1)
    dtb, alog, dsk = _pad_lanes(w["dt_bias"]), _pad_lanes(w["a_log"]), _pad_lanes(w["d"])
    if late is None:
        proj = _mm(u, w_in, name=tag + "_in")
    else:
        proj, got = _mm(u, w_in, name=tag + "_in", ride=(late[0], False))
        w.update(late[1](got))
    nw = w["norm_w"].reshape(1, SSD_DI)
    cb = w["conv_b"].reshape(1, SSD_CONV_DIM)
    rode = {}
    act, rode["conv"] = _conv_fwd(proj, SSD_DI, w["conv_w"], cb, name=tag + "_conv", ride=_ride_of(rides, "conv", False))
    (yn, states), rode["scan"] = _ssd_fwd(proj, act, dtb, alog, dsk, nw, name=tag + "_scan", ride=_ride_of(rides, "scan", False))
    y, rode["out"] = _mm_r(yn, w["out_w"], _ride_of(rides, "out", False), name=tag + "_out")
    h = _ln_fwd(u, y, ln_g, ln_b, name=tag + "_ln")
    saved = dict(u=u, w_in=w_in, proj=proj, act=act, states=states, yn=yn, y=y, dtb=dtb, alog=alog, dsk=dsk, nw=nw, cb=cb)
    return h, saved, rode


def _ssd_layer_bwd(s, w, ln_g, dr_up, du_up, tag, rides=None, own=None):
    dr, dg, db = _ln_bwd(s["u"], s["y"], ln_g, dr_up, du_up, name=tag + "_ln_b")
    dyn = _mm(dr, w["out_w"], tb=True, name=tag + "_out_bx")
    d_out_w = _mm(s["yn"], dr, ta=True, out_dtype=WIRE, name=tag + "_out_bw")
    rode = {}
    (dx, dz, dbm, dcm, ddt, ddtb, dalog, ddsk, dnw), rode["scan_b"] = _ssd_bwd(
        s["proj"], s["act"], s["dtb"], s["alog"], s["dsk"], s["nw"], s["states"], dyn, name=tag + "_scan_b",
        ride=_ride_of(rides, "scan_b", True))
    dact = jnp.concatenate([dx, dbm, dcm], axis=1)
    dxbc, d_conv_w, d_conv_b = _conv_bwd(s["proj"], SSD_DI, w["conv_w"], s["cb"], dact, name=tag + "_conv_b")
    dproj = jnp.concatenate([dz, dxbc, ddt], axis=1)
    grads = dict(conv_w=d_conv_w, conv_b=d_conv_b.reshape(-1), dt_bias=ddtb[0, :SSD_H], a_log=dalog[0, :SSD_H],
                 d=ddsk[0, :SSD_H], norm_w=dnw.reshape(-1), out_w=d_out_w, ln_g=dg[0], ln_b=db[0])
    late_names = ("conv_w", "conv_b", "norm_w", "out_w")
    got = {}
    if own is None:
        d_in_w = _mm(s["u"], dproj, ta=True, out_dtype=WIRE, name=tag + "_in_bw")
        du = _mm(dproj, s["w_in"], tb=True, name=tag + "_in_bx")
        grads["in_w"] = d_in_w[:, :SSD_DI + SSD_CONV_DIM + SSD_H]
    else:
        d_in_w, late = _mm(s["u"], dproj, ta=True, out_dtype=WIRE, name=tag + "_in_bw",
                           ride=([own(k, grads[k]) for k in late_names], True))
        got = dict(zip(late_names, late))
        grads["in_w"] = d_in_w[:, :SSD_DI + SSD_CONV_DIM + SSD_H]
        du, (got["in_w"],) = _mm(dproj, s["w_in"], tb=True, name=tag + "_in_bx", ride=([own("in_w", grads["in_w"])], True))
    return dr, du, grads, rode, got


MLA_LOW = MLA_QR + MLA_KVR + LANES
MLA_ZB = MLA_LOW // LANES


def _rope_mat():
    r = lax.broadcasted_iota(jnp.int32, (LANES, LANES), 0)
    c = lax.broadcasted_iota(jnp.int32, (LANES, LANES), 1)
    hf = MLA_ROPE // 2
    return jnp.where((c < hf) & (r == c + hf), -1.0, 0.0) + jnp.where((c >= hf) & (c < 2 * hf) & (r == c - hf), 1.0, 0.0)


def _rope(x, cosf, sinf):
    return x * cosf + _sel_r(x, _rope_mat()) * sinf


def _rope_adj(d, cosf, sinf):
    return d * cosf - _sel_r(d * sinf, _rope_mat())


def _mla_low_fn(low, qnw, kvnw, cosf, sinf):
    qc, kvc, kr = low[:, :MLA_QR], low[:, MLA_QR:MLA_QR + MLA_KVR], low[:, MLA_QR + MLA_KVR:]
    qn = qc * lax.rsqrt(jnp.mean(qc * qc, -1, keepdims=True) + RMS_EPS) * qnw
    kvn = kvc * lax.rsqrt(jnp.mean(kvc * kvc, -1, keepdims=True) + RMS_EPS) * kvnw
    return qn, kvn, _rope(kr, cosf, sinf)


def _mla_low_fwd(proj, qnw, kvnw, cosf, sinf, name):
    t = proj.shape[0]
    tr = _tile(t, 256)

    def body(low_ref, qnw_ref, kvnw_ref, cos_ref, sin_ref, qn_ref, kvn_ref, kr_ref):
        qn, kvn, kr = _mla_low_fn(low_ref[...], qnw_ref[...], kvnw_ref[...], cos_ref[...], sin_ref[...])
        qn_ref[...] = qn
        kvn_ref[...] = kvn
        kr_ref[...] = kr

    row = lambda wdt: pl.BlockSpec((tr, wdt), lambda i: (i, 0))
    par = lambda wdt: pl.BlockSpec((1, wdt), lambda i: (0, 0))
    sds = jax.ShapeDtypeStruct
    return pl.pallas_call(
        body, name=name, grid=(t // tr,),
        in_specs=[row(MLA_LOW), par(MLA_QR), par(MLA_KVR), row(LANES), row(LANES)],
        out_specs=[row(MLA_QR), row(MLA_KVR), row(LANES)],
        out_shape=[sds((t, MLA_QR), F32), sds((t, MLA_KVR), F32), sds((t, LANES), F32)],
        compiler_params=_cparams(("parallel",)),
    )(proj, qnw, kvnw, cosf, sinf)


def _mla_low_bwd(proj, qnw, kvnw, cosf, sinf, dqn, dkvn, dkr, name):
    t = proj.shape[0]
    tr = _tile(t, 256)

    def body(low_ref, qnw_ref, kvnw_ref, cos_ref, sin_ref, dqn_ref, dkvn_ref, dkr_ref, dlow_ref, dqnw_ref, dkvnw_ref):
        i = pl.program_id(0)

        @pl.when(i == 0)
        def _():
            dqnw_ref[...] = jnp.zeros_like(dqnw_ref)
            dkvnw_ref[...] = jnp.zeros_like(dkvnw_ref)

        cosf, sinf = cos_ref[...], sin_ref[...]
        _, vjp = jax.vjp(lambda a, b, c: _mla_low_fn(a, b, c, cosf, sinf), low_ref[...], qnw_ref[...], kvnw_ref[...])
        dlow, dq, dk = vjp((dqn_ref[...], dkvn_ref[...], dkr_ref[...]))
        dlow_ref[...] = dlow
        dqnw_ref[...] += dq
        dkvnw_ref[...] += dk

    row = lambda wdt: pl.BlockSpec((tr, wdt), lambda i: (i, 0))
    par = lambda wdt: pl.BlockSpec((1, wdt), lambda i: (0, 0))
    sds = jax.ShapeDtypeStruct
    return pl.pallas_call(
        body, name=name, grid=(t // tr,),
        in_specs=[row(MLA_LOW), par(MLA_QR), par(MLA_KVR), row(LANES), row(LANES), row(MLA_QR), row(MLA_KVR), row(LANES)],
        out_specs=[row(MLA_LOW), par(MLA_QR), par(MLA_KVR)],
        out_shape=[sds((t, MLA_LOW), F32), sds((1, MLA_QR), F32), sds((1, MLA_KVR), F32)],
        compiler_params=_cparams(("arbitrary",)),
    )(proj, qnw, kvnw, cosf, sinf, dqn, dkvn, dkr)


def _rope_heads(x, col_blk0, cosf, sinf, adjoint, name):
    t = x.shape[0]
    tr = _tile(t, 512)

    def body(x_ref, cos_ref, sin_ref, o_ref):
        f = _rope_adj if adjoint else _rope
        o_ref[...] = f(x_ref[...], cos_ref[...], sin_ref[...])

    tab = pl.BlockSpec((tr, LANES), lambda i, h: (i, 0))
    return pl.pallas_call(
        body, name=name, grid=(t // tr, MLA_H),
        in_specs=[pl.BlockSpec((tr, LANES), lambda i, h: (i, col_blk0 + h)), tab, tab],
        out_specs=pl.BlockSpec((tr, LANES), lambda i, h: (i, h)),
        out_shape=jax.ShapeDtypeStruct((t, MLA_H * LANES), F32), compiler_params=_cparams(("parallel", "parallel")),
    )(x, cosf, sinf)


ATT_HB = 4
ATT_W = ATT_HB * LANES


def _att_qk(qn_ref, qr_ref, kn_ref, kr_ref):
    q2 = jnp.concatenate([_heads_of(qn_ref, ATT_HB), _heads_of(qr_ref, ATT_HB)], axis=2)
    kr = kr_ref[...]
    k2 = jnp.concatenate([_heads_of(kn_ref, ATT_HB), jnp.broadcast_to(kr[None], (ATT_HB,) + kr.shape)], axis=2)
    return q2, k2


def _att_scores(q2, k2, masked):
    s = _bdg(_mx(q2 * MLA_SCALE), _mx(k2), 2, 2)
    if masked:
        tq, tk = s.shape[1:]
        s = jnp.where(lax.broadcasted_iota(jnp.int32, (tq, tk), 1) <= lax.broadcasted_iota(jnp.int32, (tq, tk), 0), s, -jnp.inf)
    return s


def _on_causal_blocks(q_blk, k_blk, step):
    @pl.when(k_blk < q_blk)
    def _():
        step(False)

    @pl.when(k_blk == q_blk)
    def _():
        step(True)


def _put_heads(ref, val):
    for b in range(val.shape[0]):
        ref[:, b * LANES:(b + 1) * LANES] = val[b].astype(ref.dtype)


def _att_ds(s, v_ref, o_ref, do_ref, lse_ref):
    do = _heads_of(do_ref, ATT_HB)
    p = jnp.exp(s - _heads_of(lse_ref, ATT_HB)[:, :, 0:1])
    dp = _bdg(_mx(do), _mx(_heads_of(v_ref, ATT_HB)), 2, 2)
    dl = jnp.sum(do * _heads_of(o_ref, ATT_HB), -1, keepdims=True)
    return _mx(p), _mx(p * (dp - dl) * MLA_SCALE), do


def _attn_fwd(q, qr, kv, kr, proj, name, ride=None):
    t = q.shape[0]
    tq = tk = _tile(t, ATT_BLK)
    nq = nk = t // tq

    def body(qn_ref, qr_ref, kn_ref, kr_ref, v_ref, *rest):
        z_refs, (o_ref, og_ref, lse_ref, m_s, l_s, acc_s) = rest[:ATT_HB], rest[ATT_HB:]
        i, j = pl.program_id(1), pl.program_id(2)

        @pl.when(j == 0)
        def _():
            m_s[...] = jnp.full_like(m_s, -jnp.inf)
            l_s[...] = jnp.zeros_like(l_s)
            acc_s[...] = jnp.zeros_like(acc_s)

        def step(masked):
            s = _att_scores(*_att_qk(qn_ref, qr_ref, kn_ref, kr_ref), masked)
            m_new = jnp.maximum(m_s[...], jnp.max(s, -1, keepdims=True))
            p = jnp.exp(s - m_new)
            corr = jnp.exp(m_s[...] - m_new)
            l_s[...] = corr * l_s[...] + jnp.sum(p, -1, keepdims=True)
            acc_s[...] = corr * acc_s[...] + _bdg(_mx(p), _mx(_heads_of(v_ref, ATT_HB)), 2, 1)
            m_s[...] = m_new

        _on_causal_blocks(i, j, step)

        @pl.when(j == nk - 1)
        def _():
            o = acc_s[...] / l_s[...]
            _put_heads(o_ref, o)
            lse = m_s[...] + jnp.log(l_s[...])
            for b in range(ATT_HB):
                og_ref[:, b * LANES:(b + 1) * LANES] = (o[b] * _silu(z_refs[b][...])).astype(og_ref.dtype)
                lse_ref[:, b * LANES:(b + 1) * LANES] = jnp.broadcast_to(lse[b], (tq, LANES))

    qs = lambda off: pl.BlockSpec((tq, ATT_W), lambda h, i, j: (i, off // ATT_HB + h))
    ks = lambda off: pl.BlockSpec((tk, ATT_W), lambda h, i, j: (jnp.minimum(j, i), off // ATT_HB + h))
    zs = [pl.BlockSpec((tq, LANES), functools.partial(lambda b, h, i, j: (i, MLA_ZB + h * ATT_HB + b), b)) for b in range(ATT_HB)]
    sds = jax.ShapeDtypeStruct
    return _hosted_call(
        body, name=name, grid=(MLA_H // ATT_HB, nq, nk),
        in_specs=[qs(0), qs(0), ks(0), pl.BlockSpec((tk, LANES), lambda h, i, j: (jnp.minimum(j, i), 0)), ks(MLA_H)] + zs,
        out_specs=[qs(0), qs(0), qs(0)],
        out_shape=[sds((t, MLA_GATE), F32), sds((t, MLA_GATE), F32), sds((t, MLA_H * LANES), F32)],
        scratch_shapes=[pltpu.VMEM((ATT_HB, tq, 1), F32), pltpu.VMEM((ATT_HB, tq, 1), F32), pltpu.VMEM((ATT_HB, tq, LANES), F32)],
        sem=("parallel", "parallel", "arbitrary"), args=(q, qr, kv, kr, kv) + (proj,) * ATT_HB, ride=ride)


def _gate_bwd(dog, o, proj, name):
    t = o.shape[0]
    tr = _tile(t, 512)

    def body(d_ref, o_ref, z_ref, do_ref, dz_ref):
        z = z_ref[...]
        sg = jax.nn.sigmoid(z)
        d = d_ref[...]
        do_ref[...] = d * z * sg
        dz_ref[...] = d * o_ref[...] * (sg * (1.0 + z * (1.0 - sg)))

    blk = lambda off: pl.BlockSpec((tr, 512), lambda i, j: (i, off + j))
    assert MLA_LOW % 512 != 0 or True
    zspec = pl.BlockSpec((tr, LANES), lambda i, j: (i, MLA_ZB + j))
    b128 = pl.BlockSpec((tr, LANES), lambda i, j: (i, j))
    sds = jax.ShapeDtypeStruct
    return pl.pallas_call(
        body, name=name, grid=(t // tr, MLA_GATE // LANES),
        in_specs=[b128, b128, zspec], out_specs=[b128, b128],
        out_shape=[sds((t, MLA_GATE), F32), sds((t, MLA_GATE), F32)],
        compiler_params=_cparams(("parallel", "parallel")),
    )(dog, o, proj)


def _attn_bwd_q(q, qr, kv, kr, o, do, lse, name, ride=None):
    t = q.shape[0]
    tq = tk = _tile(t, ATT_BLK)
    nq = nk = t // tq

    def body(qn_ref, qr_ref, kn_ref, kr_ref, v_ref, o_ref, do_ref, lse_ref, dqn_ref, dqr_ref, an_s, ar_s):
        i, j = pl.program_id(1), pl.program_id(2)

        @pl.when(j == 0)
        def _():
            an_s[...] = jnp.zeros_like(an_s)
            ar_s[...] = jnp.zeros_like(ar_s)

        def step(masked):
            q2, k2 = _att_qk(qn_ref, qr_ref, kn_ref, kr_ref)
            ds = _att_ds(_att_scores(q2, k2, masked), v_ref, o_ref, do_ref, lse_ref)[1]
            dq2 = _bdg(ds, _mx(k2), 2, 1)
            an_s[...] += dq2[:, :, :LANES]
            ar_s[...] += dq2[:, :, LANES:]

        _on_causal_blocks(i, j, step)

        @pl.when(j == nk - 1)
        def _():
            _put_heads(dqn_ref, an_s[...])
            _put_heads(dqr_ref, ar_s[...])

    qs = lambda off: pl.BlockSpec((tq, ATT_W), lambda h, i, j: (i, off // ATT_HB + h))
    ks = lambda off: pl.BlockSpec((tk, ATT_W), lambda h, i, j: (jnp.minimum(j, i), off // ATT_HB + h))
    sds = jax.ShapeDtypeStruct
    return _hosted_call(
        body, name=name, grid=(MLA_H // ATT_HB, nq, nk),
        in_specs=[qs(0), qs(0), ks(0), pl.BlockSpec((tk, LANES), lambda h, i, j: (jnp.minimum(j, i), 0)), ks(MLA_H),
                  qs(0), qs(0), qs(0)],
        out_specs=[qs(0), qs(0)],
        out_shape=[sds((t, MLA_H * LANES), F32), sds((t, MLA_H * LANES), F32)],
        scratch_shapes=[pltpu.VMEM((ATT_HB, tq, LANES), F32), pltpu.VMEM((ATT_HB, tq, LANES), F32)],
        sem=("parallel", "parallel", "arbitrary"), args=(q, qr, kv, kr, kv, o, do, lse), ride=ride)


def _attn_bwd_kv(q, qr, kv, kr, o, do, lse, name, ride=None):
    t = q.shape[0]
    tq = tk = _tile(t, ATT_BLK)
    nq = nk = t // tq

    def body(qn_ref, qr_ref, kn_ref, kr_ref, v_ref, o_ref, do_ref, lse_ref, dkn_ref, dv_ref, dkr_ref, akn_s, av_s):
        j, h, i = pl.program_id(0), pl.program_id(1), pl.program_id(2)

        @pl.when((h == 0) & (i == 0))
        def _():
            dkr_ref[...] = jnp.zeros_like(dkr_ref)

        @pl.when(i == 0)
        def _():
            akn_s[...] = jnp.zeros_like(akn_s)
            av_s[...] = jnp.zeros_like(av_s)

        def step(masked):
            q2, k2 = _att_qk(qn_ref, qr_ref, kn_ref, kr_ref)
            p, ds, do = _att_ds(_att_scores(q2, k2, masked), v_ref, o_ref, do_ref, lse_ref)
            av_s[...] += _bdg(p, _mx(do), 1, 1)
            dk2 = _bdg(ds, _mx(q2), 1, 1)
            akn_s[...] += dk2[:, :, :LANES]
            dkr_ref[...] += jnp.sum(dk2[:, :, LANES:], axis=0)

        _on_causal_blocks(i, j, step)

        @pl.when(i == nq - 1)
        def _():
            _put_heads(dkn_ref, akn_s[...])
            _put_heads(dv_ref, av_s[...])

    qs = lambda off: pl.BlockSpec((tq, ATT_W), lambda j, h, i: (jnp.maximum(i, j), off // ATT_HB + h))
    ks = lambda off: pl.BlockSpec((tk, ATT_W), lambda j, h, i: (j, off // ATT_HB + h))
    sds = jax.ShapeDtypeStruct
    return _hosted_call(
        body, name=name, grid=(nk, MLA_H // ATT_HB, nq),
        in_specs=[qs(0), qs(0), ks(0), pl.BlockSpec((tk, LANES), lambda j, h, i: (j, 0)), ks(MLA_H), qs(0), qs(0), qs(0)],
        out_specs=[ks(0), ks(0), pl.BlockSpec((tk, LANES), lambda j, h, i: (j, 0))],
        out_shape=[sds((t, MLA_H * LANES), F32), sds((t, MLA_H * LANES), F32), sds((t, LANES), F32)],
        scratch_shapes=[pltpu.VMEM((ATT_HB, tk, LANES), F32), pltpu.VMEM((ATT_HB, tk, LANES), F32)],
        sem=("parallel", "arbitrary", "arbitrary"), args=(q, qr, kv, kr, kv, o, do, lse), ride=ride)


def _rope_tables(positions):
    inv_freq = ROPE_THETA ** (-jnp.arange(0, MLA_ROPE, 2, dtype=F32) / MLA_ROPE)
    ang = positions.astype(F32)[:, None] * inv_freq
    pad = jnp.zeros((positions.shape[0], LANES - MLA_ROPE), F32)
    cos, sin = jnp.cos(ang), jnp.sin(ang)
    return jnp.concatenate([cos, cos, pad], 1), jnp.concatenate([sin, sin, pad], 1)


def _mla_weights(w):
    dt = w["in_w"].dtype
    iw = w["in_w"]
    c1 = MLA_QR + MLA_KVR + MLA_ROPE
    w_in = jnp.concatenate([iw[:, :c1], jnp.zeros((D_MODEL, LANES - MLA_ROPE), dt), iw[:, c1:]], axis=1)
    qu = w["q_up_w"].reshape(MLA_QR, MLA_H, MLA_NOPE + MLA_ROPE)
    qrope = jnp.concatenate([qu[:, :, MLA_NOPE:], jnp.zeros((MLA_QR, MLA_H, LANES - MLA_ROPE), dt)], axis=2)
    w_q = jnp.concatenate([qu[:, :, :MLA_NOPE].reshape(MLA_QR, -1), qrope.reshape(MLA_QR, -1)], axis=1)
    kvu = w["kv_up_w"].reshape(MLA_KVR, MLA_H, MLA_NOPE + MLA_V)
    w_kv = jnp.concatenate([kvu[:, :, :MLA_NOPE].reshape(MLA_KVR, -1), kvu[:, :, MLA_NOPE:].reshape(MLA_KVR, -1)], axis=1)
    return w_in, w_q, w_kv


def _mla_layer_fwd(u, w, ln_g, ln_b, cosf, sinf, tag, rides=None):
    w_in, w_q, w_kv = _mla_weights(w)
    qnw, kvnw = w["q_norm_w"].reshape(1, -1), w["kv_norm_w"].reshape(1, -1)
    proj = _mm(u, w_in, name=tag + "_in")
    qn, kvn, kr = _mla_low_fwd(proj, qnw, kvnw, cosf, sinf, name=tag + "_low")
    q = _mm(qn, w_q, name=tag + "_qup")
    kv = _mm(kvn, w_kv, name=tag + "_kvup")
    qr = _rope_heads(q, MLA_H, cosf, sinf, False, name=tag + "_qrope")
    rode = {}
    (o, og, lse), rode["attn"] = _attn_fwd(q, qr, kv, kr, proj, name=tag + "_attn", ride=_ride_of(rides, "attn", False))
    y = _mm(og, w["out_w"], name=tag + "_out")
    h = _ln_fwd(u, y, ln_g, ln_b, name=tag + "_ln")
    saved = dict(u=u, w_in=w_in, w_q=w_q, w_kv=w_kv, qnw=qnw, kvnw=kvnw, proj=proj, qn=qn, kvn=kvn, kr=kr, q=q, kv=kv,
                 qr=qr, o=o, og=og, lse=lse, y=y)
    return h, saved, rode


def _mla_layer_bwd(s, w, ln_g, cosf, sinf, dr_up, du_up, tag, rides=None):
    dr, dg, db = _ln_bwd(s["u"], s["y"], ln_g, dr_up, du_up, name=tag + "_ln_b")
    dog = _mm(dr, w["out_w"], tb=True, name=tag + "_out_bx")
    d_out_w = _mm(s["og"], dr, ta=True, out_dtype=WIRE, name=tag + "_out_bw")
    do, dz = _gate_bwd(dog, s["o"], s["proj"], name=tag + "_gate_b")
    rode = {}
    (dqn_h, dqr_rot), rode["attn_bq"] = _attn_bwd_q(s["q"], s["qr"], s["kv"], s["kr"], s["o"], do, s["lse"],
                                                    name=tag + "_attn_bq", ride=_ride_of(rides, "attn_bq", True))
    (dkn_h, dv_h, dkr_rot), rode["attn_bkv"] = _attn_bwd_kv(s["q"], s["qr"], s["kv"], s["kr"], s["o"], do, s["lse"],
                                                           name=tag + "_attn_bkv", ride=_ride_of(rides, "attn_bkv", True))
    dqr = _rope_heads(dqr_rot, 0, cosf, sinf, True, name=tag + "_qrope_b")
    dq = jnp.concatenate([dqn_h, dqr], axis=1)
    dkv = jnp.concatenate([dkn_h, dv_h], axis=1)
    d_wq = _mm(s["qn"], dq, ta=True, out_dtype=WIRE, name=tag + "_qup_bw")
    dqn = _mm(dq, s["w_q"], tb=True, name=tag + "_qup_bx")
    d_wkv = _mm(s["kvn"], dkv, ta=True, out_dtype=WIRE, name=tag + "_kvup_bw")
    dkvn = _mm(dkv, s["w_kv"], tb=True, name=tag + "_kvup_bx")
    dlow, dqnw, dkvnw = _mla_low_bwd(s["proj"], s["qnw"], s["kvnw"], cosf, sinf, dqn, dkvn, dkr_rot, name=tag + "_low_b")
    dproj = jnp.concatenate([dlow, dz], axis=1)
    d_in = _mm(s["u"], dproj, ta=True, out_dtype=WIRE, name=tag + "_in_bw")
    du = _mm(dproj, s["w_in"], tb=True, name=tag + "_in_bx")
    c1 = MLA_QR + MLA_KVR + MLA_ROPE
    d_in_w = jnp.concatenate([d_in[:, :c1], d_in[:, MLA_LOW:]], axis=1)
    dq3n = d_wq[:, :MLA_H * MLA_NOPE].reshape(MLA_QR, MLA_H, MLA_NOPE)
    dq3r = d_wq[:, MLA_H * MLA_NOPE:].reshape(MLA_QR, MLA_H, LANES)[:, :, :MLA_ROPE]
    d_q_up = jnp.concatenate([dq3n, dq3r], axis=2).reshape(MLA_QR, -1)
    dkv3 = d_wkv.reshape(MLA_KVR, 2, MLA_H, MLA_NOPE)
    d_kv_up = jnp.concatenate([dkv3[:, 0], dkv3[:, 1]], axis=2).reshape(MLA_KVR, -1)
    grads = dict(in_w=d_in_w, q_norm_w=dqnw[0], q_up_w=d_q_up, kv_norm_w=dkvnw[0], kv_up_w=d_kv_up, out_w=d_out_w,
                 ln_g=dg[0], ln_b=db[0])
    return dr, du, grads, rode


GDN_REP = GDN_HV // GDN_HK
GDN_A_LANE = GDN_HV
GDN_HPB = 4
GDN_VPB = GDN_HPB * GDN_REP


def _bdg(a, b, ca, cb):
    return lax.dot_general(a, b, (((ca,), (cb,)), ((0,), (0,))), preferred_element_type=F32)


@jax.custom_vjp
def _bnn(a, b):
    return _bdg(_mx(a), _mx(b), 2, 1)


def _bnn_f(a, b):
    return _bnn(a, b), (a, b)


def _bnn_b(res, ct):
    a, b = res
    return _bdg(_mx(ct), _mx(b), 2, 2), _bdg(_mx(a), _mx(ct), 1, 1)


_bnn.defvjp(_bnn_f, _bnn_b)


@jax.custom_vjp
def _bnt(a, b):
    return _bdg(_mx(a), _mx(b), 2, 2)


def _bnt_f(a, b):
    return _bnt(a, b), (a, b)


def _bnt_b(res, ct):
    a, b = res
    return _bdg(_mx(ct), _mx(b), 2, 1), _bdg(_mx(ct), _mx(a), 1, 1)


_bnt.defvjp(_bnt_f, _bnt_b)


@jax.custom_vjp
def _btn(a, b):
    return _bdg(_mx(a), _mx(b), 1, 1)


def _btn_f(a, b):
    return _btn(a, b), (a, b)


def _btn_b(res, ct):
    a, b = res
    return _bdg(_mx(b), _mx(ct), 2, 2), _bdg(_mx(a), _mx(ct), 2, 1)


_btn.defvjp(_btn_f, _btn_b)


def _h3(a, b, ca=2, cb=1):
    ah, bh = _mx(a), _mx(b)
    al, bl = _mx(a - ah.astype(F32)), _mx(b - bh.astype(F32))
    return _bdg(ah, bh, ca, cb) + (_bdg(ah, bl, ca, cb) + _bdg(al, bh, ca, cb))


@jax.custom_vjp
def _neumann_inverse(x):
    L = x.shape[-1]
    eye = (lax.broadcasted_iota(jnp.int32, (L, L), 0) == lax.broadcasted_iota(jnp.int32, (L, L), 1)).astype(F32)
    inv = eye + x
    xp = x
    for _ in range(L.bit_length() - 2):
        xp = _h3(xp, xp)
        inv = inv + _h3(inv, xp)
    return inv


def _neumann_f(x):
    inv = _neumann_inverse(x)
    return inv, inv


def _neumann_b(inv, ct):
    return (_h3(_h3(inv, ct, 1, 1), inv, 2, 2),)


_neumann_inverse.defvjp(_neumann_f, _neumann_b)


@jax.custom_vjp
def _saved_inverse(x, inv):
    return inv


def _saved_f(x, inv):
    return inv, inv


def _saved_b(inv, ct):
    return _neumann_b(inv, ct)[0], jnp.zeros_like(inv)


_saved_inverse.defvjp(_saved_f, _saved_b)


def _cat0(parts):
    return jnp.concatenate([p[None] for p in parts], axis=0)


def _gdn_chunk(hb, q, k, v, z, ba, alog, dtb, nw, s, inv_saved=None):
    L = q.shape[1]
    r_i = lax.broadcasted_iota(jnp.int32, (L, L), 0)
    c_i = lax.broadcasted_iota(jnp.int32, (L, L), 1)
    incl, strict = r_i >= c_i, r_i > c_i
    rep = lambda t: jnp.broadcast_to(t[:, None], (GDN_HPB, GDN_REP) + t.shape[1:]).reshape((GDN_VPB,) + t.shape[1:])
    qn = rep(q * lax.rsqrt(jnp.sum(q * q, -1, keepdims=True) + RMS_EPS) * (GDN_DK ** -0.5))
    kn = rep(k * lax.rsqrt(jnp.sum(k * k, -1, keepdims=True) + RMS_EPS))
    beta_all = jax.nn.sigmoid(ba)
    g_all = -jnp.exp(alog) * _softplus(ba + dtb)
    gcs_all = _sel_l(incl.astype(F32), g_all)
    lane = lax.broadcasted_iota(jnp.int32, (L, LANES), 1)
    pick = lambda mat, idx: jnp.sum(jnp.where(lane == idx, mat, 0.0), axis=1, keepdims=True)
    beta = _cat0([pick(beta_all, GDN_VPB * hb + b) for b in range(GDN_VPB)])
    gc = _cat0([pick(gcs_all, GDN_A_LANE + GDN_VPB * hb + b) for b in range(GDN_VPB)])
    gm = jnp.broadcast_to(gc, (GDN_VPB, L, L))
    decay = jnp.exp(jnp.where(incl, gm - jnp.swapaxes(gm, 1, 2), -jnp.inf))
    kb = kn * beta
    eg = jnp.exp(gc)
    x = -jnp.where(strict, _bnt(kb, kn) * decay, 0.0)
    inv = _neumann_inverse(x) if inv_saved is None else _saved_inverse(x, inv_saved)
    uw = _bnn(inv, jnp.concatenate([v * beta, kb * eg], axis=2))
    uu, ww = uw[:, :, :GDN_DV], uw[:, :, GDN_DV:]
    qk = jnp.where(incl, _bnt(qn, kn) * decay, 0.0)
    last = lax.broadcasted_iota(jnp.int32, (L, 1), 0) == L - 1
    glast = jnp.sum(jnp.where(last, gc, 0.0), axis=1, keepdims=True)
    kdec = kn * jnp.exp(glast - gc)
    vnew = uu - _bnn(ww, s)
    o = _bnn(qn * eg, s) + _bnn(qk, vnew)
    new = s * jnp.exp(glast) + _btn(kdec, vnew)
    on = o * lax.rsqrt(jnp.mean(o * o, -1, keepdims=True) + RMS_EPS) * nw * _silu(z)
    return (on, new, inv) if inv_saved is None else (on, new)


def _heads_of(ref, n):
    return _cat0([ref[:, i * LANES:(i + 1) * LANES] for i in range(n)])


def _gdn_specs(nc, rev):
    cc = (lambda c: nc - 1 - c) if rev else (lambda c: c)
    wq, wv = GDN_HPB * GDN_DK, GDN_VPB * GDN_DV
    par = pl.BlockSpec((1, LANES), lambda c, h: (0, 0))
    return dict(
        q=pl.BlockSpec((GDN_L, wq), lambda c, h: (cc(c), h)),
        k=pl.BlockSpec((GDN_L, wq), lambda c, h: (cc(c), GDN_KEY // wq + h)),
        v=pl.BlockSpec((GDN_L, wv), lambda c, h: (cc(c), 2 * GDN_KEY // wv + h)),
        z=pl.BlockSpec((GDN_L, wv), lambda c, h: (cc(c), GDN_CONV_DIM // wv + h)),
        ba=pl.BlockSpec((GDN_L, LANES), lambda c, h: (cc(c), (GDN_CONV_DIM + GDN_VAL) // LANES)),
        par=par,
        st=pl.BlockSpec((1, GDN_VPB, GDN_DK, GDN_DV), lambda c, h: (cc(c), h, 0, 0)),
        inv=pl.BlockSpec((1, GDN_VPB, GDN_L, GDN_L), lambda c, h: (cc(c), h, 0, 0)),
        o=pl.BlockSpec((GDN_L, wv), lambda c, h: (cc(c), h)),
        qk_out=pl.BlockSpec((GDN_L, wq), lambda c, h: (cc(c), h)),
        ba_out=pl.BlockSpec((GDN_L, LANES), lambda c, h: (cc(c), 0)),
    )


def _gdn_fwd(proj, act, alog, dtb, nw, name, ride=None):
    t = proj.shape[0]
    nc = t // GDN_L
    sp = _gdn_specs(nc, False)

    def body(q_ref, k_ref, v_ref, z_ref, ba_ref, alog_ref, dtb_ref, nw_ref, o_ref, st_ref, inv_ref, state):
        c, h = pl.program_id(0), pl.program_id(1)

        mine = pl.ds(h * GDN_VPB, GDN_VPB)

        @pl.when(c == 0)
        def _():
            state[mine] = jnp.zeros((GDN_VPB, GDN_DK, GDN_DV), F32)

        prev = state[mine]
        st_ref[0] = prev
        on, new, inv = _gdn_chunk(h, _heads_of(q_ref, GDN_HPB), _heads_of(k_ref, GDN_HPB), _heads_of(v_ref, GDN_VPB),
                                  _heads_of(z_ref, GDN_VPB), ba_ref[...], alog_ref[...], dtb_ref[...], nw_ref[...], prev)
        inv_ref[0] = inv
        for b in range(GDN_VPB):
            o_ref[:, b * LANES:(b + 1) * LANES] = on[b].astype(o_ref.dtype)
        state[mine] = new

    sds = jax.ShapeDtypeStruct
    return _hosted_call(
        body, name=name, grid=(nc, GDN_HK // GDN_HPB),
        in_specs=[sp["q"], sp["k"], sp["v"], sp["z"], sp["ba"], sp["par"], sp["par"], sp["par"]],
        out_specs=[sp["o"], sp["st"], sp["inv"]],
        out_shape=[sds((t, GDN_VAL), F32), sds((nc, GDN_HV, GDN_DK, GDN_DV), F32), sds((nc, GDN_HV, GDN_L, GDN_L), F32)],
        scratch_shapes=[pltpu.VMEM((GDN_HV, GDN_DK, GDN_DV), F32)],
        sem=("arbitrary", "arbitrary"), args=(act, act, act, proj, proj, alog, dtb, nw), ride=ride)


def _gdn_bwd(proj, act, alog, dtb, nw, states, invs, don, name, ride=None):
    t = proj.shape[0]
    nc = t // GDN_L
    sp = _gdn_specs(nc, True)

    def body(q_ref, k_ref, v_ref, z_ref, ba_ref, alog_ref, dtb_ref, nw_ref, st_ref, do_ref, inv_ref,
             dq_ref, dk_ref, dv_ref, dz_ref, dba_ref, dalog_ref, ddtb_ref, dnw_ref, dstate):
        c, h = pl.program_id(0), pl.program_id(1)

        mine = pl.ds(h * GDN_VPB, GDN_VPB)

        @pl.when(c == 0)
        def _():
            dstate[mine] = jnp.zeros((GDN_VPB, GDN_DK, GDN_DV), F32)

        @pl.when((c == 0) & (h == 0))
        def _():
            dalog_ref[...] = jnp.zeros_like(dalog_ref)
            ddtb_ref[...] = jnp.zeros_like(ddtb_ref)
            dnw_ref[...] = jnp.zeros_like(dnw_ref)

        @pl.when(h == 0)
        def _():
            dba_ref[...] = jnp.zeros_like(dba_ref)

        _, vjp = jax.vjp(functools.partial(_gdn_chunk, h, inv_saved=inv_ref[0]), _heads_of(q_ref, GDN_HPB),
                         _heads_of(k_ref, GDN_HPB), _heads_of(v_ref, GDN_VPB), _heads_of(z_ref, GDN_VPB), ba_ref[...],
                         alog_ref[...], dtb_ref[...], nw_ref[...], st_ref[0])
        dq, dk, dv, dz, dba, dalog, ddtb, dnw, dprev = vjp((_heads_of(do_ref, GDN_VPB), dstate[mine]))
        for i in range(GDN_HPB):
            dq_ref[:, i * LANES:(i + 1) * LANES] = dq[i]
            dk_ref[:, i * LANES:(i + 1) * LANES] = dk[i]
        for b in range(GDN_VPB):
            dv_ref[:, b * LANES:(b + 1) * LANES] = dv[b]
            dz_ref[:, b * LANES:(b + 1) * LANES] = dz[b]
        dba_ref[...] += dba
        dalog_ref[...] += dalog
        ddtb_ref[...] += ddtb
        dnw_ref[...] += dnw
        dstate[mine] = dprev

    sds = jax.ShapeDtypeStruct
    par_out = pl.BlockSpec((1, LANES), lambda c, h: (0, 0))
    return _hosted_call(
        body, name=name, grid=(nc, GDN_HK // GDN_HPB),
        in_specs=[sp["q"], sp["k"], sp["v"], sp["z"], sp["ba"], sp["par"], sp["par"], sp["par"], sp["st"], sp["o"],
                  sp["inv"]],
        out_specs=[sp["qk_out"], sp["qk_out"], sp["o"], sp["o"], sp["ba_out"], par_out, par_out, par_out],
        out_shape=[sds((t, GDN_KEY), F32), sds((t, GDN_KEY), F32), sds((t, GDN_VAL), F32), sds((t, GDN_VAL), F32),
                   sds((t, LANES), F32), sds((1, LANES), F32), sds((1, LANES), F32), sds((1, LANES), F32)],
        scratch_shapes=[pltpu.VMEM((GDN_HV, GDN_DK, GDN_DV), F32)],
        sem=("arbitrary", "arbitrary"), args=(act, act, act, proj, proj, alog, dtb, nw, states, don, invs), ride=ride)


GDN_PROJ = GDN_CONV_DIM + GDN_VAL + 2 * GDN_HV


def _gdn_layer_fwd(u, w, ln_g, ln_b, tag, rides=None, late=None):
    w_in = jnp.concatenate([w["in_w"], jnp.zeros((D_MODEL, GDN_PROJ_PAD - GDN_PROJ), w["in_w"].dtype)], axis=1)
    alog = _pad_lanes(w["a_log"], offset=GDN_A_LANE)
    dtb = _pad_lanes(w["dt_bias"], offset=GDN_A_LANE)
    nw = w["norm_w"].reshape(1, GDN_DV)
    zb = jnp.zeros((1, GDN_CONV_DIM), F32)
    if late is None:
        proj = _mm(u, w_in, name=tag + "_in")
    else:
        proj, got = _mm(u, w_in, name=tag + "_in", ride=(late[0], False))
        w.update(late[1](got))
    act, _ = _conv_fwd(proj, 0, w["conv_w"], zb, name=tag + "_conv")
    rode = {}
    (on, states, invs), rode["delta"] = _gdn_fwd(proj, act, alog, dtb, nw, name=tag + "_delta",
                                                 ride=_ride_of(rides, "delta", False))
    y = _mm(on, w["out_w"], name=tag + "_out")
    h = _ln_fwd(u, y, ln_g, ln_b, name=tag + "_ln")
    saved = dict(u=u, w_in=w_in, proj=proj, act=act, states=states, invs=invs, on=on, y=y, alog=alog, dtb=dtb, nw=nw, zb=zb)
    return h, saved, rode


def _gdn_layer_bwd(s, w, ln_g, dr_up, du_up, tag, rides=None, own=None):
    t = s["u"].shape[0]
    dr, dg, db = _ln_bwd(s["u"], s["y"], ln_g, dr_up, du_up, name=tag + "_ln_b")
    don = _mm(dr, w["out_w"], tb=True, name=tag + "_out_bx")
    d_out_w = _mm(s["on"], dr, ta=True, out_dtype=WIRE, name=tag + "_out_bw")
    carried = list((rides or {}).get("delta_b", [])) + ([own("out_w", d_out_w)] if own else [])
    rode = {}
    (dq, dk, dv, dz, dba, dalog, ddtb, dnw), rode["delta_b"] = _gdn_bwd(
        s["proj"], s["act"], s["alog"], s["dtb"], s["nw"], s["states"], s["invs"], don, name=tag + "_delta_b",
        ride=(carried, True) if carried else None)
    dact = jnp.concatenate([dq, dk, dv], axis=1)
    dqkv, d_conv_w, _ = _conv_bwd(s["proj"], 0, w["conv_w"], s["zb"], dact, name=tag + "_conv_b")
    dproj = jnp.concatenate([dqkv, dz, dba, jnp.zeros((t, GDN_PROJ_PAD - GDN_PROJ - (LANES - 2 * GDN_HV)), F32)], axis=1)
    d_in = _mm(s["u"], dproj, ta=True, out_dtype=WIRE, name=tag + "_in_bw")
    du = _mm(dproj, s["w_in"], tb=True, name=tag + "_in_bx")
    grads = dict(in_w=d_in[:, :GDN_PROJ], conv_w=d_conv_w, a_log=dalog[0, GDN_A_LANE:GDN_A_LANE + GDN_HV],
                 dt_bias=ddtb[0, GDN_A_LANE:GDN_A_LANE + GDN_HV], norm_w=dnw[0], out_w=d_out_w, ln_g=dg[0], ln_b=db[0])
    return dr, du, grads, rode


def _mesh_pos():
    return lax.axis_index("x"), lax.axis_index("y"), lax.axis_index("c")


def _peer(k, x, y, c):
    return ((1 - x) if k & 4 else x, (1 - y) if k & 2 else y, (1 - c) if k & 1 else c)


def _ride_copies(ins, outs, send, recv, loc, scatter, with_arrivals):
    n = len(ins)
    x, y, c = _mesh_pos()
    me = 4 * x + 2 * y + c
    local = [pltpu.make_async_copy(ins[i].at[me] if scatter else ins[i], outs[i].at[me], loc.at[i]) for i in range(n)]
    sends, arrivals = [], []
    for k in range(1, N_DEV):
        peer = _peer(k, x, y, c)
        pidx = 4 * peer[0] + 2 * peer[1] + peer[2]
        for i in range(n):
            src = ins[i].at[pidx] if scatter else ins[i]
            sems = dict(send_sem=send.at[i, k - 1], recv_sem=recv.at[i, k - 1], device_id=peer,
                        device_id_type=pl.DeviceIdType.MESH)
            sends.append(pltpu.make_async_remote_copy(src_ref=src, dst_ref=outs[i].at[me], **sems))
            if with_arrivals:
                arrivals.append(pltpu.make_async_remote_copy(src_ref=src, dst_ref=outs[i].at[pidx], **sems))
    return local, sends, arrivals


def _ride_start(ins, outs, send, recv, loc, scatter):
    local, sends, _ = _ride_copies(ins, outs, send, recv, loc, scatter, False)
    for cp in local + sends:
        cp.start()


def _ride_wait(ins, outs, send, recv, loc, scatter):
    local, sends, arrivals = _ride_copies(ins, outs, send, recv, loc, scatter, True)
    for cp in arrivals:
        cp.wait_recv()
    for cp in sends:
        cp.wait_send()
    for cp in local:
        cp.wait()


def _ride_shapes(arrs, scatter):
    n = len(arrs)
    out_shape = [jax.ShapeDtypeStruct(a.shape if scatter else (N_DEV,) + a.shape, a.dtype) for a in arrs]
    scratch = [pltpu.SemaphoreType.DMA((n, N_DEV - 1)), pltpu.SemaphoreType.DMA((n, N_DEV - 1)), pltpu.SemaphoreType.DMA((n,))]
    return out_shape, scratch


def _exchange(arrs, scatter, name):
    n = len(arrs)
    hbm = pl.BlockSpec(memory_space=pltpu.HBM)

    def body(*refs):
        ins, outs = refs[:n], refs[n:2 * n]
        _ride_start(ins, outs, *refs[2 * n:], scatter)
        _ride_wait(ins, outs, *refs[2 * n:], scatter)

    out_shape, scratch = _ride_shapes(arrs, scatter)
    return pl.pallas_call(
        body, name=name, in_specs=[hbm] * n, out_specs=[hbm] * n, out_shape=out_shape, scratch_shapes=scratch,
        compiler_params=pltpu.CompilerParams(has_side_effects=True),
    )(*arrs)


def _gather_two_level(arr, name):
    hbm = pl.BlockSpec(memory_space=pltpu.HBM)

    def body(x_ref, out_ref, send, recv, loc):
        x, y, c = _mesh_pos()
        me, sibling = (x, y, c), (x, y, 1 - c)
        chips = [(x, 1 - y), (1 - x, y), (1 - x, 1 - y)]
        slot = lambda p: out_ref.at[4 * p[0] + 2 * p[1] + p[2]]

        def copy(k, block, to, src=None):
            return pltpu.make_async_remote_copy(src_ref=slot(block) if src is None else src, dst_ref=slot(block),
                                                send_sem=send.at[k], recv_sem=recv.at[k], device_id=to,
                                                device_id_type=pl.DeviceIdType.MESH)

        mine = pltpu.make_async_copy(x_ref, slot(me), loc)
        mine.start()
        first = [copy(0, me, sibling, src=x_ref)] + [copy(1 + j, me, (*ch, c), src=x_ref) for j, ch in enumerate(chips)]
        for cp in first:
            cp.start()
        passed = [copy(4 + j, (*ch, c), sibling) for j, ch in enumerate(chips)]
        for j, ch in enumerate(chips):
            copy(1 + j, (*ch, c), me).wait_recv()
            passed[j].start()
        copy(0, sibling, me).wait_recv()
        for j, ch in enumerate(chips):
            copy(4 + j, (*ch, 1 - c), me).wait_recv()
        for cp in first + passed:
            cp.wait_send()
        mine.wait()

    return pl.pallas_call(
        body, name=name, in_specs=[hbm], out_specs=hbm, out_shape=jax.ShapeDtypeStruct((N_DEV,) + arr.shape, arr.dtype),
        scratch_shapes=[pltpu.SemaphoreType.DMA((N_DEV - 1,)), pltpu.SemaphoreType.DMA((N_DEV - 1,)), pltpu.SemaphoreType.DMA],
        compiler_params=pltpu.CompilerParams(has_side_effects=True),
    )(arr)


def _hosted_call(body, *, name, grid, in_specs, out_specs, out_shape, scratch_shapes, sem, args, ride=None):
    if ride is None:
        return pl.pallas_call(body, name=name, grid=grid, in_specs=in_specs, out_specs=out_specs, out_shape=out_shape,
                              scratch_shapes=scratch_shapes, compiler_params=_cparams(sem))(*args), []
    arrs, scatter = ride
    n, ni, no, ns = len(arrs), len(in_specs), len(out_specs), len(scratch_shapes)
    hbm = pl.BlockSpec(memory_space=pltpu.HBM)
    r_shape, r_scratch = _ride_shapes(arrs, scatter)

    def full(*refs):
        a, ri = refs[:ni], refs[ni:ni + n]
        o, ro = refs[ni + n:ni + n + no], refs[ni + n + no:ni + 2 * n + no]
        s, rs = refs[ni + 2 * n + no:ni + 2 * n + no + ns], refs[ni + 2 * n + no + ns:]
        ids = [pl.program_id(d) for d in range(len(grid))]
        first, last = ids[0] == 0, ids[0] == grid[0] - 1
        for d in range(1, len(grid)):
            first, last = first & (ids[d] == 0), last & (ids[d] == grid[d] - 1)

        @pl.when(first)
        def _():
            _ride_start(ri, ro, *rs, scatter)

        body(*a, *o, *s)

        @pl.when(last)
        def _():
            _ride_wait(ri, ro, *rs, scatter)

    outs = pl.pallas_call(
        full, name=name, grid=grid, in_specs=list(in_specs) + [hbm] * n, out_specs=list(out_specs) + [hbm] * n,
        out_shape=list(out_shape) + r_shape, scratch_shapes=list(scratch_shapes) + r_scratch,
        compiler_params=pltpu.CompilerParams(dimension_semantics=("arbitrary",) * len(grid), vmem_limit_bytes=VMEM_LIMIT,
                                             has_side_effects=True),
    )(*args, *arrs)
    return outs[:no], list(outs[no:])


def _unshard(g, ax):
    g = jnp.moveaxis(g, 0, ax)
    sh = g.shape
    return g.reshape(sh[:ax] + (sh[ax] * sh[ax + 1],) + sh[ax + 2:])


def _to_parts(full, ax):
    sh = full.shape
    full = full.reshape(sh[:ax] + (N_DEV, sh[ax] // N_DEV) + sh[ax + 1:])
    return jnp.moveaxis(full, ax, 0)


def _row_tile(r, c):
    cap = max(8, (256 * 1024) // max(c, 1))
    best = None
    for d in range(8, min(r, cap) + 1, 8):
        if r % d == 0:
            best = d
    return r if best is None else best


def _adamw(w, m, v, parts, name):
    r, c = w.shape
    tr = _row_tile(r, c)

    def body(w_ref, m_ref, v_ref, p_ref, g_ref, d_ref, nm_ref, nv_ref):
        g = p_ref[0].astype(F32)
        for q in range(1, N_DEV):
            g = g + p_ref[q].astype(F32)
        nm = ADAM_B1 * m_ref[...] + (1.0 - ADAM_B1) * g
        nv = ADAM_B2 * v_ref[...] + (1.0 - ADAM_B2) * (g * g)
        m_hat = nm / (1.0 - ADAM_B1 ** ADAM_STEP)
        v_hat = nv / (1.0 - ADAM_B2 ** ADAM_STEP)
        g_ref[...] = g
        d_ref[...] = -ADAM_LR * (m_hat / (jnp.sqrt(v_hat) + ADAM_EPS) + ADAM_WD * w_ref[...])
        nm_ref[...] = nm
        nv_ref[...] = nv

    row = pl.BlockSpec((tr, c), lambda i: (i, 0))
    out = jax.ShapeDtypeStruct((r, c), F32)
    return pl.pallas_call(
        body, name=name, grid=(r // tr,),
        in_specs=[row, row, row, pl.BlockSpec((N_DEV, tr, c), lambda i: (0, i, 0))],
        out_specs=[row] * 4, out_shape=[out] * 4, compiler_params=_cparams(("parallel",)),
    )(w, m, v, parts)


WEIGHTS = ['ssd_in_w', 'ssd_conv_w', 'ssd_conv_b', 'ssd_dt_bias', 'ssd_a_log', 'ssd_d', 'ssd_norm_w', 'ssd_out_w',
           'mla_in_w', 'mla_q_norm_w', 'mla_q_up_w', 'mla_kv_norm_w', 'mla_kv_up_w', 'mla_out_w', 'gdn_in_w',
           'gdn_conv_w', 'gdn_a_log', 'gdn_dt_bias', 'gdn_norm_w', 'gdn_out_w', 'ln_g', 'ln_b']
SHARDED = {'ssd_in_w': (1, True), 'ssd_conv_w': (1, False), 'ssd_conv_b': (0, False), 'ssd_norm_w': (0, False),
           'ssd_out_w': (0, True), 'mla_in_w': (1, True), 'mla_q_up_w': (1, True), 'mla_kv_up_w': (1, True),
           'mla_out_w': (0, True), 'gdn_in_w': (1, True), 'gdn_conv_w': (1, False), 'gdn_out_w': (0, True)}
REPLICATED = [n for n in WEIGHTS if n not in SHARDED]


def _pack_small(vals):
    flat = jnp.concatenate([vals[n].reshape(-1).astype(F32) for n in REPLICATED])
    rows = -(-flat.shape[0] // (8 * LANES)) * 8
    return jnp.pad(flat, (0, rows * LANES - flat.shape[0])).reshape(rows, LANES)


def _unpack_small(slab, like):
    flat = slab.reshape(-1)
    out, off = {}, 0
    for n in REPLICATED:
        sz = like[n].size
        out[n] = flat[off:off + sz].reshape(like[n].shape)
        off += sz
    return out


def kernel(x, positions, ssd_in_w, ssd_conv_w, ssd_conv_b, ssd_dt_bias, ssd_a_log, ssd_d, ssd_norm_w, ssd_out_w, mla_in_w, mla_q_norm_w, mla_q_up_w, mla_kv_norm_w, mla_kv_up_w, mla_out_w, gdn_in_w, gdn_conv_w, gdn_a_log, gdn_dt_bias, gdn_norm_w, gdn_out_w, ln_g, ln_b, loss_target, m_ssd_in_w, m_ssd_conv_w, m_ssd_conv_b, m_ssd_dt_bias, m_ssd_a_log, m_ssd_d, m_ssd_norm_w, m_ssd_out_w, m_mla_in_w, m_mla_q_norm_w, m_mla_q_up_w, m_mla_kv_norm_w, m_mla_kv_up_w, m_mla_out_w, m_gdn_in_w, m_gdn_conv_w, m_gdn_a_log, m_gdn_dt_bias, m_gdn_norm_w, m_gdn_out_w, m_ln_g, m_ln_b, v_ssd_in_w, v_ssd_conv_w, v_ssd_conv_b, v_ssd_dt_bias, v_ssd_a_log, v_ssd_d, v_ssd_norm_w, v_ssd_out_w, v_mla_in_w, v_mla_q_norm_w, v_mla_q_up_w, v_mla_kv_norm_w, v_mla_kv_up_w, v_mla_out_w, v_gdn_in_w, v_gdn_conv_w, v_gdn_a_log, v_gdn_dt_bias, v_gdn_norm_w, v_gdn_out_w, v_ln_g, v_ln_b):
    loc = locals()
    w = {n: loc[n] for n in WEIGHTS}
    m = {n: loc["m_" + n] for n in WEIGHTS}
    v = {n: loc["v_" + n] for n in WEIGHTS}
    xs, pos, tgt = x[0], positions[0], loss_target[0]

    def names_of(prefix):
        return [n for n in SHARDED if n.startswith(prefix + "_")]

    def shard(n, j):
        return w[n][j].astype(WIRE) if SHARDED[n][1] else w[n][j]

    def full(n, gathered):
        return _unshard(gathered, SHARDED[n][0])

    def slots(n, g):
        return _to_parts(g, SHARDED[n][0]).astype(WIRE if SHARDED[n][1] else F32)

    def replicated(prefix, j):
        return {n[len(prefix) + 1:]: w[n][j] for n in REPLICATED if n.startswith(prefix + "_")}

    def late_weights(prefix, j, keys):
        return [shard(prefix + "_" + k, j) for k in keys], lambda got: {k: full(prefix + "_" + k, g) for k, g in zip(keys, got)}

    lg = lambda i: w["ln_g"][i].reshape(1, D_MODEL)
    lb = lambda i: w["ln_b"][i].reshape(1, D_MODEL)
    cosf, sinf = _rope_tables(pos)
    ssd_late = ("conv_w", "conv_b", "norm_w", "out_w")

    w_s0 = dict(replicated("ssd", 0), in_w=full("ssd_in_w", _gather_two_level(shard("ssd_in_w", 0), name="gather_ssd0")))
    h1, s0, got = _ssd_layer_fwd(
        xs, w_s0, lg(0), lb(0), "ssd0", late=late_weights("ssd", 0, ssd_late),
        rides={"conv": [shard("mla_q_up_w", 0), shard("mla_kv_up_w", 0)], "scan": [shard("mla_in_w", 0)],
               "out": [shard("mla_out_w", 0)]})
    w_m0 = dict(replicated("mla", 0), q_up_w=full("mla_q_up_w", got["conv"][0]), kv_up_w=full("mla_kv_up_w", got["conv"][1]),
                in_w=full("mla_in_w", got["scan"][0]), out_w=full("mla_out_w", got["out"][0]))
    h2, s1, got = _mla_layer_fwd(h1, w_m0, lg(1), lb(1), cosf, sinf, "mla0", rides={"attn": [shard("gdn_in_w", 0)]})
    w_g0 = dict(replicated("gdn", 0), in_w=full("gdn_in_w", got["attn"][0]))
    h3, s2, got = _gdn_layer_fwd(h2, w_g0, lg(2), lb(2), "gdn0", late=late_weights("gdn", 0, ("conv_w", "out_w")),
                                 rides={"delta": [shard("ssd_in_w", 1)]})
    w_s1 = dict(replicated("ssd", 1), in_w=full("ssd_in_w", got["delta"][0]))
    h4, s3, _ = _ssd_layer_fwd(h3, w_s1, lg(3), lb(3), "ssd1", late=late_weights("ssd", 1, ssd_late))
    loss_tile, dl = _loss_head(h4, tgt, name="loss_head")

    dr3, du3, g3, _, _ = _ssd_layer_bwd(s3, w_s1, lg(3), jnp.zeros_like(dl), dl, "ssd1")
    dr2, du2, g2, got = _gdn_layer_bwd(s2, w_g0, lg(2), dr3, du3, "gdn0", own=lambda k, g: slots("gdn_" + k, g),
                                       rides={"delta_b": [slots(n, g3[n[4:]]) for n in names_of("ssd")]})
    r3, r2_out = got["delta_b"][:-1], got["delta_b"][-1]
    gin = slots("gdn_in_w", g2["in_w"])
    half = gin.shape[1] // 2
    dr1, du1, g1, got = _mla_layer_bwd(
        s1, w_m0, lg(1), cosf, sinf, dr2, du2, "mla0",
        rides={"attn_bq": [gin[:, :half]], "attn_bkv": [gin[:, half:], slots("gdn_conv_w", g2["conv_w"])]})
    r2 = {"gdn_in_w": jnp.concatenate([got["attn_bq"][0], got["attn_bkv"][0]], axis=1), "gdn_conv_w": got["attn_bkv"][1],
          "gdn_out_w": r2_out}
    dr0, du0, g0, got, r0 = _ssd_layer_bwd(s0, w_s0, lg(0), dr1, du1, "ssd0", own=lambda k, g: slots("ssd_" + k, g),
                                           rides={"scan_b": [slots(n, g1[n[4:]]) for n in names_of("mla")]})
    r1 = got["scan_b"]
    grad_x = _axpy(dr0, du0, name="grad_x")[None]

    gsmall = {"ssd_" + k: jnp.stack([g0[k], g3[k]]) for k in ("dt_bias", "a_log", "d")}
    gsmall.update({"mla_" + k: g1[k][None] for k in ("q_norm_w", "kv_norm_w")})
    gsmall.update({"gdn_" + k: g2[k][None] for k in ("a_log", "dt_bias", "norm_w")})
    gsmall["ln_g"] = jnp.stack([g0["ln_g"], g1["ln_g"], g2["ln_g"], g3["ln_g"]])
    gsmall["ln_b"] = jnp.stack([g0["ln_b"], g1["ln_b"], g2["ln_b"], g3["ln_b"]])
    small = _pack_small(gsmall)
    rsmall, = _exchange([jnp.broadcast_to(small[None], (N_DEV,) + small.shape)], True, name="gather_small_grads")

    recvd = {n: jnp.stack([r0[n[4:]], b], axis=1) for n, b in zip(names_of("ssd"), r3)}
    recvd.update({n: a[:, None] for n, a in zip(names_of("mla"), r1)})
    recvd.update({n: a[:, None] for n, a in r2.items()})
    recvd = [recvd[n] for n in SHARDED] + [rsmall]

    grads, deltas, new_m, new_v = {}, {}, {}, {}
    for n, pt in zip(SHARDED, recvd[:-1]):
        shp = w[n].shape
        r2d = (-1, shp[-1])
        outs = _adamw(w[n].reshape(r2d), m[n].reshape(r2d), v[n].reshape(r2d), pt.reshape((N_DEV,) + w[n].reshape(r2d).shape),
                      name="adamw_" + n)
        grads[n], deltas[n], new_m[n], new_v[n] = (o.reshape(shp) for o in outs)
    outs = _adamw(_pack_small(w), _pack_small(m), _pack_small(v), recvd[-1], name="adamw_replicated")
    for dst, o in zip((grads, deltas, new_m, new_v), outs):
        dst.update(_unpack_small(o, w))

    loss = lax.psum(loss_tile[0, 0], ("x", "y", "c"))
    return (loss, grad_x, *[grads[n] for n in WEIGHTS], *[deltas[n] for n in WEIGHTS],
            *[new_m[n] for n in WEIGHTS], *[new_v[n] for n in WEIGHTS])
```

```python
import functools

import jax
import jax.numpy as jnp
from jax import lax
from jax.experimental import pallas as pl
from jax.experimental.pallas import tpu as pltpu

F32 = jnp.float32
MXU = jnp.bfloat16
WIRE = jnp.bfloat16
HI = lax.Precision.HIGHEST

N_DEV = 8
LANES = 128
VMEM_LIMIT = 56 * 1024 * 1024

D_MODEL = 2048
DEPTH = 4
ALPHA = (2.0 * DEPTH) ** 0.25
LN_EPS = 1e-5
RMS_EPS = 1e-6

SSD_DI = 4096
SSD_P = 64
SSD_H = 64
SSD_G = 8
SSD_N = 128
SSD_L = 128
SSD_GS = SSD_DI // SSD_G
SSD_CONV_DIM = SSD_DI + 2 * SSD_G * SSD_N
SSD_PROJ_PAD = SSD_DI + SSD_CONV_DIM + LANES

MLA_H = 16
MLA_QR = 768
MLA_KVR = 512
MLA_NOPE = 128
MLA_ROPE = 64
MLA_V = 128
MLA_GATE = MLA_H * MLA_V
MLA_PROJ_PAD = MLA_QR + MLA_KVR + LANES + MLA_GATE
MLA_SCALE = (MLA_NOPE + MLA_ROPE) ** -0.5
ROPE_THETA = 10000.0
ATT_BLK = 512

GDN_HK = 16
GDN_HV = 32
GDN_DK = 128
GDN_DV = 128
GDN_KEY = GDN_HK * GDN_DK
GDN_VAL = GDN_HV * GDN_DV
GDN_L = 128
GDN_CONV_DIM = 2 * GDN_KEY + GDN_VAL
GDN_PROJ_PAD = 12800

ADAM_LR = 0.001
ADAM_B1 = 0.9
ADAM_B2 = 0.999
ADAM_EPS = 1e-08
ADAM_WD = 0.01
ADAM_STEP = 10


def _cparams(sem=None):
    return pltpu.CompilerParams(dimension_semantics=sem, vmem_limit_bytes=VMEM_LIMIT)


def _tile(n, cap):
    if n <= cap:
        return n
    best = None
    for d in range(LANES, cap + 1, LANES):
        if n % d == 0:
            best = d
    assert best is not None, (n, cap)
    return best


def _dg(a, b, ca, cb, prec=None):
    return lax.dot_general(a, b, (((ca,), (cb,)), ((), ())), preferred_element_type=F32, precision=prec)


def _mx(a):
    return a.astype(MXU)


@jax.custom_vjp
def _nn(a, b):
    return _dg(_mx(a), _mx(b), 1, 0)


def _nn_f(a, b):
    return _nn(a, b), (a, b)


def _nn_b(res, ct):
    a, b = res
    return _dg(_mx(ct), _mx(b), 1, 1), _dg(_mx(a), _mx(ct), 0, 0)


_nn.defvjp(_nn_f, _nn_b)


@jax.custom_vjp
def _nt(a, b):
    return _dg(_mx(a), _mx(b), 1, 1)


def _nt_f(a, b):
    return _nt(a, b), (a, b)


def _nt_b(res, ct):
    a, b = res
    return _dg(_mx(ct), _mx(b), 1, 0), _dg(_mx(ct), _mx(a), 0, 0)


_nt.defvjp(_nt_f, _nt_b)


@jax.custom_vjp
def _tn(a, b):
    return _dg(_mx(a), _mx(b), 0, 0)


def _tn_f(a, b):
    return _tn(a, b), (a, b)


def _tn_b(res, ct):
    a, b = res
    return _dg(_mx(b), _mx(ct), 1, 1), _dg(_mx(a), _mx(ct), 1, 0)


_tn.defvjp(_tn_f, _tn_b)


def _softplus(x):
    return jnp.maximum(x, 0.0) + jnp.log(1.0 + jnp.exp(-jnp.abs(x)))


def _silu(x):
    return x * jax.nn.sigmoid(x)


MM_TM = 1024
MM_TN = 1280
MM_VMEM_BUDGET = 40 * 1024 * 1024


def _mm(a, b, *, ta=False, tb=False, out_dtype=F32, name, ride=None):
    if ta:
        kdim, m = a.shape
    else:
        m, kdim = a.shape
    if tb:
        n, kb = b.shape
    else:
        kb, n = b.shape
    assert kdim == kb, (a.shape, b.shape, ta, tb)
    tm, tn = _tile(m, MM_TM), _tile(n, MM_TN)
    abytes, bbytes, obytes = a.dtype.itemsize, b.dtype.itemsize, jnp.dtype(out_dtype).itemsize
    tk = LANES
    for d in range(LANES, kdim + 1, LANES):
        if kdim % d == 0 and 2 * d * (tm * abytes + tn * bbytes) + tm * tn * (2 * obytes + 4) <= MM_VMEM_BUDGET:
            tk = d
    nk = kdim // tk
    ca, cb = (0 if ta else 1), (1 if tb else 0)

    def body(a_ref, b_ref, o_ref, *acc):
        part = _dg(_mx(a_ref[...]), _mx(b_ref[...]), ca, cb)
        if nk == 1:
            o_ref[...] = part.astype(out_dtype)
            return
        acc_ref, = acc
        k = pl.program_id(2)

        @pl.when(k == 0)
        def _():
            acc_ref[...] = part

        @pl.when(k > 0)
        def _():
            acc_ref[...] += part

        @pl.when(k == nk - 1)
        def _():
            o_ref[...] = acc_ref[...].astype(out_dtype)

    a_spec = pl.BlockSpec((tk, tm), lambda i, j, k: (k, i)) if ta else pl.BlockSpec((tm, tk), lambda i, j, k: (i, k))
    b_spec = pl.BlockSpec((tn, tk), lambda i, j, k: (j, k)) if tb else pl.BlockSpec((tk, tn), lambda i, j, k: (k, j))
    (out,), rode = _hosted_call(
        body, name=name, grid=(m // tm, n // tn, nk),
        in_specs=[a_spec, b_spec], out_specs=[pl.BlockSpec((tm, tn), lambda i, j, k: (i, j))],
        out_shape=[jax.ShapeDtypeStruct((m, n), out_dtype)],
        scratch_shapes=[pltpu.VMEM((tm, tn), F32)] if nk > 1 else [],
        sem=("parallel", "parallel", "arbitrary"), args=(a, b), ride=ride)
    return out if ride is None else (out, rode)


def _ln_fwd(h, y, g, b, name):
    t, d = h.shape
    tr = _tile(t, 256)

    def body(h_ref, y_ref, g_ref, b_ref, o_ref):
        r = ALPHA * h_ref[...] + y_ref[...]
        mu = jnp.mean(r, -1, keepdims=True)
        xc = r - mu
        var = jnp.mean(xc * xc, -1, keepdims=True)
        o_ref[...] = xc * lax.rsqrt(var + LN_EPS) * g_ref[...] + b_ref[...]

    row = pl.BlockSpec((tr, d), lambda i: (i, 0))
    par = pl.BlockSpec((1, d), lambda i: (0, 0))
    return pl.pallas_call(
        body, name=name, grid=(t // tr,), in_specs=[row, row, par, par], out_specs=row,
        out_shape=jax.ShapeDtypeStruct((t, d), F32), compiler_params=_cparams(("parallel",)),
    )(h, y, g, b)


def _ln_bwd(h, y, g, dr_up, du_up, name):
    t, d = h.shape
    tr = _tile(t, 256)

    def body(h_ref, y_ref, g_ref, dr_ref, du_ref, o_ref, dg_ref, db_ref):
        i = pl.program_id(0)

        @pl.when(i == 0)
        def _():
            dg_ref[...] = jnp.zeros_like(dg_ref)
            db_ref[...] = jnp.zeros_like(db_ref)

        dout = ALPHA * dr_ref[...] + du_ref[...]
        r = ALPHA * h_ref[...] + y_ref[...]
        mu = jnp.mean(r, -1, keepdims=True)
        xc = r - mu
        rstd = lax.rsqrt(jnp.mean(xc * xc, -1, keepdims=True) + LN_EPS)
        xh = xc * rstd
        dxh = dout * g_ref[...]
        o_ref[...] = rstd * (dxh - jnp.mean(dxh, -1, keepdims=True) - xh * jnp.mean(dxh * xh, -1, keepdims=True))
        dg_ref[...] += jnp.sum(dout * xh, 0, keepdims=True)
        db_ref[...] += jnp.sum(dout, 0, keepdims=True)

    row = pl.BlockSpec((tr, d), lambda i: (i, 0))
    par = pl.BlockSpec((1, d), lambda i: (0, 0))
    return pl.pallas_call(
        body, name=name, grid=(t // tr,), in_specs=[row, row, par, row, row], out_specs=[row, par, par],
        out_shape=[jax.ShapeDtypeStruct((t, d), F32), jax.ShapeDtypeStruct((1, d), F32), jax.ShapeDtypeStruct((1, d), F32)],
        compiler_params=_cparams(("arbitrary",)),
    )(h, y, g, dr_up, du_up)


def _loss_head(h, tgt, name):
    t, d = h.shape
    tr = _tile(t, 256)

    def body(h_ref, t_ref, l_ref, d_ref):
        i = pl.program_id(0)

        @pl.when(i == 0)
        def _():
            l_ref[...] = jnp.zeros_like(l_ref)

        e = h_ref[...] - t_ref[...]
        d_ref[...] = e * (1.0 / d)
        l_ref[...] += 0.5 * jnp.sum(jnp.mean(e * e, -1, keepdims=True))

    row = pl.BlockSpec((tr, d), lambda i: (i, 0))
    return pl.pallas_call(
        body, name=name, grid=(t // tr,), in_specs=[row, row],
        out_specs=[pl.BlockSpec((8, LANES), lambda i: (0, 0)), row],
        out_shape=[jax.ShapeDtypeStruct((8, LANES), F32), jax.ShapeDtypeStruct((t, d), F32)],
        compiler_params=_cparams(("arbitrary",)),
    )(h, tgt)


def _axpy(dr, du, name):
    t, d = dr.shape
    tr = _tile(t, 256)

    def body(a_ref, b_ref, o_ref):
        o_ref[...] = ALPHA * a_ref[...] + b_ref[...]

    row = pl.BlockSpec((tr, d), lambda i: (i, 0))
    return pl.pallas_call(
        body, name=name, grid=(t // tr,), in_specs=[row, row], out_specs=row,
        out_shape=jax.ShapeDtypeStruct((t, d), F32), compiler_params=_cparams(("parallel",)),
    )(dr, du)


CONV_TT = 512
CONV_TC = 512


HALO = 8


def _shift_down(cur, halo, s, row):
    if s == 0:
        return cur
    tt = cur.shape[0]
    edge = pltpu.roll(halo, s, 0)
    if tt > HALO:
        edge = jnp.concatenate([edge, jnp.zeros((tt - HALO, cur.shape[1]), cur.dtype)], axis=0)
    return jnp.where(row >= s, pltpu.roll(cur, s, 0), edge)


def _shift_up(cur, halo, s, row, tt):
    if s == 0:
        return cur
    edge = jnp.concatenate([jnp.zeros((tt - HALO, cur.shape[1]), cur.dtype), pltpu.roll(halo, HALO - s, 0)], axis=0)
    return jnp.where(row < tt - s, pltpu.roll(cur, tt - s, 0), edge)


def _conv_fwd(proj, col0, w, b, name, ride=None):
    t = proj.shape[0]
    c = w.shape[1]
    tt = _tile(t, CONV_TT)
    cb0 = col0 // CONV_TC

    def body(x_ref, p_ref, w_ref, b_ref, o_ref):
        i = pl.program_id(1)
        x = x_ref[...]
        p = jnp.where(i > 0, p_ref[...], 0.0)
        row = lax.broadcasted_iota(jnp.int32, x.shape, 0)
        pre = b_ref[...] + w_ref[3:4, :] * x
        for s in (1, 2, 3):
            pre = pre + w_ref[3 - s:4 - s, :] * _shift_down(x, p, s, row)
        o_ref[...] = _silu(pre)

    (act,), rode = _hosted_call(
        body, name=name, grid=(c // CONV_TC, t // tt),
        in_specs=[pl.BlockSpec((tt, CONV_TC), lambda j, i: (i, cb0 + j)),
                  pl.BlockSpec((HALO, CONV_TC), lambda j, i: (jnp.maximum(i * (tt // HALO) - 1, 0), cb0 + j)),
                  pl.BlockSpec((4, CONV_TC), lambda j, i: (0, j)),
                  pl.BlockSpec((1, CONV_TC), lambda j, i: (0, j))],
        out_specs=[pl.BlockSpec((tt, CONV_TC), lambda j, i: (i, j))],
        out_shape=[jax.ShapeDtypeStruct((t, c), F32)], scratch_shapes=[],
        sem=("parallel", "parallel"), args=(proj, proj, w, b), ride=ride)
    return act, rode


def _conv_bwd(proj, col0, w, b, dact, name):
    t = proj.shape[0]
    c = w.shape[1]
    tt = _tile(t, CONV_TT)
    nt = t // tt
    cb0 = col0 // CONV_TC
    per = tt // HALO

    def dpre_of(x, halo, d, w_ref, b_ref):
        row = lax.broadcasted_iota(jnp.int32, x.shape, 0)
        sh = [_shift_down(x, halo, s, row) for s in range(4)]
        pre = b_ref[...] + w_ref[3:4, :] * sh[0]
        for s in (1, 2, 3):
            pre = pre + w_ref[3 - s:4 - s, :] * sh[s]
        sg = jax.nn.sigmoid(pre)
        return d * (sg * (1.0 + pre * (1.0 - sg))), sh

    def body(x_ref, p_ref, n_ref, w_ref, b_ref, d_ref, dn_ref, dx_ref, dw_ref, db_ref):
        i = pl.program_id(1)

        @pl.when(i == 0)
        def _():
            dw_ref[...] = jnp.zeros_like(dw_ref)
            db_ref[...] = jnp.zeros_like(db_ref)

        x = x_ref[...]
        dpre, sh = dpre_of(x, jnp.where(i > 0, p_ref[...], 0.0), d_ref[...], w_ref, b_ref)
        dnext, _ = dpre_of(n_ref[...], x[tt - HALO:, :], jnp.where(i < nt - 1, dn_ref[...], 0.0), w_ref, b_ref)
        row = lax.broadcasted_iota(jnp.int32, x.shape, 0)
        acc = w_ref[3:4, :] * dpre
        for s in (1, 2, 3):
            acc = acc + w_ref[3 - s:4 - s, :] * _shift_up(dpre, dnext, s, row, tt)
        dx_ref[...] = acc
        for s in range(4):
            dw_ref[3 - s:4 - s, :] += jnp.sum(dpre * sh[s], 0, keepdims=True)
        db_ref[...] += jnp.sum(dpre, 0, keepdims=True)

    tile = lambda off: pl.BlockSpec((tt, CONV_TC), lambda j, i: (i, off + j))
    before = lambda off: pl.BlockSpec((HALO, CONV_TC), lambda j, i: (jnp.maximum(i * per - 1, 0), off + j))
    after = lambda off: pl.BlockSpec((HALO, CONV_TC), lambda j, i: (jnp.minimum((i + 1) * per, t // HALO - 1), off + j))
    wspec = lambda rows: pl.BlockSpec((rows, CONV_TC), lambda j, i: (0, j))
    return pl.pallas_call(
        body, name=name, grid=(c // CONV_TC, nt),
        in_specs=[tile(cb0), before(cb0), after(cb0), wspec(4), wspec(1), tile(0), after(0)],
        out_specs=[tile(0), wspec(4), wspec(1)],
        out_shape=[jax.ShapeDtypeStruct((t, c), F32), jax.ShapeDtypeStruct((4, c), F32), jax.ShapeDtypeStruct((1, c), F32)],
        compiler_params=_cparams(("parallel", "arbitrary")),
    )(proj, proj, proj, w, b, dact, dact)


SSD_GB = 4


def _split3(a):
    a1 = _mx(a)
    r = a - a1.astype(F32)
    a2 = _mx(r)
    return a1, a2, _mx(r - a2.astype(F32))


@jax.custom_vjp
def _sel_r(a, c):
    cm = _mx(c)
    p1, p2, p3 = _split3(a)
    return _dg(p1, cm, 1, 0) + (_dg(p2, cm, 1, 0) + _dg(p3, cm, 1, 0))


def _sel_r_f(a, c):
    return _sel_r(a, c), c


def _sel_r_b(c, ct):
    cm = _mx(c)
    p1, p2, p3 = _split3(ct)
    return _dg(p1, cm, 1, 1) + (_dg(p2, cm, 1, 1) + _dg(p3, cm, 1, 1)), jnp.zeros_like(c)


_sel_r.defvjp(_sel_r_f, _sel_r_b)


@jax.custom_vjp
def _sel_l(c, a):
    cm = _mx(c)
    p1, p2, p3 = _split3(a)
    return _dg(cm, p1, 1, 0) + (_dg(cm, p2, 1, 0) + _dg(cm, p3, 1, 0))


def _sel_l_f(c, a):
    return _sel_l(c, a), c


def _sel_l_b(c, ct):
    cm = _mx(c)
    p1, p2, p3 = _split3(ct)
    return jnp.zeros_like(c), _dg(cm, p1, 0, 0) + (_dg(cm, p2, 0, 0) + _dg(cm, p3, 0, 0))


_sel_l.defvjp(_sel_l_f, _sel_l_b)


def _ssd_chunk(gb, x, z, bm, cm, dtraw, dtb, alog, dsk, nw, prev):
    L = x.shape[1]
    r_i = lax.broadcasted_iota(jnp.int32, (L, L), 0)
    c_i = lax.broadcasted_iota(jnp.int32, (L, L), 1)
    causal = r_i >= c_i
    dt = _softplus(dtraw + dtb)
    a = dt * (-jnp.exp(alog))
    acs = _sel_l(causal.astype(F32), a)
    e_r = lax.broadcasted_iota(jnp.int32, (LANES, SSD_GS), 0)
    e_c = lax.broadcasted_iota(jnp.int32, (LANES, SSD_GS), 1)
    hpg = SSD_H // SSD_G
    sels = [(e_r == (gb * SSD_GB + i) * hpg + jnp.right_shift(e_c, 6)).astype(F32) for i in range(SSD_GB)]
    dt_x = _cat0([_sel_r(dt, s) for s in sels])
    acs_x = _cat0([_sel_r(acs, s) for s in sels])
    d_x = _cat0([_sel_r(jnp.broadcast_to(dsk, (8, LANES)), s)[0:1] for s in sels])
    last = lax.broadcasted_iota(jnp.int32, (L, 1), 0) == L - 1
    alast = jnp.sum(jnp.where(last, acs_x, 0.0), axis=1, keepdims=True)
    xdt = x * dt_x
    cb = _bnt(cm, bm)
    lane = lax.broadcasted_iota(jnp.int32, (L, LANES), 1)
    ys = []
    for j in range(SSD_GS // LANES):
        xp = xdt[:, :, j * LANES:(j + 1) * LANES]
        yp = None
        for hh in range(2):
            c0 = (2 * j + hh) * SSD_P
            cmx = jnp.broadcast_to(acs_x[:, :, c0:c0 + 1], (SSD_GB, L, L))
            dec = jnp.exp(jnp.where(causal, cmx - jnp.swapaxes(cmx, 1, 2), -jnp.inf))
            half = (lane < SSD_P) if hh == 0 else (lane >= SSD_P)
            t = _bnn(cb * dec, jnp.where(half, xp, 0.0))
            yp = t if yp is None else yp + t
        ys.append(yp)
    y_diag = jnp.concatenate(ys, axis=2)
    st = _btn(bm, xdt * jnp.exp(alast - acs_x))
    new = prev * jnp.exp(alast) + st
    y_off = _bnn(cm, prev) * jnp.exp(acs_x)
    y = y_diag + y_off + x * d_x
    yg = y * _silu(z)
    yn = yg * lax.rsqrt(jnp.mean(yg * yg, -1, keepdims=True) + RMS_EPS) * nw
    return yn, new


def _groups_of(ref, width):
    return _cat0([ref[:, i * width:(i + 1) * width] for i in range(SSD_GB)])


def _put_groups(ref, val, width):
    for i in range(SSD_GB):
        ref[:, i * width:(i + 1) * width] = val[i].astype(ref.dtype)


def _ssd_specs(nc, rev):
    cc = (lambda c: nc - 1 - c) if rev else (lambda c: c)
    wx, wb = SSD_GB * SSD_GS, SSD_GB * SSD_N
    dtb = (SSD_DI + SSD_CONV_DIM) // LANES
    bb = SSD_DI // wb
    cbk = (SSD_DI + SSD_G * SSD_N) // wb
    par = pl.BlockSpec((1, LANES), lambda c, g: (0, 0))
    return dict(
        z=pl.BlockSpec((SSD_L, wx), lambda c, g: (cc(c), g)),
        dt=pl.BlockSpec((SSD_L, LANES), lambda c, g: (cc(c), dtb)),
        x=pl.BlockSpec((SSD_L, wx), lambda c, g: (cc(c), g)),
        bm=pl.BlockSpec((SSD_L, wb), lambda c, g: (cc(c), bb + g)),
        cm=pl.BlockSpec((SSD_L, wb), lambda c, g: (cc(c), cbk + g)),
        par=par,
        nw=pl.BlockSpec((1, wx), lambda c, g: (0, g)),
        st=pl.BlockSpec((1, SSD_GB, SSD_N, SSD_GS), lambda c, g: (cc(c), g, 0, 0)),
        y=pl.BlockSpec((SSD_L, wx), lambda c, g: (cc(c), g)),
        bc=pl.BlockSpec((SSD_L, wb), lambda c, g: (cc(c), g)),
        dtout=pl.BlockSpec((SSD_L, LANES), lambda c, g: (cc(c), 0)),
    )


def _ssd_fwd(proj, act, dtb, alog, dsk, nw, name, ride=None):
    t = proj.shape[0]
    nc = t // SSD_L
    sp = _ssd_specs(nc, False)

    def body(z_ref, dt_ref, x_ref, bm_ref, cm_ref, dtb_ref, alog_ref, dsk_ref, nw_ref, y_ref, st_ref, state):
        c, g = pl.program_id(0), pl.program_id(1)
        mine = pl.ds(g * SSD_GB, SSD_GB)

        @pl.when(c == 0)
        def _():
            state[mine] = jnp.zeros((SSD_GB, SSD_N, SSD_GS), F32)

        prev = state[mine]
        st_ref[0] = prev
        yn, new = _ssd_chunk(g, _groups_of(x_ref, SSD_GS), _groups_of(z_ref, SSD_GS), _groups_of(bm_ref, SSD_N),
                             _groups_of(cm_ref, SSD_N), dt_ref[...], dtb_ref[...], alog_ref[...], dsk_ref[...],
                             _groups_of(nw_ref, SSD_GS), prev)
        _put_groups(y_ref, yn, SSD_GS)
        state[mine] = new

    return _hosted_call(
        body, name=name, grid=(nc, SSD_G // SSD_GB),
        in_specs=[sp["z"], sp["dt"], sp["x"], sp["bm"], sp["cm"], sp["par"], sp["par"], sp["par"], sp["nw"]],
        out_specs=[sp["y"], sp["st"]],
        out_shape=[jax.ShapeDtypeStruct((t, SSD_DI), F32), jax.ShapeDtypeStruct((nc, SSD_G, SSD_N, SSD_GS), F32)],
        scratch_shapes=[pltpu.VMEM((SSD_G, SSD_N, SSD_GS), F32)],
        sem=("arbitrary", "arbitrary"), args=(proj, proj, act, act, act, dtb, alog, dsk, nw), ride=ride)


def _ssd_bwd(proj, act, dtb, alog, dsk, nw, states, dyn, name, ride=None):
    t = proj.shape[0]
    nc = t // SSD_L
    sp = _ssd_specs(nc, True)

    def body(z_ref, dt_ref, x_ref, bm_ref, cm_ref, dtb_ref, alog_ref, dsk_ref, nw_ref, st_ref, dy_ref,
             dact_ref, dz_ref, ddt_ref, ddtb_ref, dalog_ref, ddsk_ref, dnw_ref, dstate):
        c, g = pl.program_id(0), pl.program_id(1)
        mine = pl.ds(g * SSD_GB, SSD_GB)

        @pl.when(c == 0)
        def _():
            dstate[mine] = jnp.zeros((SSD_GB, SSD_N, SSD_GS), F32)

        @pl.when((c == 0) & (g == 0))
        def _():
            ddtb_ref[...] = jnp.zeros_like(ddtb_ref)
            dalog_ref[...] = jnp.zeros_like(dalog_ref)
            ddsk_ref[...] = jnp.zeros_like(ddsk_ref)
            dnw_ref[...] = jnp.zeros_like(dnw_ref)

        @pl.when(g == 0)
        def _():
            ddt_ref[...] = jnp.zeros_like(ddt_ref)

        _, vjp = jax.vjp(functools.partial(_ssd_chunk, g), _groups_of(x_ref, SSD_GS), _groups_of(z_ref, SSD_GS),
                         _groups_of(bm_ref, SSD_N), _groups_of(cm_ref, SSD_N), dt_ref[...], dtb_ref[...], alog_ref[...],
                         dsk_ref[...], _groups_of(nw_ref, SSD_GS), st_ref[0])
        dx, dz, dbm, dcm, ddt, ddtb, dalog, ddsk, dnw, dprev = vjp((_groups_of(dy_ref, SSD_GS), dstate[mine]))
        _put_groups(dz_ref, dz, SSD_GS)
        for gi in range(SSD_G // SSD_GB):
            @pl.when(g == gi)
            def _(gi=gi):
                for i in range(SSD_GB):
                    gg = gi * SSD_GB + i
                    dact_ref[:, gg * SSD_GS:(gg + 1) * SSD_GS] = dx[i]
                    dact_ref[:, SSD_DI + gg * SSD_N:SSD_DI + (gg + 1) * SSD_N] = dbm[i]
                    dact_ref[:, SSD_DI + (SSD_G + gg) * SSD_N:SSD_DI + (SSD_G + gg + 1) * SSD_N] = dcm[i]
        ddt_ref[...] += ddt
        ddtb_ref[...] += ddtb
        dalog_ref[...] += dalog
        ddsk_ref[...] += ddsk
        dnw_ref[mine] += dnw
        dstate[mine] = dprev

    par_out = pl.BlockSpec((1, LANES), lambda c, g: (0, 0))
    sds = jax.ShapeDtypeStruct
    return _hosted_call(
        body, name=name, grid=(nc, SSD_G // SSD_GB),
        in_specs=[sp["z"], sp["dt"], sp["x"], sp["bm"], sp["cm"], sp["par"], sp["par"], sp["par"], sp["nw"],
                  sp["st"], sp["y"]],
        out_specs=[pl.BlockSpec((SSD_L, SSD_CONV_DIM), lambda c, g: (nc - 1 - c, 0)), sp["y"], sp["dtout"],
                   par_out, par_out, par_out, pl.BlockSpec((SSD_G, 1, SSD_GS), lambda c, g: (0, 0, 0))],
        out_shape=[sds((t, SSD_CONV_DIM), F32), sds((t, SSD_DI), F32), sds((t, LANES), F32), sds((1, LANES), F32),
                   sds((1, LANES), F32), sds((1, LANES), F32), sds((SSD_G, 1, SSD_GS), F32)],
        scratch_shapes=[pltpu.VMEM((SSD_G, SSD_N, SSD_GS), F32)],
        sem=("arbitrary", "arbitrary"), args=(proj, proj, act, act, act, dtb, alog, dsk, nw, states, dyn), ride=ride)


def _pad_lanes(v, width=LANES, offset=0):
    return jnp.pad(v.astype(F32), (offset, width - offset - v.shape[0])).reshape(1, width)


def _ride_of(rides, key, scatter):
    arrs = (rides or {}).get(key)
    return (arrs, scatter) if arrs else None


def _mm_r(a, b, ride, **kw):
    out = _mm(a, b, ride=ride, **kw)
    return (out, []) if ride is None else out


def _ssd_layer_fwd(u, w, ln_g, ln_b, tag, rides=None, late=None):
    w_in = jnp.concatenate([w["in_w"], jnp.zeros((D_MODEL, LANES - SSD_H), w["in_w"].dtype)], axis=1)
    dtb, alog, dsk = _pad_lanes(w["dt_bias"]), _pad_lanes(w["a_log"]), _pad_lanes(w["d"])
    if late is None:
        proj = _mm(u, w_in, name=tag + "_in")
    else:
        proj, got = _mm(u, w_in, name=tag + "_in", ride=(late[0], False))
        w.update(late[1](got))
    nw = w["norm_w"].reshape(1, SSD_DI)
    cb = w["conv_b"].reshape(1, SSD_CONV_DIM)
    rode = {}
    act, rode["conv"] = _conv_fwd(proj, SSD_DI, w["conv_w"], cb, name=tag + "_conv", ride=_ride_of(rides, "conv", False))
    (yn, states), rode["scan"] = _ssd_fwd(proj, act, dtb, alog, dsk, nw, name=tag + "_scan", ride=_ride_of(rides, "scan", False))
    y, rode["out"] = _mm_r(yn, w["out_w"], _ride_of(rides, "out", False), name=tag + "_out")
    h = _ln_fwd(u, y, ln_g, ln_b, name=tag + "_ln")
    saved = dict(u=u, w_in=w_in, proj=proj, act=act, states=states, yn=yn, y=y, dtb=dtb, alog=alog, dsk=dsk, nw=nw, cb=cb)
    return h, saved, rode


def _ssd_layer_bwd(s, w, ln_g, dr_up, du_up, tag, rides=None, own=None):
    dr, dg, db = _ln_bwd(s["u"], s["y"], ln_g, dr_up, du_up, name=tag + "_ln_b")
    dyn = _mm(dr, w["out_w"], tb=True, name=tag + "_out_bx")
    d_out_w = _mm(s["yn"], dr, ta=True, out_dtype=WIRE, name=tag + "_out_bw")
    rode = {}
    (dact, dz, ddt, ddtb, dalog, ddsk, dnw), rode["scan_b"] = _ssd_bwd(
        s["proj"], s["act"], s["dtb"], s["alog"], s["dsk"], s["nw"], s["states"], dyn, name=tag + "_scan_b",
        ride=_ride_of(rides, "scan_b", True))
    dxbc, d_conv_w, d_conv_b = _conv_bwd(s["proj"], SSD_DI, w["conv_w"], s["cb"], dact, name=tag + "_conv_b")
    dproj = jnp.concatenate([dz, dxbc, ddt], axis=1)
    grads = dict(conv_w=d_conv_w, conv_b=d_conv_b.reshape(-1), dt_bias=ddtb[0, :SSD_H], a_log=dalog[0, :SSD_H],
                 d=ddsk[0, :SSD_H], norm_w=dnw.reshape(-1), out_w=d_out_w, ln_g=dg[0], ln_b=db[0])
    late_names = ("conv_w", "conv_b", "norm_w", "out_w")
    got = {}
    if own is None:
        d_in_w = _mm(s["u"], dproj, ta=True, out_dtype=WIRE, name=tag + "_in_bw")
        du = _mm(dproj, s["w_in"], tb=True, name=tag + "_in_bx")
        grads["in_w"] = d_in_w[:, :SSD_DI + SSD_CONV_DIM + SSD_H]
    else:
        d_in_w, late = _mm(s["u"], dproj, ta=True, out_dtype=WIRE, name=tag + "_in_bw",
                           ride=([own(k, grads[k]) for k in late_names], True))
        got = dict(zip(late_names, late))
        grads["in_w"] = d_in_w[:, :SSD_DI + SSD_CONV_DIM + SSD_H]
        du, (got["in_w"],) = _mm(dproj, s["w_in"], tb=True, name=tag + "_in_bx", ride=([own("in_w", grads["in_w"])], True))
    return dr, du, grads, rode, got


MLA_LOW = MLA_QR + MLA_KVR + LANES
MLA_ZB = MLA_LOW // LANES


def _rope_mat():
    r = lax.broadcasted_iota(jnp.int32, (LANES, LANES), 0)
    c = lax.broadcasted_iota(jnp.int32, (LANES, LANES), 1)
    hf = MLA_ROPE // 2
    return jnp.where((c < hf) & (r == c + hf), -1.0, 0.0) + jnp.where((c >= hf) & (c < 2 * hf) & (r == c - hf), 1.0, 0.0)


def _rope(x, cosf, sinf):
    return x * cosf + _sel_r(x, _rope_mat()) * sinf


def _rope_adj(d, cosf, sinf):
    return d * cosf - _sel_r(d * sinf, _rope_mat())


def _mla_low_fn(low, qnw, kvnw, cosf, sinf):
    qc, kvc, kr = low[:, :MLA_QR], low[:, MLA_QR:MLA_QR + MLA_KVR], low[:, MLA_QR + MLA_KVR:]
    qn = qc * lax.rsqrt(jnp.mean(qc * qc, -1, keepdims=True) + RMS_EPS) * qnw
    kvn = kvc * lax.rsqrt(jnp.mean(kvc * kvc, -1, keepdims=True) + RMS_EPS) * kvnw
    return qn, kvn, _rope(kr, cosf, sinf)


def _mla_low_fwd(proj, qnw, kvnw, cosf, sinf, name):
    t = proj.shape[0]
    tr = _tile(t, 256)

    def body(low_ref, qnw_ref, kvnw_ref, cos_ref, sin_ref, qn_ref, kvn_ref, kr_ref):
        qn, kvn, kr = _mla_low_fn(low_ref[...], qnw_ref[...], kvnw_ref[...], cos_ref[...], sin_ref[...])
        qn_ref[...] = qn
        kvn_ref[...] = kvn
        kr_ref[...] = kr

    row = lambda wdt: pl.BlockSpec((tr, wdt), lambda i: (i, 0))
    par = lambda wdt: pl.BlockSpec((1, wdt), lambda i: (0, 0))
    sds = jax.ShapeDtypeStruct
    return pl.pallas_call(
        body, name=name, grid=(t // tr,),
        in_specs=[row(MLA_LOW), par(MLA_QR), par(MLA_KVR), row(LANES), row(LANES)],
        out_specs=[row(MLA_QR), row(MLA_KVR), row(LANES)],
        out_shape=[sds((t, MLA_QR), F32), sds((t, MLA_KVR), F32), sds((t, LANES), F32)],
        compiler_params=_cparams(("parallel",)),
    )(proj, qnw, kvnw, cosf, sinf)


def _mla_low_bwd(proj, qnw, kvnw, cosf, sinf, dqn, dkvn, dkr, name):
    t = proj.shape[0]
    tr = _tile(t, 256)

    def body(low_ref, qnw_ref, kvnw_ref, cos_ref, sin_ref, dqn_ref, dkvn_ref, dkr_ref, dlow_ref, dqnw_ref, dkvnw_ref):
        i = pl.program_id(0)

        @pl.when(i == 0)
        def _():
            dqnw_ref[...] = jnp.zeros_like(dqnw_ref)
            dkvnw_ref[...] = jnp.zeros_like(dkvnw_ref)

        cosf, sinf = cos_ref[...], sin_ref[...]
        _, vjp = jax.vjp(lambda a, b, c: _mla_low_fn(a, b, c, cosf, sinf), low_ref[...], qnw_ref[...], kvnw_ref[...])
        dlow, dq, dk = vjp((dqn_ref[...], dkvn_ref[...], dkr_ref[...]))
        dlow_ref[...] = dlow
        dqnw_ref[...] += dq
        dkvnw_ref[...] += dk

    row = lambda wdt: pl.BlockSpec((tr, wdt), lambda i: (i, 0))
    par = lambda wdt: pl.BlockSpec((1, wdt), lambda i: (0, 0))
    sds = jax.ShapeDtypeStruct
    return pl.pallas_call(
        body, name=name, grid=(t // tr,),
        in_specs=[row(MLA_LOW), par(MLA_QR), par(MLA_KVR), row(LANES), row(LANES), row(MLA_QR), row(MLA_KVR), row(LANES)],
        out_specs=[row(MLA_LOW), par(MLA_QR), par(MLA_KVR)],
        out_shape=[sds((t, MLA_LOW), F32), sds((1, MLA_QR), F32), sds((1, MLA_KVR), F32)],
        compiler_params=_cparams(("arbitrary",)),
    )(proj, qnw, kvnw, cosf, sinf, dqn, dkvn, dkr)


def _rope_heads(x, col_blk0, cosf, sinf, adjoint, name):
    t = x.shape[0]
    tr = _tile(t, 512)

    def body(x_ref, cos_ref, sin_ref, o_ref):
        f = _rope_adj if adjoint else _rope
        o_ref[...] = f(x_ref[...], cos_ref[...], sin_ref[...])

    tab = pl.BlockSpec((tr, LANES), lambda i, h: (i, 0))
    return pl.pallas_call(
        body, name=name, grid=(t // tr, MLA_H),
        in_specs=[pl.BlockSpec((tr, LANES), lambda i, h: (i, col_blk0 + h)), tab, tab],
        out_specs=pl.BlockSpec((tr, LANES), lambda i, h: (i, h)),
        out_shape=jax.ShapeDtypeStruct((t, MLA_H * LANES), F32), compiler_params=_cparams(("parallel", "parallel")),
    )(x, cosf, sinf)


ATT_HB = 4
ATT_W = ATT_HB * LANES


def _att_qk(qn_ref, qr_ref, kn_ref, kr_ref):
    q2 = jnp.concatenate([_heads_of(qn_ref, ATT_HB), _heads_of(qr_ref, ATT_HB)], axis=2)
    kr = kr_ref[...]
    k2 = jnp.concatenate([_heads_of(kn_ref, ATT_HB), jnp.broadcast_to(kr[None], (ATT_HB,) + kr.shape)], axis=2)
    return q2, k2


def _att_scores(q2, k2, masked):
    s = _bdg(_mx(q2 * MLA_SCALE), _mx(k2), 2, 2)
    if masked:
        tq, tk = s.shape[1:]
        s = jnp.where(lax.broadcasted_iota(jnp.int32, (tq, tk), 1) <= lax.broadcasted_iota(jnp.int32, (tq, tk), 0), s, -jnp.inf)
    return s


def _on_causal_blocks(q_blk, k_blk, step):
    @pl.when(k_blk < q_blk)
    def _():
        step(False)

    @pl.when(k_blk == q_blk)
    def _():
        step(True)


def _put_heads(ref, val):
    for b in range(val.shape[0]):
        ref[:, b * LANES:(b + 1) * LANES] = val[b].astype(ref.dtype)


def _att_ds(s, v_ref, o_ref, do_ref, lse_ref):
    do = _heads_of(do_ref, ATT_HB)
    p = jnp.exp(s - _heads_of(lse_ref, ATT_HB)[:, :, 0:1])
    dp = _bdg(_mx(do), _mx(_heads_of(v_ref, ATT_HB)), 2, 2)
    dl = jnp.sum(do * _heads_of(o_ref, ATT_HB), -1, keepdims=True)
    return _mx(p), _mx(p * (dp - dl) * MLA_SCALE), do


def _attn_fwd(q, qr, kv, kr, proj, name, ride=None):
    t = q.shape[0]
    tq = tk = _tile(t, ATT_BLK)
    nq = nk = t // tq

    def body(qn_ref, qr_ref, kn_ref, kr_ref, v_ref, *rest):
        z_refs, (o_ref, og_ref, lse_ref, m_s, l_s, acc_s) = rest[:ATT_HB], rest[ATT_HB:]
        i, j = pl.program_id(1), pl.program_id(2)

        @pl.when(j == 0)
        def _():
            m_s[...] = jnp.full_like(m_s, -jnp.inf)
            l_s[...] = jnp.zeros_like(l_s)
            acc_s[...] = jnp.zeros_like(acc_s)

        def step(masked):
            s = _att_scores(*_att_qk(qn_ref, qr_ref, kn_ref, kr_ref), masked)
            m_new = jnp.maximum(m_s[...], jnp.max(s, -1, keepdims=True))
            p = jnp.exp(s - m_new)
            corr = jnp.exp(m_s[...] - m_new)
            l_s[...] = corr * l_s[...] + jnp.sum(p, -1, keepdims=True)
            acc_s[...] = corr * acc_s[...] + _bdg(_mx(p), _mx(_heads_of(v_ref, ATT_HB)), 2, 1)
            m_s[...] = m_new

        _on_causal_blocks(i, j, step)

        @pl.when(j == nk - 1)
        def _():
            o = acc_s[...] / l_s[...]
            _put_heads(o_ref, o)
            lse = m_s[...] + jnp.log(l_s[...])
            for b in range(ATT_HB):
                og_ref[:, b * LANES:(b + 1) * LANES] = (o[b] * _silu(z_refs[b][...])).astype(og_ref.dtype)
                lse_ref[:, b * LANES:(b + 1) * LANES] = jnp.broadcast_to(lse[b], (tq, LANES))

    qs = lambda off: pl.BlockSpec((tq, ATT_W), lambda h, i, j: (i, off // ATT_HB + h))
    ks = lambda off: pl.BlockSpec((tk, ATT_W), lambda h, i, j: (jnp.minimum(j, i), off // ATT_HB + h))
    zs = [pl.BlockSpec((tq, LANES), functools.partial(lambda b, h, i, j: (i, MLA_ZB + h * ATT_HB + b), b)) for b in range(ATT_HB)]
    sds = jax.ShapeDtypeStruct
    return _hosted_call(
        body, name=name, grid=(MLA_H // ATT_HB, nq, nk),
        in_specs=[qs(0), qs(0), ks(0), pl.BlockSpec((tk, LANES), lambda h, i, j: (jnp.minimum(j, i), 0)), ks(MLA_H)] + zs,
        out_specs=[qs(0), qs(0), qs(0)],
        out_shape=[sds((t, MLA_GATE), F32), sds((t, MLA_GATE), F32), sds((t, MLA_H * LANES), F32)],
        scratch_shapes=[pltpu.VMEM((ATT_HB, tq, 1), F32), pltpu.VMEM((ATT_HB, tq, 1), F32), pltpu.VMEM((ATT_HB, tq, LANES), F32)],
        sem=("parallel", "parallel", "arbitrary"), args=(q, qr, kv, kr, kv) + (proj,) * ATT_HB, ride=ride)


def _gate_bwd(dog, o, proj, name):
    t = o.shape[0]
    tr = _tile(t, 512)

    def body(d_ref, o_ref, z_ref, do_ref, dz_ref):
        z = z_ref[...]
        sg = jax.nn.sigmoid(z)
        d = d_ref[...]
        do_ref[...] = d * z * sg
        dz_ref[...] = d * o_ref[...] * (sg * (1.0 + z * (1.0 - sg)))

    blk = lambda off: pl.BlockSpec((tr, 512), lambda i, j: (i, off + j))
    assert MLA_LOW % 512 != 0 or True
    zspec = pl.BlockSpec((tr, LANES), lambda i, j: (i, MLA_ZB + j))
    b128 = pl.BlockSpec((tr, LANES), lambda i, j: (i, j))
    sds = jax.ShapeDtypeStruct
    return pl.pallas_call(
        body, name=name, grid=(t // tr, MLA_GATE // LANES),
        in_specs=[b128, b128, zspec], out_specs=[b128, b128],
        out_shape=[sds((t, MLA_GATE), F32), sds((t, MLA_GATE), F32)],
        compiler_params=_cparams(("parallel", "parallel")),
    )(dog, o, proj)


def _attn_bwd_q(q, qr, kv, kr, o, do, lse, name, ride=None):
    t = q.shape[0]
    tq = tk = _tile(t, ATT_BLK)
    nq = nk = t // tq

    def body(qn_ref, qr_ref, kn_ref, kr_ref, v_ref, o_ref, do_ref, lse_ref, dqn_ref, dqr_ref, an_s, ar_s):
        i, j = pl.program_id(1), pl.program_id(2)

        @pl.when(j == 0)
        def _():
            an_s[...] = jnp.zeros_like(an_s)
            ar_s[...] = jnp.zeros_like(ar_s)

        def step(masked):
            q2, k2 = _att_qk(qn_ref, qr_ref, kn_ref, kr_ref)
            ds = _att_ds(_att_scores(q2, k2, masked), v_ref, o_ref, do_ref, lse_ref)[1]
            dq2 = _bdg(ds, _mx(k2), 2, 1)
            an_s[...] += dq2[:, :, :LANES]
            ar_s[...] += dq2[:, :, LANES:]

        _on_causal_blocks(i, j, step)

        @pl.when(j == nk - 1)
        def _():
            _put_heads(dqn_ref, an_s[...])
            _put_heads(dqr_ref, ar_s[...])

    qs = lambda off: pl.BlockSpec((tq, ATT_W), lambda h, i, j: (i, off // ATT_HB + h))
    ks = lambda off: pl.BlockSpec((tk, ATT_W), lambda h, i, j: (jnp.minimum(j, i), off // ATT_HB + h))
    sds = jax.ShapeDtypeStruct
    return _hosted_call(
        body, name=name, grid=(MLA_H // ATT_HB, nq, nk),
        in_specs=[qs(0), qs(0), ks(0), pl.BlockSpec((tk, LANES), lambda h, i, j: (jnp.minimum(j, i), 0)), ks(MLA_H),
                  qs(0), qs(0), qs(0)],
        out_specs=[qs(0), qs(0)],
        out_shape=[sds((t, MLA_H * LANES), F32), sds((t, MLA_H * LANES), F32)],
        scratch_shapes=[pltpu.VMEM((ATT_HB, tq, LANES), F32), pltpu.VMEM((ATT_HB, tq, LANES), F32)],
        sem=("parallel", "parallel", "arbitrary"), args=(q, qr, kv, kr, kv, o, do, lse), ride=ride)


def _attn_bwd_kv(q, qr, kv, kr, o, do, lse, name, ride=None):
    t = q.shape[0]
    tq = tk = _tile(t, ATT_BLK)
    nq = nk = t // tq

    def body(qn_ref, qr_ref, kn_ref, kr_ref, v_ref, o_ref, do_ref, lse_ref, dkn_ref, dv_ref, dkr_ref, akn_s, av_s):
        j, h, i = pl.program_id(0), pl.program_id(1), pl.program_id(2)

        @pl.when((h == 0) & (i == 0))
        def _():
            dkr_ref[...] = jnp.zeros_like(dkr_ref)

        @pl.when(i == 0)
        def _():
            akn_s[...] = jnp.zeros_like(akn_s)
            av_s[...] = jnp.zeros_like(av_s)

        def step(masked):
            q2, k2 = _att_qk(qn_ref, qr_ref, kn_ref, kr_ref)
            p, ds, do = _att_ds(_att_scores(q2, k2, masked), v_ref, o_ref, do_ref, lse_ref)
            av_s[...] += _bdg(p, _mx(do), 1, 1)
            dk2 = _bdg(ds, _mx(q2), 1, 1)
            akn_s[...] += dk2[:, :, :LANES]
            dkr_ref[...] += jnp.sum(dk2[:, :, LANES:], axis=0)

        _on_causal_blocks(i, j, step)

        @pl.when(i == nq - 1)
        def _():
            _put_heads(dkn_ref, akn_s[...])
            _put_heads(dv_ref, av_s[...])

    qs = lambda off: pl.BlockSpec((tq, ATT_W), lambda j, h, i: (jnp.maximum(i, j), off // ATT_HB + h))
    ks = lambda off: pl.BlockSpec((tk, ATT_W), lambda j, h, i: (j, off // ATT_HB + h))
    sds = jax.ShapeDtypeStruct
    return _hosted_call(
        body, name=name, grid=(nk, MLA_H // ATT_HB, nq),
        in_specs=[qs(0), qs(0), ks(0), pl.BlockSpec((tk, LANES), lambda j, h, i: (j, 0)), ks(MLA_H), qs(0), qs(0), qs(0)],
        out_specs=[ks(0), ks(0), pl.BlockSpec((tk, LANES), lambda j, h, i: (j, 0))],
        out_shape=[sds((t, MLA_H * LANES), F32), sds((t, MLA_H * LANES), F32), sds((t, LANES), F32)],
        scratch_shapes=[pltpu.VMEM((ATT_HB, tk, LANES), F32), pltpu.VMEM((ATT_HB, tk, LANES), F32)],
        sem=("parallel", "arbitrary", "arbitrary"), args=(q, qr, kv, kr, kv, o, do, lse), ride=ride)


def _rope_tables(positions):
    lane = jnp.arange(LANES)
    valid = lane < MLA_ROPE
    inv_freq = ROPE_THETA ** (-(2 * (lane % (MLA_ROPE // 2))).astype(F32) / MLA_ROPE)
    ang = positions.astype(F32)[:, None] * inv_freq[None, :]
    return jnp.where(valid, jnp.cos(ang), 0.0), jnp.where(valid, jnp.sin(ang), 0.0)


def _mla_weights(w):
    dt = w["in_w"].dtype
    iw = w["in_w"]
    c1 = MLA_QR + MLA_KVR + MLA_ROPE
    w_in = jnp.concatenate([iw[:, :c1], jnp.zeros((D_MODEL, LANES - MLA_ROPE), dt), iw[:, c1:]], axis=1)
    qu = w["q_up_w"].reshape(MLA_QR, MLA_H, MLA_NOPE + MLA_ROPE)
    qrope = jnp.concatenate([qu[:, :, MLA_NOPE:], jnp.zeros((MLA_QR, MLA_H, LANES - MLA_ROPE), dt)], axis=2)
    w_q = jnp.concatenate([qu[:, :, :MLA_NOPE].reshape(MLA_QR, -1), qrope.reshape(MLA_QR, -1)], axis=1)
    kvu = w["kv_up_w"].reshape(MLA_KVR, MLA_H, MLA_NOPE + MLA_V)
    w_kv = jnp.concatenate([kvu[:, :, :MLA_NOPE].reshape(MLA_KVR, -1), kvu[:, :, MLA_NOPE:].reshape(MLA_KVR, -1)], axis=1)
    return w_in, w_q, w_kv


def _mla_layer_fwd(u, w, ln_g, ln_b, cosf, sinf, tag, rides=None):
    w_in, w_q, w_kv = _mla_weights(w)
    qnw, kvnw = w["q_norm_w"].reshape(1, -1), w["kv_norm_w"].reshape(1, -1)
    proj = _mm(u, w_in, name=tag + "_in")
    qn, kvn, kr = _mla_low_fwd(proj, qnw, kvnw, cosf, sinf, name=tag + "_low")
    q = _mm(qn, w_q, name=tag + "_qup")
    kv = _mm(kvn, w_kv, name=tag + "_kvup")
    qr = _rope_heads(q, MLA_H, cosf, sinf, False, name=tag + "_qrope")
    rode = {}
    (o, og, lse), rode["attn"] = _attn_fwd(q, qr, kv, kr, proj, name=tag + "_attn", ride=_ride_of(rides, "attn", False))
    y = _mm(og, w["out_w"], name=tag + "_out")
    h = _ln_fwd(u, y, ln_g, ln_b, name=tag + "_ln")
    saved = dict(u=u, w_in=w_in, w_q=w_q, w_kv=w_kv, qnw=qnw, kvnw=kvnw, proj=proj, qn=qn, kvn=kvn, kr=kr, q=q, kv=kv,
                 qr=qr, o=o, og=og, lse=lse, y=y)
    return h, saved, rode


def _mla_layer_bwd(s, w, ln_g, cosf, sinf, dr_up, du_up, tag, rides=None):
    dr, dg, db = _ln_bwd(s["u"], s["y"], ln_g, dr_up, du_up, name=tag + "_ln_b")
    dog = _mm(dr, w["out_w"], tb=True, name=tag + "_out_bx")
    d_out_w = _mm(s["og"], dr, ta=True, out_dtype=WIRE, name=tag + "_out_bw")
    do, dz = _gate_bwd(dog, s["o"], s["proj"], name=tag + "_gate_b")
    rode = {}
    (dqn_h, dqr_rot), rode["attn_bq"] = _attn_bwd_q(s["q"], s["qr"], s["kv"], s["kr"], s["o"], do, s["lse"],
                                                    name=tag + "_attn_bq", ride=_ride_of(rides, "attn_bq", True))
    (dkn_h, dv_h, dkr_rot), rode["attn_bkv"] = _attn_bwd_kv(s["q"], s["qr"], s["kv"], s["kr"], s["o"], do, s["lse"],
                                                           name=tag + "_attn_bkv", ride=_ride_of(rides, "attn_bkv", True))
    dqr = _rope_heads(dqr_rot, 0, cosf, sinf, True, name=tag + "_qrope_b")
    dq = jnp.concatenate([dqn_h, dqr], axis=1)
    dkv = jnp.concatenate([dkn_h, dv_h], axis=1)
    d_wq = _mm(s["qn"], dq, ta=True, out_dtype=WIRE, name=tag + "_qup_bw")
    dqn = _mm(dq, s["w_q"], tb=True, name=tag + "_qup_bx")
    d_wkv = _mm(s["kvn"], dkv, ta=True, out_dtype=WIRE, name=tag + "_kvup_bw")
    dkvn = _mm(dkv, s["w_kv"], tb=True, name=tag + "_kvup_bx")
    dlow, dqnw, dkvnw = _mla_low_bwd(s["proj"], s["qnw"], s["kvnw"], cosf, sinf, dqn, dkvn, dkr_rot, name=tag + "_low_b")
    dproj = jnp.concatenate([dlow, dz], axis=1)
    d_in = _mm(s["u"], dproj, ta=True, out_dtype=WIRE, name=tag + "_in_bw")
    du = _mm(dproj, s["w_in"], tb=True, name=tag + "_in_bx")
    c1 = MLA_QR + MLA_KVR + MLA_ROPE
    d_in_w = jnp.concatenate([d_in[:, :c1], d_in[:, MLA_LOW:]], axis=1)
    dq3n = d_wq[:, :MLA_H * MLA_NOPE].reshape(MLA_QR, MLA_H, MLA_NOPE)
    dq3r = d_wq[:, MLA_H * MLA_NOPE:].reshape(MLA_QR, MLA_H, LANES)[:, :, :MLA_ROPE]
    d_q_up = jnp.concatenate([dq3n, dq3r], axis=2).reshape(MLA_QR, -1)
    dkv3 = d_wkv.reshape(MLA_KVR, 2, MLA_H, MLA_NOPE)
    d_kv_up = jnp.concatenate([dkv3[:, 0], dkv3[:, 1]], axis=2).reshape(MLA_KVR, -1)
    grads = dict(in_w=d_in_w, q_norm_w=dqnw[0], q_up_w=d_q_up, kv_norm_w=dkvnw[0], kv_up_w=d_kv_up, out_w=d_out_w,
                 ln_g=dg[0], ln_b=db[0])
    return dr, du, grads, rode


GDN_REP = GDN_HV // GDN_HK
GDN_A_LANE = GDN_HV
GDN_HPB = 4
GDN_VPB = GDN_HPB * GDN_REP


def _bdg(a, b, ca, cb):
    return lax.dot_general(a, b, (((ca,), (cb,)), ((0,), (0,))), preferred_element_type=F32)


@jax.custom_vjp
def _bnn(a, b):
    return _bdg(_mx(a), _mx(b), 2, 1)


def _bnn_f(a, b):
    return _bnn(a, b), (a, b)


def _bnn_b(res, ct):
    a, b = res
    return _bdg(_mx(ct), _mx(b), 2, 2), _bdg(_mx(a), _mx(ct), 1, 1)


_bnn.defvjp(_bnn_f, _bnn_b)


@jax.custom_vjp
def _bnt(a, b):
    return _bdg(_mx(a), _mx(b), 2, 2)


def _bnt_f(a, b):
    return _bnt(a, b), (a, b)


def _bnt_b(res, ct):
    a, b = res
    return _bdg(_mx(ct), _mx(b), 2, 1), _bdg(_mx(ct), _mx(a), 1, 1)


_bnt.defvjp(_bnt_f, _bnt_b)


@jax.custom_vjp
def _btn(a, b):
    return _bdg(_mx(a), _mx(b), 1, 1)


def _btn_f(a, b):
    return _btn(a, b), (a, b)


def _btn_b(res, ct):
    a, b = res
    return _bdg(_mx(b), _mx(ct), 2, 2), _bdg(_mx(a), _mx(ct), 2, 1)


_btn.defvjp(_btn_f, _btn_b)


def _h3(a, b, ca=2, cb=1):
    ah, bh = _mx(a), _mx(b)
    al, bl = _mx(a - ah.astype(F32)), _mx(b - bh.astype(F32))
    return _bdg(ah, bh, ca, cb) + (_bdg(ah, bl, ca, cb) + _bdg(al, bh, ca, cb))


@jax.custom_vjp
def _neumann_inverse(x):
    L = x.shape[-1]
    eye = (lax.broadcasted_iota(jnp.int32, (L, L), 0) == lax.broadcasted_iota(jnp.int32, (L, L), 1)).astype(F32)
    inv = eye + x
    xp = x
    for _ in range(L.bit_length() - 2):
        xp = _h3(xp, xp)
        inv = inv + _h3(inv, xp)
    return inv


def _neumann_f(x):
    inv = _neumann_inverse(x)
    return inv, inv


def _neumann_b(inv, ct):
    return (_h3(_h3(inv, ct, 1, 1), inv, 2, 2),)


_neumann_inverse.defvjp(_neumann_f, _neumann_b)


@jax.custom_vjp
def _saved_inverse(x, inv):
    return inv


def _saved_f(x, inv):
    return inv, inv


def _saved_b(inv, ct):
    return _neumann_b(inv, ct)[0], jnp.zeros_like(inv)


_saved_inverse.defvjp(_saved_f, _saved_b)


def _cat0(parts):
    return jnp.concatenate([p[None] for p in parts], axis=0)


def _gdn_chunk(hb, q, k, v, z, ba, alog, dtb, nw, s, inv_saved=None):
    L = q.shape[1]
    r_i = lax.broadcasted_iota(jnp.int32, (L, L), 0)
    c_i = lax.broadcasted_iota(jnp.int32, (L, L), 1)
    incl, strict = r_i >= c_i, r_i > c_i
    rep = lambda t: jnp.broadcast_to(t[:, None], (GDN_HPB, GDN_REP) + t.shape[1:]).reshape((GDN_VPB,) + t.shape[1:])
    qn = rep(q * lax.rsqrt(jnp.sum(q * q, -1, keepdims=True) + RMS_EPS) * (GDN_DK ** -0.5))
    kn = rep(k * lax.rsqrt(jnp.sum(k * k, -1, keepdims=True) + RMS_EPS))
    beta_all = jax.nn.sigmoid(ba)
    g_all = -jnp.exp(alog) * _softplus(ba + dtb)
    gcs_all = _sel_l(incl.astype(F32), g_all)
    lane = lax.broadcasted_iota(jnp.int32, (L, LANES), 1)
    pick = lambda mat, idx: jnp.sum(jnp.where(lane == idx, mat, 0.0), axis=1, keepdims=True)
    beta = _cat0([pick(beta_all, GDN_VPB * hb + b) for b in range(GDN_VPB)])
    gc = _cat0([pick(gcs_all, GDN_A_LANE + GDN_VPB * hb + b) for b in range(GDN_VPB)])
    gm = jnp.broadcast_to(gc, (GDN_VPB, L, L))
    decay = jnp.exp(jnp.where(incl, gm - jnp.swapaxes(gm, 1, 2), -jnp.inf))
    kb = kn * beta
    eg = jnp.exp(gc)
    x = -jnp.where(strict, _bnt(kb, kn) * decay, 0.0)
    inv = _neumann_inverse(x) if inv_saved is None else _saved_inverse(x, inv_saved)
    uw = _bnn(inv, jnp.concatenate([v * beta, kb * eg], axis=2))
    uu, ww = uw[:, :, :GDN_DV], uw[:, :, GDN_DV:]
    qk = jnp.where(incl, _bnt(qn, kn) * decay, 0.0)
    last = lax.broadcasted_iota(jnp.int32, (L, 1), 0) == L - 1
    glast = jnp.sum(jnp.where(last, gc, 0.0), axis=1, keepdims=True)
    kdec = kn * jnp.exp(glast - gc)
    vnew = uu - _bnn(ww, s)
    o = _bnn(qn * eg, s) + _bnn(qk, vnew)
    new = s * jnp.exp(glast) + _btn(kdec, vnew)
    on = o * lax.rsqrt(jnp.mean(o * o, -1, keepdims=True) + RMS_EPS) * nw * _silu(z)
    return (on, new, inv) if inv_saved is None else (on, new)


def _heads_of(ref, n):
    return _cat0([ref[:, i * LANES:(i + 1) * LANES] for i in range(n)])


def _gdn_specs(nc, rev):
    cc = (lambda c: nc - 1 - c) if rev else (lambda c: c)
    wq, wv = GDN_HPB * GDN_DK, GDN_VPB * GDN_DV
    par = pl.BlockSpec((1, LANES), lambda c, h: (0, 0))
    return dict(
        q=pl.BlockSpec((GDN_L, wq), lambda c, h: (cc(c), h)),
        k=pl.BlockSpec((GDN_L, wq), lambda c, h: (cc(c), GDN_KEY // wq + h)),
        v=pl.BlockSpec((GDN_L, wv), lambda c, h: (cc(c), 2 * GDN_KEY // wv + h)),
        z=pl.BlockSpec((GDN_L, wv), lambda c, h: (cc(c), GDN_CONV_DIM // wv + h)),
        ba=pl.BlockSpec((GDN_L, LANES), lambda c, h: (cc(c), (GDN_CONV_DIM + GDN_VAL) // LANES)),
        par=par,
        st=pl.BlockSpec((1, GDN_VPB, GDN_DK, GDN_DV), lambda c, h: (cc(c), h, 0, 0)),
        inv=pl.BlockSpec((1, GDN_VPB, GDN_L, GDN_L), lambda c, h: (cc(c), h, 0, 0)),
        o=pl.BlockSpec((GDN_L, wv), lambda c, h: (cc(c), h)),
        qk_out=pl.BlockSpec((GDN_L, wq), lambda c, h: (cc(c), h)),
        ba_out=pl.BlockSpec((GDN_L, LANES), lambda c, h: (cc(c), 0)),
    )


def _gdn_fwd(proj, act, alog, dtb, nw, name, ride=None):
    t = proj.shape[0]
    nc = t // GDN_L
    sp = _gdn_specs(nc, False)

    def body(q_ref, k_ref, v_ref, z_ref, ba_ref, alog_ref, dtb_ref, nw_ref, o_ref, st_ref, inv_ref, state):
        c, h = pl.program_id(0), pl.program_id(1)

        mine = pl.ds(h * GDN_VPB, GDN_VPB)

        @pl.when(c == 0)
        def _():
            state[mine] = jnp.zeros((GDN_VPB, GDN_DK, GDN_DV), F32)

        prev = state[mine]
        st_ref[0] = prev
        on, new, inv = _gdn_chunk(h, _heads_of(q_ref, GDN_HPB), _heads_of(k_ref, GDN_HPB), _heads_of(v_ref, GDN_VPB),
                                  _heads_of(z_ref, GDN_VPB), ba_ref[...], alog_ref[...], dtb_ref[...], nw_ref[...], prev)
        inv_ref[0] = inv
        for b in range(GDN_VPB):
            o_ref[:, b * LANES:(b + 1) * LANES] = on[b].astype(o_ref.dtype)
        state[mine] = new

    sds = jax.ShapeDtypeStruct
    return _hosted_call(
        body, name=name, grid=(nc, GDN_HK // GDN_HPB),
        in_specs=[sp["q"], sp["k"], sp["v"], sp["z"], sp["ba"], sp["par"], sp["par"], sp["par"]],
        out_specs=[sp["o"], sp["st"], sp["inv"]],
        out_shape=[sds((t, GDN_VAL), F32), sds((nc, GDN_HV, GDN_DK, GDN_DV), F32), sds((nc, GDN_HV, GDN_L, GDN_L), F32)],
        scratch_shapes=[pltpu.VMEM((GDN_HV, GDN_DK, GDN_DV), F32)],
        sem=("arbitrary", "arbitrary"), args=(act, act, act, proj, proj, alog, dtb, nw), ride=ride)


def _gdn_bwd(proj, act, alog, dtb, nw, states, invs, don, name, ride=None):
    t = proj.shape[0]
    nc = t // GDN_L
    sp = _gdn_specs(nc, True)

    def body(q_ref, k_ref, v_ref, z_ref, ba_ref, alog_ref, dtb_ref, nw_ref, st_ref, do_ref, inv_ref,
             dact_ref, dz_ref, dba_ref, dalog_ref, ddtb_ref, dnw_ref, dstate):
        c, h = pl.program_id(0), pl.program_id(1)

        mine = pl.ds(h * GDN_VPB, GDN_VPB)

        @pl.when(c == 0)
        def _():
            dstate[mine] = jnp.zeros((GDN_VPB, GDN_DK, GDN_DV), F32)

        @pl.when((c == 0) & (h == 0))
        def _():
            dalog_ref[...] = jnp.zeros_like(dalog_ref)
            ddtb_ref[...] = jnp.zeros_like(ddtb_ref)
            dnw_ref[...] = jnp.zeros_like(dnw_ref)

        @pl.when(h == 0)
        def _():
            dba_ref[...] = jnp.zeros_like(dba_ref)

        _, vjp = jax.vjp(functools.partial(_gdn_chunk, h, inv_saved=inv_ref[0]), _heads_of(q_ref, GDN_HPB),
                         _heads_of(k_ref, GDN_HPB), _heads_of(v_ref, GDN_VPB), _heads_of(z_ref, GDN_VPB), ba_ref[...],
                         alog_ref[...], dtb_ref[...], nw_ref[...], st_ref[0])
        dq, dk, dv, dz, dba, dalog, ddtb, dnw, dprev = vjp((_heads_of(do_ref, GDN_VPB), dstate[mine]))
        for b in range(GDN_VPB):
            dz_ref[:, b * LANES:(b + 1) * LANES] = dz[b]
        for hi in range(GDN_HK // GDN_HPB):
            @pl.when(h == hi)
            def _(hi=hi):
                for i in range(GDN_HPB):
                    c0 = (hi * GDN_HPB + i) * GDN_DK
                    dact_ref[:, c0:c0 + GDN_DK] = dq[i]
                    dact_ref[:, GDN_KEY + c0:GDN_KEY + c0 + GDN_DK] = dk[i]
                for b in range(GDN_VPB):
                    c0 = 2 * GDN_KEY + (hi * GDN_VPB + b) * GDN_DV
                    dact_ref[:, c0:c0 + GDN_DV] = dv[b]
        dba_ref[...] += dba
        dalog_ref[...] += dalog
        ddtb_ref[...] += ddtb
        dnw_ref[...] += dnw
        dstate[mine] = dprev

    sds = jax.ShapeDtypeStruct
    par_out = pl.BlockSpec((1, LANES), lambda c, h: (0, 0))
    return _hosted_call(
        body, name=name, grid=(nc, GDN_HK // GDN_HPB),
        in_specs=[sp["q"], sp["k"], sp["v"], sp["z"], sp["ba"], sp["par"], sp["par"], sp["par"], sp["st"], sp["o"],
                  sp["inv"]],
        out_specs=[pl.BlockSpec((GDN_L, GDN_CONV_DIM), lambda c, h: (nc - 1 - c, 0)), sp["o"], sp["ba_out"],
                   par_out, par_out, par_out],
        out_shape=[sds((t, GDN_CONV_DIM), F32), sds((t, GDN_VAL), F32), sds((t, LANES), F32), sds((1, LANES), F32),
                   sds((1, LANES), F32), sds((1, LANES), F32)],
        scratch_shapes=[pltpu.VMEM((GDN_HV, GDN_DK, GDN_DV), F32)],
        sem=("arbitrary", "arbitrary"), args=(act, act, act, proj, proj, alog, dtb, nw, states, don, invs), ride=ride)


GDN_PROJ = GDN_CONV_DIM + GDN_VAL + 2 * GDN_HV


def _gdn_layer_fwd(u, w, ln_g, ln_b, tag, rides=None, late=None):
    w_in = jnp.concatenate([w["in_w"], jnp.zeros((D_MODEL, GDN_PROJ_PAD - GDN_PROJ), w["in_w"].dtype)], axis=1)
    alog = _pad_lanes(w["a_log"], offset=GDN_A_LANE)
    dtb = _pad_lanes(w["dt_bias"], offset=GDN_A_LANE)
    nw = w["norm_w"].reshape(1, GDN_DV)
    zb = jnp.zeros((1, GDN_CONV_DIM), F32)
    if late is None:
        proj = _mm(u, w_in, name=tag + "_in")
    else:
        proj, got = _mm(u, w_in, name=tag + "_in", ride=(late[0], False))
        w.update(late[1](got))
    act, _ = _conv_fwd(proj, 0, w["conv_w"], zb, name=tag + "_conv")
    rode = {}
    (on, states, invs), rode["delta"] = _gdn_fwd(proj, act, alog, dtb, nw, name=tag + "_delta",
                                                 ride=_ride_of(rides, "delta", False))
    y = _mm(on, w["out_w"], name=tag + "_out")
    h = _ln_fwd(u, y, ln_g, ln_b, name=tag + "_ln")
    saved = dict(u=u, w_in=w_in, proj=proj, act=act, states=states, invs=invs, on=on, y=y, alog=alog, dtb=dtb, nw=nw, zb=zb)
    return h, saved, rode


def _gdn_layer_bwd(s, w, ln_g, dr_up, du_up, tag, rides=None, own=None):
    t = s["u"].shape[0]
    dr, dg, db = _ln_bwd(s["u"], s["y"], ln_g, dr_up, du_up, name=tag + "_ln_b")
    don = _mm(dr, w["out_w"], tb=True, name=tag + "_out_bx")
    d_out_w = _mm(s["on"], dr, ta=True, out_dtype=WIRE, name=tag + "_out_bw")
    carried = list((rides or {}).get("delta_b", [])) + ([own("out_w", d_out_w)] if own else [])
    rode = {}
    (dact, dz, dba, dalog, ddtb, dnw), rode["delta_b"] = _gdn_bwd(
        s["proj"], s["act"], s["alog"], s["dtb"], s["nw"], s["states"], s["invs"], don, name=tag + "_delta_b",
        ride=(carried, True) if carried else None)
    dqkv, d_conv_w, _ = _conv_bwd(s["proj"], 0, w["conv_w"], s["zb"], dact, name=tag + "_conv_b")
    dproj = jnp.concatenate([dqkv, dz, dba, jnp.zeros((t, GDN_PROJ_PAD - GDN_PROJ - (LANES - 2 * GDN_HV)), F32)], axis=1)
    d_in = _mm(s["u"], dproj, ta=True, out_dtype=WIRE, name=tag + "_in_bw")
    du = _mm(dproj, s["w_in"], tb=True, name=tag + "_in_bx")
    grads = dict(in_w=d_in[:, :GDN_PROJ], conv_w=d_conv_w, a_log=dalog[0, GDN_A_LANE:GDN_A_LANE + GDN_HV],
                 dt_bias=ddtb[0, GDN_A_LANE:GDN_A_LANE + GDN_HV], norm_w=dnw[0], out_w=d_out_w, ln_g=dg[0], ln_b=db[0])
    return dr, du, grads, rode


def _mesh_pos():
    return lax.axis_index("x"), lax.axis_index("y"), lax.axis_index("c")


def _peer(k, x, y, c):
    return ((1 - x) if k & 4 else x, (1 - y) if k & 2 else y, (1 - c) if k & 1 else c)


def _ride_copies(ins, outs, send, recv, loc, scatter, with_arrivals):
    n = len(ins)
    x, y, c = _mesh_pos()
    me = 4 * x + 2 * y + c
    local = [pltpu.make_async_copy(ins[i].at[me] if scatter else ins[i], outs[i].at[me], loc.at[i]) for i in range(n)]
    sends, arrivals = [], []
    for k in range(1, N_DEV):
        peer = _peer(k, x, y, c)
        pidx = 4 * peer[0] + 2 * peer[1] + peer[2]
        for i in range(n):
            src = ins[i].at[pidx] if scatter else ins[i]
            sems = dict(send_sem=send.at[i, k - 1], recv_sem=recv.at[i, k - 1], device_id=peer,
                        device_id_type=pl.DeviceIdType.MESH)
            sends.append(pltpu.make_async_remote_copy(src_ref=src, dst_ref=outs[i].at[me], **sems))
            if with_arrivals:
                arrivals.append(pltpu.make_async_remote_copy(src_ref=src, dst_ref=outs[i].at[pidx], **sems))
    return local, sends, arrivals


def _gather_copy(ins, outs, send, recv, i, k, block, to, from_input=False):
    slot = outs[i].at[4 * block[0] + 2 * block[1] + block[2]]
    return pltpu.make_async_remote_copy(src_ref=ins[i] if from_input else slot, dst_ref=slot, send_sem=send.at[i, k],
                                        recv_sem=recv.at[i, k], device_id=to, device_id_type=pl.DeviceIdType.MESH)


def _gather_places():
    x, y, c = _mesh_pos()
    return (x, y, c), (x, y, 1 - c), [(x, 1 - y), (1 - x, y), (1 - x, 1 - y)], c


def _ride_start(ins, outs, send, recv, loc, scatter):
    if scatter:
        local, sends, _ = _ride_copies(ins, outs, send, recv, loc, scatter, False)
        for cp in local + sends:
            cp.start()
        return
    me, sibling, chips, c = _gather_places()
    for i in range(len(ins)):
        pltpu.make_async_copy(ins[i], outs[i].at[4 * me[0] + 2 * me[1] + me[2]], loc.at[i]).start()
        _gather_copy(ins, outs, send, recv, i, 0, me, sibling, True).start()
        for j, ch in enumerate(chips):
            _gather_copy(ins, outs, send, recv, i, 1 + j, me, (*ch, c), True).start()


def _ride_wait(ins, outs, send, recv, loc, scatter):
    if scatter:
        local, sends, arrivals = _ride_copies(ins, outs, send, recv, loc, scatter, True)
        for cp in arrivals:
            cp.wait_recv()
        for cp in sends:
            cp.wait_send()
        for cp in local:
            cp.wait()
        return
    me, sibling, chips, c = _gather_places()
    n = len(ins)
    passed = []
    for j, ch in enumerate(chips):
        for i in range(n):
            _gather_copy(ins, outs, send, recv, i, 1 + j, (*ch, c), me).wait_recv()
            cp = _gather_copy(ins, outs, send, recv, i, 4 + j, (*ch, c), sibling)
            cp.start()
            passed.append(cp)
    for i in range(n):
        _gather_copy(ins, outs, send, recv, i, 0, sibling, me).wait_recv()
        for j, ch in enumerate(chips):
            _gather_copy(ins, outs, send, recv, i, 4 + j, (*ch, 1 - c), me).wait_recv()
    for i in range(n):
        _gather_copy(ins, outs, send, recv, i, 0, me, sibling, True).wait_send()
        for j, ch in enumerate(chips):
            _gather_copy(ins, outs, send, recv, i, 1 + j, me, (*ch, c), True).wait_send()
    for cp in passed:
        cp.wait_send()
    for i in range(n):
        pltpu.make_async_copy(ins[i], outs[i].at[4 * me[0] + 2 * me[1] + me[2]], loc.at[i]).wait()


def _ride_shapes(arrs, scatter):
    n = len(arrs)
    out_shape = [jax.ShapeDtypeStruct(a.shape if scatter else (N_DEV,) + a.shape, a.dtype) for a in arrs]
    scratch = [pltpu.SemaphoreType.DMA((n, N_DEV - 1)), pltpu.SemaphoreType.DMA((n, N_DEV - 1)), pltpu.SemaphoreType.DMA((n,))]
    return out_shape, scratch


def _exchange(arrs, scatter, name):
    n = len(arrs)
    hbm = pl.BlockSpec(memory_space=pltpu.HBM)

    def body(*refs):
        ins, outs = refs[:n], refs[n:2 * n]
        _ride_start(ins, outs, *refs[2 * n:], scatter)
        _ride_wait(ins, outs, *refs[2 * n:], scatter)

    out_shape, scratch = _ride_shapes(arrs, scatter)
    return pl.pallas_call(
        body, name=name, in_specs=[hbm] * n, out_specs=[hbm] * n, out_shape=out_shape, scratch_shapes=scratch,
        compiler_params=pltpu.CompilerParams(has_side_effects=True),
    )(*arrs)


def _hosted_call(body, *, name, grid, in_specs, out_specs, out_shape, scratch_shapes, sem, args, ride=None):
    if ride is None:
        return pl.pallas_call(body, name=name, grid=grid, in_specs=in_specs, out_specs=out_specs, out_shape=out_shape,
                              scratch_shapes=scratch_shapes, compiler_params=_cparams(sem))(*args), []
    arrs, scatter = ride
    n, ni, no, ns = len(arrs), len(in_specs), len(out_specs), len(scratch_shapes)
    hbm = pl.BlockSpec(memory_space=pltpu.HBM)
    r_shape, r_scratch = _ride_shapes(arrs, scatter)

    def full(*refs):
        a, ri = refs[:ni], refs[ni:ni + n]
        o, ro = refs[ni + n:ni + n + no], refs[ni + n + no:ni + 2 * n + no]
        s, rs = refs[ni + 2 * n + no:ni + 2 * n + no + ns], refs[ni + 2 * n + no + ns:]
        ids = [pl.program_id(d) for d in range(len(grid))]
        first, last = ids[0] == 0, ids[0] == grid[0] - 1
        for d in range(1, len(grid)):
            first, last = first & (ids[d] == 0), last & (ids[d] == grid[d] - 1)

        @pl.when(first)
        def _():
            _ride_start(ri, ro, *rs, scatter)

        body(*a, *o, *s)

        @pl.when(last)
        def _():
            _ride_wait(ri, ro, *rs, scatter)

    outs = pl.pallas_call(
        full, name=name, grid=grid, in_specs=list(in_specs) + [hbm] * n, out_specs=list(out_specs) + [hbm] * n,
        out_shape=list(out_shape) + r_shape, scratch_shapes=list(scratch_shapes) + r_scratch,
        compiler_params=pltpu.CompilerParams(dimension_semantics=("arbitrary",) * len(grid), vmem_limit_bytes=VMEM_LIMIT,
                                             has_side_effects=True),
    )(*args, *arrs)
    return outs[:no], list(outs[no:])


def _unshard(g, ax):
    g = jnp.moveaxis(g, 0, ax)
    sh = g.shape
    return g.reshape(sh[:ax] + (sh[ax] * sh[ax + 1],) + sh[ax + 2:])


def _to_parts(full, ax):
    sh = full.shape
    full = full.reshape(sh[:ax] + (N_DEV, sh[ax] // N_DEV) + sh[ax + 1:])
    return jnp.moveaxis(full, ax, 0)


def _row_tile(r, c):
    cap = max(8, (256 * 1024) // max(c, 1))
    best = None
    for d in range(8, min(r, cap) + 1, 8):
        if r % d == 0:
            best = d
    return r if best is None else best


def _adamw(w, m, v, parts, name):
    r, c = w.shape
    tr = _row_tile(r, c)

    def body(w_ref, m_ref, v_ref, p_ref, g_ref, d_ref, nm_ref, nv_ref):
        g = p_ref[0].astype(F32)
        for q in range(1, N_DEV):
            g = g + p_ref[q].astype(F32)
        nm = ADAM_B1 * m_ref[...] + (1.0 - ADAM_B1) * g
        nv = ADAM_B2 * v_ref[...] + (1.0 - ADAM_B2) * (g * g)
        m_hat = nm / (1.0 - ADAM_B1 ** ADAM_STEP)
        v_hat = nv / (1.0 - ADAM_B2 ** ADAM_STEP)
        g_ref[...] = g
        d_ref[...] = -ADAM_LR * (m_hat / (jnp.sqrt(v_hat) + ADAM_EPS) + ADAM_WD * w_ref[...])
        nm_ref[...] = nm
        nv_ref[...] = nv

    row = pl.BlockSpec((tr, c), lambda i: (i, 0))
    out = jax.ShapeDtypeStruct((r, c), F32)
    return pl.pallas_call(
        body, name=name, grid=(r // tr,),
        in_specs=[row, row, row, pl.BlockSpec((N_DEV, tr, c), lambda i: (0, i, 0))],
        out_specs=[row] * 4, out_shape=[out] * 4, compiler_params=_cparams(("parallel",)),
    )(w, m, v, parts)


WEIGHTS = ['ssd_in_w', 'ssd_conv_w', 'ssd_conv_b', 'ssd_dt_bias', 'ssd_a_log', 'ssd_d', 'ssd_norm_w', 'ssd_out_w',
           'mla_in_w', 'mla_q_norm_w', 'mla_q_up_w', 'mla_kv_norm_w', 'mla_kv_up_w', 'mla_out_w', 'gdn_in_w',
           'gdn_conv_w', 'gdn_a_log', 'gdn_dt_bias', 'gdn_norm_w', 'gdn_out_w', 'ln_g', 'ln_b']
SHARDED = {'ssd_in_w': (1, True), 'ssd_conv_w': (1, False), 'ssd_conv_b': (0, False), 'ssd_norm_w': (0, False),
           'ssd_out_w': (0, True), 'mla_in_w': (1, True), 'mla_q_up_w': (1, True), 'mla_kv_up_w': (1, True),
           'mla_out_w': (0, True), 'gdn_in_w': (1, True), 'gdn_conv_w': (1, False), 'gdn_out_w': (0, True)}
REPLICATED = [n for n in WEIGHTS if n not in SHARDED]


def _pack_small(vals):
    flat = jnp.concatenate([vals[n].reshape(-1).astype(F32) for n in REPLICATED])
    rows = -(-flat.shape[0] // (8 * LANES)) * 8
    return jnp.pad(flat, (0, rows * LANES - flat.shape[0])).reshape(rows, LANES)


def _unpack_small(slab, like):
    flat = slab.reshape(-1)
    out, off = {}, 0
    for n in REPLICATED:
        sz = like[n].size
        out[n] = flat[off:off + sz].reshape(like[n].shape)
        off += sz
    return out


def kernel(x, positions, ssd_in_w, ssd_conv_w, ssd_conv_b, ssd_dt_bias, ssd_a_log, ssd_d, ssd_norm_w, ssd_out_w, mla_in_w, mla_q_norm_w, mla_q_up_w, mla_kv_norm_w, mla_kv_up_w, mla_out_w, gdn_in_w, gdn_conv_w, gdn_a_log, gdn_dt_bias, gdn_norm_w, gdn_out_w, ln_g, ln_b, loss_target, m_ssd_in_w, m_ssd_conv_w, m_ssd_conv_b, m_ssd_dt_bias, m_ssd_a_log, m_ssd_d, m_ssd_norm_w, m_ssd_out_w, m_mla_in_w, m_mla_q_norm_w, m_mla_q_up_w, m_mla_kv_norm_w, m_mla_kv_up_w, m_mla_out_w, m_gdn_in_w, m_gdn_conv_w, m_gdn_a_log, m_gdn_dt_bias, m_gdn_norm_w, m_gdn_out_w, m_ln_g, m_ln_b, v_ssd_in_w, v_ssd_conv_w, v_ssd_conv_b, v_ssd_dt_bias, v_ssd_a_log, v_ssd_d, v_ssd_norm_w, v_ssd_out_w, v_mla_in_w, v_mla_q_norm_w, v_mla_q_up_w, v_mla_kv_norm_w, v_mla_kv_up_w, v_mla_out_w, v_gdn_in_w, v_gdn_conv_w, v_gdn_a_log, v_gdn_dt_bias, v_gdn_norm_w, v_gdn_out_w, v_ln_g, v_ln_b):
    loc = locals()
    w = {n: loc[n] for n in WEIGHTS}
    m = {n: loc["m_" + n] for n in WEIGHTS}
    v = {n: loc["v_" + n] for n in WEIGHTS}
    xs, pos, tgt = x[0], positions[0], loss_target[0]

    def names_of(prefix):
        return [n for n in SHARDED if n.startswith(prefix + "_")]

    def shard(n, j):
        return w[n][j].astype(WIRE) if SHARDED[n][1] else w[n][j]

    def full(n, gathered):
        return _unshard(gathered, SHARDED[n][0])

    def slots(n, g):
        return _to_parts(g, SHARDED[n][0]).astype(WIRE if SHARDED[n][1] else F32)

    def replicated(prefix, j):
        return {n[len(prefix) + 1:]: w[n][j] for n in REPLICATED if n.startswith(prefix + "_")}

    def late_weights(prefix, j, keys):
        return [shard(prefix + "_" + k, j) for k in keys], lambda got: {k: full(prefix + "_" + k, g) for k, g in zip(keys, got)}

    lg = lambda i: w["ln_g"][i].reshape(1, D_MODEL)
    lb = lambda i: w["ln_b"][i].reshape(1, D_MODEL)
    cosf, sinf = _rope_tables(pos)
    ssd_late = ("conv_w", "conv_b", "norm_w", "out_w")

    w_s0 = dict(replicated("ssd", 0), in_w=full("ssd_in_w", _exchange([shard("ssd_in_w", 0)], False, name="gather_ssd0")[0]))
    h1, s0, got = _ssd_layer_fwd(
        xs, w_s0, lg(0), lb(0), "ssd0", late=late_weights("ssd", 0, ssd_late),
        rides={"conv": [shard("mla_q_up_w", 0), shard("mla_kv_up_w", 0)], "scan": [shard("mla_in_w", 0)],
               "out": [shard("mla_out_w", 0)]})
    w_m0 = dict(replicated("mla", 0), q_up_w=full("mla_q_up_w", got["conv"][0]), kv_up_w=full("mla_kv_up_w", got["conv"][1]),
                in_w=full("mla_in_w", got["scan"][0]), out_w=full("mla_out_w", got["out"][0]))
    h2, s1, got = _mla_layer_fwd(h1, w_m0, lg(1), lb(1), cosf, sinf, "mla0", rides={"attn": [shard("gdn_in_w", 0)]})
    w_g0 = dict(replicated("gdn", 0), in_w=full("gdn_in_w", got["attn"][0]))
    h3, s2, got = _gdn_layer_fwd(h2, w_g0, lg(2), lb(2), "gdn0", late=late_weights("gdn", 0, ("conv_w", "out_w")),
                                 rides={"delta": [shard("ssd_in_w", 1)]})
    w_s1 = dict(replicated("ssd", 1), in_w=full("ssd_in_w", got["delta"][0]))
    h4, s3, _ = _ssd_layer_fwd(h3, w_s1, lg(3), lb(3), "ssd1", late=late_weights("ssd", 1, ssd_late))
    loss_tile, dl = _loss_head(h4, tgt, name="loss_head")

    dr3, du3, g3, _, _ = _ssd_layer_bwd(s3, w_s1, lg(3), jnp.zeros_like(dl), dl, "ssd1")
    dr2, du2, g2, got = _gdn_layer_bwd(s2, w_g0, lg(2), dr3, du3, "gdn0", own=lambda k, g: slots("gdn_" + k, g),
                                       rides={"delta_b": [slots(n, g3[n[4:]]) for n in names_of("ssd")]})
    r3, r2_out = got["delta_b"][:-1], got["delta_b"][-1]
    gin = slots("gdn_in_w", g2["in_w"])
    half = gin.shape[1] // 2
    dr1, du1, g1, got = _mla_layer_bwd(
        s1, w_m0, lg(1), cosf, sinf, dr2, du2, "mla0",
        rides={"attn_bq": [gin[:, :half]], "attn_bkv": [gin[:, half:], slots("gdn_conv_w", g2["conv_w"])]})
    r2 = {"gdn_in_w": jnp.concatenate([got["attn_bq"][0], got["attn_bkv"][0]], axis=1), "gdn_conv_w": got["attn_bkv"][1],
          "gdn_out_w": r2_out}
    dr0, du0, g0, got, r0 = _ssd_layer_bwd(s0, w_s0, lg(0), dr1, du1, "ssd0", own=lambda k, g: slots("ssd_" + k, g),
                                           rides={"scan_b": [slots(n, g1[n[4:]]) for n in names_of("mla")]})
    r1 = got["scan_b"]
    grad_x = _axpy(dr0, du0, name="grad_x")[None]

    gsmall = {"ssd_" + k: jnp.stack([g0[k], g3[k]]) for k in ("dt_bias", "a_log", "d")}
    gsmall.update({"mla_" + k: g1[k][None] for k in ("q_norm_w", "kv_norm_w")})
    gsmall.update({"gdn_" + k: g2[k][None] for k in ("a_log", "dt_bias", "norm_w")})
    gsmall["ln_g"] = jnp.stack([g0["ln_g"], g1["ln_g"], g2["ln_g"], g3["ln_g"]])
    gsmall["ln_b"] = jnp.stack([g0["ln_b"], g1["ln_b"], g2["ln_b"], g3["ln_b"]])
    small = _pack_small(gsmall)
    rsmall, = _exchange([jnp.broadcast_to(small[None], (N_DEV,) + small.shape)], True, name="gather_small_grads")

    recvd = {n: jnp.stack([r0[n[4:]], b], axis=1) for n, b in zip(names_of("ssd"), r3)}
    recvd.update({n: a[:, None] for n, a in zip(names_of("mla"), r1)})
    recvd.update({n: a[:, None] for n, a in r2.items()})
    recvd = [recvd[n] for n in SHARDED] + [rsmall]

    grads, deltas, new_m, new_v = {}, {}, {}, {}
    for n, pt in zip(SHARDED, recvd[:-1]):
        shp = w[n].shape
        r2d = (-1, shp[-1])
        outs = _adamw(w[n].reshape(r2d), m[n].reshape(r2d), v[n].reshape(r2d), pt.reshape((N_DEV,) + w[n].reshape(r2d).shape),
                      name="adamw_" + n)
        grads[n], deltas[n], new_m[n], new_v[n] = (o.reshape(shp) for o in outs)
    outs = _adamw(_pack_small(w), _pack_small(m), _pack_small(v), recvd[-1], name="adamw_replicated")
    for dst, o in zip((grads, deltas, new_m, new_v), outs):
        dst.update(_unpack_small(o, w))

    loss = lax.psum(loss_tile[0, 0], ("x", "y", "c"))
    return (loss, grad_x, *[grads[n] for n in WEIGHTS], *[deltas[n] for n in WEIGHTS],
            *[new_m[n] for n in WEIGHTS], *[new_v[n] for n in WEIGHTS])
```

```python
import functools

import jax
import jax.numpy as jnp
from jax import lax
from jax.experimental import pallas as pl
from jax.experimental.pallas import tpu as pltpu

F32 = jnp.float32
MXU = jnp.bfloat16
WIRE = jnp.bfloat16
HI = lax.Precision.HIGHEST

N_DEV = 8
LANES = 128
VMEM_LIMIT = 56 * 1024 * 1024

D_MODEL = 2048
DEPTH = 4
ALPHA = (2.0 * DEPTH) ** 0.25
LN_EPS = 1e-5
RMS_EPS = 1e-6

SSD_DI = 4096
SSD_P = 64
SSD_H = 64
SSD_G = 8
SSD_N = 128
SSD_L = 128
SSD_GS = SSD_DI // SSD_G
SSD_CONV_DIM = SSD_DI + 2 * SSD_G * SSD_N
SSD_PROJ_PAD = SSD_DI + SSD_CONV_DIM + LANES

MLA_H = 16
MLA_QR = 768
MLA_KVR = 512
MLA_NOPE = 128
MLA_ROPE = 64
MLA_V = 128
MLA_GATE = MLA_H * MLA_V
MLA_PROJ_PAD = MLA_QR + MLA_KVR + LANES + MLA_GATE
MLA_SCALE = (MLA_NOPE + MLA_ROPE) ** -0.5
ROPE_THETA = 10000.0
ATT_BLK = 512

GDN_HK = 16
GDN_HV = 32
GDN_DK = 128
GDN_DV = 128
GDN_KEY = GDN_HK * GDN_DK
GDN_VAL = GDN_HV * GDN_DV
GDN_L = 128
GDN_CONV_DIM = 2 * GDN_KEY + GDN_VAL
GDN_PROJ_PAD = 12800

ADAM_LR = 0.001
ADAM_B1 = 0.9
ADAM_B2 = 0.999
ADAM_EPS = 1e-08
ADAM_WD = 0.01
ADAM_STEP = 10


def _cparams(sem=None):
    return pltpu.CompilerParams(dimension_semantics=sem, vmem_limit_bytes=VMEM_LIMIT)


def _tile(n, cap):
    if n <= cap:
        return n
    best = None
    for d in range(LANES, cap + 1, LANES):
        if n % d == 0:
            best = d
    assert best is not None, (n, cap)
    return best


def _dg(a, b, ca, cb, prec=None):
    return lax.dot_general(a, b, (((ca,), (cb,)), ((), ())), preferred_element_type=F32, precision=prec)


def _mx(a):
    return a.astype(MXU)


@jax.custom_vjp
def _nn(a, b):
    return _dg(_mx(a), _mx(b), 1, 0)


def _nn_f(a, b):
    return _nn(a, b), (a, b)


def _nn_b(res, ct):
    a, b = res
    return _dg(_mx(ct), _mx(b), 1, 1), _dg(_mx(a), _mx(ct), 0, 0)


_nn.defvjp(_nn_f, _nn_b)


@jax.custom_vjp
def _nt(a, b):
    return _dg(_mx(a), _mx(b), 1, 1)


def _nt_f(a, b):
    return _nt(a, b), (a, b)


def _nt_b(res, ct):
    a, b = res
    return _dg(_mx(ct), _mx(b), 1, 0), _dg(_mx(ct), _mx(a), 0, 0)


_nt.defvjp(_nt_f, _nt_b)


@jax.custom_vjp
def _tn(a, b):
    return _dg(_mx(a), _mx(b), 0, 0)


def _tn_f(a, b):
    return _tn(a, b), (a, b)


def _tn_b(res, ct):
    a, b = res
    return _dg(_mx(b), _mx(ct), 1, 1), _dg(_mx(a), _mx(ct), 1, 0)


_tn.defvjp(_tn_f, _tn_b)


def _softplus(x):
    return jnp.maximum(x, 0.0) + jnp.log(1.0 + jnp.exp(-jnp.abs(x)))


def _silu(x):
    return x * jax.nn.sigmoid(x)


MM_TM = 1024
MM_TN = 1280
MM_VMEM_BUDGET = 40 * 1024 * 1024


def _mm(a, b, *, ta=False, tb=False, out_dtype=F32, name, ride=None):
    if ta:
        kdim, m = a.shape
    else:
        m, kdim = a.shape
    if tb:
        n, kb = b.shape
    else:
        kb, n = b.shape
    assert kdim == kb, (a.shape, b.shape, ta, tb)
    tm, tn = _tile(m, MM_TM), _tile(n, MM_TN)
    abytes, bbytes, obytes = a.dtype.itemsize, b.dtype.itemsize, jnp.dtype(out_dtype).itemsize
    tk = LANES
    for d in range(LANES, kdim + 1, LANES):
        if kdim % d == 0 and 2 * d * (tm * abytes + tn * bbytes) + tm * tn * (2 * obytes + 4) <= MM_VMEM_BUDGET:
            tk = d
    nk = kdim // tk
    ca, cb = (0 if ta else 1), (1 if tb else 0)

    def body(a_ref, b_ref, o_ref, *acc):
        part = _dg(_mx(a_ref[...]), _mx(b_ref[...]), ca, cb)
        if nk == 1:
            o_ref[...] = part.astype(out_dtype)
            return
        acc_ref, = acc
        k = pl.program_id(2)

        @pl.when(k == 0)
        def _():
            acc_ref[...] = part

        @pl.when(k > 0)
        def _():
            acc_ref[...] += part

        @pl.when(k == nk - 1)
        def _():
            o_ref[...] = acc_ref[...].astype(out_dtype)

    a_spec = pl.BlockSpec((tk, tm), lambda i, j, k: (k, i)) if ta else pl.BlockSpec((tm, tk), lambda i, j, k: (i, k))
    b_spec = pl.BlockSpec((tn, tk), lambda i, j, k: (j, k)) if tb else pl.BlockSpec((tk, tn), lambda i, j, k: (k, j))
    (out,), rode = _hosted_call(
        body, name=name, grid=(m // tm, n // tn, nk),
        in_specs=[a_spec, b_spec], out_specs=[pl.BlockSpec((tm, tn), lambda i, j, k: (i, j))],
        out_shape=[jax.ShapeDtypeStruct((m, n), out_dtype)],
        scratch_shapes=[pltpu.VMEM((tm, tn), F32)] if nk > 1 else [],
        sem=("parallel", "parallel", "arbitrary"), args=(a, b), ride=ride)
    return out if ride is None else (out, rode)


def _ln_fwd(h, y, g, b, name):
    t, d = h.shape
    tr = _tile(t, 256)

    def body(h_ref, y_ref, g_ref, b_ref, o_ref):
        r = ALPHA * h_ref[...] + y_ref[...]
        mu = jnp.mean(r, -1, keepdims=True)
        xc = r - mu
        var = jnp.mean(xc * xc, -1, keepdims=True)
        o_ref[...] = xc * lax.rsqrt(var + LN_EPS) * g_ref[...] + b_ref[...]

    row = pl.BlockSpec((tr, d), lambda i: (i, 0))
    par = pl.BlockSpec((1, d), lambda i: (0, 0))
    return pl.pallas_call(
        body, name=name, grid=(t // tr,), in_specs=[row, row, par, par], out_specs=row,
        out_shape=jax.ShapeDtypeStruct((t, d), F32), compiler_params=_cparams(("parallel",)),
    )(h, y, g, b)


def _ln_bwd(h, y, g, dr_up, du_up, name):
    t, d = h.shape
    tr = _tile(t, 256)

    def body(h_ref, y_ref, g_ref, dr_ref, du_ref, o_ref, dg_ref, db_ref):
        i = pl.program_id(0)

        @pl.when(i == 0)
        def _():
            dg_ref[...] = jnp.zeros_like(dg_ref)
            db_ref[...] = jnp.zeros_like(db_ref)

        dout = ALPHA * dr_ref[...] + du_ref[...]
        r = ALPHA * h_ref[...] + y_ref[...]
        mu = jnp.mean(r, -1, keepdims=True)
        xc = r - mu
        rstd = lax.rsqrt(jnp.mean(xc * xc, -1, keepdims=True) + LN_EPS)
        xh = xc * rstd
        dxh = dout * g_ref[...]
        o_ref[...] = rstd * (dxh - jnp.mean(dxh, -1, keepdims=True) - xh * jnp.mean(dxh * xh, -1, keepdims=True))
        dg_ref[...] += jnp.sum(dout * xh, 0, keepdims=True)
        db_ref[...] += jnp.sum(dout, 0, keepdims=True)

    row = pl.BlockSpec((tr, d), lambda i: (i, 0))
    par = pl.BlockSpec((1, d), lambda i: (0, 0))
    return pl.pallas_call(
        body, name=name, grid=(t // tr,), in_specs=[row, row, par, row, row], out_specs=[row, par, par],
        out_shape=[jax.ShapeDtypeStruct((t, d), F32), jax.ShapeDtypeStruct((1, d), F32), jax.ShapeDtypeStruct((1, d), F32)],
        compiler_params=_cparams(("arbitrary",)),
    )(h, y, g, dr_up, du_up)


def _loss_head(h, tgt, name):
    t, d = h.shape
    tr = _tile(t, 256)

    def body(h_ref, t_ref, l_ref, d_ref):
        i = pl.program_id(0)

        @pl.when(i == 0)
        def _():
            l_ref[...] = jnp.zeros_like(l_ref)

        e = h_ref[...] - t_ref[...]
        d_ref[...] = e * (1.0 / d)
        l_ref[...] += 0.5 * jnp.sum(jnp.mean(e * e, -1, keepdims=True))

    row = pl.BlockSpec((tr, d), lambda i: (i, 0))
    return pl.pallas_call(
        body, name=name, grid=(t // tr,), in_specs=[row, row],
        out_specs=[pl.BlockSpec((8, LANES), lambda i: (0, 0)), row],
        out_shape=[jax.ShapeDtypeStruct((8, LANES), F32), jax.ShapeDtypeStruct((t, d), F32)],
        compiler_params=_cparams(("arbitrary",)),
    )(h, tgt)


def _axpy(dr, du, name):
    t, d = dr.shape
    tr = _tile(t, 256)

    def body(a_ref, b_ref, o_ref):
        o_ref[...] = ALPHA * a_ref[...] + b_ref[...]

    row = pl.BlockSpec((tr, d), lambda i: (i, 0))
    return pl.pallas_call(
        body, name=name, grid=(t // tr,), in_specs=[row, row], out_specs=row,
        out_shape=jax.ShapeDtypeStruct((t, d), F32), compiler_params=_cparams(("parallel",)),
    )(dr, du)


CONV_TT = 512
CONV_TC = 512


HALO = 8


def _shift_down(cur, halo, s, row):
    if s == 0:
        return cur
    tt = cur.shape[0]
    edge = pltpu.roll(halo, s, 0)
    if tt > HALO:
        edge = jnp.concatenate([edge, jnp.zeros((tt - HALO, cur.shape[1]), cur.dtype)], axis=0)
    return jnp.where(row >= s, pltpu.roll(cur, s, 0), edge)


def _shift_up(cur, halo, s, row, tt):
    if s == 0:
        return cur
    edge = jnp.concatenate([jnp.zeros((tt - HALO, cur.shape[1]), cur.dtype), pltpu.roll(halo, HALO - s, 0)], axis=0)
    return jnp.where(row < tt - s, pltpu.roll(cur, tt - s, 0), edge)


def _conv_fwd(proj, col0, w, b, name, ride=None):
    t = proj.shape[0]
    c = w.shape[1]
    tt = _tile(t, CONV_TT)
    cb0 = col0 // CONV_TC

    def body(x_ref, p_ref, w_ref, b_ref, o_ref):
        i = pl.program_id(1)
        x = x_ref[...]
        p = jnp.where(i > 0, p_ref[...], 0.0)
        row = lax.broadcasted_iota(jnp.int32, x.shape, 0)
        pre = b_ref[...] + w_ref[3:4, :] * x
        for s in (1, 2, 3):
            pre = pre + w_ref[3 - s:4 - s, :] * _shift_down(x, p, s, row)
        o_ref[...] = _silu(pre)

    (act,), rode = _hosted_call(
        body, name=name, grid=(c // CONV_TC, t // tt),
        in_specs=[pl.BlockSpec((tt, CONV_TC), lambda j, i: (i, cb0 + j)),
                  pl.BlockSpec((HALO, CONV_TC), lambda j, i: (jnp.maximum(i * (tt // HALO) - 1, 0), cb0 + j)),
                  pl.BlockSpec((4, CONV_TC), lambda j, i: (0, j)),
                  pl.BlockSpec((1, CONV_TC), lambda j, i: (0, j))],
        out_specs=[pl.BlockSpec((tt, CONV_TC), lambda j, i: (i, j))],
        out_shape=[jax.ShapeDtypeStruct((t, c), F32)], scratch_shapes=[],
        sem=("parallel", "parallel"), args=(proj, proj, w, b), ride=ride)
    return act, rode


def _conv_bwd(proj, col0, w, b, dact, name):
    t = proj.shape[0]
    c = w.shape[1]
    tt = _tile(t, CONV_TT)
    nt = t // tt
    cb0 = col0 // CONV_TC
    per = tt // HALO

    def dpre_of(x, halo, d, w_ref, b_ref):
        row = lax.broadcasted_iota(jnp.int32, x.shape, 0)
        sh = [_shift_down(x, halo, s, row) for s in range(4)]
        pre = b_ref[...] + w_ref[3:4, :] * sh[0]
        for s in (1, 2, 3):
            pre = pre + w_ref[3 - s:4 - s, :] * sh[s]
        sg = jax.nn.sigmoid(pre)
        return d * (sg * (1.0 + pre * (1.0 - sg))), sh

    def body(x_ref, p_ref, n_ref, w_ref, b_ref, d_ref, dn_ref, dx_ref, dw_ref, db_ref):
        i = pl.program_id(1)

        @pl.when(i == 0)
        def _():
            dw_ref[...] = jnp.zeros_like(dw_ref)
            db_ref[...] = jnp.zeros_like(db_ref)

        x = x_ref[...]
        dpre, sh = dpre_of(x, jnp.where(i > 0, p_ref[...], 0.0), d_ref[...], w_ref, b_ref)
        dnext, _ = dpre_of(n_ref[...], x[tt - HALO:, :], jnp.where(i < nt - 1, dn_ref[...], 0.0), w_ref, b_ref)
        row = lax.broadcasted_iota(jnp.int32, x.shape, 0)
        acc = w_ref[3:4, :] * dpre
        for s in (1, 2, 3):
            acc = acc + w_ref[3 - s:4 - s, :] * _shift_up(dpre, dnext, s, row, tt)
        dx_ref[...] = acc
        for s in range(4):
            dw_ref[3 - s:4 - s, :] += jnp.sum(dpre * sh[s], 0, keepdims=True)
        db_ref[...] += jnp.sum(dpre, 0, keepdims=True)

    tile = lambda off: pl.BlockSpec((tt, CONV_TC), lambda j, i: (i, off + j))
    before = lambda off: pl.BlockSpec((HALO, CONV_TC), lambda j, i: (jnp.maximum(i * per - 1, 0), off + j))
    after = lambda off: pl.BlockSpec((HALO, CONV_TC), lambda j, i: (jnp.minimum((i + 1) * per, t // HALO - 1), off + j))
    wspec = lambda rows: pl.BlockSpec((rows, CONV_TC), lambda j, i: (0, j))
    return pl.pallas_call(
        body, name=name, grid=(c // CONV_TC, nt),
        in_specs=[tile(cb0), before(cb0), after(cb0), wspec(4), wspec(1), tile(0), after(0)],
        out_specs=[tile(0), wspec(4), wspec(1)],
        out_shape=[jax.ShapeDtypeStruct((t, c), F32), jax.ShapeDtypeStruct((4, c), F32), jax.ShapeDtypeStruct((1, c), F32)],
        compiler_params=_cparams(("parallel", "arbitrary")),
    )(proj, proj, proj, w, b, dact, dact)


SSD_GB = 4


def _split3(a):
    a1 = _mx(a)
    r = a - a1.astype(F32)
    a2 = _mx(r)
    return a1, a2, _mx(r - a2.astype(F32))


@jax.custom_vjp
def _sel_r(a, c):
    cm = _mx(c)
    p1, p2, p3 = _split3(a)
    return _dg(p1, cm, 1, 0) + (_dg(p2, cm, 1, 0) + _dg(p3, cm, 1, 0))


def _sel_r_f(a, c):
    return _sel_r(a, c), c


def _sel_r_b(c, ct):
    cm = _mx(c)
    p1, p2, p3 = _split3(ct)
    return _dg(p1, cm, 1, 1) + (_dg(p2, cm, 1, 1) + _dg(p3, cm, 1, 1)), jnp.zeros_like(c)


_sel_r.defvjp(_sel_r_f, _sel_r_b)


@jax.custom_vjp
def _sel_l(c, a):
    cm = _mx(c)
    p1, p2, p3 = _split3(a)
    return _dg(cm, p1, 1, 0) + (_dg(cm, p2, 1, 0) + _dg(cm, p3, 1, 0))


def _sel_l_f(c, a):
    return _sel_l(c, a), c


def _sel_l_b(c, ct):
    cm = _mx(c)
    p1, p2, p3 = _split3(ct)
    return jnp.zeros_like(c), _dg(cm, p1, 0, 0) + (_dg(cm, p2, 0, 0) + _dg(cm, p3, 0, 0))


_sel_l.defvjp(_sel_l_f, _sel_l_b)


def _ssd_chunk(gb, x, z, bm, cm, dtraw, dtb, alog, dsk, nw, prev):
    L = x.shape[1]
    r_i = lax.broadcasted_iota(jnp.int32, (L, L), 0)
    c_i = lax.broadcasted_iota(jnp.int32, (L, L), 1)
    causal = r_i >= c_i
    dt = _softplus(dtraw + dtb)
    a = dt * (-jnp.exp(alog))
    acs = _sel_l(causal.astype(F32), a)
    e_r = lax.broadcasted_iota(jnp.int32, (LANES, SSD_GS), 0)
    e_c = lax.broadcasted_iota(jnp.int32, (LANES, SSD_GS), 1)
    hpg = SSD_H // SSD_G
    sels = [(e_r == (gb * SSD_GB + i) * hpg + jnp.right_shift(e_c, 6)).astype(F32) for i in range(SSD_GB)]
    dt_x = _cat0([_sel_r(dt, s) for s in sels])
    acs_x = _cat0([_sel_r(acs, s) for s in sels])
    d_x = _cat0([_sel_r(jnp.broadcast_to(dsk, (8, LANES)), s)[0:1] for s in sels])
    last = lax.broadcasted_iota(jnp.int32, (L, 1), 0) == L - 1
    alast = jnp.sum(jnp.where(last, acs_x, 0.0), axis=1, keepdims=True)
    xdt = x * dt_x
    cb = _bnt(cm, bm)
    lane = lax.broadcasted_iota(jnp.int32, (L, LANES), 1)
    ys = []
    for j in range(SSD_GS // LANES):
        xp = xdt[:, :, j * LANES:(j + 1) * LANES]
        yp = None
        for hh in range(2):
            c0 = (2 * j + hh) * SSD_P
            cmx = jnp.broadcast_to(acs_x[:, :, c0:c0 + 1], (SSD_GB, L, L))
            dec = jnp.exp(jnp.where(causal, cmx - jnp.swapaxes(cmx, 1, 2), -jnp.inf))
            half = (lane < SSD_P) if hh == 0 else (lane >= SSD_P)
            t = _bnn(cb * dec, jnp.where(half, xp, 0.0))
            yp = t if yp is None else yp + t
        ys.append(yp)
    y_diag = jnp.concatenate(ys, axis=2)
    st = _btn(bm, xdt * jnp.exp(alast - acs_x))
    new = prev * jnp.exp(alast) + st
    y_off = _bnn(cm, prev) * jnp.exp(acs_x)
    y = y_diag + y_off + x * d_x
    yg = y * _silu(z)
    yn = yg * lax.rsqrt(jnp.mean(yg * yg, -1, keepdims=True) + RMS_EPS) * nw
    return yn, new


def _groups_of(ref, width):
    return _cat0([ref[:, i * width:(i + 1) * width] for i in range(SSD_GB)])


def _put_groups(ref, val, width):
    for i in range(SSD_GB):
        ref[:, i * width:(i + 1) * width] = val[i].astype(ref.dtype)


def _ssd_specs(nc, rev):
    cc = (lambda c: nc - 1 - c) if rev else (lambda c: c)
    wx, wb = SSD_GB * SSD_GS, SSD_GB * SSD_N
    dtb = (SSD_DI + SSD_CONV_DIM) // LANES
    bb = SSD_DI // wb
    cbk = (SSD_DI + SSD_G * SSD_N) // wb
    par = pl.BlockSpec((1, LANES), lambda c, g: (0, 0))
    return dict(
        z=pl.BlockSpec((SSD_L, wx), lambda c, g: (cc(c), g)),
        dt=pl.BlockSpec((SSD_L, LANES), lambda c, g: (cc(c), dtb)),
        x=pl.BlockSpec((SSD_L, wx), lambda c, g: (cc(c), g)),
        bm=pl.BlockSpec((SSD_L, wb), lambda c, g: (cc(c), bb + g)),
        cm=pl.BlockSpec((SSD_L, wb), lambda c, g: (cc(c), cbk + g)),
        par=par,
        nw=pl.BlockSpec((1, wx), lambda c, g: (0, g)),
        st=pl.BlockSpec((1, SSD_GB, SSD_N, SSD_GS), lambda c, g: (cc(c), g, 0, 0)),
        y=pl.BlockSpec((SSD_L, wx), lambda c, g: (cc(c), g)),
        bc=pl.BlockSpec((SSD_L, wb), lambda c, g: (cc(c), g)),
        dtout=pl.BlockSpec((SSD_L, LANES), lambda c, g: (cc(c), 0)),
    )


def _ssd_fwd(proj, act, dtb, alog, dsk, nw, name, ride=None):
    t = proj.shape[0]
    nc = t // SSD_L
    sp = _ssd_specs(nc, False)

    def body(z_ref, dt_ref, x_ref, bm_ref, cm_ref, dtb_ref, alog_ref, dsk_ref, nw_ref, y_ref, st_ref, state):
        c, g = pl.program_id(0), pl.program_id(1)
        mine = pl.ds(g * SSD_GB, SSD_GB)

        @pl.when(c == 0)
        def _():
            state[mine] = jnp.zeros((SSD_GB, SSD_N, SSD_GS), F32)

        prev = state[mine]
        st_ref[0] = prev
        yn, new = _ssd_chunk(g, _groups_of(x_ref, SSD_GS), _groups_of(z_ref, SSD_GS), _groups_of(bm_ref, SSD_N),
                             _groups_of(cm_ref, SSD_N), dt_ref[...], dtb_ref[...], alog_ref[...], dsk_ref[...],
                             _groups_of(nw_ref, SSD_GS), prev)
        _put_groups(y_ref, yn, SSD_GS)
        state[mine] = new

    return _hosted_call(
        body, name=name, grid=(nc, SSD_G // SSD_GB),
        in_specs=[sp["z"], sp["dt"], sp["x"], sp["bm"], sp["cm"], sp["par"], sp["par"], sp["par"], sp["nw"]],
        out_specs=[sp["y"], sp["st"]],
        out_shape=[jax.ShapeDtypeStruct((t, SSD_DI), MXU), jax.ShapeDtypeStruct((nc, SSD_G, SSD_N, SSD_GS), F32)],
        scratch_shapes=[pltpu.VMEM((SSD_G, SSD_N, SSD_GS), F32)],
        sem=("arbitrary", "arbitrary"), args=(proj, proj, act, act, act, dtb, alog, dsk, nw), ride=ride)


def _ssd_bwd(proj, act, dtb, alog, dsk, nw, states, dyn, name, ride=None):
    t = proj.shape[0]
    nc = t // SSD_L
    sp = _ssd_specs(nc, True)

    def body(z_ref, dt_ref, x_ref, bm_ref, cm_ref, dtb_ref, alog_ref, dsk_ref, nw_ref, st_ref, dy_ref,
             dact_ref, dz_ref, ddt_ref, ddtb_ref, dalog_ref, ddsk_ref, dnw_ref, dstate):
        c, g = pl.program_id(0), pl.program_id(1)
        mine = pl.ds(g * SSD_GB, SSD_GB)

        @pl.when(c == 0)
        def _():
            dstate[mine] = jnp.zeros((SSD_GB, SSD_N, SSD_GS), F32)

        @pl.when((c == 0) & (g == 0))
        def _():
            ddtb_ref[...] = jnp.zeros_like(ddtb_ref)
            dalog_ref[...] = jnp.zeros_like(dalog_ref)
            ddsk_ref[...] = jnp.zeros_like(ddsk_ref)
            dnw_ref[...] = jnp.zeros_like(dnw_ref)

        @pl.when(g == 0)
        def _():
            ddt_ref[...] = jnp.zeros_like(ddt_ref)

        _, vjp = jax.vjp(functools.partial(_ssd_chunk, g), _groups_of(x_ref, SSD_GS), _groups_of(z_ref, SSD_GS),
                         _groups_of(bm_ref, SSD_N), _groups_of(cm_ref, SSD_N), dt_ref[...], dtb_ref[...], alog_ref[...],
                         dsk_ref[...], _groups_of(nw_ref, SSD_GS), st_ref[0])
        dx, dz, dbm, dcm, ddt, ddtb, dalog, ddsk, dnw, dprev = vjp((_groups_of(dy_ref, SSD_GS), dstate[mine]))
        _put_groups(dz_ref, dz, SSD_GS)
        for gi in range(SSD_G // SSD_GB):
            @pl.when(g == gi)
            def _(gi=gi):
                for i in range(SSD_GB):
                    gg = gi * SSD_GB + i
                    dact_ref[:, gg * SSD_GS:(gg + 1) * SSD_GS] = dx[i]
                    dact_ref[:, SSD_DI + gg * SSD_N:SSD_DI + (gg + 1) * SSD_N] = dbm[i]
                    dact_ref[:, SSD_DI + (SSD_G + gg) * SSD_N:SSD_DI + (SSD_G + gg + 1) * SSD_N] = dcm[i]
        ddt_ref[...] += ddt
        ddtb_ref[...] += ddtb
        dalog_ref[...] += dalog
        ddsk_ref[...] += ddsk
        dnw_ref[mine] += dnw
        dstate[mine] = dprev

    par_out = pl.BlockSpec((1, LANES), lambda c, g: (0, 0))
    sds = jax.ShapeDtypeStruct
    return _hosted_call(
        body, name=name, grid=(nc, SSD_G // SSD_GB),
        in_specs=[sp["z"], sp["dt"], sp["x"], sp["bm"], sp["cm"], sp["par"], sp["par"], sp["par"], sp["nw"],
                  sp["st"], sp["y"]],
        out_specs=[pl.BlockSpec((SSD_L, SSD_CONV_DIM), lambda c, g: (nc - 1 - c, 0)), sp["y"], sp["dtout"],
                   par_out, par_out, par_out, pl.BlockSpec((SSD_G, 1, SSD_GS), lambda c, g: (0, 0, 0))],
        out_shape=[sds((t, SSD_CONV_DIM), F32), sds((t, SSD_DI), F32), sds((t, LANES), F32), sds((1, LANES), F32),
                   sds((1, LANES), F32), sds((1, LANES), F32), sds((SSD_G, 1, SSD_GS), F32)],
        scratch_shapes=[pltpu.VMEM((SSD_G, SSD_N, SSD_GS), F32)],
        sem=("arbitrary", "arbitrary"), args=(proj, proj, act, act, act, dtb, alog, dsk, nw, states, dyn), ride=ride)


def _pad_lanes(v, width=LANES, offset=0):
    return jnp.pad(v.astype(F32), (offset, width - offset - v.shape[0])).reshape(1, width)


def _ride_of(rides, key, scatter):
    arrs = (rides or {}).get(key)
    return (arrs, scatter) if arrs else None


def _mm_r(a, b, ride, **kw):
    out = _mm(a, b, ride=ride, **kw)
    return (out, []) if ride is None else out


def _ssd_layer_fwd(u, w, ln_g, ln_b, tag, rides=None, late=None):
    w_in = jnp.concatenate([w["in_w"], jnp.zeros((D_MODEL, LANES - SSD_H), w["in_w"].dtype)], axis=1)
    dtb, alog, dsk = _pad_lanes(w["dt_bias"]), _pad_lanes(w["a_log"]), _pad_lanes(w["d"])
    if late is None:
        proj = _mm(u.astype(MXU), w_in, name=tag + "_in")
    else:
        proj, got = _mm(u.astype(MXU), w_in, name=tag + "_in", ride=(late[0], False))
        w.update(late[1](got))
    nw = w["norm_w"].reshape(1, SSD_DI)
    cb = w["conv_b"].reshape(1, SSD_CONV_DIM)
    rode = {}
    act, rode["conv"] = _conv_fwd(proj, SSD_DI, w["conv_w"], cb, name=tag + "_conv", ride=_ride_of(rides, "conv", False))
    (yn, states), rode["scan"] = _ssd_fwd(proj, act, dtb, alog, dsk, nw, name=tag + "_scan", ride=_ride_of(rides, "scan", False))
    y, rode["out"] = _mm_r(yn, w["out_w"], _ride_of(rides, "out", False), name=tag + "_out")
    h = _ln_fwd(u, y, ln_g, ln_b, name=tag + "_ln")
    saved = dict(u=u, w_in=w_in, proj=proj, act=act, states=states, yn=yn, y=y, dtb=dtb, alog=alog, dsk=dsk, nw=nw, cb=cb)
    return h, saved, rode


def _ssd_layer_bwd(s, w, ln_g, dr_up, du_up, tag, rides=None, own=None):
    dr, dg, db = _ln_bwd(s["u"], s["y"], ln_g, dr_up, du_up, name=tag + "_ln_b")
    dyn = _mm(dr, w["out_w"], tb=True, name=tag + "_out_bx")
    d_out_w = _mm(s["yn"], dr, ta=True, out_dtype=WIRE, name=tag + "_out_bw")
    rode = {}
    (dact, dz, ddt, ddtb, dalog, ddsk, dnw), rode["scan_b"] = _ssd_bwd(
        s["proj"], s["act"], s["dtb"], s["alog"], s["dsk"], s["nw"], s["states"], dyn, name=tag + "_scan_b",
        ride=_ride_of(rides, "scan_b", True))
    dxbc, d_conv_w, d_conv_b = _conv_bwd(s["proj"], SSD_DI, w["conv_w"], s["cb"], dact, name=tag + "_conv_b")
    dproj = jnp.concatenate([dz, dxbc, ddt], axis=1).astype(MXU)
    grads = dict(conv_w=d_conv_w, conv_b=d_conv_b.reshape(-1), dt_bias=ddtb[0, :SSD_H], a_log=dalog[0, :SSD_H],
                 d=ddsk[0, :SSD_H], norm_w=dnw.reshape(-1), out_w=d_out_w, ln_g=dg[0], ln_b=db[0])
    late_names = ("conv_w", "conv_b", "norm_w", "out_w")
    got = {}
    if own is None:
        d_in_w = _mm(s["u"].astype(MXU), dproj, ta=True, out_dtype=WIRE, name=tag + "_in_bw")
        du = _mm(dproj, s["w_in"], tb=True, name=tag + "_in_bx")
        grads["in_w"] = d_in_w[:, :SSD_DI + SSD_CONV_DIM + SSD_H]
    else:
        d_in_w, late = _mm(s["u"].astype(MXU), dproj, ta=True, out_dtype=WIRE, name=tag + "_in_bw",
                           ride=([own(k, grads[k]) for k in late_names], True))
        got = dict(zip(late_names, late))
        grads["in_w"] = d_in_w[:, :SSD_DI + SSD_CONV_DIM + SSD_H]
        du, (got["in_w"],) = _mm(dproj, s["w_in"], tb=True, name=tag + "_in_bx", ride=([own("in_w", grads["in_w"])], True))
    return dr, du, grads, rode, got


MLA_LOW = MLA_QR + MLA_KVR + LANES
MLA_ZB = MLA_LOW // LANES


def _rope_mat():
    r = lax.broadcasted_iota(jnp.int32, (LANES, LANES), 0)
    c = lax.broadcasted_iota(jnp.int32, (LANES, LANES), 1)
    hf = MLA_ROPE // 2
    return jnp.where((c < hf) & (r == c + hf), -1.0, 0.0) + jnp.where((c >= hf) & (c < 2 * hf) & (r == c - hf), 1.0, 0.0)


def _rope(x, cosf, sinf):
    return x * cosf + _sel_r(x, _rope_mat()) * sinf


def _rope_adj(d, cosf, sinf):
    return d * cosf - _sel_r(d * sinf, _rope_mat())


def _mla_low_fn(low, qnw, kvnw, cosf, sinf):
    qc, kvc, kr = low[:, :MLA_QR], low[:, MLA_QR:MLA_QR + MLA_KVR], low[:, MLA_QR + MLA_KVR:]
    qn = qc * lax.rsqrt(jnp.mean(qc * qc, -1, keepdims=True) + RMS_EPS) * qnw
    kvn = kvc * lax.rsqrt(jnp.mean(kvc * kvc, -1, keepdims=True) + RMS_EPS) * kvnw
    return qn, kvn, _rope(kr, cosf, sinf)


def _mla_low_fwd(proj, qnw, kvnw, cosf, sinf, name):
    t = proj.shape[0]
    tr = _tile(t, 256)

    def body(low_ref, qnw_ref, kvnw_ref, cos_ref, sin_ref, qn_ref, kvn_ref, kr_ref):
        qn, kvn, kr = _mla_low_fn(low_ref[...], qnw_ref[...], kvnw_ref[...], cos_ref[...], sin_ref[...])
        qn_ref[...] = qn
        kvn_ref[...] = kvn
        kr_ref[...] = kr

    row = lambda wdt: pl.BlockSpec((tr, wdt), lambda i: (i, 0))
    par = lambda wdt: pl.BlockSpec((1, wdt), lambda i: (0, 0))
    sds = jax.ShapeDtypeStruct
    return pl.pallas_call(
        body, name=name, grid=(t // tr,),
        in_specs=[row(MLA_LOW), par(MLA_QR), par(MLA_KVR), row(LANES), row(LANES)],
        out_specs=[row(MLA_QR), row(MLA_KVR), row(LANES)],
        out_shape=[sds((t, MLA_QR), F32), sds((t, MLA_KVR), F32), sds((t, LANES), F32)],
        compiler_params=_cparams(("parallel",)),
    )(proj, qnw, kvnw, cosf, sinf)


def _mla_low_bwd(proj, qnw, kvnw, cosf, sinf, dqn, dkvn, dkr, name):
    t = proj.shape[0]
    tr = _tile(t, 256)

    def body(low_ref, qnw_ref, kvnw_ref, cos_ref, sin_ref, dqn_ref, dkvn_ref, dkr_ref, dlow_ref, dqnw_ref, dkvnw_ref):
        i = pl.program_id(0)

        @pl.when(i == 0)
        def _():
            dqnw_ref[...] = jnp.zeros_like(dqnw_ref)
            dkvnw_ref[...] = jnp.zeros_like(dkvnw_ref)

        cosf, sinf = cos_ref[...], sin_ref[...]
        _, vjp = jax.vjp(lambda a, b, c: _mla_low_fn(a, b, c, cosf, sinf), low_ref[...], qnw_ref[...], kvnw_ref[...])
        dlow, dq, dk = vjp((dqn_ref[...], dkvn_ref[...], dkr_ref[...]))
        dlow_ref[...] = dlow
        dqnw_ref[...] += dq
        dkvnw_ref[...] += dk

    row = lambda wdt: pl.BlockSpec((tr, wdt), lambda i: (i, 0))
    par = lambda wdt: pl.BlockSpec((1, wdt), lambda i: (0, 0))
    sds = jax.ShapeDtypeStruct
    return pl.pallas_call(
        body, name=name, grid=(t // tr,),
        in_specs=[row(MLA_LOW), par(MLA_QR), par(MLA_KVR), row(LANES), row(LANES), row(MLA_QR), row(MLA_KVR), row(LANES)],
        out_specs=[row(MLA_LOW), par(MLA_QR), par(MLA_KVR)],
        out_shape=[sds((t, MLA_LOW), F32), sds((1, MLA_QR), F32), sds((1, MLA_KVR), F32)],
        compiler_params=_cparams(("arbitrary",)),
    )(proj, qnw, kvnw, cosf, sinf, dqn, dkvn, dkr)


def _rope_heads(x, col_blk0, cosf, sinf, adjoint, name):
    t = x.shape[0]
    tr = _tile(t, 512)

    def body(x_ref, cos_ref, sin_ref, o_ref):
        f = _rope_adj if adjoint else _rope
        o_ref[...] = f(x_ref[...], cos_ref[...], sin_ref[...])

    tab = pl.BlockSpec((tr, LANES), lambda i, h: (i, 0))
    return pl.pallas_call(
        body, name=name, grid=(t // tr, MLA_H),
        in_specs=[pl.BlockSpec((tr, LANES), lambda i, h: (i, col_blk0 + h)), tab, tab],
        out_specs=pl.BlockSpec((tr, LANES), lambda i, h: (i, h)),
        out_shape=jax.ShapeDtypeStruct((t, MLA_H * LANES), F32), compiler_params=_cparams(("parallel", "parallel")),
    )(x, cosf, sinf)


ATT_HB = 4
ATT_W = ATT_HB * LANES


def _att_qk(qn_ref, qr_ref, kn_ref, kr_ref):
    q2 = jnp.concatenate([_heads_of(qn_ref, ATT_HB), _heads_of(qr_ref, ATT_HB)], axis=2)
    kr = kr_ref[...]
    k2 = jnp.concatenate([_heads_of(kn_ref, ATT_HB), jnp.broadcast_to(kr[None], (ATT_HB,) + kr.shape)], axis=2)
    return q2, k2


def _att_scores(q2, k2, masked):
    s = _bdg(_mx(q2 * MLA_SCALE), _mx(k2), 2, 2)
    if masked:
        tq, tk = s.shape[1:]
        s = jnp.where(lax.broadcasted_iota(jnp.int32, (tq, tk), 1) <= lax.broadcasted_iota(jnp.int32, (tq, tk), 0), s, -jnp.inf)
    return s


def _on_causal_blocks(q_blk, k_blk, step):
    @pl.when(k_blk < q_blk)
    def _():
        step(False)

    @pl.when(k_blk == q_blk)
    def _():
        step(True)


def _put_heads(ref, val):
    for b in range(val.shape[0]):
        ref[:, b * LANES:(b + 1) * LANES] = val[b].astype(ref.dtype)


def _att_ds(s, v_ref, o_ref, do_ref, lse_ref):
    do = _heads_of(do_ref, ATT_HB)
    p = jnp.exp(s - _heads_of(lse_ref, ATT_HB)[:, :, 0:1])
    dp = _bdg(_mx(do), _mx(_heads_of(v_ref, ATT_HB)), 2, 2)
    dl = jnp.sum(do * _heads_of(o_ref, ATT_HB), -1, keepdims=True)
    return _mx(p), _mx(p * (dp - dl) * MLA_SCALE), do


def _attn_fwd(q, qr, kv, kr, proj, name, ride=None):
    t = q.shape[0]
    tq = tk = _tile(t, ATT_BLK)
    nq = nk = t // tq

    def body(qn_ref, qr_ref, kn_ref, kr_ref, v_ref, *rest):
        z_refs, (o_ref, og_ref, lse_ref, m_s, l_s, acc_s) = rest[:ATT_HB], rest[ATT_HB:]
        i, j = pl.program_id(1), pl.program_id(2)

        @pl.when(j == 0)
        def _():
            m_s[...] = jnp.full_like(m_s, -jnp.inf)
            l_s[...] = jnp.zeros_like(l_s)
            acc_s[...] = jnp.zeros_like(acc_s)

        def step(masked):
            s = _att_scores(*_att_qk(qn_ref, qr_ref, kn_ref, kr_ref), masked)
            m_new = jnp.maximum(m_s[...], jnp.max(s, -1, keepdims=True))
            p = jnp.exp(s - m_new)
            corr = jnp.exp(m_s[...] - m_new)
            l_s[...] = corr * l_s[...] + jnp.sum(p, -1, keepdims=True)
            acc_s[...] = corr * acc_s[...] + _bdg(_mx(p), _mx(_heads_of(v_ref, ATT_HB)), 2, 1)
            m_s[...] = m_new

        _on_causal_blocks(i, j, step)

        @pl.when(j == nk - 1)
        def _():
            o = acc_s[...] / l_s[...]
            _put_heads(o_ref, o)
            lse = m_s[...] + jnp.log(l_s[...])
            for b in range(ATT_HB):
                og_ref[:, b * LANES:(b + 1) * LANES] = (o[b] * _silu(z_refs[b][...])).astype(og_ref.dtype)
                lse_ref[:, b * LANES:(b + 1) * LANES] = jnp.broadcast_to(lse[b], (tq, LANES))

    qs = lambda off: pl.BlockSpec((tq, ATT_W), lambda h, i, j: (i, off // ATT_HB + h))
    ks = lambda off: pl.BlockSpec((tk, ATT_W), lambda h, i, j: (jnp.minimum(j, i), off // ATT_HB + h))
    zs = [pl.BlockSpec((tq, LANES), functools.partial(lambda b, h, i, j: (i, MLA_ZB + h * ATT_HB + b), b)) for b in range(ATT_HB)]
    sds = jax.ShapeDtypeStruct
    return _hosted_call(
        body, name=name, grid=(MLA_H // ATT_HB, nq, nk),
        in_specs=[qs(0), qs(0), ks(0), pl.BlockSpec((tk, LANES), lambda h, i, j: (jnp.minimum(j, i), 0)), ks(MLA_H)] + zs,
        out_specs=[qs(0), qs(0), qs(0)],
        out_shape=[sds((t, MLA_GATE), F32), sds((t, MLA_GATE), MXU), sds((t, MLA_H * LANES), F32)],
        scratch_shapes=[pltpu.VMEM((ATT_HB, tq, 1), F32), pltpu.VMEM((ATT_HB, tq, 1), F32), pltpu.VMEM((ATT_HB, tq, LANES), F32)],
        sem=("parallel", "parallel", "arbitrary"), args=(q, qr, kv, kr, kv) + (proj,) * ATT_HB, ride=ride)


def _gate_bwd(dog, o, proj, name):
    t = o.shape[0]
    tr = _tile(t, 512)

    def body(d_ref, o_ref, z_ref, do_ref, dz_ref):
        z = z_ref[...]
        sg = jax.nn.sigmoid(z)
        d = d_ref[...]
        do_ref[...] = d * z * sg
        dz_ref[...] = d * o_ref[...] * (sg * (1.0 + z * (1.0 - sg)))

    blk = lambda off: pl.BlockSpec((tr, 512), lambda i, j: (i, off + j))
    assert MLA_LOW % 512 != 0 or True
    zspec = pl.BlockSpec((tr, LANES), lambda i, j: (i, MLA_ZB + j))
    b128 = pl.BlockSpec((tr, LANES), lambda i, j: (i, j))
    sds = jax.ShapeDtypeStruct
    return pl.pallas_call(
        body, name=name, grid=(t // tr, MLA_GATE // LANES),
        in_specs=[b128, b128, zspec], out_specs=[b128, b128],
        out_shape=[sds((t, MLA_GATE), F32), sds((t, MLA_GATE), F32)],
        compiler_params=_cparams(("parallel", "parallel")),
    )(dog, o, proj)


def _attn_bwd_q(q, qr, kv, kr, o, do, lse, name, ride=None):
    t = q.shape[0]
    tq = tk = _tile(t, ATT_BLK)
    nq = nk = t // tq

    def body(qn_ref, qr_ref, kn_ref, kr_ref, v_ref, o_ref, do_ref, lse_ref, dqn_ref, dqr_ref, an_s, ar_s):
        i, j = pl.program_id(1), pl.program_id(2)

        @pl.when(j == 0)
        def _():
            an_s[...] = jnp.zeros_like(an_s)
            ar_s[...] = jnp.zeros_like(ar_s)

        def step(masked):
            q2, k2 = _att_qk(qn_ref, qr_ref, kn_ref, kr_ref)
            ds = _att_ds(_att_scores(q2, k2, masked), v_ref, o_ref, do_ref, lse_ref)[1]
            dq2 = _bdg(ds, _mx(k2), 2, 1)
            an_s[...] += dq2[:, :, :LANES]
            ar_s[...] += dq2[:, :, LANES:]

        _on_causal_blocks(i, j, step)

        @pl.when(j == nk - 1)
        def _():
            _put_heads(dqn_ref, an_s[...])
            _put_heads(dqr_ref, ar_s[...])

    qs = lambda off: pl.BlockSpec((tq, ATT_W), lambda h, i, j: (i, off // ATT_HB + h))
    ks = lambda off: pl.BlockSpec((tk, ATT_W), lambda h, i, j: (jnp.minimum(j, i), off // ATT_HB + h))
    sds = jax.ShapeDtypeStruct
    return _hosted_call(
        body, name=name, grid=(MLA_H // ATT_HB, nq, nk),
        in_specs=[qs(0), qs(0), ks(0), pl.BlockSpec((tk, LANES), lambda h, i, j: (jnp.minimum(j, i), 0)), ks(MLA_H),
                  qs(0), qs(0), qs(0)],
        out_specs=[qs(0), qs(0)],
        out_shape=[sds((t, MLA_H * LANES), F32), sds((t, MLA_H * LANES), F32)],
        scratch_shapes=[pltpu.VMEM((ATT_HB, tq, LANES), F32), pltpu.VMEM((ATT_HB, tq, LANES), F32)],
        sem=("parallel", "parallel", "arbitrary"), args=(q, qr, kv, kr, kv, o, do, lse), ride=ride)


def _attn_bwd_kv(q, qr, kv, kr, o, do, lse, name, ride=None):
    t = q.shape[0]
    tq = tk = _tile(t, ATT_BLK)
    nq = nk = t // tq

    def body(qn_ref, qr_ref, kn_ref, kr_ref, v_ref, o_ref, do_ref, lse_ref, dkn_ref, dv_ref, dkr_ref, akn_s, av_s):
        j, h, i = pl.program_id(0), pl.program_id(1), pl.program_id(2)

        @pl.when((h == 0) & (i == 0))
        def _():
            dkr_ref[...] = jnp.zeros_like(dkr_ref)

        @pl.when(i == 0)
        def _():
            akn_s[...] = jnp.zeros_like(akn_s)
            av_s[...] = jnp.zeros_like(av_s)

        def step(masked):
            q2, k2 = _att_qk(qn_ref, qr_ref, kn_ref, kr_ref)
            p, ds, do = _att_ds(_att_scores(q2, k2, masked), v_ref, o_ref, do_ref, lse_ref)
            av_s[...] += _bdg(p, _mx(do), 1, 1)
            dk2 = _bdg(ds, _mx(q2), 1, 1)
            akn_s[...] += dk2[:, :, :LANES]
            dkr_ref[...] += jnp.sum(dk2[:, :, LANES:], axis=0)

        _on_causal_blocks(i, j, step)

        @pl.when(i == nq - 1)
        def _():
            _put_heads(dkn_ref, akn_s[...])
            _put_heads(dv_ref, av_s[...])

    qs = lambda off: pl.BlockSpec((tq, ATT_W), lambda j, h, i: (jnp.maximum(i, j), off // ATT_HB + h))
    ks = lambda off: pl.BlockSpec((tk, ATT_W), lambda j, h, i: (j, off // ATT_HB + h))
    sds = jax.ShapeDtypeStruct
    return _hosted_call(
        body, name=name, grid=(nk, MLA_H // ATT_HB, nq),
        in_specs=[qs(0), qs(0), ks(0), pl.BlockSpec((tk, LANES), lambda j, h, i: (j, 0)), ks(MLA_H), qs(0), qs(0), qs(0)],
        out_specs=[ks(0), ks(0), pl.BlockSpec((tk, LANES), lambda j, h, i: (j, 0))],
        out_shape=[sds((t, MLA_H * LANES), F32), sds((t, MLA_H * LANES), F32), sds((t, LANES), F32)],
        scratch_shapes=[pltpu.VMEM((ATT_HB, tk, LANES), F32), pltpu.VMEM((ATT_HB, tk, LANES), F32)],
        sem=("parallel", "arbitrary", "arbitrary"), args=(q, qr, kv, kr, kv, o, do, lse), ride=ride)


def _rope_tables(positions):
    lane = jnp.arange(LANES)
    valid = lane < MLA_ROPE
    inv_freq = ROPE_THETA ** (-(2 * (lane % (MLA_ROPE // 2))).astype(F32) / MLA_ROPE)
    ang = positions.astype(F32)[:, None] * inv_freq[None, :]
    return jnp.where(valid, jnp.cos(ang), 0.0), jnp.where(valid, jnp.sin(ang), 0.0)


def _mla_weights(w):
    dt = w["in_w"].dtype
    iw = w["in_w"]
    c1 = MLA_QR + MLA_KVR + MLA_ROPE
    w_in = jnp.concatenate([iw[:, :c1], jnp.zeros((D_MODEL, LANES - MLA_ROPE), dt), iw[:, c1:]], axis=1)
    qu = w["q_up_w"].reshape(MLA_QR, MLA_H, MLA_NOPE + MLA_ROPE)
    qrope = jnp.concatenate([qu[:, :, MLA_NOPE:], jnp.zeros((MLA_QR, MLA_H, LANES - MLA_ROPE), dt)], axis=2)
    w_q = jnp.concatenate([qu[:, :, :MLA_NOPE].reshape(MLA_QR, -1), qrope.reshape(MLA_QR, -1)], axis=1)
    kvu = w["kv_up_w"].reshape(MLA_KVR, MLA_H, MLA_NOPE + MLA_V)
    w_kv = jnp.concatenate([kvu[:, :, :MLA_NOPE].reshape(MLA_KVR, -1), kvu[:, :, MLA_NOPE:].reshape(MLA_KVR, -1)], axis=1)
    return w_in, w_q, w_kv


def _mla_layer_fwd(u, w, ln_g, ln_b, cosf, sinf, tag, rides=None):
    w_in, w_q, w_kv = _mla_weights(w)
    qnw, kvnw = w["q_norm_w"].reshape(1, -1), w["kv_norm_w"].reshape(1, -1)
    proj = _mm(u.astype(MXU), w_in, name=tag + "_in")
    qn, kvn, kr = _mla_low_fwd(proj, qnw, kvnw, cosf, sinf, name=tag + "_low")
    q = _mm(qn, w_q, name=tag + "_qup")
    kv = _mm(kvn, w_kv, name=tag + "_kvup")
    qr = _rope_heads(q, MLA_H, cosf, sinf, False, name=tag + "_qrope")
    rode = {}
    (o, og, lse), rode["attn"] = _attn_fwd(q, qr, kv, kr, proj, name=tag + "_attn", ride=_ride_of(rides, "attn", False))
    y = _mm(og, w["out_w"], name=tag + "_out")
    h = _ln_fwd(u, y, ln_g, ln_b, name=tag + "_ln")
    saved = dict(u=u, w_in=w_in, w_q=w_q, w_kv=w_kv, qnw=qnw, kvnw=kvnw, proj=proj, qn=qn, kvn=kvn, kr=kr, q=q, kv=kv,
                 qr=qr, o=o, og=og, lse=lse, y=y)
    return h, saved, rode


def _mla_layer_bwd(s, w, ln_g, cosf, sinf, dr_up, du_up, tag, rides=None):
    dr, dg, db = _ln_bwd(s["u"], s["y"], ln_g, dr_up, du_up, name=tag + "_ln_b")
    dog = _mm(dr, w["out_w"], tb=True, name=tag + "_out_bx")
    d_out_w = _mm(s["og"], dr, ta=True, out_dtype=WIRE, name=tag + "_out_bw")
    do, dz = _gate_bwd(dog, s["o"], s["proj"], name=tag + "_gate_b")
    rode = {}
    (dqn_h, dqr_rot), rode["attn_bq"] = _attn_bwd_q(s["q"], s["qr"], s["kv"], s["kr"], s["o"], do, s["lse"],
                                                    name=tag + "_attn_bq", ride=_ride_of(rides, "attn_bq", True))
    (dkn_h, dv_h, dkr_rot), rode["attn_bkv"] = _attn_bwd_kv(s["q"], s["qr"], s["kv"], s["kr"], s["o"], do, s["lse"],
                                                           name=tag + "_attn_bkv", ride=_ride_of(rides, "attn_bkv", True))
    dqr = _rope_heads(dqr_rot, 0, cosf, sinf, True, name=tag + "_qrope_b")
    dq = jnp.concatenate([dqn_h, dqr], axis=1).astype(MXU)
    dkv = jnp.concatenate([dkn_h, dv_h], axis=1).astype(MXU)
    d_wq = _mm(s["qn"], dq, ta=True, out_dtype=WIRE, name=tag + "_qup_bw")
    dqn = _mm(dq, s["w_q"], tb=True, name=tag + "_qup_bx")
    d_wkv = _mm(s["kvn"], dkv, ta=True, out_dtype=WIRE, name=tag + "_kvup_bw")
    dkvn = _mm(dkv, s["w_kv"], tb=True, name=tag + "_kvup_bx")
    dlow, dqnw, dkvnw = _mla_low_bwd(s["proj"], s["qnw"], s["kvnw"], cosf, sinf, dqn, dkvn, dkr_rot, name=tag + "_low_b")
    dproj = jnp.concatenate([dlow, dz], axis=1).astype(MXU)
    d_in = _mm(s["u"].astype(MXU), dproj, ta=True, out_dtype=WIRE, name=tag + "_in_bw")
    du = _mm(dproj, s["w_in"], tb=True, name=tag + "_in_bx")
    c1 = MLA_QR + MLA_KVR + MLA_ROPE
    d_in_w = jnp.concatenate([d_in[:, :c1], d_in[:, MLA_LOW:]], axis=1)
    dq3n = d_wq[:, :MLA_H * MLA_NOPE].reshape(MLA_QR, MLA_H, MLA_NOPE)
    dq3r = d_wq[:, MLA_H * MLA_NOPE:].reshape(MLA_QR, MLA_H, LANES)[:, :, :MLA_ROPE]
    d_q_up = jnp.concatenate([dq3n, dq3r], axis=2).reshape(MLA_QR, -1)
    dkv3 = d_wkv.reshape(MLA_KVR, 2, MLA_H, MLA_NOPE)
    d_kv_up = jnp.concatenate([dkv3[:, 0], dkv3[:, 1]], axis=2).reshape(MLA_KVR, -1)
    grads = dict(in_w=d_in_w, q_norm_w=dqnw[0], q_up_w=d_q_up, kv_norm_w=dkvnw[0], kv_up_w=d_kv_up, out_w=d_out_w,
                 ln_g=dg[0], ln_b=db[0])
    return dr, du, grads, rode


GDN_REP = GDN_HV // GDN_HK
GDN_A_LANE = GDN_HV
GDN_HPB = 4
GDN_VPB = GDN_HPB * GDN_REP


def _bdg(a, b, ca, cb):
    return lax.dot_general(a, b, (((ca,), (cb,)), ((0,), (0,))), preferred_element_type=F32)


@jax.custom_vjp
def _bnn(a, b):
    return _bdg(_mx(a), _mx(b), 2, 1)


def _bnn_f(a, b):
    return _bnn(a, b), (a, b)


def _bnn_b(res, ct):
    a, b = res
    return _bdg(_mx(ct), _mx(b), 2, 2), _bdg(_mx(a), _mx(ct), 1, 1)


_bnn.defvjp(_bnn_f, _bnn_b)


@jax.custom_vjp
def _bnt(a, b):
    return _bdg(_mx(a), _mx(b), 2, 2)


def _bnt_f(a, b):
    return _bnt(a, b), (a, b)


def _bnt_b(res, ct):
    a, b = res
    return _bdg(_mx(ct), _mx(b), 2, 1), _bdg(_mx(ct), _mx(a), 1, 1)


_bnt.defvjp(_bnt_f, _bnt_b)


@jax.custom_vjp
def _btn(a, b):
    return _bdg(_mx(a), _mx(b), 1, 1)


def _btn_f(a, b):
    return _btn(a, b), (a, b)


def _btn_b(res, ct):
    a, b = res
    return _bdg(_mx(b), _mx(ct), 2, 2), _bdg(_mx(a), _mx(ct), 2, 1)


_btn.defvjp(_btn_f, _btn_b)


def _h3(a, b, ca=2, cb=1):
    ah, bh = _mx(a), _mx(b)
    al, bl = _mx(a - ah.astype(F32)), _mx(b - bh.astype(F32))
    return _bdg(ah, bh, ca, cb) + (_bdg(ah, bl, ca, cb) + _bdg(al, bh, ca, cb))


@jax.custom_vjp
def _neumann_inverse(x):
    L = x.shape[-1]
    eye = (lax.broadcasted_iota(jnp.int32, (L, L), 0) == lax.broadcasted_iota(jnp.int32, (L, L), 1)).astype(F32)
    inv = eye + x
    xp = x
    for _ in range(L.bit_length() - 2):
        xp = _h3(xp, xp)
        inv = inv + _h3(inv, xp)
    return inv


def _neumann_f(x):
    inv = _neumann_inverse(x)
    return inv, inv


def _neumann_b(inv, ct):
    return (_h3(_h3(inv, ct, 1, 1), inv, 2, 2),)


_neumann_inverse.defvjp(_neumann_f, _neumann_b)


@jax.custom_vjp
def _saved_inverse(x, inv):
    return inv


def _saved_f(x, inv):
    return inv, inv


def _saved_b(inv, ct):
    return _neumann_b(inv, ct)[0], jnp.zeros_like(inv)


_saved_inverse.defvjp(_saved_f, _saved_b)


def _cat0(parts):
    return jnp.concatenate([p[None] for p in parts], axis=0)


def _gdn_chunk(hb, q, k, v, z, ba, alog, dtb, nw, s, inv_saved=None):
    L = q.shape[1]
    r_i = lax.broadcasted_iota(jnp.int32, (L, L), 0)
    c_i = lax.broadcasted_iota(jnp.int32, (L, L), 1)
    incl, strict = r_i >= c_i, r_i > c_i
    rep = lambda t: jnp.broadcast_to(t[:, None], (GDN_HPB, GDN_REP) + t.shape[1:]).reshape((GDN_VPB,) + t.shape[1:])
    qn = rep(q * lax.rsqrt(jnp.sum(q * q, -1, keepdims=True) + RMS_EPS) * (GDN_DK ** -0.5))
    kn = rep(k * lax.rsqrt(jnp.sum(k * k, -1, keepdims=True) + RMS_EPS))
    beta_all = jax.nn.sigmoid(ba)
    g_all = -jnp.exp(alog) * _softplus(ba + dtb)
    gcs_all = _sel_l(incl.astype(F32), g_all)
    lane = lax.broadcasted_iota(jnp.int32, (L, LANES), 1)
    pick = lambda mat, idx: jnp.sum(jnp.where(lane == idx, mat, 0.0), axis=1, keepdims=True)
    beta = _cat0([pick(beta_all, GDN_VPB * hb + b) for b in range(GDN_VPB)])
    gc = _cat0([pick(gcs_all, GDN_A_LANE + GDN_VPB * hb + b) for b in range(GDN_VPB)])
    gm = jnp.broadcast_to(gc, (GDN_VPB, L, L))
    decay = jnp.exp(jnp.where(incl, gm - jnp.swapaxes(gm, 1, 2), -jnp.inf))
    kb = kn * beta
    eg = jnp.exp(gc)
    x = -jnp.where(strict, _bnt(kb, kn) * decay, 0.0)
    inv = _neumann_inverse(x) if inv_saved is None else _saved_inverse(x, inv_saved)
    uw = _bnn(inv, jnp.concatenate([v * beta, kb * eg], axis=2))
    uu, ww = uw[:, :, :GDN_DV], uw[:, :, GDN_DV:]
    qk = jnp.where(incl, _bnt(qn, kn) * decay, 0.0)
    last = lax.broadcasted_iota(jnp.int32, (L, 1), 0) == L - 1
    glast = jnp.sum(jnp.where(last, gc, 0.0), axis=1, keepdims=True)
    kdec = kn * jnp.exp(glast - gc)
    vnew = uu - _bnn(ww, s)
    o = _bnn(qn * eg, s) + _bnn(qk, vnew)
    new = s * jnp.exp(glast) + _btn(kdec, vnew)
    on = o * lax.rsqrt(jnp.mean(o * o, -1, keepdims=True) + RMS_EPS) * nw * _silu(z)
    return (on, new, inv) if inv_saved is None else (on, new)


def _heads_of(ref, n):
    return _cat0([ref[:, i * LANES:(i + 1) * LANES] for i in range(n)])


def _gdn_specs(nc, rev):
    cc = (lambda c: nc - 1 - c) if rev else (lambda c: c)
    wq, wv = GDN_HPB * GDN_DK, GDN_VPB * GDN_DV
    par = pl.BlockSpec((1, LANES), lambda c, h: (0, 0))
    return dict(
        q=pl.BlockSpec((GDN_L, wq), lambda c, h: (cc(c), h)),
        k=pl.BlockSpec((GDN_L, wq), lambda c, h: (cc(c), GDN_KEY // wq + h)),
        v=pl.BlockSpec((GDN_L, wv), lambda c, h: (cc(c), 2 * GDN_KEY // wv + h)),
        z=pl.BlockSpec((GDN_L, wv), lambda c, h: (cc(c), GDN_CONV_DIM // wv + h)),
        ba=pl.BlockSpec((GDN_L, LANES), lambda c, h: (cc(c), (GDN_CONV_DIM + GDN_VAL) // LANES)),
        par=par,
        st=pl.BlockSpec((1, GDN_VPB, GDN_DK, GDN_DV), lambda c, h: (cc(c), h, 0, 0)),
        inv=pl.BlockSpec((1, GDN_VPB, GDN_L, GDN_L), lambda c, h: (cc(c), h, 0, 0)),
        o=pl.BlockSpec((GDN_L, wv), lambda c, h: (cc(c), h)),
        qk_out=pl.BlockSpec((GDN_L, wq), lambda c, h: (cc(c), h)),
        ba_out=pl.BlockSpec((GDN_L, LANES), lambda c, h: (cc(c), 0)),
    )


def _gdn_fwd(proj, act, alog, dtb, nw, name, ride=None):
    t = proj.shape[0]
    nc = t // GDN_L
    sp = _gdn_specs(nc, False)

    def body(q_ref, k_ref, v_ref, z_ref, ba_ref, alog_ref, dtb_ref, nw_ref, o_ref, st_ref, inv_ref, state):
        c, h = pl.program_id(0), pl.program_id(1)

        mine = pl.ds(h * GDN_VPB, GDN_VPB)

        @pl.when(c == 0)
        def _():
            state[mine] = jnp.zeros((GDN_VPB, GDN_DK, GDN_DV), F32)

        prev = state[mine]
        st_ref[0] = prev
        on, new, inv = _gdn_chunk(h, _heads_of(q_ref, GDN_HPB), _heads_of(k_ref, GDN_HPB), _heads_of(v_ref, GDN_VPB),
                                  _heads_of(z_ref, GDN_VPB), ba_ref[...], alog_ref[...], dtb_ref[...], nw_ref[...], prev)
        inv_ref[0] = inv
        for b in range(GDN_VPB):
            o_ref[:, b * LANES:(b + 1) * LANES] = on[b].astype(o_ref.dtype)
        state[mine] = new

    sds = jax.ShapeDtypeStruct
    return _hosted_call(
        body, name=name, grid=(nc, GDN_HK // GDN_HPB),
        in_specs=[sp["q"], sp["k"], sp["v"], sp["z"], sp["ba"], sp["par"], sp["par"], sp["par"]],
        out_specs=[sp["o"], sp["st"], sp["inv"]],
        out_shape=[sds((t, GDN_VAL), MXU), sds((nc, GDN_HV, GDN_DK, GDN_DV), F32), sds((nc, GDN_HV, GDN_L, GDN_L), F32)],
        scratch_shapes=[pltpu.VMEM((GDN_HV, GDN_DK, GDN_DV), F32)],
        sem=("arbitrary", "arbitrary"), args=(act, act, act, proj, proj, alog, dtb, nw), ride=ride)


def _gdn_bwd(proj, act, alog, dtb, nw, states, invs, don, name, ride=None):
    t = proj.shape[0]
    nc = t // GDN_L
    sp = _gdn_specs(nc, True)

    def body(q_ref, k_ref, v_ref, z_ref, ba_ref, alog_ref, dtb_ref, nw_ref, st_ref, do_ref, inv_ref,
             dact_ref, dz_ref, dba_ref, dalog_ref, ddtb_ref, dnw_ref, dstate):
        c, h = pl.program_id(0), pl.program_id(1)

        mine = pl.ds(h * GDN_VPB, GDN_VPB)

        @pl.when(c == 0)
        def _():
            dstate[mine] = jnp.zeros((GDN_VPB, GDN_DK, GDN_DV), F32)

        @pl.when((c == 0) & (h == 0))
        def _():
            dalog_ref[...] = jnp.zeros_like(dalog_ref)
            ddtb_ref[...] = jnp.zeros_like(ddtb_ref)
            dnw_ref[...] = jnp.zeros_like(dnw_ref)

        @pl.when(h == 0)
        def _():
            dba_ref[...] = jnp.zeros_like(dba_ref)

        _, vjp = jax.vjp(functools.partial(_gdn_chunk, h, inv_saved=inv_ref[0]), _heads_of(q_ref, GDN_HPB),
                         _heads_of(k_ref, GDN_HPB), _heads_of(v_ref, GDN_VPB), _heads_of(z_ref, GDN_VPB), ba_ref[...],
                         alog_ref[...], dtb_ref[...], nw_ref[...], st_ref[0])
        dq, dk, dv, dz, dba, dalog, ddtb, dnw, dprev = vjp((_heads_of(do_ref, GDN_VPB), dstate[mine]))
        for b in range(GDN_VPB):
            dz_ref[:, b * LANES:(b + 1) * LANES] = dz[b]
        for hi in range(GDN_HK // GDN_HPB):
            @pl.when(h == hi)
            def _(hi=hi):
                for i in range(GDN_HPB):
                    c0 = (hi * GDN_HPB + i) * GDN_DK
                    dact_ref[:, c0:c0 + GDN_DK] = dq[i]
                    dact_ref[:, GDN_KEY + c0:GDN_KEY + c0 + GDN_DK] = dk[i]
                for b in range(GDN_VPB):
                    c0 = 2 * GDN_KEY + (hi * GDN_VPB + b) * GDN_DV
                    dact_ref[:, c0:c0 + GDN_DV] = dv[b]
        dba_ref[...] += dba
        dalog_ref[...] += dalog
        ddtb_ref[...] += ddtb
        dnw_ref[...] += dnw
        dstate[mine] = dprev

    sds = jax.ShapeDtypeStruct
    par_out = pl.BlockSpec((1, LANES), lambda c, h: (0, 0))
    return _hosted_call(
        body, name=name, grid=(nc, GDN_HK // GDN_HPB),
        in_specs=[sp["q"], sp["k"], sp["v"], sp["z"], sp["ba"], sp["par"], sp["par"], sp["par"], sp["st"], sp["o"],
                  sp["inv"]],
        out_specs=[pl.BlockSpec((GDN_L, GDN_CONV_DIM), lambda c, h: (nc - 1 - c, 0)), sp["o"], sp["ba_out"],
                   par_out, par_out, par_out],
        out_shape=[sds((t, GDN_CONV_DIM), F32), sds((t, GDN_VAL), F32), sds((t, LANES), F32), sds((1, LANES), F32),
                   sds((1, LANES), F32), sds((1, LANES), F32)],
        scratch_shapes=[pltpu.VMEM((GDN_HV, GDN_DK, GDN_DV), F32)],
        sem=("arbitrary", "arbitrary"), args=(act, act, act, proj, proj, alog, dtb, nw, states, don, invs), ride=ride)


GDN_PROJ = GDN_CONV_DIM + GDN_VAL + 2 * GDN_HV


def _gdn_layer_fwd(u, w, ln_g, ln_b, tag, rides=None, late=None):
    w_in = jnp.concatenate([w["in_w"], jnp.zeros((D_MODEL, GDN_PROJ_PAD - GDN_PROJ), w["in_w"].dtype)], axis=1)
    alog = _pad_lanes(w["a_log"], offset=GDN_A_LANE)
    dtb = _pad_lanes(w["dt_bias"], offset=GDN_A_LANE)
    nw = w["norm_w"].reshape(1, GDN_DV)
    zb = jnp.zeros((1, GDN_CONV_DIM), F32)
    if late is None:
        proj = _mm(u.astype(MXU), w_in, name=tag + "_in")
    else:
        proj, got = _mm(u.astype(MXU), w_in, name=tag + "_in", ride=(late[0], False))
        w.update(late[1](got))
    act, _ = _conv_fwd(proj, 0, w["conv_w"], zb, name=tag + "_conv")
    rode = {}
    (on, states, invs), rode["delta"] = _gdn_fwd(proj, act, alog, dtb, nw, name=tag + "_delta",
                                                 ride=_ride_of(rides, "delta", False))
    y = _mm(on, w["out_w"], name=tag + "_out")
    h = _ln_fwd(u, y, ln_g, ln_b, name=tag + "_ln")
    saved = dict(u=u, w_in=w_in, proj=proj, act=act, states=states, invs=invs, on=on, y=y, alog=alog, dtb=dtb, nw=nw, zb=zb)
    return h, saved, rode


def _gdn_layer_bwd(s, w, ln_g, dr_up, du_up, tag, rides=None, own=None):
    t = s["u"].shape[0]
    dr, dg, db = _ln_bwd(s["u"], s["y"], ln_g, dr_up, du_up, name=tag + "_ln_b")
    don = _mm(dr, w["out_w"], tb=True, name=tag + "_out_bx")
    d_out_w = _mm(s["on"], dr, ta=True, out_dtype=WIRE, name=tag + "_out_bw")
    carried = list((rides or {}).get("delta_b", [])) + ([own("out_w", d_out_w)] if own else [])
    rode = {}
    (dact, dz, dba, dalog, ddtb, dnw), rode["delta_b"] = _gdn_bwd(
        s["proj"], s["act"], s["alog"], s["dtb"], s["nw"], s["states"], s["invs"], don, name=tag + "_delta_b",
        ride=(carried, True) if carried else None)
    dqkv, d_conv_w, _ = _conv_bwd(s["proj"], 0, w["conv_w"], s["zb"], dact, name=tag + "_conv_b")
    dproj = jnp.concatenate([dqkv, dz, dba, jnp.zeros((t, GDN_PROJ_PAD - GDN_PROJ - (LANES - 2 * GDN_HV)), F32)],
                            axis=1).astype(MXU)
    d_in = _mm(s["u"].astype(MXU), dproj, ta=True, out_dtype=WIRE, name=tag + "_in_bw")
    du = _mm(dproj, s["w_in"], tb=True, name=tag + "_in_bx")
    grads = dict(in_w=d_in[:, :GDN_PROJ], conv_w=d_conv_w, a_log=dalog[0, GDN_A_LANE:GDN_A_LANE + GDN_HV],
                 dt_bias=ddtb[0, GDN_A_LANE:GDN_A_LANE + GDN_HV], norm_w=dnw[0], out_w=d_out_w, ln_g=dg[0], ln_b=db[0])
    return dr, du, grads, rode


def _mesh_pos():
    return lax.axis_index("x"), lax.axis_index("y"), lax.axis_index("c")


def _peer(k, x, y, c):
    return ((1 - x) if k & 4 else x, (1 - y) if k & 2 else y, (1 - c) if k & 1 else c)


def _ride_copies(ins, outs, send, recv, loc, scatter, with_arrivals):
    n = len(ins)
    x, y, c = _mesh_pos()
    me = 4 * x + 2 * y + c
    local = [pltpu.make_async_copy(ins[i].at[me] if scatter else ins[i], outs[i].at[me], loc.at[i]) for i in range(n)]
    sends, arrivals = [], []
    for k in range(1, N_DEV):
        peer = _peer(k, x, y, c)
        pidx = 4 * peer[0] + 2 * peer[1] + peer[2]
        for i in range(n):
            src = ins[i].at[pidx] if scatter else ins[i]
            sems = dict(send_sem=send.at[i, k - 1], recv_sem=recv.at[i, k - 1], device_id=peer,
                        device_id_type=pl.DeviceIdType.MESH)
            sends.append(pltpu.make_async_remote_copy(src_ref=src, dst_ref=outs[i].at[me], **sems))
            if with_arrivals:
                arrivals.append(pltpu.make_async_remote_copy(src_ref=src, dst_ref=outs[i].at[pidx], **sems))
    return local, sends, arrivals


def _gather_copy(ins, outs, send, recv, i, k, block, to, from_input=False):
    slot = outs[i].at[4 * block[0] + 2 * block[1] + block[2]]
    return pltpu.make_async_remote_copy(src_ref=ins[i] if from_input else slot, dst_ref=slot, send_sem=send.at[i, k],
                                        recv_sem=recv.at[i, k], device_id=to, device_id_type=pl.DeviceIdType.MESH)


def _gather_places():
    x, y, c = _mesh_pos()
    return (x, y, c), (x, y, 1 - c), [(x, 1 - y), (1 - x, y), (1 - x, 1 - y)], c


def _ride_start(ins, outs, send, recv, loc, scatter, direct=True):
    if direct:
        local, sends, _ = _ride_copies(ins, outs, send, recv, loc, scatter, False)
        for cp in local + sends:
            cp.start()
        return
    me, sibling, chips, c = _gather_places()
    for i in range(len(ins)):
        pltpu.make_async_copy(ins[i], outs[i].at[4 * me[0] + 2 * me[1] + me[2]], loc.at[i]).start()
        _gather_copy(ins, outs, send, recv, i, 0, me, sibling, True).start()
        for j, ch in enumerate(chips):
            _gather_copy(ins, outs, send, recv, i, 1 + j, me, (*ch, c), True).start()


def _ride_wait(ins, outs, send, recv, loc, scatter, direct=True):
    if direct:
        local, sends, arrivals = _ride_copies(ins, outs, send, recv, loc, scatter, True)
        for cp in arrivals:
            cp.wait_recv()
        for cp in sends:
            cp.wait_send()
        for cp in local:
            cp.wait()
        return
    me, sibling, chips, c = _gather_places()
    n = len(ins)
    passed = []
    for j, ch in enumerate(chips):
        for i in range(n):
            _gather_copy(ins, outs, send, recv, i, 1 + j, (*ch, c), me).wait_recv()
            cp = _gather_copy(ins, outs, send, recv, i, 4 + j, (*ch, c), sibling)
            cp.start()
            passed.append(cp)
    for i in range(n):
        _gather_copy(ins, outs, send, recv, i, 0, sibling, me).wait_recv()
        for j, ch in enumerate(chips):
            _gather_copy(ins, outs, send, recv, i, 4 + j, (*ch, 1 - c), me).wait_recv()
    for i in range(n):
        _gather_copy(ins, outs, send, recv, i, 0, me, sibling, True).wait_send()
        for j, ch in enumerate(chips):
            _gather_copy(ins, outs, send, recv, i, 1 + j, me, (*ch, c), True).wait_send()
    for cp in passed:
        cp.wait_send()
    for i in range(n):
        pltpu.make_async_copy(ins[i], outs[i].at[4 * me[0] + 2 * me[1] + me[2]], loc.at[i]).wait()


def _ride_shapes(arrs, scatter):
    n = len(arrs)
    out_shape = [jax.ShapeDtypeStruct(a.shape if scatter else (N_DEV,) + a.shape, a.dtype) for a in arrs]
    scratch = [pltpu.SemaphoreType.DMA((n, N_DEV - 1)), pltpu.SemaphoreType.DMA((n, N_DEV - 1)), pltpu.SemaphoreType.DMA((n,))]
    return out_shape, scratch


def _exchange(arrs, scatter, name):
    n = len(arrs)
    hbm = pl.BlockSpec(memory_space=pltpu.HBM)

    def body(*refs):
        ins, outs = refs[:n], refs[n:2 * n]
        _ride_start(ins, outs, *refs[2 * n:], scatter, direct=scatter)
        _ride_wait(ins, outs, *refs[2 * n:], scatter, direct=scatter)

    out_shape, scratch = _ride_shapes(arrs, scatter)
    return pl.pallas_call(
        body, name=name, in_specs=[hbm] * n, out_specs=[hbm] * n, out_shape=out_shape, scratch_shapes=scratch,
        compiler_params=pltpu.CompilerParams(has_side_effects=True),
    )(*arrs)


def _hosted_call(body, *, name, grid, in_specs, out_specs, out_shape, scratch_shapes, sem, args, ride=None):
    if ride is None:
        return pl.pallas_call(body, name=name, grid=grid, in_specs=in_specs, out_specs=out_specs, out_shape=out_shape,
                              scratch_shapes=scratch_shapes, compiler_params=_cparams(sem))(*args), []
    arrs, scatter = ride
    n, ni, no, ns = len(arrs), len(in_specs), len(out_specs), len(scratch_shapes)
    hbm = pl.BlockSpec(memory_space=pltpu.HBM)
    r_shape, r_scratch = _ride_shapes(arrs, scatter)

    def full(*refs):
        a, ri = refs[:ni], refs[ni:ni + n]
        o, ro = refs[ni + n:ni + n + no], refs[ni + n + no:ni + 2 * n + no]
        s, rs = refs[ni + 2 * n + no:ni + 2 * n + no + ns], refs[ni + 2 * n + no + ns:]
        ids = [pl.program_id(d) for d in range(len(grid))]
        first, last = ids[0] == 0, ids[0] == grid[0] - 1
        for d in range(1, len(grid)):
            first, last = first & (ids[d] == 0), last & (ids[d] == grid[d] - 1)

        @pl.when(first)
        def _():
            _ride_start(ri, ro, *rs, scatter)

        body(*a, *o, *s)

        @pl.when(last)
        def _():
            _ride_wait(ri, ro, *rs, scatter)

    outs = pl.pallas_call(
        full, name=name, grid=grid, in_specs=list(in_specs) + [hbm] * n, out_specs=list(out_specs) + [hbm] * n,
        out_shape=list(out_shape) + r_shape, scratch_shapes=list(scratch_shapes) + r_scratch,
        compiler_params=pltpu.CompilerParams(dimension_semantics=("arbitrary",) * len(grid), vmem_limit_bytes=VMEM_LIMIT,
                                             has_side_effects=True),
    )(*args, *arrs)
    return outs[:no], list(outs[no:])


def _unshard(g, ax):
    g = jnp.moveaxis(g, 0, ax)
    sh = g.shape
    return g.reshape(sh[:ax] + (sh[ax] * sh[ax + 1],) + sh[ax + 2:])


def _to_parts(full, ax):
    sh = full.shape
    full = full.reshape(sh[:ax] + (N_DEV, sh[ax] // N_DEV) + sh[ax + 1:])
    return jnp.moveaxis(full, ax, 0)


def _row_tile(r, c):
    cap = max(8, (256 * 1024) // max(c, 1))
    best = None
    for d in range(8, min(r, cap) + 1, 8):
        if r % d == 0:
            best = d
    return r if best is None else best


def _adamw(w, m, v, parts, name):
    r, c = w.shape
    tr = _row_tile(r, c)

    def body(w_ref, m_ref, v_ref, p_ref, g_ref, d_ref, nm_ref, nv_ref):
        g = p_ref[0].astype(F32)
        for q in range(1, N_DEV):
            g = g + p_ref[q].astype(F32)
        nm = ADAM_B1 * m_ref[...] + (1.0 - ADAM_B1) * g
        nv = ADAM_B2 * v_ref[...] + (1.0 - ADAM_B2) * (g * g)
        m_hat = nm / (1.0 - ADAM_B1 ** ADAM_STEP)
        v_hat = nv / (1.0 - ADAM_B2 ** ADAM_STEP)
        g_ref[...] = g
        d_ref[...] = -ADAM_LR * (m_hat / (jnp.sqrt(v_hat) + ADAM_EPS) + ADAM_WD * w_ref[...])
        nm_ref[...] = nm
        nv_ref[...] = nv

    row = pl.BlockSpec((tr, c), lambda i: (i, 0))
    out = jax.ShapeDtypeStruct((r, c), F32)
    return pl.pallas_call(
        body, name=name, grid=(r // tr,),
        in_specs=[row, row, row, pl.BlockSpec((N_DEV, tr, c), lambda i: (0, i, 0))],
        out_specs=[row] * 4, out_shape=[out] * 4, compiler_params=_cparams(("parallel",)),
    )(w, m, v, parts)


WEIGHTS = ['ssd_in_w', 'ssd_conv_w', 'ssd_conv_b', 'ssd_dt_bias', 'ssd_a_log', 'ssd_d', 'ssd_norm_w', 'ssd_out_w',
           'mla_in_w', 'mla_q_norm_w', 'mla_q_up_w', 'mla_kv_norm_w', 'mla_kv_up_w', 'mla_out_w', 'gdn_in_w',
           'gdn_conv_w', 'gdn_a_log', 'gdn_dt_bias', 'gdn_norm_w', 'gdn_out_w', 'ln_g', 'ln_b']
SHARDED = {'ssd_in_w': (1, True), 'ssd_conv_w': (1, False), 'ssd_conv_b': (0, False), 'ssd_norm_w': (0, False),
           'ssd_out_w': (0, True), 'mla_in_w': (1, True), 'mla_q_up_w': (1, True), 'mla_kv_up_w': (1, True),
           'mla_out_w': (0, True), 'gdn_in_w': (1, True), 'gdn_conv_w': (1, False), 'gdn_out_w': (0, True)}
REPLICATED = [n for n in WEIGHTS if n not in SHARDED]


def _pack_small(vals):
    flat = jnp.concatenate([vals[n].reshape(-1).astype(F32) for n in REPLICATED])
    rows = -(-flat.shape[0] // (8 * LANES)) * 8
    return jnp.pad(flat, (0, rows * LANES - flat.shape[0])).reshape(rows, LANES)


def _unpack_small(slab, like):
    flat = slab.reshape(-1)
    out, off = {}, 0
    for n in REPLICATED:
        sz = like[n].size
        out[n] = flat[off:off + sz].reshape(like[n].shape)
        off += sz
    return out


def kernel(x, positions, ssd_in_w, ssd_conv_w, ssd_conv_b, ssd_dt_bias, ssd_a_log, ssd_d, ssd_norm_w, ssd_out_w, mla_in_w, mla_q_norm_w, mla_q_up_w, mla_kv_norm_w, mla_kv_up_w, mla_out_w, gdn_in_w, gdn_conv_w, gdn_a_log, gdn_dt_bias, gdn_norm_w, gdn_out_w, ln_g, ln_b, loss_target, m_ssd_in_w, m_ssd_conv_w, m_ssd_conv_b, m_ssd_dt_bias, m_ssd_a_log, m_ssd_d, m_ssd_norm_w, m_ssd_out_w, m_mla_in_w, m_mla_q_norm_w, m_mla_q_up_w, m_mla_kv_norm_w, m_mla_kv_up_w, m_mla_out_w, m_gdn_in_w, m_gdn_conv_w, m_gdn_a_log, m_gdn_dt_bias, m_gdn_norm_w, m_gdn_out_w, m_ln_g, m_ln_b, v_ssd_in_w, v_ssd_conv_w, v_ssd_conv_b, v_ssd_dt_bias, v_ssd_a_log, v_ssd_d, v_ssd_norm_w, v_ssd_out_w, v_mla_in_w, v_mla_q_norm_w, v_mla_q_up_w, v_mla_kv_norm_w, v_mla_kv_up_w, v_mla_out_w, v_gdn_in_w, v_gdn_conv_w, v_gdn_a_log, v_gdn_dt_bias, v_gdn_norm_w, v_gdn_out_w, v_ln_g, v_ln_b):
    loc = locals()
    w = {n: loc[n] for n in WEIGHTS}
    m = {n: loc["m_" + n] for n in WEIGHTS}
    v = {n: loc["v_" + n] for n in WEIGHTS}
    xs, pos, tgt = x[0], positions[0], loss_target[0]

    def names_of(prefix):
        return [n for n in SHARDED if n.startswith(prefix + "_")]

    def shard(n, j):
        return w[n][j].astype(WIRE) if SHARDED[n][1] else w[n][j]

    def full(n, gathered):
        return _unshard(gathered, SHARDED[n][0])

    def slots(n, g):
        return _to_parts(g, SHARDED[n][0]).astype(WIRE if SHARDED[n][1] else F32)

    def replicated(prefix, j):
        return {n[len(prefix) + 1:]: w[n][j] for n in REPLICATED if n.startswith(prefix + "_")}

    def late_weights(prefix, j, keys):
        return [shard(prefix + "_" + k, j) for k in keys], lambda got: {k: full(prefix + "_" + k, g) for k, g in zip(keys, got)}

    lg = lambda i: w["ln_g"][i].reshape(1, D_MODEL)
    lb = lambda i: w["ln_b"][i].reshape(1, D_MODEL)
    cosf, sinf = _rope_tables(pos)
    ssd_late = ("conv_w", "conv_b", "norm_w", "out_w")

    w_s0 = dict(replicated("ssd", 0), in_w=full("ssd_in_w", _exchange([shard("ssd_in_w", 0)], False, name="gather_ssd0")[0]))
    h1, s0, got = _ssd_layer_fwd(
        xs, w_s0, lg(0), lb(0), "ssd0", late=late_weights("ssd", 0, ssd_late),
        rides={"conv": [shard("mla_q_up_w", 0), shard("mla_kv_up_w", 0)], "scan": [shard("mla_in_w", 0)],
               "out": [shard("mla_out_w", 0)]})
    w_m0 = dict(replicated("mla", 0), q_up_w=full("mla_q_up_w", got["conv"][0]), kv_up_w=full("mla_kv_up_w", got["conv"][1]),
                in_w=full("mla_in_w", got["scan"][0]), out_w=full("mla_out_w", got["out"][0]))
    h2, s1, got = _mla_layer_fwd(h1, w_m0, lg(1), lb(1), cosf, sinf, "mla0", rides={"attn": [shard("gdn_in_w", 0)]})
    w_g0 = dict(replicated("gdn", 0), in_w=full("gdn_in_w", got["attn"][0]))
    h3, s2, got = _gdn_layer_fwd(h2, w_g0, lg(2), lb(2), "gdn0", late=late_weights("gdn", 0, ("conv_w", "out_w")),
                                 rides={"delta": [shard("ssd_in_w", 1)]})
    w_s1 = dict(replicated("ssd", 1), in_w=full("ssd_in_w", got["delta"][0]))
    h4, s3, _ = _ssd_layer_fwd(h3, w_s1, lg(3), lb(3), "ssd1", late=late_weights("ssd", 1, ssd_late))
    loss_tile, dl = _loss_head(h4, tgt, name="loss_head")

    dr3, du3, g3, _, _ = _ssd_layer_bwd(s3, w_s1, lg(3), jnp.zeros_like(dl), dl, "ssd1")
    dr2, du2, g2, got = _gdn_layer_bwd(s2, w_g0, lg(2), dr3, du3, "gdn0", own=lambda k, g: slots("gdn_" + k, g),
                                       rides={"delta_b": [slots(n, g3[n[4:]]) for n in names_of("ssd")]})
    r3, r2_out = got["delta_b"][:-1], got["delta_b"][-1]
    gin = slots("gdn_in_w", g2["in_w"])
    half = gin.shape[1] // 2
    dr1, du1, g1, got = _mla_layer_bwd(
        s1, w_m0, lg(1), cosf, sinf, dr2, du2, "mla0",
        rides={"attn_bq": [gin[:, :half]], "attn_bkv": [gin[:, half:], slots("gdn_conv_w", g2["conv_w"])]})
    r2 = {"gdn_in_w": jnp.concatenate([got["attn_bq"][0], got["attn_bkv"][0]], axis=1), "gdn_conv_w": got["attn_bkv"][1],
          "gdn_out_w": r2_out}
    dr0, du0, g0, got, r0 = _ssd_layer_bwd(s0, w_s0, lg(0), dr1, du1, "ssd0", own=lambda k, g: slots("ssd_" + k, g),
                                           rides={"scan_b": [slots(n, g1[n[4:]]) for n in names_of("mla")]})
    r1 = got["scan_b"]
    grad_x = _axpy(dr0, du0, name="grad_x")[None]

    gsmall = {"ssd_" + k: jnp.stack([g0[k], g3[k]]) for k in ("dt_bias", "a_log", "d")}
    gsmall.update({"mla_" + k: g1[k][None] for k in ("q_norm_w", "kv_norm_w")})
    gsmall.update({"gdn_" + k: g2[k][None] for k in ("a_log", "dt_bias", "norm_w")})
    gsmall["ln_g"] = jnp.stack([g0["ln_g"], g1["ln_g"], g2["ln_g"], g3["ln_g"]])
    gsmall["ln_b"] = jnp.stack([g0["ln_b"], g1["ln_b"], g2["ln_b"], g3["ln_b"]])
    small = _pack_small(gsmall)
    rsmall, = _exchange([jnp.broadcast_to(small[None], (N_DEV,) + small.shape)], True, name="gather_small_grads")

    recvd = {n: jnp.stack([r0[n[4:]], b], axis=1) for n, b in zip(names_of("ssd"), r3)}
    recvd.update({n: a[:, None] for n, a in zip(names_of("mla"), r1)})
    recvd.update({n: a[:, None] for n, a in r2.items()})
    recvd = [recvd[n] for n in SHARDED] + [rsmall]

    grads, deltas, new_m, new_v = {}, {}, {}, {}
    for n, pt in zip(SHARDED, recvd[:-1]):
        shp = w[n].shape
        r2d = (-1, shp[-1])
        outs = _adamw(w[n].reshape(r2d), m[n].reshape(r2d), v[n].reshape(r2d), pt.reshape((N_DEV,) + w[n].reshape(r2d).shape),
                      name="adamw_" + n)
        grads[n], deltas[n], new_m[n], new_v[n] = (o.reshape(shp) for o in outs)
    outs = _adamw(_pack_small(w), _pack_small(m), _pack_small(v), recvd[-1], name="adamw_replicated")
    for dst, o in zip((grads, deltas, new_m, new_v), outs):
        dst.update(_unpack_small(o, w))

    loss = lax.psum(loss_tile[0, 0], ("x", "y", "c"))
    return (loss, grad_x, *[grads[n] for n in WEIGHTS], *[deltas[n] for n in WEIGHTS],
            *[new_m[n] for n in WEIGHTS], *[new_v[n] for n in WEIGHTS])
```

```python
import functools

import jax
import jax.numpy as jnp
from jax import lax
from jax.experimental import pallas as pl
from jax.experimental.pallas import tpu as pltpu

F32 = jnp.float32
MXU = jnp.bfloat16
WIRE = jnp.bfloat16
HI = lax.Precision.HIGHEST

N_DEV = 8
LANES = 128
VMEM_LIMIT = 56 * 1024 * 1024

D_MODEL = 2048
DEPTH = 4
ALPHA = (2.0 * DEPTH) ** 0.25
LN_EPS = 1e-5
RMS_EPS = 1e-6

SSD_DI = 4096
SSD_P = 64
SSD_H = 64
SSD_G = 8
SSD_N = 128
SSD_L = 128
SSD_GS = SSD_DI // SSD_G
SSD_CONV_DIM = SSD_DI + 2 * SSD_G * SSD_N
SSD_PROJ_PAD = SSD_DI + SSD_CONV_DIM + LANES

MLA_H = 16
MLA_QR = 768
MLA_KVR = 512
MLA_NOPE = 128
MLA_ROPE = 64
MLA_V = 128
MLA_GATE = MLA_H * MLA_V
MLA_PROJ_PAD = MLA_QR + MLA_KVR + LANES + MLA_GATE
MLA_SCALE = (MLA_NOPE + MLA_ROPE) ** -0.5
ROPE_THETA = 10000.0
ATT_BLK = 512

GDN_HK = 16
GDN_HV = 32
GDN_DK = 128
GDN_DV = 128
GDN_KEY = GDN_HK * GDN_DK
GDN_VAL = GDN_HV * GDN_DV
GDN_L = 128
GDN_CONV_DIM = 2 * GDN_KEY + GDN_VAL
GDN_PROJ_PAD = 12800

ADAM_LR = 0.001
ADAM_B1 = 0.9
ADAM_B2 = 0.999
ADAM_EPS = 1e-08
ADAM_WD = 0.01
ADAM_STEP = 10


def _cparams(sem=None):
    return pltpu.CompilerParams(dimension_semantics=sem, vmem_limit_bytes=VMEM_LIMIT)


def _tile(n, cap):
    if n <= cap:
        return n
    best = None
    for d in range(LANES, cap + 1, LANES):
        if n % d == 0:
            best = d
    assert best is not None, (n, cap)
    return best


def _dg(a, b, ca, cb, prec=None):
    return lax.dot_general(a, b, (((ca,), (cb,)), ((), ())), preferred_element_type=F32, precision=prec)


def _mx(a):
    return a.astype(MXU)


@jax.custom_vjp
def _nn(a, b):
    return _dg(_mx(a), _mx(b), 1, 0)


def _nn_f(a, b):
    return _nn(a, b), (a, b)


def _nn_b(res, ct):
    a, b = res
    return _dg(_mx(ct), _mx(b), 1, 1), _dg(_mx(a), _mx(ct), 0, 0)


_nn.defvjp(_nn_f, _nn_b)


@jax.custom_vjp
def _nt(a, b):
    return _dg(_mx(a), _mx(b), 1, 1)


def _nt_f(a, b):
    return _nt(a, b), (a, b)


def _nt_b(res, ct):
    a, b = res
    return _dg(_mx(ct), _mx(b), 1, 0), _dg(_mx(ct), _mx(a), 0, 0)


_nt.defvjp(_nt_f, _nt_b)


@jax.custom_vjp
def _tn(a, b):
    return _dg(_mx(a), _mx(b), 0, 0)


def _tn_f(a, b):
    return _tn(a, b), (a, b)


def _tn_b(res, ct):
    a, b = res
    return _dg(_mx(b), _mx(ct), 1, 1), _dg(_mx(a), _mx(ct), 1, 0)


_tn.defvjp(_tn_f, _tn_b)


def _softplus(x):
    return jnp.maximum(x, 0.0) + jnp.log(1.0 + jnp.exp(-jnp.abs(x)))


def _silu(x):
    return x * jax.nn.sigmoid(x)


MM_TM = 1024
MM_TN = 1280
MM_VMEM_BUDGET = 40 * 1024 * 1024


def _mm(a, b, *, ta=False, tb=False, out_dtype=F32, name, ride=None):
    if ta:
        kdim, m = a.shape
    else:
        m, kdim = a.shape
    if tb:
        n, kb = b.shape
    else:
        kb, n = b.shape
    assert kdim == kb, (a.shape, b.shape, ta, tb)
    tm, tn = _tile(m, MM_TM), _tile(n, MM_TN)
    abytes, bbytes, obytes = a.dtype.itemsize, b.dtype.itemsize, jnp.dtype(out_dtype).itemsize
    tk = LANES
    for d in range(LANES, kdim + 1, LANES):
        if kdim % d == 0 and 2 * d * (tm * abytes + tn * bbytes) + tm * tn * (2 * obytes + 4) <= MM_VMEM_BUDGET:
            tk = d
    nk = kdim // tk
    ca, cb = (0 if ta else 1), (1 if tb else 0)

    def body(a_ref, b_ref, o_ref, *acc):
        part = _dg(_mx(a_ref[...]), _mx(b_ref[...]), ca, cb)
        if nk == 1:
            o_ref[...] = part.astype(out_dtype)
            return
        acc_ref, = acc
        k = pl.program_id(2)

        @pl.when(k == 0)
        def _():
            acc_ref[...] = part

        @pl.when(k > 0)
        def _():
            acc_ref[...] += part

        @pl.when(k == nk - 1)
        def _():
            o_ref[...] = acc_ref[...].astype(out_dtype)

    a_spec = pl.BlockSpec((tk, tm), lambda i, j, k: (k, i)) if ta else pl.BlockSpec((tm, tk), lambda i, j, k: (i, k))
    b_spec = pl.BlockSpec((tn, tk), lambda i, j, k: (j, k)) if tb else pl.BlockSpec((tk, tn), lambda i, j, k: (k, j))
    (out,), rode = _hosted_call(
        body, name=name, grid=(m // tm, n // tn, nk),
        in_specs=[a_spec, b_spec], out_specs=[pl.BlockSpec((tm, tn), lambda i, j, k: (i, j))],
        out_shape=[jax.ShapeDtypeStruct((m, n), out_dtype)],
        scratch_shapes=[pltpu.VMEM((tm, tn), F32)] if nk > 1 else [],
        sem=("parallel", "parallel", "arbitrary"), args=(a, b), ride=ride)
    return out if ride is None else (out, rode)


def _ln_fwd(h, y, g, b, name):
    t, d = h.shape
    tr = _tile(t, 256)

    def body(h_ref, y_ref, g_ref, b_ref, o_ref):
        r = ALPHA * h_ref[...] + y_ref[...]
        mu = jnp.mean(r, -1, keepdims=True)
        xc = r - mu
        var = jnp.mean(xc * xc, -1, keepdims=True)
        o_ref[...] = xc * lax.rsqrt(var + LN_EPS) * g_ref[...] + b_ref[...]

    row = pl.BlockSpec((tr, d), lambda i: (i, 0))
    par = pl.BlockSpec((1, d), lambda i: (0, 0))
    return pl.pallas_call(
        body, name=name, grid=(t // tr,), in_specs=[row, row, par, par], out_specs=row,
        out_shape=jax.ShapeDtypeStruct((t, d), F32), compiler_params=_cparams(("parallel",)),
    )(h, y, g, b)


def _ln_bwd(h, y, g, dr_up, du_up, name):
    t, d = h.shape
    tr = _tile(t, 256)

    def body(h_ref, y_ref, g_ref, dr_ref, du_ref, o_ref, dg_ref, db_ref):
        i = pl.program_id(0)

        @pl.when(i == 0)
        def _():
            dg_ref[...] = jnp.zeros_like(dg_ref)
            db_ref[...] = jnp.zeros_like(db_ref)

        dout = ALPHA * dr_ref[...] + du_ref[...]
        r = ALPHA * h_ref[...] + y_ref[...]
        mu = jnp.mean(r, -1, keepdims=True)
        xc = r - mu
        rstd = lax.rsqrt(jnp.mean(xc * xc, -1, keepdims=True) + LN_EPS)
        xh = xc * rstd
        dxh = dout * g_ref[...]
        o_ref[...] = rstd * (dxh - jnp.mean(dxh, -1, keepdims=True) - xh * jnp.mean(dxh * xh, -1, keepdims=True))
        dg_ref[...] += jnp.sum(dout * xh, 0, keepdims=True)
        db_ref[...] += jnp.sum(dout, 0, keepdims=True)

    row = pl.BlockSpec((tr, d), lambda i: (i, 0))
    par = pl.BlockSpec((1, d), lambda i: (0, 0))
    return pl.pallas_call(
        body, name=name, grid=(t // tr,), in_specs=[row, row, par, row, row], out_specs=[row, par, par],
        out_shape=[jax.ShapeDtypeStruct((t, d), F32), jax.ShapeDtypeStruct((1, d), F32), jax.ShapeDtypeStruct((1, d), F32)],
        compiler_params=_cparams(("arbitrary",)),
    )(h, y, g, dr_up, du_up)


def _loss_head(h, tgt, name):
    t, d = h.shape
    tr = _tile(t, 256)

    def body(h_ref, t_ref, l_ref, d_ref):
        i = pl.program_id(0)

        @pl.when(i == 0)
        def _():
            l_ref[...] = jnp.zeros_like(l_ref)

        e = h_ref[...] - t_ref[...]
        d_ref[...] = e * (1.0 / d)
        l_ref[...] += 0.5 * jnp.sum(jnp.mean(e * e, -1, keepdims=True))

    row = pl.BlockSpec((tr, d), lambda i: (i, 0))
    return pl.pallas_call(
        body, name=name, grid=(t // tr,), in_specs=[row, row],
        out_specs=[pl.BlockSpec((8, LANES), lambda i: (0, 0)), row],
        out_shape=[jax.ShapeDtypeStruct((8, LANES), F32), jax.ShapeDtypeStruct((t, d), F32)],
        compiler_params=_cparams(("arbitrary",)),
    )(h, tgt)


def _axpy(dr, du, name):
    t, d = dr.shape
    tr = _tile(t, 256)

    def body(a_ref, b_ref, o_ref):
        o_ref[...] = ALPHA * a_ref[...] + b_ref[...]

    row = pl.BlockSpec((tr, d), lambda i: (i, 0))
    return pl.pallas_call(
        body, name=name, grid=(t // tr,), in_specs=[row, row], out_specs=row,
        out_shape=jax.ShapeDtypeStruct((t, d), F32), compiler_params=_cparams(("parallel",)),
    )(dr, du)


CONV_TT = 512
CONV_TC = 512


HALO = 8


def _shift_down(cur, halo, s, row):
    if s == 0:
        return cur
    tt = cur.shape[0]
    edge = pltpu.roll(halo, s, 0)
    if tt > HALO:
        edge = jnp.concatenate([edge, jnp.zeros((tt - HALO, cur.shape[1]), cur.dtype)], axis=0)
    return jnp.where(row >= s, pltpu.roll(cur, s, 0), edge)


def _shift_up(cur, halo, s, row, tt):
    if s == 0:
        return cur
    edge = jnp.concatenate([jnp.zeros((tt - HALO, cur.shape[1]), cur.dtype), pltpu.roll(halo, HALO - s, 0)], axis=0)
    return jnp.where(row < tt - s, pltpu.roll(cur, tt - s, 0), edge)


def _conv_fwd(proj, col0, w, b, name, ride=None):
    t = proj.shape[0]
    c = w.shape[1]
    tt = _tile(t, CONV_TT)
    cb0 = col0 // CONV_TC

    def body(x_ref, p_ref, w_ref, b_ref, o_ref):
        i = pl.program_id(1)
        x = x_ref[...]
        p = jnp.where(i > 0, p_ref[...], 0.0)
        row = lax.broadcasted_iota(jnp.int32, x.shape, 0)
        pre = b_ref[...] + w_ref[3:4, :] * x
        for s in (1, 2, 3):
            pre = pre + w_ref[3 - s:4 - s, :] * _shift_down(x, p, s, row)
        o_ref[...] = _silu(pre)

    (act,), rode = _hosted_call(
        body, name=name, grid=(c // CONV_TC, t // tt),
        in_specs=[pl.BlockSpec((tt, CONV_TC), lambda j, i: (i, cb0 + j)),
                  pl.BlockSpec((HALO, CONV_TC), lambda j, i: (jnp.maximum(i * (tt // HALO) - 1, 0), cb0 + j)),
                  pl.BlockSpec((4, CONV_TC), lambda j, i: (0, j)),
                  pl.BlockSpec((1, CONV_TC), lambda j, i: (0, j))],
        out_specs=[pl.BlockSpec((tt, CONV_TC), lambda j, i: (i, j))],
        out_shape=[jax.ShapeDtypeStruct((t, c), F32)], scratch_shapes=[],
        sem=("parallel", "parallel"), args=(proj, proj, w, b), ride=ride)
    return act, rode


def _conv_bwd(proj, col0, w, b, dact, name):
    t = proj.shape[0]
    c = w.shape[1]
    tt = _tile(t, CONV_TT)
    nt = t // tt
    cb0 = col0 // CONV_TC
    per = tt // HALO

    def dpre_of(x, halo, d, w_ref, b_ref):
        row = lax.broadcasted_iota(jnp.int32, x.shape, 0)
        sh = [_shift_down(x, halo, s, row) for s in range(4)]
        pre = b_ref[...] + w_ref[3:4, :] * sh[0]
        for s in (1, 2, 3):
            pre = pre + w_ref[3 - s:4 - s, :] * sh[s]
        sg = jax.nn.sigmoid(pre)
        return d * (sg * (1.0 + pre * (1.0 - sg))), sh

    def body(x_ref, p_ref, n_ref, w_ref, b_ref, d_ref, dn_ref, dx_ref, dw_ref, db_ref):
        i = pl.program_id(1)

        @pl.when(i == 0)
        def _():
            dw_ref[...] = jnp.zeros_like(dw_ref)
            db_ref[...] = jnp.zeros_like(db_ref)

        x = x_ref[...]
        dpre, sh = dpre_of(x, jnp.where(i > 0, p_ref[...], 0.0), d_ref[...], w_ref, b_ref)
        dnext, _ = dpre_of(n_ref[...], x[tt - HALO:, :], jnp.where(i < nt - 1, dn_ref[...], 0.0), w_ref, b_ref)
        row = lax.broadcasted_iota(jnp.int32, x.shape, 0)
        acc = w_ref[3:4, :] * dpre
        for s in (1, 2, 3):
            acc = acc + w_ref[3 - s:4 - s, :] * _shift_up(dpre, dnext, s, row, tt)
        dx_ref[...] = acc
        for s in range(4):
            dw_ref[3 - s:4 - s, :] += jnp.sum(dpre * sh[s], 0, keepdims=True)
        db_ref[...] += jnp.sum(dpre, 0, keepdims=True)

    tile = lambda off: pl.BlockSpec((tt, CONV_TC), lambda j, i: (i, off + j))
    before = lambda off: pl.BlockSpec((HALO, CONV_TC), lambda j, i: (jnp.maximum(i * per - 1, 0), off + j))
    after = lambda off: pl.BlockSpec((HALO, CONV_TC), lambda j, i: (jnp.minimum((i + 1) * per, t // HALO - 1), off + j))
    wspec = lambda rows: pl.BlockSpec((rows, CONV_TC), lambda j, i: (0, j))
    return pl.pallas_call(
        body, name=name, grid=(c // CONV_TC, nt),
        in_specs=[tile(cb0), before(cb0), after(cb0), wspec(4), wspec(1), tile(0), after(0)],
        out_specs=[tile(0), wspec(4), wspec(1)],
        out_shape=[jax.ShapeDtypeStruct((t, c), F32), jax.ShapeDtypeStruct((4, c), F32), jax.ShapeDtypeStruct((1, c), F32)],
        compiler_params=_cparams(("parallel", "arbitrary")),
    )(proj, proj, proj, w, b, dact, dact)


SSD_GB = 4


def _split3(a):
    a1 = _mx(a)
    r = a - a1.astype(F32)
    a2 = _mx(r)
    return a1, a2, _mx(r - a2.astype(F32))


@jax.custom_vjp
def _sel_r(a, c):
    cm = _mx(c)
    p1, p2, p3 = _split3(a)
    return _dg(p1, cm, 1, 0) + (_dg(p2, cm, 1, 0) + _dg(p3, cm, 1, 0))


def _sel_r_f(a, c):
    return _sel_r(a, c), c


def _sel_r_b(c, ct):
    cm = _mx(c)
    p1, p2, p3 = _split3(ct)
    return _dg(p1, cm, 1, 1) + (_dg(p2, cm, 1, 1) + _dg(p3, cm, 1, 1)), jnp.zeros_like(c)


_sel_r.defvjp(_sel_r_f, _sel_r_b)


@jax.custom_vjp
def _sel_l(c, a):
    cm = _mx(c)
    p1, p2, p3 = _split3(a)
    return _dg(cm, p1, 1, 0) + (_dg(cm, p2, 1, 0) + _dg(cm, p3, 1, 0))


def _sel_l_f(c, a):
    return _sel_l(c, a), c


def _sel_l_b(c, ct):
    cm = _mx(c)
    p1, p2, p3 = _split3(ct)
    return jnp.zeros_like(c), _dg(cm, p1, 0, 0) + (_dg(cm, p2, 0, 0) + _dg(cm, p3, 0, 0))


_sel_l.defvjp(_sel_l_f, _sel_l_b)


def _ssd_chunk(gb, x, z, bm, cm, dtraw, dtb, alog, dsk, nw, prev):
    L = x.shape[1]
    r_i = lax.broadcasted_iota(jnp.int32, (L, L), 0)
    c_i = lax.broadcasted_iota(jnp.int32, (L, L), 1)
    causal = r_i >= c_i
    dt = _softplus(dtraw + dtb)
    a = dt * (-jnp.exp(alog))
    acs = _sel_l(causal.astype(F32), a)
    e_r = lax.broadcasted_iota(jnp.int32, (LANES, SSD_GS), 0)
    e_c = lax.broadcasted_iota(jnp.int32, (LANES, SSD_GS), 1)
    hpg = SSD_H // SSD_G
    sels = [(e_r == (gb * SSD_GB + i) * hpg + jnp.right_shift(e_c, 6)).astype(F32) for i in range(SSD_GB)]
    dt_x = _cat0([_sel_r(dt, s) for s in sels])
    acs_x = _cat0([_sel_r(acs, s) for s in sels])
    d_x = _cat0([_sel_r(jnp.broadcast_to(dsk, (8, LANES)), s)[0:1] for s in sels])
    last = lax.broadcasted_iota(jnp.int32, (L, 1), 0) == L - 1
    alast = jnp.sum(jnp.where(last, acs_x, 0.0), axis=1, keepdims=True)
    xdt = x * dt_x
    cb = _bnt(cm, bm)
    lane = lax.broadcasted_iota(jnp.int32, (L, LANES), 1)
    ys = []
    for j in range(SSD_GS // LANES):
        xp = xdt[:, :, j * LANES:(j + 1) * LANES]
        yp = None
        for hh in range(2):
            c0 = (2 * j + hh) * SSD_P
            cmx = jnp.broadcast_to(acs_x[:, :, c0:c0 + 1], (SSD_GB, L, L))
            dec = jnp.exp(jnp.where(causal, cmx - jnp.swapaxes(cmx, 1, 2), -jnp.inf))
            half = (lane < SSD_P) if hh == 0 else (lane >= SSD_P)
            t = _bnn(cb * dec, jnp.where(half, xp, 0.0))
            yp = t if yp is None else yp + t
        ys.append(yp)
    y_diag = jnp.concatenate(ys, axis=2)
    st = _btn(bm, xdt * jnp.exp(alast - acs_x))
    new = prev * jnp.exp(alast) + st
    y_off = _bnn(cm, prev) * jnp.exp(acs_x)
    y = y_diag + y_off + x * d_x
    yg = y * _silu(z)
    yn = yg * lax.rsqrt(jnp.mean(yg * yg, -1, keepdims=True) + RMS_EPS) * nw
    return yn, new


def _groups_of(ref, width):
    return _cat0([ref[:, i * width:(i + 1) * width] for i in range(SSD_GB)])


def _put_groups(ref, val, width):
    for i in range(SSD_GB):
        ref[:, i * width:(i + 1) * width] = val[i].astype(ref.dtype)


def _ssd_specs(nc, rev):
    cc = (lambda c: nc - 1 - c) if rev else (lambda c: c)
    wx, wb = SSD_GB * SSD_GS, SSD_GB * SSD_N
    dtb = (SSD_DI + SSD_CONV_DIM) // LANES
    bb = SSD_DI // wb
    cbk = (SSD_DI + SSD_G * SSD_N) // wb
    par = pl.BlockSpec((1, LANES), lambda c, g: (0, 0))
    return dict(
        z=pl.BlockSpec((SSD_L, wx), lambda c, g: (cc(c), g)),
        dt=pl.BlockSpec((SSD_L, LANES), lambda c, g: (cc(c), dtb)),
        x=pl.BlockSpec((SSD_L, wx), lambda c, g: (cc(c), g)),
        bm=pl.BlockSpec((SSD_L, wb), lambda c, g: (cc(c), bb + g)),
        cm=pl.BlockSpec((SSD_L, wb), lambda c, g: (cc(c), cbk + g)),
        par=par,
        nw=pl.BlockSpec((1, wx), lambda c, g: (0, g)),
        st=pl.BlockSpec((1, SSD_GB, SSD_N, SSD_GS), lambda c, g: (cc(c), g, 0, 0)),
        y=pl.BlockSpec((SSD_L, wx), lambda c, g: (cc(c), g)),
        bc=pl.BlockSpec((SSD_L, wb), lambda c, g: (cc(c), g)),
        dtout=pl.BlockSpec((SSD_L, LANES), lambda c, g: (cc(c), 0)),
    )


def _ssd_fwd(proj, act, dtb, alog, dsk, nw, name, ride=None):
    t = proj.shape[0]
    nc = t // SSD_L
    sp = _ssd_specs(nc, False)

    def body(z_ref, dt_ref, x_ref, bm_ref, cm_ref, dtb_ref, alog_ref, dsk_ref, nw_ref, y_ref, st_ref, state):
        c, g = pl.program_id(0), pl.program_id(1)
        mine = pl.ds(g * SSD_GB, SSD_GB)

        @pl.when(c == 0)
        def _():
            state[mine] = jnp.zeros((SSD_GB, SSD_N, SSD_GS), F32)

        prev = state[mine]
        st_ref[0] = prev
        yn, new = _ssd_chunk(g, _groups_of(x_ref, SSD_GS), _groups_of(z_ref, SSD_GS), _groups_of(bm_ref, SSD_N),
                             _groups_of(cm_ref, SSD_N), dt_ref[...], dtb_ref[...], alog_ref[...], dsk_ref[...],
                             _groups_of(nw_ref, SSD_GS), prev)
        _put_groups(y_ref, yn, SSD_GS)
        state[mine] = new

    return _hosted_call(
        body, name=name, grid=(nc, SSD_G // SSD_GB),
        in_specs=[sp["z"], sp["dt"], sp["x"], sp["bm"], sp["cm"], sp["par"], sp["par"], sp["par"], sp["nw"]],
        out_specs=[sp["y"], sp["st"]],
        out_shape=[jax.ShapeDtypeStruct((t, SSD_DI), MXU), jax.ShapeDtypeStruct((nc, SSD_G, SSD_N, SSD_GS), F32)],
        scratch_shapes=[pltpu.VMEM((SSD_G, SSD_N, SSD_GS), F32)],
        sem=("arbitrary", "arbitrary"), args=(proj, proj, act, act, act, dtb, alog, dsk, nw), ride=ride)


def _ssd_bwd(proj, act, dtb, alog, dsk, nw, states, dyn, name, ride=None):
    t = proj.shape[0]
    nc = t // SSD_L
    sp = _ssd_specs(nc, True)

    def body(z_ref, dt_ref, x_ref, bm_ref, cm_ref, dtb_ref, alog_ref, dsk_ref, nw_ref, st_ref, dy_ref,
             dact_ref, dz_ref, ddt_ref, ddtb_ref, dalog_ref, ddsk_ref, dnw_ref, dstate):
        c, g = pl.program_id(0), pl.program_id(1)
        mine = pl.ds(g * SSD_GB, SSD_GB)

        @pl.when(c == 0)
        def _():
            dstate[mine] = jnp.zeros((SSD_GB, SSD_N, SSD_GS), F32)

        @pl.when((c == 0) & (g == 0))
        def _():
            ddtb_ref[...] = jnp.zeros_like(ddtb_ref)
            dalog_ref[...] = jnp.zeros_like(dalog_ref)
            ddsk_ref[...] = jnp.zeros_like(ddsk_ref)
            dnw_ref[...] = jnp.zeros_like(dnw_ref)

        @pl.when(g == 0)
        def _():
            ddt_ref[...] = jnp.zeros_like(ddt_ref)

        _, vjp = jax.vjp(functools.partial(_ssd_chunk, g), _groups_of(x_ref, SSD_GS), _groups_of(z_ref, SSD_GS),
                         _groups_of(bm_ref, SSD_N), _groups_of(cm_ref, SSD_N), dt_ref[...], dtb_ref[...], alog_ref[...],
                         dsk_ref[...], _groups_of(nw_ref, SSD_GS), st_ref[0])
        dx, dz, dbm, dcm, ddt, ddtb, dalog, ddsk, dnw, dprev = vjp((_groups_of(dy_ref, SSD_GS), dstate[mine]))
        _put_groups(dz_ref, dz, SSD_GS)
        for gi in range(SSD_G // SSD_GB):
            @pl.when(g == gi)
            def _(gi=gi):
                for i in range(SSD_GB):
                    gg = gi * SSD_GB + i
                    dact_ref[:, gg * SSD_GS:(gg + 1) * SSD_GS] = dx[i]
                    dact_ref[:, SSD_DI + gg * SSD_N:SSD_DI + (gg + 1) * SSD_N] = dbm[i]
                    dact_ref[:, SSD_DI + (SSD_G + gg) * SSD_N:SSD_DI + (SSD_G + gg + 1) * SSD_N] = dcm[i]
        ddt_ref[...] += ddt
        ddtb_ref[...] += ddtb
        dalog_ref[...] += dalog
        ddsk_ref[...] += ddsk
        dnw_ref[mine] += dnw
        dstate[mine] = dprev

    par_out = pl.BlockSpec((1, LANES), lambda c, g: (0, 0))
    sds = jax.ShapeDtypeStruct
    return _hosted_call(
        body, name=name, grid=(nc, SSD_G // SSD_GB),
        in_specs=[sp["z"], sp["dt"], sp["x"], sp["bm"], sp["cm"], sp["par"], sp["par"], sp["par"], sp["nw"],
                  sp["st"], sp["y"]],
        out_specs=[pl.BlockSpec((SSD_L, SSD_CONV_DIM), lambda c, g: (nc - 1 - c, 0)), sp["y"], sp["dtout"],
                   par_out, par_out, par_out, pl.BlockSpec((SSD_G, 1, SSD_GS), lambda c, g: (0, 0, 0))],
        out_shape=[sds((t, SSD_CONV_DIM), F32), sds((t, SSD_DI), F32), sds((t, LANES), F32), sds((1, LANES), F32),
                   sds((1, LANES), F32), sds((1, LANES), F32), sds((SSD_G, 1, SSD_GS), F32)],
        scratch_shapes=[pltpu.VMEM((SSD_G, SSD_N, SSD_GS), F32)],
        sem=("arbitrary", "arbitrary"), args=(proj, proj, act, act, act, dtb, alog, dsk, nw, states, dyn), ride=ride)


def _pad_lanes(v, width=LANES, offset=0):
    return jnp.pad(v.astype(F32), (offset, width - offset - v.shape[0])).reshape(1, width)


def _ride_of(rides, key, scatter):
    arrs = (rides or {}).get(key)
    return (arrs, scatter) if arrs else None


def _mm_r(a, b, ride, **kw):
    out = _mm(a, b, ride=ride, **kw)
    return (out, []) if ride is None else out


def _ssd_layer_fwd(u, w, ln_g, ln_b, tag, rides=None, late=None):
    w_in = jnp.concatenate([w["in_w"], jnp.zeros((D_MODEL, LANES - SSD_H), w["in_w"].dtype)], axis=1)
    dtb, alog, dsk = _pad_lanes(w["dt_bias"]), _pad_lanes(w["a_log"]), _pad_lanes(w["d"])
    if late is None:
        proj = _mm(u.astype(MXU), w_in, name=tag + "_in")
    else:
        proj, got = _mm(u.astype(MXU), w_in, name=tag + "_in", ride=(late[0], False))
        w.update(late[1](got))
    nw = w["norm_w"].reshape(1, SSD_DI)
    cb = w["conv_b"].reshape(1, SSD_CONV_DIM)
    rode = {}
    act, rode["conv"] = _conv_fwd(proj, SSD_DI, w["conv_w"], cb, name=tag + "_conv", ride=_ride_of(rides, "conv", False))
    (yn, states), rode["scan"] = _ssd_fwd(proj, act, dtb, alog, dsk, nw, name=tag + "_scan", ride=_ride_of(rides, "scan", False))
    y, rode["out"] = _mm_r(yn, w["out_w"], _ride_of(rides, "out", False), name=tag + "_out")
    h = _ln_fwd(u, y, ln_g, ln_b, name=tag + "_ln")
    saved = dict(u=u, w_in=w_in, proj=proj, act=act, states=states, yn=yn, y=y, dtb=dtb, alog=alog, dsk=dsk, nw=nw, cb=cb)
    return h, saved, rode


def _ssd_layer_bwd(s, w, ln_g, dr_up, du_up, tag, rides=None, own=None):
    dr, dg, db = _ln_bwd(s["u"], s["y"], ln_g, dr_up, du_up, name=tag + "_ln_b")
    dyn = _mm(dr, w["out_w"], tb=True, name=tag + "_out_bx")
    d_out_w = _mm(s["yn"], dr, ta=True, out_dtype=WIRE, name=tag + "_out_bw")
    rode = {}
    (dact, dz, ddt, ddtb, dalog, ddsk, dnw), rode["scan_b"] = _ssd_bwd(
        s["proj"], s["act"], s["dtb"], s["alog"], s["dsk"], s["nw"], s["states"], dyn, name=tag + "_scan_b",
        ride=_ride_of(rides, "scan_b", True))
    dxbc, d_conv_w, d_conv_b = _conv_bwd(s["proj"], SSD_DI, w["conv_w"], s["cb"], dact, name=tag + "_conv_b")
    dproj = jnp.concatenate([dz, dxbc, ddt], axis=1).astype(MXU)
    grads = dict(conv_w=d_conv_w, conv_b=d_conv_b.reshape(-1), dt_bias=ddtb[0, :SSD_H], a_log=dalog[0, :SSD_H],
                 d=ddsk[0, :SSD_H], norm_w=dnw.reshape(-1), out_w=d_out_w, ln_g=dg[0], ln_b=db[0])
    late_names = ("conv_w", "conv_b", "norm_w", "out_w")
    got = {}
    if own is None:
        d_in_w = _mm(s["u"].astype(MXU), dproj, ta=True, out_dtype=WIRE, name=tag + "_in_bw")
        du = _mm(dproj, s["w_in"], tb=True, name=tag + "_in_bx")
        grads["in_w"] = d_in_w[:, :SSD_DI + SSD_CONV_DIM + SSD_H]
    else:
        d_in_w, late = _mm(s["u"].astype(MXU), dproj, ta=True, out_dtype=WIRE, name=tag + "_in_bw",
                           ride=([own(k, grads[k]) for k in late_names], True))
        got = dict(zip(late_names, late))
        grads["in_w"] = d_in_w[:, :SSD_DI + SSD_CONV_DIM + SSD_H]
        du, (got["in_w"],) = _mm(dproj, s["w_in"], tb=True, name=tag + "_in_bx", ride=([own("in_w", grads["in_w"])], True))
    return dr, du, grads, rode, got


MLA_LOW = MLA_QR + MLA_KVR + LANES
MLA_ZB = MLA_LOW // LANES


def _rope_mat():
    r = lax.broadcasted_iota(jnp.int32, (LANES, LANES), 0)
    c = lax.broadcasted_iota(jnp.int32, (LANES, LANES), 1)
    hf = MLA_ROPE // 2
    return jnp.where((c < hf) & (r == c + hf), -1.0, 0.0) + jnp.where((c >= hf) & (c < 2 * hf) & (r == c - hf), 1.0, 0.0)


def _rope(x, cosf, sinf):
    return x * cosf + _sel_r(x, _rope_mat()) * sinf


def _rope_adj(d, cosf, sinf):
    return d * cosf - _sel_r(d * sinf, _rope_mat())


def _mla_low_fn(low, qnw, kvnw, cosf, sinf):
    qc, kvc, kr = low[:, :MLA_QR], low[:, MLA_QR:MLA_QR + MLA_KVR], low[:, MLA_QR + MLA_KVR:]
    qn = qc * lax.rsqrt(jnp.mean(qc * qc, -1, keepdims=True) + RMS_EPS) * qnw
    kvn = kvc * lax.rsqrt(jnp.mean(kvc * kvc, -1, keepdims=True) + RMS_EPS) * kvnw
    return qn, kvn, _rope(kr, cosf, sinf)


def _mla_low_fwd(proj, qnw, kvnw, cosf, sinf, name):
    t = proj.shape[0]
    tr = _tile(t, 256)

    def body(low_ref, qnw_ref, kvnw_ref, cos_ref, sin_ref, qn_ref, kvn_ref, kr_ref):
        qn, kvn, kr = _mla_low_fn(low_ref[...], qnw_ref[...], kvnw_ref[...], cos_ref[...], sin_ref[...])
        qn_ref[...] = qn
        kvn_ref[...] = kvn
        kr_ref[...] = kr

    row = lambda wdt: pl.BlockSpec((tr, wdt), lambda i: (i, 0))
    par = lambda wdt: pl.BlockSpec((1, wdt), lambda i: (0, 0))
    sds = jax.ShapeDtypeStruct
    return pl.pallas_call(
        body, name=name, grid=(t // tr,),
        in_specs=[row(MLA_LOW), par(MLA_QR), par(MLA_KVR), row(LANES), row(LANES)],
        out_specs=[row(MLA_QR), row(MLA_KVR), row(LANES)],
        out_shape=[sds((t, MLA_QR), F32), sds((t, MLA_KVR), F32), sds((t, LANES), F32)],
        compiler_params=_cparams(("parallel",)),
    )(proj, qnw, kvnw, cosf, sinf)


def _mla_low_bwd(proj, qnw, kvnw, cosf, sinf, dqn, dkvn, dkr, name):
    t = proj.shape[0]
    tr = _tile(t, 256)

    def body(low_ref, qnw_ref, kvnw_ref, cos_ref, sin_ref, dqn_ref, dkvn_ref, dkr_ref, dlow_ref, dqnw_ref, dkvnw_ref):
        i = pl.program_id(0)

        @pl.when(i == 0)
        def _():
            dqnw_ref[...] = jnp.zeros_like(dqnw_ref)
            dkvnw_ref[...] = jnp.zeros_like(dkvnw_ref)

        cosf, sinf = cos_ref[...], sin_ref[...]
        _, vjp = jax.vjp(lambda a, b, c: _mla_low_fn(a, b, c, cosf, sinf), low_ref[...], qnw_ref[...], kvnw_ref[...])
        dlow, dq, dk = vjp((dqn_ref[...], dkvn_ref[...], dkr_ref[...]))
        dlow_ref[...] = dlow
        dqnw_ref[...] += dq
        dkvnw_ref[...] += dk

    row = lambda wdt: pl.BlockSpec((tr, wdt), lambda i: (i, 0))
    par = lambda wdt: pl.BlockSpec((1, wdt), lambda i: (0, 0))
    sds = jax.ShapeDtypeStruct
    return pl.pallas_call(
        body, name=name, grid=(t // tr,),
        in_specs=[row(MLA_LOW), par(MLA_QR), par(MLA_KVR), row(LANES), row(LANES), row(MLA_QR), row(MLA_KVR), row(LANES)],
        out_specs=[row(MLA_LOW), par(MLA_QR), par(MLA_KVR)],
        out_shape=[sds((t, MLA_LOW), F32), sds((1, MLA_QR), F32), sds((1, MLA_KVR), F32)],
        compiler_params=_cparams(("arbitrary",)),
    )(proj, qnw, kvnw, cosf, sinf, dqn, dkvn, dkr)


def _rope_heads(x, col_blk0, cosf, sinf, adjoint, name):
    t = x.shape[0]
    tr = _tile(t, 512)
    hb = 4
    assert col_blk0 % hb == 0

    def body(x_ref, cos_ref, sin_ref, o_ref):
        f = _rope_adj if adjoint else _rope
        for b in range(hb):
            o_ref[:, b * LANES:(b + 1) * LANES] = f(x_ref[:, b * LANES:(b + 1) * LANES], cos_ref[...], sin_ref[...])

    tab = pl.BlockSpec((tr, LANES), lambda i, h: (i, 0))
    return pl.pallas_call(
        body, name=name, grid=(t // tr, MLA_H // hb),
        in_specs=[pl.BlockSpec((tr, hb * LANES), lambda i, h: (i, col_blk0 // hb + h)), tab, tab],
        out_specs=pl.BlockSpec((tr, hb * LANES), lambda i, h: (i, h)),
        out_shape=jax.ShapeDtypeStruct((t, MLA_H * LANES), F32), compiler_params=_cparams(("parallel", "parallel")),
    )(x, cosf, sinf)


ATT_HB = 4
ATT_W = ATT_HB * LANES


def _att_qk(qn_ref, qr_ref, kn_ref, kr_ref):
    q2 = jnp.concatenate([_heads_of(qn_ref, ATT_HB), _heads_of(qr_ref, ATT_HB)], axis=2)
    kr = kr_ref[...]
    k2 = jnp.concatenate([_heads_of(kn_ref, ATT_HB), jnp.broadcast_to(kr[None], (ATT_HB,) + kr.shape)], axis=2)
    return q2, k2


def _att_scores(q2, k2, masked):
    s = _bdg(_mx(q2 * MLA_SCALE), _mx(k2), 2, 2)
    if masked:
        tq, tk = s.shape[1:]
        s = jnp.where(lax.broadcasted_iota(jnp.int32, (tq, tk), 1) <= lax.broadcasted_iota(jnp.int32, (tq, tk), 0), s, -jnp.inf)
    return s


def _on_causal_blocks(q_blk, k_blk, step):
    @pl.when(k_blk < q_blk)
    def _():
        step(False)

    @pl.when(k_blk == q_blk)
    def _():
        step(True)


def _put_heads(ref, val):
    for b in range(val.shape[0]):
        ref[:, b * LANES:(b + 1) * LANES] = val[b].astype(ref.dtype)


def _att_ds(s, v_ref, o_ref, do_ref, lse_ref):
    do = _heads_of(do_ref, ATT_HB)
    p = jnp.exp(s - _heads_of(lse_ref, ATT_HB)[:, :, 0:1])
    dp = _bdg(_mx(do), _mx(_heads_of(v_ref, ATT_HB)), 2, 2)
    dl = jnp.sum(do * _heads_of(o_ref, ATT_HB), -1, keepdims=True)
    return _mx(p), _mx(p * (dp - dl) * MLA_SCALE), do


def _attn_fwd(q, qr, kv, kr, proj, name, ride=None):
    t = q.shape[0]
    tq = tk = _tile(t, ATT_BLK)
    nq = nk = t // tq

    def body(qn_ref, qr_ref, kn_ref, kr_ref, v_ref, *rest):
        z_refs, (o_ref, og_ref, lse_ref, m_s, l_s, acc_s) = rest[:ATT_HB], rest[ATT_HB:]
        i, j = pl.program_id(1), pl.program_id(2)

        @pl.when(j == 0)
        def _():
            m_s[...] = jnp.full_like(m_s, -jnp.inf)
            l_s[...] = jnp.zeros_like(l_s)
            acc_s[...] = jnp.zeros_like(acc_s)

        def step(masked):
            s = _att_scores(*_att_qk(qn_ref, qr_ref, kn_ref, kr_ref), masked)
            m_new = jnp.maximum(m_s[...], jnp.max(s, -1, keepdims=True))
            p = jnp.exp(s - m_new)
            corr = jnp.exp(m_s[...] - m_new)
            l_s[...] = corr * l_s[...] + jnp.sum(p, -1, keepdims=True)
            acc_s[...] = corr * acc_s[...] + _bdg(_mx(p), _mx(_heads_of(v_ref, ATT_HB)), 2, 1)
            m_s[...] = m_new

        _on_causal_blocks(i, j, step)

        @pl.when(j == nk - 1)
        def _():
            o = acc_s[...] / l_s[...]
            _put_heads(o_ref, o)
            lse = m_s[...] + jnp.log(l_s[...])
            for b in range(ATT_HB):
                og_ref[:, b * LANES:(b + 1) * LANES] = (o[b] * _silu(z_refs[b][...])).astype(og_ref.dtype)
                lse_ref[:, b * LANES:(b + 1) * LANES] = jnp.broadcast_to(lse[b], (tq, LANES))

    qs = lambda off: pl.BlockSpec((tq, ATT_W), lambda h, i, j: (i, off // ATT_HB + h))
    ks = lambda off: pl.BlockSpec((tk, ATT_W), lambda h, i, j: (jnp.minimum(j, i), off // ATT_HB + h))
    zs = [pl.BlockSpec((tq, LANES), functools.partial(lambda b, h, i, j: (i, MLA_ZB + h * ATT_HB + b), b)) for b in range(ATT_HB)]
    sds = jax.ShapeDtypeStruct
    return _hosted_call(
        body, name=name, grid=(MLA_H // ATT_HB, nq, nk),
        in_specs=[qs(0), qs(0), ks(0), pl.BlockSpec((tk, LANES), lambda h, i, j: (jnp.minimum(j, i), 0)), ks(MLA_H)] + zs,
        out_specs=[qs(0), qs(0), qs(0)],
        out_shape=[sds((t, MLA_GATE), F32), sds((t, MLA_GATE), MXU), sds((t, MLA_H * LANES), F32)],
        scratch_shapes=[pltpu.VMEM((ATT_HB, tq, 1), F32), pltpu.VMEM((ATT_HB, tq, 1), F32), pltpu.VMEM((ATT_HB, tq, LANES), F32)],
        sem=("parallel", "parallel", "arbitrary"), args=(q, qr, kv, kr, kv) + (proj,) * ATT_HB, ride=ride)


def _gate_bwd(dog, o, proj, name):
    t = o.shape[0]
    tr = _tile(t, 512)

    def body(d_ref, o_ref, z_ref, do_ref, dz_ref):
        z = z_ref[...]
        sg = jax.nn.sigmoid(z)
        d = d_ref[...]
        do_ref[...] = d * z * sg
        dz_ref[...] = d * o_ref[...] * (sg * (1.0 + z * (1.0 - sg)))

    blk = lambda off: pl.BlockSpec((tr, 512), lambda i, j: (i, off + j))
    assert MLA_LOW % 512 != 0 or True
    zspec = pl.BlockSpec((tr, LANES), lambda i, j: (i, MLA_ZB + j))
    b128 = pl.BlockSpec((tr, LANES), lambda i, j: (i, j))
    sds = jax.ShapeDtypeStruct
    return pl.pallas_call(
        body, name=name, grid=(t // tr, MLA_GATE // LANES),
        in_specs=[b128, b128, zspec], out_specs=[b128, b128],
        out_shape=[sds((t, MLA_GATE), F32), sds((t, MLA_GATE), F32)],
        compiler_params=_cparams(("parallel", "parallel")),
    )(dog, o, proj)


def _attn_bwd_q(q, qr, kv, kr, o, do, lse, name, ride=None):
    t = q.shape[0]
    tq = tk = _tile(t, ATT_BLK)
    nq = nk = t // tq

    def body(qn_ref, qr_ref, kn_ref, kr_ref, v_ref, o_ref, do_ref, lse_ref, dqn_ref, dqr_ref, an_s, ar_s):
        i, j = pl.program_id(1), pl.program_id(2)

        @pl.when(j == 0)
        def _():
            an_s[...] = jnp.zeros_like(an_s)
            ar_s[...] = jnp.zeros_like(ar_s)

        def step(masked):
            q2, k2 = _att_qk(qn_ref, qr_ref, kn_ref, kr_ref)
            ds = _att_ds(_att_scores(q2, k2, masked), v_ref, o_ref, do_ref, lse_ref)[1]
            dq2 = _bdg(ds, _mx(k2), 2, 1)
            an_s[...] += dq2[:, :, :LANES]
            ar_s[...] += dq2[:, :, LANES:]

        _on_causal_blocks(i, j, step)

        @pl.when(j == nk - 1)
        def _():
            _put_heads(dqn_ref, an_s[...])
            _put_heads(dqr_ref, ar_s[...])

    qs = lambda off: pl.BlockSpec((tq, ATT_W), lambda h, i, j: (i, off // ATT_HB + h))
    ks = lambda off: pl.BlockSpec((tk, ATT_W), lambda h, i, j: (jnp.minimum(j, i), off // ATT_HB + h))
    sds = jax.ShapeDtypeStruct
    return _hosted_call(
        body, name=name, grid=(MLA_H // ATT_HB, nq, nk),
        in_specs=[qs(0), qs(0), ks(0), pl.BlockSpec((tk, LANES), lambda h, i, j: (jnp.minimum(j, i), 0)), ks(MLA_H),
                  qs(0), qs(0), qs(0)],
        out_specs=[qs(0), qs(0)],
        out_shape=[sds((t, MLA_H * LANES), F32), sds((t, MLA_H * LANES), F32)],
        scratch_shapes=[pltpu.VMEM((ATT_HB, tq, LANES), F32), pltpu.VMEM((ATT_HB, tq, LANES), F32)],
        sem=("parallel", "parallel", "arbitrary"), args=(q, qr, kv, kr, kv, o, do, lse), ride=ride)


def _attn_bwd_kv(q, qr, kv, kr, o, do, lse, name, ride=None):
    t = q.shape[0]
    tq = tk = _tile(t, ATT_BLK)
    nq = nk = t // tq

    def body(qn_ref, qr_ref, kn_ref, kr_ref, v_ref, o_ref, do_ref, lse_ref, dkn_ref, dv_ref, dkr_ref, akn_s, av_s):
        j, h, i = pl.program_id(0), pl.program_id(1), pl.program_id(2)

        @pl.when((h == 0) & (i == 0))
        def _():
            dkr_ref[...] = jnp.zeros_like(dkr_ref)

        @pl.when(i == 0)
        def _():
            akn_s[...] = jnp.zeros_like(akn_s)
            av_s[...] = jnp.zeros_like(av_s)

        def step(masked):
            q2, k2 = _att_qk(qn_ref, qr_ref, kn_ref, kr_ref)
            p, ds, do = _att_ds(_att_scores(q2, k2, masked), v_ref, o_ref, do_ref, lse_ref)
            av_s[...] += _bdg(p, _mx(do), 1, 1)
            dk2 = _bdg(ds, _mx(q2), 1, 1)
            akn_s[...] += dk2[:, :, :LANES]
            dkr_ref[...] += jnp.sum(dk2[:, :, LANES:], axis=0)

        _on_causal_blocks(i, j, step)

        @pl.when(i == nq - 1)
        def _():
            _put_heads(dkn_ref, akn_s[...])
            _put_heads(dv_ref, av_s[...])

    qs = lambda off: pl.BlockSpec((tq, ATT_W), lambda j, h, i: (jnp.maximum(i, j), off // ATT_HB + h))
    ks = lambda off: pl.BlockSpec((tk, ATT_W), lambda j, h, i: (j, off // ATT_HB + h))
    sds = jax.ShapeDtypeStruct
    return _hosted_call(
        body, name=name, grid=(nk, MLA_H // ATT_HB, nq),
        in_specs=[qs(0), qs(0), ks(0), pl.BlockSpec((tk, LANES), lambda j, h, i: (j, 0)), ks(MLA_H), qs(0), qs(0), qs(0)],
        out_specs=[ks(0), ks(0), pl.BlockSpec((tk, LANES), lambda j, h, i: (j, 0))],
        out_shape=[sds((t, MLA_H * LANES), F32), sds((t, MLA_H * LANES), F32), sds((t, LANES), F32)],
        scratch_shapes=[pltpu.VMEM((ATT_HB, tk, LANES), F32), pltpu.VMEM((ATT_HB, tk, LANES), F32)],
        sem=("parallel", "arbitrary", "arbitrary"), args=(q, qr, kv, kr, kv, o, do, lse), ride=ride)


def _rope_tables(positions):
    lane = jnp.arange(LANES)
    valid = lane < MLA_ROPE
    inv_freq = ROPE_THETA ** (-(2 * (lane % (MLA_ROPE // 2))).astype(F32) / MLA_ROPE)
    ang = positions.astype(F32)[:, None] * inv_freq[None, :]
    return jnp.where(valid, jnp.cos(ang), 0.0), jnp.where(valid, jnp.sin(ang), 0.0)


def _mla_weights(w):
    dt = w["in_w"].dtype
    iw = w["in_w"]
    c1 = MLA_QR + MLA_KVR + MLA_ROPE
    w_in = jnp.concatenate([iw[:, :c1], jnp.zeros((D_MODEL, LANES - MLA_ROPE), dt), iw[:, c1:]], axis=1)
    qu = w["q_up_w"].reshape(MLA_QR, MLA_H, MLA_NOPE + MLA_ROPE)
    qrope = jnp.concatenate([qu[:, :, MLA_NOPE:], jnp.zeros((MLA_QR, MLA_H, LANES - MLA_ROPE), dt)], axis=2)
    w_q = jnp.concatenate([qu[:, :, :MLA_NOPE].reshape(MLA_QR, -1), qrope.reshape(MLA_QR, -1)], axis=1)
    kvu = w["kv_up_w"].reshape(MLA_KVR, MLA_H, MLA_NOPE + MLA_V)
    w_kv = jnp.concatenate([kvu[:, :, :MLA_NOPE].reshape(MLA_KVR, -1), kvu[:, :, MLA_NOPE:].reshape(MLA_KVR, -1)], axis=1)
    return w_in, w_q, w_kv


def _mla_layer_fwd(u, w, ln_g, ln_b, cosf, sinf, tag, rides=None):
    w_in, w_q, w_kv = _mla_weights(w)
    qnw, kvnw = w["q_norm_w"].reshape(1, -1), w["kv_norm_w"].reshape(1, -1)
    proj = _mm(u.astype(MXU), w_in, name=tag + "_in")
    qn, kvn, kr = _mla_low_fwd(proj, qnw, kvnw, cosf, sinf, name=tag + "_low")
    q = _mm(qn, w_q, name=tag + "_qup")
    kv = _mm(kvn, w_kv, name=tag + "_kvup")
    qr = _rope_heads(q, MLA_H, cosf, sinf, False, name=tag + "_qrope")
    rode = {}
    (o, og, lse), rode["attn"] = _attn_fwd(q, qr, kv, kr, proj, name=tag + "_attn", ride=_ride_of(rides, "attn", False))
    y = _mm(og, w["out_w"], name=tag + "_out")
    h = _ln_fwd(u, y, ln_g, ln_b, name=tag + "_ln")
    saved = dict(u=u, w_in=w_in, w_q=w_q, w_kv=w_kv, qnw=qnw, kvnw=kvnw, proj=proj, qn=qn, kvn=kvn, kr=kr, q=q, kv=kv,
                 qr=qr, o=o, og=og, lse=lse, y=y)
    return h, saved, rode


def _mla_layer_bwd(s, w, ln_g, cosf, sinf, dr_up, du_up, tag, rides=None):
    dr, dg, db = _ln_bwd(s["u"], s["y"], ln_g, dr_up, du_up, name=tag + "_ln_b")
    dog = _mm(dr, w["out_w"], tb=True, name=tag + "_out_bx")
    d_out_w = _mm(s["og"], dr, ta=True, out_dtype=WIRE, name=tag + "_out_bw")
    do, dz = _gate_bwd(dog, s["o"], s["proj"], name=tag + "_gate_b")
    rode = {}
    (dqn_h, dqr_rot), rode["attn_bq"] = _attn_bwd_q(s["q"], s["qr"], s["kv"], s["kr"], s["o"], do, s["lse"],
                                                    name=tag + "_attn_bq", ride=_ride_of(rides, "attn_bq", True))
    (dkn_h, dv_h, dkr_rot), rode["attn_bkv"] = _attn_bwd_kv(s["q"], s["qr"], s["kv"], s["kr"], s["o"], do, s["lse"],
                                                           name=tag + "_attn_bkv", ride=_ride_of(rides, "attn_bkv", True))
    dqr = _rope_heads(dqr_rot, 0, cosf, sinf, True, name=tag + "_qrope_b")
    dq = jnp.concatenate([dqn_h, dqr], axis=1).astype(MXU)
    dkv = jnp.concatenate([dkn_h, dv_h], axis=1).astype(MXU)
    d_wq = _mm(s["qn"], dq, ta=True, out_dtype=WIRE, name=tag + "_qup_bw")
    dqn = _mm(dq, s["w_q"], tb=True, name=tag + "_qup_bx")
    d_wkv = _mm(s["kvn"], dkv, ta=True, out_dtype=WIRE, name=tag + "_kvup_bw")
    dkvn = _mm(dkv, s["w_kv"], tb=True, name=tag + "_kvup_bx")
    dlow, dqnw, dkvnw = _mla_low_bwd(s["proj"], s["qnw"], s["kvnw"], cosf, sinf, dqn, dkvn, dkr_rot, name=tag + "_low_b")
    dproj = jnp.concatenate([dlow, dz], axis=1).astype(MXU)
    d_in = _mm(s["u"].astype(MXU), dproj, ta=True, out_dtype=WIRE, name=tag + "_in_bw")
    du = _mm(dproj, s["w_in"], tb=True, name=tag + "_in_bx")
    c1 = MLA_QR + MLA_KVR + MLA_ROPE
    d_in_w = jnp.concatenate([d_in[:, :c1], d_in[:, MLA_LOW:]], axis=1)
    dq3n = d_wq[:, :MLA_H * MLA_NOPE].reshape(MLA_QR, MLA_H, MLA_NOPE)
    dq3r = d_wq[:, MLA_H * MLA_NOPE:].reshape(MLA_QR, MLA_H, LANES)[:, :, :MLA_ROPE]
    d_q_up = jnp.concatenate([dq3n, dq3r], axis=2).reshape(MLA_QR, -1)
    dkv3 = d_wkv.reshape(MLA_KVR, 2, MLA_H, MLA_NOPE)
    d_kv_up = jnp.concatenate([dkv3[:, 0], dkv3[:, 1]], axis=2).reshape(MLA_KVR, -1)
    grads = dict(in_w=d_in_w, q_norm_w=dqnw[0], q_up_w=d_q_up, kv_norm_w=dkvnw[0], kv_up_w=d_kv_up, out_w=d_out_w,
                 ln_g=dg[0], ln_b=db[0])
    return dr, du, grads, rode


GDN_REP = GDN_HV // GDN_HK
GDN_A_LANE = GDN_HV
GDN_HPB = 4
GDN_VPB = GDN_HPB * GDN_REP


def _bdg(a, b, ca, cb):
    return lax.dot_general(a, b, (((ca,), (cb,)), ((0,), (0,))), preferred_element_type=F32)


@jax.custom_vjp
def _bnn(a, b):
    return _bdg(_mx(a), _mx(b), 2, 1)


def _bnn_f(a, b):
    return _bnn(a, b), (a, b)


def _bnn_b(res, ct):
    a, b = res
    return _bdg(_mx(ct), _mx(b), 2, 2), _bdg(_mx(a), _mx(ct), 1, 1)


_bnn.defvjp(_bnn_f, _bnn_b)


@jax.custom_vjp
def _bnt(a, b):
    return _bdg(_mx(a), _mx(b), 2, 2)


def _bnt_f(a, b):
    return _bnt(a, b), (a, b)


def _bnt_b(res, ct):
    a, b = res
    return _bdg(_mx(ct), _mx(b), 2, 1), _bdg(_mx(ct), _mx(a), 1, 1)


_bnt.defvjp(_bnt_f, _bnt_b)


@jax.custom_vjp
def _btn(a, b):
    return _bdg(_mx(a), _mx(b), 1, 1)


def _btn_f(a, b):
    return _btn(a, b), (a, b)


def _btn_b(res, ct):
    a, b = res
    return _bdg(_mx(b), _mx(ct), 2, 2), _bdg(_mx(a), _mx(ct), 2, 1)


_btn.defvjp(_btn_f, _btn_b)


def _h3(a, b, ca=2, cb=1):
    ah, bh = _mx(a), _mx(b)
    al, bl = _mx(a - ah.astype(F32)), _mx(b - bh.astype(F32))
    return _bdg(ah, bh, ca, cb) + (_bdg(ah, bl, ca, cb) + _bdg(al, bh, ca, cb))


@jax.custom_vjp
def _neumann_inverse(x):
    L = x.shape[-1]
    eye = (lax.broadcasted_iota(jnp.int32, (L, L), 0) == lax.broadcasted_iota(jnp.int32, (L, L), 1)).astype(F32)
    inv = eye + x
    xp = x
    for _ in range(L.bit_length() - 2):
        xp = _h3(xp, xp)
        inv = inv + _h3(inv, xp)
    return inv


def _neumann_f(x):
    inv = _neumann_inverse(x)
    return inv, inv


def _neumann_b(inv, ct):
    return (_h3(_h3(inv, ct, 1, 1), inv, 2, 2),)


_neumann_inverse.defvjp(_neumann_f, _neumann_b)


@jax.custom_vjp
def _saved_inverse(x, inv):
    return inv


def _saved_f(x, inv):
    return inv, inv


def _saved_b(inv, ct):
    return _neumann_b(inv, ct)[0], jnp.zeros_like(inv)


_saved_inverse.defvjp(_saved_f, _saved_b)


def _cat0(parts):
    return jnp.concatenate([p[None] for p in parts], axis=0)


def _gdn_chunk(hb, q, k, v, z, ba, alog, dtb, nw, s, inv_saved=None):
    L = q.shape[1]
    r_i = lax.broadcasted_iota(jnp.int32, (L, L), 0)
    c_i = lax.broadcasted_iota(jnp.int32, (L, L), 1)
    incl, strict = r_i >= c_i, r_i > c_i
    rep = lambda t: jnp.broadcast_to(t[:, None], (GDN_HPB, GDN_REP) + t.shape[1:]).reshape((GDN_VPB,) + t.shape[1:])
    qn = rep(q * lax.rsqrt(jnp.sum(q * q, -1, keepdims=True) + RMS_EPS) * (GDN_DK ** -0.5))
    kn = rep(k * lax.rsqrt(jnp.sum(k * k, -1, keepdims=True) + RMS_EPS))
    beta_all = jax.nn.sigmoid(ba)
    g_all = -jnp.exp(alog) * _softplus(ba + dtb)
    gcs_all = _sel_l(incl.astype(F32), g_all)
    lane = lax.broadcasted_iota(jnp.int32, (L, LANES), 1)
    pick = lambda mat, idx: jnp.sum(jnp.where(lane == idx, mat, 0.0), axis=1, keepdims=True)
    beta = _cat0([pick(beta_all, GDN_VPB * hb + b) for b in range(GDN_VPB)])
    gc = _cat0([pick(gcs_all, GDN_A_LANE + GDN_VPB * hb + b) for b in range(GDN_VPB)])
    gm = jnp.broadcast_to(gc, (GDN_VPB, L, L))
    decay = jnp.exp(jnp.where(incl, gm - jnp.swapaxes(gm, 1, 2), -jnp.inf))
    kb = kn * beta
    eg = jnp.exp(gc)
    x = -jnp.where(strict, _bnt(kb, kn) * decay, 0.0)
    inv = _neumann_inverse(x) if inv_saved is None else _saved_inverse(x, inv_saved)
    uw = _bnn(inv, jnp.concatenate([v * beta, kb * eg], axis=2))
    uu, ww = uw[:, :, :GDN_DV], uw[:, :, GDN_DV:]
    qk = jnp.where(incl, _bnt(qn, kn) * decay, 0.0)
    last = lax.broadcasted_iota(jnp.int32, (L, 1), 0) == L - 1
    glast = jnp.sum(jnp.where(last, gc, 0.0), axis=1, keepdims=True)
    kdec = kn * jnp.exp(glast - gc)
    vnew = uu - _bnn(ww, s)
    o = _bnn(qn * eg, s) + _bnn(qk, vnew)
    new = s * jnp.exp(glast) + _btn(kdec, vnew)
    on = o * lax.rsqrt(jnp.mean(o * o, -1, keepdims=True) + RMS_EPS) * nw * _silu(z)
    return (on, new, inv) if inv_saved is None else (on, new)


def _heads_of(ref, n):
    return _cat0([ref[:, i * LANES:(i + 1) * LANES] for i in range(n)])


def _gdn_specs(nc, rev):
    cc = (lambda c: nc - 1 - c) if rev else (lambda c: c)
    wq, wv = GDN_HPB * GDN_DK, GDN_VPB * GDN_DV
    par = pl.BlockSpec((1, LANES), lambda c, h: (0, 0))
    return dict(
        q=pl.BlockSpec((GDN_L, wq), lambda c, h: (cc(c), h)),
        k=pl.BlockSpec((GDN_L, wq), lambda c, h: (cc(c), GDN_KEY // wq + h)),
        v=pl.BlockSpec((GDN_L, wv), lambda c, h: (cc(c), 2 * GDN_KEY // wv + h)),
        z=pl.BlockSpec((GDN_L, wv), lambda c, h: (cc(c), GDN_CONV_DIM // wv + h)),
        ba=pl.BlockSpec((GDN_L, LANES), lambda c, h: (cc(c), (GDN_CONV_DIM + GDN_VAL) // LANES)),
        par=par,
        st=pl.BlockSpec((1, GDN_VPB, GDN_DK, GDN_DV), lambda c, h: (cc(c), h, 0, 0)),
        inv=pl.BlockSpec((1, GDN_VPB, GDN_L, GDN_L), lambda c, h: (cc(c), h, 0, 0)),
        o=pl.BlockSpec((GDN_L, wv), lambda c, h: (cc(c), h)),
        qk_out=pl.BlockSpec((GDN_L, wq), lambda c, h: (cc(c), h)),
        ba_out=pl.BlockSpec((GDN_L, LANES), lambda c, h: (cc(c), 0)),
    )


def _gdn_fwd(proj, act, alog, dtb, nw, name, ride=None):
    t = proj.shape[0]
    nc = t // GDN_L
    sp = _gdn_specs(nc, False)

    def body(q_ref, k_ref, v_ref, z_ref, ba_ref, alog_ref, dtb_ref, nw_ref, o_ref, st_ref, inv_ref, state):
        c, h = pl.program_id(0), pl.program_id(1)

        mine = pl.ds(h * GDN_VPB, GDN_VPB)

        @pl.when(c == 0)
        def _():
            state[mine] = jnp.zeros((GDN_VPB, GDN_DK, GDN_DV), F32)

        prev = state[mine]
        st_ref[0] = prev
        on, new, inv = _gdn_chunk(h, _heads_of(q_ref, GDN_HPB), _heads_of(k_ref, GDN_HPB), _heads_of(v_ref, GDN_VPB),
                                  _heads_of(z_ref, GDN_VPB), ba_ref[...], alog_ref[...], dtb_ref[...], nw_ref[...], prev)
        inv_ref[0] = inv
        for b in range(GDN_VPB):
            o_ref[:, b * LANES:(b + 1) * LANES] = on[b].astype(o_ref.dtype)
        state[mine] = new

    sds = jax.ShapeDtypeStruct
    return _hosted_call(
        body, name=name, grid=(nc, GDN_HK // GDN_HPB),
        in_specs=[sp["q"], sp["k"], sp["v"], sp["z"], sp["ba"], sp["par"], sp["par"], sp["par"]],
        out_specs=[sp["o"], sp["st"], sp["inv"]],
        out_shape=[sds((t, GDN_VAL), MXU), sds((nc, GDN_HV, GDN_DK, GDN_DV), F32), sds((nc, GDN_HV, GDN_L, GDN_L), F32)],
        scratch_shapes=[pltpu.VMEM((GDN_HV, GDN_DK, GDN_DV), F32)],
        sem=("arbitrary", "arbitrary"), args=(act, act, act, proj, proj, alog, dtb, nw), ride=ride)


def _gdn_bwd(proj, act, alog, dtb, nw, states, invs, don, name, ride=None):
    t = proj.shape[0]
    nc = t // GDN_L
    sp = _gdn_specs(nc, True)

    def body(q_ref, k_ref, v_ref, z_ref, ba_ref, alog_ref, dtb_ref, nw_ref, st_ref, do_ref, inv_ref,
             dact_ref, dz_ref, dba_ref, dalog_ref, ddtb_ref, dnw_ref, dstate):
        c, h = pl.program_id(0), pl.program_id(1)

        mine = pl.ds(h * GDN_VPB, GDN_VPB)

        @pl.when(c == 0)
        def _():
            dstate[mine] = jnp.zeros((GDN_VPB, GDN_DK, GDN_DV), F32)

        @pl.when((c == 0) & (h == 0))
        def _():
            dalog_ref[...] = jnp.zeros_like(dalog_ref)
            ddtb_ref[...] = jnp.zeros_like(ddtb_ref)
            dnw_ref[...] = jnp.zeros_like(dnw_ref)

        @pl.when(h == 0)
        def _():
            dba_ref[...] = jnp.zeros_like(dba_ref)

        _, vjp = jax.vjp(functools.partial(_gdn_chunk, h, inv_saved=inv_ref[0]), _heads_of(q_ref, GDN_HPB),
                         _heads_of(k_ref, GDN_HPB), _heads_of(v_ref, GDN_VPB), _heads_of(z_ref, GDN_VPB), ba_ref[...],
                         alog_ref[...], dtb_ref[...], nw_ref[...], st_ref[0])
        dq, dk, dv, dz, dba, dalog, ddtb, dnw, dprev = vjp((_heads_of(do_ref, GDN_VPB), dstate[mine]))
        for b in range(GDN_VPB):
            dz_ref[:, b * LANES:(b + 1) * LANES] = dz[b]
        for hi in range(GDN_HK // GDN_HPB):
            @pl.when(h == hi)
            def _(hi=hi):
                for i in range(GDN_HPB):
                    c0 = (hi * GDN_HPB + i) * GDN_DK
                    dact_ref[:, c0:c0 + GDN_DK] = dq[i]
                    dact_ref[:, GDN_KEY + c0:GDN_KEY + c0 + GDN_DK] = dk[i]
                for b in range(GDN_VPB):
                    c0 = 2 * GDN_KEY + (hi * GDN_VPB + b) * GDN_DV
                    dact_ref[:, c0:c0 + GDN_DV] = dv[b]
        dba_ref[...] += dba
        dalog_ref[...] += dalog
        ddtb_ref[...] += ddtb
        dnw_ref[...] += dnw
        dstate[mine] = dprev

    sds = jax.ShapeDtypeStruct
    par_out = pl.BlockSpec((1, LANES), lambda c, h: (0, 0))
    return _hosted_call(
        body, name=name, grid=(nc, GDN_HK // GDN_HPB),
        in_specs=[sp["q"], sp["k"], sp["v"], sp["z"], sp["ba"], sp["par"], sp["par"], sp["par"], sp["st"], sp["o"],
                  sp["inv"]],
        out_specs=[pl.BlockSpec((GDN_L, GDN_CONV_DIM), lambda c, h: (nc - 1 - c, 0)), sp["o"], sp["ba_out"],
                   par_out, par_out, par_out],
        out_shape=[sds((t, GDN_CONV_DIM), F32), sds((t, GDN_VAL), F32), sds((t, LANES), F32), sds((1, LANES), F32),
                   sds((1, LANES), F32), sds((1, LANES), F32)],
        scratch_shapes=[pltpu.VMEM((GDN_HV, GDN_DK, GDN_DV), F32)],
        sem=("arbitrary", "arbitrary"), args=(act, act, act, proj, proj, alog, dtb, nw, states, don, invs), ride=ride)


GDN_PROJ = GDN_CONV_DIM + GDN_VAL + 2 * GDN_HV


def _gdn_layer_fwd(u, w, ln_g, ln_b, tag, rides=None, late=None):
    w_in = jnp.concatenate([w["in_w"], jnp.zeros((D_MODEL, GDN_PROJ_PAD - GDN_PROJ), w["in_w"].dtype)], axis=1)
    alog = _pad_lanes(w["a_log"], offset=GDN_A_LANE)
    dtb = _pad_lanes(w["dt_bias"], offset=GDN_A_LANE)
    nw = w["norm_w"].reshape(1, GDN_DV)
    zb = jnp.zeros((1, GDN_CONV_DIM), F32)
    if late is None:
        proj = _mm(u.astype(MXU), w_in, name=tag + "_in")
    else:
        proj, got = _mm(u.astype(MXU), w_in, name=tag + "_in", ride=(late[0], False))
        w.update(late[1](got))
    act, _ = _conv_fwd(proj, 0, w["conv_w"], zb, name=tag + "_conv")
    rode = {}
    (on, states, invs), rode["delta"] = _gdn_fwd(proj, act, alog, dtb, nw, name=tag + "_delta",
                                                 ride=_ride_of(rides, "delta", False))
    y = _mm(on, w["out_w"], name=tag + "_out")
    h = _ln_fwd(u, y, ln_g, ln_b, name=tag + "_ln")
    saved = dict(u=u, w_in=w_in, proj=proj, act=act, states=states, invs=invs, on=on, y=y, alog=alog, dtb=dtb, nw=nw, zb=zb)
    return h, saved, rode


def _gdn_layer_bwd(s, w, ln_g, dr_up, du_up, tag, rides=None, own=None):
    t = s["u"].shape[0]
    dr, dg, db = _ln_bwd(s["u"], s["y"], ln_g, dr_up, du_up, name=tag + "_ln_b")
    don = _mm(dr, w["out_w"], tb=True, name=tag + "_out_bx")
    d_out_w = _mm(s["on"], dr, ta=True, out_dtype=WIRE, name=tag + "_out_bw")
    carried = list((rides or {}).get("delta_b", [])) + ([own("out_w", d_out_w)] if own else [])
    rode = {}
    (dact, dz, dba, dalog, ddtb, dnw), rode["delta_b"] = _gdn_bwd(
        s["proj"], s["act"], s["alog"], s["dtb"], s["nw"], s["states"], s["invs"], don, name=tag + "_delta_b",
        ride=(carried, True) if carried else None)
    dqkv, d_conv_w, _ = _conv_bwd(s["proj"], 0, w["conv_w"], s["zb"], dact, name=tag + "_conv_b")
    dproj = jnp.concatenate([dqkv, dz, dba, jnp.zeros((t, GDN_PROJ_PAD - GDN_PROJ - (LANES - 2 * GDN_HV)), F32)],
                            axis=1).astype(MXU)
    d_in = _mm(s["u"].astype(MXU), dproj, ta=True, out_dtype=WIRE, name=tag + "_in_bw")
    du = _mm(dproj, s["w_in"], tb=True, name=tag + "_in_bx")
    grads = dict(in_w=d_in[:, :GDN_PROJ], conv_w=d_conv_w, a_log=dalog[0, GDN_A_LANE:GDN_A_LANE + GDN_HV],
                 dt_bias=ddtb[0, GDN_A_LANE:GDN_A_LANE + GDN_HV], norm_w=dnw[0], out_w=d_out_w, ln_g=dg[0], ln_b=db[0])
    return dr, du, grads, rode


def _mesh_pos():
    return lax.axis_index("x"), lax.axis_index("y"), lax.axis_index("c")


def _peer(k, x, y, c):
    return ((1 - x) if k & 4 else x, (1 - y) if k & 2 else y, (1 - c) if k & 1 else c)


def _ride_copies(ins, outs, send, recv, loc, scatter, with_arrivals):
    n = len(ins)
    x, y, c = _mesh_pos()
    me = 4 * x + 2 * y + c
    local = [pltpu.make_async_copy(ins[i].at[me] if scatter else ins[i], outs[i].at[me], loc.at[i]) for i in range(n)]
    sends, arrivals = [], []
    for k in range(1, N_DEV):
        peer = _peer(k, x, y, c)
        pidx = 4 * peer[0] + 2 * peer[1] + peer[2]
        for i in range(n):
            src = ins[i].at[pidx] if scatter else ins[i]
            sems = dict(send_sem=send.at[i, k - 1], recv_sem=recv.at[i, k - 1], device_id=peer,
                        device_id_type=pl.DeviceIdType.MESH)
            sends.append(pltpu.make_async_remote_copy(src_ref=src, dst_ref=outs[i].at[me], **sems))
            if with_arrivals:
                arrivals.append(pltpu.make_async_remote_copy(src_ref=src, dst_ref=outs[i].at[pidx], **sems))
    return local, sends, arrivals


def _gather_copy(ins, outs, send, recv, i, k, block, to, from_input=False):
    slot = outs[i].at[4 * block[0] + 2 * block[1] + block[2]]
    return pltpu.make_async_remote_copy(src_ref=ins[i] if from_input else slot, dst_ref=slot, send_sem=send.at[i, k],
                                        recv_sem=recv.at[i, k], device_id=to, device_id_type=pl.DeviceIdType.MESH)


def _gather_places():
    x, y, c = _mesh_pos()
    return (x, y, c), (x, y, 1 - c), [(x, 1 - y), (1 - x, y), (1 - x, 1 - y)], c


def _ride_start(ins, outs, send, recv, loc, scatter, direct=True):
    if direct:
        local, sends, _ = _ride_copies(ins, outs, send, recv, loc, scatter, False)
        for cp in local + sends:
            cp.start()
        return
    me, sibling, chips, c = _gather_places()
    for i in range(len(ins)):
        pltpu.make_async_copy(ins[i], outs[i].at[4 * me[0] + 2 * me[1] + me[2]], loc.at[i]).start()
        _gather_copy(ins, outs, send, recv, i, 0, me, sibling, True).start()
        for j, ch in enumerate(chips):
            _gather_copy(ins, outs, send, recv, i, 1 + j, me, (*ch, c), True).start()


def _ride_wait(ins, outs, send, recv, loc, scatter, direct=True):
    if direct:
        local, sends, arrivals = _ride_copies(ins, outs, send, recv, loc, scatter, True)
        for cp in arrivals:
            cp.wait_recv()
        for cp in sends:
            cp.wait_send()
        for cp in local:
            cp.wait()
        return
    me, sibling, chips, c = _gather_places()
    n = len(ins)
    passed = []
    for j, ch in enumerate(chips):
        for i in range(n):
            _gather_copy(ins, outs, send, recv, i, 1 + j, (*ch, c), me).wait_recv()
            cp = _gather_copy(ins, outs, send, recv, i, 4 + j, (*ch, c), sibling)
            cp.start()
            passed.append(cp)
    for i in range(n):
        _gather_copy(ins, outs, send, recv, i, 0, sibling, me).wait_recv()
        for j, ch in enumerate(chips):
            _gather_copy(ins, outs, send, recv, i, 4 + j, (*ch, 1 - c), me).wait_recv()
    for i in range(n):
        _gather_copy(ins, outs, send, recv, i, 0, me, sibling, True).wait_send()
        for j, ch in enumerate(chips):
            _gather_copy(ins, outs, send, recv, i, 1 + j, me, (*ch, c), True).wait_send()
    for cp in passed:
        cp.wait_send()
    for i in range(n):
        pltpu.make_async_copy(ins[i], outs[i].at[4 * me[0] + 2 * me[1] + me[2]], loc.at[i]).wait()


def _ride_shapes(arrs, scatter):
    n = len(arrs)
    out_shape = [jax.ShapeDtypeStruct(a.shape if scatter else (N_DEV,) + a.shape, a.dtype) for a in arrs]
    scratch = [pltpu.SemaphoreType.DMA((n, N_DEV - 1)), pltpu.SemaphoreType.DMA((n, N_DEV - 1)), pltpu.SemaphoreType.DMA((n,))]
    return out_shape, scratch


def _exchange(arrs, scatter, name):
    n = len(arrs)
    hbm = pl.BlockSpec(memory_space=pltpu.HBM)

    def body(*refs):
        ins, outs = refs[:n], refs[n:2 * n]
        _ride_start(ins, outs, *refs[2 * n:], scatter, direct=scatter)
        _ride_wait(ins, outs, *refs[2 * n:], scatter, direct=scatter)

    out_shape, scratch = _ride_shapes(arrs, scatter)
    return pl.pallas_call(
        body, name=name, in_specs=[hbm] * n, out_specs=[hbm] * n, out_shape=out_shape, scratch_shapes=scratch,
        compiler_params=pltpu.CompilerParams(has_side_effects=True),
    )(*arrs)


def _hosted_call(body, *, name, grid, in_specs, out_specs, out_shape, scratch_shapes, sem, args, ride=None):
    if ride is None:
        return pl.pallas_call(body, name=name, grid=grid, in_specs=in_specs, out_specs=out_specs, out_shape=out_shape,
                              scratch_shapes=scratch_shapes, compiler_params=_cparams(sem))(*args), []
    arrs, scatter = ride
    n, ni, no, ns = len(arrs), len(in_specs), len(out_specs), len(scratch_shapes)
    hbm = pl.BlockSpec(memory_space=pltpu.HBM)
    r_shape, r_scratch = _ride_shapes(arrs, scatter)

    def full(*refs):
        a, ri = refs[:ni], refs[ni:ni + n]
        o, ro = refs[ni + n:ni + n + no], refs[ni + n + no:ni + 2 * n + no]
        s, rs = refs[ni + 2 * n + no:ni + 2 * n + no + ns], refs[ni + 2 * n + no + ns:]
        ids = [pl.program_id(d) for d in range(len(grid))]
        first, last = ids[0] == 0, ids[0] == grid[0] - 1
        for d in range(1, len(grid)):
            first, last = first & (ids[d] == 0), last & (ids[d] == grid[d] - 1)

        @pl.when(first)
        def _():
            _ride_start(ri, ro, *rs, scatter)

        body(*a, *o, *s)

        @pl.when(last)
        def _():
            _ride_wait(ri, ro, *rs, scatter)

    outs = pl.pallas_call(
        full, name=name, grid=grid, in_specs=list(in_specs) + [hbm] * n, out_specs=list(out_specs) + [hbm] * n,
        out_shape=list(out_shape) + r_shape, scratch_shapes=list(scratch_shapes) + r_scratch,
        compiler_params=pltpu.CompilerParams(dimension_semantics=("arbitrary",) * len(grid), vmem_limit_bytes=VMEM_LIMIT,
                                             has_side_effects=True),
    )(*args, *arrs)
    return outs[:no], list(outs[no:])


def _to_wire(w3, j, name):
    _, r, c = w3.shape
    tr = _row_tile(r, c)

    def body(w_ref, o_ref):
        o_ref[...] = w_ref[0].astype(o_ref.dtype)

    return pl.pallas_call(
        body, name=name, grid=(r // tr,), in_specs=[pl.BlockSpec((1, tr, c), lambda i: (j, i, 0))],
        out_specs=pl.BlockSpec((tr, c), lambda i: (i, 0)), out_shape=jax.ShapeDtypeStruct((r, c), WIRE),
        compiler_params=_cparams(("parallel",)),
    )(w3)


def _unshard(g, ax):
    g = jnp.moveaxis(g, 0, ax)
    sh = g.shape
    return g.reshape(sh[:ax] + (sh[ax] * sh[ax + 1],) + sh[ax + 2:])


def _to_parts(full, ax):
    sh = full.shape
    full = full.reshape(sh[:ax] + (N_DEV, sh[ax] // N_DEV) + sh[ax + 1:])
    return jnp.moveaxis(full, ax, 0)


def _row_tile(r, c):
    cap = max(8, (256 * 1024) // max(c, 1))
    best = None
    for d in range(8, min(r, cap) + 1, 8):
        if r % d == 0:
            best = d
    return r if best is None else best


def _adamw(w, m, v, parts, name):
    r, c = w.shape
    tr = _row_tile(r, c)

    def body(w_ref, m_ref, v_ref, p_ref, g_ref, d_ref, nm_ref, nv_ref):
        g = p_ref[0].astype(F32)
        for q in range(1, N_DEV):
            g = g + p_ref[q].astype(F32)
        nm = ADAM_B1 * m_ref[...] + (1.0 - ADAM_B1) * g
        nv = ADAM_B2 * v_ref[...] + (1.0 - ADAM_B2) * (g * g)
        m_hat = nm / (1.0 - ADAM_B1 ** ADAM_STEP)
        v_hat = nv / (1.0 - ADAM_B2 ** ADAM_STEP)
        g_ref[...] = g
        d_ref[...] = -ADAM_LR * (m_hat / (jnp.sqrt(v_hat) + ADAM_EPS) + ADAM_WD * w_ref[...])
        nm_ref[...] = nm
        nv_ref[...] = nv

    row = pl.BlockSpec((tr, c), lambda i: (i, 0))
    out = jax.ShapeDtypeStruct((r, c), F32)
    return pl.pallas_call(
        body, name=name, grid=(r // tr,),
        in_specs=[row, row, row, pl.BlockSpec((N_DEV, tr, c), lambda i: (0, i, 0))],
        out_specs=[row] * 4, out_shape=[out] * 4, compiler_params=_cparams(("parallel",)),
    )(w, m, v, parts)


WEIGHTS = ['ssd_in_w', 'ssd_conv_w', 'ssd_conv_b', 'ssd_dt_bias', 'ssd_a_log', 'ssd_d', 'ssd_norm_w', 'ssd_out_w',
           'mla_in_w', 'mla_q_norm_w', 'mla_q_up_w', 'mla_kv_norm_w', 'mla_kv_up_w', 'mla_out_w', 'gdn_in_w',
           'gdn_conv_w', 'gdn_a_log', 'gdn_dt_bias', 'gdn_norm_w', 'gdn_out_w', 'ln_g', 'ln_b']
SHARDED = {'ssd_in_w': (1, True), 'ssd_conv_w': (1, False), 'ssd_conv_b': (0, False), 'ssd_norm_w': (0, False),
           'ssd_out_w': (0, True), 'mla_in_w': (1, True), 'mla_q_up_w': (1, True), 'mla_kv_up_w': (1, True),
           'mla_out_w': (0, True), 'gdn_in_w': (1, True), 'gdn_conv_w': (1, False), 'gdn_out_w': (0, True)}
REPLICATED = [n for n in WEIGHTS if n not in SHARDED]


def _pack_small(vals):
    flat = jnp.concatenate([vals[n].reshape(-1).astype(F32) for n in REPLICATED])
    rows = -(-flat.shape[0] // (8 * LANES)) * 8
    return jnp.pad(flat, (0, rows * LANES - flat.shape[0])).reshape(rows, LANES)


def _unpack_small(slab, like):
    flat = slab.reshape(-1)
    out, off = {}, 0
    for n in REPLICATED:
        sz = like[n].size
        out[n] = flat[off:off + sz].reshape(like[n].shape)
        off += sz
    return out


def kernel(x, positions, ssd_in_w, ssd_conv_w, ssd_conv_b, ssd_dt_bias, ssd_a_log, ssd_d, ssd_norm_w, ssd_out_w, mla_in_w, mla_q_norm_w, mla_q_up_w, mla_kv_norm_w, mla_kv_up_w, mla_out_w, gdn_in_w, gdn_conv_w, gdn_a_log, gdn_dt_bias, gdn_norm_w, gdn_out_w, ln_g, ln_b, loss_target, m_ssd_in_w, m_ssd_conv_w, m_ssd_conv_b, m_ssd_dt_bias, m_ssd_a_log, m_ssd_d, m_ssd_norm_w, m_ssd_out_w, m_mla_in_w, m_mla_q_norm_w, m_mla_q_up_w, m_mla_kv_norm_w, m_mla_kv_up_w, m_mla_out_w, m_gdn_in_w, m_gdn_conv_w, m_gdn_a_log, m_gdn_dt_bias, m_gdn_norm_w, m_gdn_out_w, m_ln_g, m_ln_b, v_ssd_in_w, v_ssd_conv_w, v_ssd_conv_b, v_ssd_dt_bias, v_ssd_a_log, v_ssd_d, v_ssd_norm_w, v_ssd_out_w, v_mla_in_w, v_mla_q_norm_w, v_mla_q_up_w, v_mla_kv_norm_w, v_mla_kv_up_w, v_mla_out_w, v_gdn_in_w, v_gdn_conv_w, v_gdn_a_log, v_gdn_dt_bias, v_gdn_norm_w, v_gdn_out_w, v_ln_g, v_ln_b):
    loc = locals()
    w = {n: loc[n] for n in WEIGHTS}
    m = {n: loc["m_" + n] for n in WEIGHTS}
    v = {n: loc["v_" + n] for n in WEIGHTS}
    xs, pos, tgt = x[0], positions[0], loss_target[0]

    def names_of(prefix):
        return [n for n in SHARDED if n.startswith(prefix + "_")]

    def shard(n, j):
        return _to_wire(w[n], j, name="wire_%s_%d" % (n, j)) if SHARDED[n][1] else w[n][j]

    def full(n, gathered):
        return _unshard(gathered, SHARDED[n][0])

    def slots(n, g):
        return _to_parts(g, SHARDED[n][0]).astype(WIRE if SHARDED[n][1] else F32)

    def replicated(prefix, j):
        return {n[len(prefix) + 1:]: w[n][j] for n in REPLICATED if n.startswith(prefix + "_")}

    def late_weights(prefix, j, keys):
        return [shard(prefix + "_" + k, j) for k in keys], lambda got: {k: full(prefix + "_" + k, g) for k, g in zip(keys, got)}

    lg = lambda i: w["ln_g"][i].reshape(1, D_MODEL)
    lb = lambda i: w["ln_b"][i].reshape(1, D_MODEL)
    cosf, sinf = _rope_tables(pos)
    ssd_late = ("conv_w", "conv_b", "norm_w", "out_w")

    w_s0 = dict(replicated("ssd", 0), in_w=full("ssd_in_w", _exchange([shard("ssd_in_w", 0)], False, name="gather_ssd0")[0]))
    h1, s0, got = _ssd_layer_fwd(
        xs, w_s0, lg(0), lb(0), "ssd0", late=late_weights("ssd", 0, ssd_late),
        rides={"conv": [shard("mla_q_up_w", 0), shard("mla_kv_up_w", 0)], "scan": [shard("mla_in_w", 0)],
               "out": [shard("mla_out_w", 0)]})
    w_m0 = dict(replicated("mla", 0), q_up_w=full("mla_q_up_w", got["conv"][0]), kv_up_w=full("mla_kv_up_w", got["conv"][1]),
                in_w=full("mla_in_w", got["scan"][0]), out_w=full("mla_out_w", got["out"][0]))
    h2, s1, got = _mla_layer_fwd(h1, w_m0, lg(1), lb(1), cosf, sinf, "mla0", rides={"attn": [shard("gdn_in_w", 0)]})
    w_g0 = dict(replicated("gdn", 0), in_w=full("gdn_in_w", got["attn"][0]))
    h3, s2, got = _gdn_layer_fwd(h2, w_g0, lg(2), lb(2), "gdn0", late=late_weights("gdn", 0, ("conv_w", "out_w")),
                                 rides={"delta": [shard("ssd_in_w", 1)]})
    w_s1 = dict(replicated("ssd", 1), in_w=full("ssd_in_w", got["delta"][0]))
    h4, s3, _ = _ssd_layer_fwd(h3, w_s1, lg(3), lb(3), "ssd1", late=late_weights("ssd", 1, ssd_late))
    loss_tile, dl = _loss_head(h4, tgt, name="loss_head")

    dr3, du3, g3, _, _ = _ssd_layer_bwd(s3, w_s1, lg(3), jnp.zeros_like(dl), dl, "ssd1")
    dr2, du2, g2, got = _gdn_layer_bwd(s2, w_g0, lg(2), dr3, du3, "gdn0", own=lambda k, g: slots("gdn_" + k, g),
                                       rides={"delta_b": [slots(n, g3[n[4:]]) for n in names_of("ssd")]})
    r3, r2_out = got["delta_b"][:-1], got["delta_b"][-1]
    gin = slots("gdn_in_w", g2["in_w"])
    half = gin.shape[1] // 2
    dr1, du1, g1, got = _mla_layer_bwd(
        s1, w_m0, lg(1), cosf, sinf, dr2, du2, "mla0",
        rides={"attn_bq": [gin[:, :half]], "attn_bkv": [gin[:, half:], slots("gdn_conv_w", g2["conv_w"])]})
    r2 = {"gdn_in_w": jnp.concatenate([got["attn_bq"][0], got["attn_bkv"][0]], axis=1), "gdn_conv_w": got["attn_bkv"][1],
          "gdn_out_w": r2_out}
    dr0, du0, g0, got, r0 = _ssd_layer_bwd(s0, w_s0, lg(0), dr1, du1, "ssd0", own=lambda k, g: slots("ssd_" + k, g),
                                           rides={"scan_b": [slots(n, g1[n[4:]]) for n in names_of("mla")]})
    r1 = got["scan_b"]
    grad_x = _axpy(dr0, du0, name="grad_x")[None]

    gsmall = {"ssd_" + k: jnp.stack([g0[k], g3[k]]) for k in ("dt_bias", "a_log", "d")}
    gsmall.update({"mla_" + k: g1[k][None] for k in ("q_norm_w", "kv_norm_w")})
    gsmall.update({"gdn_" + k: g2[k][None] for k in ("a_log", "dt_bias", "norm_w")})
    gsmall["ln_g"] = jnp.stack([g0["ln_g"], g1["ln_g"], g2["ln_g"], g3["ln_g"]])
    gsmall["ln_b"] = jnp.stack([g0["ln_b"], g1["ln_b"], g2["ln_b"], g3["ln_b"]])
    small = _pack_small(gsmall)
    rsmall, = _exchange([jnp.broadcast_to(small[None], (N_DEV,) + small.shape)], True, name="gather_small_grads")

    recvd = {n: jnp.stack([r0[n[4:]], b], axis=1) for n, b in zip(names_of("ssd"), r3)}
    recvd.update({n: a[:, None] for n, a in zip(names_of("mla"), r1)})
    recvd.update({n: a[:, None] for n, a in r2.items()})
    recvd = [recvd[n] for n in SHARDED] + [rsmall]

    grads, deltas, new_m, new_v = {}, {}, {}, {}
    for n, pt in zip(SHARDED, recvd[:-1]):
        shp = w[n].shape
        r2d = (-1, shp[-1])
        outs = _adamw(w[n].reshape(r2d), m[n].reshape(r2d), v[n].reshape(r2d), pt.reshape((N_DEV,) + w[n].reshape(r2d).shape),
                      name="adamw_" + n)
        grads[n], deltas[n], new_m[n], new_v[n] = (o.reshape(shp) for o in outs)
    outs = _adamw(_pack_small(w), _pack_small(m), _pack_small(v), recvd[-1], name="adamw_replicated")
    for dst, o in zip((grads, deltas, new_m, new_v), outs):
        dst.update(_unpack_small(o, w))

    loss = lax.psum(loss_tile[0, 0], ("x", "y", "c"))
    return (loss, grad_x, *[grads[n] for n in WEIGHTS], *[deltas[n] for n in WEIGHTS],
            *[new_m[n] for n in WEIGHTS], *[new_v[n] for n in WEIGHTS])
```

```python
import functools

import jax
import jax.numpy as jnp
from jax import lax
from jax.experimental import pallas as pl
from jax.experimental.pallas import tpu as pltpu

F32 = jnp.float32
MXU = jnp.bfloat16
WIRE = jnp.bfloat16
HI = lax.Precision.HIGHEST

N_DEV = 8
LANES = 128
VMEM_LIMIT = 56 * 1024 * 1024

D_MODEL = 2048
DEPTH = 4
ALPHA = (2.0 * DEPTH) ** 0.25
LN_EPS = 1e-5
RMS_EPS = 1e-6

SSD_DI = 4096
SSD_P = 64
SSD_H = 64
SSD_G = 8
SSD_N = 128
SSD_L = 128
SSD_GS = SSD_DI // SSD_G
SSD_CONV_DIM = SSD_DI + 2 * SSD_G * SSD_N
SSD_PROJ_PAD = SSD_DI + SSD_CONV_DIM + LANES

MLA_H = 16
MLA_QR = 768
MLA_KVR = 512
MLA_NOPE = 128
MLA_ROPE = 64
MLA_V = 128
MLA_GATE = MLA_H * MLA_V
MLA_PROJ_PAD = MLA_QR + MLA_KVR + LANES + MLA_GATE
MLA_SCALE = (MLA_NOPE + MLA_ROPE) ** -0.5
ROPE_THETA = 10000.0
ATT_BLK = 512

GDN_HK = 16
GDN_HV = 32
GDN_DK = 128
GDN_DV = 128
GDN_KEY = GDN_HK * GDN_DK
GDN_VAL = GDN_HV * GDN_DV
GDN_L = 128
GDN_CONV_DIM = 2 * GDN_KEY + GDN_VAL
GDN_PROJ_PAD = 12800

ADAM_LR = 0.001
ADAM_B1 = 0.9
ADAM_B2 = 0.999
ADAM_EPS = 1e-08
ADAM_WD = 0.01
ADAM_STEP = 10


def _cparams(sem=None):
    return pltpu.CompilerParams(dimension_semantics=sem, vmem_limit_bytes=VMEM_LIMIT)


def _tile(n, cap):
    if n <= cap:
        return n
    best = None
    for d in range(LANES, cap + 1, LANES):
        if n % d == 0:
            best = d
    assert best is not None, (n, cap)
    return best


def _dg(a, b, ca, cb, prec=None):
    return lax.dot_general(a, b, (((ca,), (cb,)), ((), ())), preferred_element_type=F32, precision=prec)


def _mx(a):
    return a.astype(MXU)


@jax.custom_vjp
def _nn(a, b):
    return _dg(_mx(a), _mx(b), 1, 0)


def _nn_f(a, b):
    return _nn(a, b), (a, b)


def _nn_b(res, ct):
    a, b = res
    return _dg(_mx(ct), _mx(b), 1, 1), _dg(_mx(a), _mx(ct), 0, 0)


_nn.defvjp(_nn_f, _nn_b)


@jax.custom_vjp
def _nt(a, b):
    return _dg(_mx(a), _mx(b), 1, 1)


def _nt_f(a, b):
    return _nt(a, b), (a, b)


def _nt_b(res, ct):
    a, b = res
    return _dg(_mx(ct), _mx(b), 1, 0), _dg(_mx(ct), _mx(a), 0, 0)


_nt.defvjp(_nt_f, _nt_b)


@jax.custom_vjp
def _tn(a, b):
    return _dg(_mx(a), _mx(b), 0, 0)


def _tn_f(a, b):
    return _tn(a, b), (a, b)


def _tn_b(res, ct):
    a, b = res
    return _dg(_mx(b), _mx(ct), 1, 1), _dg(_mx(a), _mx(ct), 1, 0)


_tn.defvjp(_tn_f, _tn_b)


def _softplus(x):
    return jnp.maximum(x, 0.0) + jnp.log(1.0 + jnp.exp(-jnp.abs(x)))


def _silu(x):
    return x * jax.nn.sigmoid(x)


MM_TM = 1024
MM_TN = 1280
MM_VMEM_BUDGET = 40 * 1024 * 1024


def _mm(a, b, *, ta=False, tb=False, out_dtype=F32, name, ride=None):
    if ta:
        kdim, m = a.shape
    else:
        m, kdim = a.shape
    if tb:
        n, kb = b.shape
    else:
        kb, n = b.shape
    assert kdim == kb, (a.shape, b.shape, ta, tb)
    tm, tn = _tile(m, MM_TM), _tile(n, MM_TN)
    abytes, bbytes, obytes = a.dtype.itemsize, b.dtype.itemsize, jnp.dtype(out_dtype).itemsize
    tk = LANES
    for d in range(LANES, kdim + 1, LANES):
        if kdim % d == 0 and 2 * d * (tm * abytes + tn * bbytes) + tm * tn * (2 * obytes + 4) <= MM_VMEM_BUDGET:
            tk = d
    nk = kdim // tk
    ca, cb = (0 if ta else 1), (1 if tb else 0)

    def body(a_ref, b_ref, o_ref, *acc):
        part = _dg(_mx(a_ref[...]), _mx(b_ref[...]), ca, cb)
        if nk == 1:
            o_ref[...] = part.astype(out_dtype)
            return
        acc_ref, = acc
        k = pl.program_id(2)

        @pl.when(k == 0)
        def _():
            acc_ref[...] = part

        @pl.when(k > 0)
        def _():
            acc_ref[...] += part

        @pl.when(k == nk - 1)
        def _():
            o_ref[...] = acc_ref[...].astype(out_dtype)

    a_spec = pl.BlockSpec((tk, tm), lambda i, j, k: (k, i)) if ta else pl.BlockSpec((tm, tk), lambda i, j, k: (i, k))
    b_spec = pl.BlockSpec((tn, tk), lambda i, j, k: (j, k)) if tb else pl.BlockSpec((tk, tn), lambda i, j, k: (k, j))
    (out,), rode = _hosted_call(
        body, name=name, grid=(m // tm, n // tn, nk),
        in_specs=[a_spec, b_spec], out_specs=[pl.BlockSpec((tm, tn), lambda i, j, k: (i, j))],
        out_shape=[jax.ShapeDtypeStruct((m, n), out_dtype)],
        scratch_shapes=[pltpu.VMEM((tm, tn), F32)] if nk > 1 else [],
        sem=("parallel", "parallel", "arbitrary"), args=(a, b), ride=ride)
    return out if ride is None else (out, rode)


def _ln_fwd(h, y, g, b, name):
    t, d = h.shape
    tr = _tile(t, 256)

    def body(h_ref, y_ref, g_ref, b_ref, o_ref):
        r = ALPHA * h_ref[...] + y_ref[...]
        mu = jnp.mean(r, -1, keepdims=True)
        xc = r - mu
        var = jnp.mean(xc * xc, -1, keepdims=True)
        o_ref[...] = xc * lax.rsqrt(var + LN_EPS) * g_ref[...] + b_ref[...]

    row = pl.BlockSpec((tr, d), lambda i: (i, 0))
    par = pl.BlockSpec((1, d), lambda i: (0, 0))
    return pl.pallas_call(
        body, name=name, grid=(t // tr,), in_specs=[row, row, par, par], out_specs=row,
        out_shape=jax.ShapeDtypeStruct((t, d), F32), compiler_params=_cparams(("parallel",)),
    )(h, y, g, b)


def _ln_bwd(h, y, g, dr_up, du_up, name):
    t, d = h.shape
    tr = _tile(t, 256)

    def body(h_ref, y_ref, g_ref, dr_ref, du_ref, o_ref, dg_ref, db_ref):
        i = pl.program_id(0)

        @pl.when(i == 0)
        def _():
            dg_ref[...] = jnp.zeros_like(dg_ref)
            db_ref[...] = jnp.zeros_like(db_ref)

        dout = ALPHA * dr_ref[...] + du_ref[...]
        r = ALPHA * h_ref[...] + y_ref[...]
        mu = jnp.mean(r, -1, keepdims=True)
        xc = r - mu
        rstd = lax.rsqrt(jnp.mean(xc * xc, -1, keepdims=True) + LN_EPS)
        xh = xc * rstd
        dxh = dout * g_ref[...]
        o_ref[...] = rstd * (dxh - jnp.mean(dxh, -1, keepdims=True) - xh * jnp.mean(dxh * xh, -1, keepdims=True))
        dg_ref[...] += jnp.sum(dout * xh, 0, keepdims=True)
        db_ref[...] += jnp.sum(dout, 0, keepdims=True)

    row = pl.BlockSpec((tr, d), lambda i: (i, 0))
    par = pl.BlockSpec((1, d), lambda i: (0, 0))
    return pl.pallas_call(
        body, name=name, grid=(t // tr,), in_specs=[row, row, par, row, row], out_specs=[row, par, par],
        out_shape=[jax.ShapeDtypeStruct((t, d), F32), jax.ShapeDtypeStruct((1, d), F32), jax.ShapeDtypeStruct((1, d), F32)],
        compiler_params=_cparams(("arbitrary",)),
    )(h, y, g, dr_up, du_up)


def _loss_head(h, tgt, name):
    t, d = h.shape
    tr = _tile(t, 256)

    def body(h_ref, t_ref, l_ref, d_ref):
        i = pl.program_id(0)

        @pl.when(i == 0)
        def _():
            l_ref[...] = jnp.zeros_like(l_ref)

        e = h_ref[...] - t_ref[...]
        d_ref[...] = e * (1.0 / d)
        l_ref[...] += 0.5 * jnp.sum(jnp.mean(e * e, -1, keepdims=True))

    row = pl.BlockSpec((tr, d), lambda i: (i, 0))
    return pl.pallas_call(
        body, name=name, grid=(t // tr,), in_specs=[row, row],
        out_specs=[pl.BlockSpec((8, LANES), lambda i: (0, 0)), row],
        out_shape=[jax.ShapeDtypeStruct((8, LANES), F32), jax.ShapeDtypeStruct((t, d), F32)],
        compiler_params=_cparams(("arbitrary",)),
    )(h, tgt)


def _axpy(dr, du, name):
    t, d = dr.shape
    tr = _tile(t, 256)

    def body(a_ref, b_ref, o_ref):
        o_ref[...] = ALPHA * a_ref[...] + b_ref[...]

    row = pl.BlockSpec((tr, d), lambda i: (i, 0))
    return pl.pallas_call(
        body, name=name, grid=(t // tr,), in_specs=[row, row], out_specs=row,
        out_shape=jax.ShapeDtypeStruct((t, d), F32), compiler_params=_cparams(("parallel",)),
    )(dr, du)


CONV_TT = 512
CONV_TC = 512


HALO = 8


def _shift_down(cur, halo, s, row):
    if s == 0:
        return cur
    tt = cur.shape[0]
    edge = pltpu.roll(halo, s, 0)
    if tt > HALO:
        edge = jnp.concatenate([edge, jnp.zeros((tt - HALO, cur.shape[1]), cur.dtype)], axis=0)
    return jnp.where(row >= s, pltpu.roll(cur, s, 0), edge)


def _shift_up(cur, halo, s, row, tt):
    if s == 0:
        return cur
    edge = jnp.concatenate([jnp.zeros((tt - HALO, cur.shape[1]), cur.dtype), pltpu.roll(halo, HALO - s, 0)], axis=0)
    return jnp.where(row < tt - s, pltpu.roll(cur, tt - s, 0), edge)


def _conv_fwd(proj, col0, w, b, name, ride=None):
    t = proj.shape[0]
    c = w.shape[1]
    tt = _tile(t, CONV_TT)
    cb0 = col0 // CONV_TC

    def body(x_ref, p_ref, w_ref, b_ref, o_ref):
        i = pl.program_id(1)
        x = x_ref[...]
        p = jnp.where(i > 0, p_ref[...], 0.0)
        row = lax.broadcasted_iota(jnp.int32, x.shape, 0)
        pre = b_ref[...] + w_ref[3:4, :] * x
        for s in (1, 2, 3):
            pre = pre + w_ref[3 - s:4 - s, :] * _shift_down(x, p, s, row)
        o_ref[...] = _silu(pre)

    (act,), rode = _hosted_call(
        body, name=name, grid=(c // CONV_TC, t // tt),
        in_specs=[pl.BlockSpec((tt, CONV_TC), lambda j, i: (i, cb0 + j)),
                  pl.BlockSpec((HALO, CONV_TC), lambda j, i: (jnp.maximum(i * (tt // HALO) - 1, 0), cb0 + j)),
                  pl.BlockSpec((4, CONV_TC), lambda j, i: (0, j)),
                  pl.BlockSpec((1, CONV_TC), lambda j, i: (0, j))],
        out_specs=[pl.BlockSpec((tt, CONV_TC), lambda j, i: (i, j))],
        out_shape=[jax.ShapeDtypeStruct((t, c), F32)], scratch_shapes=[],
        sem=("parallel", "parallel"), args=(proj, proj, w, b), ride=ride)
    return act, rode


def _conv_bwd(proj, col0, w, b, dact, name):
    t = proj.shape[0]
    c = w.shape[1]
    tt = _tile(t, CONV_TT)
    nt = t // tt
    cb0 = col0 // CONV_TC
    per = tt // HALO

    def dpre_of(x, halo, d, w_ref, b_ref):
        row = lax.broadcasted_iota(jnp.int32, x.shape, 0)
        sh = [_shift_down(x, halo, s, row) for s in range(4)]
        pre = b_ref[...] + w_ref[3:4, :] * sh[0]
        for s in (1, 2, 3):
            pre = pre + w_ref[3 - s:4 - s, :] * sh[s]
        sg = jax.nn.sigmoid(pre)
        return d * (sg * (1.0 + pre * (1.0 - sg))), sh

    def body(x_ref, p_ref, n_ref, w_ref, b_ref, d_ref, dn_ref, dx_ref, dw_ref, db_ref):
        i = pl.program_id(1)

        @pl.when(i == 0)
        def _():
            dw_ref[...] = jnp.zeros_like(dw_ref)
            db_ref[...] = jnp.zeros_like(db_ref)

        x = x_ref[...]
        dpre, sh = dpre_of(x, jnp.where(i > 0, p_ref[...], 0.0), d_ref[...], w_ref, b_ref)
        dnext, _ = dpre_of(n_ref[...], x[tt - HALO:, :], jnp.where(i < nt - 1, dn_ref[...], 0.0), w_ref, b_ref)
        row = lax.broadcasted_iota(jnp.int32, x.shape, 0)
        acc = w_ref[3:4, :] * dpre
        for s in (1, 2, 3):
            acc = acc + w_ref[3 - s:4 - s, :] * _shift_up(dpre, dnext, s, row, tt)
        dx_ref[...] = acc
        for s in range(4):
            dw_ref[3 - s:4 - s, :] += jnp.sum(dpre * sh[s], 0, keepdims=True)
        db_ref[...] += jnp.sum(dpre, 0, keepdims=True)

    tile = lambda off: pl.BlockSpec((tt, CONV_TC), lambda j, i: (i, off + j))
    before = lambda off: pl.BlockSpec((HALO, CONV_TC), lambda j, i: (jnp.maximum(i * per - 1, 0), off + j))
    after = lambda off: pl.BlockSpec((HALO, CONV_TC), lambda j, i: (jnp.minimum((i + 1) * per, t // HALO - 1), off + j))
    wspec = lambda rows: pl.BlockSpec((rows, CONV_TC), lambda j, i: (0, j))
    return pl.pallas_call(
        body, name=name, grid=(c // CONV_TC, nt),
        in_specs=[tile(cb0), before(cb0), after(cb0), wspec(4), wspec(1), tile(0), after(0)],
        out_specs=[tile(0), wspec(4), wspec(1)],
        out_shape=[jax.ShapeDtypeStruct((t, c), F32), jax.ShapeDtypeStruct((4, c), F32), jax.ShapeDtypeStruct((1, c), F32)],
        compiler_params=_cparams(("parallel", "arbitrary")),
    )(proj, proj, proj, w, b, dact, dact)


SSD_GB = 4


def _split3(a):
    a1 = _mx(a)
    r = a - a1.astype(F32)
    a2 = _mx(r)
    return a1, a2, _mx(r - a2.astype(F32))


@jax.custom_vjp
def _sel_r(a, c):
    cm = _mx(c)
    p1, p2, p3 = _split3(a)
    return _dg(p1, cm, 1, 0) + (_dg(p2, cm, 1, 0) + _dg(p3, cm, 1, 0))


def _sel_r_f(a, c):
    return _sel_r(a, c), c


def _sel_r_b(c, ct):
    cm = _mx(c)
    p1, p2, p3 = _split3(ct)
    return _dg(p1, cm, 1, 1) + (_dg(p2, cm, 1, 1) + _dg(p3, cm, 1, 1)), jnp.zeros_like(c)


_sel_r.defvjp(_sel_r_f, _sel_r_b)


@jax.custom_vjp
def _sel_l(c, a):
    cm = _mx(c)
    p1, p2, p3 = _split3(a)
    return _dg(cm, p1, 1, 0) + (_dg(cm, p2, 1, 0) + _dg(cm, p3, 1, 0))


def _sel_l_f(c, a):
    return _sel_l(c, a), c


def _sel_l_b(c, ct):
    cm = _mx(c)
    p1, p2, p3 = _split3(ct)
    return jnp.zeros_like(c), _dg(cm, p1, 0, 0) + (_dg(cm, p2, 0, 0) + _dg(cm, p3, 0, 0))


_sel_l.defvjp(_sel_l_f, _sel_l_b)


def _ssd_chunk(gb, x, z, bm, cm, dtraw, dtb, alog, dsk, nw, prev):
    L = x.shape[1]
    r_i = lax.broadcasted_iota(jnp.int32, (L, L), 0)
    c_i = lax.broadcasted_iota(jnp.int32, (L, L), 1)
    causal = r_i >= c_i
    dt = _softplus(dtraw + dtb)
    a = dt * (-jnp.exp(alog))
    acs = _sel_l(causal.astype(F32), a)
    e_r = lax.broadcasted_iota(jnp.int32, (LANES, SSD_GS), 0)
    e_c = lax.broadcasted_iota(jnp.int32, (LANES, SSD_GS), 1)
    hpg = SSD_H // SSD_G
    sels = [(e_r == (gb * SSD_GB + i) * hpg + jnp.right_shift(e_c, 6)).astype(F32) for i in range(SSD_GB)]
    dt_x = _cat0([_sel_r(dt, s) for s in sels])
    acs_x = _cat0([_sel_r(acs, s) for s in sels])
    d_x = _cat0([_sel_r(jnp.broadcast_to(dsk, (8, LANES)), s)[0:1] for s in sels])
    last = lax.broadcasted_iota(jnp.int32, (L, 1), 0) == L - 1
    alast = jnp.sum(jnp.where(last, acs_x, 0.0), axis=1, keepdims=True)
    xdt = x * dt_x
    cb = _bnt(cm, bm)
    lane = lax.broadcasted_iota(jnp.int32, (L, LANES), 1)
    ys = []
    for j in range(SSD_GS // LANES):
        xp = xdt[:, :, j * LANES:(j + 1) * LANES]
        yp = None
        for hh in range(2):
            c0 = (2 * j + hh) * SSD_P
            cmx = jnp.broadcast_to(acs_x[:, :, c0:c0 + 1], (SSD_GB, L, L))
            dec = jnp.exp(jnp.where(causal, cmx - jnp.swapaxes(cmx, 1, 2), -jnp.inf))
            half = (lane < SSD_P) if hh == 0 else (lane >= SSD_P)
            t = _bnn(cb * dec, jnp.where(half, xp, 0.0))
            yp = t if yp is None else yp + t
        ys.append(yp)
    y_diag = jnp.concatenate(ys, axis=2)
    st = _btn(bm, xdt * jnp.exp(alast - acs_x))
    new = prev * jnp.exp(alast) + st
    y_off = _bnn(cm, prev) * jnp.exp(acs_x)
    y = y_diag + y_off + x * d_x
    yg = y * _silu(z)
    yn = yg * lax.rsqrt(jnp.mean(yg * yg, -1, keepdims=True) + RMS_EPS) * nw
    return yn, new


def _groups_of(ref, width):
    return _cat0([ref[:, i * width:(i + 1) * width] for i in range(SSD_GB)])


def _put_groups(ref, val, width):
    for i in range(SSD_GB):
        ref[:, i * width:(i + 1) * width] = val[i].astype(ref.dtype)


def _ssd_specs(nc, rev):
    cc = (lambda c: nc - 1 - c) if rev else (lambda c: c)
    wx, wb = SSD_GB * SSD_GS, SSD_GB * SSD_N
    dtb = (SSD_DI + SSD_CONV_DIM) // LANES
    bb = SSD_DI // wb
    cbk = (SSD_DI + SSD_G * SSD_N) // wb
    par = pl.BlockSpec((1, LANES), lambda c, g: (0, 0))
    return dict(
        z=pl.BlockSpec((SSD_L, wx), lambda c, g: (cc(c), g)),
        dt=pl.BlockSpec((SSD_L, LANES), lambda c, g: (cc(c), dtb)),
        x=pl.BlockSpec((SSD_L, wx), lambda c, g: (cc(c), g)),
        bm=pl.BlockSpec((SSD_L, wb), lambda c, g: (cc(c), bb + g)),
        cm=pl.BlockSpec((SSD_L, wb), lambda c, g: (cc(c), cbk + g)),
        par=par,
        nw=pl.BlockSpec((1, wx), lambda c, g: (0, g)),
        st=pl.BlockSpec((1, SSD_GB, SSD_N, SSD_GS), lambda c, g: (cc(c), g, 0, 0)),
        y=pl.BlockSpec((SSD_L, wx), lambda c, g: (cc(c), g)),
        bc=pl.BlockSpec((SSD_L, wb), lambda c, g: (cc(c), g)),
        dtout=pl.BlockSpec((SSD_L, LANES), lambda c, g: (cc(c), 0)),
    )


def _ssd_fwd(proj, act, dtb, alog, dsk, nw, name, ride=None):
    t = proj.shape[0]
    nc = t // SSD_L
    sp = _ssd_specs(nc, False)

    def body(z_ref, dt_ref, x_ref, bm_ref, cm_ref, dtb_ref, alog_ref, dsk_ref, nw_ref, y_ref, st_ref, state):
        c, g = pl.program_id(0), pl.program_id(1)
        mine = pl.ds(g * SSD_GB, SSD_GB)

        @pl.when(c == 0)
        def _():
            state[mine] = jnp.zeros((SSD_GB, SSD_N, SSD_GS), F32)

        prev = state[mine]
        st_ref[0] = prev
        yn, new = _ssd_chunk(g, _groups_of(x_ref, SSD_GS), _groups_of(z_ref, SSD_GS), _groups_of(bm_ref, SSD_N),
                             _groups_of(cm_ref, SSD_N), dt_ref[...], dtb_ref[...], alog_ref[...], dsk_ref[...],
                             _groups_of(nw_ref, SSD_GS), prev)
        _put_groups(y_ref, yn, SSD_GS)
        state[mine] = new

    return _hosted_call(
        body, name=name, grid=(nc, SSD_G // SSD_GB),
        in_specs=[sp["z"], sp["dt"], sp["x"], sp["bm"], sp["cm"], sp["par"], sp["par"], sp["par"], sp["nw"]],
        out_specs=[sp["y"], sp["st"]],
        out_shape=[jax.ShapeDtypeStruct((t, SSD_DI), MXU), jax.ShapeDtypeStruct((nc, SSD_G, SSD_N, SSD_GS), F32)],
        scratch_shapes=[pltpu.VMEM((SSD_G, SSD_N, SSD_GS), F32)],
        sem=("arbitrary", "arbitrary"), args=(proj, proj, act, act, act, dtb, alog, dsk, nw), ride=ride)


def _ssd_bwd(proj, act, dtb, alog, dsk, nw, states, dyn, name, ride=None):
    t = proj.shape[0]
    nc = t // SSD_L
    sp = _ssd_specs(nc, True)

    def body(z_ref, dt_ref, x_ref, bm_ref, cm_ref, dtb_ref, alog_ref, dsk_ref, nw_ref, st_ref, dy_ref,
             dact_ref, dz_ref, ddt_ref, ddtb_ref, dalog_ref, ddsk_ref, dnw_ref, dstate):
        c, g = pl.program_id(0), pl.program_id(1)
        mine = pl.ds(g * SSD_GB, SSD_GB)

        @pl.when(c == 0)
        def _():
            dstate[mine] = jnp.zeros((SSD_GB, SSD_N, SSD_GS), F32)

        @pl.when((c == 0) & (g == 0))
        def _():
            ddtb_ref[...] = jnp.zeros_like(ddtb_ref)
            dalog_ref[...] = jnp.zeros_like(dalog_ref)
            ddsk_ref[...] = jnp.zeros_like(ddsk_ref)
            dnw_ref[...] = jnp.zeros_like(dnw_ref)

        @pl.when(g == 0)
        def _():
            ddt_ref[...] = jnp.zeros_like(ddt_ref)

        _, vjp = jax.vjp(functools.partial(_ssd_chunk, g), _groups_of(x_ref, SSD_GS), _groups_of(z_ref, SSD_GS),
                         _groups_of(bm_ref, SSD_N), _groups_of(cm_ref, SSD_N), dt_ref[...], dtb_ref[...], alog_ref[...],
                         dsk_ref[...], _groups_of(nw_ref, SSD_GS), st_ref[0])
        dx, dz, dbm, dcm, ddt, ddtb, dalog, ddsk, dnw, dprev = vjp((_groups_of(dy_ref, SSD_GS), dstate[mine]))
        _put_groups(dz_ref, dz, SSD_GS)
        for gi in range(SSD_G // SSD_GB):
            @pl.when(g == gi)
            def _(gi=gi):
                for i in range(SSD_GB):
                    gg = gi * SSD_GB + i
                    dact_ref[:, gg * SSD_GS:(gg + 1) * SSD_GS] = dx[i]
                    dact_ref[:, SSD_DI + gg * SSD_N:SSD_DI + (gg + 1) * SSD_N] = dbm[i]
                    dact_ref[:, SSD_DI + (SSD_G + gg) * SSD_N:SSD_DI + (SSD_G + gg + 1) * SSD_N] = dcm[i]
        ddt_ref[...] += ddt
        ddtb_ref[...] += ddtb
        dalog_ref[...] += dalog
        ddsk_ref[...] += ddsk
        dnw_ref[mine] += dnw
        dstate[mine] = dprev

    par_out = pl.BlockSpec((1, LANES), lambda c, g: (0, 0))
    sds = jax.ShapeDtypeStruct
    return _hosted_call(
        body, name=name, grid=(nc, SSD_G // SSD_GB),
        in_specs=[sp["z"], sp["dt"], sp["x"], sp["bm"], sp["cm"], sp["par"], sp["par"], sp["par"], sp["nw"],
                  sp["st"], sp["y"]],
        out_specs=[pl.BlockSpec((SSD_L, SSD_CONV_DIM), lambda c, g: (nc - 1 - c, 0)), sp["y"], sp["dtout"],
                   par_out, par_out, par_out, pl.BlockSpec((SSD_G, 1, SSD_GS), lambda c, g: (0, 0, 0))],
        out_shape=[sds((t, SSD_CONV_DIM), F32), sds((t, SSD_DI), F32), sds((t, LANES), F32), sds((1, LANES), F32),
                   sds((1, LANES), F32), sds((1, LANES), F32), sds((SSD_G, 1, SSD_GS), F32)],
        scratch_shapes=[pltpu.VMEM((SSD_G, SSD_N, SSD_GS), F32)],
        sem=("arbitrary", "arbitrary"), args=(proj, proj, act, act, act, dtb, alog, dsk, nw, states, dyn), ride=ride)


def _pad_lanes(v, width=LANES, offset=0):
    return jnp.pad(v.astype(F32), (offset, width - offset - v.shape[0])).reshape(1, width)


def _ride_of(rides, key, scatter):
    arrs = (rides or {}).get(key)
    return (arrs, scatter) if arrs else None


def _mm_r(a, b, ride, **kw):
    out = _mm(a, b, ride=ride, **kw)
    return (out, []) if ride is None else out


def _ssd_layer_fwd(u, w, ln_g, ln_b, tag, rides=None, late=None):
    w_in = jnp.concatenate([w["in_w"], jnp.zeros((D_MODEL, LANES - SSD_H), w["in_w"].dtype)], axis=1)
    dtb, alog, dsk = _pad_lanes(w["dt_bias"]), _pad_lanes(w["a_log"]), _pad_lanes(w["d"])
    if late is None:
        proj = _mm(u.astype(MXU), w_in, name=tag + "_in")
    else:
        proj, got = _mm(u.astype(MXU), w_in, name=tag + "_in", ride=(late[0], False))
        w.update(late[1](got))
    nw = w["norm_w"].reshape(1, SSD_DI)
    cb = w["conv_b"].reshape(1, SSD_CONV_DIM)
    rode = {}
    act, rode["conv"] = _conv_fwd(proj, SSD_DI, w["conv_w"], cb, name=tag + "_conv", ride=_ride_of(rides, "conv", False))
    (yn, states), rode["scan"] = _ssd_fwd(proj, act, dtb, alog, dsk, nw, name=tag + "_scan", ride=_ride_of(rides, "scan", False))
    y, rode["out"] = _mm_r(yn, w["out_w"], _ride_of(rides, "out", False), name=tag + "_out")
    h = _ln_fwd(u, y, ln_g, ln_b, name=tag + "_ln")
    saved = dict(u=u, w_in=w_in, proj=proj, act=act, states=states, yn=yn, y=y, dtb=dtb, alog=alog, dsk=dsk, nw=nw, cb=cb)
    return h, saved, rode


def _ssd_layer_bwd(s, w, ln_g, dr_up, du_up, tag, rides=None, own=None):
    dr, dg, db = _ln_bwd(s["u"], s["y"], ln_g, dr_up, du_up, name=tag + "_ln_b")
    dyn = _mm(dr, w["out_w"], tb=True, name=tag + "_out_bx")
    d_out_w = _mm(s["yn"], dr, ta=True, out_dtype=WIRE, name=tag + "_out_bw")
    rode = {}
    (dact, dz, ddt, ddtb, dalog, ddsk, dnw), rode["scan_b"] = _ssd_bwd(
        s["proj"], s["act"], s["dtb"], s["alog"], s["dsk"], s["nw"], s["states"], dyn, name=tag + "_scan_b",
        ride=_ride_of(rides, "scan_b", True))
    dxbc, d_conv_w, d_conv_b = _conv_bwd(s["proj"], SSD_DI, w["conv_w"], s["cb"], dact, name=tag + "_conv_b")
    dproj = jnp.concatenate([dz, dxbc, ddt], axis=1).astype(MXU)
    grads = dict(conv_w=d_conv_w, conv_b=d_conv_b.reshape(-1), dt_bias=ddtb[0, :SSD_H], a_log=dalog[0, :SSD_H],
                 d=ddsk[0, :SSD_H], norm_w=dnw.reshape(-1), out_w=d_out_w, ln_g=dg[0], ln_b=db[0])
    late_names = ("conv_w", "conv_b", "norm_w", "out_w")
    got = {}
    if own is None:
        d_in_w = _mm(s["u"].astype(MXU), dproj, ta=True, out_dtype=WIRE, name=tag + "_in_bw")
        du = _mm(dproj, s["w_in"], tb=True, name=tag + "_in_bx")
        grads["in_w"] = d_in_w[:, :SSD_DI + SSD_CONV_DIM + SSD_H]
    else:
        d_in_w, late = _mm(s["u"].astype(MXU), dproj, ta=True, out_dtype=WIRE, name=tag + "_in_bw",
                           ride=([own(k, grads[k]) for k in late_names], True))
        got = dict(zip(late_names, late))
        grads["in_w"] = d_in_w[:, :SSD_DI + SSD_CONV_DIM + SSD_H]
        du, (got["in_w"],) = _mm(dproj, s["w_in"], tb=True, name=tag + "_in_bx", ride=([own("in_w", grads["in_w"])], True))
    return dr, du, grads, rode, got


MLA_LOW = MLA_QR + MLA_KVR + LANES
MLA_ZB = MLA_LOW // LANES


def _rope_mat():
    r = lax.broadcasted_iota(jnp.int32, (LANES, LANES), 0)
    c = lax.broadcasted_iota(jnp.int32, (LANES, LANES), 1)
    hf = MLA_ROPE // 2
    return jnp.where((c < hf) & (r == c + hf), -1.0, 0.0) + jnp.where((c >= hf) & (c < 2 * hf) & (r == c - hf), 1.0, 0.0)


def _rope(x, cosf, sinf):
    return x * cosf + _sel_r(x, _rope_mat()) * sinf


def _rope_adj(d, cosf, sinf):
    return d * cosf - _sel_r(d * sinf, _rope_mat())


def _mla_low_fn(low, qnw, kvnw, cosf, sinf):
    qc, kvc, kr = low[:, :MLA_QR], low[:, MLA_QR:MLA_QR + MLA_KVR], low[:, MLA_QR + MLA_KVR:]
    qn = qc * lax.rsqrt(jnp.mean(qc * qc, -1, keepdims=True) + RMS_EPS) * qnw
    kvn = kvc * lax.rsqrt(jnp.mean(kvc * kvc, -1, keepdims=True) + RMS_EPS) * kvnw
    return qn, kvn, _rope(kr, cosf, sinf)


def _mla_low_fwd(proj, qnw, kvnw, cosf, sinf, name):
    t = proj.shape[0]
    tr = _tile(t, 256)

    def body(low_ref, qnw_ref, kvnw_ref, cos_ref, sin_ref, qn_ref, kvn_ref, kr_ref):
        qn, kvn, kr = _mla_low_fn(low_ref[...], qnw_ref[...], kvnw_ref[...], cos_ref[...], sin_ref[...])
        qn_ref[...] = qn
        kvn_ref[...] = kvn
        kr_ref[...] = kr

    row = lambda wdt: pl.BlockSpec((tr, wdt), lambda i: (i, 0))
    par = lambda wdt: pl.BlockSpec((1, wdt), lambda i: (0, 0))
    sds = jax.ShapeDtypeStruct
    return pl.pallas_call(
        body, name=name, grid=(t // tr,),
        in_specs=[row(MLA_LOW), par(MLA_QR), par(MLA_KVR), row(LANES), row(LANES)],
        out_specs=[row(MLA_QR), row(MLA_KVR), row(LANES)],
        out_shape=[sds((t, MLA_QR), F32), sds((t, MLA_KVR), F32), sds((t, LANES), F32)],
        compiler_params=_cparams(("parallel",)),
    )(proj, qnw, kvnw, cosf, sinf)


def _mla_low_bwd(proj, qnw, kvnw, cosf, sinf, dqn, dkvn, dkr, name):
    t = proj.shape[0]
    tr = _tile(t, 256)

    def body(low_ref, qnw_ref, kvnw_ref, cos_ref, sin_ref, dqn_ref, dkvn_ref, dkr_ref, dlow_ref, dqnw_ref, dkvnw_ref):
        i = pl.program_id(0)

        @pl.when(i == 0)
        def _():
            dqnw_ref[...] = jnp.zeros_like(dqnw_ref)
            dkvnw_ref[...] = jnp.zeros_like(dkvnw_ref)

        cosf, sinf = cos_ref[...], sin_ref[...]
        _, vjp = jax.vjp(lambda a, b, c: _mla_low_fn(a, b, c, cosf, sinf), low_ref[...], qnw_ref[...], kvnw_ref[...])
        dlow, dq, dk = vjp((dqn_ref[...], dkvn_ref[...], dkr_ref[...]))
        dlow_ref[...] = dlow
        dqnw_ref[...] += dq
        dkvnw_ref[...] += dk

    row = lambda wdt: pl.BlockSpec((tr, wdt), lambda i: (i, 0))
    par = lambda wdt: pl.BlockSpec((1, wdt), lambda i: (0, 0))
    sds = jax.ShapeDtypeStruct
    return pl.pallas_call(
        body, name=name, grid=(t // tr,),
        in_specs=[row(MLA_LOW), par(MLA_QR), par(MLA_KVR), row(LANES), row(LANES), row(MLA_QR), row(MLA_KVR), row(LANES)],
        out_specs=[row(MLA_LOW), par(MLA_QR), par(MLA_KVR)],
        out_shape=[sds((t, MLA_LOW), F32), sds((1, MLA_QR), F32), sds((1, MLA_KVR), F32)],
        compiler_params=_cparams(("arbitrary",)),
    )(proj, qnw, kvnw, cosf, sinf, dqn, dkvn, dkr)


def _rope_heads(x, col_blk0, cosf, sinf, adjoint, name):
    t = x.shape[0]
    tr = _tile(t, 512)
    hb = 4
    assert col_blk0 % hb == 0

    def body(x_ref, cos_ref, sin_ref, o_ref):
        f = _rope_adj if adjoint else _rope
        for b in range(hb):
            o_ref[:, b * LANES:(b + 1) * LANES] = f(x_ref[:, b * LANES:(b + 1) * LANES], cos_ref[...], sin_ref[...])

    tab = pl.BlockSpec((tr, LANES), lambda i, h: (i, 0))
    return pl.pallas_call(
        body, name=name, grid=(t // tr, MLA_H // hb),
        in_specs=[pl.BlockSpec((tr, hb * LANES), lambda i, h: (i, col_blk0 // hb + h)), tab, tab],
        out_specs=pl.BlockSpec((tr, hb * LANES), lambda i, h: (i, h)),
        out_shape=jax.ShapeDtypeStruct((t, MLA_H * LANES), F32), compiler_params=_cparams(("parallel", "parallel")),
    )(x, cosf, sinf)


ATT_HB = 4
ATT_W = ATT_HB * LANES


def _att_qk(qn_ref, qr_ref, kn_ref, kr_ref):
    q2 = jnp.concatenate([_heads_of(qn_ref, ATT_HB), _heads_of(qr_ref, ATT_HB)], axis=2)
    kr = kr_ref[...]
    k2 = jnp.concatenate([_heads_of(kn_ref, ATT_HB), jnp.broadcast_to(kr[None], (ATT_HB,) + kr.shape)], axis=2)
    return q2, k2


def _att_scores(q2, k2, masked):
    s = _bdg(_mx(q2 * MLA_SCALE), _mx(k2), 2, 2)
    if masked:
        tq, tk = s.shape[1:]
        s = jnp.where(lax.broadcasted_iota(jnp.int32, (tq, tk), 1) <= lax.broadcasted_iota(jnp.int32, (tq, tk), 0), s, -jnp.inf)
    return s


def _on_causal_blocks(q_blk, k_blk, step):
    @pl.when(k_blk < q_blk)
    def _():
        step(False)

    @pl.when(k_blk == q_blk)
    def _():
        step(True)


def _put_heads(ref, val):
    for b in range(val.shape[0]):
        ref[:, b * LANES:(b + 1) * LANES] = val[b].astype(ref.dtype)


def _att_ds(s, v_ref, o_ref, do_ref, lse_ref):
    do = _heads_of(do_ref, ATT_HB)
    p = jnp.exp(s - _heads_of(lse_ref, ATT_HB)[:, :, 0:1])
    dp = _bdg(_mx(do), _mx(_heads_of(v_ref, ATT_HB)), 2, 2)
    dl = jnp.sum(do * _heads_of(o_ref, ATT_HB), -1, keepdims=True)
    return _mx(p), _mx(p * (dp - dl) * MLA_SCALE), do


def _attn_fwd(q, qr, kv, kr, proj, name, ride=None):
    t = q.shape[0]
    tq = tk = _tile(t, ATT_BLK)
    nq = nk = t // tq

    def body(qn_ref, qr_ref, kn_ref, kr_ref, v_ref, *rest):
        z_refs, (o_ref, og_ref, lse_ref, m_s, l_s, acc_s) = rest[:ATT_HB], rest[ATT_HB:]
        i, j = pl.program_id(1), pl.program_id(2)

        @pl.when(j == 0)
        def _():
            m_s[...] = jnp.full_like(m_s, -jnp.inf)
            l_s[...] = jnp.zeros_like(l_s)
            acc_s[...] = jnp.zeros_like(acc_s)

        def step(masked):
            s = _att_scores(*_att_qk(qn_ref, qr_ref, kn_ref, kr_ref), masked)
            m_new = jnp.maximum(m_s[...], jnp.max(s, -1, keepdims=True))
            p = jnp.exp(s - m_new)
            corr = jnp.exp(m_s[...] - m_new)
            l_s[...] = corr * l_s[...] + jnp.sum(p, -1, keepdims=True)
            acc_s[...] = corr * acc_s[...] + _bdg(_mx(p), _mx(_heads_of(v_ref, ATT_HB)), 2, 1)
            m_s[...] = m_new

        _on_causal_blocks(i, j, step)

        @pl.when(j == nk - 1)
        def _():
            o = acc_s[...] / l_s[...]
            _put_heads(o_ref, o)
            lse = m_s[...] + jnp.log(l_s[...])
            for b in range(ATT_HB):
                og_ref[:, b * LANES:(b + 1) * LANES] = (o[b] * _silu(z_refs[b][...])).astype(og_ref.dtype)
                lse_ref[:, b * LANES:(b + 1) * LANES] = jnp.broadcast_to(lse[b], (tq, LANES))

    qs = lambda off: pl.BlockSpec((tq, ATT_W), lambda h, i, j: (i, off // ATT_HB + h))
    ks = lambda off: pl.BlockSpec((tk, ATT_W), lambda h, i, j: (jnp.minimum(j, i), off // ATT_HB + h))
    zs = [pl.BlockSpec((tq, LANES), functools.partial(lambda b, h, i, j: (i, MLA_ZB + h * ATT_HB + b), b)) for b in range(ATT_HB)]
    sds = jax.ShapeDtypeStruct
    return _hosted_call(
        body, name=name, grid=(MLA_H // ATT_HB, nq, nk),
        in_specs=[qs(0), qs(0), ks(0), pl.BlockSpec((tk, LANES), lambda h, i, j: (jnp.minimum(j, i), 0)), ks(MLA_H)] + zs,
        out_specs=[qs(0), qs(0), qs(0)],
        out_shape=[sds((t, MLA_GATE), F32), sds((t, MLA_GATE), MXU), sds((t, MLA_H * LANES), F32)],
        scratch_shapes=[pltpu.VMEM((ATT_HB, tq, 1), F32), pltpu.VMEM((ATT_HB, tq, 1), F32), pltpu.VMEM((ATT_HB, tq, LANES), F32)],
        sem=("parallel", "parallel", "arbitrary"), args=(q, qr, kv, kr, kv) + (proj,) * ATT_HB, ride=ride)


def _gate_bwd(dog, o, proj, name):
    t = o.shape[0]
    tr = _tile(t, 512)

    def body(d_ref, o_ref, z_ref, do_ref, dz_ref):
        z = z_ref[...]
        sg = jax.nn.sigmoid(z)
        d = d_ref[...]
        do_ref[...] = d * z * sg
        dz_ref[...] = d * o_ref[...] * (sg * (1.0 + z * (1.0 - sg)))

    blk = lambda off: pl.BlockSpec((tr, 512), lambda i, j: (i, off + j))
    assert MLA_LOW % 512 != 0 or True
    zspec = pl.BlockSpec((tr, LANES), lambda i, j: (i, MLA_ZB + j))
    b128 = pl.BlockSpec((tr, LANES), lambda i, j: (i, j))
    sds = jax.ShapeDtypeStruct
    return pl.pallas_call(
        body, name=name, grid=(t // tr, MLA_GATE // LANES),
        in_specs=[b128, b128, zspec], out_specs=[b128, b128],
        out_shape=[sds((t, MLA_GATE), F32), sds((t, MLA_GATE), F32)],
        compiler_params=_cparams(("parallel", "parallel")),
    )(dog, o, proj)


def _attn_bwd_q(q, qr, kv, kr, o, do, lse, name, ride=None):
    t = q.shape[0]
    tq = tk = _tile(t, ATT_BLK)
    nq = nk = t // tq

    def body(qn_ref, qr_ref, kn_ref, kr_ref, v_ref, o_ref, do_ref, lse_ref, dqn_ref, dqr_ref, an_s, ar_s):
        i, j = pl.program_id(1), pl.program_id(2)

        @pl.when(j == 0)
        def _():
            an_s[...] = jnp.zeros_like(an_s)
            ar_s[...] = jnp.zeros_like(ar_s)

        def step(masked):
            q2, k2 = _att_qk(qn_ref, qr_ref, kn_ref, kr_ref)
            ds = _att_ds(_att_scores(q2, k2, masked), v_ref, o_ref, do_ref, lse_ref)[1]
            dq2 = _bdg(ds, _mx(k2), 2, 1)
            an_s[...] += dq2[:, :, :LANES]
            ar_s[...] += dq2[:, :, LANES:]

        _on_causal_blocks(i, j, step)

        @pl.when(j == nk - 1)
        def _():
            _put_heads(dqn_ref, an_s[...])
            _put_heads(dqr_ref, ar_s[...])

    qs = lambda off: pl.BlockSpec((tq, ATT_W), lambda h, i, j: (i, off // ATT_HB + h))
    ks = lambda off: pl.BlockSpec((tk, ATT_W), lambda h, i, j: (jnp.minimum(j, i), off // ATT_HB + h))
    sds = jax.ShapeDtypeStruct
    return _hosted_call(
        body, name=name, grid=(MLA_H // ATT_HB, nq, nk),
        in_specs=[qs(0), qs(0), ks(0), pl.BlockSpec((tk, LANES), lambda h, i, j: (jnp.minimum(j, i), 0)), ks(MLA_H),
                  qs(0), qs(0), qs(0)],
        out_specs=[qs(0), qs(0)],
        out_shape=[sds((t, MLA_H * LANES), F32), sds((t, MLA_H * LANES), F32)],
        scratch_shapes=[pltpu.VMEM((ATT_HB, tq, LANES), F32), pltpu.VMEM((ATT_HB, tq, LANES), F32)],
        sem=("parallel", "parallel", "arbitrary"), args=(q, qr, kv, kr, kv, o, do, lse), ride=ride)


def _attn_bwd_kv(q, qr, kv, kr, o, do, lse, name, ride=None):
    t = q.shape[0]
    tq = tk = _tile(t, ATT_BLK)
    nq = nk = t // tq

    def body(qn_ref, qr_ref, kn_ref, kr_ref, v_ref, o_ref, do_ref, lse_ref, dkn_ref, dv_ref, dkr_ref, akn_s, av_s):
        j, h, i = pl.program_id(0), pl.program_id(1), pl.program_id(2)

        @pl.when((h == 0) & (i == 0))
        def _():
            dkr_ref[...] = jnp.zeros_like(dkr_ref)

        @pl.when(i == 0)
        def _():
            akn_s[...] = jnp.zeros_like(akn_s)
            av_s[...] = jnp.zeros_like(av_s)

        def step(masked):
            q2, k2 = _att_qk(qn_ref, qr_ref, kn_ref, kr_ref)
            p, ds, do = _att_ds(_att_scores(q2, k2, masked), v_ref, o_ref, do_ref, lse_ref)
            av_s[...] += _bdg(p, _mx(do), 1, 1)
            dk2 = _bdg(ds, _mx(q2), 1, 1)
            akn_s[...] += dk2[:, :, :LANES]
            dkr_ref[...] += jnp.sum(dk2[:, :, LANES:], axis=0)

        _on_causal_blocks(i, j, step)

        @pl.when(i == nq - 1)
        def _():
            _put_heads(dkn_ref, akn_s[...])
            _put_heads(dv_ref, av_s[...])

    qs = lambda off: pl.BlockSpec((tq, ATT_W), lambda j, h, i: (jnp.maximum(i, j), off // ATT_HB + h))
    ks = lambda off: pl.BlockSpec((tk, ATT_W), lambda j, h, i: (j, off // ATT_HB + h))
    sds = jax.ShapeDtypeStruct
    return _hosted_call(
        body, name=name, grid=(nk, MLA_H // ATT_HB, nq),
        in_specs=[qs(0), qs(0), ks(0), pl.BlockSpec((tk, LANES), lambda j, h, i: (j, 0)), ks(MLA_H), qs(0), qs(0), qs(0)],
        out_specs=[ks(0), ks(0), pl.BlockSpec((tk, LANES), lambda j, h, i: (j, 0))],
        out_shape=[sds((t, MLA_H * LANES), F32), sds((t, MLA_H * LANES), F32), sds((t, LANES), F32)],
        scratch_shapes=[pltpu.VMEM((ATT_HB, tk, LANES), F32), pltpu.VMEM((ATT_HB, tk, LANES), F32)],
        sem=("parallel", "arbitrary", "arbitrary"), args=(q, qr, kv, kr, kv, o, do, lse), ride=ride)


def _rope_tables(positions):
    lane = jnp.arange(LANES)
    valid = lane < MLA_ROPE
    inv_freq = ROPE_THETA ** (-(2 * (lane % (MLA_ROPE // 2))).astype(F32) / MLA_ROPE)
    ang = positions.astype(F32)[:, None] * inv_freq[None, :]
    return jnp.where(valid, jnp.cos(ang), 0.0), jnp.where(valid, jnp.sin(ang), 0.0)


def _mla_weights(w):
    dt = w["in_w"].dtype
    iw = w["in_w"]
    c1 = MLA_QR + MLA_KVR + MLA_ROPE
    w_in = jnp.concatenate([iw[:, :c1], jnp.zeros((D_MODEL, LANES - MLA_ROPE), dt), iw[:, c1:]], axis=1)
    qu = w["q_up_w"].reshape(MLA_QR, MLA_H, MLA_NOPE + MLA_ROPE)
    qrope = jnp.concatenate([qu[:, :, MLA_NOPE:], jnp.zeros((MLA_QR, MLA_H, LANES - MLA_ROPE), dt)], axis=2)
    w_q = jnp.concatenate([qu[:, :, :MLA_NOPE].reshape(MLA_QR, -1), qrope.reshape(MLA_QR, -1)], axis=1)
    kvu = w["kv_up_w"].reshape(MLA_KVR, MLA_H, MLA_NOPE + MLA_V)
    w_kv = jnp.concatenate([kvu[:, :, :MLA_NOPE].reshape(MLA_KVR, -1), kvu[:, :, MLA_NOPE:].reshape(MLA_KVR, -1)], axis=1)
    return w_in, w_q, w_kv


def _mla_layer_fwd(u, w, ln_g, ln_b, cosf, sinf, tag, rides=None):
    w_in, w_q, w_kv = _mla_weights(w)
    qnw, kvnw = w["q_norm_w"].reshape(1, -1), w["kv_norm_w"].reshape(1, -1)
    proj = _mm(u.astype(MXU), w_in, name=tag + "_in")
    qn, kvn, kr = _mla_low_fwd(proj, qnw, kvnw, cosf, sinf, name=tag + "_low")
    q = _mm(qn, w_q, name=tag + "_qup")
    kv = _mm(kvn, w_kv, name=tag + "_kvup")
    qr = _rope_heads(q, MLA_H, cosf, sinf, False, name=tag + "_qrope")
    rode = {}
    (o, og, lse), rode["attn"] = _attn_fwd(q, qr, kv, kr, proj, name=tag + "_attn", ride=_ride_of(rides, "attn", False))
    y = _mm(og, w["out_w"], name=tag + "_out")
    h = _ln_fwd(u, y, ln_g, ln_b, name=tag + "_ln")
    saved = dict(u=u, w_in=w_in, w_q=w_q, w_kv=w_kv, qnw=qnw, kvnw=kvnw, proj=proj, qn=qn, kvn=kvn, kr=kr, q=q, kv=kv,
                 qr=qr, o=o, og=og, lse=lse, y=y)
    return h, saved, rode


def _mla_layer_bwd(s, w, ln_g, cosf, sinf, dr_up, du_up, tag, rides=None):
    dr, dg, db = _ln_bwd(s["u"], s["y"], ln_g, dr_up, du_up, name=tag + "_ln_b")
    dog = _mm(dr, w["out_w"], tb=True, name=tag + "_out_bx")
    d_out_w = _mm(s["og"], dr, ta=True, out_dtype=WIRE, name=tag + "_out_bw")
    do, dz = _gate_bwd(dog, s["o"], s["proj"], name=tag + "_gate_b")
    rode = {}
    (dqn_h, dqr_rot), rode["attn_bq"] = _attn_bwd_q(s["q"], s["qr"], s["kv"], s["kr"], s["o"], do, s["lse"],
                                                    name=tag + "_attn_bq", ride=_ride_of(rides, "attn_bq", True))
    (dkn_h, dv_h, dkr_rot), rode["attn_bkv"] = _attn_bwd_kv(s["q"], s["qr"], s["kv"], s["kr"], s["o"], do, s["lse"],
                                                           name=tag + "_attn_bkv", ride=_ride_of(rides, "attn_bkv", True))
    dqr = _rope_heads(dqr_rot, 0, cosf, sinf, True, name=tag + "_qrope_b")
    dq = jnp.concatenate([dqn_h, dqr], axis=1).astype(MXU)
    dkv = jnp.concatenate([dkn_h, dv_h], axis=1).astype(MXU)
    d_wq = _mm(s["qn"], dq, ta=True, out_dtype=WIRE, name=tag + "_qup_bw")
    dqn = _mm(dq, s["w_q"], tb=True, name=tag + "_qup_bx")
    d_wkv = _mm(s["kvn"], dkv, ta=True, out_dtype=WIRE, name=tag + "_kvup_bw")
    dkvn = _mm(dkv, s["w_kv"], tb=True, name=tag + "_kvup_bx")
    dlow, dqnw, dkvnw = _mla_low_bwd(s["proj"], s["qnw"], s["kvnw"], cosf, sinf, dqn, dkvn, dkr_rot, name=tag + "_low_b")
    dproj = jnp.concatenate([dlow, dz], axis=1).astype(MXU)
    d_in = _mm(s["u"].astype(MXU), dproj, ta=True, out_dtype=WIRE, name=tag + "_in_bw")
    du = _mm(dproj, s["w_in"], tb=True, name=tag + "_in_bx")
    c1 = MLA_QR + MLA_KVR + MLA_ROPE
    d_in_w = jnp.concatenate([d_in[:, :c1], d_in[:, MLA_LOW:]], axis=1)
    dq3n = d_wq[:, :MLA_H * MLA_NOPE].reshape(MLA_QR, MLA_H, MLA_NOPE)
    dq3r = d_wq[:, MLA_H * MLA_NOPE:].reshape(MLA_QR, MLA_H, LANES)[:, :, :MLA_ROPE]
    d_q_up = jnp.concatenate([dq3n, dq3r], axis=2).reshape(MLA_QR, -1)
    dkv3 = d_wkv.reshape(MLA_KVR, 2, MLA_H, MLA_NOPE)
    d_kv_up = jnp.concatenate([dkv3[:, 0], dkv3[:, 1]], axis=2).reshape(MLA_KVR, -1)
    grads = dict(in_w=d_in_w, q_norm_w=dqnw[0], q_up_w=d_q_up, kv_norm_w=dkvnw[0], kv_up_w=d_kv_up, out_w=d_out_w,
                 ln_g=dg[0], ln_b=db[0])
    return dr, du, grads, rode


GDN_REP = GDN_HV // GDN_HK
GDN_A_LANE = GDN_HV
GDN_HPB = 8
GDN_VPB = GDN_HPB * GDN_REP


def _bdg(a, b, ca, cb):
    return lax.dot_general(a, b, (((ca,), (cb,)), ((0,), (0,))), preferred_element_type=F32)


@jax.custom_vjp
def _bnn(a, b):
    return _bdg(_mx(a), _mx(b), 2, 1)


def _bnn_f(a, b):
    return _bnn(a, b), (a, b)


def _bnn_b(res, ct):
    a, b = res
    return _bdg(_mx(ct), _mx(b), 2, 2), _bdg(_mx(a), _mx(ct), 1, 1)


_bnn.defvjp(_bnn_f, _bnn_b)


@jax.custom_vjp
def _bnt(a, b):
    return _bdg(_mx(a), _mx(b), 2, 2)


def _bnt_f(a, b):
    return _bnt(a, b), (a, b)


def _bnt_b(res, ct):
    a, b = res
    return _bdg(_mx(ct), _mx(b), 2, 1), _bdg(_mx(ct), _mx(a), 1, 1)


_bnt.defvjp(_bnt_f, _bnt_b)


@jax.custom_vjp
def _btn(a, b):
    return _bdg(_mx(a), _mx(b), 1, 1)


def _btn_f(a, b):
    return _btn(a, b), (a, b)


def _btn_b(res, ct):
    a, b = res
    return _bdg(_mx(b), _mx(ct), 2, 2), _bdg(_mx(a), _mx(ct), 2, 1)


_btn.defvjp(_btn_f, _btn_b)


def _h3(a, b, ca=2, cb=1):
    ah, bh = _mx(a), _mx(b)
    al, bl = _mx(a - ah.astype(F32)), _mx(b - bh.astype(F32))
    return _bdg(ah, bh, ca, cb) + (_bdg(ah, bl, ca, cb) + _bdg(al, bh, ca, cb))


@jax.custom_vjp
def _neumann_inverse(x):
    L = x.shape[-1]
    eye = (lax.broadcasted_iota(jnp.int32, (L, L), 0) == lax.broadcasted_iota(jnp.int32, (L, L), 1)).astype(F32)
    inv = eye + x
    xp = x
    for _ in range(L.bit_length() - 2):
        xp = _h3(xp, xp)
        inv = inv + _h3(inv, xp)
    return inv


def _neumann_f(x):
    inv = _neumann_inverse(x)
    return inv, inv


def _neumann_b(inv, ct):
    return (_h3(_h3(inv, ct, 1, 1), inv, 2, 2),)


_neumann_inverse.defvjp(_neumann_f, _neumann_b)


@jax.custom_vjp
def _saved_inverse(x, inv):
    return inv


def _saved_f(x, inv):
    return inv, inv


def _saved_b(inv, ct):
    return _neumann_b(inv, ct)[0], jnp.zeros_like(inv)


_saved_inverse.defvjp(_saved_f, _saved_b)


def _cat0(parts):
    return jnp.concatenate([p[None] for p in parts], axis=0)


def _gdn_chunk(hb, q, k, v, z, ba, alog, dtb, nw, s, inv_saved=None):
    L = q.shape[1]
    r_i = lax.broadcasted_iota(jnp.int32, (L, L), 0)
    c_i = lax.broadcasted_iota(jnp.int32, (L, L), 1)
    incl, strict = r_i >= c_i, r_i > c_i
    rep = lambda t: jnp.broadcast_to(t[:, None], (GDN_HPB, GDN_REP) + t.shape[1:]).reshape((GDN_VPB,) + t.shape[1:])
    qn = rep(q * lax.rsqrt(jnp.sum(q * q, -1, keepdims=True) + RMS_EPS) * (GDN_DK ** -0.5))
    kn = rep(k * lax.rsqrt(jnp.sum(k * k, -1, keepdims=True) + RMS_EPS))
    beta_all = jax.nn.sigmoid(ba)
    g_all = -jnp.exp(alog) * _softplus(ba + dtb)
    gcs_all = _sel_l(incl.astype(F32), g_all)
    lane = lax.broadcasted_iota(jnp.int32, (L, LANES), 1)
    pick = lambda mat, idx: jnp.sum(jnp.where(lane == idx, mat, 0.0), axis=1, keepdims=True)
    beta = _cat0([pick(beta_all, GDN_VPB * hb + b) for b in range(GDN_VPB)])
    gc = _cat0([pick(gcs_all, GDN_A_LANE + GDN_VPB * hb + b) for b in range(GDN_VPB)])
    gm = jnp.broadcast_to(gc, (GDN_VPB, L, L))
    decay = jnp.exp(jnp.where(incl, gm - jnp.swapaxes(gm, 1, 2), -jnp.inf))
    kb = kn * beta
    eg = jnp.exp(gc)
    x = -jnp.where(strict, _bnt(kb, kn) * decay, 0.0)
    inv = _neumann_inverse(x) if inv_saved is None else _saved_inverse(x, inv_saved)
    uw = _bnn(inv, jnp.concatenate([v * beta, kb * eg], axis=2))
    uu, ww = uw[:, :, :GDN_DV], uw[:, :, GDN_DV:]
    qk = jnp.where(incl, _bnt(qn, kn) * decay, 0.0)
    last = lax.broadcasted_iota(jnp.int32, (L, 1), 0) == L - 1
    glast = jnp.sum(jnp.where(last, gc, 0.0), axis=1, keepdims=True)
    kdec = kn * jnp.exp(glast - gc)
    vnew = uu - _bnn(ww, s)
    o = _bnn(qn * eg, s) + _bnn(qk, vnew)
    new = s * jnp.exp(glast) + _btn(kdec, vnew)
    on = o * lax.rsqrt(jnp.mean(o * o, -1, keepdims=True) + RMS_EPS) * nw * _silu(z)
    return (on, new, inv) if inv_saved is None else (on, new)


def _heads_of(ref, n):
    return _cat0([ref[:, i * LANES:(i + 1) * LANES] for i in range(n)])


def _gdn_specs(nc, rev):
    cc = (lambda c: nc - 1 - c) if rev else (lambda c: c)
    wq, wv = GDN_HPB * GDN_DK, GDN_VPB * GDN_DV
    par = pl.BlockSpec((1, LANES), lambda c, h: (0, 0))
    return dict(
        q=pl.BlockSpec((GDN_L, wq), lambda c, h: (cc(c), h)),
        k=pl.BlockSpec((GDN_L, wq), lambda c, h: (cc(c), GDN_KEY // wq + h)),
        v=pl.BlockSpec((GDN_L, wv), lambda c, h: (cc(c), 2 * GDN_KEY // wv + h)),
        z=pl.BlockSpec((GDN_L, wv), lambda c, h: (cc(c), GDN_CONV_DIM // wv + h)),
        ba=pl.BlockSpec((GDN_L, LANES), lambda c, h: (cc(c), (GDN_CONV_DIM + GDN_VAL) // LANES)),
        par=par,
        st=pl.BlockSpec((1, GDN_VPB, GDN_DK, GDN_DV), lambda c, h: (cc(c), h, 0, 0)),
        inv=pl.BlockSpec((1, GDN_VPB, GDN_L, GDN_L), lambda c, h: (cc(c), h, 0, 0)),
        o=pl.BlockSpec((GDN_L, wv), lambda c, h: (cc(c), h)),
        qk_out=pl.BlockSpec((GDN_L, wq), lambda c, h: (cc(c), h)),
        ba_out=pl.BlockSpec((GDN_L, LANES), lambda c, h: (cc(c), 0)),
    )


def _gdn_fwd(proj, act, alog, dtb, nw, name, ride=None):
    t = proj.shape[0]
    nc = t // GDN_L
    sp = _gdn_specs(nc, False)

    def body(q_ref, k_ref, v_ref, z_ref, ba_ref, alog_ref, dtb_ref, nw_ref, o_ref, st_ref, inv_ref, state):
        c, h = pl.program_id(0), pl.program_id(1)

        mine = pl.ds(h * GDN_VPB, GDN_VPB)

        @pl.when(c == 0)
        def _():
            state[mine] = jnp.zeros((GDN_VPB, GDN_DK, GDN_DV), F32)

        prev = state[mine]
        st_ref[0] = prev
        on, new, inv = _gdn_chunk(h, _heads_of(q_ref, GDN_HPB), _heads_of(k_ref, GDN_HPB), _heads_of(v_ref, GDN_VPB),
                                  _heads_of(z_ref, GDN_VPB), ba_ref[...], alog_ref[...], dtb_ref[...], nw_ref[...], prev)
        inv_ref[0] = inv
        for b in range(GDN_VPB):
            o_ref[:, b * LANES:(b + 1) * LANES] = on[b].astype(o_ref.dtype)
        state[mine] = new

    sds = jax.ShapeDtypeStruct
    return _hosted_call(
        body, name=name, grid=(nc, GDN_HK // GDN_HPB),
        in_specs=[sp["q"], sp["k"], sp["v"], sp["z"], sp["ba"], sp["par"], sp["par"], sp["par"]],
        out_specs=[sp["o"], sp["st"], sp["inv"]],
        out_shape=[sds((t, GDN_VAL), MXU), sds((nc, GDN_HV, GDN_DK, GDN_DV), F32), sds((nc, GDN_HV, GDN_L, GDN_L), F32)],
        scratch_shapes=[pltpu.VMEM((GDN_HV, GDN_DK, GDN_DV), F32)],
        sem=("arbitrary", "arbitrary"), args=(act, act, act, proj, proj, alog, dtb, nw), ride=ride)


def _gdn_bwd(proj, act, alog, dtb, nw, states, invs, don, name, ride=None):
    t = proj.shape[0]
    nc = t // GDN_L
    sp = _gdn_specs(nc, True)

    def body(q_ref, k_ref, v_ref, z_ref, ba_ref, alog_ref, dtb_ref, nw_ref, st_ref, do_ref, inv_ref,
             dact_ref, dz_ref, dba_ref, dalog_ref, ddtb_ref, dnw_ref, dstate):
        c, h = pl.program_id(0), pl.program_id(1)

        mine = pl.ds(h * GDN_VPB, GDN_VPB)

        @pl.when(c == 0)
        def _():
            dstate[mine] = jnp.zeros((GDN_VPB, GDN_DK, GDN_DV), F32)

        @pl.when((c == 0) & (h == 0))
        def _():
            dalog_ref[...] = jnp.zeros_like(dalog_ref)
            ddtb_ref[...] = jnp.zeros_like(ddtb_ref)
            dnw_ref[...] = jnp.zeros_like(dnw_ref)

        @pl.when(h == 0)
        def _():
            dba_ref[...] = jnp.zeros_like(dba_ref)

        _, vjp = jax.vjp(functools.partial(_gdn_chunk, h, inv_saved=inv_ref[0]), _heads_of(q_ref, GDN_HPB),
                         _heads_of(k_ref, GDN_HPB), _heads_of(v_ref, GDN_VPB), _heads_of(z_ref, GDN_VPB), ba_ref[...],
                         alog_ref[...], dtb_ref[...], nw_ref[...], st_ref[0])
        dq, dk, dv, dz, dba, dalog, ddtb, dnw, dprev = vjp((_heads_of(do_ref, GDN_VPB), dstate[mine]))
        for b in range(GDN_VPB):
            dz_ref[:, b * LANES:(b + 1) * LANES] = dz[b]
        for hi in range(GDN_HK // GDN_HPB):
            @pl.when(h == hi)
            def _(hi=hi):
                for i in range(GDN_HPB):
                    c0 = (hi * GDN_HPB + i) * GDN_DK
                    dact_ref[:, c0:c0 + GDN_DK] = dq[i]
                    dact_ref[:, GDN_KEY + c0:GDN_KEY + c0 + GDN_DK] = dk[i]
                for b in range(GDN_VPB):
                    c0 = 2 * GDN_KEY + (hi * GDN_VPB + b) * GDN_DV
                    dact_ref[:, c0:c0 + GDN_DV] = dv[b]
        dba_ref[...] += dba
        dalog_ref[...] += dalog
        ddtb_ref[...] += ddtb
        dnw_ref[...] += dnw
        dstate[mine] = dprev

    sds = jax.ShapeDtypeStruct
    par_out = pl.BlockSpec((1, LANES), lambda c, h: (0, 0))
    return _hosted_call(
        body, name=name, grid=(nc, GDN_HK // GDN_HPB),
        in_specs=[sp["q"], sp["k"], sp["v"], sp["z"], sp["ba"], sp["par"], sp["par"], sp["par"], sp["st"], sp["o"],
                  sp["inv"]],
        out_specs=[pl.BlockSpec((GDN_L, GDN_CONV_DIM), lambda c, h: (nc - 1 - c, 0)), sp["o"], sp["ba_out"],
                   par_out, par_out, par_out],
        out_shape=[sds((t, GDN_CONV_DIM), F32), sds((t, GDN_VAL), F32), sds((t, LANES), F32), sds((1, LANES), F32),
                   sds((1, LANES), F32), sds((1, LANES), F32)],
        scratch_shapes=[pltpu.VMEM((GDN_HV, GDN_DK, GDN_DV), F32)],
        sem=("arbitrary", "arbitrary"), args=(act, act, act, proj, proj, alog, dtb, nw, states, don, invs), ride=ride)


GDN_PROJ = GDN_CONV_DIM + GDN_VAL + 2 * GDN_HV


def _gdn_layer_fwd(u, w, ln_g, ln_b, tag, rides=None, late=None):
    w_in = jnp.concatenate([w["in_w"], jnp.zeros((D_MODEL, GDN_PROJ_PAD - GDN_PROJ), w["in_w"].dtype)], axis=1)
    alog = _pad_lanes(w["a_log"], offset=GDN_A_LANE)
    dtb = _pad_lanes(w["dt_bias"], offset=GDN_A_LANE)
    nw = w["norm_w"].reshape(1, GDN_DV)
    zb = jnp.zeros((1, GDN_CONV_DIM), F32)
    if late is None:
        proj = _mm(u.astype(MXU), w_in, name=tag + "_in")
    else:
        proj, got = _mm(u.astype(MXU), w_in, name=tag + "_in", ride=(late[0], False))
        w.update(late[1](got))
    act, _ = _conv_fwd(proj, 0, w["conv_w"], zb, name=tag + "_conv")
    rode = {}
    (on, states, invs), rode["delta"] = _gdn_fwd(proj, act, alog, dtb, nw, name=tag + "_delta",
                                                 ride=_ride_of(rides, "delta", False))
    y = _mm(on, w["out_w"], name=tag + "_out")
    h = _ln_fwd(u, y, ln_g, ln_b, name=tag + "_ln")
    saved = dict(u=u, w_in=w_in, proj=proj, act=act, states=states, invs=invs, on=on, y=y, alog=alog, dtb=dtb, nw=nw, zb=zb)
    return h, saved, rode


def _gdn_layer_bwd(s, w, ln_g, dr_up, du_up, tag, rides=None, own=None):
    t = s["u"].shape[0]
    dr, dg, db = _ln_bwd(s["u"], s["y"], ln_g, dr_up, du_up, name=tag + "_ln_b")
    don = _mm(dr, w["out_w"], tb=True, name=tag + "_out_bx")
    d_out_w = _mm(s["on"], dr, ta=True, out_dtype=WIRE, name=tag + "_out_bw")
    carried = list((rides or {}).get("delta_b", [])) + ([own("out_w", d_out_w)] if own else [])
    rode = {}
    (dact, dz, dba, dalog, ddtb, dnw), rode["delta_b"] = _gdn_bwd(
        s["proj"], s["act"], s["alog"], s["dtb"], s["nw"], s["states"], s["invs"], don, name=tag + "_delta_b",
        ride=(carried, True) if carried else None)
    dqkv, d_conv_w, _ = _conv_bwd(s["proj"], 0, w["conv_w"], s["zb"], dact, name=tag + "_conv_b")
    dproj = jnp.concatenate([dqkv, dz, dba, jnp.zeros((t, GDN_PROJ_PAD - GDN_PROJ - (LANES - 2 * GDN_HV)), F32)],
                            axis=1).astype(MXU)
    d_in = _mm(s["u"].astype(MXU), dproj, ta=True, out_dtype=WIRE, name=tag + "_in_bw")
    du = _mm(dproj, s["w_in"], tb=True, name=tag + "_in_bx")
    grads = dict(in_w=d_in[:, :GDN_PROJ], conv_w=d_conv_w, a_log=dalog[0, GDN_A_LANE:GDN_A_LANE + GDN_HV],
                 dt_bias=ddtb[0, GDN_A_LANE:GDN_A_LANE + GDN_HV], norm_w=dnw[0], out_w=d_out_w, ln_g=dg[0], ln_b=db[0])
    return dr, du, grads, rode


def _mesh_pos():
    return lax.axis_index("x"), lax.axis_index("y"), lax.axis_index("c")


def _peer(k, x, y, c):
    return ((1 - x) if k & 4 else x, (1 - y) if k & 2 else y, (1 - c) if k & 1 else c)


def _ride_copies(ins, outs, send, recv, loc, scatter, with_arrivals):
    n = len(ins)
    x, y, c = _mesh_pos()
    me = 4 * x + 2 * y + c
    local = [pltpu.make_async_copy(ins[i].at[me] if scatter else ins[i], outs[i].at[me], loc.at[i]) for i in range(n)]
    sends, arrivals = [], []
    for k in range(1, N_DEV):
        peer = _peer(k, x, y, c)
        pidx = 4 * peer[0] + 2 * peer[1] + peer[2]
        for i in range(n):
            src = ins[i].at[pidx] if scatter else ins[i]
            sems = dict(send_sem=send.at[i, k - 1], recv_sem=recv.at[i, k - 1], device_id=peer,
                        device_id_type=pl.DeviceIdType.MESH)
            sends.append(pltpu.make_async_remote_copy(src_ref=src, dst_ref=outs[i].at[me], **sems))
            if with_arrivals:
                arrivals.append(pltpu.make_async_remote_copy(src_ref=src, dst_ref=outs[i].at[pidx], **sems))
    return local, sends, arrivals


def _gather_copy(ins, outs, send, recv, i, k, block, to, from_input=False):
    slot = outs[i].at[4 * block[0] + 2 * block[1] + block[2]]
    return pltpu.make_async_remote_copy(src_ref=ins[i] if from_input else slot, dst_ref=slot, send_sem=send.at[i, k],
                                        recv_sem=recv.at[i, k], device_id=to, device_id_type=pl.DeviceIdType.MESH)


def _gather_places():
    x, y, c = _mesh_pos()
    return (x, y, c), (x, y, 1 - c), [(x, 1 - y), (1 - x, y), (1 - x, 1 - y)], c


def _ride_start(ins, outs, send, recv, loc, scatter, direct=True):
    if direct:
        local, sends, _ = _ride_copies(ins, outs, send, recv, loc, scatter, False)
        for cp in local + sends:
            cp.start()
        return
    me, sibling, chips, c = _gather_places()
    for i in range(len(ins)):
        pltpu.make_async_copy(ins[i], outs[i].at[4 * me[0] + 2 * me[1] + me[2]], loc.at[i]).start()
        _gather_copy(ins, outs, send, recv, i, 0, me, sibling, True).start()
        for j, ch in enumerate(chips):
            _gather_copy(ins, outs, send, recv, i, 1 + j, me, (*ch, c), True).start()


def _ride_wait(ins, outs, send, recv, loc, scatter, direct=True):
    if direct:
        local, sends, arrivals = _ride_copies(ins, outs, send, recv, loc, scatter, True)
        for cp in arrivals:
            cp.wait_recv()
        for cp in sends:
            cp.wait_send()
        for cp in local:
            cp.wait()
        return
    me, sibling, chips, c = _gather_places()
    n = len(ins)
    passed = []
    for j, ch in enumerate(chips):
        for i in range(n):
            _gather_copy(ins, outs, send, recv, i, 1 + j, (*ch, c), me).wait_recv()
            cp = _gather_copy(ins, outs, send, recv, i, 4 + j, (*ch, c), sibling)
            cp.start()
            passed.append(cp)
    for i in range(n):
        _gather_copy(ins, outs, send, recv, i, 0, sibling, me).wait_recv()
        for j, ch in enumerate(chips):
            _gather_copy(ins, outs, send, recv, i, 4 + j, (*ch, 1 - c), me).wait_recv()
    for i in range(n):
        _gather_copy(ins, outs, send, recv, i, 0, me, sibling, True).wait_send()
        for j, ch in enumerate(chips):
            _gather_copy(ins, outs, send, recv, i, 1 + j, me, (*ch, c), True).wait_send()
    for cp in passed:
        cp.wait_send()
    for i in range(n):
        pltpu.make_async_copy(ins[i], outs[i].at[4 * me[0] + 2 * me[1] + me[2]], loc.at[i]).wait()


def _ride_shapes(arrs, scatter):
    n = len(arrs)
    out_shape = [jax.ShapeDtypeStruct(a.shape if scatter else (N_DEV,) + a.shape, a.dtype) for a in arrs]
    scratch = [pltpu.SemaphoreType.DMA((n, N_DEV - 1)), pltpu.SemaphoreType.DMA((n, N_DEV - 1)), pltpu.SemaphoreType.DMA((n,))]
    return out_shape, scratch


def _exchange(arrs, scatter, name):
    n = len(arrs)
    hbm = pl.BlockSpec(memory_space=pltpu.HBM)

    def body(*refs):
        ins, outs = refs[:n], refs[n:2 * n]
        _ride_start(ins, outs, *refs[2 * n:], scatter, direct=scatter)
        _ride_wait(ins, outs, *refs[2 * n:], scatter, direct=scatter)

    out_shape, scratch = _ride_shapes(arrs, scatter)
    return pl.pallas_call(
        body, name=name, in_specs=[hbm] * n, out_specs=[hbm] * n, out_shape=out_shape, scratch_shapes=scratch,
        compiler_params=pltpu.CompilerParams(has_side_effects=True),
    )(*arrs)


def _hosted_call(body, *, name, grid, in_specs, out_specs, out_shape, scratch_shapes, sem, args, ride=None):
    if ride is None:
        return pl.pallas_call(body, name=name, grid=grid, in_specs=in_specs, out_specs=out_specs, out_shape=out_shape,
                              scratch_shapes=scratch_shapes, compiler_params=_cparams(sem))(*args), []
    arrs, scatter = ride
    n, ni, no, ns = len(arrs), len(in_specs), len(out_specs), len(scratch_shapes)
    hbm = pl.BlockSpec(memory_space=pltpu.HBM)
    r_shape, r_scratch = _ride_shapes(arrs, scatter)

    def full(*refs):
        a, ri = refs[:ni], refs[ni:ni + n]
        o, ro = refs[ni + n:ni + n + no], refs[ni + n + no:ni + 2 * n + no]
        s, rs = refs[ni + 2 * n + no:ni + 2 * n + no + ns], refs[ni + 2 * n + no + ns:]
        ids = [pl.program_id(d) for d in range(len(grid))]
        first, last = ids[0] == 0, ids[0] == grid[0] - 1
        for d in range(1, len(grid)):
            first, last = first & (ids[d] == 0), last & (ids[d] == grid[d] - 1)

        @pl.when(first)
        def _():
            _ride_start(ri, ro, *rs, scatter)

        body(*a, *o, *s)

        @pl.when(last)
        def _():
            _ride_wait(ri, ro, *rs, scatter)

    outs = pl.pallas_call(
        full, name=name, grid=grid, in_specs=list(in_specs) + [hbm] * n, out_specs=list(out_specs) + [hbm] * n,
        out_shape=list(out_shape) + r_shape, scratch_shapes=list(scratch_shapes) + r_scratch,
        compiler_params=pltpu.CompilerParams(dimension_semantics=("arbitrary",) * len(grid), vmem_limit_bytes=VMEM_LIMIT,
                                             has_side_effects=True),
    )(*args, *arrs)
    return outs[:no], list(outs[no:])


def _to_wire(w3, j, name):
    _, r, c = w3.shape
    tr = _row_tile(r, c)

    def body(w_ref, o_ref):
        o_ref[...] = w_ref[0].astype(o_ref.dtype)

    return pl.pallas_call(
        body, name=name, grid=(r // tr,), in_specs=[pl.BlockSpec((1, tr, c), lambda i: (j, i, 0))],
        out_specs=pl.BlockSpec((tr, c), lambda i: (i, 0)), out_shape=jax.ShapeDtypeStruct((r, c), WIRE),
        compiler_params=_cparams(("parallel",)),
    )(w3)


def _unshard(g, ax):
    g = jnp.moveaxis(g, 0, ax)
    sh = g.shape
    return g.reshape(sh[:ax] + (sh[ax] * sh[ax + 1],) + sh[ax + 2:])


def _to_parts(full, ax):
    sh = full.shape
    full = full.reshape(sh[:ax] + (N_DEV, sh[ax] // N_DEV) + sh[ax + 1:])
    return jnp.moveaxis(full, ax, 0)


def _row_tile(r, c):
    cap = max(8, (256 * 1024) // max(c, 1))
    best = None
    for d in range(8, min(r, cap) + 1, 8):
        if r % d == 0:
            best = d
    return r if best is None else best


def _adamw(w, m, v, parts, name):
    r, c = w.shape
    tr = _row_tile(r, c)

    def body(w_ref, m_ref, v_ref, p_ref, g_ref, d_ref, nm_ref, nv_ref):
        g = p_ref[0].astype(F32)
        for q in range(1, N_DEV):
            g = g + p_ref[q].astype(F32)
        nm = ADAM_B1 * m_ref[...] + (1.0 - ADAM_B1) * g
        nv = ADAM_B2 * v_ref[...] + (1.0 - ADAM_B2) * (g * g)
        m_hat = nm / (1.0 - ADAM_B1 ** ADAM_STEP)
        v_hat = nv / (1.0 - ADAM_B2 ** ADAM_STEP)
        g_ref[...] = g
        d_ref[...] = -ADAM_LR * (m_hat / (jnp.sqrt(v_hat) + ADAM_EPS) + ADAM_WD * w_ref[...])
        nm_ref[...] = nm
        nv_ref[...] = nv

    row = pl.BlockSpec((tr, c), lambda i: (i, 0))
    out = jax.ShapeDtypeStruct((r, c), F32)
    return pl.pallas_call(
        body, name=name, grid=(r // tr,),
        in_specs=[row, row, row, pl.BlockSpec((N_DEV, tr, c), lambda i: (0, i, 0))],
        out_specs=[row] * 4, out_shape=[out] * 4, compiler_params=_cparams(("parallel",)),
    )(w, m, v, parts)


WEIGHTS = ['ssd_in_w', 'ssd_conv_w', 'ssd_conv_b', 'ssd_dt_bias', 'ssd_a_log', 'ssd_d', 'ssd_norm_w', 'ssd_out_w',
           'mla_in_w', 'mla_q_norm_w', 'mla_q_up_w', 'mla_kv_norm_w', 'mla_kv_up_w', 'mla_out_w', 'gdn_in_w',
           'gdn_conv_w', 'gdn_a_log', 'gdn_dt_bias', 'gdn_norm_w', 'gdn_out_w', 'ln_g', 'ln_b']
SHARDED = {'ssd_in_w': (1, True), 'ssd_conv_w': (1, False), 'ssd_conv_b': (0, False), 'ssd_norm_w': (0, False),
           'ssd_out_w': (0, True), 'mla_in_w': (1, True), 'mla_q_up_w': (1, True), 'mla_kv_up_w': (1, True),
           'mla_out_w': (0, True), 'gdn_in_w': (1, True), 'gdn_conv_w': (1, False), 'gdn_out_w': (0, True)}
REPLICATED = [n for n in WEIGHTS if n not in SHARDED]


def _pack_small(vals):
    flat = jnp.concatenate([vals[n].reshape(-1).astype(F32) for n in REPLICATED])
    rows = -(-flat.shape[0] // (8 * LANES)) * 8
    return jnp.pad(flat, (0, rows * LANES - flat.shape[0])).reshape(rows, LANES)


def _unpack_small(slab, like):
    flat = slab.reshape(-1)
    out, off = {}, 0
    for n in REPLICATED:
        sz = like[n].size
        out[n] = flat[off:off + sz].reshape(like[n].shape)
        off += sz
    return out


def kernel(x, positions, ssd_in_w, ssd_conv_w, ssd_conv_b, ssd_dt_bias, ssd_a_log, ssd_d, ssd_norm_w, ssd_out_w, mla_in_w, mla_q_norm_w, mla_q_up_w, mla_kv_norm_w, mla_kv_up_w, mla_out_w, gdn_in_w, gdn_conv_w, gdn_a_log, gdn_dt_bias, gdn_norm_w, gdn_out_w, ln_g, ln_b, loss_target, m_ssd_in_w, m_ssd_conv_w, m_ssd_conv_b, m_ssd_dt_bias, m_ssd_a_log, m_ssd_d, m_ssd_norm_w, m_ssd_out_w, m_mla_in_w, m_mla_q_norm_w, m_mla_q_up_w, m_mla_kv_norm_w, m_mla_kv_up_w, m_mla_out_w, m_gdn_in_w, m_gdn_conv_w, m_gdn_a_log, m_gdn_dt_bias, m_gdn_norm_w, m_gdn_out_w, m_ln_g, m_ln_b, v_ssd_in_w, v_ssd_conv_w, v_ssd_conv_b, v_ssd_dt_bias, v_ssd_a_log, v_ssd_d, v_ssd_norm_w, v_ssd_out_w, v_mla_in_w, v_mla_q_norm_w, v_mla_q_up_w, v_mla_kv_norm_w, v_mla_kv_up_w, v_mla_out_w, v_gdn_in_w, v_gdn_conv_w, v_gdn_a_log, v_gdn_dt_bias, v_gdn_norm_w, v_gdn_out_w, v_ln_g, v_ln_b):
    loc = locals()
    w = {n: loc[n] for n in WEIGHTS}
    m = {n: loc["m_" + n] for n in WEIGHTS}
    v = {n: loc["v_" + n] for n in WEIGHTS}
    xs, pos, tgt = x[0], positions[0], loss_target[0]

    def names_of(prefix):
        return [n for n in SHARDED if n.startswith(prefix + "_")]

    def shard(n, j):
        return _to_wire(w[n], j, name="wire_%s_%d" % (n, j)) if SHARDED[n][1] else w[n][j]

    def full(n, gathered):
        return _unshard(gathered, SHARDED[n][0])

    def slots(n, g):
        return _to_parts(g, SHARDED[n][0]).astype(WIRE if SHARDED[n][1] else F32)

    def replicated(prefix, j):
        return {n[len(prefix) + 1:]: w[n][j] for n in REPLICATED if n.startswith(prefix + "_")}

    def late_weights(prefix, j, keys):
        return [shard(prefix + "_" + k, j) for k in keys], lambda got: {k: full(prefix + "_" + k, g) for k, g in zip(keys, got)}

    lg = lambda i: w["ln_g"][i].reshape(1, D_MODEL)
    lb = lambda i: w["ln_b"][i].reshape(1, D_MODEL)
    cosf, sinf = _rope_tables(pos)
    ssd_late = ("conv_w", "conv_b", "norm_w", "out_w")

    w_s0 = dict(replicated("ssd", 0), in_w=full("ssd_in_w", _exchange([shard("ssd_in_w", 0)], False, name="gather_ssd0")[0]))
    h1, s0, got = _ssd_layer_fwd(
        xs, w_s0, lg(0), lb(0), "ssd0", late=late_weights("ssd", 0, ssd_late),
        rides={"conv": [shard("mla_q_up_w", 0), shard("mla_kv_up_w", 0)], "scan": [shard("mla_in_w", 0)],
               "out": [shard("mla_out_w", 0)]})
    w_m0 = dict(replicated("mla", 0), q_up_w=full("mla_q_up_w", got["conv"][0]), kv_up_w=full("mla_kv_up_w", got["conv"][1]),
                in_w=full("mla_in_w", got["scan"][0]), out_w=full("mla_out_w", got["out"][0]))
    h2, s1, got = _mla_layer_fwd(h1, w_m0, lg(1), lb(1), cosf, sinf, "mla0", rides={"attn": [shard("gdn_in_w", 0)]})
    w_g0 = dict(replicated("gdn", 0), in_w=full("gdn_in_w", got["attn"][0]))
    h3, s2, got = _gdn_layer_fwd(h2, w_g0, lg(2), lb(2), "gdn0", late=late_weights("gdn", 0, ("conv_w", "out_w")),
                                 rides={"delta": [shard("ssd_in_w", 1)]})
    w_s1 = dict(replicated("ssd", 1), in_w=full("ssd_in_w", got["delta"][0]))
    h4, s3, _ = _ssd_layer_fwd(h3, w_s1, lg(3), lb(3), "ssd1", late=late_weights("ssd", 1, ssd_late))
    loss_tile, dl = _loss_head(h4, tgt, name="loss_head")

    dr3, du3, g3, _, _ = _ssd_layer_bwd(s3, w_s1, lg(3), jnp.zeros_like(dl), dl, "ssd1")
    dr2, du2, g2, got = _gdn_layer_bwd(s2, w_g0, lg(2), dr3, du3, "gdn0", own=lambda k, g: slots("gdn_" + k, g),
                                       rides={"delta_b": [slots(n, g3[n[4:]]) for n in names_of("ssd")]})
    r3, r2_out = got["delta_b"][:-1], got["delta_b"][-1]
    gin = slots("gdn_in_w", g2["in_w"])
    half = gin.shape[1] // 2
    dr1, du1, g1, got = _mla_layer_bwd(
        s1, w_m0, lg(1), cosf, sinf, dr2, du2, "mla0",
        rides={"attn_bq": [gin[:, :half]], "attn_bkv": [gin[:, half:], slots("gdn_conv_w", g2["conv_w"])]})
    r2 = {"gdn_in_w": jnp.concatenate([got["attn_bq"][0], got["attn_bkv"][0]], axis=1), "gdn_conv_w": got["attn_bkv"][1],
          "gdn_out_w": r2_out}
    dr0, du0, g0, got, r0 = _ssd_layer_bwd(s0, w_s0, lg(0), dr1, du1, "ssd0", own=lambda k, g: slots("ssd_" + k, g),
                                           rides={"scan_b": [slots(n, g1[n[4:]]) for n in names_of("mla")]})
    r1 = got["scan_b"]
    grad_x = _axpy(dr0, du0, name="grad_x")[None]

    gsmall = {"ssd_" + k: jnp.stack([g0[k], g3[k]]) for k in ("dt_bias", "a_log", "d")}
    gsmall.update({"mla_" + k: g1[k][None] for k in ("q_norm_w", "kv_norm_w")})
    gsmall.update({"gdn_" + k: g2[k][None] for k in ("a_log", "dt_bias", "norm_w")})
    gsmall["ln_g"] = jnp.stack([g0["ln_g"], g1["ln_g"], g2["ln_g"], g3["ln_g"]])
    gsmall["ln_b"] = jnp.stack([g0["ln_b"], g1["ln_b"], g2["ln_b"], g3["ln_b"]])
    small = _pack_small(gsmall)
    rsmall, = _exchange([jnp.broadcast_to(small[None], (N_DEV,) + small.shape)], True, name="gather_small_grads")

    recvd = {n: jnp.stack([r0[n[4:]], b], axis=1) for n, b in zip(names_of("ssd"), r3)}
    recvd.update({n: a[:, None] for n, a in zip(names_of("mla"), r1)})
    recvd.update({n: a[:, None] for n, a in r2.items()})
    recvd = [recvd[n] for n in SHARDED] + [rsmall]

    grads, deltas, new_m, new_v = {}, {}, {}, {}
    for n, pt in zip(SHARDED, recvd[:-1]):
        shp = w[n].shape
        r2d = (-1, shp[-1])
        outs = _adamw(w[n].reshape(r2d), m[n].reshape(r2d), v[n].reshape(r2d), pt.reshape((N_DEV,) + w[n].reshape(r2d).shape),
                      name="adamw_" + n)
        grads[n], deltas[n], new_m[n], new_v[n] = (o.reshape(shp) for o in outs)
    outs = _adamw(_pack_small(w), _pack_small(m), _pack_small(v), recvd[-1], name="adamw_replicated")
    for dst, o in zip((grads, deltas, new_m, new_v), outs):
        dst.update(_unpack_small(o, w))

    loss = lax.psum(loss_tile[0, 0], ("x", "y", "c"))
    return (loss, grad_x, *[grads[n] for n in WEIGHTS], *[deltas[n] for n in WEIGHTS],
            *[new_m[n] for n in WEIGHTS], *[new_v[n] for n in WEIGHTS])
```

```python
import functools

import jax
import jax.numpy as jnp
from jax import lax
from jax.experimental import pallas as pl
from jax.experimental.pallas import tpu as pltpu

F32 = jnp.float32
MXU = jnp.bfloat16
WIRE = jnp.bfloat16
HI = lax.Precision.HIGHEST

N_DEV = 8
LANES = 128
VMEM_LIMIT = 56 * 1024 * 1024

D_MODEL = 2048
DEPTH = 4
ALPHA = (2.0 * DEPTH) ** 0.25
LN_EPS = 1e-5
RMS_EPS = 1e-6

SSD_DI = 4096
SSD_P = 64
SSD_H = 64
SSD_G = 8
SSD_N = 128
SSD_L = 128
SSD_GS = SSD_DI // SSD_G
SSD_CONV_DIM = SSD_DI + 2 * SSD_G * SSD_N
SSD_PROJ_PAD = SSD_DI + SSD_CONV_DIM + LANES

MLA_H = 16
MLA_QR = 768
MLA_KVR = 512
MLA_NOPE = 128
MLA_ROPE = 64
MLA_V = 128
MLA_GATE = MLA_H * MLA_V
MLA_PROJ_PAD = MLA_QR + MLA_KVR + LANES + MLA_GATE
MLA_SCALE = (MLA_NOPE + MLA_ROPE) ** -0.5
ROPE_THETA = 10000.0
ATT_BLK = 512

GDN_HK = 16
GDN_HV = 32
GDN_DK = 128
GDN_DV = 128
GDN_KEY = GDN_HK * GDN_DK
GDN_VAL = GDN_HV * GDN_DV
GDN_L = 128
GDN_CONV_DIM = 2 * GDN_KEY + GDN_VAL
GDN_PROJ_PAD = 12800

ADAM_LR = 0.001
ADAM_B1 = 0.9
ADAM_B2 = 0.999
ADAM_EPS = 1e-08
ADAM_WD = 0.01
ADAM_STEP = 10


def _cparams(sem=None):
    return pltpu.CompilerParams(dimension_semantics=sem, vmem_limit_bytes=VMEM_LIMIT)


def _tile(n, cap):
    if n <= cap:
        return n
    best = None
    for d in range(LANES, cap + 1, LANES):
        if n % d == 0:
            best = d
    assert best is not None, (n, cap)
    return best


def _dg(a, b, ca, cb, prec=None):
    return lax.dot_general(a, b, (((ca,), (cb,)), ((), ())), preferred_element_type=F32, precision=prec)


def _mx(a):
    return a.astype(MXU)


@jax.custom_vjp
def _nn(a, b):
    return _dg(_mx(a), _mx(b), 1, 0)


def _nn_f(a, b):
    return _nn(a, b), (a, b)


def _nn_b(res, ct):
    a, b = res
    return _dg(_mx(ct), _mx(b), 1, 1), _dg(_mx(a), _mx(ct), 0, 0)


_nn.defvjp(_nn_f, _nn_b)


@jax.custom_vjp
def _nt(a, b):
    return _dg(_mx(a), _mx(b), 1, 1)


def _nt_f(a, b):
    return _nt(a, b), (a, b)


def _nt_b(res, ct):
    a, b = res
    return _dg(_mx(ct), _mx(b), 1, 0), _dg(_mx(ct), _mx(a), 0, 0)


_nt.defvjp(_nt_f, _nt_b)


@jax.custom_vjp
def _tn(a, b):
    return _dg(_mx(a), _mx(b), 0, 0)


def _tn_f(a, b):
    return _tn(a, b), (a, b)


def _tn_b(res, ct):
    a, b = res
    return _dg(_mx(b), _mx(ct), 1, 1), _dg(_mx(a), _mx(ct), 1, 0)


_tn.defvjp(_tn_f, _tn_b)


def _softplus(x):
    return jnp.maximum(x, 0.0) + jnp.log(1.0 + jnp.exp(-jnp.abs(x)))


def _silu(x):
    return x * jax.nn.sigmoid(x)


MM_TM = 1024
MM_TN = 1280
MM_VMEM_BUDGET = 40 * 1024 * 1024


def _mm(a, b, *, ta=False, tb=False, out_dtype=F32, name, ride=None):
    if ta:
        kdim, m = a.shape
    else:
        m, kdim = a.shape
    if tb:
        n, kb = b.shape
    else:
        kb, n = b.shape
    assert kdim == kb, (a.shape, b.shape, ta, tb)
    tm, tn = _tile(m, MM_TM), _tile(n, MM_TN)
    abytes, bbytes, obytes = a.dtype.itemsize, b.dtype.itemsize, jnp.dtype(out_dtype).itemsize
    tk = LANES
    for d in range(LANES, kdim + 1, LANES):
        if kdim % d == 0 and 2 * d * (tm * abytes + tn * bbytes) + tm * tn * (2 * obytes + 4) <= MM_VMEM_BUDGET:
            tk = d
    nk = kdim // tk
    ca, cb = (0 if ta else 1), (1 if tb else 0)

    def body(a_ref, b_ref, o_ref, *acc):
        part = _dg(_mx(a_ref[...]), _mx(b_ref[...]), ca, cb)
        if nk == 1:
            o_ref[...] = part.astype(out_dtype)
            return
        acc_ref, = acc
        k = pl.program_id(2)

        @pl.when(k == 0)
        def _():
            acc_ref[...] = part

        @pl.when(k > 0)
        def _():
            acc_ref[...] += part

        @pl.when(k == nk - 1)
        def _():
            o_ref[...] = acc_ref[...].astype(out_dtype)

    a_spec = pl.BlockSpec((tk, tm), lambda i, j, k: (k, i)) if ta else pl.BlockSpec((tm, tk), lambda i, j, k: (i, k))
    b_spec = pl.BlockSpec((tn, tk), lambda i, j, k: (j, k)) if tb else pl.BlockSpec((tk, tn), lambda i, j, k: (k, j))
    (out,), rode = _hosted_call(
        body, name=name, grid=(m // tm, n // tn, nk),
        in_specs=[a_spec, b_spec], out_specs=[pl.BlockSpec((tm, tn), lambda i, j, k: (i, j))],
        out_shape=[jax.ShapeDtypeStruct((m, n), out_dtype)],
        scratch_shapes=[pltpu.VMEM((tm, tn), F32)] if nk > 1 else [],
        sem=("parallel", "parallel", "arbitrary"), args=(a, b), ride=ride)
    return out if ride is None else (out, rode)


def _ln_fwd(h, y, g, b, name):
    t, d = h.shape
    tr = _tile(t, 256)

    def body(h_ref, y_ref, g_ref, b_ref, o_ref):
        r = ALPHA * h_ref[...] + y_ref[...]
        mu = jnp.mean(r, -1, keepdims=True)
        xc = r - mu
        var = jnp.mean(xc * xc, -1, keepdims=True)
        o_ref[...] = xc * lax.rsqrt(var + LN_EPS) * g_ref[...] + b_ref[...]

    row = pl.BlockSpec((tr, d), lambda i: (i, 0))
    par = pl.BlockSpec((1, d), lambda i: (0, 0))
    return pl.pallas_call(
        body, name=name, grid=(t // tr,), in_specs=[row, row, par, par], out_specs=row,
        out_shape=jax.ShapeDtypeStruct((t, d), F32), compiler_params=_cparams(("parallel",)),
    )(h, y, g, b)


def _ln_bwd(h, y, g, dr_up, du_up, name):
    t, d = h.shape
    tr = _tile(t, 256)

    def body(h_ref, y_ref, g_ref, dr_ref, du_ref, o_ref, dg_ref, db_ref):
        i = pl.program_id(0)

        @pl.when(i == 0)
        def _():
            dg_ref[...] = jnp.zeros_like(dg_ref)
            db_ref[...] = jnp.zeros_like(db_ref)

        dout = ALPHA * dr_ref[...] + du_ref[...]
        r = ALPHA * h_ref[...] + y_ref[...]
        mu = jnp.mean(r, -1, keepdims=True)
        xc = r - mu
        rstd = lax.rsqrt(jnp.mean(xc * xc, -1, keepdims=True) + LN_EPS)
        xh = xc * rstd
        dxh = dout * g_ref[...]
        o_ref[...] = rstd * (dxh - jnp.mean(dxh, -1, keepdims=True) - xh * jnp.mean(dxh * xh, -1, keepdims=True))
        dg_ref[...] += jnp.sum(dout * xh, 0, keepdims=True)
        db_ref[...] += jnp.sum(dout, 0, keepdims=True)

    row = pl.BlockSpec((tr, d), lambda i: (i, 0))
    par = pl.BlockSpec((1, d), lambda i: (0, 0))
    return pl.pallas_call(
        body, name=name, grid=(t // tr,), in_specs=[row, row, par, row, row], out_specs=[row, par, par],
        out_shape=[jax.ShapeDtypeStruct((t, d), F32), jax.ShapeDtypeStruct((1, d), F32), jax.ShapeDtypeStruct((1, d), F32)],
        compiler_params=_cparams(("arbitrary",)),
    )(h, y, g, dr_up, du_up)


def _loss_head(h, tgt, name):
    t, d = h.shape
    tr = _tile(t, 256)

    def body(h_ref, t_ref, l_ref, d_ref):
        i = pl.program_id(0)

        @pl.when(i == 0)
        def _():
            l_ref[...] = jnp.zeros_like(l_ref)

        e = h_ref[...] - t_ref[...]
        d_ref[...] = e * (1.0 / d)
        l_ref[...] += 0.5 * jnp.sum(jnp.mean(e * e, -1, keepdims=True))

    row = pl.BlockSpec((tr, d), lambda i: (i, 0))
    return pl.pallas_call(
        body, name=name, grid=(t // tr,), in_specs=[row, row],
        out_specs=[pl.BlockSpec((8, LANES), lambda i: (0, 0)), row],
        out_shape=[jax.ShapeDtypeStruct((8, LANES), F32), jax.ShapeDtypeStruct((t, d), F32)],
        compiler_params=_cparams(("arbitrary",)),
    )(h, tgt)


def _axpy(dr, du, name):
    t, d = dr.shape
    tr = _tile(t, 256)

    def body(a_ref, b_ref, o_ref):
        o_ref[...] = ALPHA * a_ref[...] + b_ref[...]

    row = pl.BlockSpec((tr, d), lambda i: (i, 0))
    return pl.pallas_call(
        body, name=name, grid=(t // tr,), in_specs=[row, row], out_specs=row,
        out_shape=jax.ShapeDtypeStruct((t, d), F32), compiler_params=_cparams(("parallel",)),
    )(dr, du)


CONV_TT = 512
CONV_TC = 512


HALO = 8


def _shift_down(cur, halo, s, row):
    if s == 0:
        return cur
    tt = cur.shape[0]
    edge = pltpu.roll(halo, s, 0)
    if tt > HALO:
        edge = jnp.concatenate([edge, jnp.zeros((tt - HALO, cur.shape[1]), cur.dtype)], axis=0)
    return jnp.where(row >= s, pltpu.roll(cur, s, 0), edge)


def _shift_up(cur, halo, s, row, tt):
    if s == 0:
        return cur
    edge = jnp.concatenate([jnp.zeros((tt - HALO, cur.shape[1]), cur.dtype), pltpu.roll(halo, HALO - s, 0)], axis=0)
    return jnp.where(row < tt - s, pltpu.roll(cur, tt - s, 0), edge)


def _conv_fwd(proj, col0, w, b, name, ride=None):
    t = proj.shape[0]
    c = w.shape[1]
    tt = _tile(t, CONV_TT)
    cb0 = col0 // CONV_TC

    def body(x_ref, p_ref, w_ref, b_ref, o_ref):
        i = pl.program_id(1)
        x = x_ref[...]
        p = jnp.where(i > 0, p_ref[...], 0.0)
        row = lax.broadcasted_iota(jnp.int32, x.shape, 0)
        pre = b_ref[...] + w_ref[3:4, :] * x
        for s in (1, 2, 3):
            pre = pre + w_ref[3 - s:4 - s, :] * _shift_down(x, p, s, row)
        o_ref[...] = _silu(pre)

    (act,), rode = _hosted_call(
        body, name=name, grid=(c // CONV_TC, t // tt),
        in_specs=[pl.BlockSpec((tt, CONV_TC), lambda j, i: (i, cb0 + j)),
                  pl.BlockSpec((HALO, CONV_TC), lambda j, i: (jnp.maximum(i * (tt // HALO) - 1, 0), cb0 + j)),
                  pl.BlockSpec((4, CONV_TC), lambda j, i: (0, j)),
                  pl.BlockSpec((1, CONV_TC), lambda j, i: (0, j))],
        out_specs=[pl.BlockSpec((tt, CONV_TC), lambda j, i: (i, j))],
        out_shape=[jax.ShapeDtypeStruct((t, c), F32)], scratch_shapes=[],
        sem=("parallel", "parallel"), args=(proj, proj, w, b), ride=ride)
    return act, rode


def _conv_bwd(proj, col0, w, b, dact, name):
    t = proj.shape[0]
    c = w.shape[1]
    tt = _tile(t, CONV_TT)
    nt = t // tt
    cb0 = col0 // CONV_TC
    per = tt // HALO

    def dpre_of(x, halo, d, w_ref, b_ref):
        row = lax.broadcasted_iota(jnp.int32, x.shape, 0)
        sh = [_shift_down(x, halo, s, row) for s in range(4)]
        pre = b_ref[...] + w_ref[3:4, :] * sh[0]
        for s in (1, 2, 3):
            pre = pre + w_ref[3 - s:4 - s, :] * sh[s]
        sg = jax.nn.sigmoid(pre)
        return d * (sg * (1.0 + pre * (1.0 - sg))), sh

    def body(x_ref, p_ref, n_ref, w_ref, b_ref, d_ref, dn_ref, dx_ref, dw_ref, db_ref):
        i = pl.program_id(1)

        @pl.when(i == 0)
        def _():
            dw_ref[...] = jnp.zeros_like(dw_ref)
            db_ref[...] = jnp.zeros_like(db_ref)

        x = x_ref[...]
        dpre, sh = dpre_of(x, jnp.where(i > 0, p_ref[...], 0.0), d_ref[...], w_ref, b_ref)
        dnext, _ = dpre_of(n_ref[...], x[tt - HALO:, :], jnp.where(i < nt - 1, dn_ref[...], 0.0), w_ref, b_ref)
        row = lax.broadcasted_iota(jnp.int32, x.shape, 0)
        acc = w_ref[3:4, :] * dpre
        for s in (1, 2, 3):
            acc = acc + w_ref[3 - s:4 - s, :] * _shift_up(dpre, dnext, s, row, tt)
        dx_ref[...] = acc
        for s in range(4):
            dw_ref[3 - s:4 - s, :] += jnp.sum(dpre * sh[s], 0, keepdims=True)
        db_ref[...] += jnp.sum(dpre, 0, keepdims=True)

    tile = lambda off: pl.BlockSpec((tt, CONV_TC), lambda j, i: (i, off + j))
    before = lambda off: pl.BlockSpec((HALO, CONV_TC), lambda j, i: (jnp.maximum(i * per - 1, 0), off + j))
    after = lambda off: pl.BlockSpec((HALO, CONV_TC), lambda j, i: (jnp.minimum((i + 1) * per, t // HALO - 1), off + j))
    wspec = lambda rows: pl.BlockSpec((rows, CONV_TC), lambda j, i: (0, j))
    return pl.pallas_call(
        body, name=name, grid=(c // CONV_TC, nt),
        in_specs=[tile(cb0), before(cb0), after(cb0), wspec(4), wspec(1), tile(0), after(0)],
        out_specs=[tile(0), wspec(4), wspec(1)],
        out_shape=[jax.ShapeDtypeStruct((t, c), F32), jax.ShapeDtypeStruct((4, c), F32), jax.ShapeDtypeStruct((1, c), F32)],
        compiler_params=_cparams(("parallel", "arbitrary")),
    )(proj, proj, proj, w, b, dact, dact)


SSD_GB = 4


def _split3(a):
    a1 = _mx(a)
    r = a - a1.astype(F32)
    a2 = _mx(r)
    return a1, a2, _mx(r - a2.astype(F32))


@jax.custom_vjp
def _sel_r(a, c):
    cm = _mx(c)
    p1, p2, p3 = _split3(a)
    return _dg(p1, cm, 1, 0) + (_dg(p2, cm, 1, 0) + _dg(p3, cm, 1, 0))


def _sel_r_f(a, c):
    return _sel_r(a, c), c


def _sel_r_b(c, ct):
    cm = _mx(c)
    p1, p2, p3 = _split3(ct)
    return _dg(p1, cm, 1, 1) + (_dg(p2, cm, 1, 1) + _dg(p3, cm, 1, 1)), jnp.zeros_like(c)


_sel_r.defvjp(_sel_r_f, _sel_r_b)


@jax.custom_vjp
def _sel_l(c, a):
    cm = _mx(c)
    p1, p2, p3 = _split3(a)
    return _dg(cm, p1, 1, 0) + (_dg(cm, p2, 1, 0) + _dg(cm, p3, 1, 0))


def _sel_l_f(c, a):
    return _sel_l(c, a), c


def _sel_l_b(c, ct):
    cm = _mx(c)
    p1, p2, p3 = _split3(ct)
    return jnp.zeros_like(c), _dg(cm, p1, 0, 0) + (_dg(cm, p2, 0, 0) + _dg(cm, p3, 0, 0))


_sel_l.defvjp(_sel_l_f, _sel_l_b)


def _ssd_chunk(gb, x, z, bm, cm, dtraw, dtb, alog, dsk, nw, prev):
    L = x.shape[1]
    r_i = lax.broadcasted_iota(jnp.int32, (L, L), 0)
    c_i = lax.broadcasted_iota(jnp.int32, (L, L), 1)
    causal = r_i >= c_i
    dt = _softplus(dtraw + dtb)
    a = dt * (-jnp.exp(alog))
    acs = _sel_l(causal.astype(F32), a)
    e_r = lax.broadcasted_iota(jnp.int32, (LANES, SSD_GS), 0)
    e_c = lax.broadcasted_iota(jnp.int32, (LANES, SSD_GS), 1)
    hpg = SSD_H // SSD_G
    sels = [(e_r == (gb * SSD_GB + i) * hpg + jnp.right_shift(e_c, 6)).astype(F32) for i in range(SSD_GB)]
    dt_x = _cat0([_sel_r(dt, s) for s in sels])
    acs_x = _cat0([_sel_r(acs, s) for s in sels])
    d_x = _cat0([_sel_r(jnp.broadcast_to(dsk, (8, LANES)), s)[0:1] for s in sels])
    last = lax.broadcasted_iota(jnp.int32, (L, 1), 0) == L - 1
    alast = jnp.sum(jnp.where(last, acs_x, 0.0), axis=1, keepdims=True)
    xdt = x * dt_x
    cb = _bnt(cm, bm)
    lane = lax.broadcasted_iota(jnp.int32, (L, LANES), 1)
    ys = []
    for j in range(SSD_GS // LANES):
        xp = xdt[:, :, j * LANES:(j + 1) * LANES]
        yp = None
        for hh in range(2):
            c0 = (2 * j + hh) * SSD_P
            cmx = jnp.broadcast_to(acs_x[:, :, c0:c0 + 1], (SSD_GB, L, L))
            dec = jnp.exp(jnp.where(causal, cmx - jnp.swapaxes(cmx, 1, 2), -jnp.inf))
            half = (lane < SSD_P) if hh == 0 else (lane >= SSD_P)
            t = _bnn(cb * dec, jnp.where(half, xp, 0.0))
            yp = t if yp is None else yp + t
        ys.append(yp)
    y_diag = jnp.concatenate(ys, axis=2)
    st = _btn(bm, xdt * jnp.exp(alast - acs_x))
    new = prev * jnp.exp(alast) + st
    y_off = _bnn(cm, prev) * jnp.exp(acs_x)
    y = y_diag + y_off + x * d_x
    yg = y * _silu(z)
    yn = yg * lax.rsqrt(jnp.mean(yg * yg, -1, keepdims=True) + RMS_EPS) * nw
    return yn, new


def _groups_of(ref, width):
    return _cat0([ref[:, i * width:(i + 1) * width] for i in range(SSD_GB)])


def _put_groups(ref, val, width):
    for i in range(SSD_GB):
        ref[:, i * width:(i + 1) * width] = val[i].astype(ref.dtype)


def _ssd_specs(nc, rev):
    cc = (lambda c: nc - 1 - c) if rev else (lambda c: c)
    wx, wb = SSD_GB * SSD_GS, SSD_GB * SSD_N
    dtb = (SSD_DI + SSD_CONV_DIM) // LANES
    bb = SSD_DI // wb
    cbk = (SSD_DI + SSD_G * SSD_N) // wb
    par = pl.BlockSpec((1, LANES), lambda c, g: (0, 0))
    return dict(
        z=pl.BlockSpec((SSD_L, wx), lambda c, g: (cc(c), g)),
        dt=pl.BlockSpec((SSD_L, LANES), lambda c, g: (cc(c), dtb)),
        x=pl.BlockSpec((SSD_L, wx), lambda c, g: (cc(c), g)),
        bm=pl.BlockSpec((SSD_L, wb), lambda c, g: (cc(c), bb + g)),
        cm=pl.BlockSpec((SSD_L, wb), lambda c, g: (cc(c), cbk + g)),
        par=par,
        nw=pl.BlockSpec((1, wx), lambda c, g: (0, g)),
        st=pl.BlockSpec((1, SSD_GB, SSD_N, SSD_GS), lambda c, g: (cc(c), g, 0, 0)),
        y=pl.BlockSpec((SSD_L, wx), lambda c, g: (cc(c), g)),
        bc=pl.BlockSpec((SSD_L, wb), lambda c, g: (cc(c), g)),
        dtout=pl.BlockSpec((SSD_L, LANES), lambda c, g: (cc(c), 0)),
    )


def _ssd_fwd(proj, act, dtb, alog, dsk, nw, name, ride=None):
    t = proj.shape[0]
    nc = t // SSD_L
    sp = _ssd_specs(nc, False)

    def body(z_ref, dt_ref, x_ref, bm_ref, cm_ref, dtb_ref, alog_ref, dsk_ref, nw_ref, y_ref, st_ref, state):
        c, g = pl.program_id(0), pl.program_id(1)
        mine = pl.ds(g * SSD_GB, SSD_GB)

        @pl.when(c == 0)
        def _():
            state[mine] = jnp.zeros((SSD_GB, SSD_N, SSD_GS), F32)

        prev = state[mine]
        st_ref[0] = prev
        yn, new = _ssd_chunk(g, _groups_of(x_ref, SSD_GS), _groups_of(z_ref, SSD_GS), _groups_of(bm_ref, SSD_N),
                             _groups_of(cm_ref, SSD_N), dt_ref[...], dtb_ref[...], alog_ref[...], dsk_ref[...],
                             _groups_of(nw_ref, SSD_GS), prev)
        _put_groups(y_ref, yn, SSD_GS)
        state[mine] = new

    return _hosted_call(
        body, name=name, grid=(nc, SSD_G // SSD_GB),
        in_specs=[sp["z"], sp["dt"], sp["x"], sp["bm"], sp["cm"], sp["par"], sp["par"], sp["par"], sp["nw"]],
        out_specs=[sp["y"], sp["st"]],
        out_shape=[jax.ShapeDtypeStruct((t, SSD_DI), MXU), jax.ShapeDtypeStruct((nc, SSD_G, SSD_N, SSD_GS), F32)],
        scratch_shapes=[pltpu.VMEM((SSD_G, SSD_N, SSD_GS), F32)],
        sem=("arbitrary", "arbitrary"), args=(proj, proj, act, act, act, dtb, alog, dsk, nw), ride=ride)


def _ssd_bwd(proj, act, dtb, alog, dsk, nw, states, dyn, name, ride=None):
    t = proj.shape[0]
    nc = t // SSD_L
    sp = _ssd_specs(nc, True)

    def body(z_ref, dt_ref, x_ref, bm_ref, cm_ref, dtb_ref, alog_ref, dsk_ref, nw_ref, st_ref, dy_ref,
             dact_ref, dz_ref, ddt_ref, ddtb_ref, dalog_ref, ddsk_ref, dnw_ref, dstate):
        c, g = pl.program_id(0), pl.program_id(1)
        mine = pl.ds(g * SSD_GB, SSD_GB)

        @pl.when(c == 0)
        def _():
            dstate[mine] = jnp.zeros((SSD_GB, SSD_N, SSD_GS), F32)

        @pl.when((c == 0) & (g == 0))
        def _():
            ddtb_ref[...] = jnp.zeros_like(ddtb_ref)
            dalog_ref[...] = jnp.zeros_like(dalog_ref)
            ddsk_ref[...] = jnp.zeros_like(ddsk_ref)
            dnw_ref[...] = jnp.zeros_like(dnw_ref)

        @pl.when(g == 0)
        def _():
            ddt_ref[...] = jnp.zeros_like(ddt_ref)

        _, vjp = jax.vjp(functools.partial(_ssd_chunk, g), _groups_of(x_ref, SSD_GS), _groups_of(z_ref, SSD_GS),
                         _groups_of(bm_ref, SSD_N), _groups_of(cm_ref, SSD_N), dt_ref[...], dtb_ref[...], alog_ref[...],
                         dsk_ref[...], _groups_of(nw_ref, SSD_GS), st_ref[0])
        dx, dz, dbm, dcm, ddt, ddtb, dalog, ddsk, dnw, dprev = vjp((_groups_of(dy_ref, SSD_GS), dstate[mine]))
        _put_groups(dz_ref, dz, SSD_GS)
        for gi in range(SSD_G // SSD_GB):
            @pl.when(g == gi)
            def _(gi=gi):
                for i in range(SSD_GB):
                    gg = gi * SSD_GB + i
                    dact_ref[:, gg * SSD_GS:(gg + 1) * SSD_GS] = dx[i]
                    dact_ref[:, SSD_DI + gg * SSD_N:SSD_DI + (gg + 1) * SSD_N] = dbm[i]
                    dact_ref[:, SSD_DI + (SSD_G + gg) * SSD_N:SSD_DI + (SSD_G + gg + 1) * SSD_N] = dcm[i]
        ddt_ref[...] += ddt
        ddtb_ref[...] += ddtb
        dalog_ref[...] += dalog
        ddsk_ref[...] += ddsk
        dnw_ref[mine] += dnw
        dstate[mine] = dprev

    par_out = pl.BlockSpec((1, LANES), lambda c, g: (0, 0))
    sds = jax.ShapeDtypeStruct
    return _hosted_call(
        body, name=name, grid=(nc, SSD_G // SSD_GB),
        in_specs=[sp["z"], sp["dt"], sp["x"], sp["bm"], sp["cm"], sp["par"], sp["par"], sp["par"], sp["nw"],
                  sp["st"], sp["y"]],
        out_specs=[pl.BlockSpec((SSD_L, SSD_CONV_DIM), lambda c, g: (nc - 1 - c, 0)), sp["y"], sp["dtout"],
                   par_out, par_out, par_out, pl.BlockSpec((SSD_G, 1, SSD_GS), lambda c, g: (0, 0, 0))],
        out_shape=[sds((t, SSD_CONV_DIM), F32), sds((t, SSD_DI), F32), sds((t, LANES), F32), sds((1, LANES), F32),
                   sds((1, LANES), F32), sds((1, LANES), F32), sds((SSD_G, 1, SSD_GS), F32)],
        scratch_shapes=[pltpu.VMEM((SSD_G, SSD_N, SSD_GS), F32)],
        sem=("arbitrary", "arbitrary"), args=(proj, proj, act, act, act, dtb, alog, dsk, nw, states, dyn), ride=ride)


def _pad_lanes(v, width=LANES, offset=0):
    return jnp.pad(v.astype(F32), (offset, width - offset - v.shape[0])).reshape(1, width)


def _ride_of(rides, key, scatter):
    arrs = (rides or {}).get(key)
    return (arrs, scatter) if arrs else None


def _mm_r(a, b, ride, **kw):
    out = _mm(a, b, ride=ride, **kw)
    return (out, []) if ride is None else out


def _ssd_layer_fwd(u, w, ln_g, ln_b, tag, rides=None, late=None):
    w_in = jnp.concatenate([w["in_w"], jnp.zeros((D_MODEL, LANES - SSD_H), w["in_w"].dtype)], axis=1)
    dtb, alog, dsk = _pad_lanes(w["dt_bias"]), _pad_lanes(w["a_log"]), _pad_lanes(w["d"])
    if late is None:
        proj = _mm(u.astype(MXU), w_in, name=tag + "_in")
    else:
        proj, got = _mm(u.astype(MXU), w_in, name=tag + "_in", ride=(late[0], False))
        w.update(late[1](got))
    nw = w["norm_w"].reshape(1, SSD_DI)
    cb = w["conv_b"].reshape(1, SSD_CONV_DIM)
    rode = {}
    act, rode["conv"] = _conv_fwd(proj, SSD_DI, w["conv_w"], cb, name=tag + "_conv", ride=_ride_of(rides, "conv", False))
    (yn, states), rode["scan"] = _ssd_fwd(proj, act, dtb, alog, dsk, nw, name=tag + "_scan", ride=_ride_of(rides, "scan", False))
    y, rode["out"] = _mm_r(yn, w["out_w"], _ride_of(rides, "out", False), name=tag + "_out")
    h = _ln_fwd(u, y, ln_g, ln_b, name=tag + "_ln")
    saved = dict(u=u, w_in=w_in, proj=proj, act=act, states=states, yn=yn, y=y, dtb=dtb, alog=alog, dsk=dsk, nw=nw, cb=cb)
    return h, saved, rode


def _ssd_layer_bwd(s, w, ln_g, dr_up, du_up, tag, rides=None, own=None):
    dr, dg, db = _ln_bwd(s["u"], s["y"], ln_g, dr_up, du_up, name=tag + "_ln_b")
    dyn = _mm(dr, w["out_w"], tb=True, name=tag + "_out_bx")
    d_out_w = _mm(s["yn"], dr, ta=True, out_dtype=WIRE, name=tag + "_out_bw")
    rode = {}
    (dact, dz, ddt, ddtb, dalog, ddsk, dnw), rode["scan_b"] = _ssd_bwd(
        s["proj"], s["act"], s["dtb"], s["alog"], s["dsk"], s["nw"], s["states"], dyn, name=tag + "_scan_b",
        ride=_ride_of(rides, "scan_b", True))
    dxbc, d_conv_w, d_conv_b = _conv_bwd(s["proj"], SSD_DI, w["conv_w"], s["cb"], dact, name=tag + "_conv_b")
    dproj = jnp.concatenate([dz, dxbc, ddt], axis=1).astype(MXU)
    grads = dict(conv_w=d_conv_w, conv_b=d_conv_b.reshape(-1), dt_bias=ddtb[0, :SSD_H], a_log=dalog[0, :SSD_H],
                 d=ddsk[0, :SSD_H], norm_w=dnw.reshape(-1), out_w=d_out_w, ln_g=dg[0], ln_b=db[0])
    late_names = ("conv_w", "conv_b", "norm_w", "out_w")
    got = {}
    if own is None:
        d_in_w = _mm(s["u"].astype(MXU), dproj, ta=True, out_dtype=WIRE, name=tag + "_in_bw")
        du = _mm(dproj, s["w_in"], tb=True, name=tag + "_in_bx")
        grads["in_w"] = d_in_w[:, :SSD_DI + SSD_CONV_DIM + SSD_H]
    else:
        d_in_w, late = _mm(s["u"].astype(MXU), dproj, ta=True, out_dtype=WIRE, name=tag + "_in_bw",
                           ride=([own(k, grads[k]) for k in late_names], True))
        got = dict(zip(late_names, late))
        grads["in_w"] = d_in_w[:, :SSD_DI + SSD_CONV_DIM + SSD_H]
        du, (got["in_w"],) = _mm(dproj, s["w_in"], tb=True, name=tag + "_in_bx", ride=([own("in_w", grads["in_w"])], True))
    return dr, du, grads, rode, got


MLA_LOW = MLA_QR + MLA_KVR + LANES
MLA_ZB = MLA_LOW // LANES


def _rope_mat():
    r = lax.broadcasted_iota(jnp.int32, (LANES, LANES), 0)
    c = lax.broadcasted_iota(jnp.int32, (LANES, LANES), 1)
    hf = MLA_ROPE // 2
    return jnp.where((c < hf) & (r == c + hf), -1.0, 0.0) + jnp.where((c >= hf) & (c < 2 * hf) & (r == c - hf), 1.0, 0.0)


def _rope(x, cosf, sinf):
    return x * cosf + _sel_r(x, _rope_mat()) * sinf


def _rope_adj(d, cosf, sinf):
    return d * cosf - _sel_r(d * sinf, _rope_mat())


def _mla_low_fn(low, qnw, kvnw, cosf, sinf):
    qc, kvc, kr = low[:, :MLA_QR], low[:, MLA_QR:MLA_QR + MLA_KVR], low[:, MLA_QR + MLA_KVR:]
    qn = qc * lax.rsqrt(jnp.mean(qc * qc, -1, keepdims=True) + RMS_EPS) * qnw
    kvn = kvc * lax.rsqrt(jnp.mean(kvc * kvc, -1, keepdims=True) + RMS_EPS) * kvnw
    return qn, kvn, _rope(kr, cosf, sinf)


def _mla_low_fwd(proj, qnw, kvnw, cosf, sinf, name):
    t = proj.shape[0]
    tr = _tile(t, 256)

    def body(low_ref, qnw_ref, kvnw_ref, cos_ref, sin_ref, qn_ref, kvn_ref, kr_ref):
        qn, kvn, kr = _mla_low_fn(low_ref[...], qnw_ref[...], kvnw_ref[...], cos_ref[...], sin_ref[...])
        qn_ref[...] = qn
        kvn_ref[...] = kvn
        kr_ref[...] = kr

    row = lambda wdt: pl.BlockSpec((tr, wdt), lambda i: (i, 0))
    par = lambda wdt: pl.BlockSpec((1, wdt), lambda i: (0, 0))
    sds = jax.ShapeDtypeStruct
    return pl.pallas_call(
        body, name=name, grid=(t // tr,),
        in_specs=[row(MLA_LOW), par(MLA_QR), par(MLA_KVR), row(LANES), row(LANES)],
        out_specs=[row(MLA_QR), row(MLA_KVR), row(LANES)],
        out_shape=[sds((t, MLA_QR), F32), sds((t, MLA_KVR), F32), sds((t, LANES), F32)],
        compiler_params=_cparams(("parallel",)),
    )(proj, qnw, kvnw, cosf, sinf)


def _mla_low_bwd(proj, qnw, kvnw, cosf, sinf, dqn, dkvn, dkr, name):
    t = proj.shape[0]
    tr = _tile(t, 256)

    def body(low_ref, qnw_ref, kvnw_ref, cos_ref, sin_ref, dqn_ref, dkvn_ref, dkr_ref, dlow_ref, dqnw_ref, dkvnw_ref):
        i = pl.program_id(0)

        @pl.when(i == 0)
        def _():
            dqnw_ref[...] = jnp.zeros_like(dqnw_ref)
            dkvnw_ref[...] = jnp.zeros_like(dkvnw_ref)

        cosf, sinf = cos_ref[...], sin_ref[...]
        _, vjp = jax.vjp(lambda a, b, c: _mla_low_fn(a, b, c, cosf, sinf), low_ref[...], qnw_ref[...], kvnw_ref[...])
        dlow, dq, dk = vjp((dqn_ref[...], dkvn_ref[...], dkr_ref[...]))
        dlow_ref[...] = dlow
        dqnw_ref[...] += dq
        dkvnw_ref[...] += dk

    row = lambda wdt: pl.BlockSpec((tr, wdt), lambda i: (i, 0))
    par = lambda wdt: pl.BlockSpec((1, wdt), lambda i: (0, 0))
    sds = jax.ShapeDtypeStruct
    return pl.pallas_call(
        body, name=name, grid=(t // tr,),
        in_specs=[row(MLA_LOW), par(MLA_QR), par(MLA_KVR), row(LANES), row(LANES), row(MLA_QR), row(MLA_KVR), row(LANES)],
        out_specs=[row(MLA_LOW), par(MLA_QR), par(MLA_KVR)],
        out_shape=[sds((t, MLA_LOW), F32), sds((1, MLA_QR), F32), sds((1, MLA_KVR), F32)],
        compiler_params=_cparams(("arbitrary",)),
    )(proj, qnw, kvnw, cosf, sinf, dqn, dkvn, dkr)


def _rope_heads(x, col_blk0, cosf, sinf, adjoint, name):
    t = x.shape[0]
    tr = _tile(t, 512)
    hb = 4
    assert col_blk0 % hb == 0

    def body(x_ref, cos_ref, sin_ref, o_ref):
        f = _rope_adj if adjoint else _rope
        for b in range(hb):
            o_ref[:, b * LANES:(b + 1) * LANES] = f(x_ref[:, b * LANES:(b + 1) * LANES], cos_ref[...], sin_ref[...])

    tab = pl.BlockSpec((tr, LANES), lambda i, h: (i, 0))
    return pl.pallas_call(
        body, name=name, grid=(t // tr, MLA_H // hb),
        in_specs=[pl.BlockSpec((tr, hb * LANES), lambda i, h: (i, col_blk0 // hb + h)), tab, tab],
        out_specs=pl.BlockSpec((tr, hb * LANES), lambda i, h: (i, h)),
        out_shape=jax.ShapeDtypeStruct((t, MLA_H * LANES), F32), compiler_params=_cparams(("parallel", "parallel")),
    )(x, cosf, sinf)


ATT_HB = 4
ATT_W = ATT_HB * LANES


def _att_qk(qn_ref, qr_ref, kn_ref, kr_ref):
    q2 = jnp.concatenate([_heads_of(qn_ref, ATT_HB), _heads_of(qr_ref, ATT_HB)], axis=2)
    kr = kr_ref[...]
    k2 = jnp.concatenate([_heads_of(kn_ref, ATT_HB), jnp.broadcast_to(kr[None], (ATT_HB,) + kr.shape)], axis=2)
    return q2, k2


def _att_scores(q2, k2, masked):
    s = _bdg(_mx(q2 * MLA_SCALE), _mx(k2), 2, 2)
    if masked:
        tq, tk = s.shape[1:]
        s = jnp.where(lax.broadcasted_iota(jnp.int32, (tq, tk), 1) <= lax.broadcasted_iota(jnp.int32, (tq, tk), 0), s, -jnp.inf)
    return s


def _on_causal_blocks(q_blk, k_blk, step):
    @pl.when(k_blk < q_blk)
    def _():
        step(False)

    @pl.when(k_blk == q_blk)
    def _():
        step(True)


def _put_heads(ref, val):
    for b in range(val.shape[0]):
        ref[:, b * LANES:(b + 1) * LANES] = val[b].astype(ref.dtype)


def _att_ds(s, v_ref, o_ref, do_ref, lse_ref):
    do = _heads_of(do_ref, ATT_HB)
    p = jnp.exp(s - _heads_of(lse_ref, ATT_HB)[:, :, 0:1])
    dp = _bdg(_mx(do), _mx(_heads_of(v_ref, ATT_HB)), 2, 2)
    dl = jnp.sum(do * _heads_of(o_ref, ATT_HB), -1, keepdims=True)
    return _mx(p), _mx(p * (dp - dl) * MLA_SCALE), do


def _attn_fwd(q, qr, kv, kr, proj, name, ride=None):
    t = q.shape[0]
    tq = tk = _tile(t, ATT_BLK)
    nq = nk = t // tq

    def body(qn_ref, qr_ref, kn_ref, kr_ref, v_ref, *rest):
        z_refs, (o_ref, og_ref, lse_ref, m_s, l_s, acc_s) = rest[:ATT_HB], rest[ATT_HB:]
        i, j = pl.program_id(1), pl.program_id(2)

        @pl.when(j == 0)
        def _():
            m_s[...] = jnp.full_like(m_s, -jnp.inf)
            l_s[...] = jnp.zeros_like(l_s)
            acc_s[...] = jnp.zeros_like(acc_s)

        def step(masked):
            s = _att_scores(*_att_qk(qn_ref, qr_ref, kn_ref, kr_ref), masked)
            m_new = jnp.maximum(m_s[...], jnp.max(s, -1, keepdims=True))
            p = jnp.exp(s - m_new)
            corr = jnp.exp(m_s[...] - m_new)
            l_s[...] = corr * l_s[...] + jnp.sum(p, -1, keepdims=True)
            acc_s[...] = corr * acc_s[...] + _bdg(_mx(p), _mx(_heads_of(v_ref, ATT_HB)), 2, 1)
            m_s[...] = m_new

        _on_causal_blocks(i, j, step)

        @pl.when(j == nk - 1)
        def _():
            o = acc_s[...] / l_s[...]
            _put_heads(o_ref, o)
            lse = m_s[...] + jnp.log(l_s[...])
            for b in range(ATT_HB):
                og_ref[:, b * LANES:(b + 1) * LANES] = (o[b] * _silu(z_refs[b][...])).astype(og_ref.dtype)
                lse_ref[:, b * LANES:(b + 1) * LANES] = jnp.broadcast_to(lse[b], (tq, LANES))

    qs = lambda off: pl.BlockSpec((tq, ATT_W), lambda h, i, j: (i, off // ATT_HB + h))
    ks = lambda off: pl.BlockSpec((tk, ATT_W), lambda h, i, j: (jnp.minimum(j, i), off // ATT_HB + h))
    zs = [pl.BlockSpec((tq, LANES), functools.partial(lambda b, h, i, j: (i, MLA_ZB + h * ATT_HB + b), b)) for b in range(ATT_HB)]
    sds = jax.ShapeDtypeStruct
    return _hosted_call(
        body, name=name, grid=(MLA_H // ATT_HB, nq, nk),
        in_specs=[qs(0), qs(0), ks(0), pl.BlockSpec((tk, LANES), lambda h, i, j: (jnp.minimum(j, i), 0)), ks(MLA_H)] + zs,
        out_specs=[qs(0), qs(0), qs(0)],
        out_shape=[sds((t, MLA_GATE), F32), sds((t, MLA_GATE), MXU), sds((t, MLA_H * LANES), F32)],
        scratch_shapes=[pltpu.VMEM((ATT_HB, tq, 1), F32), pltpu.VMEM((ATT_HB, tq, 1), F32), pltpu.VMEM((ATT_HB, tq, LANES), F32)],
        sem=("parallel", "parallel", "arbitrary"), args=(q, qr, kv, kr, kv) + (proj,) * ATT_HB, ride=ride)


def _gate_bwd(dog, o, proj, name):
    t = o.shape[0]
    tr = _tile(t, 512)

    def body(d_ref, o_ref, z_ref, do_ref, dz_ref):
        z = z_ref[...]
        sg = jax.nn.sigmoid(z)
        d = d_ref[...]
        do_ref[...] = d * z * sg
        dz_ref[...] = d * o_ref[...] * (sg * (1.0 + z * (1.0 - sg)))

    blk = lambda off: pl.BlockSpec((tr, 512), lambda i, j: (i, off + j))
    assert MLA_LOW % 512 != 0 or True
    zspec = pl.BlockSpec((tr, LANES), lambda i, j: (i, MLA_ZB + j))
    b128 = pl.BlockSpec((tr, LANES), lambda i, j: (i, j))
    sds = jax.ShapeDtypeStruct
    return pl.pallas_call(
        body, name=name, grid=(t // tr, MLA_GATE // LANES),
        in_specs=[b128, b128, zspec], out_specs=[b128, b128],
        out_shape=[sds((t, MLA_GATE), F32), sds((t, MLA_GATE), F32)],
        compiler_params=_cparams(("parallel", "parallel")),
    )(dog, o, proj)


def _attn_bwd_q(q, qr, kv, kr, o, do, lse, name, ride=None):
    t = q.shape[0]
    tq = tk = _tile(t, ATT_BLK)
    nq = nk = t // tq

    def body(qn_ref, qr_ref, kn_ref, kr_ref, v_ref, o_ref, do_ref, lse_ref, dqn_ref, dqr_ref, an_s, ar_s):
        i, j = pl.program_id(1), pl.program_id(2)

        @pl.when(j == 0)
        def _():
            an_s[...] = jnp.zeros_like(an_s)
            ar_s[...] = jnp.zeros_like(ar_s)

        def step(masked):
            q2, k2 = _att_qk(qn_ref, qr_ref, kn_ref, kr_ref)
            ds = _att_ds(_att_scores(q2, k2, masked), v_ref, o_ref, do_ref, lse_ref)[1]
            dq2 = _bdg(ds, _mx(k2), 2, 1)
            an_s[...] += dq2[:, :, :LANES]
            ar_s[...] += dq2[:, :, LANES:]

        _on_causal_blocks(i, j, step)

        @pl.when(j == nk - 1)
        def _():
            _put_heads(dqn_ref, an_s[...])
            _put_heads(dqr_ref, ar_s[...])

    qs = lambda off: pl.BlockSpec((tq, ATT_W), lambda h, i, j: (i, off // ATT_HB + h))
    ks = lambda off: pl.BlockSpec((tk, ATT_W), lambda h, i, j: (jnp.minimum(j, i), off // ATT_HB + h))
    sds = jax.ShapeDtypeStruct
    return _hosted_call(
        body, name=name, grid=(MLA_H // ATT_HB, nq, nk),
        in_specs=[qs(0), qs(0), ks(0), pl.BlockSpec((tk, LANES), lambda h, i, j: (jnp.minimum(j, i), 0)), ks(MLA_H),
                  qs(0), qs(0), qs(0)],
        out_specs=[qs(0), qs(0)],
        out_shape=[sds((t, MLA_H * LANES), F32), sds((t, MLA_H * LANES), F32)],
        scratch_shapes=[pltpu.VMEM((ATT_HB, tq, LANES), F32), pltpu.VMEM((ATT_HB, tq, LANES), F32)],
        sem=("parallel", "parallel", "arbitrary"), args=(q, qr, kv, kr, kv, o, do, lse), ride=ride)


def _attn_bwd_kv(q, qr, kv, kr, o, do, lse, name, ride=None):
    t = q.shape[0]
    tq = tk = _tile(t, ATT_BLK)
    nq = nk = t // tq

    def body(qn_ref, qr_ref, kn_ref, kr_ref, v_ref, o_ref, do_ref, lse_ref, dkn_ref, dv_ref, dkr_ref, akn_s, av_s):
        j, h, i = pl.program_id(0), pl.program_id(1), pl.program_id(2)

        @pl.when((h == 0) & (i == 0))
        def _():
            dkr_ref[...] = jnp.zeros_like(dkr_ref)

        @pl.when(i == 0)
        def _():
            akn_s[...] = jnp.zeros_like(akn_s)
            av_s[...] = jnp.zeros_like(av_s)

        def step(masked):
            q2, k2 = _att_qk(qn_ref, qr_ref, kn_ref, kr_ref)
            p, ds, do = _att_ds(_att_scores(q2, k2, masked), v_ref, o_ref, do_ref, lse_ref)
            av_s[...] += _bdg(p, _mx(do), 1, 1)
            dk2 = _bdg(ds, _mx(q2), 1, 1)
            akn_s[...] += dk2[:, :, :LANES]
            dkr_ref[...] += jnp.sum(dk2[:, :, LANES:], axis=0)

        _on_causal_blocks(i, j, step)

        @pl.when(i == nq - 1)
        def _():
            _put_heads(dkn_ref, akn_s[...])
            _put_heads(dv_ref, av_s[...])

    qs = lambda off: pl.BlockSpec((tq, ATT_W), lambda j, h, i: (jnp.maximum(i, j), off // ATT_HB + h))
    ks = lambda off: pl.BlockSpec((tk, ATT_W), lambda j, h, i: (j, off // ATT_HB + h))
    sds = jax.ShapeDtypeStruct
    return _hosted_call(
        body, name=name, grid=(nk, MLA_H // ATT_HB, nq),
        in_specs=[qs(0), qs(0), ks(0), pl.BlockSpec((tk, LANES), lambda j, h, i: (j, 0)), ks(MLA_H), qs(0), qs(0), qs(0)],
        out_specs=[ks(0), ks(0), pl.BlockSpec((tk, LANES), lambda j, h, i: (j, 0))],
        out_shape=[sds((t, MLA_H * LANES), F32), sds((t, MLA_H * LANES), F32), sds((t, LANES), F32)],
        scratch_shapes=[pltpu.VMEM((ATT_HB, tk, LANES), F32), pltpu.VMEM((ATT_HB, tk, LANES), F32)],
        sem=("parallel", "arbitrary", "arbitrary"), args=(q, qr, kv, kr, kv, o, do, lse), ride=ride)


def _rope_tables(positions):
    lane = jnp.arange(LANES)
    valid = lane < MLA_ROPE
    inv_freq = ROPE_THETA ** (-(2 * (lane % (MLA_ROPE // 2))).astype(F32) / MLA_ROPE)
    ang = positions.astype(F32)[:, None] * inv_freq[None, :]
    return jnp.where(valid, jnp.cos(ang), 0.0), jnp.where(valid, jnp.sin(ang), 0.0)


def _mla_weights(w):
    dt = w["in_w"].dtype
    iw = w["in_w"]
    c1 = MLA_QR + MLA_KVR + MLA_ROPE
    w_in = jnp.concatenate([iw[:, :c1], jnp.zeros((D_MODEL, LANES - MLA_ROPE), dt), iw[:, c1:]], axis=1)
    qu = w["q_up_w"].reshape(MLA_QR, MLA_H, MLA_NOPE + MLA_ROPE)
    qrope = jnp.concatenate([qu[:, :, MLA_NOPE:], jnp.zeros((MLA_QR, MLA_H, LANES - MLA_ROPE), dt)], axis=2)
    w_q = jnp.concatenate([qu[:, :, :MLA_NOPE].reshape(MLA_QR, -1), qrope.reshape(MLA_QR, -1)], axis=1)
    kvu = w["kv_up_w"].reshape(MLA_KVR, MLA_H, MLA_NOPE + MLA_V)
    w_kv = jnp.concatenate([kvu[:, :, :MLA_NOPE].reshape(MLA_KVR, -1), kvu[:, :, MLA_NOPE:].reshape(MLA_KVR, -1)], axis=1)
    return w_in, w_q, w_kv


def _mla_layer_fwd(u, w, ln_g, ln_b, cosf, sinf, tag, rides=None):
    w_in, w_q, w_kv = _mla_weights(w)
    qnw, kvnw = w["q_norm_w"].reshape(1, -1), w["kv_norm_w"].reshape(1, -1)
    proj = _mm(u.astype(MXU), w_in, name=tag + "_in")
    qn, kvn, kr = _mla_low_fwd(proj, qnw, kvnw, cosf, sinf, name=tag + "_low")
    q = _mm(qn, w_q, name=tag + "_qup")
    kv = _mm(kvn, w_kv, name=tag + "_kvup")
    qr = _rope_heads(q, MLA_H, cosf, sinf, False, name=tag + "_qrope")
    rode = {}
    (o, og, lse), rode["attn"] = _attn_fwd(q, qr, kv, kr, proj, name=tag + "_attn", ride=_ride_of(rides, "attn", False))
    y = _mm(og, w["out_w"], name=tag + "_out")
    h = _ln_fwd(u, y, ln_g, ln_b, name=tag + "_ln")
    saved = dict(u=u, w_in=w_in, w_q=w_q, w_kv=w_kv, qnw=qnw, kvnw=kvnw, proj=proj, qn=qn, kvn=kvn, kr=kr, q=q, kv=kv,
                 qr=qr, o=o, og=og, lse=lse, y=y)
    return h, saved, rode


def _mla_layer_bwd(s, w, ln_g, cosf, sinf, dr_up, du_up, tag, rides=None):
    dr, dg, db = _ln_bwd(s["u"], s["y"], ln_g, dr_up, du_up, name=tag + "_ln_b")
    dog = _mm(dr, w["out_w"], tb=True, name=tag + "_out_bx")
    d_out_w = _mm(s["og"], dr, ta=True, out_dtype=WIRE, name=tag + "_out_bw")
    do, dz = _gate_bwd(dog, s["o"], s["proj"], name=tag + "_gate_b")
    rode = {}
    (dqn_h, dqr_rot), rode["attn_bq"] = _attn_bwd_q(s["q"], s["qr"], s["kv"], s["kr"], s["o"], do, s["lse"],
                                                    name=tag + "_attn_bq", ride=_ride_of(rides, "attn_bq", True))
    (dkn_h, dv_h, dkr_rot), rode["attn_bkv"] = _attn_bwd_kv(s["q"], s["qr"], s["kv"], s["kr"], s["o"], do, s["lse"],
                                                           name=tag + "_attn_bkv", ride=_ride_of(rides, "attn_bkv", True))
    dqr = _rope_heads(dqr_rot, 0, cosf, sinf, True, name=tag + "_qrope_b")
    dq = jnp.concatenate([dqn_h, dqr], axis=1).astype(MXU)
    dkv = jnp.concatenate([dkn_h, dv_h], axis=1).astype(MXU)
    d_wq = _mm(s["qn"], dq, ta=True, out_dtype=WIRE, name=tag + "_qup_bw")
    dqn = _mm(dq, s["w_q"], tb=True, name=tag + "_qup_bx")
    d_wkv = _mm(s["kvn"], dkv, ta=True, out_dtype=WIRE, name=tag + "_kvup_bw")
    dkvn = _mm(dkv, s["w_kv"], tb=True, name=tag + "_kvup_bx")
    dlow, dqnw, dkvnw = _mla_low_bwd(s["proj"], s["qnw"], s["kvnw"], cosf, sinf, dqn, dkvn, dkr_rot, name=tag + "_low_b")
    dproj = jnp.concatenate([dlow, dz], axis=1).astype(MXU)
    d_in = _mm(s["u"].astype(MXU), dproj, ta=True, out_dtype=WIRE, name=tag + "_in_bw")
    du = _mm(dproj, s["w_in"], tb=True, name=tag + "_in_bx")
    c1 = MLA_QR + MLA_KVR + MLA_ROPE
    d_in_w = jnp.concatenate([d_in[:, :c1], d_in[:, MLA_LOW:]], axis=1)
    dq3n = d_wq[:, :MLA_H * MLA_NOPE].reshape(MLA_QR, MLA_H, MLA_NOPE)
    dq3r = d_wq[:, MLA_H * MLA_NOPE:].reshape(MLA_QR, MLA_H, LANES)[:, :, :MLA_ROPE]
    d_q_up = jnp.concatenate([dq3n, dq3r], axis=2).reshape(MLA_QR, -1)
    dkv3 = d_wkv.reshape(MLA_KVR, 2, MLA_H, MLA_NOPE)
    d_kv_up = jnp.concatenate([dkv3[:, 0], dkv3[:, 1]], axis=2).reshape(MLA_KVR, -1)
    grads = dict(in_w=d_in_w, q_norm_w=dqnw[0], q_up_w=d_q_up, kv_norm_w=dkvnw[0], kv_up_w=d_kv_up, out_w=d_out_w,
                 ln_g=dg[0], ln_b=db[0])
    return dr, du, grads, rode


GDN_REP = GDN_HV // GDN_HK
GDN_A_LANE = GDN_HV
GDN_HPB = 8
GDN_VPB = GDN_HPB * GDN_REP


def _bdg(a, b, ca, cb):
    return lax.dot_general(a, b, (((ca,), (cb,)), ((0,), (0,))), preferred_element_type=F32)


@jax.custom_vjp
def _bnn(a, b):
    return _bdg(_mx(a), _mx(b), 2, 1)


def _bnn_f(a, b):
    return _bnn(a, b), (a, b)


def _bnn_b(res, ct):
    a, b = res
    return _bdg(_mx(ct), _mx(b), 2, 2), _bdg(_mx(a), _mx(ct), 1, 1)


_bnn.defvjp(_bnn_f, _bnn_b)


@jax.custom_vjp
def _bnt(a, b):
    return _bdg(_mx(a), _mx(b), 2, 2)


def _bnt_f(a, b):
    return _bnt(a, b), (a, b)


def _bnt_b(res, ct):
    a, b = res
    return _bdg(_mx(ct), _mx(b), 2, 1), _bdg(_mx(ct), _mx(a), 1, 1)


_bnt.defvjp(_bnt_f, _bnt_b)


@jax.custom_vjp
def _btn(a, b):
    return _bdg(_mx(a), _mx(b), 1, 1)


def _btn_f(a, b):
    return _btn(a, b), (a, b)


def _btn_b(res, ct):
    a, b = res
    return _bdg(_mx(b), _mx(ct), 2, 2), _bdg(_mx(a), _mx(ct), 2, 1)


_btn.defvjp(_btn_f, _btn_b)


def _h3(a, b, ca=2, cb=1):
    ah, bh = _mx(a), _mx(b)
    al, bl = _mx(a - ah.astype(F32)), _mx(b - bh.astype(F32))
    return _bdg(ah, bh, ca, cb) + (_bdg(ah, bl, ca, cb) + _bdg(al, bh, ca, cb))


@jax.custom_vjp
def _neumann_inverse(x):
    L = x.shape[-1]
    eye = (lax.broadcasted_iota(jnp.int32, (L, L), 0) == lax.broadcasted_iota(jnp.int32, (L, L), 1)).astype(F32)
    inv = eye + x
    xp = x
    for _ in range(L.bit_length() - 2):
        xp = _h3(xp, xp)
        inv = inv + _h3(inv, xp)
    return inv


def _neumann_f(x):
    inv = _neumann_inverse(x)
    return inv, inv


def _neumann_b(inv, ct):
    return (_h3(_h3(inv, ct, 1, 1), inv, 2, 2),)


_neumann_inverse.defvjp(_neumann_f, _neumann_b)


@jax.custom_vjp
def _saved_inverse(x, inv):
    return inv


def _saved_f(x, inv):
    return inv, inv


def _saved_b(inv, ct):
    return _neumann_b(inv, ct)[0], jnp.zeros_like(inv)


_saved_inverse.defvjp(_saved_f, _saved_b)


def _cat0(parts):
    return jnp.concatenate([p[None] for p in parts], axis=0)


def _gdn_chunk(hb, q, k, v, z, ba, alog, dtb, nw, s, inv_saved=None):
    L = q.shape[1]
    r_i = lax.broadcasted_iota(jnp.int32, (L, L), 0)
    c_i = lax.broadcasted_iota(jnp.int32, (L, L), 1)
    incl, strict = r_i >= c_i, r_i > c_i
    rep = lambda t: jnp.broadcast_to(t[:, None], (GDN_HPB, GDN_REP) + t.shape[1:]).reshape((GDN_VPB,) + t.shape[1:])
    qn = rep(q * lax.rsqrt(jnp.sum(q * q, -1, keepdims=True) + RMS_EPS) * (GDN_DK ** -0.5))
    kn = rep(k * lax.rsqrt(jnp.sum(k * k, -1, keepdims=True) + RMS_EPS))
    beta_all = jax.nn.sigmoid(ba)
    g_all = -jnp.exp(alog) * _softplus(ba + dtb)
    gcs_all = _sel_l(incl.astype(F32), g_all)
    lane = lax.broadcasted_iota(jnp.int32, (L, LANES), 1)
    pick = lambda mat, idx: jnp.sum(jnp.where(lane == idx, mat, 0.0), axis=1, keepdims=True)
    beta = _cat0([pick(beta_all, GDN_VPB * hb + b) for b in range(GDN_VPB)])
    gc = _cat0([pick(gcs_all, GDN_A_LANE + GDN_VPB * hb + b) for b in range(GDN_VPB)])
    gm = jnp.broadcast_to(gc, (GDN_VPB, L, L))
    decay = jnp.exp(jnp.where(incl, gm - jnp.swapaxes(gm, 1, 2), -jnp.inf))
    kb = kn * beta
    eg = jnp.exp(gc)
    x = -jnp.where(strict, _bnt(kb, kn) * decay, 0.0)
    inv = _neumann_inverse(x) if inv_saved is None else _saved_inverse(x, inv_saved)
    uw = _bnn(inv, jnp.concatenate([v * beta, kb * eg], axis=2))
    uu, ww = uw[:, :, :GDN_DV], uw[:, :, GDN_DV:]
    qk = jnp.where(incl, _bnt(qn, kn) * decay, 0.0)
    last = lax.broadcasted_iota(jnp.int32, (L, 1), 0) == L - 1
    glast = jnp.sum(jnp.where(last, gc, 0.0), axis=1, keepdims=True)
    kdec = kn * jnp.exp(glast - gc)
    vnew = uu - _bnn(ww, s)
    o = _bnn(qn * eg, s) + _bnn(qk, vnew)
    new = s * jnp.exp(glast) + _btn(kdec, vnew)
    on = o * lax.rsqrt(jnp.mean(o * o, -1, keepdims=True) + RMS_EPS) * nw * _silu(z)
    return (on, new, inv) if inv_saved is None else (on, new)


def _heads_of(ref, n):
    return _cat0([ref[:, i * LANES:(i + 1) * LANES] for i in range(n)])


def _gdn_specs(nc, rev):
    cc = (lambda c: nc - 1 - c) if rev else (lambda c: c)
    wq, wv = GDN_HPB * GDN_DK, GDN_VPB * GDN_DV
    par = pl.BlockSpec((1, LANES), lambda c, h: (0, 0))
    return dict(
        q=pl.BlockSpec((GDN_L, wq), lambda c, h: (cc(c), h)),
        k=pl.BlockSpec((GDN_L, wq), lambda c, h: (cc(c), GDN_KEY // wq + h)),
        v=pl.BlockSpec((GDN_L, wv), lambda c, h: (cc(c), 2 * GDN_KEY // wv + h)),
        z=pl.BlockSpec((GDN_L, wv), lambda c, h: (cc(c), GDN_CONV_DIM // wv + h)),
        ba=pl.BlockSpec((GDN_L, LANES), lambda c, h: (cc(c), (GDN_CONV_DIM + GDN_VAL) // LANES)),
        par=par,
        st=pl.BlockSpec((1, GDN_VPB, GDN_DK, GDN_DV), lambda c, h: (cc(c), h, 0, 0)),
        inv=pl.BlockSpec((1, GDN_VPB, GDN_L, GDN_L), lambda c, h: (cc(c), h, 0, 0)),
        o=pl.BlockSpec((GDN_L, wv), lambda c, h: (cc(c), h)),
        qk_out=pl.BlockSpec((GDN_L, wq), lambda c, h: (cc(c), h)),
        ba_out=pl.BlockSpec((GDN_L, LANES), lambda c, h: (cc(c), 0)),
    )


def _gdn_fwd(proj, act, alog, dtb, nw, name, ride=None):
    t = proj.shape[0]
    nc = t // GDN_L
    sp = _gdn_specs(nc, False)

    def body(q_ref, k_ref, v_ref, z_ref, ba_ref, alog_ref, dtb_ref, nw_ref, o_ref, st_ref, inv_ref, state):
        c, h = pl.program_id(0), pl.program_id(1)

        mine = pl.ds(h * GDN_VPB, GDN_VPB)

        @pl.when(c == 0)
        def _():
            state[mine] = jnp.zeros((GDN_VPB, GDN_DK, GDN_DV), F32)

        prev = state[mine]
        st_ref[0] = prev
        on, new, inv = _gdn_chunk(h, _heads_of(q_ref, GDN_HPB), _heads_of(k_ref, GDN_HPB), _heads_of(v_ref, GDN_VPB),
                                  _heads_of(z_ref, GDN_VPB), ba_ref[...], alog_ref[...], dtb_ref[...], nw_ref[...], prev)
        inv_ref[0] = inv
        for b in range(GDN_VPB):
            o_ref[:, b * LANES:(b + 1) * LANES] = on[b].astype(o_ref.dtype)
        state[mine] = new

    sds = jax.ShapeDtypeStruct
    return _hosted_call(
        body, name=name, grid=(nc, GDN_HK // GDN_HPB),
        in_specs=[sp["q"], sp["k"], sp["v"], sp["z"], sp["ba"], sp["par"], sp["par"], sp["par"]],
        out_specs=[sp["o"], sp["st"], sp["inv"]],
        out_shape=[sds((t, GDN_VAL), MXU), sds((nc, GDN_HV, GDN_DK, GDN_DV), F32), sds((nc, GDN_HV, GDN_L, GDN_L), F32)],
        scratch_shapes=[pltpu.VMEM((GDN_HV, GDN_DK, GDN_DV), F32)],
        sem=("arbitrary", "arbitrary"), args=(act, act, act, proj, proj, alog, dtb, nw), ride=ride)


def _gdn_bwd(proj, act, alog, dtb, nw, states, invs, don, name, ride=None):
    t = proj.shape[0]
    nc = t // GDN_L
    sp = _gdn_specs(nc, True)

    def body(q_ref, k_ref, v_ref, z_ref, ba_ref, alog_ref, dtb_ref, nw_ref, st_ref, do_ref, inv_ref,
             dact_ref, dz_ref, dba_ref, dalog_ref, ddtb_ref, dnw_ref, dstate):
        c, h = pl.program_id(0), pl.program_id(1)

        mine = pl.ds(h * GDN_VPB, GDN_VPB)

        @pl.when(c == 0)
        def _():
            dstate[mine] = jnp.zeros((GDN_VPB, GDN_DK, GDN_DV), F32)

        @pl.when((c == 0) & (h == 0))
        def _():
            dalog_ref[...] = jnp.zeros_like(dalog_ref)
            ddtb_ref[...] = jnp.zeros_like(ddtb_ref)
            dnw_ref[...] = jnp.zeros_like(dnw_ref)

        @pl.when(h == 0)
        def _():
            dba_ref[...] = jnp.zeros_like(dba_ref)

        _, vjp = jax.vjp(functools.partial(_gdn_chunk, h, inv_saved=inv_ref[0]), _heads_of(q_ref, GDN_HPB),
                         _heads_of(k_ref, GDN_HPB), _heads_of(v_ref, GDN_VPB), _heads_of(z_ref, GDN_VPB), ba_ref[...],
                         alog_ref[...], dtb_ref[...], nw_ref[...], st_ref[0])
        dq, dk, dv, dz, dba, dalog, ddtb, dnw, dprev = vjp((_heads_of(do_ref, GDN_VPB), dstate[mine]))
        for b in range(GDN_VPB):
            dz_ref[:, b * LANES:(b + 1) * LANES] = dz[b]
        for hi in range(GDN_HK // GDN_HPB):
            @pl.when(h == hi)
            def _(hi=hi):
                for i in range(GDN_HPB):
                    c0 = (hi * GDN_HPB + i) * GDN_DK
                    dact_ref[:, c0:c0 + GDN_DK] = dq[i]
                    dact_ref[:, GDN_KEY + c0:GDN_KEY + c0 + GDN_DK] = dk[i]
                for b in range(GDN_VPB):
                    c0 = 2 * GDN_KEY + (hi * GDN_VPB + b) * GDN_DV
                    dact_ref[:, c0:c0 + GDN_DV] = dv[b]
        dba_ref[...] += dba
        dalog_ref[...] += dalog
        ddtb_ref[...] += ddtb
        dnw_ref[...] += dnw
        dstate[mine] = dprev

    sds = jax.ShapeDtypeStruct
    par_out = pl.BlockSpec((1, LANES), lambda c, h: (0, 0))
    return _hosted_call(
        body, name=name, grid=(nc, GDN_HK // GDN_HPB),
        in_specs=[sp["q"], sp["k"], sp["v"], sp["z"], sp["ba"], sp["par"], sp["par"], sp["par"], sp["st"], sp["o"],
                  sp["inv"]],
        out_specs=[pl.BlockSpec((GDN_L, GDN_CONV_DIM), lambda c, h: (nc - 1 - c, 0)), sp["o"], sp["ba_out"],
                   par_out, par_out, par_out],
        out_shape=[sds((t, GDN_CONV_DIM), F32), sds((t, GDN_VAL), F32), sds((t, LANES), F32), sds((1, LANES), F32),
                   sds((1, LANES), F32), sds((1, LANES), F32)],
        scratch_shapes=[pltpu.VMEM((GDN_HV, GDN_DK, GDN_DV), F32)],
        sem=("arbitrary", "arbitrary"), args=(act, act, act, proj, proj, alog, dtb, nw, states, don, invs), ride=ride)


GDN_PROJ = GDN_CONV_DIM + GDN_VAL + 2 * GDN_HV


def _gdn_layer_fwd(u, w, ln_g, ln_b, tag, rides=None, late=None):
    w_in = jnp.concatenate([w["in_w"], jnp.zeros((D_MODEL, GDN_PROJ_PAD - GDN_PROJ), w["in_w"].dtype)], axis=1)
    alog = _pad_lanes(w["a_log"], offset=GDN_A_LANE)
    dtb = _pad_lanes(w["dt_bias"], offset=GDN_A_LANE)
    nw = w["norm_w"].reshape(1, GDN_DV)
    zb = jnp.zeros((1, GDN_CONV_DIM), F32)
    if late is None:
        proj = _mm(u.astype(MXU), w_in, name=tag + "_in")
    else:
        proj, got = _mm(u.astype(MXU), w_in, name=tag + "_in", ride=(late[0], False))
        w.update(late[1](got))
    act, _ = _conv_fwd(proj, 0, w["conv_w"], zb, name=tag + "_conv")
    rode = {}
    (on, states, invs), rode["delta"] = _gdn_fwd(proj, act, alog, dtb, nw, name=tag + "_delta",
                                                 ride=_ride_of(rides, "delta", False))
    y = _mm(on, w["out_w"], name=tag + "_out")
    h = _ln_fwd(u, y, ln_g, ln_b, name=tag + "_ln")
    saved = dict(u=u, w_in=w_in, proj=proj, act=act, states=states, invs=invs, on=on, y=y, alog=alog, dtb=dtb, nw=nw, zb=zb)
    return h, saved, rode


def _gdn_layer_bwd(s, w, ln_g, dr_up, du_up, tag, rides=None, own=None):
    t = s["u"].shape[0]
    dr, dg, db = _ln_bwd(s["u"], s["y"], ln_g, dr_up, du_up, name=tag + "_ln_b")
    don = _mm(dr, w["out_w"], tb=True, name=tag + "_out_bx")
    d_out_w = _mm(s["on"], dr, ta=True, out_dtype=WIRE, name=tag + "_out_bw")
    rode = {}
    (dact, dz, dba, dalog, ddtb, dnw), rode["delta_b"] = _gdn_bwd(
        s["proj"], s["act"], s["alog"], s["dtb"], s["nw"], s["states"], s["invs"], don, name=tag + "_delta_b",
        ride=_ride_of(rides, "delta_b", True))
    dqkv, d_conv_w, _ = _conv_bwd(s["proj"], 0, w["conv_w"], s["zb"], dact, name=tag + "_conv_b")
    dproj = jnp.concatenate([dqkv, dz, dba, jnp.zeros((t, GDN_PROJ_PAD - GDN_PROJ - (LANES - 2 * GDN_HV)), F32)],
                            axis=1).astype(MXU)
    d_in, rode["in_bw"] = _mm_r(s["u"].astype(MXU), dproj, ([own("out_w", d_out_w)], True) if own else None, ta=True,
                                out_dtype=WIRE, name=tag + "_in_bw")
    du = _mm(dproj, s["w_in"], tb=True, name=tag + "_in_bx")
    grads = dict(in_w=d_in[:, :GDN_PROJ], conv_w=d_conv_w, a_log=dalog[0, GDN_A_LANE:GDN_A_LANE + GDN_HV],
                 dt_bias=ddtb[0, GDN_A_LANE:GDN_A_LANE + GDN_HV], norm_w=dnw[0], out_w=d_out_w, ln_g=dg[0], ln_b=db[0])
    return dr, du, grads, rode


def _mesh_pos():
    return lax.axis_index("x"), lax.axis_index("y"), lax.axis_index("c")


def _peer(k, x, y, c):
    return ((1 - x) if k & 4 else x, (1 - y) if k & 2 else y, (1 - c) if k & 1 else c)


def _ride_copies(ins, outs, send, recv, loc, scatter, with_arrivals):
    n = len(ins)
    x, y, c = _mesh_pos()
    me = 4 * x + 2 * y + c
    local = [pltpu.make_async_copy(ins[i].at[me] if scatter else ins[i], outs[i].at[me], loc.at[i]) for i in range(n)]
    sends, arrivals = [], []
    for k in range(1, N_DEV):
        peer = _peer(k, x, y, c)
        pidx = 4 * peer[0] + 2 * peer[1] + peer[2]
        for i in range(n):
            src = ins[i].at[pidx] if scatter else ins[i]
            sems = dict(send_sem=send.at[i, k - 1], recv_sem=recv.at[i, k - 1], device_id=peer,
                        device_id_type=pl.DeviceIdType.MESH)
            sends.append(pltpu.make_async_remote_copy(src_ref=src, dst_ref=outs[i].at[me], **sems))
            if with_arrivals:
                arrivals.append(pltpu.make_async_remote_copy(src_ref=src, dst_ref=outs[i].at[pidx], **sems))
    return local, sends, arrivals


def _gather_copy(ins, outs, send, recv, i, k, block, to, from_input=False):
    slot = outs[i].at[4 * block[0] + 2 * block[1] + block[2]]
    return pltpu.make_async_remote_copy(src_ref=ins[i] if from_input else slot, dst_ref=slot, send_sem=send.at[i, k],
                                        recv_sem=recv.at[i, k], device_id=to, device_id_type=pl.DeviceIdType.MESH)


def _gather_places():
    x, y, c = _mesh_pos()
    return (x, y, c), (x, y, 1 - c), [(x, 1 - y), (1 - x, y), (1 - x, 1 - y)], c


def _ride_start(ins, outs, send, recv, loc, scatter, direct=True):
    if direct:
        local, sends, _ = _ride_copies(ins, outs, send, recv, loc, scatter, False)
        for cp in local + sends:
            cp.start()
        return
    me, sibling, chips, c = _gather_places()
    for i in range(len(ins)):
        pltpu.make_async_copy(ins[i], outs[i].at[4 * me[0] + 2 * me[1] + me[2]], loc.at[i]).start()
        _gather_copy(ins, outs, send, recv, i, 0, me, sibling, True).start()
        for j, ch in enumerate(chips):
            _gather_copy(ins, outs, send, recv, i, 1 + j, me, (*ch, c), True).start()


def _ride_wait(ins, outs, send, recv, loc, scatter, direct=True):
    if direct:
        local, sends, arrivals = _ride_copies(ins, outs, send, recv, loc, scatter, True)
        for cp in arrivals:
            cp.wait_recv()
        for cp in sends:
            cp.wait_send()
        for cp in local:
            cp.wait()
        return
    me, sibling, chips, c = _gather_places()
    n = len(ins)
    passed = []
    for j, ch in enumerate(chips):
        for i in range(n):
            _gather_copy(ins, outs, send, recv, i, 1 + j, (*ch, c), me).wait_recv()
            cp = _gather_copy(ins, outs, send, recv, i, 4 + j, (*ch, c), sibling)
            cp.start()
            passed.append(cp)
    for i in range(n):
        _gather_copy(ins, outs, send, recv, i, 0, sibling, me).wait_recv()
        for j, ch in enumerate(chips):
            _gather_copy(ins, outs, send, recv, i, 4 + j, (*ch, 1 - c), me).wait_recv()
    for i in range(n):
        _gather_copy(ins, outs, send, recv, i, 0, me, sibling, True).wait_send()
        for j, ch in enumerate(chips):
            _gather_copy(ins, outs, send, recv, i, 1 + j, me, (*ch, c), True).wait_send()
    for cp in passed:
        cp.wait_send()
    for i in range(n):
        pltpu.make_async_copy(ins[i], outs[i].at[4 * me[0] + 2 * me[1] + me[2]], loc.at[i]).wait()


def _ride_shapes(arrs, scatter):
    n = len(arrs)
    out_shape = [jax.ShapeDtypeStruct(a.shape if scatter else (N_DEV,) + a.shape, a.dtype) for a in arrs]
    scratch = [pltpu.SemaphoreType.DMA((n, N_DEV - 1)), pltpu.SemaphoreType.DMA((n, N_DEV - 1)), pltpu.SemaphoreType.DMA((n,))]
    return out_shape, scratch


def _exchange(arrs, scatter, name):
    n = len(arrs)
    hbm = pl.BlockSpec(memory_space=pltpu.HBM)

    def body(*refs):
        ins, outs = refs[:n], refs[n:2 * n]
        _ride_start(ins, outs, *refs[2 * n:], scatter, direct=scatter)
        _ride_wait(ins, outs, *refs[2 * n:], scatter, direct=scatter)

    out_shape, scratch = _ride_shapes(arrs, scatter)
    return pl.pallas_call(
        body, name=name, in_specs=[hbm] * n, out_specs=[hbm] * n, out_shape=out_shape, scratch_shapes=scratch,
        compiler_params=pltpu.CompilerParams(has_side_effects=True),
    )(*arrs)


def _hosted_call(body, *, name, grid, in_specs, out_specs, out_shape, scratch_shapes, sem, args, ride=None):
    if ride is None:
        return pl.pallas_call(body, name=name, grid=grid, in_specs=in_specs, out_specs=out_specs, out_shape=out_shape,
                              scratch_shapes=scratch_shapes, compiler_params=_cparams(sem))(*args), []
    arrs, scatter = ride
    n, ni, no, ns = len(arrs), len(in_specs), len(out_specs), len(scratch_shapes)
    hbm = pl.BlockSpec(memory_space=pltpu.HBM)
    r_shape, r_scratch = _ride_shapes(arrs, scatter)

    def full(*refs):
        a, ri = refs[:ni], refs[ni:ni + n]
        o, ro = refs[ni + n:ni + n + no], refs[ni + n + no:ni + 2 * n + no]
        s, rs = refs[ni + 2 * n + no:ni + 2 * n + no + ns], refs[ni + 2 * n + no + ns:]
        ids = [pl.program_id(d) for d in range(len(grid))]
        first, last = ids[0] == 0, ids[0] == grid[0] - 1
        for d in range(1, len(grid)):
            first, last = first & (ids[d] == 0), last & (ids[d] == grid[d] - 1)

        @pl.when(first)
        def _():
            _ride_start(ri, ro, *rs, scatter)

        body(*a, *o, *s)

        @pl.when(last)
        def _():
            _ride_wait(ri, ro, *rs, scatter)

    outs = pl.pallas_call(
        full, name=name, grid=grid, in_specs=list(in_specs) + [hbm] * n, out_specs=list(out_specs) + [hbm] * n,
        out_shape=list(out_shape) + r_shape, scratch_shapes=list(scratch_shapes) + r_scratch,
        compiler_params=pltpu.CompilerParams(dimension_semantics=("arbitrary",) * len(grid), vmem_limit_bytes=VMEM_LIMIT,
                                             has_side_effects=True),
    )(*args, *arrs)
    return outs[:no], list(outs[no:])


def _to_wire(w3, j, name):
    _, r, c = w3.shape
    tr = _row_tile(r, c)

    def body(w_ref, o_ref):
        o_ref[...] = w_ref[0].astype(o_ref.dtype)

    return pl.pallas_call(
        body, name=name, grid=(r // tr,), in_specs=[pl.BlockSpec((1, tr, c), lambda i: (j, i, 0))],
        out_specs=pl.BlockSpec((tr, c), lambda i: (i, 0)), out_shape=jax.ShapeDtypeStruct((r, c), WIRE),
        compiler_params=_cparams(("parallel",)),
    )(w3)


def _unshard(g, ax):
    g = jnp.moveaxis(g, 0, ax)
    sh = g.shape
    return g.reshape(sh[:ax] + (sh[ax] * sh[ax + 1],) + sh[ax + 2:])


def _to_parts(full, ax):
    sh = full.shape
    full = full.reshape(sh[:ax] + (N_DEV, sh[ax] // N_DEV) + sh[ax + 1:])
    return jnp.moveaxis(full, ax, 0)


def _row_tile(r, c):
    cap = max(8, (256 * 1024) // max(c, 1))
    best = None
    for d in range(8, min(r, cap) + 1, 8):
        if r % d == 0:
            best = d
    return r if best is None else best


def _adamw(w, m, v, parts, name):
    r, c = w.shape
    tr = _row_tile(r, c)

    def body(w_ref, m_ref, v_ref, p_ref, g_ref, d_ref, nm_ref, nv_ref):
        g = p_ref[0].astype(F32)
        for q in range(1, N_DEV):
            g = g + p_ref[q].astype(F32)
        nm = ADAM_B1 * m_ref[...] + (1.0 - ADAM_B1) * g
        nv = ADAM_B2 * v_ref[...] + (1.0 - ADAM_B2) * (g * g)
        m_hat = nm / (1.0 - ADAM_B1 ** ADAM_STEP)
        v_hat = nv / (1.0 - ADAM_B2 ** ADAM_STEP)
        g_ref[...] = g
        d_ref[...] = -ADAM_LR * (m_hat / (jnp.sqrt(v_hat) + ADAM_EPS) + ADAM_WD * w_ref[...])
        nm_ref[...] = nm
        nv_ref[...] = nv

    row = pl.BlockSpec((tr, c), lambda i: (i, 0))
    out = jax.ShapeDtypeStruct((r, c), F32)
    return pl.pallas_call(
        body, name=name, grid=(r // tr,),
        in_specs=[row, row, row, pl.BlockSpec((N_DEV, tr, c), lambda i: (0, i, 0))],
        out_specs=[row] * 4, out_shape=[out] * 4, compiler_params=_cparams(("parallel",)),
    )(w, m, v, parts)


WEIGHTS = ['ssd_in_w', 'ssd_conv_w', 'ssd_conv_b', 'ssd_dt_bias', 'ssd_a_log', 'ssd_d', 'ssd_norm_w', 'ssd_out_w',
           'mla_in_w', 'mla_q_norm_w', 'mla_q_up_w', 'mla_kv_norm_w', 'mla_kv_up_w', 'mla_out_w', 'gdn_in_w',
           'gdn_conv_w', 'gdn_a_log', 'gdn_dt_bias', 'gdn_norm_w', 'gdn_out_w', 'ln_g', 'ln_b']
SHARDED = {'ssd_in_w': (1, True), 'ssd_conv_w': (1, False), 'ssd_conv_b': (0, False), 'ssd_norm_w': (0, False),
           'ssd_out_w': (0, True), 'mla_in_w': (1, True), 'mla_q_up_w': (1, True), 'mla_kv_up_w': (1, True),
           'mla_out_w': (0, True), 'gdn_in_w': (1, True), 'gdn_conv_w': (1, False), 'gdn_out_w': (0, True)}
REPLICATED = [n for n in WEIGHTS if n not in SHARDED]


def _pack_small(vals):
    flat = jnp.concatenate([vals[n].reshape(-1).astype(F32) for n in REPLICATED])
    rows = -(-flat.shape[0] // (8 * LANES)) * 8
    return jnp.pad(flat, (0, rows * LANES - flat.shape[0])).reshape(rows, LANES)


def _unpack_small(slab, like):
    flat = slab.reshape(-1)
    out, off = {}, 0
    for n in REPLICATED:
        sz = like[n].size
        out[n] = flat[off:off + sz].reshape(like[n].shape)
        off += sz
    return out


def kernel(x, positions, ssd_in_w, ssd_conv_w, ssd_conv_b, ssd_dt_bias, ssd_a_log, ssd_d, ssd_norm_w, ssd_out_w, mla_in_w, mla_q_norm_w, mla_q_up_w, mla_kv_norm_w, mla_kv_up_w, mla_out_w, gdn_in_w, gdn_conv_w, gdn_a_log, gdn_dt_bias, gdn_norm_w, gdn_out_w, ln_g, ln_b, loss_target, m_ssd_in_w, m_ssd_conv_w, m_ssd_conv_b, m_ssd_dt_bias, m_ssd_a_log, m_ssd_d, m_ssd_norm_w, m_ssd_out_w, m_mla_in_w, m_mla_q_norm_w, m_mla_q_up_w, m_mla_kv_norm_w, m_mla_kv_up_w, m_mla_out_w, m_gdn_in_w, m_gdn_conv_w, m_gdn_a_log, m_gdn_dt_bias, m_gdn_norm_w, m_gdn_out_w, m_ln_g, m_ln_b, v_ssd_in_w, v_ssd_conv_w, v_ssd_conv_b, v_ssd_dt_bias, v_ssd_a_log, v_ssd_d, v_ssd_norm_w, v_ssd_out_w, v_mla_in_w, v_mla_q_norm_w, v_mla_q_up_w, v_mla_kv_norm_w, v_mla_kv_up_w, v_mla_out_w, v_gdn_in_w, v_gdn_conv_w, v_gdn_a_log, v_gdn_dt_bias, v_gdn_norm_w, v_gdn_out_w, v_ln_g, v_ln_b):
    loc = locals()
    w = {n: loc[n] for n in WEIGHTS}
    m = {n: loc["m_" + n] for n in WEIGHTS}
    v = {n: loc["v_" + n] for n in WEIGHTS}
    xs, pos, tgt = x[0], positions[0], loss_target[0]

    def names_of(prefix):
        return [n for n in SHARDED if n.startswith(prefix + "_")]

    def shard(n, j):
        return _to_wire(w[n], j, name="wire_%s_%d" % (n, j)) if SHARDED[n][1] else w[n][j]

    def full(n, gathered):
        return _unshard(gathered, SHARDED[n][0])

    def slots(n, g):
        return _to_parts(g, SHARDED[n][0]).astype(WIRE if SHARDED[n][1] else F32)

    def replicated(prefix, j):
        return {n[len(prefix) + 1:]: w[n][j] for n in REPLICATED if n.startswith(prefix + "_")}

    def late_weights(prefix, j, keys):
        return [shard(prefix + "_" + k, j) for k in keys], lambda got: {k: full(prefix + "_" + k, g) for k, g in zip(keys, got)}

    lg = lambda i: w["ln_g"][i].reshape(1, D_MODEL)
    lb = lambda i: w["ln_b"][i].reshape(1, D_MODEL)
    cosf, sinf = _rope_tables(pos)
    ssd_late = ("conv_w", "conv_b", "norm_w", "out_w")

    w_s0 = dict(replicated("ssd", 0), in_w=full("ssd_in_w", _exchange([shard("ssd_in_w", 0)], False, name="gather_ssd0")[0]))
    h1, s0, got = _ssd_layer_fwd(
        xs, w_s0, lg(0), lb(0), "ssd0", late=late_weights("ssd", 0, ssd_late),
        rides={"conv": [shard("mla_q_up_w", 0), shard("mla_kv_up_w", 0)], "scan": [shard("mla_in_w", 0)],
               "out": [shard("mla_out_w", 0)]})
    w_m0 = dict(replicated("mla", 0), q_up_w=full("mla_q_up_w", got["conv"][0]), kv_up_w=full("mla_kv_up_w", got["conv"][1]),
                in_w=full("mla_in_w", got["scan"][0]), out_w=full("mla_out_w", got["out"][0]))
    h2, s1, got = _mla_layer_fwd(h1, w_m0, lg(1), lb(1), cosf, sinf, "mla0", rides={"attn": [shard("gdn_in_w", 0)]})
    w_g0 = dict(replicated("gdn", 0), in_w=full("gdn_in_w", got["attn"][0]))
    h3, s2, got = _gdn_layer_fwd(h2, w_g0, lg(2), lb(2), "gdn0", late=late_weights("gdn", 0, ("conv_w", "out_w")),
                                 rides={"delta": [shard("ssd_in_w", 1)]})
    w_s1 = dict(replicated("ssd", 1), in_w=full("ssd_in_w", got["delta"][0]))
    h4, s3, _ = _ssd_layer_fwd(h3, w_s1, lg(3), lb(3), "ssd1", late=late_weights("ssd", 1, ssd_late))
    loss_tile, dl = _loss_head(h4, tgt, name="loss_head")

    dr3, du3, g3, _, _ = _ssd_layer_bwd(s3, w_s1, lg(3), jnp.zeros_like(dl), dl, "ssd1")
    dr2, du2, g2, got = _gdn_layer_bwd(s2, w_g0, lg(2), dr3, du3, "gdn0", own=lambda k, g: slots("gdn_" + k, g),
                                       rides={"delta_b": [slots(n, g3[n[4:]]) for n in names_of("ssd")]})
    r3, r2_out = got["delta_b"], got["in_bw"][0]
    gin = slots("gdn_in_w", g2["in_w"])
    half = gin.shape[1] // 2
    dr1, du1, g1, got = _mla_layer_bwd(
        s1, w_m0, lg(1), cosf, sinf, dr2, du2, "mla0",
        rides={"attn_bq": [gin[:, :half]], "attn_bkv": [gin[:, half:], slots("gdn_conv_w", g2["conv_w"])]})
    r2 = {"gdn_in_w": jnp.concatenate([got["attn_bq"][0], got["attn_bkv"][0]], axis=1), "gdn_conv_w": got["attn_bkv"][1],
          "gdn_out_w": r2_out}
    dr0, du0, g0, got, r0 = _ssd_layer_bwd(s0, w_s0, lg(0), dr1, du1, "ssd0", own=lambda k, g: slots("ssd_" + k, g),
                                           rides={"scan_b": [slots(n, g1[n[4:]]) for n in names_of("mla")]})
    r1 = got["scan_b"]
    grad_x = _axpy(dr0, du0, name="grad_x")[None]

    gsmall = {"ssd_" + k: jnp.stack([g0[k], g3[k]]) for k in ("dt_bias", "a_log", "d")}
    gsmall.update({"mla_" + k: g1[k][None] for k in ("q_norm_w", "kv_norm_w")})
    gsmall.update({"gdn_" + k: g2[k][None] for k in ("a_log", "dt_bias", "norm_w")})
    gsmall["ln_g"] = jnp.stack([g0["ln_g"], g1["ln_g"], g2["ln_g"], g3["ln_g"]])
    gsmall["ln_b"] = jnp.stack([g0["ln_b"], g1["ln_b"], g2["ln_b"], g3["ln_b"]])
    small = _pack_small(gsmall)
    rsmall, = _exchange([jnp.broadcast_to(small[None], (N_DEV,) + small.shape)], True, name="gather_small_grads")

    recvd = {n: jnp.stack([r0[n[4:]], b], axis=1) for n, b in zip(names_of("ssd"), r3)}
    recvd.update({n: a[:, None] for n, a in zip(names_of("mla"), r1)})
    recvd.update({n: a[:, None] for n, a in r2.items()})
    recvd = [recvd[n] for n in SHARDED] + [rsmall]

    grads, deltas, new_m, new_v = {}, {}, {}, {}
    for n, pt in zip(SHARDED, recvd[:-1]):
        shp = w[n].shape
        r2d = (-1, shp[-1])
        outs = _adamw(w[n].reshape(r2d), m[n].reshape(r2d), v[n].reshape(r2d), pt.reshape((N_DEV,) + w[n].reshape(r2d).shape),
                      name="adamw_" + n)
        grads[n], deltas[n], new_m[n], new_v[n] = (o.reshape(shp) for o in outs)
    outs = _adamw(_pack_small(w), _pack_small(m), _pack_small(v), recvd[-1], name="adamw_replicated")
    for dst, o in zip((grads, deltas, new_m, new_v), outs):
        dst.update(_unpack_small(o, w))

    loss = lax.psum(loss_tile[0, 0], ("x", "y", "c"))
    return (loss, grad_x, *[grads[n] for n in WEIGHTS], *[deltas[n] for n in WEIGHTS],
            *[new_m[n] for n in WEIGHTS], *[new_v[n] for n in WEIGHTS])
```

```python
import functools

import jax
import jax.numpy as jnp
from jax import lax
from jax.experimental import pallas as pl
from jax.experimental.pallas import tpu as pltpu

F32 = jnp.float32
MXU = jnp.bfloat16
WIRE = jnp.bfloat16
HI = lax.Precision.HIGHEST

N_DEV = 8
LANES = 128
VMEM_LIMIT = 56 * 1024 * 1024

D_MODEL = 2048
DEPTH = 4
ALPHA = (2.0 * DEPTH) ** 0.25
LN_EPS = 1e-5
RMS_EPS = 1e-6

SSD_DI = 4096
SSD_P = 64
SSD_H = 64
SSD_G = 8
SSD_N = 128
SSD_L = 128
SSD_GS = SSD_DI // SSD_G
SSD_CONV_DIM = SSD_DI + 2 * SSD_G * SSD_N
SSD_PROJ_PAD = SSD_DI + SSD_CONV_DIM + LANES

MLA_H = 16
MLA_QR = 768
MLA_KVR = 512
MLA_NOPE = 128
MLA_ROPE = 64
MLA_V = 128
MLA_GATE = MLA_H * MLA_V
MLA_PROJ_PAD = MLA_QR + MLA_KVR + LANES + MLA_GATE
MLA_SCALE = (MLA_NOPE + MLA_ROPE) ** -0.5
ROPE_THETA = 10000.0
ATT_BLK = 512

GDN_HK = 16
GDN_HV = 32
GDN_DK = 128
GDN_DV = 128
GDN_KEY = GDN_HK * GDN_DK
GDN_VAL = GDN_HV * GDN_DV
GDN_L = 128
GDN_CONV_DIM = 2 * GDN_KEY + GDN_VAL
GDN_PROJ_PAD = 12800

ADAM_LR = 0.001
ADAM_B1 = 0.9
ADAM_B2 = 0.999
ADAM_EPS = 1e-08
ADAM_WD = 0.01
ADAM_STEP = 10


def _cparams(sem=None):
    return pltpu.CompilerParams(dimension_semantics=sem, vmem_limit_bytes=VMEM_LIMIT)


def _tile(n, cap):
    if n <= cap:
        return n
    best = None
    for d in range(LANES, cap + 1, LANES):
        if n % d == 0:
            best = d
    assert best is not None, (n, cap)
    return best


def _dg(a, b, ca, cb, prec=None):
    return lax.dot_general(a, b, (((ca,), (cb,)), ((), ())), preferred_element_type=F32, precision=prec)


def _mx(a):
    return a.astype(MXU)


@jax.custom_vjp
def _nn(a, b):
    return _dg(_mx(a), _mx(b), 1, 0)


def _nn_f(a, b):
    return _nn(a, b), (a, b)


def _nn_b(res, ct):
    a, b = res
    return _dg(_mx(ct), _mx(b), 1, 1), _dg(_mx(a), _mx(ct), 0, 0)


_nn.defvjp(_nn_f, _nn_b)


@jax.custom_vjp
def _nt(a, b):
    return _dg(_mx(a), _mx(b), 1, 1)


def _nt_f(a, b):
    return _nt(a, b), (a, b)


def _nt_b(res, ct):
    a, b = res
    return _dg(_mx(ct), _mx(b), 1, 0), _dg(_mx(ct), _mx(a), 0, 0)


_nt.defvjp(_nt_f, _nt_b)


@jax.custom_vjp
def _tn(a, b):
    return _dg(_mx(a), _mx(b), 0, 0)


def _tn_f(a, b):
    return _tn(a, b), (a, b)


def _tn_b(res, ct):
    a, b = res
    return _dg(_mx(b), _mx(ct), 1, 1), _dg(_mx(a), _mx(ct), 1, 0)


_tn.defvjp(_tn_f, _tn_b)


def _softplus(x):
    return jnp.maximum(x, 0.0) + jnp.log(1.0 + jnp.exp(-jnp.abs(x)))


def _silu(x):
    return x * jax.nn.sigmoid(x)


MM_TM = 1024
MM_TN = 1280
MM_VMEM_BUDGET = 40 * 1024 * 1024


def _mm(a, b, *, ta=False, tb=False, out_dtype=F32, name, ride=None):
    if ta:
        kdim, m = a.shape
    else:
        m, kdim = a.shape
    if tb:
        n, kb = b.shape
    else:
        kb, n = b.shape
    assert kdim == kb, (a.shape, b.shape, ta, tb)
    tm, tn = _tile(m, MM_TM), _tile(n, MM_TN)
    abytes, bbytes, obytes = a.dtype.itemsize, b.dtype.itemsize, jnp.dtype(out_dtype).itemsize
    tk = LANES
    for d in range(LANES, kdim + 1, LANES):
        if kdim % d == 0 and 2 * d * (tm * abytes + tn * bbytes) + tm * tn * (2 * obytes + 4) <= MM_VMEM_BUDGET:
            tk = d
    nk = kdim // tk
    ca, cb = (0 if ta else 1), (1 if tb else 0)

    def body(a_ref, b_ref, o_ref, *acc):
        part = _dg(_mx(a_ref[...]), _mx(b_ref[...]), ca, cb)
        if nk == 1:
            o_ref[...] = part.astype(out_dtype)
            return
        acc_ref, = acc
        k = pl.program_id(2)

        @pl.when(k == 0)
        def _():
            acc_ref[...] = part

        @pl.when(k > 0)
        def _():
            acc_ref[...] += part

        @pl.when(k == nk - 1)
        def _():
            o_ref[...] = acc_ref[...].astype(out_dtype)

    a_spec = pl.BlockSpec((tk, tm), lambda i, j, k: (k, i)) if ta else pl.BlockSpec((tm, tk), lambda i, j, k: (i, k))
    b_spec = pl.BlockSpec((tn, tk), lambda i, j, k: (j, k)) if tb else pl.BlockSpec((tk, tn), lambda i, j, k: (k, j))
    (out,), rode = _hosted_call(
        body, name=name, grid=(m // tm, n // tn, nk),
        in_specs=[a_spec, b_spec], out_specs=[pl.BlockSpec((tm, tn), lambda i, j, k: (i, j))],
        out_shape=[jax.ShapeDtypeStruct((m, n), out_dtype)],
        scratch_shapes=[pltpu.VMEM((tm, tn), F32)] if nk > 1 else [],
        sem=("parallel", "parallel", "arbitrary"), args=(a, b), ride=ride)
    return out if ride is None else (out, rode)


def _ln_fwd(h, y, g, b, name):
    t, d = h.shape
    tr = _tile(t, 256)

    def body(h_ref, y_ref, g_ref, b_ref, o_ref):
        r = ALPHA * h_ref[...] + y_ref[...]
        mu = jnp.mean(r, -1, keepdims=True)
        xc = r - mu
        var = jnp.mean(xc * xc, -1, keepdims=True)
        o_ref[...] = xc * lax.rsqrt(var + LN_EPS) * g_ref[...] + b_ref[...]

    row = pl.BlockSpec((tr, d), lambda i: (i, 0))
    par = pl.BlockSpec((1, d), lambda i: (0, 0))
    return pl.pallas_call(
        body, name=name, grid=(t // tr,), in_specs=[row, row, par, par], out_specs=row,
        out_shape=jax.ShapeDtypeStruct((t, d), F32), compiler_params=_cparams(("parallel",)),
    )(h, y, g, b)


def _ln_bwd(h, y, g, dr_up, du_up, name):
    t, d = h.shape
    tr = _tile(t, 256)

    def body(h_ref, y_ref, g_ref, dr_ref, du_ref, o_ref, dg_ref, db_ref):
        i = pl.program_id(0)

        @pl.when(i == 0)
        def _():
            dg_ref[...] = jnp.zeros_like(dg_ref)
            db_ref[...] = jnp.zeros_like(db_ref)

        dout = ALPHA * dr_ref[...] + du_ref[...]
        r = ALPHA * h_ref[...] + y_ref[...]
        mu = jnp.mean(r, -1, keepdims=True)
        xc = r - mu
        rstd = lax.rsqrt(jnp.mean(xc * xc, -1, keepdims=True) + LN_EPS)
        xh = xc * rstd
        dxh = dout * g_ref[...]
        o_ref[...] = rstd * (dxh - jnp.mean(dxh, -1, keepdims=True) - xh * jnp.mean(dxh * xh, -1, keepdims=True))
        dg_ref[...] += jnp.sum(dout * xh, 0, keepdims=True)
        db_ref[...] += jnp.sum(dout, 0, keepdims=True)

    row = pl.BlockSpec((tr, d), lambda i: (i, 0))
    par = pl.BlockSpec((1, d), lambda i: (0, 0))
    return pl.pallas_call(
        body, name=name, grid=(t // tr,), in_specs=[row, row, par, row, row], out_specs=[row, par, par],
        out_shape=[jax.ShapeDtypeStruct((t, d), F32), jax.ShapeDtypeStruct((1, d), F32), jax.ShapeDtypeStruct((1, d), F32)],
        compiler_params=_cparams(("arbitrary",)),
    )(h, y, g, dr_up, du_up)


def _loss_head(h, tgt, name):
    t, d = h.shape
    tr = _tile(t, 256)

    def body(h_ref, t_ref, l_ref, d_ref):
        i = pl.program_id(0)

        @pl.when(i == 0)
        def _():
            l_ref[...] = jnp.zeros_like(l_ref)

        e = h_ref[...] - t_ref[...]
        d_ref[...] = e * (1.0 / d)
        l_ref[...] += 0.5 * jnp.sum(jnp.mean(e * e, -1, keepdims=True))

    row = pl.BlockSpec((tr, d), lambda i: (i, 0))
    return pl.pallas_call(
        body, name=name, grid=(t // tr,), in_specs=[row, row],
        out_specs=[pl.BlockSpec((8, LANES), lambda i: (0, 0)), row],
        out_shape=[jax.ShapeDtypeStruct((8, LANES), F32), jax.ShapeDtypeStruct((t, d), F32)],
        compiler_params=_cparams(("arbitrary",)),
    )(h, tgt)


def _axpy(dr, du, name):
    t, d = dr.shape
    tr = _tile(t, 256)

    def body(a_ref, b_ref, o_ref):
        o_ref[...] = ALPHA * a_ref[...] + b_ref[...]

    row = pl.BlockSpec((tr, d), lambda i: (i, 0))
    return pl.pallas_call(
        body, name=name, grid=(t // tr,), in_specs=[row, row], out_specs=row,
        out_shape=jax.ShapeDtypeStruct((t, d), F32), compiler_params=_cparams(("parallel",)),
    )(dr, du)


CONV_TT = 512
CONV_TC = 512


HALO = 8


def _shift_down(cur, halo, s, row):
    if s == 0:
        return cur
    tt = cur.shape[0]
    edge = pltpu.roll(halo, s, 0)
    if tt > HALO:
        edge = jnp.concatenate([edge, jnp.zeros((tt - HALO, cur.shape[1]), cur.dtype)], axis=0)
    return jnp.where(row >= s, pltpu.roll(cur, s, 0), edge)


def _shift_up(cur, halo, s, row, tt):
    if s == 0:
        return cur
    edge = jnp.concatenate([jnp.zeros((tt - HALO, cur.shape[1]), cur.dtype), pltpu.roll(halo, HALO - s, 0)], axis=0)
    return jnp.where(row < tt - s, pltpu.roll(cur, tt - s, 0), edge)


def _conv_fwd(proj, col0, w, b, name, ride=None):
    t = proj.shape[0]
    c = w.shape[1]
    tt = _tile(t, CONV_TT)
    cb0 = col0 // CONV_TC

    def body(x_ref, p_ref, w_ref, b_ref, o_ref):
        i = pl.program_id(1)
        x = x_ref[...]
        p = jnp.where(i > 0, p_ref[...], 0.0)
        row = lax.broadcasted_iota(jnp.int32, x.shape, 0)
        pre = b_ref[...] + w_ref[3:4, :] * x
        for s in (1, 2, 3):
            pre = pre + w_ref[3 - s:4 - s, :] * _shift_down(x, p, s, row)
        o_ref[...] = _silu(pre)

    (act,), rode = _hosted_call(
        body, name=name, grid=(c // CONV_TC, t // tt),
        in_specs=[pl.BlockSpec((tt, CONV_TC), lambda j, i: (i, cb0 + j)),
                  pl.BlockSpec((HALO, CONV_TC), lambda j, i: (jnp.maximum(i * (tt // HALO) - 1, 0), cb0 + j)),
                  pl.BlockSpec((4, CONV_TC), lambda j, i: (0, j)),
                  pl.BlockSpec((1, CONV_TC), lambda j, i: (0, j))],
        out_specs=[pl.BlockSpec((tt, CONV_TC), lambda j, i: (i, j))],
        out_shape=[jax.ShapeDtypeStruct((t, c), F32)], scratch_shapes=[],
        sem=("parallel", "parallel"), args=(proj, proj, w, b), ride=ride)
    return act, rode


def _conv_bwd(proj, col0, w, b, dact, name):
    t = proj.shape[0]
    c = w.shape[1]
    tt = _tile(t, CONV_TT)
    nt = t // tt
    cb0 = col0 // CONV_TC
    per = tt // HALO

    def dpre_of(x, halo, d, w_ref, b_ref):
        row = lax.broadcasted_iota(jnp.int32, x.shape, 0)
        sh = [_shift_down(x, halo, s, row) for s in range(4)]
        pre = b_ref[...] + w_ref[3:4, :] * sh[0]
        for s in (1, 2, 3):
            pre = pre + w_ref[3 - s:4 - s, :] * sh[s]
        sg = jax.nn.sigmoid(pre)
        return d * (sg * (1.0 + pre * (1.0 - sg))), sh

    def body(x_ref, p_ref, n_ref, w_ref, b_ref, d_ref, dn_ref, dx_ref, dw_ref, db_ref):
        i = pl.program_id(1)

        @pl.when(i == 0)
        def _():
            dw_ref[...] = jnp.zeros_like(dw_ref)
            db_ref[...] = jnp.zeros_like(db_ref)

        x = x_ref[...]
        dpre, sh = dpre_of(x, jnp.where(i > 0, p_ref[...], 0.0), d_ref[...], w_ref, b_ref)
        dnext, _ = dpre_of(n_ref[...], x[tt - HALO:, :], jnp.where(i < nt - 1, dn_ref[...], 0.0), w_ref, b_ref)
        row = lax.broadcasted_iota(jnp.int32, x.shape, 0)
        acc = w_ref[3:4, :] * dpre
        for s in (1, 2, 3):
            acc = acc + w_ref[3 - s:4 - s, :] * _shift_up(dpre, dnext, s, row, tt)
        dx_ref[...] = acc
        for s in range(4):
            dw_ref[3 - s:4 - s, :] += jnp.sum(dpre * sh[s], 0, keepdims=True)
        db_ref[...] += jnp.sum(dpre, 0, keepdims=True)

    tile = lambda off: pl.BlockSpec((tt, CONV_TC), lambda j, i: (i, off + j))
    before = lambda off: pl.BlockSpec((HALO, CONV_TC), lambda j, i: (jnp.maximum(i * per - 1, 0), off + j))
    after = lambda off: pl.BlockSpec((HALO, CONV_TC), lambda j, i: (jnp.minimum((i + 1) * per, t // HALO - 1), off + j))
    wspec = lambda rows: pl.BlockSpec((rows, CONV_TC), lambda j, i: (0, j))
    return pl.pallas_call(
        body, name=name, grid=(c // CONV_TC, nt),
        in_specs=[tile(cb0), before(cb0), after(cb0), wspec(4), wspec(1), tile(0), after(0)],
        out_specs=[tile(0), wspec(4), wspec(1)],
        out_shape=[jax.ShapeDtypeStruct((t, c), F32), jax.ShapeDtypeStruct((4, c), F32), jax.ShapeDtypeStruct((1, c), F32)],
        compiler_params=_cparams(("parallel", "arbitrary")),
    )(proj, proj, proj, w, b, dact, dact)


SSD_GB = 4


def _split3(a):
    a1 = _mx(a)
    r = a - a1.astype(F32)
    a2 = _mx(r)
    return a1, a2, _mx(r - a2.astype(F32))


@jax.custom_vjp
def _sel_r(a, c):
    cm = _mx(c)
    p1, p2, p3 = _split3(a)
    return _dg(p1, cm, 1, 0) + (_dg(p2, cm, 1, 0) + _dg(p3, cm, 1, 0))


def _sel_r_f(a, c):
    return _sel_r(a, c), c


def _sel_r_b(c, ct):
    cm = _mx(c)
    p1, p2, p3 = _split3(ct)
    return _dg(p1, cm, 1, 1) + (_dg(p2, cm, 1, 1) + _dg(p3, cm, 1, 1)), jnp.zeros_like(c)


_sel_r.defvjp(_sel_r_f, _sel_r_b)


@jax.custom_vjp
def _sel_l(c, a):
    cm = _mx(c)
    p1, p2, p3 = _split3(a)
    return _dg(cm, p1, 1, 0) + (_dg(cm, p2, 1, 0) + _dg(cm, p3, 1, 0))


def _sel_l_f(c, a):
    return _sel_l(c, a), c


def _sel_l_b(c, ct):
    cm = _mx(c)
    p1, p2, p3 = _split3(ct)
    return jnp.zeros_like(c), _dg(cm, p1, 0, 0) + (_dg(cm, p2, 0, 0) + _dg(cm, p3, 0, 0))


_sel_l.defvjp(_sel_l_f, _sel_l_b)


def _ssd_chunk(gb, x, z, bm, cm, dtraw, dtb, alog, dsk, nw, prev):
    L = x.shape[1]
    r_i = lax.broadcasted_iota(jnp.int32, (L, L), 0)
    c_i = lax.broadcasted_iota(jnp.int32, (L, L), 1)
    causal = r_i >= c_i
    dt = _softplus(dtraw + dtb)
    a = dt * (-jnp.exp(alog))
    acs = _sel_l(causal.astype(F32), a)
    e_r = lax.broadcasted_iota(jnp.int32, (LANES, SSD_GS), 0)
    e_c = lax.broadcasted_iota(jnp.int32, (LANES, SSD_GS), 1)
    hpg = SSD_H // SSD_G
    sels = [(e_r == (gb * SSD_GB + i) * hpg + jnp.right_shift(e_c, 6)).astype(F32) for i in range(SSD_GB)]
    dt_x = _cat0([_sel_r(dt, s) for s in sels])
    acs_x = _cat0([_sel_r(acs, s) for s in sels])
    d_x = _cat0([_sel_r(jnp.broadcast_to(dsk, (8, LANES)), s)[0:1] for s in sels])
    last = lax.broadcasted_iota(jnp.int32, (L, 1), 0) == L - 1
    alast = jnp.sum(jnp.where(last, acs_x, 0.0), axis=1, keepdims=True)
    xdt = x * dt_x
    cb = _bnt(cm, bm)
    lane = lax.broadcasted_iota(jnp.int32, (L, LANES), 1)
    ys = []
    for j in range(SSD_GS // LANES):
        xp = xdt[:, :, j * LANES:(j + 1) * LANES]
        yp = None
        for hh in range(2):
            c0 = (2 * j + hh) * SSD_P
            cmx = jnp.broadcast_to(acs_x[:, :, c0:c0 + 1], (SSD_GB, L, L))
            dec = jnp.exp(jnp.where(causal, cmx - jnp.swapaxes(cmx, 1, 2), -jnp.inf))
            half = (lane < SSD_P) if hh == 0 else (lane >= SSD_P)
            t = _bnn(cb * dec, jnp.where(half, xp, 0.0))
            yp = t if yp is None else yp + t
        ys.append(yp)
    y_diag = jnp.concatenate(ys, axis=2)
    st = _btn(bm, xdt * jnp.exp(alast - acs_x))
    new = prev * jnp.exp(alast) + st
    y_off = _bnn(cm, prev) * jnp.exp(acs_x)
    y = y_diag + y_off + x * d_x
    yg = y * _silu(z)
    yn = yg * lax.rsqrt(jnp.mean(yg * yg, -1, keepdims=True) + RMS_EPS) * nw
    return yn, new


def _groups_of(ref, width):
    return _cat0([ref[:, i * width:(i + 1) * width] for i in range(SSD_GB)])


def _put_groups(ref, val, width):
    for i in range(SSD_GB):
        ref[:, i * width:(i + 1) * width] = val[i].astype(ref.dtype)


def _ssd_specs(nc, rev):
    cc = (lambda c: nc - 1 - c) if rev else (lambda c: c)
    wx, wb = SSD_GB * SSD_GS, SSD_GB * SSD_N
    dtb = (SSD_DI + SSD_CONV_DIM) // LANES
    bb = SSD_DI // wb
    cbk = (SSD_DI + SSD_G * SSD_N) // wb
    par = pl.BlockSpec((1, LANES), lambda c, g: (0, 0))
    return dict(
        z=pl.BlockSpec((SSD_L, wx), lambda c, g: (cc(c), g)),
        dt=pl.BlockSpec((SSD_L, LANES), lambda c, g: (cc(c), dtb)),
        x=pl.BlockSpec((SSD_L, wx), lambda c, g: (cc(c), g)),
        bm=pl.BlockSpec((SSD_L, wb), lambda c, g: (cc(c), bb + g)),
        cm=pl.BlockSpec((SSD_L, wb), lambda c, g: (cc(c), cbk + g)),
        par=par,
        nw=pl.BlockSpec((1, wx), lambda c, g: (0, g)),
        st=pl.BlockSpec((1, SSD_GB, SSD_N, SSD_GS), lambda c, g: (cc(c), g, 0, 0)),
        y=pl.BlockSpec((SSD_L, wx), lambda c, g: (cc(c), g)),
        bc=pl.BlockSpec((SSD_L, wb), lambda c, g: (cc(c), g)),
        dtout=pl.BlockSpec((SSD_L, LANES), lambda c, g: (cc(c), 0)),
    )


def _ssd_fwd(proj, act, dtb, alog, dsk, nw, name, ride=None):
    t = proj.shape[0]
    nc = t // SSD_L
    sp = _ssd_specs(nc, False)

    def body(z_ref, dt_ref, x_ref, bm_ref, cm_ref, dtb_ref, alog_ref, dsk_ref, nw_ref, y_ref, st_ref, state):
        c, g = pl.program_id(0), pl.program_id(1)
        mine = pl.ds(g * SSD_GB, SSD_GB)

        @pl.when(c == 0)
        def _():
            state[mine] = jnp.zeros((SSD_GB, SSD_N, SSD_GS), F32)

        prev = state[mine]
        st_ref[0] = prev
        yn, new = _ssd_chunk(g, _groups_of(x_ref, SSD_GS), _groups_of(z_ref, SSD_GS), _groups_of(bm_ref, SSD_N),
                             _groups_of(cm_ref, SSD_N), dt_ref[...], dtb_ref[...], alog_ref[...], dsk_ref[...],
                             _groups_of(nw_ref, SSD_GS), prev)
        _put_groups(y_ref, yn, SSD_GS)
        state[mine] = new

    return _hosted_call(
        body, name=name, grid=(nc, SSD_G // SSD_GB),
        in_specs=[sp["z"], sp["dt"], sp["x"], sp["bm"], sp["cm"], sp["par"], sp["par"], sp["par"], sp["nw"]],
        out_specs=[sp["y"], sp["st"]],
        out_shape=[jax.ShapeDtypeStruct((t, SSD_DI), MXU), jax.ShapeDtypeStruct((nc, SSD_G, SSD_N, SSD_GS), F32)],
        scratch_shapes=[pltpu.VMEM((SSD_G, SSD_N, SSD_GS), F32)],
        sem=("arbitrary", "arbitrary"), args=(proj, proj, act, act, act, dtb, alog, dsk, nw), ride=ride)


def _ssd_bwd(proj, act, dtb, alog, dsk, nw, states, dyn, name, ride=None):
    t = proj.shape[0]
    nc = t // SSD_L
    sp = _ssd_specs(nc, True)

    def body(z_ref, dt_ref, x_ref, bm_ref, cm_ref, dtb_ref, alog_ref, dsk_ref, nw_ref, st_ref, dy_ref,
             dact_ref, dz_ref, ddt_ref, ddtb_ref, dalog_ref, ddsk_ref, dnw_ref, dstate):
        c, g = pl.program_id(0), pl.program_id(1)
        mine = pl.ds(g * SSD_GB, SSD_GB)

        @pl.when(c == 0)
        def _():
            dstate[mine] = jnp.zeros((SSD_GB, SSD_N, SSD_GS), F32)

        @pl.when((c == 0) & (g == 0))
        def _():
            ddtb_ref[...] = jnp.zeros_like(ddtb_ref)
            dalog_ref[...] = jnp.zeros_like(dalog_ref)
            ddsk_ref[...] = jnp.zeros_like(ddsk_ref)
            dnw_ref[...] = jnp.zeros_like(dnw_ref)

        @pl.when(g == 0)
        def _():
            ddt_ref[...] = jnp.zeros_like(ddt_ref)

        _, vjp = jax.vjp(functools.partial(_ssd_chunk, g), _groups_of(x_ref, SSD_GS), _groups_of(z_ref, SSD_GS),
                         _groups_of(bm_ref, SSD_N), _groups_of(cm_ref, SSD_N), dt_ref[...], dtb_ref[...], alog_ref[...],
                         dsk_ref[...], _groups_of(nw_ref, SSD_GS), st_ref[0])
        dx, dz, dbm, dcm, ddt, ddtb, dalog, ddsk, dnw, dprev = vjp((_groups_of(dy_ref, SSD_GS), dstate[mine]))
        _put_groups(dz_ref, dz, SSD_GS)
        for gi in range(SSD_G // SSD_GB):
            @pl.when(g == gi)
            def _(gi=gi):
                for i in range(SSD_GB):
                    gg = gi * SSD_GB + i
                    dact_ref[:, gg * SSD_GS:(gg + 1) * SSD_GS] = dx[i]
                    dact_ref[:, SSD_DI + gg * SSD_N:SSD_DI + (gg + 1) * SSD_N] = dbm[i]
                    dact_ref[:, SSD_DI + (SSD_G + gg) * SSD_N:SSD_DI + (SSD_G + gg + 1) * SSD_N] = dcm[i]
        ddt_ref[...] += ddt
        ddtb_ref[...] += ddtb
        dalog_ref[...] += dalog
        ddsk_ref[...] += ddsk
        dnw_ref[mine] += dnw
        dstate[mine] = dprev

    par_out = pl.BlockSpec((1, LANES), lambda c, g: (0, 0))
    sds = jax.ShapeDtypeStruct
    return _hosted_call(
        body, name=name, grid=(nc, SSD_G // SSD_GB),
        in_specs=[sp["z"], sp["dt"], sp["x"], sp["bm"], sp["cm"], sp["par"], sp["par"], sp["par"], sp["nw"],
                  sp["st"], sp["y"]],
        out_specs=[pl.BlockSpec((SSD_L, SSD_CONV_DIM), lambda c, g: (nc - 1 - c, 0)), sp["y"], sp["dtout"],
                   par_out, par_out, par_out, pl.BlockSpec((SSD_G, 1, SSD_GS), lambda c, g: (0, 0, 0))],
        out_shape=[sds((t, SSD_CONV_DIM), F32), sds((t, SSD_DI), F32), sds((t, LANES), F32), sds((1, LANES), F32),
                   sds((1, LANES), F32), sds((1, LANES), F32), sds((SSD_G, 1, SSD_GS), F32)],
        scratch_shapes=[pltpu.VMEM((SSD_G, SSD_N, SSD_GS), F32)],
        sem=("arbitrary", "arbitrary"), args=(proj, proj, act, act, act, dtb, alog, dsk, nw, states, dyn), ride=ride)


def _pad_lanes(v, width=LANES, offset=0):
    return jnp.pad(v.astype(F32), (offset, width - offset - v.shape[0])).reshape(1, width)


def _ride_of(rides, key, scatter):
    arrs = (rides or {}).get(key)
    return (arrs, scatter) if arrs else None


def _mm_r(a, b, ride, **kw):
    out = _mm(a, b, ride=ride, **kw)
    return (out, []) if ride is None else out


def _ssd_layer_fwd(u, w, ln_g, ln_b, tag, rides=None, late=None):
    w_in = jnp.concatenate([w["in_w"], jnp.zeros((D_MODEL, LANES - SSD_H), w["in_w"].dtype)], axis=1)
    dtb, alog, dsk = _pad_lanes(w["dt_bias"]), _pad_lanes(w["a_log"]), _pad_lanes(w["d"])
    if late is None:
        proj = _mm(u.astype(MXU), w_in, name=tag + "_in")
    else:
        proj, got = _mm(u.astype(MXU), w_in, name=tag + "_in", ride=(late[0], False))
        w.update(late[1](got))
    nw = w["norm_w"].reshape(1, SSD_DI)
    cb = w["conv_b"].reshape(1, SSD_CONV_DIM)
    rode = {}
    act, rode["conv"] = _conv_fwd(proj, SSD_DI, w["conv_w"], cb, name=tag + "_conv", ride=_ride_of(rides, "conv", False))
    (yn, states), rode["scan"] = _ssd_fwd(proj, act, dtb, alog, dsk, nw, name=tag + "_scan", ride=_ride_of(rides, "scan", False))
    y, rode["out"] = _mm_r(yn, w["out_w"], _ride_of(rides, "out", False), name=tag + "_out")
    h = _ln_fwd(u, y, ln_g, ln_b, name=tag + "_ln")
    saved = dict(u=u, w_in=w_in, proj=proj, act=act, states=states, yn=yn, y=y, dtb=dtb, alog=alog, dsk=dsk, nw=nw, cb=cb)
    return h, saved, rode


def _ssd_layer_bwd(s, w, ln_g, dr_up, du_up, tag, rides=None, own=None):
    dr, dg, db = _ln_bwd(s["u"], s["y"], ln_g, dr_up, du_up, name=tag + "_ln_b")
    dyn = _mm(dr, w["out_w"], tb=True, name=tag + "_out_bx")
    d_out_w = _mm(s["yn"], dr, ta=True, out_dtype=WIRE, name=tag + "_out_bw")
    rode = {}
    (dact, dz, ddt, ddtb, dalog, ddsk, dnw), rode["scan_b"] = _ssd_bwd(
        s["proj"], s["act"], s["dtb"], s["alog"], s["dsk"], s["nw"], s["states"], dyn, name=tag + "_scan_b",
        ride=_ride_of(rides, "scan_b", True))
    dxbc, d_conv_w, d_conv_b = _conv_bwd(s["proj"], SSD_DI, w["conv_w"], s["cb"], dact, name=tag + "_conv_b")
    dproj = jnp.concatenate([dz, dxbc, ddt], axis=1).astype(MXU)
    grads = dict(conv_w=d_conv_w, conv_b=d_conv_b.reshape(-1), dt_bias=ddtb[0, :SSD_H], a_log=dalog[0, :SSD_H],
                 d=ddsk[0, :SSD_H], norm_w=dnw.reshape(-1), out_w=d_out_w, ln_g=dg[0], ln_b=db[0])
    late_names = ("conv_w", "conv_b", "norm_w", "out_w")
    got = {}
    if own is None:
        d_in_w = _mm(s["u"].astype(MXU), dproj, ta=True, out_dtype=WIRE, name=tag + "_in_bw")
        du = _mm(dproj, s["w_in"], tb=True, name=tag + "_in_bx")
        grads["in_w"] = d_in_w[:, :SSD_DI + SSD_CONV_DIM + SSD_H]
    else:
        u_mx = s["u"].astype(MXU)
        half, keep = D_MODEL // 2, SSD_DI + SSD_CONV_DIM + SSD_H
        d_a, late = _mm(u_mx[:, :half], dproj, ta=True, out_dtype=WIRE, name=tag + "_in_bw",
                        ride=([own(k, grads[k]) for k in late_names], True))
        got = dict(zip(late_names, late))
        d_b, (got_a,) = _mm(u_mx[:, half:], dproj, ta=True, out_dtype=WIRE, name=tag + "_in_bw2",
                            ride=([own("in_w", d_a[:, :keep])], True))
        du, (got_b,) = _mm(dproj, s["w_in"], tb=True, name=tag + "_in_bx", ride=([own("in_w", d_b[:, :keep])], True))
        got["in_w"] = jnp.concatenate([got_a, got_b], axis=1)
    return dr, du, grads, rode, got


MLA_LOW = MLA_QR + MLA_KVR + LANES
MLA_ZB = MLA_LOW // LANES


def _rope_mat():
    r = lax.broadcasted_iota(jnp.int32, (LANES, LANES), 0)
    c = lax.broadcasted_iota(jnp.int32, (LANES, LANES), 1)
    hf = MLA_ROPE // 2
    return jnp.where((c < hf) & (r == c + hf), -1.0, 0.0) + jnp.where((c >= hf) & (c < 2 * hf) & (r == c - hf), 1.0, 0.0)


def _rope(x, cosf, sinf):
    return x * cosf + _sel_r(x, _rope_mat()) * sinf


def _rope_adj(d, cosf, sinf):
    return d * cosf - _sel_r(d * sinf, _rope_mat())


def _mla_low_fn(low, qnw, kvnw, cosf, sinf):
    qc, kvc, kr = low[:, :MLA_QR], low[:, MLA_QR:MLA_QR + MLA_KVR], low[:, MLA_QR + MLA_KVR:]
    qn = qc * lax.rsqrt(jnp.mean(qc * qc, -1, keepdims=True) + RMS_EPS) * qnw
    kvn = kvc * lax.rsqrt(jnp.mean(kvc * kvc, -1, keepdims=True) + RMS_EPS) * kvnw
    return qn, kvn, _rope(kr, cosf, sinf)


def _mla_low_fwd(proj, qnw, kvnw, cosf, sinf, name):
    t = proj.shape[0]
    tr = _tile(t, 256)

    def body(low_ref, qnw_ref, kvnw_ref, cos_ref, sin_ref, qn_ref, kvn_ref, kr_ref):
        qn, kvn, kr = _mla_low_fn(low_ref[...], qnw_ref[...], kvnw_ref[...], cos_ref[...], sin_ref[...])
        qn_ref[...] = qn
        kvn_ref[...] = kvn
        kr_ref[...] = kr

    row = lambda wdt: pl.BlockSpec((tr, wdt), lambda i: (i, 0))
    par = lambda wdt: pl.BlockSpec((1, wdt), lambda i: (0, 0))
    sds = jax.ShapeDtypeStruct
    return pl.pallas_call(
        body, name=name, grid=(t // tr,),
        in_specs=[row(MLA_LOW), par(MLA_QR), par(MLA_KVR), row(LANES), row(LANES)],
        out_specs=[row(MLA_QR), row(MLA_KVR), row(LANES)],
        out_shape=[sds((t, MLA_QR), F32), sds((t, MLA_KVR), F32), sds((t, LANES), F32)],
        compiler_params=_cparams(("parallel",)),
    )(proj, qnw, kvnw, cosf, sinf)


def _mla_low_bwd(proj, qnw, kvnw, cosf, sinf, dqn, dkvn, dkr, name):
    t = proj.shape[0]
    tr = _tile(t, 256)

    def body(low_ref, qnw_ref, kvnw_ref, cos_ref, sin_ref, dqn_ref, dkvn_ref, dkr_ref, dlow_ref, dqnw_ref, dkvnw_ref):
        i = pl.program_id(0)

        @pl.when(i == 0)
        def _():
            dqnw_ref[...] = jnp.zeros_like(dqnw_ref)
            dkvnw_ref[...] = jnp.zeros_like(dkvnw_ref)

        cosf, sinf = cos_ref[...], sin_ref[...]
        _, vjp = jax.vjp(lambda a, b, c: _mla_low_fn(a, b, c, cosf, sinf), low_ref[...], qnw_ref[...], kvnw_ref[...])
        dlow, dq, dk = vjp((dqn_ref[...], dkvn_ref[...], dkr_ref[...]))
        dlow_ref[...] = dlow
        dqnw_ref[...] += dq
        dkvnw_ref[...] += dk

    row = lambda wdt: pl.BlockSpec((tr, wdt), lambda i: (i, 0))
    par = lambda wdt: pl.BlockSpec((1, wdt), lambda i: (0, 0))
    sds = jax.ShapeDtypeStruct
    return pl.pallas_call(
        body, name=name, grid=(t // tr,),
        in_specs=[row(MLA_LOW), par(MLA_QR), par(MLA_KVR), row(LANES), row(LANES), row(MLA_QR), row(MLA_KVR), row(LANES)],
        out_specs=[row(MLA_LOW), par(MLA_QR), par(MLA_KVR)],
        out_shape=[sds((t, MLA_LOW), F32), sds((1, MLA_QR), F32), sds((1, MLA_KVR), F32)],
        compiler_params=_cparams(("arbitrary",)),
    )(proj, qnw, kvnw, cosf, sinf, dqn, dkvn, dkr)


def _rope_heads(x, col_blk0, cosf, sinf, adjoint, name):
    t = x.shape[0]
    tr = _tile(t, 512)
    hb = 4
    assert col_blk0 % hb == 0

    def body(x_ref, cos_ref, sin_ref, o_ref):
        f = _rope_adj if adjoint else _rope
        for b in range(hb):
            o_ref[:, b * LANES:(b + 1) * LANES] = f(x_ref[:, b * LANES:(b + 1) * LANES], cos_ref[...], sin_ref[...])

    tab = pl.BlockSpec((tr, LANES), lambda i, h: (i, 0))
    return pl.pallas_call(
        body, name=name, grid=(t // tr, MLA_H // hb),
        in_specs=[pl.BlockSpec((tr, hb * LANES), lambda i, h: (i, col_blk0 // hb + h)), tab, tab],
        out_specs=pl.BlockSpec((tr, hb * LANES), lambda i, h: (i, h)),
        out_shape=jax.ShapeDtypeStruct((t, MLA_H * LANES), F32), compiler_params=_cparams(("parallel", "parallel")),
    )(x, cosf, sinf)


ATT_HB = 4
ATT_W = ATT_HB * LANES


def _att_qk(qn_ref, qr_ref, kn_ref, kr_ref):
    q2 = jnp.concatenate([_heads_of(qn_ref, ATT_HB), _heads_of(qr_ref, ATT_HB)], axis=2)
    kr = kr_ref[...]
    k2 = jnp.concatenate([_heads_of(kn_ref, ATT_HB), jnp.broadcast_to(kr[None], (ATT_HB,) + kr.shape)], axis=2)
    return q2, k2


def _att_scores(q2, k2, masked):
    s = _bdg(_mx(q2 * MLA_SCALE), _mx(k2), 2, 2)
    if masked:
        tq, tk = s.shape[1:]
        s = jnp.where(lax.broadcasted_iota(jnp.int32, (tq, tk), 1) <= lax.broadcasted_iota(jnp.int32, (tq, tk), 0), s, -jnp.inf)
    return s


def _on_causal_blocks(q_blk, k_blk, step):
    @pl.when(k_blk < q_blk)
    def _():
        step(False)

    @pl.when(k_blk == q_blk)
    def _():
        step(True)


def _put_heads(ref, val):
    for b in range(val.shape[0]):
        ref[:, b * LANES:(b + 1) * LANES] = val[b].astype(ref.dtype)


def _att_ds(s, v_ref, o_ref, do_ref, lse_ref):
    do = _heads_of(do_ref, ATT_HB)
    p = jnp.exp(s - _heads_of(lse_ref, ATT_HB)[:, :, 0:1])
    dp = _bdg(_mx(do), _mx(_heads_of(v_ref, ATT_HB)), 2, 2)
    dl = jnp.sum(do * _heads_of(o_ref, ATT_HB), -1, keepdims=True)
    return _mx(p), _mx(p * (dp - dl) * MLA_SCALE), do


def _attn_fwd(q, qr, kv, kr, proj, name, ride=None):
    t = q.shape[0]
    tq = tk = _tile(t, ATT_BLK)
    nq = nk = t // tq

    def body(qn_ref, qr_ref, kn_ref, kr_ref, v_ref, *rest):
        z_refs, (o_ref, og_ref, lse_ref, m_s, l_s, acc_s) = rest[:ATT_HB], rest[ATT_HB:]
        i, j = pl.program_id(1), pl.program_id(2)

        @pl.when(j == 0)
        def _():
            m_s[...] = jnp.full_like(m_s, -jnp.inf)
            l_s[...] = jnp.zeros_like(l_s)
            acc_s[...] = jnp.zeros_like(acc_s)

        def step(masked):
            s = _att_scores(*_att_qk(qn_ref, qr_ref, kn_ref, kr_ref), masked)
            m_new = jnp.maximum(m_s[...], jnp.max(s, -1, keepdims=True))
            p = jnp.exp(s - m_new)
            corr = jnp.exp(m_s[...] - m_new)
            l_s[...] = corr * l_s[...] + jnp.sum(p, -1, keepdims=True)
            acc_s[...] = corr * acc_s[...] + _bdg(_mx(p), _mx(_heads_of(v_ref, ATT_HB)), 2, 1)
            m_s[...] = m_new

        _on_causal_blocks(i, j, step)

        @pl.when(j == nk - 1)
        def _():
            o = acc_s[...] / l_s[...]
            _put_heads(o_ref, o)
            lse = m_s[...] + jnp.log(l_s[...])
            for b in range(ATT_HB):
                og_ref[:, b * LANES:(b + 1) * LANES] = (o[b] * _silu(z_refs[b][...])).astype(og_ref.dtype)
                lse_ref[:, b * LANES:(b + 1) * LANES] = jnp.broadcast_to(lse[b], (tq, LANES))

    qs = lambda off: pl.BlockSpec((tq, ATT_W), lambda h, i, j: (i, off // ATT_HB + h))
    ks = lambda off: pl.BlockSpec((tk, ATT_W), lambda h, i, j: (jnp.minimum(j, i), off // ATT_HB + h))
    zs = [pl.BlockSpec((tq, LANES), functools.partial(lambda b, h, i, j: (i, MLA_ZB + h * ATT_HB + b), b)) for b in range(ATT_HB)]
    sds = jax.ShapeDtypeStruct
    return _hosted_call(
        body, name=name, grid=(MLA_H // ATT_HB, nq, nk),
        in_specs=[qs(0), qs(0), ks(0), pl.BlockSpec((tk, LANES), lambda h, i, j: (jnp.minimum(j, i), 0)), ks(MLA_H)] + zs,
        out_specs=[qs(0), qs(0), qs(0)],
        out_shape=[sds((t, MLA_GATE), F32), sds((t, MLA_GATE), MXU), sds((t, MLA_H * LANES), F32)],
        scratch_shapes=[pltpu.VMEM((ATT_HB, tq, 1), F32), pltpu.VMEM((ATT_HB, tq, 1), F32), pltpu.VMEM((ATT_HB, tq, LANES), F32)],
        sem=("parallel", "parallel", "arbitrary"), args=(q, qr, kv, kr, kv) + (proj,) * ATT_HB, ride=ride)


def _gate_bwd(dog, o, proj, name):
    t = o.shape[0]
    tr = _tile(t, 512)

    def body(d_ref, o_ref, z_ref, do_ref, dz_ref):
        z = z_ref[...]
        sg = jax.nn.sigmoid(z)
        d = d_ref[...]
        do_ref[...] = d * z * sg
        dz_ref[...] = d * o_ref[...] * (sg * (1.0 + z * (1.0 - sg)))

    blk = lambda off: pl.BlockSpec((tr, 512), lambda i, j: (i, off + j))
    assert MLA_LOW % 512 != 0 or True
    zspec = pl.BlockSpec((tr, LANES), lambda i, j: (i, MLA_ZB + j))
    b128 = pl.BlockSpec((tr, LANES), lambda i, j: (i, j))
    sds = jax.ShapeDtypeStruct
    return pl.pallas_call(
        body, name=name, grid=(t // tr, MLA_GATE // LANES),
        in_specs=[b128, b128, zspec], out_specs=[b128, b128],
        out_shape=[sds((t, MLA_GATE), F32), sds((t, MLA_GATE), F32)],
        compiler_params=_cparams(("parallel", "parallel")),
    )(dog, o, proj)


def _attn_bwd_q(q, qr, kv, kr, o, do, lse, name, ride=None):
    t = q.shape[0]
    tq = tk = _tile(t, ATT_BLK)
    nq = nk = t // tq

    def body(qn_ref, qr_ref, kn_ref, kr_ref, v_ref, o_ref, do_ref, lse_ref, dqn_ref, dqr_ref, an_s, ar_s):
        i, j = pl.program_id(1), pl.program_id(2)

        @pl.when(j == 0)
        def _():
            an_s[...] = jnp.zeros_like(an_s)
            ar_s[...] = jnp.zeros_like(ar_s)

        def step(masked):
            q2, k2 = _att_qk(qn_ref, qr_ref, kn_ref, kr_ref)
            ds = _att_ds(_att_scores(q2, k2, masked), v_ref, o_ref, do_ref, lse_ref)[1]
            dq2 = _bdg(ds, _mx(k2), 2, 1)
            an_s[...] += dq2[:, :, :LANES]
            ar_s[...] += dq2[:, :, LANES:]

        _on_causal_blocks(i, j, step)

        @pl.when(j == nk - 1)
        def _():
            _put_heads(dqn_ref, an_s[...])
            _put_heads(dqr_ref, ar_s[...])

    qs = lambda off: pl.BlockSpec((tq, ATT_W), lambda h, i, j: (i, off // ATT_HB + h))
    ks = lambda off: pl.BlockSpec((tk, ATT_W), lambda h, i, j: (jnp.minimum(j, i), off // ATT_HB + h))
    sds = jax.ShapeDtypeStruct
    return _hosted_call(
        body, name=name, grid=(MLA_H // ATT_HB, nq, nk),
        in_specs=[qs(0), qs(0), ks(0), pl.BlockSpec((tk, LANES), lambda h, i, j: (jnp.minimum(j, i), 0)), ks(MLA_H),
                  qs(0), qs(0), qs(0)],
        out_specs=[qs(0), qs(0)],
        out_shape=[sds((t, MLA_H * LANES), F32), sds((t, MLA_H * LANES), F32)],
        scratch_shapes=[pltpu.VMEM((ATT_HB, tq, LANES), F32), pltpu.VMEM((ATT_HB, tq, LANES), F32)],
        sem=("parallel", "parallel", "arbitrary"), args=(q, qr, kv, kr, kv, o, do, lse), ride=ride)


def _attn_bwd_kv(q, qr, kv, kr, o, do, lse, name, ride=None):
    t = q.shape[0]
    tq = tk = _tile(t, ATT_BLK)
    nq = nk = t // tq

    def body(qn_ref, qr_ref, kn_ref, kr_ref, v_ref, o_ref, do_ref, lse_ref, dkn_ref, dv_ref, dkr_ref, akn_s, av_s):
        j, h, i = pl.program_id(0), pl.program_id(1), pl.program_id(2)

        @pl.when((h == 0) & (i == 0))
        def _():
            dkr_ref[...] = jnp.zeros_like(dkr_ref)

        @pl.when(i == 0)
        def _():
            akn_s[...] = jnp.zeros_like(akn_s)
            av_s[...] = jnp.zeros_like(av_s)

        def step(masked):
            q2, k2 = _att_qk(qn_ref, qr_ref, kn_ref, kr_ref)
            p, ds, do = _att_ds(_att_scores(q2, k2, masked), v_ref, o_ref, do_ref, lse_ref)
            av_s[...] += _bdg(p, _mx(do), 1, 1)
            dk2 = _bdg(ds, _mx(q2), 1, 1)
            akn_s[...] += dk2[:, :, :LANES]
            dkr_ref[...] += jnp.sum(dk2[:, :, LANES:], axis=0)

        _on_causal_blocks(i, j, step)

        @pl.when(i == nq - 1)
        def _():
            _put_heads(dkn_ref, akn_s[...])
            _put_heads(dv_ref, av_s[...])

    qs = lambda off: pl.BlockSpec((tq, ATT_W), lambda j, h, i: (jnp.maximum(i, j), off // ATT_HB + h))
    ks = lambda off: pl.BlockSpec((tk, ATT_W), lambda j, h, i: (j, off // ATT_HB + h))
    sds = jax.ShapeDtypeStruct
    return _hosted_call(
        body, name=name, grid=(nk, MLA_H // ATT_HB, nq),
        in_specs=[qs(0), qs(0), ks(0), pl.BlockSpec((tk, LANES), lambda j, h, i: (j, 0)), ks(MLA_H), qs(0), qs(0), qs(0)],
        out_specs=[ks(0), ks(0), pl.BlockSpec((tk, LANES), lambda j, h, i: (j, 0))],
        out_shape=[sds((t, MLA_H * LANES), F32), sds((t, MLA_H * LANES), F32), sds((t, LANES), F32)],
        scratch_shapes=[pltpu.VMEM((ATT_HB, tk, LANES), F32), pltpu.VMEM((ATT_HB, tk, LANES), F32)],
        sem=("parallel", "arbitrary", "arbitrary"), args=(q, qr, kv, kr, kv, o, do, lse), ride=ride)


def _rope_tables(positions):
    lane = jnp.arange(LANES)
    valid = lane < MLA_ROPE
    inv_freq = ROPE_THETA ** (-(2 * (lane % (MLA_ROPE // 2))).astype(F32) / MLA_ROPE)
    ang = positions.astype(F32)[:, None] * inv_freq[None, :]
    return jnp.where(valid, jnp.cos(ang), 0.0), jnp.where(valid, jnp.sin(ang), 0.0)


def _mla_weights(w):
    dt = w["in_w"].dtype
    iw = w["in_w"]
    c1 = MLA_QR + MLA_KVR + MLA_ROPE
    w_in = jnp.concatenate([iw[:, :c1], jnp.zeros((D_MODEL, LANES - MLA_ROPE), dt), iw[:, c1:]], axis=1)
    qu = w["q_up_w"].reshape(MLA_QR, MLA_H, MLA_NOPE + MLA_ROPE)
    qrope = jnp.concatenate([qu[:, :, MLA_NOPE:], jnp.zeros((MLA_QR, MLA_H, LANES - MLA_ROPE), dt)], axis=2)
    w_q = jnp.concatenate([qu[:, :, :MLA_NOPE].reshape(MLA_QR, -1), qrope.reshape(MLA_QR, -1)], axis=1)
    kvu = w["kv_up_w"].reshape(MLA_KVR, MLA_H, MLA_NOPE + MLA_V)
    w_kv = jnp.concatenate([kvu[:, :, :MLA_NOPE].reshape(MLA_KVR, -1), kvu[:, :, MLA_NOPE:].reshape(MLA_KVR, -1)], axis=1)
    return w_in, w_q, w_kv


def _mla_layer_fwd(u, w, ln_g, ln_b, cosf, sinf, tag, rides=None):
    w_in, w_q, w_kv = _mla_weights(w)
    qnw, kvnw = w["q_norm_w"].reshape(1, -1), w["kv_norm_w"].reshape(1, -1)
    proj = _mm(u.astype(MXU), w_in, name=tag + "_in")
    qn, kvn, kr = _mla_low_fwd(proj, qnw, kvnw, cosf, sinf, name=tag + "_low")
    q = _mm(qn, w_q, name=tag + "_qup")
    kv = _mm(kvn, w_kv, name=tag + "_kvup")
    qr = _rope_heads(q, MLA_H, cosf, sinf, False, name=tag + "_qrope")
    rode = {}
    (o, og, lse), rode["attn"] = _attn_fwd(q, qr, kv, kr, proj, name=tag + "_attn", ride=_ride_of(rides, "attn", False))
    y = _mm(og, w["out_w"], name=tag + "_out")
    h = _ln_fwd(u, y, ln_g, ln_b, name=tag + "_ln")
    saved = dict(u=u, w_in=w_in, w_q=w_q, w_kv=w_kv, qnw=qnw, kvnw=kvnw, proj=proj, qn=qn, kvn=kvn, kr=kr, q=q, kv=kv,
                 qr=qr, o=o, og=og, lse=lse, y=y)
    return h, saved, rode


def _mla_layer_bwd(s, w, ln_g, cosf, sinf, dr_up, du_up, tag, rides=None):
    dr, dg, db = _ln_bwd(s["u"], s["y"], ln_g, dr_up, du_up, name=tag + "_ln_b")
    dog = _mm(dr, w["out_w"], tb=True, name=tag + "_out_bx")
    d_out_w = _mm(s["og"], dr, ta=True, out_dtype=WIRE, name=tag + "_out_bw")
    do, dz = _gate_bwd(dog, s["o"], s["proj"], name=tag + "_gate_b")
    rode = {}
    (dqn_h, dqr_rot), rode["attn_bq"] = _attn_bwd_q(s["q"], s["qr"], s["kv"], s["kr"], s["o"], do, s["lse"],
                                                    name=tag + "_attn_bq", ride=_ride_of(rides, "attn_bq", True))
    (dkn_h, dv_h, dkr_rot), rode["attn_bkv"] = _attn_bwd_kv(s["q"], s["qr"], s["kv"], s["kr"], s["o"], do, s["lse"],
                                                           name=tag + "_attn_bkv", ride=_ride_of(rides, "attn_bkv", True))
    dqr = _rope_heads(dqr_rot, 0, cosf, sinf, True, name=tag + "_qrope_b")
    dq = jnp.concatenate([dqn_h, dqr], axis=1).astype(MXU)
    dkv = jnp.concatenate([dkn_h, dv_h], axis=1).astype(MXU)
    d_wq = _mm(s["qn"], dq, ta=True, out_dtype=WIRE, name=tag + "_qup_bw")
    dqn = _mm(dq, s["w_q"], tb=True, name=tag + "_qup_bx")
    d_wkv = _mm(s["kvn"], dkv, ta=True, out_dtype=WIRE, name=tag + "_kvup_bw")
    dkvn = _mm(dkv, s["w_kv"], tb=True, name=tag + "_kvup_bx")
    dlow, dqnw, dkvnw = _mla_low_bwd(s["proj"], s["qnw"], s["kvnw"], cosf, sinf, dqn, dkvn, dkr_rot, name=tag + "_low_b")
    dproj = jnp.concatenate([dlow, dz], axis=1).astype(MXU)
    d_in = _mm(s["u"].astype(MXU), dproj, ta=True, out_dtype=WIRE, name=tag + "_in_bw")
    du = _mm(dproj, s["w_in"], tb=True, name=tag + "_in_bx")
    c1 = MLA_QR + MLA_KVR + MLA_ROPE
    d_in_w = jnp.concatenate([d_in[:, :c1], d_in[:, MLA_LOW:]], axis=1)
    dq3n = d_wq[:, :MLA_H * MLA_NOPE].reshape(MLA_QR, MLA_H, MLA_NOPE)
    dq3r = d_wq[:, MLA_H * MLA_NOPE:].reshape(MLA_QR, MLA_H, LANES)[:, :, :MLA_ROPE]
    d_q_up = jnp.concatenate([dq3n, dq3r], axis=2).reshape(MLA_QR, -1)
    dkv3 = d_wkv.reshape(MLA_KVR, 2, MLA_H, MLA_NOPE)
    d_kv_up = jnp.concatenate([dkv3[:, 0], dkv3[:, 1]], axis=2).reshape(MLA_KVR, -1)
    grads = dict(in_w=d_in_w, q_norm_w=dqnw[0], q_up_w=d_q_up, kv_norm_w=dkvnw[0], kv_up_w=d_kv_up, out_w=d_out_w,
                 ln_g=dg[0], ln_b=db[0])
    return dr, du, grads, rode


GDN_REP = GDN_HV // GDN_HK
GDN_A_LANE = GDN_HV
GDN_HPB = 8
GDN_VPB = GDN_HPB * GDN_REP


def _bdg(a, b, ca, cb):
    return lax.dot_general(a, b, (((ca,), (cb,)), ((0,), (0,))), preferred_element_type=F32)


@jax.custom_vjp
def _bnn(a, b):
    return _bdg(_mx(a), _mx(b), 2, 1)


def _bnn_f(a, b):
    return _bnn(a, b), (a, b)


def _bnn_b(res, ct):
    a, b = res
    return _bdg(_mx(ct), _mx(b), 2, 2), _bdg(_mx(a), _mx(ct), 1, 1)


_bnn.defvjp(_bnn_f, _bnn_b)


@jax.custom_vjp
def _bnt(a, b):
    return _bdg(_mx(a), _mx(b), 2, 2)


def _bnt_f(a, b):
    return _bnt(a, b), (a, b)


def _bnt_b(res, ct):
    a, b = res
    return _bdg(_mx(ct), _mx(b), 2, 1), _bdg(_mx(ct), _mx(a), 1, 1)


_bnt.defvjp(_bnt_f, _bnt_b)


@jax.custom_vjp
def _btn(a, b):
    return _bdg(_mx(a), _mx(b), 1, 1)


def _btn_f(a, b):
    return _btn(a, b), (a, b)


def _btn_b(res, ct):
    a, b = res
    return _bdg(_mx(b), _mx(ct), 2, 2), _bdg(_mx(a), _mx(ct), 2, 1)


_btn.defvjp(_btn_f, _btn_b)


def _h3(a, b, ca=2, cb=1):
    ah, bh = _mx(a), _mx(b)
    al, bl = _mx(a - ah.astype(F32)), _mx(b - bh.astype(F32))
    return _bdg(ah, bh, ca, cb) + (_bdg(ah, bl, ca, cb) + _bdg(al, bh, ca, cb))


@jax.custom_vjp
def _neumann_inverse(x):
    L = x.shape[-1]
    eye = (lax.broadcasted_iota(jnp.int32, (L, L), 0) == lax.broadcasted_iota(jnp.int32, (L, L), 1)).astype(F32)
    inv = eye + x
    xp = x
    for _ in range(L.bit_length() - 2):
        xp = _h3(xp, xp)
        inv = inv + _h3(inv, xp)
    return inv


def _neumann_f(x):
    inv = _neumann_inverse(x)
    return inv, inv


def _neumann_b(inv, ct):
    return (_h3(_h3(inv, ct, 1, 1), inv, 2, 2),)


_neumann_inverse.defvjp(_neumann_f, _neumann_b)


@jax.custom_vjp
def _saved_inverse(x, inv):
    return inv


def _saved_f(x, inv):
    return inv, inv


def _saved_b(inv, ct):
    return _neumann_b(inv, ct)[0], jnp.zeros_like(inv)


_saved_inverse.defvjp(_saved_f, _saved_b)


def _cat0(parts):
    return jnp.concatenate([p[None] for p in parts], axis=0)


def _gdn_chunk(hb, q, k, v, z, ba, alog, dtb, nw, s, inv_saved=None):
    L = q.shape[1]
    r_i = lax.broadcasted_iota(jnp.int32, (L, L), 0)
    c_i = lax.broadcasted_iota(jnp.int32, (L, L), 1)
    incl, strict = r_i >= c_i, r_i > c_i
    rep = lambda t: jnp.broadcast_to(t[:, None], (GDN_HPB, GDN_REP) + t.shape[1:]).reshape((GDN_VPB,) + t.shape[1:])
    qn = rep(q * lax.rsqrt(jnp.sum(q * q, -1, keepdims=True) + RMS_EPS) * (GDN_DK ** -0.5))
    kn = rep(k * lax.rsqrt(jnp.sum(k * k, -1, keepdims=True) + RMS_EPS))
    beta_all = jax.nn.sigmoid(ba)
    g_all = -jnp.exp(alog) * _softplus(ba + dtb)
    gcs_all = _sel_l(incl.astype(F32), g_all)
    lane = lax.broadcasted_iota(jnp.int32, (L, LANES), 1)
    pick = lambda mat, idx: jnp.sum(jnp.where(lane == idx, mat, 0.0), axis=1, keepdims=True)
    beta = _cat0([pick(beta_all, GDN_VPB * hb + b) for b in range(GDN_VPB)])
    gc = _cat0([pick(gcs_all, GDN_A_LANE + GDN_VPB * hb + b) for b in range(GDN_VPB)])
    gm = jnp.broadcast_to(gc, (GDN_VPB, L, L))
    decay = jnp.exp(jnp.where(incl, gm - jnp.swapaxes(gm, 1, 2), -jnp.inf))
    kb = kn * beta
    eg = jnp.exp(gc)
    x = -jnp.where(strict, _bnt(kb, kn) * decay, 0.0)
    inv = _neumann_inverse(x) if inv_saved is None else _saved_inverse(x, inv_saved)
    uw = _bnn(inv, jnp.concatenate([v * beta, kb * eg], axis=2))
    uu, ww = uw[:, :, :GDN_DV], uw[:, :, GDN_DV:]
    qk = jnp.where(incl, _bnt(qn, kn) * decay, 0.0)
    last = lax.broadcasted_iota(jnp.int32, (L, 1), 0) == L - 1
    glast = jnp.sum(jnp.where(last, gc, 0.0), axis=1, keepdims=True)
    kdec = kn * jnp.exp(glast - gc)
    vnew = uu - _bnn(ww, s)
    o = _bnn(qn * eg, s) + _bnn(qk, vnew)
    new = s * jnp.exp(glast) + _btn(kdec, vnew)
    on = o * lax.rsqrt(jnp.mean(o * o, -1, keepdims=True) + RMS_EPS) * nw * _silu(z)
    return (on, new, inv) if inv_saved is None else (on, new)


def _heads_of(ref, n):
    return _cat0([ref[:, i * LANES:(i + 1) * LANES] for i in range(n)])


def _gdn_specs(nc, rev):
    cc = (lambda c: nc - 1 - c) if rev else (lambda c: c)
    wq, wv = GDN_HPB * GDN_DK, GDN_VPB * GDN_DV
    par = pl.BlockSpec((1, LANES), lambda c, h: (0, 0))
    return dict(
        q=pl.BlockSpec((GDN_L, wq), lambda c, h: (cc(c), h)),
        k=pl.BlockSpec((GDN_L, wq), lambda c, h: (cc(c), GDN_KEY // wq + h)),
        v=pl.BlockSpec((GDN_L, wv), lambda c, h: (cc(c), 2 * GDN_KEY // wv + h)),
        z=pl.BlockSpec((GDN_L, wv), lambda c, h: (cc(c), GDN_CONV_DIM // wv + h)),
        ba=pl.BlockSpec((GDN_L, LANES), lambda c, h: (cc(c), (GDN_CONV_DIM + GDN_VAL) // LANES)),
        par=par,
        st=pl.BlockSpec((1, GDN_VPB, GDN_DK, GDN_DV), lambda c, h: (cc(c), h, 0, 0)),
        inv=pl.BlockSpec((1, GDN_VPB, GDN_L, GDN_L), lambda c, h: (cc(c), h, 0, 0)),
        o=pl.BlockSpec((GDN_L, wv), lambda c, h: (cc(c), h)),
        qk_out=pl.BlockSpec((GDN_L, wq), lambda c, h: (cc(c), h)),
        ba_out=pl.BlockSpec((GDN_L, LANES), lambda c, h: (cc(c), 0)),
    )


def _gdn_fwd(proj, act, alog, dtb, nw, name, ride=None):
    t = proj.shape[0]
    nc = t // GDN_L
    sp = _gdn_specs(nc, False)

    def body(q_ref, k_ref, v_ref, z_ref, ba_ref, alog_ref, dtb_ref, nw_ref, o_ref, st_ref, inv_ref, state):
        c, h = pl.program_id(0), pl.program_id(1)

        mine = pl.ds(h * GDN_VPB, GDN_VPB)

        @pl.when(c == 0)
        def _():
            state[mine] = jnp.zeros((GDN_VPB, GDN_DK, GDN_DV), F32)

        prev = state[mine]
        st_ref[0] = prev
        on, new, inv = _gdn_chunk(h, _heads_of(q_ref, GDN_HPB), _heads_of(k_ref, GDN_HPB), _heads_of(v_ref, GDN_VPB),
                                  _heads_of(z_ref, GDN_VPB), ba_ref[...], alog_ref[...], dtb_ref[...], nw_ref[...], prev)
        inv_ref[0] = inv
        for b in range(GDN_VPB):
            o_ref[:, b * LANES:(b + 1) * LANES] = on[b].astype(o_ref.dtype)
        state[mine] = new

    sds = jax.ShapeDtypeStruct
    return _hosted_call(
        body, name=name, grid=(nc, GDN_HK // GDN_HPB),
        in_specs=[sp["q"], sp["k"], sp["v"], sp["z"], sp["ba"], sp["par"], sp["par"], sp["par"]],
        out_specs=[sp["o"], sp["st"], sp["inv"]],
        out_shape=[sds((t, GDN_VAL), MXU), sds((nc, GDN_HV, GDN_DK, GDN_DV), F32), sds((nc, GDN_HV, GDN_L, GDN_L), F32)],
        scratch_shapes=[pltpu.VMEM((GDN_HV, GDN_DK, GDN_DV), F32)],
        sem=("arbitrary", "arbitrary"), args=(act, act, act, proj, proj, alog, dtb, nw), ride=ride)


def _gdn_bwd(proj, act, alog, dtb, nw, states, invs, don, name, ride=None):
    t = proj.shape[0]
    nc = t // GDN_L
    sp = _gdn_specs(nc, True)

    def body(q_ref, k_ref, v_ref, z_ref, ba_ref, alog_ref, dtb_ref, nw_ref, st_ref, do_ref, inv_ref,
             dact_ref, dz_ref, dba_ref, dalog_ref, ddtb_ref, dnw_ref, dstate):
        c, h = pl.program_id(0), pl.program_id(1)

        mine = pl.ds(h * GDN_VPB, GDN_VPB)

        @pl.when(c == 0)
        def _():
            dstate[mine] = jnp.zeros((GDN_VPB, GDN_DK, GDN_DV), F32)

        @pl.when((c == 0) & (h == 0))
        def _():
            dalog_ref[...] = jnp.zeros_like(dalog_ref)
            ddtb_ref[...] = jnp.zeros_like(ddtb_ref)
            dnw_ref[...] = jnp.zeros_like(dnw_ref)

        @pl.when(h == 0)
        def _():
            dba_ref[...] = jnp.zeros_like(dba_ref)

        _, vjp = jax.vjp(functools.partial(_gdn_chunk, h, inv_saved=inv_ref[0]), _heads_of(q_ref, GDN_HPB),
                         _heads_of(k_ref, GDN_HPB), _heads_of(v_ref, GDN_VPB), _heads_of(z_ref, GDN_VPB), ba_ref[...],
                         alog_ref[...], dtb_ref[...], nw_ref[...], st_ref[0])
        dq, dk, dv, dz, dba, dalog, ddtb, dnw, dprev = vjp((_heads_of(do_ref, GDN_VPB), dstate[mine]))
        for b in range(GDN_VPB):
            dz_ref[:, b * LANES:(b + 1) * LANES] = dz[b]
        for hi in range(GDN_HK // GDN_HPB):
            @pl.when(h == hi)
            def _(hi=hi):
                for i in range(GDN_HPB):
                    c0 = (hi * GDN_HPB + i) * GDN_DK
                    dact_ref[:, c0:c0 + GDN_DK] = dq[i]
                    dact_ref[:, GDN_KEY + c0:GDN_KEY + c0 + GDN_DK] = dk[i]
                for b in range(GDN_VPB):
                    c0 = 2 * GDN_KEY + (hi * GDN_VPB + b) * GDN_DV
                    dact_ref[:, c0:c0 + GDN_DV] = dv[b]
        dba_ref[...] += dba
        dalog_ref[...] += dalog
        ddtb_ref[...] += ddtb
        dnw_ref[...] += dnw
        dstate[mine] = dprev

    sds = jax.ShapeDtypeStruct
    par_out = pl.BlockSpec((1, LANES), lambda c, h: (0, 0))
    return _hosted_call(
        body, name=name, grid=(nc, GDN_HK // GDN_HPB),
        in_specs=[sp["q"], sp["k"], sp["v"], sp["z"], sp["ba"], sp["par"], sp["par"], sp["par"], sp["st"], sp["o"],
                  sp["inv"]],
        out_specs=[pl.BlockSpec((GDN_L, GDN_CONV_DIM), lambda c, h: (nc - 1 - c, 0)), sp["o"], sp["ba_out"],
                   par_out, par_out, par_out],
        out_shape=[sds((t, GDN_CONV_DIM), F32), sds((t, GDN_VAL), F32), sds((t, LANES), F32), sds((1, LANES), F32),
                   sds((1, LANES), F32), sds((1, LANES), F32)],
        scratch_shapes=[pltpu.VMEM((GDN_HV, GDN_DK, GDN_DV), F32)],
        sem=("arbitrary", "arbitrary"), args=(act, act, act, proj, proj, alog, dtb, nw, states, don, invs), ride=ride)


GDN_PROJ = GDN_CONV_DIM + GDN_VAL + 2 * GDN_HV


def _gdn_layer_fwd(u, w, ln_g, ln_b, tag, rides=None, late=None):
    w_in = jnp.concatenate([w["in_w"], jnp.zeros((D_MODEL, GDN_PROJ_PAD - GDN_PROJ), w["in_w"].dtype)], axis=1)
    alog = _pad_lanes(w["a_log"], offset=GDN_A_LANE)
    dtb = _pad_lanes(w["dt_bias"], offset=GDN_A_LANE)
    nw = w["norm_w"].reshape(1, GDN_DV)
    zb = jnp.zeros((1, GDN_CONV_DIM), F32)
    if late is None:
        proj = _mm(u.astype(MXU), w_in, name=tag + "_in")
    else:
        proj, got = _mm(u.astype(MXU), w_in, name=tag + "_in", ride=(late[0], False))
        w.update(late[1](got))
    act, _ = _conv_fwd(proj, 0, w["conv_w"], zb, name=tag + "_conv")
    rode = {}
    (on, states, invs), rode["delta"] = _gdn_fwd(proj, act, alog, dtb, nw, name=tag + "_delta",
                                                 ride=_ride_of(rides, "delta", False))
    y = _mm(on, w["out_w"], name=tag + "_out")
    h = _ln_fwd(u, y, ln_g, ln_b, name=tag + "_ln")
    saved = dict(u=u, w_in=w_in, proj=proj, act=act, states=states, invs=invs, on=on, y=y, alog=alog, dtb=dtb, nw=nw, zb=zb)
    return h, saved, rode


def _gdn_layer_bwd(s, w, ln_g, dr_up, du_up, tag, rides=None, own=None):
    t = s["u"].shape[0]
    dr, dg, db = _ln_bwd(s["u"], s["y"], ln_g, dr_up, du_up, name=tag + "_ln_b")
    don = _mm(dr, w["out_w"], tb=True, name=tag + "_out_bx")
    d_out_w = _mm(s["on"], dr, ta=True, out_dtype=WIRE, name=tag + "_out_bw")
    rode = {}
    (dact, dz, dba, dalog, ddtb, dnw), rode["delta_b"] = _gdn_bwd(
        s["proj"], s["act"], s["alog"], s["dtb"], s["nw"], s["states"], s["invs"], don, name=tag + "_delta_b",
        ride=_ride_of(rides, "delta_b", True))
    dqkv, d_conv_w, _ = _conv_bwd(s["proj"], 0, w["conv_w"], s["zb"], dact, name=tag + "_conv_b")
    dproj = jnp.concatenate([dqkv, dz, dba, jnp.zeros((t, GDN_PROJ_PAD - GDN_PROJ - (LANES - 2 * GDN_HV)), F32)],
                            axis=1).astype(MXU)
    d_in, rode["in_bw"] = _mm_r(s["u"].astype(MXU), dproj, ([own("out_w", d_out_w)], True) if own else None, ta=True,
                                out_dtype=WIRE, name=tag + "_in_bw")
    du = _mm(dproj, s["w_in"], tb=True, name=tag + "_in_bx")
    grads = dict(in_w=d_in[:, :GDN_PROJ], conv_w=d_conv_w, a_log=dalog[0, GDN_A_LANE:GDN_A_LANE + GDN_HV],
                 dt_bias=ddtb[0, GDN_A_LANE:GDN_A_LANE + GDN_HV], norm_w=dnw[0], out_w=d_out_w, ln_g=dg[0], ln_b=db[0])
    return dr, du, grads, rode


def _mesh_pos():
    return lax.axis_index("x"), lax.axis_index("y"), lax.axis_index("c")


def _peer(k, x, y, c):
    return ((1 - x) if k & 4 else x, (1 - y) if k & 2 else y, (1 - c) if k & 1 else c)


def _ride_copies(ins, outs, send, recv, loc, scatter, with_arrivals):
    n = len(ins)
    x, y, c = _mesh_pos()
    me = 4 * x + 2 * y + c
    local = [pltpu.make_async_copy(ins[i].at[me] if scatter else ins[i], outs[i].at[me], loc.at[i]) for i in range(n)]
    sends, arrivals = [], []
    for k in range(1, N_DEV):
        peer = _peer(k, x, y, c)
        pidx = 4 * peer[0] + 2 * peer[1] + peer[2]
        for i in range(n):
            src = ins[i].at[pidx] if scatter else ins[i]
            sems = dict(send_sem=send.at[i, k - 1], recv_sem=recv.at[i, k - 1], device_id=peer,
                        device_id_type=pl.DeviceIdType.MESH)
            sends.append(pltpu.make_async_remote_copy(src_ref=src, dst_ref=outs[i].at[me], **sems))
            if with_arrivals:
                arrivals.append(pltpu.make_async_remote_copy(src_ref=src, dst_ref=outs[i].at[pidx], **sems))
    return local, sends, arrivals


def _gather_copy(ins, outs, send, recv, i, k, block, to, from_input=False):
    slot = outs[i].at[4 * block[0] + 2 * block[1] + block[2]]
    return pltpu.make_async_remote_copy(src_ref=ins[i] if from_input else slot, dst_ref=slot, send_sem=send.at[i, k],
                                        recv_sem=recv.at[i, k], device_id=to, device_id_type=pl.DeviceIdType.MESH)


def _gather_places():
    x, y, c = _mesh_pos()
    return (x, y, c), (x, y, 1 - c), [(x, 1 - y), (1 - x, y), (1 - x, 1 - y)], c


def _ride_start(ins, outs, send, recv, loc, scatter, direct=True):
    if direct:
        local, sends, _ = _ride_copies(ins, outs, send, recv, loc, scatter, False)
        for cp in local + sends:
            cp.start()
        return
    me, sibling, chips, c = _gather_places()
    for i in range(len(ins)):
        pltpu.make_async_copy(ins[i], outs[i].at[4 * me[0] + 2 * me[1] + me[2]], loc.at[i]).start()
        _gather_copy(ins, outs, send, recv, i, 0, me, sibling, True).start()
        for j, ch in enumerate(chips):
            _gather_copy(ins, outs, send, recv, i, 1 + j, me, (*ch, c), True).start()


def _ride_wait(ins, outs, send, recv, loc, scatter, direct=True):
    if direct:
        local, sends, arrivals = _ride_copies(ins, outs, send, recv, loc, scatter, True)
        for cp in arrivals:
            cp.wait_recv()
        for cp in sends:
            cp.wait_send()
        for cp in local:
            cp.wait()
        return
    me, sibling, chips, c = _gather_places()
    n = len(ins)
    passed = []
    for j, ch in enumerate(chips):
        for i in range(n):
            _gather_copy(ins, outs, send, recv, i, 1 + j, (*ch, c), me).wait_recv()
            cp = _gather_copy(ins, outs, send, recv, i, 4 + j, (*ch, c), sibling)
            cp.start()
            passed.append(cp)
    for i in range(n):
        _gather_copy(ins, outs, send, recv, i, 0, sibling, me).wait_recv()
        for j, ch in enumerate(chips):
            _gather_copy(ins, outs, send, recv, i, 4 + j, (*ch, 1 - c), me).wait_recv()
    for i in range(n):
        _gather_copy(ins, outs, send, recv, i, 0, me, sibling, True).wait_send()
        for j, ch in enumerate(chips):
            _gather_copy(ins, outs, send, recv, i, 1 + j, me, (*ch, c), True).wait_send()
    for cp in passed:
        cp.wait_send()
    for i in range(n):
        pltpu.make_async_copy(ins[i], outs[i].at[4 * me[0] + 2 * me[1] + me[2]], loc.at[i]).wait()


def _ride_shapes(arrs, scatter):
    n = len(arrs)
    out_shape = [jax.ShapeDtypeStruct(a.shape if scatter else (N_DEV,) + a.shape, a.dtype) for a in arrs]
    scratch = [pltpu.SemaphoreType.DMA((n, N_DEV - 1)), pltpu.SemaphoreType.DMA((n, N_DEV - 1)), pltpu.SemaphoreType.DMA((n,))]
    return out_shape, scratch


def _exchange(arrs, scatter, name):
    n = len(arrs)
    hbm = pl.BlockSpec(memory_space=pltpu.HBM)

    def body(*refs):
        ins, outs = refs[:n], refs[n:2 * n]
        _ride_start(ins, outs, *refs[2 * n:], scatter, direct=scatter)
        _ride_wait(ins, outs, *refs[2 * n:], scatter, direct=scatter)

    out_shape, scratch = _ride_shapes(arrs, scatter)
    return pl.pallas_call(
        body, name=name, in_specs=[hbm] * n, out_specs=[hbm] * n, out_shape=out_shape, scratch_shapes=scratch,
        compiler_params=pltpu.CompilerParams(has_side_effects=True),
    )(*arrs)


def _hosted_call(body, *, name, grid, in_specs, out_specs, out_shape, scratch_shapes, sem, args, ride=None):
    if ride is None:
        return pl.pallas_call(body, name=name, grid=grid, in_specs=in_specs, out_specs=out_specs, out_shape=out_shape,
                              scratch_shapes=scratch_shapes, compiler_params=_cparams(sem))(*args), []
    arrs, scatter = ride
    n, ni, no, ns = len(arrs), len(in_specs), len(out_specs), len(scratch_shapes)
    hbm = pl.BlockSpec(memory_space=pltpu.HBM)
    r_shape, r_scratch = _ride_shapes(arrs, scatter)

    def full(*refs):
        a, ri = refs[:ni], refs[ni:ni + n]
        o, ro = refs[ni + n:ni + n + no], refs[ni + n + no:ni + 2 * n + no]
        s, rs = refs[ni + 2 * n + no:ni + 2 * n + no + ns], refs[ni + 2 * n + no + ns:]
        ids = [pl.program_id(d) for d in range(len(grid))]
        first, last = ids[0] == 0, ids[0] == grid[0] - 1
        for d in range(1, len(grid)):
            first, last = first & (ids[d] == 0), last & (ids[d] == grid[d] - 1)

        @pl.when(first)
        def _():
            _ride_start(ri, ro, *rs, scatter)

        body(*a, *o, *s)

        @pl.when(last)
        def _():
            _ride_wait(ri, ro, *rs, scatter)

    outs = pl.pallas_call(
        full, name=name, grid=grid, in_specs=list(in_specs) + [hbm] * n, out_specs=list(out_specs) + [hbm] * n,
        out_shape=list(out_shape) + r_shape, scratch_shapes=list(scratch_shapes) + r_scratch,
        compiler_params=pltpu.CompilerParams(dimension_semantics=("arbitrary",) * len(grid), vmem_limit_bytes=VMEM_LIMIT,
                                             has_side_effects=True),
    )(*args, *arrs)
    return outs[:no], list(outs[no:])


def _to_wire(w3, j, name):
    _, r, c = w3.shape
    tr = _row_tile(r, c)

    def body(w_ref, o_ref):
        o_ref[...] = w_ref[0].astype(o_ref.dtype)

    return pl.pallas_call(
        body, name=name, grid=(r // tr,), in_specs=[pl.BlockSpec((1, tr, c), lambda i: (j, i, 0))],
        out_specs=pl.BlockSpec((tr, c), lambda i: (i, 0)), out_shape=jax.ShapeDtypeStruct((r, c), WIRE),
        compiler_params=_cparams(("parallel",)),
    )(w3)


def _unshard(g, ax):
    g = jnp.moveaxis(g, 0, ax)
    sh = g.shape
    return g.reshape(sh[:ax] + (sh[ax] * sh[ax + 1],) + sh[ax + 2:])


def _to_parts(full, ax):
    sh = full.shape
    full = full.reshape(sh[:ax] + (N_DEV, sh[ax] // N_DEV) + sh[ax + 1:])
    return jnp.moveaxis(full, ax, 0)


def _row_tile(r, c):
    cap = max(8, (256 * 1024) // max(c, 1))
    best = None
    for d in range(8, min(r, cap) + 1, 8):
        if r % d == 0:
            best = d
    return r if best is None else best


def _adamw(w, m, v, parts, name):
    r, c = w.shape
    tr = _row_tile(r, c)

    def body(w_ref, m_ref, v_ref, p_ref, g_ref, d_ref, nm_ref, nv_ref):
        g = p_ref[0].astype(F32)
        for q in range(1, N_DEV):
            g = g + p_ref[q].astype(F32)
        nm = ADAM_B1 * m_ref[...] + (1.0 - ADAM_B1) * g
        nv = ADAM_B2 * v_ref[...] + (1.0 - ADAM_B2) * (g * g)
        m_hat = nm / (1.0 - ADAM_B1 ** ADAM_STEP)
        v_hat = nv / (1.0 - ADAM_B2 ** ADAM_STEP)
        g_ref[...] = g
        d_ref[...] = -ADAM_LR * (m_hat / (jnp.sqrt(v_hat) + ADAM_EPS) + ADAM_WD * w_ref[...])
        nm_ref[...] = nm
        nv_ref[...] = nv

    row = pl.BlockSpec((tr, c), lambda i: (i, 0))
    out = jax.ShapeDtypeStruct((r, c), F32)
    return pl.pallas_call(
        body, name=name, grid=(r // tr,),
        in_specs=[row, row, row, pl.BlockSpec((N_DEV, tr, c), lambda i: (0, i, 0))],
        out_specs=[row] * 4, out_shape=[out] * 4, compiler_params=_cparams(("parallel",)),
    )(w, m, v, parts)


WEIGHTS = ['ssd_in_w', 'ssd_conv_w', 'ssd_conv_b', 'ssd_dt_bias', 'ssd_a_log', 'ssd_d', 'ssd_norm_w', 'ssd_out_w',
           'mla_in_w', 'mla_q_norm_w', 'mla_q_up_w', 'mla_kv_norm_w', 'mla_kv_up_w', 'mla_out_w', 'gdn_in_w',
           'gdn_conv_w', 'gdn_a_log', 'gdn_dt_bias', 'gdn_norm_w', 'gdn_out_w', 'ln_g', 'ln_b']
SHARDED = {'ssd_in_w': (1, True), 'ssd_conv_w': (1, False), 'ssd_conv_b': (0, False), 'ssd_norm_w': (0, False),
           'ssd_out_w': (0, True), 'mla_in_w': (1, True), 'mla_q_up_w': (1, True), 'mla_kv_up_w': (1, True),
           'mla_out_w': (0, True), 'gdn_in_w': (1, True), 'gdn_conv_w': (1, False), 'gdn_out_w': (0, True)}
REPLICATED = [n for n in WEIGHTS if n not in SHARDED]


def _pack_small(vals):
    flat = jnp.concatenate([vals[n].reshape(-1).astype(F32) for n in REPLICATED])
    rows = -(-flat.shape[0] // (8 * LANES)) * 8
    return jnp.pad(flat, (0, rows * LANES - flat.shape[0])).reshape(rows, LANES)


def _unpack_small(slab, like):
    flat = slab.reshape(-1)
    out, off = {}, 0
    for n in REPLICATED:
        sz = like[n].size
        out[n] = flat[off:off + sz].reshape(like[n].shape)
        off += sz
    return out


def kernel(x, positions, ssd_in_w, ssd_conv_w, ssd_conv_b, ssd_dt_bias, ssd_a_log, ssd_d, ssd_norm_w, ssd_out_w, mla_in_w, mla_q_norm_w, mla_q_up_w, mla_kv_norm_w, mla_kv_up_w, mla_out_w, gdn_in_w, gdn_conv_w, gdn_a_log, gdn_dt_bias, gdn_norm_w, gdn_out_w, ln_g, ln_b, loss_target, m_ssd_in_w, m_ssd_conv_w, m_ssd_conv_b, m_ssd_dt_bias, m_ssd_a_log, m_ssd_d, m_ssd_norm_w, m_ssd_out_w, m_mla_in_w, m_mla_q_norm_w, m_mla_q_up_w, m_mla_kv_norm_w, m_mla_kv_up_w, m_mla_out_w, m_gdn_in_w, m_gdn_conv_w, m_gdn_a_log, m_gdn_dt_bias, m_gdn_norm_w, m_gdn_out_w, m_ln_g, m_ln_b, v_ssd_in_w, v_ssd_conv_w, v_ssd_conv_b, v_ssd_dt_bias, v_ssd_a_log, v_ssd_d, v_ssd_norm_w, v_ssd_out_w, v_mla_in_w, v_mla_q_norm_w, v_mla_q_up_w, v_mla_kv_norm_w, v_mla_kv_up_w, v_mla_out_w, v_gdn_in_w, v_gdn_conv_w, v_gdn_a_log, v_gdn_dt_bias, v_gdn_norm_w, v_gdn_out_w, v_ln_g, v_ln_b):
    loc = locals()
    w = {n: loc[n] for n in WEIGHTS}
    m = {n: loc["m_" + n] for n in WEIGHTS}
    v = {n: loc["v_" + n] for n in WEIGHTS}
    xs, pos, tgt = x[0], positions[0], loss_target[0]

    def names_of(prefix):
        return [n for n in SHARDED if n.startswith(prefix + "_")]

    def shard(n, j):
        return _to_wire(w[n], j, name="wire_%s_%d" % (n, j)) if SHARDED[n][1] else w[n][j]

    def full(n, gathered):
        return _unshard(gathered, SHARDED[n][0])

    def slots(n, g):
        return _to_parts(g, SHARDED[n][0]).astype(WIRE if SHARDED[n][1] else F32)

    def replicated(prefix, j):
        return {n[len(prefix) + 1:]: w[n][j] for n in REPLICATED if n.startswith(prefix + "_")}

    def late_weights(prefix, j, keys):
        return [shard(prefix + "_" + k, j) for k in keys], lambda got: {k: full(prefix + "_" + k, g) for k, g in zip(keys, got)}

    lg = lambda i: w["ln_g"][i].reshape(1, D_MODEL)
    lb = lambda i: w["ln_b"][i].reshape(1, D_MODEL)
    cosf, sinf = _rope_tables(pos)
    ssd_late = ("conv_w", "conv_b", "norm_w", "out_w")

    w_s0 = dict(replicated("ssd", 0), in_w=full("ssd_in_w", _exchange([shard("ssd_in_w", 0)], False, name="gather_ssd0")[0]))
    h1, s0, got = _ssd_layer_fwd(
        xs, w_s0, lg(0), lb(0), "ssd0", late=late_weights("ssd", 0, ssd_late),
        rides={"conv": [shard("mla_q_up_w", 0), shard("mla_kv_up_w", 0)], "scan": [shard("mla_in_w", 0)],
               "out": [shard("mla_out_w", 0)]})
    w_m0 = dict(replicated("mla", 0), q_up_w=full("mla_q_up_w", got["conv"][0]), kv_up_w=full("mla_kv_up_w", got["conv"][1]),
                in_w=full("mla_in_w", got["scan"][0]), out_w=full("mla_out_w", got["out"][0]))
    h2, s1, got = _mla_layer_fwd(h1, w_m0, lg(1), lb(1), cosf, sinf, "mla0", rides={"attn": [shard("gdn_in_w", 0)]})
    w_g0 = dict(replicated("gdn", 0), in_w=full("gdn_in_w", got["attn"][0]))
    h3, s2, got = _gdn_layer_fwd(h2, w_g0, lg(2), lb(2), "gdn0", late=late_weights("gdn", 0, ("conv_w", "out_w")),
                                 rides={"delta": [shard("ssd_in_w", 1)]})
    w_s1 = dict(replicated("ssd", 1), in_w=full("ssd_in_w", got["delta"][0]))
    h4, s3, _ = _ssd_layer_fwd(h3, w_s1, lg(3), lb(3), "ssd1", late=late_weights("ssd", 1, ssd_late))
    loss_tile, dl = _loss_head(h4, tgt, name="loss_head")

    dr3, du3, g3, _, _ = _ssd_layer_bwd(s3, w_s1, lg(3), jnp.zeros_like(dl), dl, "ssd1")
    dr2, du2, g2, got = _gdn_layer_bwd(s2, w_g0, lg(2), dr3, du3, "gdn0", own=lambda k, g: slots("gdn_" + k, g),
                                       rides={"delta_b": [slots(n, g3[n[4:]]) for n in names_of("ssd")]})
    r3, r2_out = got["delta_b"], got["in_bw"][0]
    gin = slots("gdn_in_w", g2["in_w"])
    half = gin.shape[1] // 2
    dr1, du1, g1, got = _mla_layer_bwd(
        s1, w_m0, lg(1), cosf, sinf, dr2, du2, "mla0",
        rides={"attn_bq": [gin[:, :half]], "attn_bkv": [gin[:, half:], slots("gdn_conv_w", g2["conv_w"])]})
    r2 = {"gdn_in_w": jnp.concatenate([got["attn_bq"][0], got["attn_bkv"][0]], axis=1), "gdn_conv_w": got["attn_bkv"][1],
          "gdn_out_w": r2_out}
    dr0, du0, g0, got, r0 = _ssd_layer_bwd(s0, w_s0, lg(0), dr1, du1, "ssd0", own=lambda k, g: slots("ssd_" + k, g),
                                           rides={"scan_b": [slots(n, g1[n[4:]]) for n in names_of("mla")]})
    r1 = got["scan_b"]
    grad_x = _axpy(dr0, du0, name="grad_x")[None]

    gsmall = {"ssd_" + k: jnp.stack([g0[k], g3[k]]) for k in ("dt_bias", "a_log", "d")}
    gsmall.update({"mla_" + k: g1[k][None] for k in ("q_norm_w", "kv_norm_w")})
    gsmall.update({"gdn_" + k: g2[k][None] for k in ("a_log", "dt_bias", "norm_w")})
    gsmall["ln_g"] = jnp.stack([g0["ln_g"], g1["ln_g"], g2["ln_g"], g3["ln_g"]])
    gsmall["ln_b"] = jnp.stack([g0["ln_b"], g1["ln_b"], g2["ln_b"], g3["ln_b"]])
    small = _pack_small(gsmall)
    rsmall, = _exchange([jnp.broadcast_to(small[None], (N_DEV,) + small.shape)], True, name="gather_small_grads")

    recvd = {n: jnp.stack([r0[n[4:]], b], axis=1) for n, b in zip(names_of("ssd"), r3)}
    recvd.update({n: a[:, None] for n, a in zip(names_of("mla"), r1)})
    recvd.update({n: a[:, None] for n, a in r2.items()})
    recvd = [recvd[n] for n in SHARDED] + [rsmall]

    grads, deltas, new_m, new_v = {}, {}, {}, {}
    for n, pt in zip(SHARDED, recvd[:-1]):
        shp = w[n].shape
        r2d = (-1, shp[-1])
        outs = _adamw(w[n].reshape(r2d), m[n].reshape(r2d), v[n].reshape(r2d), pt.reshape((N_DEV,) + w[n].reshape(r2d).shape),
                      name="adamw_" + n)
        grads[n], deltas[n], new_m[n], new_v[n] = (o.reshape(shp) for o in outs)
    outs = _adamw(_pack_small(w), _pack_small(m), _pack_small(v), recvd[-1], name="adamw_replicated")
    for dst, o in zip((grads, deltas, new_m, new_v), outs):
        dst.update(_unpack_small(o, w))

    loss = lax.psum(loss_tile[0, 0], ("x", "y", "c"))
    return (loss, grad_x, *[grads[n] for n in WEIGHTS], *[deltas[n] for n in WEIGHTS],
            *[new_m[n] for n in WEIGHTS], *[new_v[n] for n in WEIGHTS])
```
